```python
import jax, jax.numpy as jnp
from jax import lax
import numpy as np

D_MODEL = 1024
BATCH = 8
SEQ = 8192
DEPTH = 1

CHUNK = 64
D_SHORT = D_MODEL
D_CONF = D_MODEL
SHORT_K = 3
CONF_K = 31
D_FF = -(-8 * D_MODEL // (3 * 256)) * 256
N_MOD = 6
EPS = 1e-6
LN_EPS = 1e-5
SPLITS = (D_SHORT, 2 * D_SHORT, 3 * D_SHORT,
          3 * D_SHORT + D_CONF, 3 * D_SHORT + 2 * D_CONF,
          3 * D_SHORT + 2 * D_CONF + D_MODEL)
D_IN = 3 * D_SHORT + 2 * D_CONF + 2 * D_MODEL

kernel_name = 'hybrid_shortconv_conformer_gated_block'


def rms_norm(x, g):
    xf = x.astype(jnp.float32)
    y = xf * lax.rsqrt(jnp.mean(xf * xf, axis=-1, keepdims=True) + EPS)
    return (y * g.astype(jnp.float32)).astype(x.dtype)


def layer_norm(x, g, b):
    xf = x.astype(jnp.float32)
    mu = jnp.mean(xf, axis=-1, keepdims=True)
    var = jnp.mean(jnp.square(xf - mu), axis=-1, keepdims=True)
    y = (xf - mu) * lax.rsqrt(var + LN_EPS)
    return (y * g.astype(jnp.float32) + b.astype(jnp.float32)).astype(x.dtype)


def causal_dwconv(u, w):
    k, ch = w.shape
    return lax.conv_general_dilated(
        u, w.astype(u.dtype)[:, None, :], window_strides=(1,), padding=[(k - 1, 0)],
        dimension_numbers=('NWC', 'WIO', 'NWC'), feature_group_count=ch)


def _fwd_setup_inputs(seed: int = 0) -> dict:
    key = jax.random.key(seed)
    ks = jax.random.split(key, 20)
    n = jax.random.normal
    d = D_MODEL
    return {
        'x': n(ks[0], (BATCH, SEQ, d), jnp.float32),
        'c': n(ks[1], (BATCH, d), jnp.float32),
        'w_ada': n(ks[2], (DEPTH, d, N_MOD * d), jnp.float32) * (0.5 * d ** -0.5),
        'b_ada': n(ks[3], (DEPTH, N_MOD * d), jnp.float32) * 0.02,
        'norm_mix_g': 1.0 + 0.05 * n(ks[4], (DEPTH, d), jnp.float32),
        'w_in': n(ks[5], (DEPTH, d, D_IN), jnp.float32) * d ** -0.5,
        'conv_short_w': n(ks[6], (DEPTH, SHORT_K, D_SHORT), jnp.float32) * SHORT_K ** -0.5,
        'w_short_out': n(ks[7], (DEPTH, D_SHORT, d), jnp.float32) * D_SHORT ** -0.5,
        'conv_conf_w': n(ks[8], (DEPTH, CONF_K, D_CONF), jnp.float32) * CONF_K ** -0.5,
        'conv_conf_b': n(ks[9], (DEPTH, D_CONF), jnp.float32) * 0.02,
        'conf_ln_g': 1.0 + 0.05 * n(ks[10], (DEPTH, D_CONF), jnp.float32),
        'conf_ln_b': n(ks[11], (DEPTH, D_CONF), jnp.float32) * 0.02,
        'w_conf_out': n(ks[12], (DEPTH, D_CONF, d), jnp.float32) * D_CONF ** -0.5,
        'w_o': n(ks[13], (DEPTH, d, d), jnp.float32) * d ** -0.5,
        'norm_ffn_g': 1.0 + 0.05 * n(ks[14], (DEPTH, d), jnp.float32),
        'w_ffn_in': n(ks[15], (DEPTH, d, 2 * D_FF), jnp.float32) * d ** -0.5,
        'w_ffn_out': n(ks[16], (DEPTH, D_FF, d), jnp.float32) * D_FF ** -0.5,
        'final_norm_g': 1.0 + 0.05 * n(ks[17], (d,), jnp.float32),
    }


def _fwd_reference(x, c, w_ada, b_ada, norm_mix_g, w_in, conv_short_w, w_short_out,
              conv_conf_w, conv_conf_b, conf_ln_g, conf_ln_b, w_conf_out, w_o,
              norm_ffn_g, w_ffn_in, w_ffn_out, final_norm_g):
    for l in range(DEPTH):
        mod = jax.nn.silu(c) @ w_ada[l] + b_ada[l]
        sh1, sc1, g1, sh2, sc2, g2 = jnp.split(mod[:, None, :], N_MOD, axis=-1)

        h = rms_norm(x, norm_mix_g[l]) * (1.0 + sc1) + sh1
        proj = jnp.einsum('bsd,de->bse', h, w_in[l])
        b_s, c_s, v_s, v_c, gl_c, gate_br = jnp.split(proj, SPLITS[:-1], axis=-1)

        y_a = b_s * causal_dwconv(c_s * v_s, conv_short_w[l])
        y_a = jnp.einsum('bsc,cd->bsd', y_a, w_short_out[l])

        u = v_c * jax.nn.sigmoid(gl_c)
        u = causal_dwconv(u, conv_conf_w[l]) + conv_conf_b[l]
        u = jax.nn.silu(layer_norm(u, conf_ln_g[l], conf_ln_b[l]))
        y_b = jnp.einsum('bsc,cd->bsd', u, w_conf_out[l])

        g_a, g_b = jnp.split(jax.nn.sigmoid(gate_br), 2, axis=-1)
        mix = jnp.einsum('bsd,de->bse', g_a * y_a + g_b * y_b, w_o[l])
        x = x + g1 * mix

        h2 = rms_norm(x, norm_ffn_g[l]) * (1.0 + sc2) + sh2
        a, bgate = jnp.split(jnp.einsum('bsd,df->bsf', h2, w_ffn_in[l]), 2, axis=-1)
        x = x + g2 * jnp.einsum('bsf,fd->bsd', jax.nn.silu(a) * bgate, w_ffn_out[l])

    return rms_norm(x, final_norm_g)


import jax as _jax
import jax.numpy as _jnp

TWIN_FORMAT = 'train_step'
FWD_PARAMS = ['x', 'c', 'w_ada', 'b_ada', 'norm_mix_g', 'w_in', 'conv_short_w', 'w_short_out', 'conv_conf_w', 'conv_conf_b', 'conf_ln_g', 'conf_ln_b', 'w_conf_out', 'w_o', 'norm_ffn_g', 'w_ffn_in', 'w_ffn_out', 'final_norm_g']
TWIN_WEIGHTS = ['w_ada', 'b_ada', 'norm_mix_g', 'w_in', 'conv_short_w', 'w_short_out', 'conv_conf_w', 'conv_conf_b', 'conf_ln_g', 'conf_ln_b', 'w_conf_out', 'w_o', 'norm_ffn_g', 'w_ffn_in', 'w_ffn_out', 'final_norm_g']
TWIN_DIFF_INPUT = 'x'
TWIN_INPUTS = ['x', 'c', 'w_ada', 'b_ada', 'norm_mix_g', 'w_in', 'conv_short_w', 'w_short_out', 'conv_conf_w', 'conv_conf_b', 'conf_ln_g', 'conf_ln_b', 'w_conf_out', 'w_o', 'norm_ffn_g', 'w_ffn_in', 'w_ffn_out', 'final_norm_g', 'loss_target', 'm_w_ada', 'm_b_ada', 'm_norm_mix_g', 'm_w_in', 'm_conv_short_w', 'm_w_short_out', 'm_conv_conf_w', 'm_conv_conf_b', 'm_conf_ln_g', 'm_conf_ln_b', 'm_w_conf_out', 'm_w_o', 'm_norm_ffn_g', 'm_w_ffn_in', 'm_w_ffn_out', 'm_final_norm_g', 'v_w_ada', 'v_b_ada', 'v_norm_mix_g', 'v_w_in', 'v_conv_short_w', 'v_w_short_out', 'v_conv_conf_w', 'v_conv_conf_b', 'v_conf_ln_g', 'v_conf_ln_b', 'v_w_conf_out', 'v_w_o', 'v_norm_ffn_g', 'v_w_ffn_in', 'v_w_ffn_out', 'v_final_norm_g']
TWIN_OUTPUTS = ['loss', 'grad_x', 'grad_w_ada', 'grad_b_ada', 'grad_norm_mix_g', 'grad_w_in', 'grad_conv_short_w', 'grad_w_short_out', 'grad_conv_conf_w', 'grad_conv_conf_b', 'grad_conf_ln_g', 'grad_conf_ln_b', 'grad_w_conf_out', 'grad_w_o', 'grad_norm_ffn_g', 'grad_w_ffn_in', 'grad_w_ffn_out', 'grad_final_norm_g', 'delta_w_ada', 'delta_b_ada', 'delta_norm_mix_g', 'delta_w_in', 'delta_conv_short_w', 'delta_w_short_out', 'delta_conv_conf_w', 'delta_conv_conf_b', 'delta_conf_ln_g', 'delta_conf_ln_b', 'delta_w_conf_out', 'delta_w_o', 'delta_norm_ffn_g', 'delta_w_ffn_in', 'delta_w_ffn_out', 'delta_final_norm_g', 'new_m_w_ada', 'new_m_b_ada', 'new_m_norm_mix_g', 'new_m_w_in', 'new_m_conv_short_w', 'new_m_w_short_out', 'new_m_conv_conf_w', 'new_m_conv_conf_b', 'new_m_conf_ln_g', 'new_m_conf_ln_b', 'new_m_w_conf_out', 'new_m_w_o', 'new_m_norm_ffn_g', 'new_m_w_ffn_in', 'new_m_w_ffn_out', 'new_m_final_norm_g', 'new_v_w_ada', 'new_v_b_ada', 'new_v_norm_mix_g', 'new_v_w_in', 'new_v_conv_short_w', 'new_v_w_short_out', 'new_v_conv_conf_w', 'new_v_conv_conf_b', 'new_v_conf_ln_g', 'new_v_conf_ln_b', 'new_v_w_conf_out', 'new_v_w_o', 'new_v_norm_ffn_g', 'new_v_w_ffn_in', 'new_v_w_ffn_out', 'new_v_final_norm_g']
TWIN_LEAF_KINDS = {'loss': 'loss', 'grad_x': 'grad_x', 'grad_w_ada': 'grad_w', 'grad_b_ada': 'grad_w', 'grad_norm_mix_g': 'grad_w', 'grad_w_in': 'grad_w', 'grad_conv_short_w': 'grad_w', 'grad_w_short_out': 'grad_w', 'grad_conv_conf_w': 'grad_w', 'grad_conv_conf_b': 'grad_w', 'grad_conf_ln_g': 'grad_w', 'grad_conf_ln_b': 'grad_w', 'grad_w_conf_out': 'grad_w', 'grad_w_o': 'grad_w', 'grad_norm_ffn_g': 'grad_w', 'grad_w_ffn_in': 'grad_w', 'grad_w_ffn_out': 'grad_w', 'grad_final_norm_g': 'grad_w', 'delta_w_ada': 'delta_w', 'delta_b_ada': 'delta_w', 'delta_norm_mix_g': 'delta_w', 'delta_w_in': 'delta_w', 'delta_conv_short_w': 'delta_w', 'delta_w_short_out': 'delta_w', 'delta_conv_conf_w': 'delta_w', 'delta_conv_conf_b': 'delta_w', 'delta_conf_ln_g': 'delta_w', 'delta_conf_ln_b': 'delta_w', 'delta_w_conf_out': 'delta_w', 'delta_w_o': 'delta_w', 'delta_norm_ffn_g': 'delta_w', 'delta_w_ffn_in': 'delta_w', 'delta_w_ffn_out': 'delta_w', 'delta_final_norm_g': 'delta_w', 'new_m_w_ada': 'new_m', 'new_m_b_ada': 'new_m', 'new_m_norm_mix_g': 'new_m', 'new_m_w_in': 'new_m', 'new_m_conv_short_w': 'new_m', 'new_m_w_short_out': 'new_m', 'new_m_conv_conf_w': 'new_m', 'new_m_conv_conf_b': 'new_m', 'new_m_conf_ln_g': 'new_m', 'new_m_conf_ln_b': 'new_m', 'new_m_w_conf_out': 'new_m', 'new_m_w_o': 'new_m', 'new_m_norm_ffn_g': 'new_m', 'new_m_w_ffn_in': 'new_m', 'new_m_w_ffn_out': 'new_m', 'new_m_final_norm_g': 'new_m', 'new_v_w_ada': 'new_v', 'new_v_b_ada': 'new_v', 'new_v_norm_mix_g': 'new_v', 'new_v_w_in': 'new_v', 'new_v_conv_short_w': 'new_v', 'new_v_w_short_out': 'new_v', 'new_v_conv_conf_w': 'new_v', 'new_v_conv_conf_b': 'new_v', 'new_v_conf_ln_g': 'new_v', 'new_v_conf_ln_b': 'new_v', 'new_v_w_conf_out': 'new_v', 'new_v_w_o': 'new_v', 'new_v_norm_ffn_g': 'new_v', 'new_v_w_ffn_in': 'new_v', 'new_v_w_ffn_out': 'new_v', 'new_v_final_norm_g': 'new_v'}


def _forward(args):
    return _fwd_reference(*[args[k] for k in FWD_PARAMS])


def _output_shape():
    out = _jax.eval_shape(lambda: _forward(_fwd_setup_inputs(0)))
    return out.shape, out.dtype

N_MICROBATCH = 1
ADAM_LR = 0.001
ADAM_B1 = 0.9
ADAM_B2 = 0.999
ADAM_EPS = 1e-08
ADAM_WD = 0.01
ADAM_STEP = 10
PER_EXAMPLE_BATCH_AXIS = {'x': 0, 'c': 0, 'loss_target': 0}
SHARED_INPUTS = []
_WEIGHT_DTYPES = {'w_ada': _jnp.float32, 'b_ada': _jnp.float32, 'norm_mix_g': _jnp.float32, 'w_in': _jnp.float32, 'conv_short_w': _jnp.float32, 'w_short_out': _jnp.float32, 'conv_conf_w': _jnp.float32, 'conv_conf_b': _jnp.float32, 'conf_ln_g': _jnp.float32, 'conf_ln_b': _jnp.float32, 'w_conf_out': _jnp.float32, 'w_o': _jnp.float32, 'norm_ffn_g': _jnp.float32, 'w_ffn_in': _jnp.float32, 'w_ffn_out': _jnp.float32, 'final_norm_g': _jnp.float32}
MOMENT_SCALE = {'w_ada': 1.162929e-01, 'b_ada': 2.121064e-01, 'norm_mix_g': 1.032684e-01, 'w_in': 3.905999e-02, 'conv_short_w': 5.521462e-02, 'w_short_out': 5.640649e-02, 'conv_conf_w': 2.646277e-02, 'conv_conf_b': 5.149115e-02, 'conf_ln_g': 3.166692e-02, 'conf_ln_b': 2.654955e-02, 'w_conf_out': 2.551976e-02, 'w_o': 6.211557e-02, 'norm_ffn_g': 7.730129e-02, 'w_ffn_in': 3.272872e-02, 'w_ffn_out': 5.341366e-02, 'final_norm_g': 6.424944e+01}


def _to_microbatches(a, axis):
    t = _jnp.moveaxis(a, axis, 0)
    t = t.reshape((N_MICROBATCH, t.shape[0] // N_MICROBATCH) + t.shape[1:])
    return _jnp.moveaxis(t, 1, axis + 1)


def setup_inputs(seed: int = 0) -> dict:
    inp = _fwd_setup_inputs(seed)
    key = _jax.random.fold_in(_jax.random.key(seed), 7919)
    shape, _ = _output_shape()
    out = dict(inp)
    out["loss_target"] = _jax.random.normal(_jax.random.fold_in(key, 0), shape, _jnp.float32)
    for i, name in enumerate(TWIN_WEIGHTS):
        w = inp[name].astype(_jnp.float32)
        if MOMENT_SCALE is None:
            s = _jnp.sqrt(_jnp.mean(_jnp.square(w)) + 1e-30)
        else:
            s = MOMENT_SCALE[name]
        km, kv = _jax.random.split(_jax.random.fold_in(key, i + 1))
        out[name] = w
        out["m_" + name] = s * _jax.random.normal(km, w.shape, _jnp.float32)
        out["v_" + name] = (s * s) * _jax.random.uniform(kv, w.shape, _jnp.float32, 0.5, 1.5)
    if N_MICROBATCH > 1:
        for name, axis in PER_EXAMPLE_BATCH_AXIS.items():
            out[name] = _to_microbatches(out[name], axis)
    return {'x': out['x'], 'c': out['c'], 'w_ada': out['w_ada'], 'b_ada': out['b_ada'], 'norm_mix_g': out['norm_mix_g'], 'w_in': out['w_in'], 'conv_short_w': out['conv_short_w'], 'w_short_out': out['w_short_out'], 'conv_conf_w': out['conv_conf_w'], 'conv_conf_b': out['conv_conf_b'], 'conf_ln_g': out['conf_ln_g'], 'conf_ln_b': out['conf_ln_b'], 'w_conf_out': out['w_conf_out'], 'w_o': out['w_o'], 'norm_ffn_g': out['norm_ffn_g'], 'w_ffn_in': out['w_ffn_in'], 'w_ffn_out': out['w_ffn_out'], 'final_norm_g': out['final_norm_g'], 'loss_target': out['loss_target'], 'm_w_ada': out['m_w_ada'], 'm_b_ada': out['m_b_ada'], 'm_norm_mix_g': out['m_norm_mix_g'], 'm_w_in': out['m_w_in'], 'm_conv_short_w': out['m_conv_short_w'], 'm_w_short_out': out['m_w_short_out'], 'm_conv_conf_w': out['m_conv_conf_w'], 'm_conv_conf_b': out['m_conv_conf_b'], 'm_conf_ln_g': out['m_conf_ln_g'], 'm_conf_ln_b': out['m_conf_ln_b'], 'm_w_conf_out': out['m_w_conf_out'], 'm_w_o': out['m_w_o'], 'm_norm_ffn_g': out['m_norm_ffn_g'], 'm_w_ffn_in': out['m_w_ffn_in'], 'm_w_ffn_out': out['m_w_ffn_out'], 'm_final_norm_g': out['m_final_norm_g'], 'v_w_ada': out['v_w_ada'], 'v_b_ada': out['v_b_ada'], 'v_norm_mix_g': out['v_norm_mix_g'], 'v_w_in': out['v_w_in'], 'v_conv_short_w': out['v_conv_short_w'], 'v_w_short_out': out['v_w_short_out'], 'v_conv_conf_w': out['v_conv_conf_w'], 'v_conv_conf_b': out['v_conv_conf_b'], 'v_conf_ln_g': out['v_conf_ln_g'], 'v_conf_ln_b': out['v_conf_ln_b'], 'v_w_conf_out': out['v_w_conf_out'], 'v_w_o': out['v_w_o'], 'v_norm_ffn_g': out['v_norm_ffn_g'], 'v_w_ffn_in': out['v_w_ffn_in'], 'v_w_ffn_out': out['v_w_ffn_out'], 'v_final_norm_g': out['v_final_norm_g']}


def _loss(weights, diff, rest, loss_target):
    with _jax.named_scope("forward"):
        args = {**rest, TWIN_DIFF_INPUT: diff, **{k: w.astype(_WEIGHT_DTYPES[k]) for k, w in weights.items()}}
        y = _forward(args)
    with _jax.named_scope("loss_head"):
        err = _jnp.square(y.astype(_jnp.float32) - loss_target)
        return 0.5 * _jnp.sum(_jnp.mean(err, axis=-1)) if err.ndim else 0.5 * err


def _adamw(w, g, m, v):
    m = ADAM_B1 * m + (1.0 - ADAM_B1) * g
    v = ADAM_B2 * v + (1.0 - ADAM_B2) * _jnp.square(g)
    m_hat = m / (1.0 - ADAM_B1 ** ADAM_STEP)
    v_hat = v / (1.0 - ADAM_B2 ** ADAM_STEP)
    delta = -ADAM_LR * (m_hat / (_jnp.sqrt(v_hat) + ADAM_EPS) + ADAM_WD * w)
    return delta, m, v


def reference(x, c, w_ada, b_ada, norm_mix_g, w_in, conv_short_w, w_short_out, conv_conf_w, conv_conf_b, conf_ln_g, conf_ln_b, w_conf_out, w_o, norm_ffn_g, w_ffn_in, w_ffn_out, final_norm_g, loss_target, m_w_ada, m_b_ada, m_norm_mix_g, m_w_in, m_conv_short_w, m_w_short_out, m_conv_conf_w, m_conv_conf_b, m_conf_ln_g, m_conf_ln_b, m_w_conf_out, m_w_o, m_norm_ffn_g, m_w_ffn_in, m_w_ffn_out, m_final_norm_g, v_w_ada, v_b_ada, v_norm_mix_g, v_w_in, v_conv_short_w, v_w_short_out, v_conv_conf_w, v_conv_conf_b, v_conf_ln_g, v_conf_ln_b, v_w_conf_out, v_w_o, v_norm_ffn_g, v_w_ffn_in, v_w_ffn_out, v_final_norm_g):
    given = dict(x=x, c=c, w_ada=w_ada, b_ada=b_ada, norm_mix_g=norm_mix_g, w_in=w_in, conv_short_w=conv_short_w, w_short_out=w_short_out, conv_conf_w=conv_conf_w, conv_conf_b=conv_conf_b, conf_ln_g=conf_ln_g, conf_ln_b=conf_ln_b, w_conf_out=w_conf_out, w_o=w_o, norm_ffn_g=norm_ffn_g, w_ffn_in=w_ffn_in, w_ffn_out=w_ffn_out, final_norm_g=final_norm_g, loss_target=loss_target, m_w_ada=m_w_ada, m_b_ada=m_b_ada, m_norm_mix_g=m_norm_mix_g, m_w_in=m_w_in, m_conv_short_w=m_conv_short_w, m_w_short_out=m_w_short_out, m_conv_conf_w=m_conv_conf_w, m_conv_conf_b=m_conv_conf_b, m_conf_ln_g=m_conf_ln_g, m_conf_ln_b=m_conf_ln_b, m_w_conf_out=m_w_conf_out, m_w_o=m_w_o, m_norm_ffn_g=m_norm_ffn_g, m_w_ffn_in=m_w_ffn_in, m_w_ffn_out=m_w_ffn_out, m_final_norm_g=m_final_norm_g, v_w_ada=v_w_ada, v_b_ada=v_b_ada, v_norm_mix_g=v_norm_mix_g, v_w_in=v_w_in, v_conv_short_w=v_conv_short_w, v_w_short_out=v_w_short_out, v_conv_conf_w=v_conv_conf_w, v_conv_conf_b=v_conv_conf_b, v_conf_ln_g=v_conf_ln_g, v_conf_ln_b=v_conf_ln_b, v_w_conf_out=v_w_conf_out, v_w_o=v_w_o, v_norm_ffn_g=v_norm_ffn_g, v_w_ffn_in=v_w_ffn_in, v_w_ffn_out=v_w_ffn_out, v_final_norm_g=v_final_norm_g)
    weights = {n: given[n] for n in TWIN_WEIGHTS}
    shared = {n: given[n] for n in SHARED_INPUTS}
    per_example = {n: given[n] for n in ['x', 'c']}
    grad_fn = _jax.value_and_grad(_loss, argnums=(0, 1))

    def one_microbatch(ex, loss_target):
        ex = dict(ex)
        diff = ex.pop(TWIN_DIFF_INPUT)
        return grad_fn(weights, diff, {**shared, **ex}, loss_target)

    if N_MICROBATCH == 1:
        loss, (grad_w, grad_x) = one_microbatch(per_example, given["loss_target"])
    else:
        def body(carry, xs):
            loss_sum, grad_sum = carry
            l_k, (gw_k, gx_k) = one_microbatch(xs[0], xs[1])
            with _jax.named_scope("update"):
                return (loss_sum + l_k, _jax.tree.map(_jnp.add, grad_sum, gw_k)), gx_k

        init = (_jnp.zeros((), _jnp.float32), _jax.tree.map(_jnp.zeros_like, weights))
        (loss, grad_w), grad_x = _jax.lax.scan(body, init, (per_example, given["loss_target"]))
    with _jax.named_scope("update"):
        delta_w, new_m, new_v = {}, {}, {}
        for n in TWIN_WEIGHTS:
            delta_w[n], new_m[n], new_v[n] = _adamw(weights[n], grad_w[n], given["m_" + n], given["v_" + n])
    return (loss, grad_x, *[grad_w[n] for n in TWIN_WEIGHTS], *[delta_w[n] for n in TWIN_WEIGHTS],
            *[new_m[n] for n in TWIN_WEIGHTS], *[new_v[n] for n in TWIN_WEIGHTS])
```

```python
import functools

import jax
import jax.numpy as jnp
from jax import lax
from jax.experimental import pallas as pl
from jax.experimental.pallas import tpu as pltpu

F32 = jnp.float32
BF16 = jnp.bfloat16
EPS = 1e-6
LN_EPS = 1e-5
SHORT_K = 3
CONF_K = 31
N_MOD = 6
N_DEV = 8
N_CHIP = 4
ADAM_LR = 0.001
ADAM_B1 = 0.9
ADAM_B2 = 0.999
ADAM_EPS = 1e-08
ADAM_WD = 0.01
ADAM_STEP = 10
MESH = pl.DeviceIdType.MESH
V7X_VMEM_LIMIT_BYTES = 60 * 1024 * 1024
HALO_SHORT = 8
HALO_CONF = 32
N_SMALL_ROWS = 48

ANY = pl.BlockSpec(memory_space=pl.ANY)
VMEM_FULL = pl.BlockSpec(memory_space=pltpu.VMEM)


def _pallas(body, **kw):
    return pl.pallas_call(body, **kw)


def _params(*sem):
    return pltpu.CompilerParams(dimension_semantics=sem, vmem_limit_bytes=V7X_VMEM_LIMIT_BYTES)


def _sig(z):
    return jax.nn.sigmoid(z)


def _dot(a, b):
    return jnp.dot(a, b, preferred_element_type=F32)


def _dot_nt(a, b):
    return lax.dot_general(a, b, (((1,), (1,)), ((), ())), preferred_element_type=F32)


def _dot_tn(a, b):
    return lax.dot_general(a, b, (((0,), (0,)), ((), ())), preferred_element_type=F32)


def _colsum(z):
    return jnp.sum(z, axis=0, keepdims=True)


def _place():
    return lax.axis_index("x"), lax.axis_index("y"), lax.axis_index("c")


def _all_gather_small(block, name):
    m_per, n = block.shape

    def body(x_ref, out_ref, send_sems, recv_sems, local_sem):
        x, y, c = _place()
        me, sibling = (x, y, c), (x, y, 1 - c)
        chips = [(1 - x, y), (x, 1 - y), (1 - x, 1 - y)]

        def rows(px, py, pc):
            return out_ref.at[pl.ds((4 * px + 2 * py + pc) * m_per, m_per), :]

        def copy(k, blk, to, src=None):
            return pltpu.make_async_remote_copy(
                src_ref=rows(*blk) if src is None else src, dst_ref=rows(*blk),
                send_sem=send_sems.at[k], recv_sem=recv_sems.at[k], device_id=to, device_id_type=MESH)

        mine = pltpu.make_async_copy(x_ref, rows(*me), local_sem)
        mine.start()
        first = [copy(0, me, sibling, src=x_ref)]
        first += [copy(1 + j, me, (*chip, c), src=x_ref) for j, chip in enumerate(chips)]
        for cp in first:
            cp.start()
        passed = [copy(4 + j, (*chip, c), sibling) for j, chip in enumerate(chips)]
        for j, chip in enumerate(chips):
            copy(1 + j, (*chip, c), me).wait_recv()
            passed[j].start()
        copy(0, sibling, me).wait_recv()
        for j, chip in enumerate(chips):
            copy(4 + j, (*chip, 1 - c), me).wait_recv()
        for cp in first + passed:
            cp.wait_send()
        mine.wait()

    return _pallas(
        body, name=name,
        out_shape=jax.ShapeDtypeStruct((N_DEV * m_per, n), block.dtype),
        in_specs=[VMEM_FULL], out_specs=VMEM_FULL,
        scratch_shapes=[pltpu.SemaphoreType.DMA((7,)), pltpu.SemaphoreType.DMA((7,)), pltpu.SemaphoreType.DMA],
    )(block)


def _shard_slice(ref, full_shape, shard_axis, chip, half):
    m, n = full_shape
    if shard_axis == 0:
        r = m // N_CHIP
        return ref.at[pl.ds(chip * r + half * (r // 2), r // 2), :]
    cw = n // N_CHIP
    return ref.at[pl.ds(half * (m // 2), m // 2), pl.ds(pl.multiple_of(chip * cw, 128), cw)]


def _gather_weights(shards, axes):
    n_w = len(shards)
    fulls = []
    for s, ax in zip(shards, axes):
        r, cc = s.shape
        fulls.append((r * N_CHIP, cc) if ax == 0 else (r, cc * N_CHIP))

    def body(*refs):
        ins, outs = refs[:n_w], refs[n_w:2 * n_w]
        send_sems, recv_sems, local_sems = refs[2 * n_w:]
        x, y, c = _place()
        chip_me = 2 * x + y
        sibling = (x, y, 1 - c)
        chips = [(1 - x, y), (x, 1 - y), (1 - x, 1 - y)]
        waits = []
        for a in range(n_w):
            r = shards[a].shape[0]

            def dst(chip, half, a=a):
                return _shard_slice(outs[a], fulls[a], axes[a], chip, half)

            for half in range(2):
                cp = pltpu.make_async_copy(ins[a].at[pl.ds(half * (r // 2), r // 2), :], dst(chip_me, half),
                                           local_sems.at[a, half])
                cp.start()
                waits.append(cp.wait)
            src = ins[a].at[pl.ds(c * (r // 2), r // 2), :]
            for j, (px, py) in enumerate(chips):
                cp = pltpu.make_async_remote_copy(src_ref=src, dst_ref=dst(chip_me, c), send_sem=send_sems.at[a, j],
                                                  recv_sem=recv_sems.at[a, j], device_id=(px, py, c), device_id_type=MESH)
                cp.start()
                waits.append(cp.wait_send)
        for a in range(n_w):
            def dst(chip, half, a=a):
                return _shard_slice(outs[a], fulls[a], axes[a], chip, half)

            for j, (px, py) in enumerate(chips):
                landed = dst(2 * px + py, c)
                pltpu.make_async_remote_copy(src_ref=landed, dst_ref=landed, send_sem=send_sems.at[a, j],
                                             recv_sem=recv_sems.at[a, j], device_id=(px, py, c),
                                             device_id_type=MESH).wait_recv()
                cp = pltpu.make_async_remote_copy(src_ref=landed, dst_ref=landed, send_sem=send_sems.at[a, 3 + j],
                                                  recv_sem=recv_sems.at[a, 3 + j], device_id=sibling, device_id_type=MESH)
                cp.start()
                waits.append(cp.wait_send)
        for a in range(n_w):
            for j, (px, py) in enumerate(chips):
                other = _shard_slice(outs[a], fulls[a], axes[a], 2 * px + py, 1 - c)
                pltpu.make_async_remote_copy(src_ref=other, dst_ref=other, send_sem=send_sems.at[a, 3 + j],
                                             recv_sem=recv_sems.at[a, 3 + j], device_id=sibling,
                                             device_id_type=MESH).wait_recv()
        for w in waits:
            w()

    return _pallas(
        body, name="gather_weights",
        out_shape=tuple(jax.ShapeDtypeStruct(f, BF16) for f in fulls),
        in_specs=[ANY] * n_w, out_specs=tuple([ANY] * n_w),
        scratch_shapes=[pltpu.SemaphoreType.DMA((n_w, 6)), pltpu.SemaphoreType.DMA((n_w, 6)),
                        pltpu.SemaphoreType.DMA((n_w, 2))],
    )(*shards)


def _exchange_grads(grads, axes):
    n_w = len(grads)
    pieces = []
    for g, ax in zip(grads, axes):
        m, n = g.shape
        pieces.append((m // N_CHIP // 2, n) if ax == 0 else (m // 2, n // N_CHIP))

    def body(*refs):
        ins, outs = refs[:n_w], refs[n_w:2 * n_w]
        send_sems, recv_sems = refs[2 * n_w:]
        x, y, c = _place()
        me = 4 * x + 2 * y + c
        waits = []
        for a in range(n_w):
            for t in range(N_DEV):
                tx, ty, tc = t // 4, (t // 2) % 2, t % 2
                src = _shard_slice(ins[a], grads[a].shape, axes[a], 2 * tx + ty, tc)

                @pl.when(me == t)
                def _(src=src, a=a):
                    pltpu.make_async_copy(src, outs[a].at[me], send_sems.at[a, t]).start()

                @pl.when(me != t)
                def _(src=src, a=a, to=(tx, ty, tc)):
                    pltpu.make_async_remote_copy(src_ref=src, dst_ref=outs[a].at[me], send_sem=send_sems.at[a, t],
                                                 recv_sem=recv_sems.at[a, me], device_id=to, device_id_type=MESH).start()
        for a in range(n_w):
            for t in range(N_DEV):
                slot = outs[a].at[t]

                @pl.when(me == t)
                def _(slot=slot, a=a):
                    pltpu.make_async_copy(slot, slot, send_sems.at[a, t]).wait()

                @pl.when(me != t)
                def _(slot=slot, a=a):
                    cp = pltpu.make_async_remote_copy(src_ref=slot, dst_ref=slot, send_sem=send_sems.at[a, t],
                                                      recv_sem=recv_sems.at[a, t], device_id=(x, y, c), device_id_type=MESH)
                    cp.wait_send()
                    cp.wait_recv()

    return _pallas(
        body, name="exchange_grads",
        out_shape=tuple(jax.ShapeDtypeStruct((N_DEV,) + p, BF16) for p in pieces),
        in_specs=[ANY] * n_w, out_specs=tuple([ANY] * n_w),
        scratch_shapes=[pltpu.SemaphoreType.DMA((n_w, N_DEV)), pltpu.SemaphoreType.DMA((n_w, N_DEV))],
    )(*grads)


def _swap_halves(halves):
    n_w = len(halves)

    def body(*refs):
        ins, outs = refs[:n_w], refs[n_w:2 * n_w]
        send_sems, recv_sems, local_sems = refs[2 * n_w:]
        x, y, c = _place()
        sibling = (x, y, 1 - c)
        waits = []
        for a in range(n_w):
            r = halves[a].shape[0]
            mine = outs[a].at[pl.ds(c * r, r), :]
            theirs = outs[a].at[pl.ds((1 - c) * r, r), :]
            lc = pltpu.make_async_copy(ins[a], mine, local_sems.at[a])
            lc.start()
            rc = pltpu.make_async_remote_copy(src_ref=ins[a], dst_ref=mine, send_sem=send_sems.at[a], recv_sem=recv_sems.at[a],
                                              device_id=sibling, device_id_type=MESH)
            rc.start()
            got = pltpu.make_async_remote_copy(src_ref=theirs, dst_ref=theirs, send_sem=send_sems.at[a],
                                               recv_sem=recv_sems.at[a], device_id=sibling, device_id_type=MESH)
            waits += [lc.wait, rc.wait_send, got.wait_recv]
        for w in waits:
            w()

    return _pallas(
        body, name="swap_halves",
        out_shape=tuple(jax.ShapeDtypeStruct((2 * h.shape[0], h.shape[1]), h.dtype) for h in halves),
        in_specs=[ANY] * n_w, out_specs=tuple([ANY] * n_w),
        scratch_shapes=[pltpu.SemaphoreType.DMA((n_w,)), pltpu.SemaphoreType.DMA((n_w,)), pltpu.SemaphoreType.DMA((n_w,))],
    )(*halves)


def _load_weights(pairs, sems):
    @pl.when(pl.program_id(0) == 0)
    def _():
        cps = [pltpu.make_async_copy(s, d, sems.at[k]) for k, (s, d) in enumerate(pairs)]
        for cp in cps:
            cp.start()
        for cp in cps:
            cp.wait()


def _mod_part(c_all, w_ada_bf, b_part):
    def body(c_ref, w_ref, b_ref, o_ref):
        cv = c_ref[...]
        o_ref[...] = _dot((cv * _sig(cv)).astype(BF16), w_ref[...]) + b_ref[...]

    return _pallas(body, name="mod_part", out_shape=jax.ShapeDtypeStruct((c_all.shape[0], w_ada_bf.shape[1]), F32),
                   in_specs=[VMEM_FULL] * 3, out_specs=VMEM_FULL, compiler_params=_params())(c_all, w_ada_bf, b_part)


def _fwd_mix(x, mod, gn1, w3, w31, cbias, lng, lnb, w_in, w_a, w_b, w_o, ts):
    s_len, d = x.shape
    n_t = s_len // ts

    def body(x_ref, mod_ref, gn1_ref, w3_ref, w31_ref, cb_ref, lng_ref, lnb_ref, win_hbm, wa_hbm, wb_hbm, wo_hbm,
             proj_ref, h_ref, yap_ref, c3_ref, n_ref, rstd_ref, u3_ref, ya_ref, yb_ref, mg_ref, mix_ref, x1_ref,
             win_v, wa_v, wb_v, wo_v, cv_ext, u0_ext, sems):
        _load_weights([(win_hbm, win_v), (wa_hbm, wa_v), (wb_hbm, wb_v), (wo_hbm, wo_v)], sems)

        @pl.when(pl.program_id(0) == 0)
        def _():
            cv_ext[0:HALO_SHORT, :] = jnp.zeros((HALO_SHORT, d), F32)
            u0_ext[0:HALO_CONF, :] = jnp.zeros((HALO_CONF, d), F32)

        xv = x_ref[...]
        sh1, sc1, g1 = mod_ref[:, 0:d], mod_ref[:, d:2 * d], mod_ref[:, 2 * d:3 * d]
        r = lax.rsqrt(jnp.mean(xv * xv, axis=-1, keepdims=True) + EPS)
        hb = (((xv * r) * gn1_ref[...]) * (1.0 + sc1) + sh1).astype(BF16)
        h_ref[...] = hb

        def pj(k):
            p = _dot(hb, win_v[:, k * d:(k + 1) * d])
            proj_ref[:, k * d:(k + 1) * d] = p.astype(BF16)
            return p

        c_s = pj(1)
        cv_ext[HALO_SHORT:HALO_SHORT + ts, :] = c_s * pj(2)
        conv3 = jnp.zeros((ts, d), F32)
        for k in range(SHORT_K):
            off = HALO_SHORT - (SHORT_K - 1) + k
            conv3 = conv3 + w3_ref[k:k + 1, :] * cv_ext[off:off + ts, :]
        cv_ext[0:HALO_SHORT, :] = cv_ext[ts:ts + HALO_SHORT, :]
        c3_ref[...] = conv3.astype(BF16)
        yap = (pj(0) * conv3).astype(BF16)
        yap_ref[...] = yap
        y_a = _dot(yap, wa_v[...])
        ya_ref[...] = y_a.astype(BF16)

        v_c = pj(3)
        u0_ext[HALO_CONF:HALO_CONF + ts, :] = v_c * _sig(pj(4))
        u1 = jnp.zeros((ts, d), F32)
        for k in range(CONF_K):
            off = HALO_CONF - (CONF_K - 1) + k
            u1 = u1 + w31_ref[k:k + 1, :] * u0_ext[off:off + ts, :]
        u0_ext[0:HALO_CONF, :] = u0_ext[ts:ts + HALO_CONF, :]
        u1 = u1 + cb_ref[...]
        mu = jnp.mean(u1, axis=-1, keepdims=True)
        var = jnp.mean(jnp.square(u1 - mu), axis=-1, keepdims=True)
        rstd = lax.rsqrt(var + LN_EPS)
        nrm = (u1 - mu) * rstd
        n_ref[...] = nrm.astype(BF16)
        rstd_ref[...] = rstd
        u2 = nrm * lng_ref[...] + lnb_ref[...]
        u3 = (u2 * _sig(u2)).astype(BF16)
        u3_ref[...] = u3
        y_b = _dot(u3, wb_v[...])
        yb_ref[...] = y_b.astype(BF16)

        merged = (_sig(pj(5)) * y_a + _sig(pj(6)) * y_b).astype(BF16)
        mg_ref[...] = merged
        mix = _dot(merged, wo_v[...])
        mix_ref[...] = mix
        x1_ref[...] = xv + g1 * mix

    row = lambda w, dt=None: pl.BlockSpec((ts, w), lambda i: (i, 0))
    full = lambda a: pl.BlockSpec(a.shape, lambda i: (0, 0))
    sd = lambda w, dt: jax.ShapeDtypeStruct((s_len, w), dt)
    outs = [sd(7 * d, BF16), sd(d, BF16), sd(d, BF16), sd(d, BF16), sd(d, BF16), sd(1, F32), sd(d, BF16), sd(d, BF16),
            sd(d, BF16), sd(d, BF16), sd(d, F32), sd(d, F32)]
    return _pallas(
        body, name="fwd_mix", grid=(n_t,), out_shape=tuple(outs),
        in_specs=[row(d), full(mod), full(gn1), full(w3), full(w31), full(cbias), full(lng), full(lnb), ANY, ANY, ANY, ANY],
        out_specs=tuple(row(o.shape[1]) for o in outs),
        scratch_shapes=[pltpu.VMEM(w_in.shape, BF16), pltpu.VMEM(w_a.shape, BF16), pltpu.VMEM(w_b.shape, BF16),
                        pltpu.VMEM(w_o.shape, BF16), pltpu.VMEM((ts + HALO_SHORT, d), F32),
                        pltpu.VMEM((ts + HALO_CONF, d), F32), pltpu.SemaphoreType.DMA((4,))],
        compiler_params=_params("arbitrary"),
    )(x, mod, gn1, w3, w31, cbias, lng, lnb, w_in, w_a, w_b, w_o)


def _ffn(x1, target, mod, gn2, gf, w_fi, w_fo, ts):
    s_len, d = x1.shape
    f = w_fo.shape[0]
    n_t = s_len // ts

    def body(x1_ref, t_ref, mod_ref, gn2_ref, gf_ref, wfi_hbm, wfo_hbm,
             h2_ref, hid_ref, df_ref, dab_ref, dx1_ref, acc_ref, wfi_v, wfo_v, sems):
        _load_weights([(wfi_hbm, wfi_v), (wfo_hbm, wfo_v)], sems)

        @pl.when(pl.program_id(0) == 0)
        def _():
            acc_ref[...] = jnp.zeros(acc_ref.shape, F32)

        x1v = x1_ref[...]
        sh2, sc2, g2 = mod_ref[:, 3 * d:4 * d], mod_ref[:, 4 * d:5 * d], mod_ref[:, 5 * d:6 * d]
        gn2v, gfv = gn2_ref[...], gf_ref[...]
        r2 = lax.rsqrt(jnp.mean(x1v * x1v, axis=-1, keepdims=True) + EPS)
        xn2 = x1v * r2
        h2 = ((xn2 * gn2v) * (1.0 + sc2) + sh2).astype(BF16)
        h2_ref[...] = h2
        a = _dot(h2, wfi_v[:, 0:f])
        bg = _dot(h2, wfi_v[:, f:2 * f])
        sa = _sig(a)
        silu_a = a * sa
        hid = (silu_a * bg).astype(BF16)
        hid_ref[...] = hid
        ffn = _dot(hid, wfo_v[...])
        x2 = x1v + g2 * ffn
        r3 = lax.rsqrt(jnp.mean(x2 * x2, axis=-1, keepdims=True) + EPS)
        xn3 = x2 * r3
        diff = xn3 * gfv - t_ref[...]
        dy = diff * (1.0 / d)
        dxn3 = dy * gfv
        dx2 = r3 * (dxn3 - xn3 * jnp.mean(dxn3 * xn3, axis=-1, keepdims=True))
        acc_ref[0:1, :] += _colsum(dy * xn3)
        acc_ref[1:2, :] += _colsum(dx2 * ffn)
        acc_ref[2:3, :] += _colsum(diff * diff)

        df = (dx2 * g2).astype(BF16)
        df_ref[...] = df
        dhid = _dot_nt(df, wfo_v[...])
        da = dhid * bg * (sa * (1.0 + a * (1.0 - sa)))
        dbg = dhid * silu_a
        dab_ref[:, 0:f] = da.astype(BF16)
        dab_ref[:, f:2 * f] = dbg.astype(BF16)
        dh2 = _dot_nt(da.astype(BF16), wfi_v[:, 0:f]) + _dot_nt(dbg.astype(BF16), wfi_v[:, f:2 * f])
        acc_ref[3:4, :] += _colsum(dh2)
        acc_ref[4:5, :] += _colsum(dh2 * (xn2 * gn2v))
        acc_ref[5:6, :] += _colsum(dh2 * xn2 * (1.0 + sc2))
        dxn2 = dh2 * gn2v * (1.0 + sc2)
        dx1_ref[...] = dx2 + r2 * (dxn2 - xn2 * jnp.mean(dxn2 * xn2, axis=-1, keepdims=True))

    row = lambda w: pl.BlockSpec((ts, w), lambda i: (i, 0))
    full = lambda a: pl.BlockSpec(a.shape, lambda i: (0, 0))
    sd = lambda w, dt: jax.ShapeDtypeStruct((s_len, w), dt)
    outs = [sd(d, BF16), sd(f, BF16), sd(d, BF16), sd(2 * f, BF16), sd(d, F32), jax.ShapeDtypeStruct((8, d), F32)]
    return _pallas(
        body, name="ffn", grid=(n_t,), out_shape=tuple(outs),
        in_specs=[row(d), row(d), full(mod), full(gn2), full(gf), ANY, ANY],
        out_specs=tuple([row(o.shape[1]) for o in outs[:-1]] + [pl.BlockSpec((8, d), lambda i: (0, 0))]),
        scratch_shapes=[pltpu.VMEM(w_fi.shape, BF16), pltpu.VMEM(w_fo.shape, BF16), pltpu.SemaphoreType.DMA((2,))],
        compiler_params=_params("arbitrary"),
    )(x1, target, mod, gn2, gf, w_fi, w_fo)


def _bwd_mix(dx1, proj, nrm, rstd, c3, ya, yb, mix, mod, w3, w31, lng, lnb, w_a, w_b, w_o, ts):
    s_len, d = dx1.shape
    n_t = s_len // ts

    def body(dx1_ref, proj_ref, n_ref, rstd_ref, c3_ref, ya_ref, yb_ref, mix_ref, mod_ref, w3_ref, w31_ref, lng_ref,
             lnb_ref, wa_hbm, wb_hbm, wo_hbm, dproj_ref, dmix_ref, dya_ref, dyb_ref, acc_ref,
             wa_v, wb_v, wo_v, dc3_ext, du1_ext, sems):
        _load_weights([(wa_hbm, wa_v), (wb_hbm, wb_v), (wo_hbm, wo_v)], sems)

        @pl.when(pl.program_id(0) == 0)
        def _():
            acc_ref[...] = jnp.zeros(acc_ref.shape, F32)
            dc3_ext[ts:ts + HALO_SHORT, :] = jnp.zeros((HALO_SHORT, d), F32)
            du1_ext[ts:ts + HALO_CONF, :] = jnp.zeros((HALO_CONF, d), F32)

        pj = lambda k: proj_ref[:, k * d:(k + 1) * d].astype(F32)
        g1 = mod_ref[:, 2 * d:3 * d]
        dx1v = dx1_ref[...]
        acc_ref[0:1, :] += _colsum(dx1v * mix_ref[...])
        dmix = (dx1v * g1).astype(BF16)
        dmix_ref[...] = dmix
        dmerged = _dot_nt(dmix, wo_v[...])
        g_a, g_b = _sig(pj(5)), _sig(pj(6))
        y_a, y_b = ya_ref[...].astype(F32), yb_ref[...].astype(F32)
        dproj_ref[:, 5 * d:6 * d] = (dmerged * y_a * g_a * (1.0 - g_a)).astype(BF16)
        dproj_ref[:, 6 * d:7 * d] = (dmerged * y_b * g_b * (1.0 - g_b)).astype(BF16)
        dya = (dmerged * g_a).astype(BF16)
        dyb = (dmerged * g_b).astype(BF16)
        dya_ref[...] = dya
        dyb_ref[...] = dyb

        dyap = _dot_nt(dya, wa_v[...])
        dproj_ref[:, 0:d] = (dyap * c3_ref[...].astype(F32)).astype(BF16)
        dc3_ext[0:ts, :] = dyap * pj(0)
        c_s, v_s = pj(1), pj(2)
        cv = c_s * v_s
        dcv = jnp.zeros((ts, d), F32)
        for j in range(SHORT_K):
            k = SHORT_K - 1 - j
            sj = dc3_ext[j:j + ts, :]
            dcv = dcv + w3_ref[k:k + 1, :] * sj
            acc_ref[1 + k:2 + k, :] += _colsum(cv * sj)
        dc3_ext[ts:ts + HALO_SHORT, :] = dc3_ext[0:HALO_SHORT, :]
        dproj_ref[:, d:2 * d] = (dcv * v_s).astype(BF16)
        dproj_ref[:, 2 * d:3 * d] = (dcv * c_s).astype(BF16)

        du3 = _dot_nt(dyb, wb_v[...])
        nv = n_ref[...].astype(F32)
        lg = lng_ref[...]
        u2 = nv * lg + lnb_ref[...]
        s2 = _sig(u2)
        du2 = du3 * (s2 * (1.0 + u2 * (1.0 - s2)))
        acc_ref[5:6, :] += _colsum(du2 * nv)
        acc_ref[6:7, :] += _colsum(du2)
        dn = du2 * lg
        du1 = rstd_ref[...] * (dn - jnp.mean(dn, axis=-1, keepdims=True) - nv * jnp.mean(dn * nv, axis=-1, keepdims=True))
        acc_ref[4:5, :] += _colsum(du1)
        du1_ext[0:ts, :] = du1
        v_c = pj(3)
        sg = _sig(pj(4))
        u0 = v_c * sg
        du0 = jnp.zeros((ts, d), F32)
        for j in range(CONF_K):
            k = CONF_K - 1 - j
            sj = du1_ext[j:j + ts, :]
            du0 = du0 + w31_ref[k:k + 1, :] * sj
            acc_ref[8 + k:9 + k, :] += _colsum(u0 * sj)
        du1_ext[ts:ts + HALO_CONF, :] = du1_ext[0:HALO_CONF, :]
        dproj_ref[:, 3 * d:4 * d] = (du0 * sg).astype(BF16)
        dproj_ref[:, 4 * d:5 * d] = (du0 * v_c * sg * (1.0 - sg)).astype(BF16)

    rev = lambda w: pl.BlockSpec((ts, w), lambda i: (n_t - 1 - i, 0))
    full = lambda a: pl.BlockSpec(a.shape, lambda i: (0, 0))
    sd = lambda w, dt: jax.ShapeDtypeStruct((s_len, w), dt)
    outs = [sd(7 * d, BF16), sd(d, BF16), sd(d, BF16), sd(d, BF16), jax.ShapeDtypeStruct((40, d), F32)]
    return _pallas(
        body, name="bwd_mix", grid=(n_t,), out_shape=tuple(outs),
        in_specs=[rev(d), rev(7 * d), rev(d), rev(1), rev(d), rev(d), rev(d), rev(d), full(mod), full(w3), full(w31),
                  full(lng), full(lnb), ANY, ANY, ANY],
        out_specs=tuple([rev(o.shape[1]) for o in outs[:-1]] + [pl.BlockSpec((40, d), lambda i: (0, 0))]),
        scratch_shapes=[pltpu.VMEM(w_a.shape, BF16), pltpu.VMEM(w_b.shape, BF16), pltpu.VMEM(w_o.shape, BF16),
                        pltpu.VMEM((ts + HALO_SHORT, d), F32), pltpu.VMEM((ts + HALO_CONF, d), F32),
                        pltpu.SemaphoreType.DMA((3,))],
        compiler_params=_params("arbitrary"),
    )(dx1, proj, nrm, rstd, c3, ya, yb, mix, mod, w3, w31, lng, lnb, w_a, w_b, w_o)


def _bwd_in(dproj, x, dx1, mod, gn1, w_in, ts):
    s_len, d = x.shape
    n_t = s_len // ts

    def body(dproj_ref, x_ref, dx1_ref, mod_ref, gn1_ref, win_hbm, gx_ref, acc_ref, win_v, sems):
        _load_weights([(win_hbm, win_v)], sems)

        @pl.when(pl.program_id(0) == 0)
        def _():
            acc_ref[...] = jnp.zeros(acc_ref.shape, F32)

        xv = x_ref[...]
        sc1 = mod_ref[:, d:2 * d]
        gn1v = gn1_ref[...]
        r = lax.rsqrt(jnp.mean(xv * xv, axis=-1, keepdims=True) + EPS)
        xn = xv * r
        dh = _dot_nt(dproj_ref[...], win_v[...])
        acc_ref[0:1, :] += _colsum(dh)
        acc_ref[1:2, :] += _colsum(dh * (xn * gn1v))
        acc_ref[2:3, :] += _colsum(dh * xn * (1.0 + sc1))
        dxn = dh * gn1v * (1.0 + sc1)
        gx_ref[...] = dx1_ref[...] + r * (dxn - xn * jnp.mean(dxn * xn, axis=-1, keepdims=True))

    row = lambda w: pl.BlockSpec((ts, w), lambda i: (i, 0))
    full = lambda a: pl.BlockSpec(a.shape, lambda i: (0, 0))
    return _pallas(
        body, name="bwd_in", grid=(n_t,),
        out_shape=(jax.ShapeDtypeStruct((s_len, d), F32), jax.ShapeDtypeStruct((8, d), F32)),
        in_specs=[row(7 * d), row(d), row(d), full(mod), full(gn1), ANY],
        out_specs=(row(d), pl.BlockSpec((8, d), lambda i: (0, 0))),
        scratch_shapes=[pltpu.VMEM(w_in.shape, BF16), pltpu.SemaphoreType.DMA((1,))],
        compiler_params=_params("arbitrary"),
    )(dproj, x, dx1, mod, gn1, w_in)


def _weight_grad(a, b, bm, bn, tk, name):
    s_len, m = a.shape
    n = b.shape[1]
    n_k = s_len // tk

    def body(a_ref, b_ref, o_ref, acc):
        k = pl.program_id(2)

        @pl.when(k == 0)
        def _():
            acc[...] = jnp.zeros(acc.shape, F32)

        acc[...] += _dot_tn(a_ref[...], b_ref[...])

        @pl.when(k == n_k - 1)
        def _():
            o_ref[...] = acc[...].astype(BF16)

    return _pallas(
        body, name=name, grid=(m // bm, n // bn, n_k), out_shape=jax.ShapeDtypeStruct((m, n), BF16),
        in_specs=[pl.BlockSpec((tk, bm), lambda i, j, k: (k, i)), pl.BlockSpec((tk, bn), lambda i, j, k: (k, j))],
        out_specs=pl.BlockSpec((bm, bn), lambda i, j, k: (i, j)),
        scratch_shapes=[pltpu.VMEM((bm, bn), F32)],
        compiler_params=_params("arbitrary", "arbitrary", "arbitrary"),
    )(a, b)


def _sum_slots(recv, name):
    n_s, r, c = recv.shape
    tr = 16
    for cand in (256, 128, 176, 64, 32, 16):
        if r % cand == 0:
            tr = cand
            break

    def body(r_ref, o_ref):
        tot = r_ref[0].astype(F32)
        for s in range(1, n_s):
            tot = tot + r_ref[s].astype(F32)
        o_ref[...] = tot

    return _pallas(body, name=name, grid=(r // tr,), out_shape=jax.ShapeDtypeStruct((r, c), F32),
                   in_specs=[pl.BlockSpec((n_s, tr, c), lambda i: (0, i, 0))], out_specs=pl.BlockSpec((tr, c), lambda i: (i, 0)),
                   compiler_params=_params("arbitrary"))(recv)


def _adamw_math(w, g, m, v):
    m = ADAM_B1 * m + (1.0 - ADAM_B1) * g
    v = ADAM_B2 * v + (1.0 - ADAM_B2) * (g * g)
    m_hat = m / (1.0 - ADAM_B1 ** ADAM_STEP)
    v_hat = v / (1.0 - ADAM_B2 ** ADAM_STEP)
    delta = -ADAM_LR * (m_hat / (jnp.sqrt(v_hat) + ADAM_EPS) + ADAM_WD * w)
    return delta, m, v


def _adamw(w, g, m, v, name):
    r, c = w.shape
    tr = r
    for cand in (256, 128, 176, 64, 32, 16, 8):
        if r % cand == 0:
            tr = cand
            break

    def body(w_ref, g_ref, m_ref, v_ref, d_ref, nm_ref, nv_ref):
        d_ref[...], nm_ref[...], nv_ref[...] = _adamw_math(w_ref[...], g_ref[...], m_ref[...], v_ref[...])

    blk = pl.BlockSpec((tr, c), lambda i: (i, 0))
    return _pallas(body, name=name, grid=(r // tr,), out_shape=tuple(jax.ShapeDtypeStruct((r, c), F32) for _ in range(3)),
                   in_specs=[blk] * 4, out_specs=(blk,) * 3, compiler_params=_params("arbitrary"))(w, g, m, v)


def _small_update(p_all, w, m, v):
    def body(p_ref, w_ref, m_ref, v_ref, g_ref, d_ref, nm_ref, nv_ref):
        g = p_ref[0]
        for s in range(1, N_DEV):
            g = g + p_ref[s]
        g_ref[...] = g
        d_ref[...], nm_ref[...], nv_ref[...] = _adamw_math(w_ref[...], g, m_ref[...], v_ref[...])

    return _pallas(body, name="small_update", out_shape=tuple(jax.ShapeDtypeStruct(w.shape, F32) for _ in range(4)),
                   in_specs=[VMEM_FULL] * 4, out_specs=(VMEM_FULL,) * 4, compiler_params=_params())(p_all, w, m, v)


def _w_ada_grad(c_t, dmod_part):
    def body(c_ref, g_ref, o_ref):
        cv = c_ref[...]
        o_ref[...] = _dot((cv * _sig(cv)).astype(BF16), g_ref[...].astype(BF16))

    return _pallas(body, name="w_ada_grad", out_shape=jax.ShapeDtypeStruct((c_t.shape[0], dmod_part.shape[1]), F32),
                   in_specs=[VMEM_FULL] * 2, out_specs=VMEM_FULL, compiler_params=_params())(c_t, dmod_part)


def _step(x, c, w_ada, b_ada, norm_mix_g, w_in, conv_short_w, w_short_out, conv_conf_w, conv_conf_b, conf_ln_g,
          conf_ln_b, w_conf_out, w_o, norm_ffn_g, w_ffn_in, w_ffn_out, final_norm_g, loss_target, moments, ts):
    xi, yi, ci = _place()
    chip = 2 * xi + yi
    me = 4 * xi + 2 * yi + ci
    x2d, tgt = x[0], loss_target[0]
    s_len, d = x2d.shape
    dq = d // N_CHIP
    mq = N_MOD * d // N_CHIP
    gf = final_norm_g.reshape(1, d)

    c_all = _all_gather_small(jnp.broadcast_to(c, (8, d)), "gather_c").reshape(N_DEV, 8, d)[:, 0]
    b_part = lax.dynamic_slice(b_ada, (0, chip * mq), (1, mq))
    mod_part = _mod_part(c_all, w_ada[0].astype(BF16), b_part)
    mod_all = _all_gather_small(mod_part, "gather_mod").reshape(N_CHIP, 2, N_DEV, mq)[:, 0]
    mod_full = jnp.transpose(mod_all, (1, 0, 2)).reshape(N_DEV, N_MOD * d)
    mod = lax.dynamic_slice(mod_full, (me, 0), (1, N_MOD * d))

    taps = jnp.concatenate([jnp.pad(conv_short_w[0], ((0, 8 - SHORT_K), (0, 0))),
                            jnp.pad(conv_conf_w[0], ((0, 32 - CONF_K), (0, 0)))], axis=0)
    taps_all = _all_gather_small(taps, "gather_taps").reshape(N_CHIP, 2, 40, dq)[:, 0]
    taps_full = jnp.transpose(taps_all, (1, 0, 2)).reshape(40, d)
    w3, w31 = taps_full[0:SHORT_K], taps_full[8:8 + CONF_K]

    big = [w_in[0], w_short_out[0], w_conf_out[0], w_o[0], w_ffn_in[0], w_ffn_out[0]]
    axes = [1, 0, 0, 0, 1, 0]
    win_f, wa_f, wb_f, wo_f, wfi_f, wfo_f = _gather_weights([w.astype(BF16) for w in big], axes)

    (proj, h, yap, c3, nrm, rstd, u3, ya, yb, merged, mix, x1) = _fwd_mix(
        x2d, mod, norm_mix_g, w3, w31, conv_conf_b, conf_ln_g, conf_ln_b, win_f, wa_f, wb_f, wo_f, ts)
    h2, hid, df, dab, dx1, acc_f = _ffn(x1, tgt, mod, norm_ffn_g, gf, wfi_f, wfo_f, ts)
    dproj, dmix, dya, dyb, acc_m = _bwd_mix(dx1, proj, nrm, rstd, c3, ya, yb, mix, mod, w3, w31, conf_ln_g, conf_ln_b,
                                            wa_f, wb_f, wo_f, ts)
    grad_x, acc_i = _bwd_in(dproj, x2d, dx1, mod, norm_mix_g, win_f, ts)

    tk = min(512, s_len)
    f = wfo_f.shape[0]
    grads = [
        _weight_grad(h, dproj, d, 7 * d // N_CHIP, tk, "grad_w_in"),
        _weight_grad(yap, dya, d, d, tk, "grad_w_short_out"),
        _weight_grad(u3, dyb, d, d, tk, "grad_w_conf_out"),
        _weight_grad(merged, dmix, d, d, tk, "grad_w_o"),
        _weight_grad(h2, dab, d, 2 * f // N_CHIP, tk, "grad_w_ffn_in"),
        _weight_grad(hid, df, f, d, tk, "grad_w_ffn_out"),
    ]
    recv = _exchange_grads(grads, axes)
    halves = [_sum_slots(r, "sum_slots_%d" % k) for k, r in enumerate(recv)]
    g_big = _swap_halves(halves)

    zrow = jnp.zeros((1, d), F32)
    p_local = jnp.concatenate([
        acc_i[0:2], acc_m[0:1], acc_f[3:5], acc_f[1:2], acc_i[2:3], acc_m[1:4], acc_m[8:8 + CONF_K], acc_m[4:7],
        acc_f[5:6], acc_f[0:1], zrow, zrow], axis=0)
    p_all = _all_gather_small(p_local, "gather_small_grads").reshape(N_DEV, N_SMALL_ROWS, d)

    def pack(b_ada_like, gn1_like, cs_like, cc_like, cb_like, lg_like, lb_like, gn2_like, gf_like):
        wide = lambda t: lax.dynamic_update_slice(jnp.zeros((t.shape[1], d), F32), t[0], (0, chip * dq))
        return jnp.concatenate([b_ada_like.reshape(N_MOD, d), gn1_like, wide(cs_like), wide(cc_like), cb_like, lg_like,
                                lb_like, gn2_like, gf_like.reshape(1, d), zrow, zrow], axis=0)

    small_names = ["b_ada", "norm_mix_g", "conv_short_w", "conv_conf_w", "conv_conf_b", "conf_ln_g", "conf_ln_b",
                   "norm_ffn_g", "final_norm_g"]
    small_w = dict(b_ada=b_ada, norm_mix_g=norm_mix_g, conv_short_w=conv_short_w, conv_conf_w=conv_conf_w,
                   conv_conf_b=conv_conf_b, conf_ln_g=conf_ln_g, conf_ln_b=conf_ln_b, norm_ffn_g=norm_ffn_g,
                   final_norm_g=final_norm_g)
    w_pack = pack(*[small_w[n] for n in small_names])
    m_pack = pack(*[moments["m_" + n] for n in small_names])
    v_pack = pack(*[moments["v_" + n] for n in small_names])
    small_out = _small_update(p_all, w_pack, m_pack, v_pack)

    def unpack(p):
        cut = lambda rows, k: lax.dynamic_slice(p[rows:rows + k], (0, chip * dq), (k, dq))[None]
        return dict(b_ada=p[0:N_MOD].reshape(1, N_MOD * d), norm_mix_g=p[6:7], conv_short_w=cut(7, SHORT_K),
                    conv_conf_w=cut(10, CONF_K), conv_conf_b=p[41:42], conf_ln_g=p[42:43], conf_ln_b=p[43:44],
                    norm_ffn_g=p[44:45], final_norm_g=p[45])

    small_g, small_d, small_m, small_v = [unpack(p) for p in small_out]

    dmod_all = p_all[:, 0:N_MOD].reshape(N_DEV, N_MOD * d)
    dmod_part = jnp.pad(lax.dynamic_slice(dmod_all, (0, chip * mq), (N_DEV, mq)), ((0, 128 - N_DEV), (0, 0)))
    c_t = jnp.pad(jnp.transpose(c_all), ((0, 0), (0, 128 - N_DEV)))
    g_ada = _w_ada_grad(c_t, dmod_part)

    big_names = ["w_in", "w_short_out", "w_conf_out", "w_o", "w_ffn_in", "w_ffn_out"]
    out_g, out_d, out_m, out_v = dict(small_g), dict(small_d), dict(small_m), dict(small_v)
    for name, w, g in zip(["w_ada"] + big_names, [w_ada[0]] + big, [g_ada] + list(g_big)):
        dl, nm, nv = _adamw(w, g, moments["m_" + name][0], moments["v_" + name][0], "adamw_" + name)
        out_g[name], out_d[name], out_m[name], out_v[name] = g[None], dl[None], nm[None], nv[None]

    loss = lax.psum(0.5 * jnp.sum(acc_f[2]) / d, ("x", "y", "c"))
    return loss, grad_x[None], out_g, out_d, out_m, out_v


_WEIGHTS = ["w_ada", "b_ada", "norm_mix_g", "w_in", "conv_short_w", "w_short_out", "conv_conf_w", "conv_conf_b",
            "conf_ln_g", "conf_ln_b", "w_conf_out", "w_o", "norm_ffn_g", "w_ffn_in", "w_ffn_out", "final_norm_g"]
ROW_TILE = 256


def kernel(x, c, w_ada, b_ada, norm_mix_g, w_in, conv_short_w, w_short_out, conv_conf_w, conv_conf_b, conf_ln_g, conf_ln_b, w_conf_out, w_o, norm_ffn_g, w_ffn_in, w_ffn_out, final_norm_g, loss_target, m_w_ada, m_b_ada, m_norm_mix_g, m_w_in, m_conv_short_w, m_w_short_out, m_conv_conf_w, m_conv_conf_b, m_conf_ln_g, m_conf_ln_b, m_w_conf_out, m_w_o, m_norm_ffn_g, m_w_ffn_in, m_w_ffn_out, m_final_norm_g, v_w_ada, v_b_ada, v_norm_mix_g, v_w_in, v_conv_short_w, v_w_short_out, v_conv_conf_w, v_conv_conf_b, v_conf_ln_g, v_conf_ln_b, v_w_conf_out, v_w_o, v_norm_ffn_g, v_w_ffn_in, v_w_ffn_out, v_final_norm_g):
    moments = dict(
        m_w_ada=m_w_ada, m_b_ada=m_b_ada, m_norm_mix_g=m_norm_mix_g, m_w_in=m_w_in, m_conv_short_w=m_conv_short_w,
        m_w_short_out=m_w_short_out, m_conv_conf_w=m_conv_conf_w, m_conv_conf_b=m_conv_conf_b, m_conf_ln_g=m_conf_ln_g,
        m_conf_ln_b=m_conf_ln_b, m_w_conf_out=m_w_conf_out, m_w_o=m_w_o, m_norm_ffn_g=m_norm_ffn_g, m_w_ffn_in=m_w_ffn_in,
        m_w_ffn_out=m_w_ffn_out, m_final_norm_g=m_final_norm_g,
        v_w_ada=v_w_ada, v_b_ada=v_b_ada, v_norm_mix_g=v_norm_mix_g, v_w_in=v_w_in, v_conv_short_w=v_conv_short_w,
        v_w_short_out=v_w_short_out, v_conv_conf_w=v_conv_conf_w, v_conv_conf_b=v_conv_conf_b, v_conf_ln_g=v_conf_ln_g,
        v_conf_ln_b=v_conf_ln_b, v_w_conf_out=v_w_conf_out, v_w_o=v_w_o, v_norm_ffn_g=v_norm_ffn_g, v_w_ffn_in=v_w_ffn_in,
        v_w_ffn_out=v_w_ffn_out, v_final_norm_g=v_final_norm_g)
    loss, grad_x, g, dl, nm, nv = _step(
        x, c, w_ada, b_ada, norm_mix_g, w_in, conv_short_w, w_short_out, conv_conf_w, conv_conf_b, conf_ln_g, conf_ln_b,
        w_conf_out, w_o, norm_ffn_g, w_ffn_in, w_ffn_out, final_norm_g, loss_target, moments, min(ROW_TILE, x.shape[1]))
    return (loss, grad_x, *[g[n] for n in _WEIGHTS], *[dl[n] for n in _WEIGHTS], *[nm[n] for n in _WEIGHTS],
            *[nv[n] for n in _WEIGHTS])
```

```python
import functools

import jax
import jax.numpy as jnp
from jax import lax
from jax.experimental import pallas as pl
from jax.experimental.pallas import tpu as pltpu

F32 = jnp.float32
BF16 = jnp.bfloat16
EPS = 1e-6
LN_EPS = 1e-5
SHORT_K = 3
CONF_K = 31
N_MOD = 6
N_DEV = 8
N_CHIP = 4
ADAM_LR = 0.001
ADAM_B1 = 0.9
ADAM_B2 = 0.999
ADAM_EPS = 1e-08
ADAM_WD = 0.01
ADAM_STEP = 10
MESH = pl.DeviceIdType.MESH
V7X_VMEM_LIMIT_BYTES = 60 * 1024 * 1024
HALO_SHORT = 8
HALO_CONF = 32
N_SMALL_ROWS = 48

ANY = pl.BlockSpec(memory_space=pl.ANY)
VMEM_FULL = pl.BlockSpec(memory_space=pltpu.VMEM)


def _pallas(body, **kw):
    return pl.pallas_call(body, **kw)


def _params(*sem):
    return pltpu.CompilerParams(dimension_semantics=sem, vmem_limit_bytes=V7X_VMEM_LIMIT_BYTES)


def _sig(z):
    return jax.nn.sigmoid(z)


def _dot(a, b):
    return jnp.dot(a, b, preferred_element_type=F32)


def _dot_nt(a, b):
    return lax.dot_general(a, b, (((1,), (1,)), ((), ())), preferred_element_type=F32)


def _dot_tn(a, b):
    return lax.dot_general(a, b, (((0,), (0,)), ((), ())), preferred_element_type=F32)


def _colsum(z):
    return jnp.sum(z, axis=0, keepdims=True)


def _place():
    return lax.axis_index("x"), lax.axis_index("y"), lax.axis_index("c")


def _all_gather_small(block, name):
    m_per, n = block.shape

    def body(x_ref, out_ref, send_sems, recv_sems, local_sem):
        x, y, c = _place()
        me, sibling = (x, y, c), (x, y, 1 - c)
        chips = [(1 - x, y), (x, 1 - y), (1 - x, 1 - y)]

        def rows(px, py, pc):
            return out_ref.at[pl.ds((4 * px + 2 * py + pc) * m_per, m_per), :]

        def copy(k, blk, to, src=None):
            return pltpu.make_async_remote_copy(
                src_ref=rows(*blk) if src is None else src, dst_ref=rows(*blk),
                send_sem=send_sems.at[k], recv_sem=recv_sems.at[k], device_id=to, device_id_type=MESH)

        mine = pltpu.make_async_copy(x_ref, rows(*me), local_sem)
        mine.start()
        first = [copy(0, me, sibling, src=x_ref)]
        first += [copy(1 + j, me, (*chip, c), src=x_ref) for j, chip in enumerate(chips)]
        for cp in first:
            cp.start()
        passed = [copy(4 + j, (*chip, c), sibling) for j, chip in enumerate(chips)]
        for j, chip in enumerate(chips):
            copy(1 + j, (*chip, c), me).wait_recv()
            passed[j].start()
        copy(0, sibling, me).wait_recv()
        for j, chip in enumerate(chips):
            copy(4 + j, (*chip, 1 - c), me).wait_recv()
        for cp in first + passed:
            cp.wait_send()
        mine.wait()

    return _pallas(
        body, name=name,
        out_shape=jax.ShapeDtypeStruct((N_DEV * m_per, n), block.dtype),
        in_specs=[VMEM_FULL], out_specs=VMEM_FULL,
        scratch_shapes=[pltpu.SemaphoreType.DMA((7,)), pltpu.SemaphoreType.DMA((7,)), pltpu.SemaphoreType.DMA],
    )(block)


def _shard_slice(ref, full_shape, shard_axis, chip, half):
    m, n = full_shape
    if shard_axis == 0:
        r = m // N_CHIP
        return ref.at[pl.ds(chip * r + half * (r // 2), r // 2), :]
    cw = n // N_CHIP
    return ref.at[pl.ds(half * (m // 2), m // 2), pl.ds(pl.multiple_of(chip * cw, 128), cw)]


def _place_shard(w, axis, chip, name):
    r, cc = w.shape
    tr = r if axis == 0 else min(256, r)
    full = (r * N_CHIP, cc) if axis == 0 else (r, cc * N_CHIP)

    def body(s_ref, w_ref, o_ref):
        o_ref[...] = w_ref[...].astype(BF16)

    out_map = (lambda i, s: (s[0], 0)) if axis == 0 else (lambda i, s: (i, s[0]))
    grid_spec = pltpu.PrefetchScalarGridSpec(
        num_scalar_prefetch=1, grid=(r // tr,), in_specs=[pl.BlockSpec((tr, cc), lambda i, s: (i, 0))],
        out_specs=pl.BlockSpec((tr, cc), out_map))
    return _pallas(body, name=name, grid_spec=grid_spec, out_shape=jax.ShapeDtypeStruct(full, BF16),
                   compiler_params=_params("arbitrary"))(jnp.reshape(chip, (1,)).astype(jnp.int32), w)


def _other_chips():
    x, y, _ = _place()
    return [(1 - x, y), (x, 1 - y), (1 - x, 1 - y)]


def _gather_send(refs, shapes, axes, send_sems, recv_sems):
    x, y, c = _place()
    for a in range(len(refs)):
        mine = _shard_slice(refs[a], shapes[a], axes[a], 2 * x + y, c)
        for j, (px, py) in enumerate(_other_chips()):
            pltpu.make_async_remote_copy(src_ref=mine, dst_ref=mine, send_sem=send_sems.at[a, j], recv_sem=recv_sems.at[a, j],
                                         device_id=(px, py, c), device_id_type=MESH).start()


def _gather_forward(refs, shapes, axes, send_sems, recv_sems):
    x, y, c = _place()
    for a in range(len(refs)):
        for j, (px, py) in enumerate(_other_chips()):
            landed = _shard_slice(refs[a], shapes[a], axes[a], 2 * px + py, c)
            pltpu.make_async_remote_copy(src_ref=landed, dst_ref=landed, send_sem=send_sems.at[a, j], recv_sem=recv_sems.at[a, j],
                                         device_id=(px, py, c), device_id_type=MESH).wait_recv()
            pltpu.make_async_remote_copy(src_ref=landed, dst_ref=landed, send_sem=send_sems.at[a, 3 + j],
                                         recv_sem=recv_sems.at[a, 3 + j], device_id=(x, y, 1 - c), device_id_type=MESH).start()


def _gather_finish(refs, shapes, axes, send_sems, recv_sems):
    x, y, c = _place()
    for a in range(len(refs)):
        for j, (px, py) in enumerate(_other_chips()):
            other = _shard_slice(refs[a], shapes[a], axes[a], 2 * px + py, 1 - c)
            landed = _shard_slice(refs[a], shapes[a], axes[a], 2 * px + py, c)
            mine = _shard_slice(refs[a], shapes[a], axes[a], 2 * x + y, c)
            me = (x, y, c)
            pltpu.make_async_remote_copy(src_ref=other, dst_ref=other, send_sem=send_sems.at[a, 3 + j],
                                         recv_sem=recv_sems.at[a, 3 + j], device_id=me, device_id_type=MESH).wait_recv()
            pltpu.make_async_remote_copy(src_ref=mine, dst_ref=mine, send_sem=send_sems.at[a, j], recv_sem=recv_sems.at[a, j],
                                         device_id=me, device_id_type=MESH).wait_send()
            pltpu.make_async_remote_copy(src_ref=landed, dst_ref=landed, send_sem=send_sems.at[a, 3 + j],
                                         recv_sem=recv_sems.at[a, 3 + j], device_id=me, device_id_type=MESH).wait_send()


def _gather_sems(n_w):
    return [pltpu.SemaphoreType.DMA((n_w, 6)), pltpu.SemaphoreType.DMA((n_w, 6))]


def _gather_weights(fulls, axes, name):
    n_w = len(fulls)
    shapes = [f.shape for f in fulls]

    def body(*refs):
        outs, sems = refs[n_w:2 * n_w], refs[2 * n_w:]
        _gather_send(outs, shapes, axes, *sems)
        _gather_forward(outs, shapes, axes, *sems)
        _gather_finish(outs, shapes, axes, *sems)

    return _pallas(
        body, name=name, out_shape=tuple(jax.ShapeDtypeStruct(f.shape, BF16) for f in fulls),
        in_specs=[ANY] * n_w, out_specs=tuple([ANY] * n_w), input_output_aliases={a: a for a in range(n_w)},
        scratch_shapes=_gather_sems(n_w),
    )(*fulls)


def _piece_shapes(grads, axes):
    return [((g.shape[0] // N_CHIP // 2, g.shape[1]) if ax == 0 else (g.shape[0] // 2, g.shape[1] // N_CHIP))
            for g, ax in zip(grads, axes)]


def _exchange_send(ins, outs, shapes, axes, send_sems, recv_sems):
    x, y, c = _place()
    me = 4 * x + 2 * y + c
    for a in range(len(ins)):
        for t in range(N_DEV):
            tx, ty, tc = t // 4, (t // 2) % 2, t % 2
            src = _shard_slice(ins[a], shapes[a], axes[a], 2 * tx + ty, tc)

            @pl.when(me != t)
            def _(src=src, a=a, t=t, to=(tx, ty, tc)):
                slot = (me - t + N_DEV) % N_DEV - 1
                pltpu.make_async_remote_copy(src_ref=src, dst_ref=outs[a].at[slot], send_sem=send_sems.at[a, t],
                                             recv_sem=recv_sems.at[a, slot], device_id=to, device_id_type=MESH).start()


def _exchange_finish(outs, send_sems, recv_sems):
    x, y, c = _place()
    me = 4 * x + 2 * y + c
    for a in range(len(outs)):
        for t in range(N_DEV):
            @pl.when(me != t)
            def _(a=a, t=t):
                slot = outs[a].at[0]
                pltpu.make_async_remote_copy(src_ref=slot, dst_ref=slot, send_sem=send_sems.at[a, t], recv_sem=recv_sems.at[a, 0],
                                             device_id=(x, y, c), device_id_type=MESH).wait_send()
        for k in range(N_DEV - 1):
            slot = outs[a].at[k]
            pltpu.make_async_remote_copy(src_ref=slot, dst_ref=slot, send_sem=send_sems.at[a, 0], recv_sem=recv_sems.at[a, k],
                                         device_id=(x, y, c), device_id_type=MESH).wait_recv()


def _exchange_sems(n_w):
    return [pltpu.SemaphoreType.DMA((n_w, N_DEV)), pltpu.SemaphoreType.DMA((n_w, N_DEV - 1))]


def _exchange_out_shapes(grads, axes):
    return tuple(jax.ShapeDtypeStruct((N_DEV - 1,) + p, BF16) for p in _piece_shapes(grads, axes))


def _exchange_grads(grads, axes, name):
    n_w = len(grads)
    shapes = [g.shape for g in grads]

    def body(*refs):
        ins, outs, sems = refs[:n_w], refs[n_w:2 * n_w], refs[2 * n_w:]
        _exchange_send(ins, outs, shapes, axes, *sems)
        _exchange_finish(outs, *sems)

    return _pallas(body, name=name, out_shape=_exchange_out_shapes(grads, axes), in_specs=[ANY] * n_w,
                   out_specs=tuple([ANY] * n_w), scratch_shapes=_exchange_sems(n_w))(*grads)


def _swap_halves(fulls):
    n_w = len(fulls)

    def body(*refs):
        outs = refs[n_w:2 * n_w]
        send_sems, recv_sems = refs[2 * n_w:]
        x, y, c = _place()
        sibling = (x, y, 1 - c)
        waits = []
        for a in range(n_w):
            r = fulls[a].shape[0] // 2
            mine = outs[a].at[pl.ds(c * r, r), :]
            theirs = outs[a].at[pl.ds((1 - c) * r, r), :]
            rc = pltpu.make_async_remote_copy(src_ref=mine, dst_ref=mine, send_sem=send_sems.at[a], recv_sem=recv_sems.at[a],
                                              device_id=sibling, device_id_type=MESH)
            rc.start()
            got = pltpu.make_async_remote_copy(src_ref=theirs, dst_ref=theirs, send_sem=send_sems.at[a],
                                               recv_sem=recv_sems.at[a], device_id=sibling, device_id_type=MESH)
            waits += [rc.wait_send, got.wait_recv]
        for w in waits:
            w()

    return _pallas(
        body, name="swap_halves",
        out_shape=tuple(jax.ShapeDtypeStruct(f.shape, f.dtype) for f in fulls),
        in_specs=[ANY] * n_w, out_specs=tuple([ANY] * n_w), input_output_aliases={a: a for a in range(n_w)},
        scratch_shapes=[pltpu.SemaphoreType.DMA((n_w,)), pltpu.SemaphoreType.DMA((n_w,))],
    )(*fulls)


def _load_weights(pairs, sems):
    @pl.when(pl.program_id(0) == 0)
    def _():
        cps = [pltpu.make_async_copy(s, d, sems.at[k]) for k, (s, d) in enumerate(pairs)]
        for cp in cps:
            cp.start()
        for cp in cps:
            cp.wait()


def _mod_part(c_all, w_ada_bf, b_part):
    def body(c_ref, w_ref, b_ref, o_ref):
        cv = c_ref[...]
        o_ref[...] = _dot((cv * _sig(cv)).astype(BF16), w_ref[...]) + b_ref[...]

    return _pallas(body, name="mod_part", out_shape=jax.ShapeDtypeStruct((c_all.shape[0], w_ada_bf.shape[1]), F32),
                   in_specs=[VMEM_FULL] * 3, out_specs=VMEM_FULL, compiler_params=_params())(c_all, w_ada_bf, b_part)


def _fwd_mix(x, mod, gn1, w3, w31, cbias, lng, lnb, w_in, w_a, w_b, w_o, later, later_axes, ts):
    s_len, d = x.shape
    n_t = s_len // ts
    n_l = len(later)
    later_shapes = [w.shape for w in later]

    def body(*refs):
        (x_ref, mod_ref, gn1_ref, w3_ref, w31_ref, cb_ref, lng_ref, lnb_ref, win_hbm, wa_hbm, wb_hbm, wo_hbm) = refs[:12]
        (proj_ref, h_ref, yap_ref, c3_ref, n_ref, rstd_ref, u3_ref, ya_ref, yb_ref, mg_ref, mix_ref,
         x1_ref) = refs[12 + n_l:24 + n_l]
        later_refs = refs[24 + n_l:24 + 2 * n_l]
        win_v, wa_v, wb_v, wo_v, cv_ext, u0_ext, sems, g_send, g_recv = refs[24 + 2 * n_l:]
        step = pl.program_id(0)
        pl.when(step == 0)(lambda: _gather_send(later_refs, later_shapes, later_axes, g_send, g_recv))
        pl.when(step == n_t // 2)(lambda: _gather_forward(later_refs, later_shapes, later_axes, g_send, g_recv))
        pl.when(step == n_t - 1)(lambda: _gather_finish(later_refs, later_shapes, later_axes, g_send, g_recv))
        _load_weights([(win_hbm, win_v), (wa_hbm, wa_v), (wb_hbm, wb_v), (wo_hbm, wo_v)], sems)

        @pl.when(pl.program_id(0) == 0)
        def _():
            cv_ext[0:HALO_SHORT, :] = jnp.zeros((HALO_SHORT, d), F32)
            u0_ext[0:HALO_CONF, :] = jnp.zeros((HALO_CONF, d), F32)

        xv = x_ref[...]
        sh1, sc1, g1 = mod_ref[:, 0:d], mod_ref[:, d:2 * d], mod_ref[:, 2 * d:3 * d]
        r = lax.rsqrt(jnp.mean(xv * xv, axis=-1, keepdims=True) + EPS)
        hb = (((xv * r) * gn1_ref[...]) * (1.0 + sc1) + sh1).astype(BF16)
        h_ref[...] = hb

        def pj(k):
            p = _dot(hb, win_v[:, k * d:(k + 1) * d])
            proj_ref[:, k * d:(k + 1) * d] = p.astype(BF16)
            return p

        c_s = pj(1)
        cv_ext[HALO_SHORT:HALO_SHORT + ts, :] = c_s * pj(2)
        conv3 = jnp.zeros((ts, d), F32)
        for k in range(SHORT_K):
            off = HALO_SHORT - (SHORT_K - 1) + k
            conv3 = conv3 + w3_ref[k:k + 1, :] * cv_ext[off:off + ts, :]
        cv_ext[0:HALO_SHORT, :] = cv_ext[ts:ts + HALO_SHORT, :]
        c3_ref[...] = conv3.astype(BF16)
        yap = (pj(0) * conv3).astype(BF16)
        yap_ref[...] = yap
        y_a = _dot(yap, wa_v[...])
        ya_ref[...] = y_a.astype(BF16)

        v_c = pj(3)
        u0_ext[HALO_CONF:HALO_CONF + ts, :] = v_c * _sig(pj(4))
        u1 = jnp.zeros((ts, d), F32)
        for k in range(CONF_K):
            off = HALO_CONF - (CONF_K - 1) + k
            u1 = u1 + w31_ref[k:k + 1, :] * u0_ext[off:off + ts, :]
        u0_ext[0:HALO_CONF, :] = u0_ext[ts:ts + HALO_CONF, :]
        u1 = u1 + cb_ref[...]
        mu = jnp.mean(u1, axis=-1, keepdims=True)
        var = jnp.mean(jnp.square(u1 - mu), axis=-1, keepdims=True)
        rstd = lax.rsqrt(var + LN_EPS)
        nrm = (u1 - mu) * rstd
        n_ref[...] = nrm.astype(BF16)
        rstd_ref[...] = rstd
        u2 = nrm * lng_ref[...] + lnb_ref[...]
        u3 = (u2 * _sig(u2)).astype(BF16)
        u3_ref[...] = u3
        y_b = _dot(u3, wb_v[...])
        yb_ref[...] = y_b.astype(BF16)

        merged = (_sig(pj(5)) * y_a + _sig(pj(6)) * y_b).astype(BF16)
        mg_ref[...] = merged
        mix = _dot(merged, wo_v[...])
        mix_ref[...] = mix
        x1_ref[...] = xv + g1 * mix

    row = lambda w, dt=None: pl.BlockSpec((ts, w), lambda i: (i, 0))
    full = lambda a: pl.BlockSpec(a.shape, lambda i: (0, 0))
    sd = lambda w, dt: jax.ShapeDtypeStruct((s_len, w), dt)
    outs = [sd(7 * d, BF16), sd(d, BF16), sd(d, BF16), sd(d, BF16), sd(d, BF16), sd(1, F32), sd(d, BF16), sd(d, BF16),
            sd(d, BF16), sd(d, BF16), sd(d, F32), sd(d, F32)]
    res = _pallas(
        body, name="fwd_mix", grid=(n_t,),
        out_shape=tuple(outs) + tuple(jax.ShapeDtypeStruct(w.shape, BF16) for w in later),
        in_specs=[row(d), full(mod), full(gn1), full(w3), full(w31), full(cbias), full(lng), full(lnb), ANY, ANY, ANY, ANY]
        + [ANY] * n_l,
        out_specs=tuple(row(o.shape[1]) for o in outs) + tuple([ANY] * n_l),
        input_output_aliases={12 + k: len(outs) + k for k in range(n_l)},
        scratch_shapes=[pltpu.VMEM(w_in.shape, BF16), pltpu.VMEM(w_a.shape, BF16), pltpu.VMEM(w_b.shape, BF16),
                        pltpu.VMEM(w_o.shape, BF16), pltpu.VMEM((ts + HALO_SHORT, d), F32),
                        pltpu.VMEM((ts + HALO_CONF, d), F32), pltpu.SemaphoreType.DMA((4,))] + _gather_sems(n_l),
        compiler_params=_params("arbitrary"),
    )(x, mod, gn1, w3, w31, cbias, lng, lnb, w_in, w_a, w_b, w_o, *later)
    return res[:len(outs)], res[len(outs):]


def _ffn(x1, target, mod, gn2, gf, w_fi, w_fo, ts):
    s_len, d = x1.shape
    f = w_fo.shape[0]
    n_t = s_len // ts

    def body(x1_ref, t_ref, mod_ref, gn2_ref, gf_ref, wfi_hbm, wfo_hbm,
             h2_ref, hid_ref, df_ref, dab_ref, dx1_ref, acc_ref, wfi_v, wfo_v, sems):
        _load_weights([(wfi_hbm, wfi_v), (wfo_hbm, wfo_v)], sems)

        @pl.when(pl.program_id(0) == 0)
        def _():
            acc_ref[...] = jnp.zeros(acc_ref.shape, F32)

        x1v = x1_ref[...]
        sh2, sc2, g2 = mod_ref[:, 3 * d:4 * d], mod_ref[:, 4 * d:5 * d], mod_ref[:, 5 * d:6 * d]
        gn2v, gfv = gn2_ref[...], gf_ref[...]
        r2 = lax.rsqrt(jnp.mean(x1v * x1v, axis=-1, keepdims=True) + EPS)
        xn2 = x1v * r2
        h2 = ((xn2 * gn2v) * (1.0 + sc2) + sh2).astype(BF16)
        h2_ref[...] = h2
        a = _dot(h2, wfi_v[:, 0:f])
        bg = _dot(h2, wfi_v[:, f:2 * f])
        sa = _sig(a)
        silu_a = a * sa
        hid = (silu_a * bg).astype(BF16)
        hid_ref[...] = hid
        ffn = _dot(hid, wfo_v[...])
        x2 = x1v + g2 * ffn
        r3 = lax.rsqrt(jnp.mean(x2 * x2, axis=-1, keepdims=True) + EPS)
        xn3 = x2 * r3
        diff = xn3 * gfv - t_ref[...]
        dy = diff * (1.0 / d)
        dxn3 = dy * gfv
        dx2 = r3 * (dxn3 - xn3 * jnp.mean(dxn3 * xn3, axis=-1, keepdims=True))
        acc_ref[0:1, :] += _colsum(dy * xn3)
        acc_ref[1:2, :] += _colsum(dx2 * ffn)
        acc_ref[2:3, :] += _colsum(diff * diff)

        df = (dx2 * g2).astype(BF16)
        df_ref[...] = df
        dhid = _dot_nt(df, wfo_v[...])
        da = dhid * bg * (sa * (1.0 + a * (1.0 - sa)))
        dbg = dhid * silu_a
        dab_ref[:, 0:f] = da.astype(BF16)
        dab_ref[:, f:2 * f] = dbg.astype(BF16)
        dh2 = _dot_nt(da.astype(BF16), wfi_v[:, 0:f]) + _dot_nt(dbg.astype(BF16), wfi_v[:, f:2 * f])
        acc_ref[3:4, :] += _colsum(dh2)
        acc_ref[4:5, :] += _colsum(dh2 * (xn2 * gn2v))
        acc_ref[5:6, :] += _colsum(dh2 * xn2 * (1.0 + sc2))
        dxn2 = dh2 * gn2v * (1.0 + sc2)
        dx1_ref[...] = dx2 + r2 * (dxn2 - xn2 * jnp.mean(dxn2 * xn2, axis=-1, keepdims=True))

    row = lambda w: pl.BlockSpec((ts, w), lambda i: (i, 0))
    full = lambda a: pl.BlockSpec(a.shape, lambda i: (0, 0))
    sd = lambda w, dt: jax.ShapeDtypeStruct((s_len, w), dt)
    outs = [sd(d, BF16), sd(f, BF16), sd(d, BF16), sd(2 * f, BF16), sd(d, F32), jax.ShapeDtypeStruct((8, d), F32)]
    return _pallas(
        body, name="ffn", grid=(n_t,), out_shape=tuple(outs),
        in_specs=[row(d), row(d), full(mod), full(gn2), full(gf), ANY, ANY],
        out_specs=tuple([row(o.shape[1]) for o in outs[:-1]] + [pl.BlockSpec((8, d), lambda i: (0, 0))]),
        scratch_shapes=[pltpu.VMEM(w_fi.shape, BF16), pltpu.VMEM(w_fo.shape, BF16), pltpu.SemaphoreType.DMA((2,))],
        compiler_params=_params("arbitrary"),
    )(x1, target, mod, gn2, gf, w_fi, w_fo)


def _bwd_mix(dx1, proj, nrm, rstd, c3, ya, yb, mix, mod, w3, w31, lng, lnb, w_a, w_b, w_o, ready, ready_axes, ts):
    s_len, d = dx1.shape
    n_t = s_len // ts
    n_r = len(ready)
    ready_shapes = [g.shape for g in ready]

    def body(*refs):
        (dx1_ref, proj_ref, n_ref, rstd_ref, c3_ref, ya_ref, yb_ref, mix_ref, mod_ref, w3_ref, w31_ref, lng_ref,
         lnb_ref, wa_hbm, wb_hbm, wo_hbm) = refs[:16]
        ready_refs = refs[16:16 + n_r]
        dproj_ref, dmix_ref, dya_ref, dyb_ref, acc_ref = refs[16 + n_r:21 + n_r]
        recv_refs = refs[21 + n_r:21 + 2 * n_r]
        wa_v, wb_v, wo_v, dc3_ext, du1_ext, sems, e_send, e_recv = refs[21 + 2 * n_r:]
        step = pl.program_id(0)
        pl.when(step == 0)(lambda: _exchange_send(ready_refs, recv_refs, ready_shapes, ready_axes, e_send, e_recv))
        pl.when(step == n_t - 1)(lambda: _exchange_finish(recv_refs, e_send, e_recv))
        _load_weights([(wa_hbm, wa_v), (wb_hbm, wb_v), (wo_hbm, wo_v)], sems)

        @pl.when(pl.program_id(0) == 0)
        def _():
            acc_ref[...] = jnp.zeros(acc_ref.shape, F32)
            dc3_ext[ts:ts + HALO_SHORT, :] = jnp.zeros((HALO_SHORT, d), F32)
            du1_ext[ts:ts + HALO_CONF, :] = jnp.zeros((HALO_CONF, d), F32)

        pj = lambda k: proj_ref[:, k * d:(k + 1) * d].astype(F32)
        g1 = mod_ref[:, 2 * d:3 * d]
        dx1v = dx1_ref[...]
        acc_ref[0:1, :] += _colsum(dx1v * mix_ref[...])
        dmix = (dx1v * g1).astype(BF16)
        dmix_ref[...] = dmix
        dmerged = _dot_nt(dmix, wo_v[...])
        g_a, g_b = _sig(pj(5)), _sig(pj(6))
        y_a, y_b = ya_ref[...].astype(F32), yb_ref[...].astype(F32)
        dproj_ref[:, 5 * d:6 * d] = (dmerged * y_a * g_a * (1.0 - g_a)).astype(BF16)
        dproj_ref[:, 6 * d:7 * d] = (dmerged * y_b * g_b * (1.0 - g_b)).astype(BF16)
        dya = (dmerged * g_a).astype(BF16)
        dyb = (dmerged * g_b).astype(BF16)
        dya_ref[...] = dya
        dyb_ref[...] = dyb

        dyap = _dot_nt(dya, wa_v[...])
        dproj_ref[:, 0:d] = (dyap * c3_ref[...].astype(F32)).astype(BF16)
        dc3_ext[0:ts, :] = dyap * pj(0)
        c_s, v_s = pj(1), pj(2)
        cv = c_s * v_s
        dcv = jnp.zeros((ts, d), F32)
        for j in range(SHORT_K):
            k = SHORT_K - 1 - j
            sj = dc3_ext[j:j + ts, :]
            dcv = dcv + w3_ref[k:k + 1, :] * sj
            acc_ref[1 + k:2 + k, :] += _colsum(cv * sj)
        dc3_ext[ts:ts + HALO_SHORT, :] = dc3_ext[0:HALO_SHORT, :]
        dproj_ref[:, d:2 * d] = (dcv * v_s).astype(BF16)
        dproj_ref[:, 2 * d:3 * d] = (dcv * c_s).astype(BF16)

        du3 = _dot_nt(dyb, wb_v[...])
        nv = n_ref[...].astype(F32)
        lg = lng_ref[...]
        u2 = nv * lg + lnb_ref[...]
        s2 = _sig(u2)
        du2 = du3 * (s2 * (1.0 + u2 * (1.0 - s2)))
        acc_ref[5:6, :] += _colsum(du2 * nv)
        acc_ref[6:7, :] += _colsum(du2)
        dn = du2 * lg
        du1 = rstd_ref[...] * (dn - jnp.mean(dn, axis=-1, keepdims=True) - nv * jnp.mean(dn * nv, axis=-1, keepdims=True))
        acc_ref[4:5, :] += _colsum(du1)
        du1_ext[0:ts, :] = du1
        v_c = pj(3)
        sg = _sig(pj(4))
        u0 = v_c * sg
        du0 = jnp.zeros((ts, d), F32)
        for j in range(CONF_K):
            k = CONF_K - 1 - j
            sj = du1_ext[j:j + ts, :]
            du0 = du0 + w31_ref[k:k + 1, :] * sj
            acc_ref[8 + k:9 + k, :] += _colsum(u0 * sj)
        du1_ext[ts:ts + HALO_CONF, :] = du1_ext[0:HALO_CONF, :]
        dproj_ref[:, 3 * d:4 * d] = (du0 * sg).astype(BF16)
        dproj_ref[:, 4 * d:5 * d] = (du0 * v_c * sg * (1.0 - sg)).astype(BF16)

    rev = lambda w: pl.BlockSpec((ts, w), lambda i: (n_t - 1 - i, 0))
    full = lambda a: pl.BlockSpec(a.shape, lambda i: (0, 0))
    sd = lambda w, dt: jax.ShapeDtypeStruct((s_len, w), dt)
    outs = [sd(7 * d, BF16), sd(d, BF16), sd(d, BF16), sd(d, BF16), jax.ShapeDtypeStruct((40, d), F32)]
    res = _pallas(
        body, name="bwd_mix", grid=(n_t,), out_shape=tuple(outs) + _exchange_out_shapes(ready, ready_axes),
        in_specs=[rev(d), rev(7 * d), rev(d), rev(1), rev(d), rev(d), rev(d), rev(d), full(mod), full(w3), full(w31),
                  full(lng), full(lnb), ANY, ANY, ANY] + [ANY] * n_r,
        out_specs=tuple([rev(o.shape[1]) for o in outs[:-1]] + [pl.BlockSpec((40, d), lambda i: (0, 0))] + [ANY] * n_r),
        scratch_shapes=[pltpu.VMEM(w_a.shape, BF16), pltpu.VMEM(w_b.shape, BF16), pltpu.VMEM(w_o.shape, BF16),
                        pltpu.VMEM((ts + HALO_SHORT, d), F32), pltpu.VMEM((ts + HALO_CONF, d), F32),
                        pltpu.SemaphoreType.DMA((3,))] + _exchange_sems(n_r),
        compiler_params=_params("arbitrary"),
    )(dx1, proj, nrm, rstd, c3, ya, yb, mix, mod, w3, w31, lng, lnb, w_a, w_b, w_o, *ready)
    return res[:len(outs)], res[len(outs):]


def _bwd_in(dproj, x, dx1, mod, gn1, w_in, ready, ready_axes, ts):
    s_len, d = x.shape
    n_t = s_len // ts
    n_r = len(ready)
    ready_shapes = [g.shape for g in ready]

    def body(*refs):
        dproj_ref, x_ref, dx1_ref, mod_ref, gn1_ref, win_hbm = refs[:6]
        ready_refs = refs[6:6 + n_r]
        gx_ref, acc_ref = refs[6 + n_r:8 + n_r]
        recv_refs = refs[8 + n_r:8 + 2 * n_r]
        win_v, sems, e_send, e_recv = refs[8 + 2 * n_r:]
        step = pl.program_id(0)
        pl.when(step == 0)(lambda: _exchange_send(ready_refs, recv_refs, ready_shapes, ready_axes, e_send, e_recv))
        pl.when(step == n_t - 1)(lambda: _exchange_finish(recv_refs, e_send, e_recv))
        _load_weights([(win_hbm, win_v)], sems)

        @pl.when(pl.program_id(0) == 0)
        def _():
            acc_ref[...] = jnp.zeros(acc_ref.shape, F32)

        xv = x_ref[...]
        sc1 = mod_ref[:, d:2 * d]
        gn1v = gn1_ref[...]
        r = lax.rsqrt(jnp.mean(xv * xv, axis=-1, keepdims=True) + EPS)
        xn = xv * r
        dh = _dot_nt(dproj_ref[...], win_v[...])
        acc_ref[0:1, :] += _colsum(dh)
        acc_ref[1:2, :] += _colsum(dh * (xn * gn1v))
        acc_ref[2:3, :] += _colsum(dh * xn * (1.0 + sc1))
        dxn = dh * gn1v * (1.0 + sc1)
        gx_ref[...] = dx1_ref[...] + r * (dxn - xn * jnp.mean(dxn * xn, axis=-1, keepdims=True))

    row = lambda w: pl.BlockSpec((ts, w), lambda i: (i, 0))
    full = lambda a: pl.BlockSpec(a.shape, lambda i: (0, 0))
    res = _pallas(
        body, name="bwd_in", grid=(n_t,),
        out_shape=(jax.ShapeDtypeStruct((s_len, d), F32), jax.ShapeDtypeStruct((8, d), F32))
        + _exchange_out_shapes(ready, ready_axes),
        in_specs=[row(7 * d), row(d), row(d), full(mod), full(gn1), ANY] + [ANY] * n_r,
        out_specs=(row(d), pl.BlockSpec((8, d), lambda i: (0, 0))) + tuple([ANY] * n_r),
        scratch_shapes=[pltpu.VMEM(w_in.shape, BF16), pltpu.SemaphoreType.DMA((1,))] + _exchange_sems(n_r),
        compiler_params=_params("arbitrary"),
    )(dproj, x, dx1, mod, gn1, w_in, *ready)
    return res[0], res[1], res[2:]


def _weight_grad(a, b, bm, bn, tk, name):
    s_len, m = a.shape
    n = b.shape[1]
    n_k = s_len // tk

    def body(a_ref, b_ref, o_ref, acc):
        k = pl.program_id(2)

        @pl.when(k == 0)
        def _():
            acc[...] = jnp.zeros(acc.shape, F32)

        acc[...] += _dot_tn(a_ref[...], b_ref[...])

        @pl.when(k == n_k - 1)
        def _():
            o_ref[...] = acc[...].astype(BF16)

    return _pallas(
        body, name=name, grid=(m // bm, n // bn, n_k), out_shape=jax.ShapeDtypeStruct((m, n), BF16),
        in_specs=[pl.BlockSpec((tk, bm), lambda i, j, k: (k, i)), pl.BlockSpec((tk, bn), lambda i, j, k: (k, j))],
        out_specs=pl.BlockSpec((bm, bn), lambda i, j, k: (i, j)),
        scratch_shapes=[pltpu.VMEM((bm, bn), F32)],
        compiler_params=_params("arbitrary", "arbitrary", "arbitrary"),
    )(a, b)


def _sum_slots(recv, grad, axis, chip, core, name):
    n_s, r, c = recv.shape
    tr = 16
    for cand in (256, 128, 176, 64, 32, 16):
        if r % cand == 0:
            tr = cand
            break
    n_i = r // tr

    def body(s_ref, own_ref, r_ref, o_ref):
        tot = own_ref[...].astype(F32)
        for k in range(n_s):
            tot = tot + r_ref[k].astype(F32)
        o_ref[...] = tot

    if axis == 0:
        own_map = lambda i, s: ((2 * s[0] + s[1]) * n_i + i, 0)
    else:
        own_map = lambda i, s: (s[1] * n_i + i, s[0])
    grid_spec = pltpu.PrefetchScalarGridSpec(
        num_scalar_prefetch=1, grid=(n_i,),
        in_specs=[pl.BlockSpec((tr, c), own_map), pl.BlockSpec((n_s, tr, c), lambda i, s: (0, i, 0))],
        out_specs=pl.BlockSpec((tr, c), lambda i, s: (s[1] * n_i + i, 0)))
    return _pallas(body, name=name, grid_spec=grid_spec, out_shape=jax.ShapeDtypeStruct((2 * r, c), F32),
                   compiler_params=_params("arbitrary"))(jnp.stack([chip, core]).astype(jnp.int32), grad, recv)


def _adamw_math(w, g, m, v):
    m = ADAM_B1 * m + (1.0 - ADAM_B1) * g
    v = ADAM_B2 * v + (1.0 - ADAM_B2) * (g * g)
    m_hat = m / (1.0 - ADAM_B1 ** ADAM_STEP)
    v_hat = v / (1.0 - ADAM_B2 ** ADAM_STEP)
    delta = -ADAM_LR * (m_hat / (jnp.sqrt(v_hat) + ADAM_EPS) + ADAM_WD * w)
    return delta, m, v


def _adamw(w, g, m, v, name):
    r, c = w.shape
    tr = r
    for cand in (256, 128, 176, 64, 32, 16, 8):
        if r % cand == 0:
            tr = cand
            break

    def body(w_ref, g_ref, m_ref, v_ref, go_ref, d_ref, nm_ref, nv_ref):
        g = g_ref[...]
        go_ref[...] = g
        d_ref[...], nm_ref[...], nv_ref[...] = _adamw_math(w_ref[...], g, m_ref[...], v_ref[...])

    blk = pl.BlockSpec((tr, c), lambda i: (i, 0))
    return _pallas(body, name=name, grid=(r // tr,), out_shape=tuple(jax.ShapeDtypeStruct((r, c), F32) for _ in range(4)),
                   in_specs=[blk] * 4, out_specs=(blk,) * 4, compiler_params=_params("arbitrary"))(w, g, m, v)


def _small_update(p_all, w, m, v):
    def body(p_ref, w_ref, m_ref, v_ref, g_ref, d_ref, nm_ref, nv_ref):
        g = p_ref[0]
        for s in range(1, N_DEV):
            g = g + p_ref[s]
        g_ref[...] = g
        d_ref[...], nm_ref[...], nv_ref[...] = _adamw_math(w_ref[...], g, m_ref[...], v_ref[...])

    return _pallas(body, name="small_update", out_shape=tuple(jax.ShapeDtypeStruct(w.shape, F32) for _ in range(4)),
                   in_specs=[VMEM_FULL] * 4, out_specs=(VMEM_FULL,) * 4, compiler_params=_params())(p_all, w, m, v)


def _w_ada_grad(c_t, dmod_part):
    def body(c_ref, g_ref, o_ref):
        cv = c_ref[...]
        o_ref[...] = _dot((cv * _sig(cv)).astype(BF16), g_ref[...].astype(BF16))

    return _pallas(body, name="w_ada_grad", out_shape=jax.ShapeDtypeStruct((c_t.shape[0], dmod_part.shape[1]), F32),
                   in_specs=[VMEM_FULL] * 2, out_specs=VMEM_FULL, compiler_params=_params())(c_t, dmod_part)


def _step(x, c, w_ada, b_ada, norm_mix_g, w_in, conv_short_w, w_short_out, conv_conf_w, conv_conf_b, conf_ln_g,
          conf_ln_b, w_conf_out, w_o, norm_ffn_g, w_ffn_in, w_ffn_out, final_norm_g, loss_target, moments, ts):
    xi, yi, ci = _place()
    chip = 2 * xi + yi
    me = 4 * xi + 2 * yi + ci
    x2d, tgt = x[0], loss_target[0]
    s_len, d = x2d.shape
    dq = d // N_CHIP
    mq = N_MOD * d // N_CHIP
    gf = final_norm_g.reshape(1, d)

    taps = jnp.concatenate([jnp.pad(conv_short_w[0], ((0, 8 - SHORT_K), (0, 0))),
                            jnp.pad(conv_conf_w[0], ((0, 32 - CONF_K), (0, 0)))], axis=0)
    n_tr = 40 * dq // d
    c_taps = _all_gather_small(jnp.concatenate([jnp.broadcast_to(c, (8, d)), taps.reshape(n_tr, d),
                                                jnp.zeros((16 - n_tr, d), F32)], axis=0), "gather_c_taps")
    c_taps = c_taps.reshape(N_DEV, 24, d)
    c_all = c_taps[:, 0]
    b_part = lax.dynamic_slice(b_ada, (0, chip * mq), (1, mq))
    mod_part = _mod_part(c_all, w_ada[0].astype(BF16), b_part)
    mod_all = _all_gather_small(mod_part, "gather_mod").reshape(N_CHIP, 2, N_DEV, mq)[:, 0]
    mod_full = jnp.transpose(mod_all, (1, 0, 2)).reshape(N_DEV, N_MOD * d)
    mod = lax.dynamic_slice(mod_full, (me, 0), (1, N_MOD * d))

    taps_all = c_taps[:, 8:8 + n_tr].reshape(N_CHIP, 2, 40, dq)[:, 0]
    taps_full = jnp.transpose(taps_all, (1, 0, 2)).reshape(40, d)
    w3, w31 = taps_full[0:SHORT_K], taps_full[8:8 + CONF_K]

    big = [w_in[0], w_short_out[0], w_conf_out[0], w_o[0], w_ffn_in[0], w_ffn_out[0]]
    axes = [1, 0, 0, 0, 1, 0]
    big_names = ["w_in", "w_short_out", "w_conf_out", "w_o", "w_ffn_in", "w_ffn_out"]
    placed = [_place_shard(w, ax, chip, "place_" + n) for w, ax, n in zip(big, axes, big_names)]
    win_f, wa_f, wb_f, wo_f = _gather_weights(placed[:4], axes[:4], "gather_weights")

    (proj, h, yap, c3, nrm, rstd, u3, ya, yb, merged, mix, x1), (wfi_f, wfo_f) = _fwd_mix(
        x2d, mod, norm_mix_g, w3, w31, conv_conf_b, conf_ln_g, conf_ln_b, win_f, wa_f, wb_f, wo_f, placed[4:], axes[4:], ts)
    h2, hid, df, dab, dx1, acc_f = _ffn(x1, tgt, mod, norm_ffn_g, gf, wfi_f, wfo_f, ts)
    tk = min(512, s_len)
    f = wfo_f.shape[0]
    g_ffn = [_weight_grad(h2, dab, d, 2 * f // N_CHIP, tk, "grad_w_ffn_in"), _weight_grad(hid, df, f, d, tk, "grad_w_ffn_out")]
    (dproj, dmix, dya, dyb, acc_m), r_ffn = _bwd_mix(dx1, proj, nrm, rstd, c3, ya, yb, mix, mod, w3, w31, conf_ln_g,
                                                    conf_ln_b, wa_f, wb_f, wo_f, g_ffn, axes[4:], ts)
    g_sq = [_weight_grad(yap, dya, d, d, tk, "grad_w_short_out"), _weight_grad(u3, dyb, d, d, tk, "grad_w_conf_out"),
            _weight_grad(merged, dmix, d, d, tk, "grad_w_o")]
    grad_x, acc_i, r_sq = _bwd_in(dproj, x2d, dx1, mod, norm_mix_g, win_f, g_sq, axes[1:4], ts)
    g_in = [_weight_grad(h, dproj, d, 7 * d // N_CHIP, tk, "grad_w_in")]
    r_in = _exchange_grads(g_in, axes[:1], "exchange_w_in")
    grads = g_in + g_sq + g_ffn
    recv = list(r_in) + list(r_sq) + list(r_ffn)
    g_big = _swap_halves([_sum_slots(r, g, ax, chip, ci, "sum_slots_" + n)
                          for r, g, ax, n in zip(recv, grads, axes, big_names)])

    zrow = jnp.zeros((1, d), F32)
    p_local = jnp.concatenate([
        acc_i[0:2], acc_m[0:1], acc_f[3:5], acc_f[1:2], acc_i[2:3], acc_m[1:4], acc_m[8:8 + CONF_K], acc_m[4:7],
        acc_f[5:6], acc_f[0:1], zrow, zrow], axis=0)
    p_all = _all_gather_small(p_local, "gather_small_grads").reshape(N_DEV, N_SMALL_ROWS, d)

    def pack(b_ada_like, gn1_like, cs_like, cc_like, cb_like, lg_like, lb_like, gn2_like, gf_like):
        wide = lambda t: lax.dynamic_update_slice(jnp.zeros((t.shape[1], d), F32), t[0], (0, chip * dq))
        return jnp.concatenate([b_ada_like.reshape(N_MOD, d), gn1_like, wide(cs_like), wide(cc_like), cb_like, lg_like,
                                lb_like, gn2_like, gf_like.reshape(1, d), zrow, zrow], axis=0)

    small_names = ["b_ada", "norm_mix_g", "conv_short_w", "conv_conf_w", "conv_conf_b", "conf_ln_g", "conf_ln_b",
                   "norm_ffn_g", "final_norm_g"]
    small_w = dict(b_ada=b_ada, norm_mix_g=norm_mix_g, conv_short_w=conv_short_w, conv_conf_w=conv_conf_w,
                   conv_conf_b=conv_conf_b, conf_ln_g=conf_ln_g, conf_ln_b=conf_ln_b, norm_ffn_g=norm_ffn_g,
                   final_norm_g=final_norm_g)
    w_pack = pack(*[small_w[n] for n in small_names])
    m_pack = pack(*[moments["m_" + n] for n in small_names])
    v_pack = pack(*[moments["v_" + n] for n in small_names])
    small_out = _small_update(p_all, w_pack, m_pack, v_pack)

    def unpack(p):
        cut = lambda rows, k: lax.dynamic_slice(p[rows:rows + k], (0, chip * dq), (k, dq))[None]
        return dict(b_ada=p[0:N_MOD].reshape(1, N_MOD * d), norm_mix_g=p[6:7], conv_short_w=cut(7, SHORT_K),
                    conv_conf_w=cut(10, CONF_K), conv_conf_b=p[41:42], conf_ln_g=p[42:43], conf_ln_b=p[43:44],
                    norm_ffn_g=p[44:45], final_norm_g=p[45])

    small_g, small_d, small_m, small_v = [unpack(p) for p in small_out]

    dmod_all = p_all[:, 0:N_MOD].reshape(N_DEV, N_MOD * d)
    dmod_part = jnp.pad(lax.dynamic_slice(dmod_all, (0, chip * mq), (N_DEV, mq)), ((0, 128 - N_DEV), (0, 0)))
    c_t = jnp.pad(jnp.transpose(c_all), ((0, 0), (0, 128 - N_DEV)))
    g_ada = _w_ada_grad(c_t, dmod_part)

    out_g, out_d, out_m, out_v = dict(small_g), dict(small_d), dict(small_m), dict(small_v)
    for name, w, g in zip(["w_ada"] + big_names, [w_ada[0]] + big, [g_ada] + list(g_big)):
        go, dl, nm, nv = _adamw(w, g, moments["m_" + name][0], moments["v_" + name][0], "adamw_" + name)
        out_g[name], out_d[name], out_m[name], out_v[name] = go[None], dl[None], nm[None], nv[None]

    loss = lax.psum(0.5 * jnp.sum(acc_f[2]) / d, ("x", "y", "c"))
    return loss, grad_x[None], out_g, out_d, out_m, out_v


_WEIGHTS = ["w_ada", "b_ada", "norm_mix_g", "w_in", "conv_short_w", "w_short_out", "conv_conf_w", "conv_conf_b",
            "conf_ln_g", "conf_ln_b", "w_conf_out", "w_o", "norm_ffn_g", "w_ffn_in", "w_ffn_out", "final_norm_g"]
ROW_TILE = 256


def kernel(x, c, w_ada, b_ada, norm_mix_g, w_in, conv_short_w, w_short_out, conv_conf_w, conv_conf_b, conf_ln_g, conf_ln_b, w_conf_out, w_o, norm_ffn_g, w_ffn_in, w_ffn_out, final_norm_g, loss_target, m_w_ada, m_b_ada, m_norm_mix_g, m_w_in, m_conv_short_w, m_w_short_out, m_conv_conf_w, m_conv_conf_b, m_conf_ln_g, m_conf_ln_b, m_w_conf_out, m_w_o, m_norm_ffn_g, m_w_ffn_in, m_w_ffn_out, m_final_norm_g, v_w_ada, v_b_ada, v_norm_mix_g, v_w_in, v_conv_short_w, v_w_short_out, v_conv_conf_w, v_conv_conf_b, v_conf_ln_g, v_conf_ln_b, v_w_conf_out, v_w_o, v_norm_ffn_g, v_w_ffn_in, v_w_ffn_out, v_final_norm_g):
    moments = dict(
        m_w_ada=m_w_ada, m_b_ada=m_b_ada, m_norm_mix_g=m_norm_mix_g, m_w_in=m_w_in, m_conv_short_w=m_conv_short_w,
        m_w_short_out=m_w_short_out, m_conv_conf_w=m_conv_conf_w, m_conv_conf_b=m_conv_conf_b, m_conf_ln_g=m_conf_ln_g,
        m_conf_ln_b=m_conf_ln_b, m_w_conf_out=m_w_conf_out, m_w_o=m_w_o, m_norm_ffn_g=m_norm_ffn_g, m_w_ffn_in=m_w_ffn_in,
        m_w_ffn_out=m_w_ffn_out, m_final_norm_g=m_final_norm_g,
        v_w_ada=v_w_ada, v_b_ada=v_b_ada, v_norm_mix_g=v_norm_mix_g, v_w_in=v_w_in, v_conv_short_w=v_conv_short_w,
        v_w_short_out=v_w_short_out, v_conv_conf_w=v_conv_conf_w, v_conv_conf_b=v_conv_conf_b, v_conf_ln_g=v_conf_ln_g,
        v_conf_ln_b=v_conf_ln_b, v_w_conf_out=v_w_conf_out, v_w_o=v_w_o, v_norm_ffn_g=v_norm_ffn_g, v_w_ffn_in=v_w_ffn_in,
        v_w_ffn_out=v_w_ffn_out, v_final_norm_g=v_final_norm_g)
    loss, grad_x, g, dl, nm, nv = _step(
        x, c, w_ada, b_ada, norm_mix_g, w_in, conv_short_w, w_short_out, conv_conf_w, conv_conf_b, conf_ln_g, conf_ln_b,
        w_conf_out, w_o, norm_ffn_g, w_ffn_in, w_ffn_out, final_norm_g, loss_target, moments, min(ROW_TILE, x.shape[1]))
    return (loss, grad_x, *[g[n] for n in _WEIGHTS], *[dl[n] for n in _WEIGHTS], *[nm[n] for n in _WEIGHTS],
            *[nv[n] for n in _WEIGHTS])
```

```python
import jax
import jax.numpy as jnp
from jax import lax
from jax.experimental import pallas as pl
from jax.experimental.pallas import tpu as pltpu

F32 = jnp.float32
BF16 = jnp.bfloat16
EPS = 1e-6
LN_EPS = 1e-5
SHORT_K = 3
CONF_K = 31
N_MOD = 6
N_DEV = 8
N_CHIP = 4
ADAM_LR = 0.001
ADAM_B1 = 0.9
ADAM_B2 = 0.999
ADAM_EPS = 1e-08
ADAM_WD = 0.01
ADAM_STEP = 10
MESH = pl.DeviceIdType.MESH
V7X_VMEM_LIMIT_BYTES = 60 * 1024 * 1024
HALO_SHORT = 8
HALO_CONF = 32
N_SMALL_ROWS = 48
CONV_ROWS = 64
CONV_LANES = 256

ANY = pl.BlockSpec(memory_space=pl.ANY)
VMEM_FULL = pl.BlockSpec(memory_space=pltpu.VMEM)


def _pallas(body, **kw):
    return pl.pallas_call(body, **kw)


def _params(*sem):
    return pltpu.CompilerParams(dimension_semantics=sem, vmem_limit_bytes=V7X_VMEM_LIMIT_BYTES)


def _sig(z):
    return jax.nn.sigmoid(z)


def _dot(a, b):
    return jnp.dot(a, b, preferred_element_type=F32)


def _dot_nt(a, b):
    return lax.dot_general(a, b, (((1,), (1,)), ((), ())), preferred_element_type=F32)


def _dot_tn(a, b):
    return lax.dot_general(a, b, (((0,), (0,)), ((), ())), preferred_element_type=F32)


def _colsum(z):
    return jnp.sum(z, axis=0, keepdims=True)


def _place():
    return lax.axis_index("x"), lax.axis_index("y"), lax.axis_index("c")


def _all_gather_small(block, name):
    m_per, n = block.shape

    def body(x_ref, out_ref, send_sems, recv_sems, local_sem):
        x, y, c = _place()
        me, sibling = (x, y, c), (x, y, 1 - c)
        chips = [(1 - x, y), (x, 1 - y), (1 - x, 1 - y)]

        def rows(px, py, pc):
            return out_ref.at[pl.ds((4 * px + 2 * py + pc) * m_per, m_per), :]

        def copy(k, blk, to, src=None):
            return pltpu.make_async_remote_copy(
                src_ref=rows(*blk) if src is None else src, dst_ref=rows(*blk),
                send_sem=send_sems.at[k], recv_sem=recv_sems.at[k], device_id=to, device_id_type=MESH)

        mine = pltpu.make_async_copy(x_ref, rows(*me), local_sem)
        mine.start()
        first = [copy(0, me, sibling, src=x_ref)]
        first += [copy(1 + j, me, (*chip, c), src=x_ref) for j, chip in enumerate(chips)]
        for cp in first:
            cp.start()
        passed = [copy(4 + j, (*chip, c), sibling) for j, chip in enumerate(chips)]
        for j, chip in enumerate(chips):
            copy(1 + j, (*chip, c), me).wait_recv()
            passed[j].start()
        copy(0, sibling, me).wait_recv()
        for j, chip in enumerate(chips):
            copy(4 + j, (*chip, 1 - c), me).wait_recv()
        for cp in first + passed:
            cp.wait_send()
        mine.wait()

    return _pallas(
        body, name=name,
        out_shape=jax.ShapeDtypeStruct((N_DEV * m_per, n), block.dtype),
        in_specs=[VMEM_FULL], out_specs=VMEM_FULL,
        scratch_shapes=[pltpu.SemaphoreType.DMA((7,)), pltpu.SemaphoreType.DMA((7,)), pltpu.SemaphoreType.DMA],
    )(block)


def _shard_slice(ref, full_shape, shard_axis, chip, half):
    m, n = full_shape
    if shard_axis == 0:
        r = m // N_CHIP
        return ref.at[pl.ds(chip * r + half * (r // 2), r // 2), :]
    cw = n // N_CHIP
    return ref.at[pl.ds(half * (m // 2), m // 2), pl.ds(pl.multiple_of(chip * cw, 128), cw)]


def _place_shard(w, axis, chip, name):
    r, cc = w.shape
    tr = r if axis == 0 else min(256, r)
    full = (r * N_CHIP, cc) if axis == 0 else (r, cc * N_CHIP)

    def body(s_ref, w_ref, o_ref):
        o_ref[...] = w_ref[...].astype(BF16)

    out_map = (lambda i, s: (s[0], 0)) if axis == 0 else (lambda i, s: (i, s[0]))
    grid_spec = pltpu.PrefetchScalarGridSpec(
        num_scalar_prefetch=1, grid=(r // tr,), in_specs=[pl.BlockSpec((tr, cc), lambda i, s: (i, 0))],
        out_specs=pl.BlockSpec((tr, cc), out_map))
    return _pallas(body, name=name, grid_spec=grid_spec, out_shape=jax.ShapeDtypeStruct(full, BF16),
                   compiler_params=_params("arbitrary"))(jnp.reshape(chip, (1,)).astype(jnp.int32), w)


def _other_chips():
    x, y, _ = _place()
    return [(1 - x, y), (x, 1 - y), (1 - x, 1 - y)]


def _gather_send(refs, shapes, axes, send_sems, recv_sems):
    x, y, c = _place()
    for a in range(len(refs)):
        mine = _shard_slice(refs[a], shapes[a], axes[a], 2 * x + y, c)
        for j, (px, py) in enumerate(_other_chips()):
            pltpu.make_async_remote_copy(src_ref=mine, dst_ref=mine, send_sem=send_sems.at[a, j], recv_sem=recv_sems.at[a, j],
                                         device_id=(px, py, c), device_id_type=MESH).start()


def _gather_forward(refs, shapes, axes, send_sems, recv_sems):
    x, y, c = _place()
    for a in range(len(refs)):
        for j, (px, py) in enumerate(_other_chips()):
            landed = _shard_slice(refs[a], shapes[a], axes[a], 2 * px + py, c)
            pltpu.make_async_remote_copy(src_ref=landed, dst_ref=landed, send_sem=send_sems.at[a, j], recv_sem=recv_sems.at[a, j],
                                         device_id=(px, py, c), device_id_type=MESH).wait_recv()
            pltpu.make_async_remote_copy(src_ref=landed, dst_ref=landed, send_sem=send_sems.at[a, 3 + j],
                                         recv_sem=recv_sems.at[a, 3 + j], device_id=(x, y, 1 - c), device_id_type=MESH).start()


def _gather_finish(refs, shapes, axes, send_sems, recv_sems):
    x, y, c = _place()
    for a in range(len(refs)):
        for j, (px, py) in enumerate(_other_chips()):
            other = _shard_slice(refs[a], shapes[a], axes[a], 2 * px + py, 1 - c)
            landed = _shard_slice(refs[a], shapes[a], axes[a], 2 * px + py, c)
            mine = _shard_slice(refs[a], shapes[a], axes[a], 2 * x + y, c)
            me = (x, y, c)
            pltpu.make_async_remote_copy(src_ref=other, dst_ref=other, send_sem=send_sems.at[a, 3 + j],
                                         recv_sem=recv_sems.at[a, 3 + j], device_id=me, device_id_type=MESH).wait_recv()
            pltpu.make_async_remote_copy(src_ref=mine, dst_ref=mine, send_sem=send_sems.at[a, j], recv_sem=recv_sems.at[a, j],
                                         device_id=me, device_id_type=MESH).wait_send()
            pltpu.make_async_remote_copy(src_ref=landed, dst_ref=landed, send_sem=send_sems.at[a, 3 + j],
                                         recv_sem=recv_sems.at[a, 3 + j], device_id=me, device_id_type=MESH).wait_send()


def _gather_sems(n_w):
    return [pltpu.SemaphoreType.DMA((n_w, 6)), pltpu.SemaphoreType.DMA((n_w, 6))]


def _gather_weights(fulls, axes, name):
    n_w = len(fulls)
    shapes = [f.shape for f in fulls]

    def body(*refs):
        outs, sems = refs[n_w:2 * n_w], refs[2 * n_w:]
        _gather_send(outs, shapes, axes, *sems)
        _gather_forward(outs, shapes, axes, *sems)
        _gather_finish(outs, shapes, axes, *sems)

    return _pallas(
        body, name=name, out_shape=tuple(jax.ShapeDtypeStruct(f.shape, BF16) for f in fulls),
        in_specs=[ANY] * n_w, out_specs=tuple([ANY] * n_w), input_output_aliases={a: a for a in range(n_w)},
        scratch_shapes=_gather_sems(n_w),
    )(*fulls)


def _piece_shapes(grads, axes):
    return [((g.shape[0] // N_CHIP // 2, g.shape[1]) if ax == 0 else (g.shape[0] // 2, g.shape[1] // N_CHIP))
            for g, ax in zip(grads, axes)]


def _exchange_send(ins, outs, shapes, axes, send_sems, recv_sems):
    x, y, c = _place()
    me = 4 * x + 2 * y + c
    for a in range(len(ins)):
        for t in range(N_DEV):
            tx, ty, tc = t // 4, (t // 2) % 2, t % 2
            src = _shard_slice(ins[a], shapes[a], axes[a], 2 * tx + ty, tc)

            @pl.when(me != t)
            def _(src=src, a=a, t=t, to=(tx, ty, tc)):
                slot = (me - t + N_DEV) % N_DEV - 1
                pltpu.make_async_remote_copy(src_ref=src, dst_ref=outs[a].at[slot], send_sem=send_sems.at[a, t],
                                             recv_sem=recv_sems.at[a, slot], device_id=to, device_id_type=MESH).start()


def _exchange_finish(outs, send_sems, recv_sems):
    x, y, c = _place()
    me = 4 * x + 2 * y + c
    for a in range(len(outs)):
        for t in range(N_DEV):
            @pl.when(me != t)
            def _(a=a, t=t):
                slot = outs[a].at[0]
                pltpu.make_async_remote_copy(src_ref=slot, dst_ref=slot, send_sem=send_sems.at[a, t], recv_sem=recv_sems.at[a, 0],
                                             device_id=(x, y, c), device_id_type=MESH).wait_send()
        for k in range(N_DEV - 1):
            slot = outs[a].at[k]
            pltpu.make_async_remote_copy(src_ref=slot, dst_ref=slot, send_sem=send_sems.at[a, 0], recv_sem=recv_sems.at[a, k],
                                         device_id=(x, y, c), device_id_type=MESH).wait_recv()


def _exchange_sems(n_w):
    return [pltpu.SemaphoreType.DMA((n_w, N_DEV)), pltpu.SemaphoreType.DMA((n_w, N_DEV - 1))]


def _exchange_out_shapes(grads, axes):
    return tuple(jax.ShapeDtypeStruct((N_DEV - 1,) + p, BF16) for p in _piece_shapes(grads, axes))


def _swap_halves(fulls):
    n_w = len(fulls)

    def body(*refs):
        outs = refs[n_w:2 * n_w]
        send_sems, recv_sems = refs[2 * n_w:]
        x, y, c = _place()
        sibling = (x, y, 1 - c)
        waits = []
        for a in range(n_w):
            r = fulls[a].shape[0] // 2
            mine = outs[a].at[pl.ds(c * r, r), :]
            theirs = outs[a].at[pl.ds((1 - c) * r, r), :]
            rc = pltpu.make_async_remote_copy(src_ref=mine, dst_ref=mine, send_sem=send_sems.at[a], recv_sem=recv_sems.at[a],
                                              device_id=sibling, device_id_type=MESH)
            rc.start()
            got = pltpu.make_async_remote_copy(src_ref=theirs, dst_ref=theirs, send_sem=send_sems.at[a],
                                               recv_sem=recv_sems.at[a], device_id=sibling, device_id_type=MESH)
            waits += [rc.wait_send, got.wait_recv]
        for w in waits:
            w()

    return _pallas(
        body, name="swap_halves",
        out_shape=tuple(jax.ShapeDtypeStruct(f.shape, f.dtype) for f in fulls),
        in_specs=[ANY] * n_w, out_specs=tuple([ANY] * n_w), input_output_aliases={a: a for a in range(n_w)},
        scratch_shapes=[pltpu.SemaphoreType.DMA((n_w,)), pltpu.SemaphoreType.DMA((n_w,))],
    )(*fulls)


def _load_weights(pairs, sems):
    @pl.when(pl.program_id(0) == 0)
    def _():
        cps = [pltpu.make_async_copy(s, d, sems.at[k]) for k, (s, d) in enumerate(pairs)]
        for cp in cps:
            cp.start()
        for cp in cps:
            cp.wait()


def _conv_chunks(ext, e_buf, ts, d, chunk_fn, unroll):
    rows, dh = e_buf.shape[1], e_buf.shape[2]
    for hf in range(d // dh):
        for r in range(1, 8):
            e_buf[r - 1, :, :] = ext[r:r + rows, hf * dh:(hf + 1) * dh]
        for lq in range(dh // CONV_LANES):
            lo, el = hf * dh + lq * CONV_LANES, lq * CONV_LANES

            def body(ci, carry, lo=lo, el=el):
                r0 = pl.multiple_of(ci * CONV_ROWS, CONV_ROWS)

                def tap(off):
                    q, r = divmod(off, 8)
                    if r == 0:
                        return ext[pl.ds(r0 + 8 * q, CONV_ROWS), lo:lo + CONV_LANES]
                    return e_buf[r - 1, pl.ds(r0 + 8 * q, CONV_ROWS), el:el + CONV_LANES]

                chunk_fn(tap, r0, lo)
                return carry

            lax.fori_loop(0, ts // CONV_ROWS, body, 0, unroll=unroll)


def _mod_part(c_all, w_ada_bf, b_part):
    def body(c_ref, w_ref, b_ref, o_ref):
        cv = c_ref[...]
        o_ref[...] = _dot((cv * _sig(cv)).astype(BF16), w_ref[...]) + b_ref[...]

    return _pallas(body, name="mod_part", out_shape=jax.ShapeDtypeStruct((c_all.shape[0], w_ada_bf.shape[1]), F32),
                   in_specs=[VMEM_FULL] * 3, out_specs=VMEM_FULL, compiler_params=_params())(c_all, w_ada_bf, b_part)


def _fwd_mix(x, mod, gn1, w3, w31, cbias, lng, lnb, w_in, w_a, w_b, w_o, later, later_axes, ts):
    s_len, d = x.shape
    n_t = s_len // ts
    n_l = len(later)
    later_shapes = [w.shape for w in later]

    def body(*refs):
        (x_ref, mod_ref, gn1_ref, w3_ref, w31_ref, cb_ref, lng_ref, lnb_ref, win_hbm, wa_hbm, wb_hbm, wo_hbm) = refs[:12]
        (proj_ref, h_ref, yap_ref, c3_ref, n_ref, rstd_ref, u3_ref, ya_ref, yb_ref, mg_ref, mix_ref,
         x1_ref) = refs[12 + n_l:24 + n_l]
        later_refs = refs[24 + n_l:24 + 2 * n_l]
        win_v, wa_v, wb_v, wo_v, cv_ext, u0_ext, e_buf, u1_buf, sems, g_send, g_recv = refs[24 + 2 * n_l:]
        step = pl.program_id(0)
        pl.when(step == 0)(lambda: _gather_send(later_refs, later_shapes, later_axes, g_send, g_recv))
        pl.when(step == n_t // 2)(lambda: _gather_forward(later_refs, later_shapes, later_axes, g_send, g_recv))
        pl.when(step == n_t - 1)(lambda: _gather_finish(later_refs, later_shapes, later_axes, g_send, g_recv))
        _load_weights([(win_hbm, win_v), (wa_hbm, wa_v), (wb_hbm, wb_v), (wo_hbm, wo_v)], sems)

        @pl.when(pl.program_id(0) == 0)
        def _():
            cv_ext[0:HALO_SHORT, :] = jnp.zeros((HALO_SHORT, d), F32)
            u0_ext[0:HALO_CONF, :] = jnp.zeros((HALO_CONF, d), F32)

        xv = x_ref[...]
        sh1, sc1, g1 = mod_ref[:, 0:d], mod_ref[:, d:2 * d], mod_ref[:, 2 * d:3 * d]
        r = lax.rsqrt(jnp.mean(xv * xv, axis=-1, keepdims=True) + EPS)
        hb = (((xv * r) * gn1_ref[...]) * (1.0 + sc1) + sh1).astype(BF16)
        h_ref[...] = hb

        def pj(k):
            p = _dot(hb, win_v[:, k * d:(k + 1) * d])
            proj_ref[:, k * d:(k + 1) * d] = p.astype(BF16)
            return p

        c_s = pj(1)
        cv_ext[HALO_SHORT:HALO_SHORT + ts, :] = c_s * pj(2)
        conv3 = jnp.zeros((ts, d), F32)
        for k in range(SHORT_K):
            off = HALO_SHORT - (SHORT_K - 1) + k
            conv3 = conv3 + w3_ref[k:k + 1, :] * cv_ext[off:off + ts, :]
        cv_ext[0:HALO_SHORT, :] = cv_ext[ts:ts + HALO_SHORT, :]
        c3_ref[...] = conv3.astype(BF16)
        yap = (pj(0) * conv3).astype(BF16)
        yap_ref[...] = yap
        y_a = _dot(yap, wa_v[...])
        ya_ref[...] = y_a.astype(BF16)

        v_c = pj(3)
        u0_ext[HALO_CONF:HALO_CONF + ts, :] = v_c * _sig(pj(4))
        def conv_chunk(tap, r0, lo):
            acc = jnp.zeros((CONV_ROWS, CONV_LANES), F32)
            for k in range(CONF_K):
                acc = acc + w31_ref[k:k + 1, lo:lo + CONV_LANES] * tap(HALO_CONF - (CONF_K - 1) + k)
            u1_buf[pl.ds(r0, CONV_ROWS), lo:lo + CONV_LANES] = acc

        _conv_chunks(u0_ext, e_buf, ts, d, conv_chunk, unroll=True)
        u0_ext[0:HALO_CONF, :] = u0_ext[ts:ts + HALO_CONF, :]
        u1 = u1_buf[...] + cb_ref[...]
        mu = jnp.mean(u1, axis=-1, keepdims=True)
        var = jnp.mean(jnp.square(u1 - mu), axis=-1, keepdims=True)
        rstd = lax.rsqrt(var + LN_EPS)
        nrm = (u1 - mu) * rstd
        n_ref[...] = nrm.astype(BF16)
        rstd_ref[...] = rstd
        u2 = nrm * lng_ref[...] + lnb_ref[...]
        u3 = (u2 * _sig(u2)).astype(BF16)
        u3_ref[...] = u3
        y_b = _dot(u3, wb_v[...])
        yb_ref[...] = y_b.astype(BF16)

        merged = (_sig(pj(5)) * y_a + _sig(pj(6)) * y_b).astype(BF16)
        mg_ref[...] = merged
        mix = _dot(merged, wo_v[...])
        mix_ref[...] = mix
        x1_ref[...] = xv + g1 * mix

    row = lambda w, dt=None: pl.BlockSpec((ts, w), lambda i: (i, 0))
    full = lambda a: pl.BlockSpec(a.shape, lambda i: (0, 0))
    sd = lambda w, dt: jax.ShapeDtypeStruct((s_len, w), dt)
    outs = [sd(7 * d, BF16), sd(d, BF16), sd(d, BF16), sd(d, BF16), sd(d, BF16), sd(1, F32), sd(d, BF16), sd(d, BF16),
            sd(d, BF16), sd(d, BF16), sd(d, F32), sd(d, F32)]
    res = _pallas(
        body, name="fwd_mix", grid=(n_t,),
        out_shape=tuple(outs) + tuple(jax.ShapeDtypeStruct(w.shape, BF16) for w in later),
        in_specs=[row(d), full(mod), full(gn1), full(w3), full(w31), full(cbias), full(lng), full(lnb), ANY, ANY, ANY, ANY]
        + [ANY] * n_l,
        out_specs=tuple(row(o.shape[1]) for o in outs) + tuple([ANY] * n_l),
        input_output_aliases={12 + k: len(outs) + k for k in range(n_l)},
        scratch_shapes=[pltpu.VMEM(w_in.shape, BF16), pltpu.VMEM(w_a.shape, BF16), pltpu.VMEM(w_b.shape, BF16),
                        pltpu.VMEM(w_o.shape, BF16), pltpu.VMEM((ts + HALO_SHORT, d), F32),
                        pltpu.VMEM((ts + HALO_CONF, d), F32), pltpu.VMEM((7, ts + HALO_CONF - 8, d // 2), F32),
                        pltpu.VMEM((ts, d), F32), pltpu.SemaphoreType.DMA((4,))] + _gather_sems(n_l),
        compiler_params=_params("arbitrary"),
    )(x, mod, gn1, w3, w31, cbias, lng, lnb, w_in, w_a, w_b, w_o, *later)
    return res[:len(outs)], res[len(outs):]


def _ffn(x1, target, mod, gn2, gf, w_fi, w_fo, ts):
    s_len, d = x1.shape
    f = w_fo.shape[0]
    n_t = s_len // ts

    def body(x1_ref, t_ref, mod_ref, gn2_ref, gf_ref, wfi_hbm, wfo_hbm,
             h2_ref, hid_ref, df_ref, dab_ref, dx1_ref, acc_ref, wfi_v, wfo_v, sems):
        _load_weights([(wfi_hbm, wfi_v), (wfo_hbm, wfo_v)], sems)

        @pl.when(pl.program_id(0) == 0)
        def _():
            acc_ref[...] = jnp.zeros(acc_ref.shape, F32)

        x1v = x1_ref[...]
        sh2, sc2, g2 = mod_ref[:, 3 * d:4 * d], mod_ref[:, 4 * d:5 * d], mod_ref[:, 5 * d:6 * d]
        gn2v, gfv = gn2_ref[...], gf_ref[...]
        r2 = lax.rsqrt(jnp.mean(x1v * x1v, axis=-1, keepdims=True) + EPS)
        xn2 = x1v * r2
        h2 = ((xn2 * gn2v) * (1.0 + sc2) + sh2).astype(BF16)
        h2_ref[...] = h2
        a = _dot(h2, wfi_v[:, 0:f])
        bg = _dot(h2, wfi_v[:, f:2 * f])
        sa = _sig(a)
        silu_a = a * sa
        hid = (silu_a * bg).astype(BF16)
        hid_ref[...] = hid
        ffn = _dot(hid, wfo_v[...])
        x2 = x1v + g2 * ffn
        r3 = lax.rsqrt(jnp.mean(x2 * x2, axis=-1, keepdims=True) + EPS)
        xn3 = x2 * r3
        diff = xn3 * gfv - t_ref[...]
        dy = diff * (1.0 / d)
        dxn3 = dy * gfv
        dx2 = r3 * (dxn3 - xn3 * jnp.mean(dxn3 * xn3, axis=-1, keepdims=True))
        acc_ref[0:1, :] += _colsum(dy * xn3)
        acc_ref[1:2, :] += _colsum(dx2 * ffn)
        acc_ref[2:3, :] += _colsum(diff * diff)

        df = (dx2 * g2).astype(BF16)
        df_ref[...] = df
        dhid = _dot_nt(df, wfo_v[...])
        da = dhid * bg * (sa * (1.0 + a * (1.0 - sa)))
        dbg = dhid * silu_a
        dab_ref[:, 0:f] = da.astype(BF16)
        dab_ref[:, f:2 * f] = dbg.astype(BF16)
        dh2 = _dot_nt(da.astype(BF16), wfi_v[:, 0:f]) + _dot_nt(dbg.astype(BF16), wfi_v[:, f:2 * f])
        acc_ref[3:4, :] += _colsum(dh2)
        acc_ref[4:5, :] += _colsum(dh2 * (xn2 * gn2v))
        acc_ref[5:6, :] += _colsum(dh2 * xn2 * (1.0 + sc2))
        dxn2 = dh2 * gn2v * (1.0 + sc2)
        dx1_ref[...] = dx2 + r2 * (dxn2 - xn2 * jnp.mean(dxn2 * xn2, axis=-1, keepdims=True))

    row = lambda w: pl.BlockSpec((ts, w), lambda i: (i, 0))
    full = lambda a: pl.BlockSpec(a.shape, lambda i: (0, 0))
    sd = lambda w, dt: jax.ShapeDtypeStruct((s_len, w), dt)
    outs = [sd(d, BF16), sd(f, BF16), sd(d, BF16), sd(2 * f, BF16), sd(d, F32), jax.ShapeDtypeStruct((8, d), F32)]
    return _pallas(
        body, name="ffn", grid=(n_t,), out_shape=tuple(outs),
        in_specs=[row(d), row(d), full(mod), full(gn2), full(gf), ANY, ANY],
        out_specs=tuple([row(o.shape[1]) for o in outs[:-1]] + [pl.BlockSpec((8, d), lambda i: (0, 0))]),
        scratch_shapes=[pltpu.VMEM(w_fi.shape, BF16), pltpu.VMEM(w_fo.shape, BF16), pltpu.SemaphoreType.DMA((2,))],
        compiler_params=_params("arbitrary"),
    )(x1, target, mod, gn2, gf, w_fi, w_fo)


def _bwd_mix(dx1, proj, nrm, rstd, c3, ya, yb, mix, mod, w3, w31, lng, lnb, w_a, w_b, w_o, ready, ready_axes, ts):
    s_len, d = dx1.shape
    n_t = s_len // ts
    n_r = len(ready)
    ready_shapes = [g.shape for g in ready]

    def body(*refs):
        (dx1_ref, proj_ref, n_ref, rstd_ref, c3_ref, ya_ref, yb_ref, mix_ref, mod_ref, w3_ref, w31_ref, lng_ref,
         lnb_ref, wa_hbm, wb_hbm, wo_hbm) = refs[:16]
        ready_refs = refs[16:16 + n_r]
        dproj_ref, dmix_ref, dya_ref, dyb_ref, acc_ref = refs[16 + n_r:21 + n_r]
        recv_refs = refs[21 + n_r:21 + 2 * n_r]
        wa_v, wb_v, wo_v, dc3_ext, du1_ext, e_buf, u0_buf, du0_buf, dw8, sems, e_send, e_recv = refs[21 + 2 * n_r:]
        step = pl.program_id(0)
        pl.when(step == 0)(lambda: _exchange_send(ready_refs, recv_refs, ready_shapes, ready_axes, e_send, e_recv))
        pl.when(step == n_t - 1)(lambda: _exchange_finish(recv_refs, e_send, e_recv))
        _load_weights([(wa_hbm, wa_v), (wb_hbm, wb_v), (wo_hbm, wo_v)], sems)

        @pl.when(pl.program_id(0) == 0)
        def _():
            acc_ref[...] = jnp.zeros(acc_ref.shape, F32)
            dc3_ext[ts:ts + HALO_SHORT, :] = jnp.zeros((HALO_SHORT, d), F32)
            du1_ext[ts:ts + HALO_CONF, :] = jnp.zeros((HALO_CONF, d), F32)
            dw8[...] = jnp.zeros(dw8.shape, F32)

        pj = lambda k: proj_ref[:, k * d:(k + 1) * d].astype(F32)
        g1 = mod_ref[:, 2 * d:3 * d]
        dx1v = dx1_ref[...]
        acc_ref[0:1, :] += _colsum(dx1v * mix_ref[...])
        dmix = (dx1v * g1).astype(BF16)
        dmix_ref[...] = dmix
        dmerged = _dot_nt(dmix, wo_v[...])
        g_a, g_b = _sig(pj(5)), _sig(pj(6))
        y_a, y_b = ya_ref[...].astype(F32), yb_ref[...].astype(F32)
        dproj_ref[:, 5 * d:6 * d] = (dmerged * y_a * g_a * (1.0 - g_a)).astype(BF16)
        dproj_ref[:, 6 * d:7 * d] = (dmerged * y_b * g_b * (1.0 - g_b)).astype(BF16)
        dya = (dmerged * g_a).astype(BF16)
        dyb = (dmerged * g_b).astype(BF16)
        dya_ref[...] = dya
        dyb_ref[...] = dyb

        dyap = _dot_nt(dya, wa_v[...])
        dproj_ref[:, 0:d] = (dyap * c3_ref[...].astype(F32)).astype(BF16)
        dc3_ext[0:ts, :] = dyap * pj(0)
        c_s, v_s = pj(1), pj(2)
        cv = c_s * v_s
        dcv = jnp.zeros((ts, d), F32)
        for j in range(SHORT_K):
            k = SHORT_K - 1 - j
            sj = dc3_ext[j:j + ts, :]
            dcv = dcv + w3_ref[k:k + 1, :] * sj
            acc_ref[1 + k:2 + k, :] += _colsum(cv * sj)
        dc3_ext[ts:ts + HALO_SHORT, :] = dc3_ext[0:HALO_SHORT, :]
        dproj_ref[:, d:2 * d] = (dcv * v_s).astype(BF16)
        dproj_ref[:, 2 * d:3 * d] = (dcv * c_s).astype(BF16)

        du3 = _dot_nt(dyb, wb_v[...])
        nv = n_ref[...].astype(F32)
        lg = lng_ref[...]
        u2 = nv * lg + lnb_ref[...]
        s2 = _sig(u2)
        du2 = du3 * (s2 * (1.0 + u2 * (1.0 - s2)))
        acc_ref[5:6, :] += _colsum(du2 * nv)
        acc_ref[6:7, :] += _colsum(du2)
        dn = du2 * lg
        du1 = rstd_ref[...] * (dn - jnp.mean(dn, axis=-1, keepdims=True) - nv * jnp.mean(dn * nv, axis=-1, keepdims=True))
        acc_ref[4:5, :] += _colsum(du1)
        du1_ext[0:ts, :] = du1
        v_c = pj(3)
        sg = _sig(pj(4))
        u0_buf[...] = v_c * sg

        def conv_chunk(tap, r0, lo):
            lanes = slice(lo, lo + CONV_LANES)
            u0c = u0_buf[pl.ds(r0, CONV_ROWS), lanes]
            acc = jnp.zeros((CONV_ROWS, CONV_LANES), F32)
            for j in range(CONF_K):
                k = CONF_K - 1 - j
                sj = tap(j)
                acc = acc + w31_ref[k:k + 1, lanes] * sj
                prod = u0c * sj
                part = prod[0:8]
                for i in range(1, CONV_ROWS // 8):
                    part = part + prod[8 * i:8 * i + 8]
                dw8[k, :, lanes] += part
            du0_buf[pl.ds(r0, CONV_ROWS), lanes] = acc

        _conv_chunks(du1_ext, e_buf, ts, d, conv_chunk, unroll=False)
        du0 = du0_buf[...]
        du1_ext[ts:ts + HALO_CONF, :] = du1_ext[0:HALO_CONF, :]

        @pl.when(step == n_t - 1)
        def _():
            for k in range(CONF_K):
                acc_ref[8 + k:9 + k, :] = _colsum(dw8[k])
        dproj_ref[:, 3 * d:4 * d] = (du0 * sg).astype(BF16)
        dproj_ref[:, 4 * d:5 * d] = (du0 * v_c * sg * (1.0 - sg)).astype(BF16)

    rev = lambda w: pl.BlockSpec((ts, w), lambda i: (n_t - 1 - i, 0))
    full = lambda a: pl.BlockSpec(a.shape, lambda i: (0, 0))
    sd = lambda w, dt: jax.ShapeDtypeStruct((s_len, w), dt)
    outs = [sd(7 * d, BF16), sd(d, BF16), sd(d, BF16), sd(d, BF16), jax.ShapeDtypeStruct((40, d), F32)]
    res = _pallas(
        body, name="bwd_mix", grid=(n_t,), out_shape=tuple(outs) + _exchange_out_shapes(ready, ready_axes),
        in_specs=[rev(d), rev(7 * d), rev(d), rev(1), rev(d), rev(d), rev(d), rev(d), full(mod), full(w3), full(w31),
                  full(lng), full(lnb), ANY, ANY, ANY] + [ANY] * n_r,
        out_specs=tuple([rev(o.shape[1]) for o in outs[:-1]] + [pl.BlockSpec((40, d), lambda i: (0, 0))] + [ANY] * n_r),
        scratch_shapes=[pltpu.VMEM(w_a.shape, BF16), pltpu.VMEM(w_b.shape, BF16), pltpu.VMEM(w_o.shape, BF16),
                        pltpu.VMEM((ts + HALO_SHORT, d), F32), pltpu.VMEM((ts + HALO_CONF, d), F32),
                        pltpu.VMEM((7, ts + HALO_CONF - 8, d // 2), F32), pltpu.VMEM((ts, d), F32), pltpu.VMEM((ts, d), F32),
                        pltpu.VMEM((CONF_K, 8, d), F32), pltpu.SemaphoreType.DMA((3,))] + _exchange_sems(n_r),
        compiler_params=_params("arbitrary"),
    )(dx1, proj, nrm, rstd, c3, ya, yb, mix, mod, w3, w31, lng, lnb, w_a, w_b, w_o, *ready)
    return res[:len(outs)], res[len(outs):]


def _bwd_in(dproj, x, dx1, mod, gn1, w_in, ready, ready_axes, ts):
    s_len, d = x.shape
    n_t = s_len // ts
    n_r = len(ready)
    ready_shapes = [g.shape for g in ready]

    def body(*refs):
        dproj_ref, x_ref, dx1_ref, mod_ref, gn1_ref, win_hbm = refs[:6]
        ready_refs = refs[6:6 + n_r]
        gx_ref, acc_ref = refs[6 + n_r:8 + n_r]
        recv_refs = refs[8 + n_r:8 + 2 * n_r]
        win_v, sems, e_send, e_recv = refs[8 + 2 * n_r:]
        step = pl.program_id(0)
        pl.when(step == 0)(lambda: _exchange_send(ready_refs, recv_refs, ready_shapes, ready_axes, e_send, e_recv))
        pl.when(step == n_t - 1)(lambda: _exchange_finish(recv_refs, e_send, e_recv))
        _load_weights([(win_hbm, win_v)], sems)

        @pl.when(pl.program_id(0) == 0)
        def _():
            acc_ref[...] = jnp.zeros(acc_ref.shape, F32)

        xv = x_ref[...]
        sc1 = mod_ref[:, d:2 * d]
        gn1v = gn1_ref[...]
        r = lax.rsqrt(jnp.mean(xv * xv, axis=-1, keepdims=True) + EPS)
        xn = xv * r
        dh = _dot_nt(dproj_ref[...], win_v[...])
        acc_ref[0:1, :] += _colsum(dh)
        acc_ref[1:2, :] += _colsum(dh * (xn * gn1v))
        acc_ref[2:3, :] += _colsum(dh * xn * (1.0 + sc1))
        dxn = dh * gn1v * (1.0 + sc1)
        gx_ref[...] = dx1_ref[...] + r * (dxn - xn * jnp.mean(dxn * xn, axis=-1, keepdims=True))

    row = lambda w: pl.BlockSpec((ts, w), lambda i: (i, 0))
    full = lambda a: pl.BlockSpec(a.shape, lambda i: (0, 0))
    res = _pallas(
        body, name="bwd_in", grid=(n_t,),
        out_shape=(jax.ShapeDtypeStruct((s_len, d), F32), jax.ShapeDtypeStruct((8, d), F32))
        + _exchange_out_shapes(ready, ready_axes),
        in_specs=[row(7 * d), row(d), row(d), full(mod), full(gn1), ANY] + [ANY] * n_r,
        out_specs=(row(d), pl.BlockSpec((8, d), lambda i: (0, 0))) + tuple([ANY] * n_r),
        scratch_shapes=[pltpu.VMEM(w_in.shape, BF16), pltpu.SemaphoreType.DMA((1,))] + _exchange_sems(n_r),
        compiler_params=_params("arbitrary"),
    )(dproj, x, dx1, mod, gn1, w_in, *ready)
    return res[0], res[1], res[2:]


def _weight_grad(a, b, bm, bn, tk, name):
    s_len, m = a.shape
    n = b.shape[1]
    n_k = s_len // tk

    def body(a_ref, b_ref, o_ref, acc):
        k = pl.program_id(2)

        @pl.when(k == 0)
        def _():
            acc[...] = jnp.zeros(acc.shape, F32)

        acc[...] += _dot_tn(a_ref[...], b_ref[...])

        @pl.when(k == n_k - 1)
        def _():
            o_ref[...] = acc[...].astype(BF16)

    return _pallas(
        body, name=name, grid=(m // bm, n // bn, n_k), out_shape=jax.ShapeDtypeStruct((m, n), BF16),
        in_specs=[pl.BlockSpec((tk, bm), lambda i, j, k: (k, i)), pl.BlockSpec((tk, bn), lambda i, j, k: (k, j))],
        out_specs=pl.BlockSpec((bm, bn), lambda i, j, k: (i, j)),
        scratch_shapes=[pltpu.VMEM((bm, bn), F32)],
        compiler_params=_params("arbitrary", "arbitrary", "arbitrary"),
    )(a, b)


def _sum_slots(recv, grad, axis, chip, core, name):
    n_s, r, c = recv.shape
    tr = 16
    for cand in (256, 128, 176, 64, 32, 16):
        if r % cand == 0:
            tr = cand
            break
    n_i = r // tr

    def body(s_ref, own_ref, r_ref, o_ref):
        tot = own_ref[...].astype(F32)
        for k in range(n_s):
            tot = tot + r_ref[k].astype(F32)
        o_ref[...] = tot

    if axis == 0:
        own_map = lambda i, s: ((2 * s[0] + s[1]) * n_i + i, 0)
    else:
        own_map = lambda i, s: (s[1] * n_i + i, s[0])
    grid_spec = pltpu.PrefetchScalarGridSpec(
        num_scalar_prefetch=1, grid=(n_i,),
        in_specs=[pl.BlockSpec((tr, c), own_map), pl.BlockSpec((n_s, tr, c), lambda i, s: (0, i, 0))],
        out_specs=pl.BlockSpec((tr, c), lambda i, s: (s[1] * n_i + i, 0)))
    return _pallas(body, name=name, grid_spec=grid_spec, out_shape=jax.ShapeDtypeStruct((2 * r, c), F32),
                   compiler_params=_params("arbitrary"))(jnp.stack([chip, core]).astype(jnp.int32), grad, recv)


def _adamw_math(w, g, m, v):
    m = ADAM_B1 * m + (1.0 - ADAM_B1) * g
    v = ADAM_B2 * v + (1.0 - ADAM_B2) * (g * g)
    m_hat = m / (1.0 - ADAM_B1 ** ADAM_STEP)
    v_hat = v / (1.0 - ADAM_B2 ** ADAM_STEP)
    delta = -ADAM_LR * (m_hat / (jnp.sqrt(v_hat) + ADAM_EPS) + ADAM_WD * w)
    return delta, m, v


def _adamw(w, g, m, v, name):
    r, c = w.shape
    tr = r
    for cand in (256, 128, 176, 64, 32, 16, 8):
        if r % cand == 0:
            tr = cand
            break

    def body(w_ref, g_ref, m_ref, v_ref, go_ref, d_ref, nm_ref, nv_ref):
        g = g_ref[...]
        go_ref[...] = g
        d_ref[...], nm_ref[...], nv_ref[...] = _adamw_math(w_ref[...], g, m_ref[...], v_ref[...])

    blk = pl.BlockSpec((tr, c), lambda i: (i, 0))
    return _pallas(body, name=name, grid=(r // tr,), out_shape=tuple(jax.ShapeDtypeStruct((r, c), F32) for _ in range(4)),
                   in_specs=[blk] * 4, out_specs=(blk,) * 4, compiler_params=_params("arbitrary"))(w, g, m, v)


def _small_update(p_all, w, m, v):
    def body(p_ref, w_ref, m_ref, v_ref, g_ref, d_ref, nm_ref, nv_ref):
        g = p_ref[0]
        for s in range(1, N_DEV):
            g = g + p_ref[s]
        g_ref[...] = g
        d_ref[...], nm_ref[...], nv_ref[...] = _adamw_math(w_ref[...], g, m_ref[...], v_ref[...])

    return _pallas(body, name="small_update", out_shape=tuple(jax.ShapeDtypeStruct(w.shape, F32) for _ in range(4)),
                   in_specs=[VMEM_FULL] * 4, out_specs=(VMEM_FULL,) * 4, compiler_params=_params())(p_all, w, m, v)


def _w_ada_grad(c_t, dmod_part):
    def body(c_ref, g_ref, o_ref):
        cv = c_ref[...]
        o_ref[...] = _dot((cv * _sig(cv)).astype(BF16), g_ref[...].astype(BF16))

    return _pallas(body, name="w_ada_grad", out_shape=jax.ShapeDtypeStruct((c_t.shape[0], dmod_part.shape[1]), F32),
                   in_specs=[VMEM_FULL] * 2, out_specs=VMEM_FULL, compiler_params=_params())(c_t, dmod_part)


def _step(x, c, w_ada, b_ada, norm_mix_g, w_in, conv_short_w, w_short_out, conv_conf_w, conv_conf_b, conf_ln_g,
          conf_ln_b, w_conf_out, w_o, norm_ffn_g, w_ffn_in, w_ffn_out, final_norm_g, loss_target, moments, ts):
    xi, yi, ci = _place()
    chip = 2 * xi + yi
    me = 4 * xi + 2 * yi + ci
    x2d, tgt = x[0], loss_target[0]
    s_len, d = x2d.shape
    dq = d // N_CHIP
    mq = N_MOD * d // N_CHIP
    gf = final_norm_g.reshape(1, d)

    taps = jnp.concatenate([jnp.pad(conv_short_w[0], ((0, 8 - SHORT_K), (0, 0))),
                            jnp.pad(conv_conf_w[0], ((0, 32 - CONF_K), (0, 0)))], axis=0)
    n_tr = 40 * dq // d
    c_taps = _all_gather_small(jnp.concatenate([jnp.broadcast_to(c, (8, d)), taps.reshape(n_tr, d),
                                                jnp.zeros((16 - n_tr, d), F32)], axis=0), "gather_c_taps")
    c_taps = c_taps.reshape(N_DEV, 24, d)
    c_all = c_taps[:, 0]
    b_part = lax.dynamic_slice(b_ada, (0, chip * mq), (1, mq))
    mod_part = _mod_part(c_all, w_ada[0].astype(BF16), b_part)
    mod_all = _all_gather_small(mod_part, "gather_mod").reshape(N_CHIP, 2, N_DEV, mq)[:, 0]
    mod_full = jnp.transpose(mod_all, (1, 0, 2)).reshape(N_DEV, N_MOD * d)
    mod = lax.dynamic_slice(mod_full, (me, 0), (1, N_MOD * d))

    taps_all = c_taps[:, 8:8 + n_tr].reshape(N_CHIP, 2, 40, dq)[:, 0]
    taps_full = jnp.transpose(taps_all, (1, 0, 2)).reshape(40, d)
    w3, w31 = taps_full[0:SHORT_K], taps_full[8:8 + CONF_K]

    big = [w_in[0], w_short_out[0], w_conf_out[0], w_o[0], w_ffn_in[0], w_ffn_out[0]]
    axes = [1, 0, 0, 0, 1, 0]
    big_names = ["w_in", "w_short_out", "w_conf_out", "w_o", "w_ffn_in", "w_ffn_out"]
    placed = [_place_shard(w, ax, chip, "place_" + n) for w, ax, n in zip(big, axes, big_names)]
    win_f, wa_f, wb_f, wo_f = _gather_weights(placed[:4], axes[:4], "gather_weights")

    (proj, h, yap, c3, nrm, rstd, u3, ya, yb, merged, mix, x1), (wfi_f, wfo_f) = _fwd_mix(
        x2d, mod, norm_mix_g, w3, w31, conv_conf_b, conf_ln_g, conf_ln_b, win_f, wa_f, wb_f, wo_f, placed[4:], axes[4:], ts)
    h2, hid, df, dab, dx1, acc_f = _ffn(x1, tgt, mod, norm_ffn_g, gf, wfi_f, wfo_f, ts)
    tk = min(512, s_len)
    f = wfo_f.shape[0]
    g_ffn = [_weight_grad(h2, dab, d, 2 * f // N_CHIP, tk, "grad_w_ffn_in"), _weight_grad(hid, df, f, d, tk, "grad_w_ffn_out")]
    (dproj, dmix, dya, dyb, acc_m), r_ffn = _bwd_mix(dx1, proj, nrm, rstd, c3, ya, yb, mix, mod, w3, w31, conf_ln_g,
                                                    conf_ln_b, wa_f, wb_f, wo_f, g_ffn, axes[4:], ts)
    g_sq = [_weight_grad(yap, dya, d, d, tk, "grad_w_short_out"), _weight_grad(u3, dyb, d, d, tk, "grad_w_conf_out"),
            _weight_grad(merged, dmix, d, d, tk, "grad_w_o")]
    g_in = [_weight_grad(h, dproj, d, 7 * d // N_CHIP, tk, "grad_w_in")]
    grad_x, acc_i, r_in_sq = _bwd_in(dproj, x2d, dx1, mod, norm_mix_g, win_f, g_in + g_sq, axes[:4], ts)
    grads = g_in + g_sq + g_ffn
    recv = list(r_in_sq) + list(r_ffn)
    g_big = _swap_halves([_sum_slots(r, g, ax, chip, ci, "sum_slots_" + n)
                          for r, g, ax, n in zip(recv, grads, axes, big_names)])

    zrow = jnp.zeros((1, d), F32)
    p_local = jnp.concatenate([
        acc_i[0:2], acc_m[0:1], acc_f[3:5], acc_f[1:2], acc_i[2:3], acc_m[1:4], acc_m[8:8 + CONF_K], acc_m[4:7],
        acc_f[5:6], acc_f[0:1], zrow, zrow], axis=0)
    p_all = _all_gather_small(p_local, "gather_small_grads").reshape(N_DEV, N_SMALL_ROWS, d)

    def pack(b_ada_like, gn1_like, cs_like, cc_like, cb_like, lg_like, lb_like, gn2_like, gf_like):
        wide = lambda t: lax.dynamic_update_slice(jnp.zeros((t.shape[1], d), F32), t[0], (0, chip * dq))
        return jnp.concatenate([b_ada_like.reshape(N_MOD, d), gn1_like, wide(cs_like), wide(cc_like), cb_like, lg_like,
                                lb_like, gn2_like, gf_like.reshape(1, d), zrow, zrow], axis=0)

    small_names = ["b_ada", "norm_mix_g", "conv_short_w", "conv_conf_w", "conv_conf_b", "conf_ln_g", "conf_ln_b",
                   "norm_ffn_g", "final_norm_g"]
    small_w = dict(b_ada=b_ada, norm_mix_g=norm_mix_g, conv_short_w=conv_short_w, conv_conf_w=conv_conf_w,
                   conv_conf_b=conv_conf_b, conf_ln_g=conf_ln_g, conf_ln_b=conf_ln_b, norm_ffn_g=norm_ffn_g,
                   final_norm_g=final_norm_g)
    w_pack = pack(*[small_w[n] for n in small_names])
    m_pack = pack(*[moments["m_" + n] for n in small_names])
    v_pack = pack(*[moments["v_" + n] for n in small_names])
    small_out = _small_update(p_all, w_pack, m_pack, v_pack)

    def unpack(p):
        cut = lambda rows, k: lax.dynamic_slice(p[rows:rows + k], (0, chip * dq), (k, dq))[None]
        return dict(b_ada=p[0:N_MOD].reshape(1, N_MOD * d), norm_mix_g=p[6:7], conv_short_w=cut(7, SHORT_K),
                    conv_conf_w=cut(10, CONF_K), conv_conf_b=p[41:42], conf_ln_g=p[42:43], conf_ln_b=p[43:44],
                    norm_ffn_g=p[44:45], final_norm_g=p[45])

    small_g, small_d, small_m, small_v = [unpack(p) for p in small_out]

    dmod_all = p_all[:, 0:N_MOD].reshape(N_DEV, N_MOD * d)
    dmod_part = jnp.pad(lax.dynamic_slice(dmod_all, (0, chip * mq), (N_DEV, mq)), ((0, 128 - N_DEV), (0, 0)))
    c_t = jnp.pad(jnp.transpose(c_all), ((0, 0), (0, 128 - N_DEV)))
    g_ada = _w_ada_grad(c_t, dmod_part)

    out_g, out_d, out_m, out_v = dict(small_g), dict(small_d), dict(small_m), dict(small_v)
    for name, w, g in zip(["w_ada"] + big_names, [w_ada[0]] + big, [g_ada] + list(g_big)):
        go, dl, nm, nv = _adamw(w, g, moments["m_" + name][0], moments["v_" + name][0], "adamw_" + name)
        out_g[name], out_d[name], out_m[name], out_v[name] = go[None], dl[None], nm[None], nv[None]

    loss = lax.psum(0.5 * jnp.sum(acc_f[2]) / d, ("x", "y", "c"))
    return loss, grad_x[None], out_g, out_d, out_m, out_v


_WEIGHTS = ["w_ada", "b_ada", "norm_mix_g", "w_in", "conv_short_w", "w_short_out", "conv_conf_w", "conv_conf_b",
            "conf_ln_g", "conf_ln_b", "w_conf_out", "w_o", "norm_ffn_g", "w_ffn_in", "w_ffn_out", "final_norm_g"]
ROW_TILE = 256


def kernel(x, c, w_ada, b_ada, norm_mix_g, w_in, conv_short_w, w_short_out, conv_conf_w, conv_conf_b, conf_ln_g, conf_ln_b, w_conf_out, w_o, norm_ffn_g, w_ffn_in, w_ffn_out, final_norm_g, loss_target, m_w_ada, m_b_ada, m_norm_mix_g, m_w_in, m_conv_short_w, m_w_short_out, m_conv_conf_w, m_conv_conf_b, m_conf_ln_g, m_conf_ln_b, m_w_conf_out, m_w_o, m_norm_ffn_g, m_w_ffn_in, m_w_ffn_out, m_final_norm_g, v_w_ada, v_b_ada, v_norm_mix_g, v_w_in, v_conv_short_w, v_w_short_out, v_conv_conf_w, v_conv_conf_b, v_conf_ln_g, v_conf_ln_b, v_w_conf_out, v_w_o, v_norm_ffn_g, v_w_ffn_in, v_w_ffn_out, v_final_norm_g):
    moments = dict(
        m_w_ada=m_w_ada, m_b_ada=m_b_ada, m_norm_mix_g=m_norm_mix_g, m_w_in=m_w_in, m_conv_short_w=m_conv_short_w,
        m_w_short_out=m_w_short_out, m_conv_conf_w=m_conv_conf_w, m_conv_conf_b=m_conv_conf_b, m_conf_ln_g=m_conf_ln_g,
        m_conf_ln_b=m_conf_ln_b, m_w_conf_out=m_w_conf_out, m_w_o=m_w_o, m_norm_ffn_g=m_norm_ffn_g, m_w_ffn_in=m_w_ffn_in,
        m_w_ffn_out=m_w_ffn_out, m_final_norm_g=m_final_norm_g,
        v_w_ada=v_w_ada, v_b_ada=v_b_ada, v_norm_mix_g=v_norm_mix_g, v_w_in=v_w_in, v_conv_short_w=v_conv_short_w,
        v_w_short_out=v_w_short_out, v_conv_conf_w=v_conv_conf_w, v_conv_conf_b=v_conv_conf_b, v_conf_ln_g=v_conf_ln_g,
        v_conf_ln_b=v_conf_ln_b, v_w_conf_out=v_w_conf_out, v_w_o=v_w_o, v_norm_ffn_g=v_norm_ffn_g, v_w_ffn_in=v_w_ffn_in,
        v_w_ffn_out=v_w_ffn_out, v_final_norm_g=v_final_norm_g)
    loss, grad_x, g, dl, nm, nv = _step(
        x, c, w_ada, b_ada, norm_mix_g, w_in, conv_short_w, w_short_out, conv_conf_w, conv_conf_b, conf_ln_g, conf_ln_b,
        w_conf_out, w_o, norm_ffn_g, w_ffn_in, w_ffn_out, final_norm_g, loss_target, moments, min(ROW_TILE, x.shape[1]))
    return (loss, grad_x, *[g[n] for n in _WEIGHTS], *[dl[n] for n in _WEIGHTS], *[nm[n] for n in _WEIGHTS],
            *[nv[n] for n in _WEIGHTS])
```

```python
import jax
import jax.numpy as jnp
from jax import lax
from jax.experimental import pallas as pl
from jax.experimental.pallas import tpu as pltpu

F32 = jnp.float32
BF16 = jnp.bfloat16
EPS = 1e-6
LN_EPS = 1e-5
SHORT_K = 3
CONF_K = 31
N_MOD = 6
N_DEV = 8
N_CHIP = 4
ADAM_LR = 0.001
ADAM_B1 = 0.9
ADAM_B2 = 0.999
ADAM_EPS = 1e-08
ADAM_WD = 0.01
ADAM_STEP = 10
MESH = pl.DeviceIdType.MESH
V7X_VMEM_LIMIT_BYTES = 60 * 1024 * 1024
HALO_SHORT = 8
HALO_CONF = 32
N_SMALL_ROWS = 48
ROW_DMOD, ROW_GN1, ROW_W3, ROW_W31, ROW_CB, ROW_LNG, ROW_LNB, ROW_GN2, ROW_GF, ROW_SQERR = 0, 6, 7, 10, 41, 42, 43, 44, 45, 46
CONV_ROWS = 64
CONV_LANES = 256

ANY = pl.BlockSpec(memory_space=pl.ANY)
VMEM_FULL = pl.BlockSpec(memory_space=pltpu.VMEM)


def _pallas(body, **kw):
    return pl.pallas_call(body, **kw)


def _params(*sem):
    return pltpu.CompilerParams(dimension_semantics=sem, vmem_limit_bytes=V7X_VMEM_LIMIT_BYTES)


def _sig(z):
    return jax.nn.sigmoid(z)


def _dot(a, b):
    return jnp.dot(a, b, preferred_element_type=F32)


def _dot_nt(a, b):
    return lax.dot_general(a, b, (((1,), (1,)), ((), ())), preferred_element_type=F32)


def _dot_tn(a, b):
    return lax.dot_general(a, b, (((0,), (0,)), ((), ())), preferred_element_type=F32)


def _colsum(z):
    return jnp.sum(z, axis=0, keepdims=True)


def _place():
    return lax.axis_index("x"), lax.axis_index("y"), lax.axis_index("c")


def _all_gather_small(block, name):
    m_per, n = block.shape

    def body(x_ref, out_ref, send_sems, recv_sems, local_sem):
        x, y, c = _place()
        me, sibling = (x, y, c), (x, y, 1 - c)
        chips = [(1 - x, y), (x, 1 - y), (1 - x, 1 - y)]

        def rows(px, py, pc):
            return out_ref.at[pl.ds((4 * px + 2 * py + pc) * m_per, m_per), :]

        def copy(k, blk, to, src=None):
            return pltpu.make_async_remote_copy(
                src_ref=rows(*blk) if src is None else src, dst_ref=rows(*blk),
                send_sem=send_sems.at[k], recv_sem=recv_sems.at[k], device_id=to, device_id_type=MESH)

        mine = pltpu.make_async_copy(x_ref, rows(*me), local_sem)
        mine.start()
        first = [copy(0, me, sibling, src=x_ref)]
        first += [copy(1 + j, me, (*chip, c), src=x_ref) for j, chip in enumerate(chips)]
        for cp in first:
            cp.start()
        passed = [copy(4 + j, (*chip, c), sibling) for j, chip in enumerate(chips)]
        for j, chip in enumerate(chips):
            copy(1 + j, (*chip, c), me).wait_recv()
            passed[j].start()
        copy(0, sibling, me).wait_recv()
        for j, chip in enumerate(chips):
            copy(4 + j, (*chip, 1 - c), me).wait_recv()
        for cp in first + passed:
            cp.wait_send()
        mine.wait()

    return _pallas(
        body, name=name,
        out_shape=jax.ShapeDtypeStruct((N_DEV * m_per, n), block.dtype),
        in_specs=[VMEM_FULL], out_specs=VMEM_FULL,
        scratch_shapes=[pltpu.SemaphoreType.DMA((7,)), pltpu.SemaphoreType.DMA((7,)), pltpu.SemaphoreType.DMA],
    )(block)


def _shard_slice(ref, full_shape, shard_axis, chip, half):
    m, n = full_shape
    if shard_axis == 0:
        r = m // N_CHIP
        return ref.at[pl.ds(chip * r + half * (r // 2), r // 2), :]
    cw = n // N_CHIP
    return ref.at[pl.ds(half * (m // 2), m // 2), pl.ds(pl.multiple_of(chip * cw, 128), cw)]


def _place_shard(w, axis, chip, name):
    r, cc = w.shape
    tr = r if axis == 0 else min(256, r)
    full = (r * N_CHIP, cc) if axis == 0 else (r, cc * N_CHIP)

    def body(s_ref, w_ref, o_ref):
        o_ref[...] = w_ref[...].astype(BF16)

    out_map = (lambda i, s: (s[0], 0)) if axis == 0 else (lambda i, s: (i, s[0]))
    grid_spec = pltpu.PrefetchScalarGridSpec(
        num_scalar_prefetch=1, grid=(r // tr,), in_specs=[pl.BlockSpec((tr, cc), lambda i, s: (i, 0))],
        out_specs=pl.BlockSpec((tr, cc), out_map))
    return _pallas(body, name=name, grid_spec=grid_spec, out_shape=jax.ShapeDtypeStruct(full, BF16),
                   compiler_params=_params("arbitrary"))(jnp.reshape(chip, (1,)).astype(jnp.int32), w)


def _other_chips():
    x, y, _ = _place()
    return [(1 - x, y), (x, 1 - y), (1 - x, 1 - y)]


def _gather_send(refs, shapes, axes, send_sems, recv_sems):
    x, y, c = _place()
    for a in range(len(refs)):
        mine = _shard_slice(refs[a], shapes[a], axes[a], 2 * x + y, c)
        for j, (px, py) in enumerate(_other_chips()):
            pltpu.make_async_remote_copy(src_ref=mine, dst_ref=mine, send_sem=send_sems.at[a, j], recv_sem=recv_sems.at[a, j],
                                         device_id=(px, py, c), device_id_type=MESH).start()


def _gather_forward(refs, shapes, axes, send_sems, recv_sems):
    x, y, c = _place()
    for a in range(len(refs)):
        for j, (px, py) in enumerate(_other_chips()):
            landed = _shard_slice(refs[a], shapes[a], axes[a], 2 * px + py, c)
            pltpu.make_async_remote_copy(src_ref=landed, dst_ref=landed, send_sem=send_sems.at[a, j], recv_sem=recv_sems.at[a, j],
                                         device_id=(px, py, c), device_id_type=MESH).wait_recv()
            pltpu.make_async_remote_copy(src_ref=landed, dst_ref=landed, send_sem=send_sems.at[a, 3 + j],
                                         recv_sem=recv_sems.at[a, 3 + j], device_id=(x, y, 1 - c), device_id_type=MESH).start()


def _gather_finish(refs, shapes, axes, send_sems, recv_sems):
    x, y, c = _place()
    for a in range(len(refs)):
        for j, (px, py) in enumerate(_other_chips()):
            other = _shard_slice(refs[a], shapes[a], axes[a], 2 * px + py, 1 - c)
            landed = _shard_slice(refs[a], shapes[a], axes[a], 2 * px + py, c)
            mine = _shard_slice(refs[a], shapes[a], axes[a], 2 * x + y, c)
            me = (x, y, c)
            pltpu.make_async_remote_copy(src_ref=other, dst_ref=other, send_sem=send_sems.at[a, 3 + j],
                                         recv_sem=recv_sems.at[a, 3 + j], device_id=me, device_id_type=MESH).wait_recv()
            pltpu.make_async_remote_copy(src_ref=mine, dst_ref=mine, send_sem=send_sems.at[a, j], recv_sem=recv_sems.at[a, j],
                                         device_id=me, device_id_type=MESH).wait_send()
            pltpu.make_async_remote_copy(src_ref=landed, dst_ref=landed, send_sem=send_sems.at[a, 3 + j],
                                         recv_sem=recv_sems.at[a, 3 + j], device_id=me, device_id_type=MESH).wait_send()


def _gather_sems(n_w):
    return [pltpu.SemaphoreType.DMA((n_w, 6)), pltpu.SemaphoreType.DMA((n_w, 6))]


def _gather_weights(fulls, axes, name):
    n_w = len(fulls)
    shapes = [f.shape for f in fulls]

    def body(*refs):
        outs, sems = refs[n_w:2 * n_w], refs[2 * n_w:]
        _gather_send(outs, shapes, axes, *sems)
        _gather_forward(outs, shapes, axes, *sems)
        _gather_finish(outs, shapes, axes, *sems)

    return _pallas(
        body, name=name, out_shape=tuple(jax.ShapeDtypeStruct(f.shape, BF16) for f in fulls),
        in_specs=[ANY] * n_w, out_specs=tuple([ANY] * n_w), input_output_aliases={a: a for a in range(n_w)},
        scratch_shapes=_gather_sems(n_w),
    )(*fulls)


def _piece_shapes(grads, axes):
    return [((g.shape[0] // N_CHIP // 2, g.shape[1]) if ax == 0 else (g.shape[0] // 2, g.shape[1] // N_CHIP))
            for g, ax in zip(grads, axes)]


def _exchange_send(ins, outs, shapes, axes, send_sems, recv_sems):
    x, y, c = _place()
    me = 4 * x + 2 * y + c
    for a in range(len(ins)):
        for t in range(N_DEV):
            tx, ty, tc = t // 4, (t // 2) % 2, t % 2
            src = _shard_slice(ins[a], shapes[a], axes[a], 2 * tx + ty, tc)

            @pl.when(me != t)
            def _(src=src, a=a, t=t, to=(tx, ty, tc)):
                slot = (me - t + N_DEV) % N_DEV - 1
                pltpu.make_async_remote_copy(src_ref=src, dst_ref=outs[a].at[slot], send_sem=send_sems.at[a, t],
                                             recv_sem=recv_sems.at[a, slot], device_id=to, device_id_type=MESH).start()


def _exchange_finish(outs, send_sems, recv_sems):
    x, y, c = _place()
    me = 4 * x + 2 * y + c
    for a in range(len(outs)):
        for t in range(N_DEV):
            @pl.when(me != t)
            def _(a=a, t=t):
                slot = outs[a].at[0]
                pltpu.make_async_remote_copy(src_ref=slot, dst_ref=slot, send_sem=send_sems.at[a, t], recv_sem=recv_sems.at[a, 0],
                                             device_id=(x, y, c), device_id_type=MESH).wait_send()
        for k in range(N_DEV - 1):
            slot = outs[a].at[k]
            pltpu.make_async_remote_copy(src_ref=slot, dst_ref=slot, send_sem=send_sems.at[a, 0], recv_sem=recv_sems.at[a, k],
                                         device_id=(x, y, c), device_id_type=MESH).wait_recv()


def _exchange_sems(n_w):
    return [pltpu.SemaphoreType.DMA((n_w, N_DEV)), pltpu.SemaphoreType.DMA((n_w, N_DEV - 1))]


def _exchange_out_shapes(grads, axes):
    return tuple(jax.ShapeDtypeStruct((N_DEV - 1,) + p, BF16) for p in _piece_shapes(grads, axes))


def _swap_halves(fulls):
    n_w = len(fulls)

    def body(*refs):
        outs = refs[n_w:2 * n_w]
        send_sems, recv_sems = refs[2 * n_w:]
        x, y, c = _place()
        sibling = (x, y, 1 - c)
        waits = []
        for a in range(n_w):
            r = fulls[a].shape[0] // 2
            mine = outs[a].at[pl.ds(c * r, r), :]
            theirs = outs[a].at[pl.ds((1 - c) * r, r), :]
            rc = pltpu.make_async_remote_copy(src_ref=mine, dst_ref=mine, send_sem=send_sems.at[a], recv_sem=recv_sems.at[a],
                                              device_id=sibling, device_id_type=MESH)
            rc.start()
            got = pltpu.make_async_remote_copy(src_ref=theirs, dst_ref=theirs, send_sem=send_sems.at[a],
                                               recv_sem=recv_sems.at[a], device_id=sibling, device_id_type=MESH)
            waits += [rc.wait_send, got.wait_recv]
        for w in waits:
            w()

    return _pallas(
        body, name="swap_halves",
        out_shape=tuple(jax.ShapeDtypeStruct(f.shape, f.dtype) for f in fulls),
        in_specs=[ANY] * n_w, out_specs=tuple([ANY] * n_w), input_output_aliases={a: a for a in range(n_w)},
        scratch_shapes=[pltpu.SemaphoreType.DMA((n_w,)), pltpu.SemaphoreType.DMA((n_w,))],
    )(*fulls)


def _load_weights(pairs, sems):
    @pl.when(pl.program_id(0) == 0)
    def _():
        cps = [pltpu.make_async_copy(s, d, sems.at[k]) for k, (s, d) in enumerate(pairs)]
        for cp in cps:
            cp.start()
        for cp in cps:
            cp.wait()


def _conv_chunks(ext, e_buf, ts, d, chunk_fn, unroll):
    rows, dh = e_buf.shape[1], e_buf.shape[2]
    for hf in range(d // dh):
        for r in range(1, 8):
            e_buf[r - 1, :, :] = ext[r:r + rows, hf * dh:(hf + 1) * dh]
        for lq in range(dh // CONV_LANES):
            lo, el = hf * dh + lq * CONV_LANES, lq * CONV_LANES

            def body(ci, carry, lo=lo, el=el):
                r0 = pl.multiple_of(ci * CONV_ROWS, CONV_ROWS)

                def tap(off):
                    q, r = divmod(off, 8)
                    if r == 0:
                        return ext[pl.ds(r0 + 8 * q, CONV_ROWS), lo:lo + CONV_LANES]
                    return e_buf[r - 1, pl.ds(r0 + 8 * q, CONV_ROWS), el:el + CONV_LANES]

                chunk_fn(tap, r0, lo)
                return carry

            lax.fori_loop(0, ts // CONV_ROWS, body, 0, unroll=unroll)


def _mod_part(c_all, w_ada_bf, b_part):
    def body(c_ref, w_ref, b_ref, o_ref):
        cv = c_ref[...]
        o_ref[...] = _dot((cv * _sig(cv)).astype(BF16), w_ref[...]) + b_ref[...]

    return _pallas(body, name="mod_part", out_shape=jax.ShapeDtypeStruct((c_all.shape[0], w_ada_bf.shape[1]), F32),
                   in_specs=[VMEM_FULL] * 3, out_specs=VMEM_FULL, compiler_params=_params())(c_all, w_ada_bf, b_part)


def _fwd_mix(x, mod, gn1, w3, w31, cbias, lng, lnb, w_in, w_a, w_b, w_o, later, later_axes, ts):
    s_len, d = x.shape
    n_t = s_len // ts
    n_l = len(later)
    later_shapes = [w.shape for w in later]

    def body(*refs):
        (x_ref, mod_ref, gn1_ref, w3_ref, w31_ref, cb_ref, lng_ref, lnb_ref, win_hbm, wa_hbm, wb_hbm, wo_hbm) = refs[:12]
        (proj_ref, h_ref, yap_ref, c3_ref, n_ref, rstd_ref, u3_ref, ya_ref, yb_ref, mg_ref, mix_ref,
         x1_ref) = refs[12 + n_l:24 + n_l]
        later_refs = refs[24 + n_l:24 + 2 * n_l]
        win_v, wa_v, wb_v, wo_v, cv_ext, u0_ext, e_buf, u1_buf, sems, g_send, g_recv = refs[24 + 2 * n_l:]
        step = pl.program_id(0)
        pl.when(step == 0)(lambda: _gather_send(later_refs, later_shapes, later_axes, g_send, g_recv))
        pl.when(step == n_t // 2)(lambda: _gather_forward(later_refs, later_shapes, later_axes, g_send, g_recv))
        pl.when(step == n_t - 1)(lambda: _gather_finish(later_refs, later_shapes, later_axes, g_send, g_recv))
        _load_weights([(win_hbm, win_v), (wa_hbm, wa_v), (wb_hbm, wb_v), (wo_hbm, wo_v)], sems)

        @pl.when(pl.program_id(0) == 0)
        def _():
            cv_ext[0:HALO_SHORT, :] = jnp.zeros((HALO_SHORT, d), F32)
            u0_ext[0:HALO_CONF, :] = jnp.zeros((HALO_CONF, d), F32)

        xv = x_ref[...]
        sh1, sc1, g1 = mod_ref[:, 0:d], mod_ref[:, d:2 * d], mod_ref[:, 2 * d:3 * d]
        r = lax.rsqrt(jnp.mean(xv * xv, axis=-1, keepdims=True) + EPS)
        hb = (((xv * r) * gn1_ref[...]) * (1.0 + sc1) + sh1).astype(BF16)
        h_ref[...] = hb

        def pj(k):
            p = _dot(hb, win_v[:, k * d:(k + 1) * d])
            proj_ref[:, k * d:(k + 1) * d] = p.astype(BF16)
            return p

        c_s = pj(1)
        cv_ext[HALO_SHORT:HALO_SHORT + ts, :] = c_s * pj(2)
        conv3 = jnp.zeros((ts, d), F32)
        for k in range(SHORT_K):
            off = HALO_SHORT - (SHORT_K - 1) + k
            conv3 = conv3 + w3_ref[k:k + 1, :] * cv_ext[off:off + ts, :]
        cv_ext[0:HALO_SHORT, :] = cv_ext[ts:ts + HALO_SHORT, :]
        c3_ref[...] = conv3.astype(BF16)
        yap = (pj(0) * conv3).astype(BF16)
        yap_ref[...] = yap
        y_a = _dot(yap, wa_v[...])
        ya_ref[...] = y_a.astype(BF16)

        v_c = pj(3)
        u0_ext[HALO_CONF:HALO_CONF + ts, :] = v_c * _sig(pj(4))
        def conv_chunk(tap, r0, lo):
            acc = jnp.zeros((CONV_ROWS, CONV_LANES), F32)
            for k in range(CONF_K):
                acc = acc + w31_ref[k:k + 1, lo:lo + CONV_LANES] * tap(HALO_CONF - (CONF_K - 1) + k)
            u1_buf[pl.ds(r0, CONV_ROWS), lo:lo + CONV_LANES] = acc

        _conv_chunks(u0_ext, e_buf, ts, d, conv_chunk, unroll=True)
        u0_ext[0:HALO_CONF, :] = u0_ext[ts:ts + HALO_CONF, :]
        u1 = u1_buf[...] + cb_ref[...]
        mu = jnp.mean(u1, axis=-1, keepdims=True)
        var = jnp.mean(jnp.square(u1 - mu), axis=-1, keepdims=True)
        rstd = lax.rsqrt(var + LN_EPS)
        nrm = (u1 - mu) * rstd
        n_ref[...] = nrm.astype(BF16)
        rstd_ref[...] = rstd
        u2 = nrm * lng_ref[...] + lnb_ref[...]
        u3 = (u2 * _sig(u2)).astype(BF16)
        u3_ref[...] = u3
        y_b = _dot(u3, wb_v[...])
        yb_ref[...] = y_b.astype(BF16)

        merged = (_sig(pj(5)) * y_a + _sig(pj(6)) * y_b).astype(BF16)
        mg_ref[...] = merged
        mix = _dot(merged, wo_v[...])
        mix_ref[...] = mix
        x1_ref[...] = xv + g1 * mix

    row = lambda w, dt=None: pl.BlockSpec((ts, w), lambda i: (i, 0))
    full = lambda a: pl.BlockSpec(a.shape, lambda i: (0, 0))
    sd = lambda w, dt: jax.ShapeDtypeStruct((s_len, w), dt)
    outs = [sd(7 * d, BF16), sd(d, BF16), sd(d, BF16), sd(d, BF16), sd(d, BF16), sd(1, F32), sd(d, BF16), sd(d, BF16),
            sd(d, BF16), sd(d, BF16), sd(d, F32), sd(d, F32)]
    res = _pallas(
        body, name="fwd_mix", grid=(n_t,),
        out_shape=tuple(outs) + tuple(jax.ShapeDtypeStruct(w.shape, BF16) for w in later),
        in_specs=[row(d), full(mod), full(gn1), full(w3), full(w31), full(cbias), full(lng), full(lnb), ANY, ANY, ANY, ANY]
        + [ANY] * n_l,
        out_specs=tuple(row(o.shape[1]) for o in outs) + tuple([ANY] * n_l),
        input_output_aliases={12 + k: len(outs) + k for k in range(n_l)},
        scratch_shapes=[pltpu.VMEM(w_in.shape, BF16), pltpu.VMEM(w_a.shape, BF16), pltpu.VMEM(w_b.shape, BF16),
                        pltpu.VMEM(w_o.shape, BF16), pltpu.VMEM((ts + HALO_SHORT, d), F32),
                        pltpu.VMEM((ts + HALO_CONF, d), F32), pltpu.VMEM((7, ts + HALO_CONF - 8, d // 2), F32),
                        pltpu.VMEM((ts, d), F32), pltpu.SemaphoreType.DMA((4,))] + _gather_sems(n_l),
        compiler_params=_params("arbitrary"),
    )(x, mod, gn1, w3, w31, cbias, lng, lnb, w_in, w_a, w_b, w_o, *later)
    return res[:len(outs)], res[len(outs):]


def _ffn(x1, target, mod, gn2, gf, w_fi, w_fo, ts):
    s_len, d = x1.shape
    f = w_fo.shape[0]
    n_t = s_len // ts

    def body(x1_ref, t_ref, mod_ref, gn2_ref, gf_ref, wfi_hbm, wfo_hbm,
             h2_ref, hid_ref, df_ref, dab_ref, dx1_ref, acc_ref, wfi_v, wfo_v, sems):
        _load_weights([(wfi_hbm, wfi_v), (wfo_hbm, wfo_v)], sems)

        @pl.when(pl.program_id(0) == 0)
        def _():
            acc_ref[...] = jnp.zeros(acc_ref.shape, F32)

        x1v = x1_ref[...]
        sh2, sc2, g2 = mod_ref[:, 3 * d:4 * d], mod_ref[:, 4 * d:5 * d], mod_ref[:, 5 * d:6 * d]
        gn2v, gfv = gn2_ref[...], gf_ref[...]
        r2 = lax.rsqrt(jnp.mean(x1v * x1v, axis=-1, keepdims=True) + EPS)
        xn2 = x1v * r2
        h2 = ((xn2 * gn2v) * (1.0 + sc2) + sh2).astype(BF16)
        h2_ref[...] = h2
        a = _dot(h2, wfi_v[:, 0:f])
        bg = _dot(h2, wfi_v[:, f:2 * f])
        sa = _sig(a)
        silu_a = a * sa
        hid = (silu_a * bg).astype(BF16)
        hid_ref[...] = hid
        ffn = _dot(hid, wfo_v[...])
        x2 = x1v + g2 * ffn
        r3 = lax.rsqrt(jnp.mean(x2 * x2, axis=-1, keepdims=True) + EPS)
        xn3 = x2 * r3
        diff = xn3 * gfv - t_ref[...]
        dy = diff * (1.0 / d)
        dxn3 = dy * gfv
        dx2 = r3 * (dxn3 - xn3 * jnp.mean(dxn3 * xn3, axis=-1, keepdims=True))
        acc_ref[ROW_GF:ROW_GF + 1, :] += _colsum(dy * xn3)
        acc_ref[ROW_DMOD + 5:ROW_DMOD + 5 + 1, :] += _colsum(dx2 * ffn)
        acc_ref[ROW_SQERR:ROW_SQERR + 1, :] += _colsum(diff * diff)

        df = (dx2 * g2).astype(BF16)
        df_ref[...] = df
        dhid = _dot_nt(df, wfo_v[...])
        da = dhid * bg * (sa * (1.0 + a * (1.0 - sa)))
        dbg = dhid * silu_a
        dab_ref[:, 0:f] = da.astype(BF16)
        dab_ref[:, f:2 * f] = dbg.astype(BF16)
        dh2 = _dot_nt(da.astype(BF16), wfi_v[:, 0:f]) + _dot_nt(dbg.astype(BF16), wfi_v[:, f:2 * f])
        acc_ref[ROW_DMOD + 3:ROW_DMOD + 3 + 1, :] += _colsum(dh2)
        acc_ref[ROW_DMOD + 4:ROW_DMOD + 4 + 1, :] += _colsum(dh2 * (xn2 * gn2v))
        acc_ref[ROW_GN2:ROW_GN2 + 1, :] += _colsum(dh2 * xn2 * (1.0 + sc2))
        dxn2 = dh2 * gn2v * (1.0 + sc2)
        dx1_ref[...] = dx2 + r2 * (dxn2 - xn2 * jnp.mean(dxn2 * xn2, axis=-1, keepdims=True))

    row = lambda w: pl.BlockSpec((ts, w), lambda i: (i, 0))
    full = lambda a: pl.BlockSpec(a.shape, lambda i: (0, 0))
    sd = lambda w, dt: jax.ShapeDtypeStruct((s_len, w), dt)
    outs = [sd(d, BF16), sd(f, BF16), sd(d, BF16), sd(2 * f, BF16), sd(d, F32), jax.ShapeDtypeStruct((N_SMALL_ROWS, d), F32)]
    return _pallas(
        body, name="ffn", grid=(n_t,), out_shape=tuple(outs),
        in_specs=[row(d), row(d), full(mod), full(gn2), full(gf), ANY, ANY],
        out_specs=tuple([row(o.shape[1]) for o in outs[:-1]] + [pl.BlockSpec((N_SMALL_ROWS, d), lambda i: (0, 0))]),
        scratch_shapes=[pltpu.VMEM(w_fi.shape, BF16), pltpu.VMEM(w_fo.shape, BF16), pltpu.SemaphoreType.DMA((2,))],
        compiler_params=_params("arbitrary"),
    )(x1, target, mod, gn2, gf, w_fi, w_fo)


def _bwd_mix(dx1, proj, nrm, rstd, c3, ya, yb, mix, mod, w3, w31, lng, lnb, w_a, w_b, w_o, ready, ready_axes, ts):
    s_len, d = dx1.shape
    n_t = s_len // ts
    n_r = len(ready)
    ready_shapes = [g.shape for g in ready]

    def body(*refs):
        (dx1_ref, proj_ref, n_ref, rstd_ref, c3_ref, ya_ref, yb_ref, mix_ref, mod_ref, w3_ref, w31_ref, lng_ref,
         lnb_ref, wa_hbm, wb_hbm, wo_hbm) = refs[:16]
        ready_refs = refs[16:16 + n_r]
        dproj_ref, dmix_ref, dya_ref, dyb_ref, acc_ref = refs[16 + n_r:21 + n_r]
        recv_refs = refs[21 + n_r:21 + 2 * n_r]
        wa_v, wb_v, wo_v, dc3_ext, du1_ext, e_buf, u0_buf, du0_buf, dw8, sems, e_send, e_recv = refs[21 + 2 * n_r:]
        step = pl.program_id(0)
        pl.when(step == 0)(lambda: _exchange_send(ready_refs, recv_refs, ready_shapes, ready_axes, e_send, e_recv))
        pl.when(step == n_t - 1)(lambda: _exchange_finish(recv_refs, e_send, e_recv))
        _load_weights([(wa_hbm, wa_v), (wb_hbm, wb_v), (wo_hbm, wo_v)], sems)

        @pl.when(pl.program_id(0) == 0)
        def _():
            acc_ref[...] = jnp.zeros(acc_ref.shape, F32)
            dc3_ext[ts:ts + HALO_SHORT, :] = jnp.zeros((HALO_SHORT, d), F32)
            du1_ext[ts:ts + HALO_CONF, :] = jnp.zeros((HALO_CONF, d), F32)
            dw8[...] = jnp.zeros(dw8.shape, F32)

        pj = lambda k: proj_ref[:, k * d:(k + 1) * d].astype(F32)
        g1 = mod_ref[:, 2 * d:3 * d]
        dx1v = dx1_ref[...]
        acc_ref[ROW_DMOD + 2:ROW_DMOD + 2 + 1, :] += _colsum(dx1v * mix_ref[...])
        dmix = (dx1v * g1).astype(BF16)
        dmix_ref[...] = dmix
        dmerged = _dot_nt(dmix, wo_v[...])
        g_a, g_b = _sig(pj(5)), _sig(pj(6))
        y_a, y_b = ya_ref[...].astype(F32), yb_ref[...].astype(F32)
        dproj_ref[:, 5 * d:6 * d] = (dmerged * y_a * g_a * (1.0 - g_a)).astype(BF16)
        dproj_ref[:, 6 * d:7 * d] = (dmerged * y_b * g_b * (1.0 - g_b)).astype(BF16)
        dya = (dmerged * g_a).astype(BF16)
        dyb = (dmerged * g_b).astype(BF16)
        dya_ref[...] = dya
        dyb_ref[...] = dyb

        dyap = _dot_nt(dya, wa_v[...])
        dproj_ref[:, 0:d] = (dyap * c3_ref[...].astype(F32)).astype(BF16)
        dc3_ext[0:ts, :] = dyap * pj(0)
        c_s, v_s = pj(1), pj(2)
        cv = c_s * v_s
        dcv = jnp.zeros((ts, d), F32)
        for j in range(SHORT_K):
            k = SHORT_K - 1 - j
            sj = dc3_ext[j:j + ts, :]
            dcv = dcv + w3_ref[k:k + 1, :] * sj
            acc_ref[ROW_W3 + k:ROW_W3 + k + 1, :] += _colsum(cv * sj)
        dc3_ext[ts:ts + HALO_SHORT, :] = dc3_ext[0:HALO_SHORT, :]
        dproj_ref[:, d:2 * d] = (dcv * v_s).astype(BF16)
        dproj_ref[:, 2 * d:3 * d] = (dcv * c_s).astype(BF16)

        du3 = _dot_nt(dyb, wb_v[...])
        nv = n_ref[...].astype(F32)
        lg = lng_ref[...]
        u2 = nv * lg + lnb_ref[...]
        s2 = _sig(u2)
        du2 = du3 * (s2 * (1.0 + u2 * (1.0 - s2)))
        acc_ref[ROW_LNG:ROW_LNG + 1, :] += _colsum(du2 * nv)
        acc_ref[ROW_LNB:ROW_LNB + 1, :] += _colsum(du2)
        dn = du2 * lg
        du1 = rstd_ref[...] * (dn - jnp.mean(dn, axis=-1, keepdims=True) - nv * jnp.mean(dn * nv, axis=-1, keepdims=True))
        acc_ref[ROW_CB:ROW_CB + 1, :] += _colsum(du1)
        du1_ext[0:ts, :] = du1
        v_c = pj(3)
        sg = _sig(pj(4))
        u0_buf[...] = v_c * sg

        def conv_chunk(tap, r0, lo):
            lanes = slice(lo, lo + CONV_LANES)
            u0c = u0_buf[pl.ds(r0, CONV_ROWS), lanes]
            acc = jnp.zeros((CONV_ROWS, CONV_LANES), F32)
            for j in range(CONF_K):
                k = CONF_K - 1 - j
                sj = tap(j)
                acc = acc + w31_ref[k:k + 1, lanes] * sj
                prod = u0c * sj
                part = prod[0:8]
                for i in range(1, CONV_ROWS // 8):
                    part = part + prod[8 * i:8 * i + 8]
                dw8[k, :, lanes] += part
            du0_buf[pl.ds(r0, CONV_ROWS), lanes] = acc

        _conv_chunks(du1_ext, e_buf, ts, d, conv_chunk, unroll=False)
        du0 = du0_buf[...]
        du1_ext[ts:ts + HALO_CONF, :] = du1_ext[0:HALO_CONF, :]

        @pl.when(step == n_t - 1)
        def _():
            for k in range(CONF_K):
                acc_ref[ROW_W31 + k:ROW_W31 + k + 1, :] = _colsum(dw8[k])
        dproj_ref[:, 3 * d:4 * d] = (du0 * sg).astype(BF16)
        dproj_ref[:, 4 * d:5 * d] = (du0 * v_c * sg * (1.0 - sg)).astype(BF16)

    rev = lambda w: pl.BlockSpec((ts, w), lambda i: (n_t - 1 - i, 0))
    full = lambda a: pl.BlockSpec(a.shape, lambda i: (0, 0))
    sd = lambda w, dt: jax.ShapeDtypeStruct((s_len, w), dt)
    outs = [sd(7 * d, BF16), sd(d, BF16), sd(d, BF16), sd(d, BF16), jax.ShapeDtypeStruct((N_SMALL_ROWS, d), F32)]
    res = _pallas(
        body, name="bwd_mix", grid=(n_t,), out_shape=tuple(outs) + _exchange_out_shapes(ready, ready_axes),
        in_specs=[rev(d), rev(7 * d), rev(d), rev(1), rev(d), rev(d), rev(d), rev(d), full(mod), full(w3), full(w31),
                  full(lng), full(lnb), ANY, ANY, ANY] + [ANY] * n_r,
        out_specs=tuple([rev(o.shape[1]) for o in outs[:-1]] + [pl.BlockSpec((N_SMALL_ROWS, d), lambda i: (0, 0))] + [ANY] * n_r),
        scratch_shapes=[pltpu.VMEM(w_a.shape, BF16), pltpu.VMEM(w_b.shape, BF16), pltpu.VMEM(w_o.shape, BF16),
                        pltpu.VMEM((ts + HALO_SHORT, d), F32), pltpu.VMEM((ts + HALO_CONF, d), F32),
                        pltpu.VMEM((7, ts + HALO_CONF - 8, d // 2), F32), pltpu.VMEM((ts, d), F32), pltpu.VMEM((ts, d), F32),
                        pltpu.VMEM((CONF_K, 8, d), F32), pltpu.SemaphoreType.DMA((3,))] + _exchange_sems(n_r),
        compiler_params=_params("arbitrary"),
    )(dx1, proj, nrm, rstd, c3, ya, yb, mix, mod, w3, w31, lng, lnb, w_a, w_b, w_o, *ready)
    return res[:len(outs)], res[len(outs):]


def _bwd_in(dproj, x, dx1, mod, gn1, w_in, ready, ready_axes, ts):
    s_len, d = x.shape
    n_t = s_len // ts
    n_r = len(ready)
    ready_shapes = [g.shape for g in ready]

    def body(*refs):
        dproj_ref, x_ref, dx1_ref, mod_ref, gn1_ref, win_hbm = refs[:6]
        ready_refs = refs[6:6 + n_r]
        gx_ref, acc_ref = refs[6 + n_r:8 + n_r]
        recv_refs = refs[8 + n_r:8 + 2 * n_r]
        win_v, sems, e_send, e_recv = refs[8 + 2 * n_r:]
        step = pl.program_id(0)
        pl.when(step == 0)(lambda: _exchange_send(ready_refs, recv_refs, ready_shapes, ready_axes, e_send, e_recv))
        pl.when(step == n_t - 1)(lambda: _exchange_finish(recv_refs, e_send, e_recv))
        _load_weights([(win_hbm, win_v)], sems)

        @pl.when(pl.program_id(0) == 0)
        def _():
            acc_ref[...] = jnp.zeros(acc_ref.shape, F32)

        xv = x_ref[...]
        sc1 = mod_ref[:, d:2 * d]
        gn1v = gn1_ref[...]
        r = lax.rsqrt(jnp.mean(xv * xv, axis=-1, keepdims=True) + EPS)
        xn = xv * r
        dh = _dot_nt(dproj_ref[...], win_v[...])
        acc_ref[ROW_DMOD:ROW_DMOD + 1, :] += _colsum(dh)
        acc_ref[ROW_DMOD + 1:ROW_DMOD + 1 + 1, :] += _colsum(dh * (xn * gn1v))
        acc_ref[ROW_GN1:ROW_GN1 + 1, :] += _colsum(dh * xn * (1.0 + sc1))
        dxn = dh * gn1v * (1.0 + sc1)
        gx_ref[...] = dx1_ref[...] + r * (dxn - xn * jnp.mean(dxn * xn, axis=-1, keepdims=True))

    row = lambda w: pl.BlockSpec((ts, w), lambda i: (i, 0))
    full = lambda a: pl.BlockSpec(a.shape, lambda i: (0, 0))
    res = _pallas(
        body, name="bwd_in", grid=(n_t,),
        out_shape=(jax.ShapeDtypeStruct((s_len, d), F32), jax.ShapeDtypeStruct((N_SMALL_ROWS, d), F32))
        + _exchange_out_shapes(ready, ready_axes),
        in_specs=[row(7 * d), row(d), row(d), full(mod), full(gn1), ANY] + [ANY] * n_r,
        out_specs=(row(d), pl.BlockSpec((N_SMALL_ROWS, d), lambda i: (0, 0))) + tuple([ANY] * n_r),
        scratch_shapes=[pltpu.VMEM(w_in.shape, BF16), pltpu.SemaphoreType.DMA((1,))] + _exchange_sems(n_r),
        compiler_params=_params("arbitrary"),
    )(dproj, x, dx1, mod, gn1, w_in, *ready)
    return res[0], res[1], res[2:]


def _weight_grad(a, b, bm, bn, tk, name, ready=(), ready_axes=()):
    s_len, m = a.shape
    n = b.shape[1]
    n_i, n_j, n_k = m // bm, n // bn, s_len // tk
    n_r = len(ready)
    ready_shapes = [g.shape for g in ready]

    def body(*refs):
        a_ref, b_ref = refs[:2]
        ready_refs = refs[2:2 + n_r]
        o_ref = refs[2 + n_r]
        recv_refs = refs[3 + n_r:3 + 2 * n_r]
        acc = refs[3 + 2 * n_r]
        k = pl.program_id(2)
        if n_r:
            e_send, e_recv = refs[4 + 2 * n_r:]
            first = (pl.program_id(0) == 0) & (pl.program_id(1) == 0) & (k == 0)
            last = (pl.program_id(0) == n_i - 1) & (pl.program_id(1) == n_j - 1) & (k == n_k - 1)
            pl.when(first)(lambda: _exchange_send(ready_refs, recv_refs, ready_shapes, ready_axes, e_send, e_recv))
            pl.when(last)(lambda: _exchange_finish(recv_refs, e_send, e_recv))
        prod = _dot_tn(a_ref[...], b_ref[...])

        @pl.when(k == 0)
        def _():
            acc[...] = prod

        @pl.when(k > 0)
        def _():
            acc[...] += prod

        @pl.when(k == n_k - 1)
        def _():
            o_ref[...] = acc[...].astype(BF16)

    res = _pallas(
        body, name=name, grid=(n_i, n_j, n_k),
        out_shape=(jax.ShapeDtypeStruct((m, n), BF16),) + _exchange_out_shapes(ready, ready_axes),
        in_specs=[pl.BlockSpec((tk, bm), lambda i, j, k: (k, i)), pl.BlockSpec((tk, bn), lambda i, j, k: (k, j))] + [ANY] * n_r,
        out_specs=(pl.BlockSpec((bm, bn), lambda i, j, k: (i, j)),) + tuple([ANY] * n_r),
        scratch_shapes=[pltpu.VMEM((bm, bn), F32)] + (_exchange_sems(n_r) if n_r else []),
        compiler_params=_params("arbitrary", "arbitrary", "arbitrary"),
    )(a, b, *ready)
    return res[0], res[1:]


def _sum_slots(recv, grad, axis, chip, core, name):
    n_s, r, c = recv.shape
    tr = 16
    for cand in (256, 128, 176, 64, 32, 16):
        if r % cand == 0:
            tr = cand
            break
    n_i = r // tr

    def body(s_ref, own_ref, r_ref, o_ref):
        tot = own_ref[...].astype(F32)
        for k in range(n_s):
            tot = tot + r_ref[k].astype(F32)
        o_ref[...] = tot

    if axis == 0:
        own_map = lambda i, s: ((2 * s[0] + s[1]) * n_i + i, 0)
    else:
        own_map = lambda i, s: (s[1] * n_i + i, s[0])
    grid_spec = pltpu.PrefetchScalarGridSpec(
        num_scalar_prefetch=1, grid=(n_i,),
        in_specs=[pl.BlockSpec((tr, c), own_map), pl.BlockSpec((n_s, tr, c), lambda i, s: (0, i, 0))],
        out_specs=pl.BlockSpec((tr, c), lambda i, s: (s[1] * n_i + i, 0)))
    return _pallas(body, name=name, grid_spec=grid_spec, out_shape=jax.ShapeDtypeStruct((2 * r, c), F32),
                   compiler_params=_params("arbitrary"))(jnp.stack([chip, core]).astype(jnp.int32), grad, recv)


def _adamw_math(w, g, m, v):
    m = ADAM_B1 * m + (1.0 - ADAM_B1) * g
    v = ADAM_B2 * v + (1.0 - ADAM_B2) * (g * g)
    m_hat = m / (1.0 - ADAM_B1 ** ADAM_STEP)
    v_hat = v / (1.0 - ADAM_B2 ** ADAM_STEP)
    delta = -ADAM_LR * (m_hat / (jnp.sqrt(v_hat) + ADAM_EPS) + ADAM_WD * w)
    return delta, m, v


def _adamw(w, g, m, v, name):
    r, c = w.shape
    tr = r
    for cand in (256, 128, 176, 64, 32, 16, 8):
        if r % cand == 0:
            tr = cand
            break

    def body(w_ref, g_ref, m_ref, v_ref, go_ref, d_ref, nm_ref, nv_ref):
        g = g_ref[...]
        go_ref[...] = g
        d_ref[...], nm_ref[...], nv_ref[...] = _adamw_math(w_ref[...], g, m_ref[...], v_ref[...])

    blk = pl.BlockSpec((tr, c), lambda i: (i, 0))
    return _pallas(body, name=name, grid=(r // tr,), out_shape=tuple(jax.ShapeDtypeStruct((r, c), F32) for _ in range(4)),
                   in_specs=[blk] * 4, out_specs=(blk,) * 4, compiler_params=_params("arbitrary"))(w, g, m, v)


_SMALL = [
    ("b_ada", ROW_DMOD, "mod"), ("norm_mix_g", ROW_GN1, "row"), ("conv_short_w", ROW_W3, "shard"),
    ("conv_conf_w", ROW_W31, "shard"), ("conv_conf_b", ROW_CB, "row"), ("conf_ln_g", ROW_LNG, "row"),
    ("conf_ln_b", ROW_LNB, "row"), ("norm_ffn_g", ROW_GN2, "row"), ("final_norm_g", ROW_GF, "row")]


def _small_update(p_all, chip_onehot, ws, ms, vs):
    n = len(_SMALL)
    d = p_all.shape[2]

    def body(*refs):
        p_ref, oh_ref = refs[:2]
        w_refs, m_refs, v_refs = refs[2:2 + n], refs[2 + n:2 + 2 * n], refs[2 + 2 * n:2 + 3 * n]
        loss_ref = refs[2 + 3 * n]
        g_out, d_out, m_out, v_out = [refs[3 + (3 + q) * n:3 + (4 + q) * n] for q in range(4)]
        g_all = p_ref[0]
        for s in range(1, N_DEV):
            g_all = g_all + p_ref[s]
        loss_ref[...] = jnp.zeros(loss_ref.shape, F32) + (0.5 / d) * jnp.sum(g_all[ROW_SQERR:ROW_SQERR + 1, :])

        def emit(i, idx, g):
            dl, nm, nv = _adamw_math(w_refs[i][idx], g, m_refs[i][idx], v_refs[i][idx])
            g_out[i][idx], d_out[i][idx], m_out[i][idx], v_out[i][idx] = g, dl, nm, nv

        everything = (slice(None), slice(None))
        for i, (_, row, kind) in enumerate(_SMALL):
            if kind == "mod":
                for k in range(N_MOD):
                    emit(i, (slice(0, 1), slice(k * d, (k + 1) * d)), g_all[row + k:row + k + 1, :])
            elif kind == "row":
                emit(i, everything, g_all[row:row + 1, :])
            else:
                taps, dq = w_refs[i].shape
                g = jnp.zeros((taps, dq), F32)
                for j in range(N_CHIP):
                    g = g + oh_ref[0:1, j:j + 1] * g_all[row:row + taps, j * dq:(j + 1) * dq]
                emit(i, everything, g)

    shapes = [jax.ShapeDtypeStruct(w.shape, F32) for w in ws]
    res = _pallas(body, name="small_update", out_shape=tuple([jax.ShapeDtypeStruct((8, 128), F32)] + shapes * 4),
                  in_specs=[VMEM_FULL] * (2 + 3 * n), out_specs=tuple([VMEM_FULL] * (1 + 4 * n)),
                  compiler_params=_params())(p_all, chip_onehot, *ws, *ms, *vs)
    return res[0], [res[1 + q * n:1 + (q + 1) * n] for q in range(4)]


def _w_ada_grad(c_t, dmod_part):
    def body(c_ref, g_ref, o_ref):
        cv = c_ref[...]
        o_ref[...] = _dot((cv * _sig(cv)).astype(BF16), g_ref[...].astype(BF16))

    return _pallas(body, name="w_ada_grad", out_shape=jax.ShapeDtypeStruct((c_t.shape[0], dmod_part.shape[1]), F32),
                   in_specs=[VMEM_FULL] * 2, out_specs=VMEM_FULL, compiler_params=_params())(c_t, dmod_part)


def _step(x, c, w_ada, b_ada, norm_mix_g, w_in, conv_short_w, w_short_out, conv_conf_w, conv_conf_b, conf_ln_g,
          conf_ln_b, w_conf_out, w_o, norm_ffn_g, w_ffn_in, w_ffn_out, final_norm_g, loss_target, moments, ts):
    xi, yi, ci = _place()
    chip = 2 * xi + yi
    me = 4 * xi + 2 * yi + ci
    x2d, tgt = x[0], loss_target[0]
    s_len, d = x2d.shape
    dq = d // N_CHIP
    mq = N_MOD * d // N_CHIP
    gf = final_norm_g.reshape(1, d)

    taps = jnp.concatenate([jnp.pad(conv_short_w[0], ((0, 8 - SHORT_K), (0, 0))),
                            jnp.pad(conv_conf_w[0], ((0, 32 - CONF_K), (0, 0)))], axis=0)
    n_tr = 40 * dq // d
    c_taps = _all_gather_small(jnp.concatenate([jnp.broadcast_to(c, (8, d)), taps.reshape(n_tr, d),
                                                jnp.zeros((16 - n_tr, d), F32)], axis=0), "gather_c_taps")
    c_taps = c_taps.reshape(N_DEV, 24, d)
    c_all = c_taps[:, 0]
    b_part = lax.dynamic_slice(b_ada, (0, chip * mq), (1, mq))
    mod_part = _mod_part(c_all, w_ada[0].astype(BF16), b_part)
    mod_all = _all_gather_small(mod_part, "gather_mod").reshape(N_CHIP, 2, N_DEV, mq)[:, 0]
    mod_full = jnp.transpose(mod_all, (1, 0, 2)).reshape(N_DEV, N_MOD * d)
    mod = lax.dynamic_slice(mod_full, (me, 0), (1, N_MOD * d))

    taps_all = c_taps[:, 8:8 + n_tr].reshape(N_CHIP, 2, 40, dq)[:, 0]
    taps_full = jnp.transpose(taps_all, (1, 0, 2)).reshape(40, d)
    w3, w31 = taps_full[0:SHORT_K], taps_full[8:8 + CONF_K]

    big = [w_in[0], w_short_out[0], w_conf_out[0], w_o[0], w_ffn_in[0], w_ffn_out[0]]
    axes = [1, 0, 0, 0, 1, 0]
    big_names = ["w_in", "w_short_out", "w_conf_out", "w_o", "w_ffn_in", "w_ffn_out"]
    placed = [_place_shard(w, ax, chip, "place_" + n) for w, ax, n in zip(big, axes, big_names)]
    win_f, wa_f, wb_f, wo_f = _gather_weights(placed[:4], axes[:4], "gather_weights")

    (proj, h, yap, c3, nrm, rstd, u3, ya, yb, merged, mix, x1), (wfi_f, wfo_f) = _fwd_mix(
        x2d, mod, norm_mix_g, w3, w31, conv_conf_b, conf_ln_g, conf_ln_b, win_f, wa_f, wb_f, wo_f, placed[4:], axes[4:], ts)
    h2, hid, df, dab, dx1, acc_f = _ffn(x1, tgt, mod, norm_ffn_g, gf, wfi_f, wfo_f, ts)
    tk = min(1024, s_len)
    f = wfo_f.shape[0]
    g_ffn = [_weight_grad(h2, dab, d, 2 * f // N_CHIP, tk, "grad_w_ffn_in")[0], _weight_grad(hid, df, f, d, tk, "grad_w_ffn_out")[0]]
    (dproj, dmix, dya, dyb, acc_m), r_ffn = _bwd_mix(dx1, proj, nrm, rstd, c3, ya, yb, mix, mod, w3, w31, conf_ln_g,
                                                    conf_ln_b, wa_f, wb_f, wo_f, g_ffn, axes[4:], ts)
    g_sq = [_weight_grad(yap, dya, d, d, tk, "grad_w_short_out")[0], _weight_grad(u3, dyb, d, d, tk, "grad_w_conf_out")[0],
            _weight_grad(merged, dmix, d, d, tk, "grad_w_o")[0]]
    g_in, r_sq = _weight_grad(h, dproj, d, 7 * d // N_CHIP, tk, "grad_w_in", g_sq, axes[1:4])
    grad_x, acc_i, r_in = _bwd_in(dproj, x2d, dx1, mod, norm_mix_g, win_f, [g_in], axes[:1], ts)
    g_in = [g_in]
    r_in_sq = list(r_in) + list(r_sq)
    grads = g_in + g_sq + g_ffn
    recv = list(r_in_sq) + list(r_ffn)
    g_big = _swap_halves([_sum_slots(r, g, ax, chip, ci, "sum_slots_" + n)
                          for r, g, ax, n in zip(recv, grads, axes, big_names)])

    p_all = _all_gather_small(acc_f + acc_m + acc_i, "gather_small_grads").reshape(N_DEV, N_SMALL_ROWS, d)
    small_w = dict(b_ada=b_ada, norm_mix_g=norm_mix_g, conv_short_w=conv_short_w, conv_conf_w=conv_conf_w,
                   conv_conf_b=conv_conf_b, conf_ln_g=conf_ln_g, conf_ln_b=conf_ln_b, norm_ffn_g=norm_ffn_g,
                   final_norm_g=final_norm_g)
    as_2d = lambda t: t.reshape((-1, t.shape[-1]))
    chip_onehot = (lax.broadcasted_iota(jnp.int32, (1, 128), 1) == chip).astype(F32)
    loss_tile, small_out = _small_update(
        p_all, chip_onehot, [as_2d(small_w[n]) for n, _, _ in _SMALL], [as_2d(moments["m_" + n]) for n, _, _ in _SMALL],
        [as_2d(moments["v_" + n]) for n, _, _ in _SMALL])
    small_g, small_d, small_m, small_v = [{n: o.reshape(small_w[n].shape) for (n, _, _), o in zip(_SMALL, outs)}
                                          for outs in small_out]

    dmod_all = p_all[:, 0:N_MOD].reshape(N_DEV, N_MOD * d)
    dmod_part = jnp.pad(lax.dynamic_slice(dmod_all, (0, chip * mq), (N_DEV, mq)), ((0, 128 - N_DEV), (0, 0)))
    c_t = jnp.pad(jnp.transpose(c_all), ((0, 0), (0, 128 - N_DEV)))
    g_ada = _w_ada_grad(c_t, dmod_part)

    out_g, out_d, out_m, out_v = dict(small_g), dict(small_d), dict(small_m), dict(small_v)
    for name, w, g in zip(["w_ada"] + big_names, [w_ada[0]] + big, [g_ada] + list(g_big)):
        go, dl, nm, nv = _adamw(w, g, moments["m_" + name][0], moments["v_" + name][0], "adamw_" + name)
        out_g[name], out_d[name], out_m[name], out_v[name] = go[None], dl[None], nm[None], nv[None]

    loss = loss_tile[0, 0]
    return loss, grad_x[None], out_g, out_d, out_m, out_v


_WEIGHTS = ["w_ada", "b_ada", "norm_mix_g", "w_in", "conv_short_w", "w_short_out", "conv_conf_w", "conv_conf_b",
            "conf_ln_g", "conf_ln_b", "w_conf_out", "w_o", "norm_ffn_g", "w_ffn_in", "w_ffn_out", "final_norm_g"]
ROW_TILE = 256


def kernel(x, c, w_ada, b_ada, norm_mix_g, w_in, conv_short_w, w_short_out, conv_conf_w, conv_conf_b, conf_ln_g, conf_ln_b, w_conf_out, w_o, norm_ffn_g, w_ffn_in, w_ffn_out, final_norm_g, loss_target, m_w_ada, m_b_ada, m_norm_mix_g, m_w_in, m_conv_short_w, m_w_short_out, m_conv_conf_w, m_conv_conf_b, m_conf_ln_g, m_conf_ln_b, m_w_conf_out, m_w_o, m_norm_ffn_g, m_w_ffn_in, m_w_ffn_out, m_final_norm_g, v_w_ada, v_b_ada, v_norm_mix_g, v_w_in, v_conv_short_w, v_w_short_out, v_conv_conf_w, v_conv_conf_b, v_conf_ln_g, v_conf_ln_b, v_w_conf_out, v_w_o, v_norm_ffn_g, v_w_ffn_in, v_w_ffn_out, v_final_norm_g):
    moments = dict(
        m_w_ada=m_w_ada, m_b_ada=m_b_ada, m_norm_mix_g=m_norm_mix_g, m_w_in=m_w_in, m_conv_short_w=m_conv_short_w,
        m_w_short_out=m_w_short_out, m_conv_conf_w=m_conv_conf_w, m_conv_conf_b=m_conv_conf_b, m_conf_ln_g=m_conf_ln_g,
        m_conf_ln_b=m_conf_ln_b, m_w_conf_out=m_w_conf_out, m_w_o=m_w_o, m_norm_ffn_g=m_norm_ffn_g, m_w_ffn_in=m_w_ffn_in,
        m_w_ffn_out=m_w_ffn_out, m_final_norm_g=m_final_norm_g,
        v_w_ada=v_w_ada, v_b_ada=v_b_ada, v_norm_mix_g=v_norm_mix_g, v_w_in=v_w_in, v_conv_short_w=v_conv_short_w,
        v_w_short_out=v_w_short_out, v_conv_conf_w=v_conv_conf_w, v_conv_conf_b=v_conv_conf_b, v_conf_ln_g=v_conf_ln_g,
        v_conf_ln_b=v_conf_ln_b, v_w_conf_out=v_w_conf_out, v_w_o=v_w_o, v_norm_ffn_g=v_norm_ffn_g, v_w_ffn_in=v_w_ffn_in,
        v_w_ffn_out=v_w_ffn_out, v_final_norm_g=v_final_norm_g)
    loss, grad_x, g, dl, nm, nv = _step(
        x, c, w_ada, b_ada, norm_mix_g, w_in, conv_short_w, w_short_out, conv_conf_w, conv_conf_b, conf_ln_g, conf_ln_b,
        w_conf_out, w_o, norm_ffn_g, w_ffn_in, w_ffn_out, final_norm_g, loss_target, moments, min(ROW_TILE, x.shape[1]))
    return (loss, grad_x, *[g[n] for n in _WEIGHTS], *[dl[n] for n in _WEIGHTS], *[nm[n] for n in _WEIGHTS],
            *[nv[n] for n in _WEIGHTS])
```

```python
import jax
import jax.numpy as jnp
from jax import lax
from jax.experimental import pallas as pl
from jax.experimental.pallas import tpu as pltpu

F32 = jnp.float32
BF16 = jnp.bfloat16
EPS = 1e-6
LN_EPS = 1e-5
SHORT_K = 3
CONF_K = 31
N_MOD = 6
N_DEV = 8
N_CHIP = 4
ADAM_LR = 0.001
ADAM_B1 = 0.9
ADAM_B2 = 0.999
ADAM_EPS = 1e-08
ADAM_WD = 0.01
ADAM_STEP = 10
MESH = pl.DeviceIdType.MESH
V7X_VMEM_LIMIT_BYTES = 60 * 1024 * 1024
HALO_SHORT = 8
HALO_CONF = 32
N_SMALL_ROWS = 48
ROW_DMOD, ROW_GN1, ROW_W3, ROW_W31, ROW_CB, ROW_LNG, ROW_LNB, ROW_GN2, ROW_GF, ROW_SQERR = 0, 6, 7, 10, 41, 42, 43, 44, 45, 46
CONV_ROWS = 64
CONV_LANES = 256

ANY = pl.BlockSpec(memory_space=pl.ANY)
VMEM_FULL = pl.BlockSpec(memory_space=pltpu.VMEM)


def _pallas(body, **kw):
    return pl.pallas_call(body, **kw)


def _params(*sem):
    return pltpu.CompilerParams(dimension_semantics=sem, vmem_limit_bytes=V7X_VMEM_LIMIT_BYTES)


def _sig(z):
    return jax.nn.sigmoid(z)


def _dot(a, b):
    return jnp.dot(a, b, preferred_element_type=F32)


def _dot_nt(a, b):
    return lax.dot_general(a, b, (((1,), (1,)), ((), ())), preferred_element_type=F32)


def _dot_tn(a, b):
    return lax.dot_general(a, b, (((0,), (0,)), ((), ())), preferred_element_type=F32)


def _colsum(z):
    return jnp.sum(z, axis=0, keepdims=True)


def _place():
    return lax.axis_index("x"), lax.axis_index("y"), lax.axis_index("c")


def _all_gather_small(block, name):
    m_per, n = block.shape

    def body(x_ref, out_ref, send_sems, recv_sems, local_sem):
        x, y, c = _place()
        me, sibling = (x, y, c), (x, y, 1 - c)
        chips = [(1 - x, y), (x, 1 - y), (1 - x, 1 - y)]

        def rows(px, py, pc):
            return out_ref.at[pl.ds((4 * px + 2 * py + pc) * m_per, m_per), :]

        def copy(k, blk, to, src=None):
            return pltpu.make_async_remote_copy(
                src_ref=rows(*blk) if src is None else src, dst_ref=rows(*blk),
                send_sem=send_sems.at[k], recv_sem=recv_sems.at[k], device_id=to, device_id_type=MESH)

        mine = pltpu.make_async_copy(x_ref, rows(*me), local_sem)
        mine.start()
        first = [copy(0, me, sibling, src=x_ref)]
        first += [copy(1 + j, me, (*chip, c), src=x_ref) for j, chip in enumerate(chips)]
        for cp in first:
            cp.start()
        passed = [copy(4 + j, (*chip, c), sibling) for j, chip in enumerate(chips)]
        for j, chip in enumerate(chips):
            copy(1 + j, (*chip, c), me).wait_recv()
            passed[j].start()
        copy(0, sibling, me).wait_recv()
        for j, chip in enumerate(chips):
            copy(4 + j, (*chip, 1 - c), me).wait_recv()
        for cp in first + passed:
            cp.wait_send()
        mine.wait()

    return _pallas(
        body, name=name,
        out_shape=jax.ShapeDtypeStruct((N_DEV * m_per, n), block.dtype),
        in_specs=[VMEM_FULL], out_specs=VMEM_FULL,
        scratch_shapes=[pltpu.SemaphoreType.DMA((7,)), pltpu.SemaphoreType.DMA((7,)), pltpu.SemaphoreType.DMA],
    )(block)


def _shard_slice(ref, full_shape, shard_axis, chip, half):
    m, n = full_shape
    if shard_axis == 0:
        r = m // N_CHIP
        return ref.at[pl.ds(chip * r + half * (r // 2), r // 2), :]
    cw = n // N_CHIP
    return ref.at[pl.ds(half * (m // 2), m // 2), pl.ds(pl.multiple_of(chip * cw, 128), cw)]


def _place_shard(w, axis, chip, name):
    r, cc = w.shape
    tr = r if axis == 0 else min(256, r)
    full = (r * N_CHIP, cc) if axis == 0 else (r, cc * N_CHIP)

    def body(s_ref, w_ref, o_ref):
        o_ref[...] = w_ref[...].astype(BF16)

    out_map = (lambda i, s: (s[0], 0)) if axis == 0 else (lambda i, s: (i, s[0]))
    grid_spec = pltpu.PrefetchScalarGridSpec(
        num_scalar_prefetch=1, grid=(r // tr,), in_specs=[pl.BlockSpec((tr, cc), lambda i, s: (i, 0))],
        out_specs=pl.BlockSpec((tr, cc), out_map))
    return _pallas(body, name=name, grid_spec=grid_spec, out_shape=jax.ShapeDtypeStruct(full, BF16),
                   compiler_params=_params("arbitrary"))(jnp.reshape(chip, (1,)).astype(jnp.int32), w)


def _other_chips():
    x, y, _ = _place()
    return [(1 - x, y), (x, 1 - y), (1 - x, 1 - y)]


def _gather_send(refs, shapes, axes, send_sems, recv_sems):
    x, y, c = _place()
    for a in range(len(refs)):
        mine = _shard_slice(refs[a], shapes[a], axes[a], 2 * x + y, c)
        for j, (px, py) in enumerate(_other_chips()):
            pltpu.make_async_remote_copy(src_ref=mine, dst_ref=mine, send_sem=send_sems.at[a, j], recv_sem=recv_sems.at[a, j],
                                         device_id=(px, py, c), device_id_type=MESH).start()


def _gather_forward(refs, shapes, axes, send_sems, recv_sems):
    x, y, c = _place()
    for a in range(len(refs)):
        for j, (px, py) in enumerate(_other_chips()):
            landed = _shard_slice(refs[a], shapes[a], axes[a], 2 * px + py, c)
            pltpu.make_async_remote_copy(src_ref=landed, dst_ref=landed, send_sem=send_sems.at[a, j], recv_sem=recv_sems.at[a, j],
                                         device_id=(px, py, c), device_id_type=MESH).wait_recv()
            pltpu.make_async_remote_copy(src_ref=landed, dst_ref=landed, send_sem=send_sems.at[a, 3 + j],
                                         recv_sem=recv_sems.at[a, 3 + j], device_id=(x, y, 1 - c), device_id_type=MESH).start()


def _gather_finish(refs, shapes, axes, send_sems, recv_sems):
    x, y, c = _place()
    for a in range(len(refs)):
        for j, (px, py) in enumerate(_other_chips()):
            other = _shard_slice(refs[a], shapes[a], axes[a], 2 * px + py, 1 - c)
            landed = _shard_slice(refs[a], shapes[a], axes[a], 2 * px + py, c)
            mine = _shard_slice(refs[a], shapes[a], axes[a], 2 * x + y, c)
            me = (x, y, c)
            pltpu.make_async_remote_copy(src_ref=other, dst_ref=other, send_sem=send_sems.at[a, 3 + j],
                                         recv_sem=recv_sems.at[a, 3 + j], device_id=me, device_id_type=MESH).wait_recv()
            pltpu.make_async_remote_copy(src_ref=mine, dst_ref=mine, send_sem=send_sems.at[a, j], recv_sem=recv_sems.at[a, j],
                                         device_id=me, device_id_type=MESH).wait_send()
            pltpu.make_async_remote_copy(src_ref=landed, dst_ref=landed, send_sem=send_sems.at[a, 3 + j],
                                         recv_sem=recv_sems.at[a, 3 + j], device_id=me, device_id_type=MESH).wait_send()


def _gather_sems(n_w):
    return [pltpu.SemaphoreType.DMA((n_w, 6)), pltpu.SemaphoreType.DMA((n_w, 6))]


def _gather_weights(fulls, axes, name):
    n_w = len(fulls)
    shapes = [f.shape for f in fulls]

    def body(*refs):
        outs, sems = refs[n_w:2 * n_w], refs[2 * n_w:]
        _gather_send(outs, shapes, axes, *sems)
        _gather_forward(outs, shapes, axes, *sems)
        _gather_finish(outs, shapes, axes, *sems)

    return _pallas(
        body, name=name, out_shape=tuple(jax.ShapeDtypeStruct(f.shape, BF16) for f in fulls),
        in_specs=[ANY] * n_w, out_specs=tuple([ANY] * n_w), input_output_aliases={a: a for a in range(n_w)},
        scratch_shapes=_gather_sems(n_w),
    )(*fulls)


def _piece_shapes(grads, axes):
    return [((g.shape[0] // N_CHIP // 2, g.shape[1]) if ax == 0 else (g.shape[0] // 2, g.shape[1] // N_CHIP))
            for g, ax in zip(grads, axes)]


def _exchange_send(ins, outs, shapes, axes, send_sems, recv_sems):
    x, y, c = _place()
    me = 4 * x + 2 * y + c
    for a in range(len(ins)):
        for t in range(N_DEV):
            tx, ty, tc = t // 4, (t // 2) % 2, t % 2
            src = _shard_slice(ins[a], shapes[a], axes[a], 2 * tx + ty, tc)

            @pl.when(me != t)
            def _(src=src, a=a, t=t, to=(tx, ty, tc)):
                slot = (me - t + N_DEV) % N_DEV - 1
                pltpu.make_async_remote_copy(src_ref=src, dst_ref=outs[a].at[slot], send_sem=send_sems.at[a, t],
                                             recv_sem=recv_sems.at[a, slot], device_id=to, device_id_type=MESH).start()


def _exchange_finish(outs, send_sems, recv_sems):
    x, y, c = _place()
    me = 4 * x + 2 * y + c
    for a in range(len(outs)):
        for t in range(N_DEV):
            @pl.when(me != t)
            def _(a=a, t=t):
                slot = outs[a].at[0]
                pltpu.make_async_remote_copy(src_ref=slot, dst_ref=slot, send_sem=send_sems.at[a, t], recv_sem=recv_sems.at[a, 0],
                                             device_id=(x, y, c), device_id_type=MESH).wait_send()
        for k in range(N_DEV - 1):
            slot = outs[a].at[k]
            pltpu.make_async_remote_copy(src_ref=slot, dst_ref=slot, send_sem=send_sems.at[a, 0], recv_sem=recv_sems.at[a, k],
                                         device_id=(x, y, c), device_id_type=MESH).wait_recv()


def _exchange_sems(n_w):
    return [pltpu.SemaphoreType.DMA((n_w, N_DEV)), pltpu.SemaphoreType.DMA((n_w, N_DEV - 1))]


def _exchange_out_shapes(grads, axes):
    return tuple(jax.ShapeDtypeStruct((N_DEV - 1,) + p, BF16) for p in _piece_shapes(grads, axes))


def _swap_halves(fulls):
    n_w = len(fulls)

    def body(*refs):
        outs = refs[n_w:2 * n_w]
        send_sems, recv_sems = refs[2 * n_w:]
        x, y, c = _place()
        sibling = (x, y, 1 - c)
        waits = []
        for a in range(n_w):
            r = fulls[a].shape[0] // 2
            mine = outs[a].at[pl.ds(c * r, r), :]
            theirs = outs[a].at[pl.ds((1 - c) * r, r), :]
            rc = pltpu.make_async_remote_copy(src_ref=mine, dst_ref=mine, send_sem=send_sems.at[a], recv_sem=recv_sems.at[a],
                                              device_id=sibling, device_id_type=MESH)
            rc.start()
            got = pltpu.make_async_remote_copy(src_ref=theirs, dst_ref=theirs, send_sem=send_sems.at[a],
                                               recv_sem=recv_sems.at[a], device_id=sibling, device_id_type=MESH)
            waits += [rc.wait_send, got.wait_recv]
        for w in waits:
            w()

    return _pallas(
        body, name="swap_halves",
        out_shape=tuple(jax.ShapeDtypeStruct(f.shape, f.dtype) for f in fulls),
        in_specs=[ANY] * n_w, out_specs=tuple([ANY] * n_w), input_output_aliases={a: a for a in range(n_w)},
        scratch_shapes=[pltpu.SemaphoreType.DMA((n_w,)), pltpu.SemaphoreType.DMA((n_w,))],
    )(*fulls)


def _load_weights(pairs, sems):
    @pl.when(pl.program_id(0) == 0)
    def _():
        cps = [pltpu.make_async_copy(s, d, sems.at[k]) for k, (s, d) in enumerate(pairs)]
        for cp in cps:
            cp.start()
        for cp in cps:
            cp.wait()


def _conv_chunks(ext, e_buf, ts, d, chunk_fn, unroll):
    rows, dh = e_buf.shape[1], e_buf.shape[2]
    for hf in range(d // dh):
        for r in range(1, 8):
            e_buf[r - 1, :, :] = ext[r:r + rows, hf * dh:(hf + 1) * dh]
        for lq in range(dh // CONV_LANES):
            lo, el = hf * dh + lq * CONV_LANES, lq * CONV_LANES

            def body(ci, carry, lo=lo, el=el):
                r0 = pl.multiple_of(ci * CONV_ROWS, CONV_ROWS)

                def tap(off):
                    q, r = divmod(off, 8)
                    if r == 0:
                        return ext[pl.ds(r0 + 8 * q, CONV_ROWS), lo:lo + CONV_LANES]
                    return e_buf[r - 1, pl.ds(r0 + 8 * q, CONV_ROWS), el:el + CONV_LANES]

                chunk_fn(tap, r0, lo)
                return carry

            lax.fori_loop(0, ts // CONV_ROWS, body, 0, unroll=unroll)


def _mod_part(c_all, w_ada_bf, b_part):
    def body(c_ref, w_ref, b_ref, o_ref):
        cv = c_ref[...]
        o_ref[...] = _dot((cv * _sig(cv)).astype(BF16), w_ref[...]) + b_ref[...]

    return _pallas(body, name="mod_part", out_shape=jax.ShapeDtypeStruct((c_all.shape[0], w_ada_bf.shape[1]), F32),
                   in_specs=[VMEM_FULL] * 3, out_specs=VMEM_FULL, compiler_params=_params())(c_all, w_ada_bf, b_part)


def _fwd_mix(x, mod, gn1, w3, w31, cbias, lng, lnb, w_in, w_a, w_b, w_o, later, later_axes, ts):
    s_len, d = x.shape
    n_t = s_len // ts
    n_l = len(later)
    later_shapes = [w.shape for w in later]

    def body(*refs):
        (x_ref, mod_ref, gn1_ref, w3_ref, w31_ref, cb_ref, lng_ref, lnb_ref, win_hbm, wa_hbm, wb_hbm, wo_hbm) = refs[:12]
        (proj_ref, h_ref, yap_ref, c3_ref, n_ref, rstd_ref, u3_ref, ya_ref, yb_ref, mg_ref, mix_ref,
         x1_ref) = refs[12 + n_l:24 + n_l]
        later_refs = refs[24 + n_l:24 + 2 * n_l]
        win_v, wa_v, wb_v, wo_v, cv_ext, u0_ext, e_buf, u1_buf, sems, g_send, g_recv = refs[24 + 2 * n_l:]
        step = pl.program_id(0)
        pl.when(step == 0)(lambda: _gather_send(later_refs, later_shapes, later_axes, g_send, g_recv))
        pl.when(step == n_t // 2)(lambda: _gather_forward(later_refs, later_shapes, later_axes, g_send, g_recv))
        pl.when(step == n_t - 1)(lambda: _gather_finish(later_refs, later_shapes, later_axes, g_send, g_recv))
        _load_weights([(win_hbm, win_v), (wa_hbm, wa_v), (wb_hbm, wb_v), (wo_hbm, wo_v)], sems)

        @pl.when(pl.program_id(0) == 0)
        def _():
            cv_ext[0:HALO_SHORT, :] = jnp.zeros((HALO_SHORT, d), F32)
            u0_ext[0:HALO_CONF, :] = jnp.zeros((HALO_CONF, d), F32)

        xv = x_ref[...]
        sh1, sc1, g1 = mod_ref[:, 0:d], mod_ref[:, d:2 * d], mod_ref[:, 2 * d:3 * d]
        r = lax.rsqrt(jnp.mean(xv * xv, axis=-1, keepdims=True) + EPS)
        hb = (((xv * r) * gn1_ref[...]) * (1.0 + sc1) + sh1).astype(BF16)
        h_ref[...] = hb

        def pj(k):
            p = _dot(hb, win_v[:, k * d:(k + 1) * d])
            proj_ref[:, k * d:(k + 1) * d] = p.astype(BF16)
            return p

        c_s = pj(1)
        cv_ext[HALO_SHORT:HALO_SHORT + ts, :] = c_s * pj(2)
        conv3 = jnp.zeros((ts, d), F32)
        for k in range(SHORT_K):
            off = HALO_SHORT - (SHORT_K - 1) + k
            conv3 = conv3 + w3_ref[k:k + 1, :] * cv_ext[off:off + ts, :]
        cv_ext[0:HALO_SHORT, :] = cv_ext[ts:ts + HALO_SHORT, :]
        c3_ref[...] = conv3.astype(BF16)
        yap = (pj(0) * conv3).astype(BF16)
        yap_ref[...] = yap
        y_a = _dot(yap, wa_v[...])
        ya_ref[...] = y_a.astype(BF16)

        v_c = pj(3)
        u0_ext[HALO_CONF:HALO_CONF + ts, :] = v_c * _sig(pj(4))
        def conv_chunk(tap, r0, lo):
            acc = jnp.zeros((CONV_ROWS, CONV_LANES), F32)
            for k in range(CONF_K):
                acc = acc + w31_ref[k:k + 1, lo:lo + CONV_LANES] * tap(HALO_CONF - (CONF_K - 1) + k)
            u1_buf[pl.ds(r0, CONV_ROWS), lo:lo + CONV_LANES] = acc

        _conv_chunks(u0_ext, e_buf, ts, d, conv_chunk, unroll=True)
        u0_ext[0:HALO_CONF, :] = u0_ext[ts:ts + HALO_CONF, :]
        u1 = u1_buf[...] + cb_ref[...]
        mu = jnp.mean(u1, axis=-1, keepdims=True)
        var = jnp.mean(jnp.square(u1 - mu), axis=-1, keepdims=True)
        rstd = lax.rsqrt(var + LN_EPS)
        nrm = (u1 - mu) * rstd
        n_ref[...] = nrm.astype(BF16)
        rstd_ref[...] = rstd
        u2 = nrm * lng_ref[...] + lnb_ref[...]
        u3 = (u2 * _sig(u2)).astype(BF16)
        u3_ref[...] = u3
        y_b = _dot(u3, wb_v[...])
        yb_ref[...] = y_b.astype(BF16)

        merged = (_sig(pj(5)) * y_a + _sig(pj(6)) * y_b).astype(BF16)
        mg_ref[...] = merged
        mix = _dot(merged, wo_v[...])
        mix_ref[...] = mix
        x1_ref[...] = xv + g1 * mix

    row = lambda w, dt=None: pl.BlockSpec((ts, w), lambda i: (i, 0))
    full = lambda a: pl.BlockSpec(a.shape, lambda i: (0, 0))
    sd = lambda w, dt: jax.ShapeDtypeStruct((s_len, w), dt)
    outs = [sd(7 * d, BF16), sd(d, BF16), sd(d, BF16), sd(d, BF16), sd(d, BF16), sd(1, F32), sd(d, BF16), sd(d, BF16),
            sd(d, BF16), sd(d, BF16), sd(d, F32), sd(d, F32)]
    res = _pallas(
        body, name="fwd_mix", grid=(n_t,),
        out_shape=tuple(outs) + tuple(jax.ShapeDtypeStruct(w.shape, BF16) for w in later),
        in_specs=[row(d), full(mod), full(gn1), full(w3), full(w31), full(cbias), full(lng), full(lnb), ANY, ANY, ANY, ANY]
        + [ANY] * n_l,
        out_specs=tuple(row(o.shape[1]) for o in outs) + tuple([ANY] * n_l),
        input_output_aliases={12 + k: len(outs) + k for k in range(n_l)},
        scratch_shapes=[pltpu.VMEM(w_in.shape, BF16), pltpu.VMEM(w_a.shape, BF16), pltpu.VMEM(w_b.shape, BF16),
                        pltpu.VMEM(w_o.shape, BF16), pltpu.VMEM((ts + HALO_SHORT, d), F32),
                        pltpu.VMEM((ts + HALO_CONF, d), F32), pltpu.VMEM((7, ts + HALO_CONF - 8, d // 2), F32),
                        pltpu.VMEM((ts, d), F32), pltpu.SemaphoreType.DMA((4,))] + _gather_sems(n_l),
        compiler_params=_params("arbitrary"),
    )(x, mod, gn1, w3, w31, cbias, lng, lnb, w_in, w_a, w_b, w_o, *later)
    return res[:len(outs)], res[len(outs):]


def _ffn(x1, target, mod, gn2, gf, w_fi, w_fo, ts):
    s_len, d = x1.shape
    f = w_fo.shape[0]
    n_t = s_len // ts

    def body(x1_ref, t_ref, mod_ref, gn2_ref, gf_ref, wfi_hbm, wfo_hbm,
             h2_ref, hid_ref, df_ref, dab_ref, dx1_ref, acc_ref, wfi_v, wfo_v, sems):
        _load_weights([(wfi_hbm, wfi_v), (wfo_hbm, wfo_v)], sems)

        @pl.when(pl.program_id(0) == 0)
        def _():
            acc_ref[...] = jnp.zeros(acc_ref.shape, F32)

        x1v = x1_ref[...]
        sh2, sc2, g2 = mod_ref[:, 3 * d:4 * d], mod_ref[:, 4 * d:5 * d], mod_ref[:, 5 * d:6 * d]
        gn2v, gfv = gn2_ref[...], gf_ref[...]
        r2 = lax.rsqrt(jnp.mean(x1v * x1v, axis=-1, keepdims=True) + EPS)
        xn2 = x1v * r2
        h2 = ((xn2 * gn2v) * (1.0 + sc2) + sh2).astype(BF16)
        h2_ref[...] = h2
        a = _dot(h2, wfi_v[:, 0:f])
        bg = _dot(h2, wfi_v[:, f:2 * f])
        sa = _sig(a)
        silu_a = a * sa
        hid = (silu_a * bg).astype(BF16)
        hid_ref[...] = hid
        ffn = _dot(hid, wfo_v[...])
        x2 = x1v + g2 * ffn
        r3 = lax.rsqrt(jnp.mean(x2 * x2, axis=-1, keepdims=True) + EPS)
        xn3 = x2 * r3
        diff = xn3 * gfv - t_ref[...]
        dy = diff * (1.0 / d)
        dxn3 = dy * gfv
        dx2 = r3 * (dxn3 - xn3 * jnp.mean(dxn3 * xn3, axis=-1, keepdims=True))
        acc_ref[ROW_GF:ROW_GF + 1, :] += _colsum(dy * xn3)
        acc_ref[ROW_DMOD + 5:ROW_DMOD + 5 + 1, :] += _colsum(dx2 * ffn)
        acc_ref[ROW_SQERR:ROW_SQERR + 1, :] += _colsum(diff * diff)

        df = (dx2 * g2).astype(BF16)
        df_ref[...] = df
        dhid = _dot_nt(df, wfo_v[...])
        da = dhid * bg * (sa * (1.0 + a * (1.0 - sa)))
        dbg = dhid * silu_a
        dab_ref[:, 0:f] = da.astype(BF16)
        dab_ref[:, f:2 * f] = dbg.astype(BF16)
        dh2 = _dot_nt(da.astype(BF16), wfi_v[:, 0:f]) + _dot_nt(dbg.astype(BF16), wfi_v[:, f:2 * f])
        acc_ref[ROW_DMOD + 3:ROW_DMOD + 3 + 1, :] += _colsum(dh2)
        acc_ref[ROW_DMOD + 4:ROW_DMOD + 4 + 1, :] += _colsum(dh2 * (xn2 * gn2v))
        acc_ref[ROW_GN2:ROW_GN2 + 1, :] += _colsum(dh2 * xn2 * (1.0 + sc2))
        dxn2 = dh2 * gn2v * (1.0 + sc2)
        dx1_ref[...] = dx2 + r2 * (dxn2 - xn2 * jnp.mean(dxn2 * xn2, axis=-1, keepdims=True))

    row = lambda w: pl.BlockSpec((ts, w), lambda i: (i, 0))
    full = lambda a: pl.BlockSpec(a.shape, lambda i: (0, 0))
    sd = lambda w, dt: jax.ShapeDtypeStruct((s_len, w), dt)
    outs = [sd(d, BF16), sd(f, BF16), sd(d, BF16), sd(2 * f, BF16), sd(d, F32), jax.ShapeDtypeStruct((N_SMALL_ROWS, d), F32)]
    return _pallas(
        body, name="ffn", grid=(n_t,), out_shape=tuple(outs),
        in_specs=[row(d), row(d), full(mod), full(gn2), full(gf), ANY, ANY],
        out_specs=tuple([row(o.shape[1]) for o in outs[:-1]] + [pl.BlockSpec((N_SMALL_ROWS, d), lambda i: (0, 0))]),
        scratch_shapes=[pltpu.VMEM(w_fi.shape, BF16), pltpu.VMEM(w_fo.shape, BF16), pltpu.SemaphoreType.DMA((2,))],
        compiler_params=_params("arbitrary"),
    )(x1, target, mod, gn2, gf, w_fi, w_fo)


def _bwd_mix(dx1, proj, nrm, rstd, c3, ya, yb, mix, mod, w3, w31, lng, lnb, w_a, w_b, w_o, ready, ready_axes, ts):
    s_len, d = dx1.shape
    n_t = s_len // ts
    n_r = len(ready)
    ready_shapes = [g.shape for g in ready]

    def body(*refs):
        (dx1_ref, proj_ref, n_ref, rstd_ref, c3_ref, ya_ref, yb_ref, mix_ref, mod_ref, w3_ref, w31_ref, lng_ref,
         lnb_ref, wa_hbm, wb_hbm, wo_hbm) = refs[:16]
        ready_refs = refs[16:16 + n_r]
        dproj_ref, dmix_ref, dya_ref, dyb_ref, acc_ref = refs[16 + n_r:21 + n_r]
        recv_refs = refs[21 + n_r:21 + 2 * n_r]
        wa_v, wb_v, wo_v, dc3_ext, du1_ext, e_buf, u0_buf, du0_buf, dw8, sems, e_send, e_recv = refs[21 + 2 * n_r:]
        step = pl.program_id(0)
        pl.when(step == 0)(lambda: _exchange_send(ready_refs, recv_refs, ready_shapes, ready_axes, e_send, e_recv))
        pl.when(step == n_t - 1)(lambda: _exchange_finish(recv_refs, e_send, e_recv))
        _load_weights([(wa_hbm, wa_v), (wb_hbm, wb_v), (wo_hbm, wo_v)], sems)

        @pl.when(pl.program_id(0) == 0)
        def _():
            acc_ref[...] = jnp.zeros(acc_ref.shape, F32)
            dc3_ext[ts:ts + HALO_SHORT, :] = jnp.zeros((HALO_SHORT, d), F32)
            du1_ext[ts:ts + HALO_CONF, :] = jnp.zeros((HALO_CONF, d), F32)
            dw8[...] = jnp.zeros(dw8.shape, F32)

        pj = lambda k: proj_ref[:, k * d:(k + 1) * d].astype(F32)
        g1 = mod_ref[:, 2 * d:3 * d]
        dx1v = dx1_ref[...]
        acc_ref[ROW_DMOD + 2:ROW_DMOD + 2 + 1, :] += _colsum(dx1v * mix_ref[...])
        dmix = (dx1v * g1).astype(BF16)
        dmix_ref[...] = dmix
        dmerged = _dot_nt(dmix, wo_v[...])
        g_a, g_b = _sig(pj(5)), _sig(pj(6))
        y_a, y_b = ya_ref[...].astype(F32), yb_ref[...].astype(F32)
        dproj_ref[:, 5 * d:6 * d] = (dmerged * y_a * g_a * (1.0 - g_a)).astype(BF16)
        dproj_ref[:, 6 * d:7 * d] = (dmerged * y_b * g_b * (1.0 - g_b)).astype(BF16)
        dya = (dmerged * g_a).astype(BF16)
        dyb = (dmerged * g_b).astype(BF16)
        dya_ref[...] = dya
        dyb_ref[...] = dyb

        dyap = _dot_nt(dya, wa_v[...])
        dproj_ref[:, 0:d] = (dyap * c3_ref[...].astype(F32)).astype(BF16)
        dc3_ext[0:ts, :] = dyap * pj(0)
        c_s, v_s = pj(1), pj(2)
        cv = c_s * v_s
        dcv = jnp.zeros((ts, d), F32)
        for j in range(SHORT_K):
            k = SHORT_K - 1 - j
            sj = dc3_ext[j:j + ts, :]
            dcv = dcv + w3_ref[k:k + 1, :] * sj
            acc_ref[ROW_W3 + k:ROW_W3 + k + 1, :] += _colsum(cv * sj)
        dc3_ext[ts:ts + HALO_SHORT, :] = dc3_ext[0:HALO_SHORT, :]
        dproj_ref[:, d:2 * d] = (dcv * v_s).astype(BF16)
        dproj_ref[:, 2 * d:3 * d] = (dcv * c_s).astype(BF16)

        du3 = _dot_nt(dyb, wb_v[...])
        nv = n_ref[...].astype(F32)
        lg = lng_ref[...]
        u2 = nv * lg + lnb_ref[...]
        s2 = _sig(u2)
        du2 = du3 * (s2 * (1.0 + u2 * (1.0 - s2)))
        acc_ref[ROW_LNG:ROW_LNG + 1, :] += _colsum(du2 * nv)
        acc_ref[ROW_LNB:ROW_LNB + 1, :] += _colsum(du2)
        dn = du2 * lg
        du1 = rstd_ref[...] * (dn - jnp.mean(dn, axis=-1, keepdims=True) - nv * jnp.mean(dn * nv, axis=-1, keepdims=True))
        acc_ref[ROW_CB:ROW_CB + 1, :] += _colsum(du1)
        du1_ext[0:ts, :] = du1
        v_c = pj(3)
        sg = _sig(pj(4))
        u0_buf[...] = v_c * sg

        def conv_chunk(tap, r0, lo):
            lanes = slice(lo, lo + CONV_LANES)
            u0c = u0_buf[pl.ds(r0, CONV_ROWS), lanes]
            acc = jnp.zeros((CONV_ROWS, CONV_LANES), F32)
            for j in range(CONF_K):
                k = CONF_K - 1 - j
                sj = tap(j)
                acc = acc + w31_ref[k:k + 1, lanes] * sj
                prod = u0c * sj
                part = prod[0:8]
                for i in range(1, CONV_ROWS // 8):
                    part = part + prod[8 * i:8 * i + 8]
                dw8[k, :, lanes] += part
            du0_buf[pl.ds(r0, CONV_ROWS), lanes] = acc

        _conv_chunks(du1_ext, e_buf, ts, d, conv_chunk, unroll=False)
        du0 = du0_buf[...]
        du1_ext[ts:ts + HALO_CONF, :] = du1_ext[0:HALO_CONF, :]

        @pl.when(step == n_t - 1)
        def _():
            for k in range(CONF_K):
                acc_ref[ROW_W31 + k:ROW_W31 + k + 1, :] = _colsum(dw8[k])
        dproj_ref[:, 3 * d:4 * d] = (du0 * sg).astype(BF16)
        dproj_ref[:, 4 * d:5 * d] = (du0 * v_c * sg * (1.0 - sg)).astype(BF16)

    rev = lambda w: pl.BlockSpec((ts, w), lambda i: (n_t - 1 - i, 0))
    full = lambda a: pl.BlockSpec(a.shape, lambda i: (0, 0))
    sd = lambda w, dt: jax.ShapeDtypeStruct((s_len, w), dt)
    outs = [sd(7 * d, BF16), sd(d, BF16), sd(d, BF16), sd(d, BF16), jax.ShapeDtypeStruct((N_SMALL_ROWS, d), F32)]
    res = _pallas(
        body, name="bwd_mix", grid=(n_t,), out_shape=tuple(outs) + _exchange_out_shapes(ready, ready_axes),
        in_specs=[rev(d), rev(7 * d), rev(d), rev(1), rev(d), rev(d), rev(d), rev(d), full(mod), full(w3), full(w31),
                  full(lng), full(lnb), ANY, ANY, ANY] + [ANY] * n_r,
        out_specs=tuple([rev(o.shape[1]) for o in outs[:-1]] + [pl.BlockSpec((N_SMALL_ROWS, d), lambda i: (0, 0))] + [ANY] * n_r),
        scratch_shapes=[pltpu.VMEM(w_a.shape, BF16), pltpu.VMEM(w_b.shape, BF16), pltpu.VMEM(w_o.shape, BF16),
                        pltpu.VMEM((ts + HALO_SHORT, d), F32), pltpu.VMEM((ts + HALO_CONF, d), F32),
                        pltpu.VMEM((7, ts + HALO_CONF - 8, d // 2), F32), pltpu.VMEM((ts, d), F32), pltpu.VMEM((ts, d), F32),
                        pltpu.VMEM((CONF_K, 8, d), F32), pltpu.SemaphoreType.DMA((3,))] + _exchange_sems(n_r),
        compiler_params=_params("arbitrary"),
    )(dx1, proj, nrm, rstd, c3, ya, yb, mix, mod, w3, w31, lng, lnb, w_a, w_b, w_o, *ready)
    return res[:len(outs)], res[len(outs):]


def _bwd_in(dproj, x, dx1, mod, gn1, w_in, ready, ready_axes, ts):
    s_len, d = x.shape
    n_t = s_len // ts
    n_r = len(ready)
    ready_shapes = [g.shape for g in ready]

    def body(*refs):
        dproj_ref, x_ref, dx1_ref, mod_ref, gn1_ref, win_hbm = refs[:6]
        ready_refs = refs[6:6 + n_r]
        gx_ref, acc_ref = refs[6 + n_r:8 + n_r]
        recv_refs = refs[8 + n_r:8 + 2 * n_r]
        win_v, sems, e_send, e_recv = refs[8 + 2 * n_r:]
        step = pl.program_id(0)
        pl.when(step == 0)(lambda: _exchange_send(ready_refs, recv_refs, ready_shapes, ready_axes, e_send, e_recv))
        pl.when(step == n_t - 1)(lambda: _exchange_finish(recv_refs, e_send, e_recv))
        _load_weights([(win_hbm, win_v)], sems)

        @pl.when(pl.program_id(0) == 0)
        def _():
            acc_ref[...] = jnp.zeros(acc_ref.shape, F32)

        xv = x_ref[...]
        sc1 = mod_ref[:, d:2 * d]
        gn1v = gn1_ref[...]
        r = lax.rsqrt(jnp.mean(xv * xv, axis=-1, keepdims=True) + EPS)
        xn = xv * r
        dh = _dot_nt(dproj_ref[...], win_v[...])
        acc_ref[ROW_DMOD:ROW_DMOD + 1, :] += _colsum(dh)
        acc_ref[ROW_DMOD + 1:ROW_DMOD + 1 + 1, :] += _colsum(dh * (xn * gn1v))
        acc_ref[ROW_GN1:ROW_GN1 + 1, :] += _colsum(dh * xn * (1.0 + sc1))
        dxn = dh * gn1v * (1.0 + sc1)
        gx_ref[...] = dx1_ref[...] + r * (dxn - xn * jnp.mean(dxn * xn, axis=-1, keepdims=True))

    row = lambda w: pl.BlockSpec((ts, w), lambda i: (i, 0))
    full = lambda a: pl.BlockSpec(a.shape, lambda i: (0, 0))
    res = _pallas(
        body, name="bwd_in", grid=(n_t,),
        out_shape=(jax.ShapeDtypeStruct((s_len, d), F32), jax.ShapeDtypeStruct((N_SMALL_ROWS, d), F32))
        + _exchange_out_shapes(ready, ready_axes),
        in_specs=[row(7 * d), row(d), row(d), full(mod), full(gn1), ANY] + [ANY] * n_r,
        out_specs=(row(d), pl.BlockSpec((N_SMALL_ROWS, d), lambda i: (0, 0))) + tuple([ANY] * n_r),
        scratch_shapes=[pltpu.VMEM(w_in.shape, BF16), pltpu.SemaphoreType.DMA((1,))] + _exchange_sems(n_r),
        compiler_params=_params("arbitrary"),
    )(dproj, x, dx1, mod, gn1, w_in, *ready)
    return res[0], res[1], res[2:]


def _weight_grad(a, b, bn, name, transposed=False, ready=(), ready_axes=()):
    s_len, m = a.shape
    n = b.shape[1]
    n_j = n // bn
    n_r = len(ready)
    ready_shapes = [g.shape for g in ready]

    def body(*refs):
        a_hbm, b_ref = refs[:2]
        ready_refs = refs[2:2 + n_r]
        o_ref = refs[2 + n_r]
        recv_refs = refs[3 + n_r:3 + 2 * n_r]
        a_v, sem = refs[3 + 2 * n_r:5 + 2 * n_r]
        step = pl.program_id(0)
        if n_r:
            e_send, e_recv = refs[5 + 2 * n_r:]
            pl.when(step == 0)(lambda: _exchange_send(ready_refs, recv_refs, ready_shapes, ready_axes, e_send, e_recv))
            pl.when(step == n_j - 1)(lambda: _exchange_finish(recv_refs, e_send, e_recv))
        _load_weights([(a_hbm, a_v)], sem)
        prod = _dot_tn(a_v[...], b_ref[...])
        o_ref[...] = (prod.T if transposed else prod).astype(BF16)

    out = jax.ShapeDtypeStruct((n, m) if transposed else (m, n), BF16)
    out_spec = pl.BlockSpec((bn, m), lambda j: (j, 0)) if transposed else pl.BlockSpec((m, bn), lambda j: (0, j))
    res = _pallas(
        body, name=name, grid=(n_j,), out_shape=(out,) + _exchange_out_shapes(ready, ready_axes),
        in_specs=[ANY, pl.BlockSpec((s_len, bn), lambda j: (0, j))] + [ANY] * n_r,
        out_specs=(out_spec,) + tuple([ANY] * n_r),
        scratch_shapes=[pltpu.VMEM(a.shape, BF16), pltpu.SemaphoreType.DMA((1,))] + (_exchange_sems(n_r) if n_r else []),
        compiler_params=_params("arbitrary"),
    )(a, b, *ready)
    return res[0], res[1:]


def _sum_slots(recv, grad, axis, chip, core, name):
    n_s, r, c = recv.shape
    tr = 16
    for cand in (256, 128, 176, 64, 32, 16):
        if r % cand == 0:
            tr = cand
            break
    n_i = r // tr

    def body(s_ref, own_ref, r_ref, o_ref):
        tot = own_ref[...].astype(F32)
        for k in range(n_s):
            tot = tot + r_ref[k].astype(F32)
        o_ref[...] = tot

    if axis == 0:
        own_map = lambda i, s: ((2 * s[0] + s[1]) * n_i + i, 0)
    else:
        own_map = lambda i, s: (s[1] * n_i + i, s[0])
    grid_spec = pltpu.PrefetchScalarGridSpec(
        num_scalar_prefetch=1, grid=(n_i,),
        in_specs=[pl.BlockSpec((tr, c), own_map), pl.BlockSpec((n_s, tr, c), lambda i, s: (0, i, 0))],
        out_specs=pl.BlockSpec((tr, c), lambda i, s: (s[1] * n_i + i, 0)))
    return _pallas(body, name=name, grid_spec=grid_spec, out_shape=jax.ShapeDtypeStruct((2 * r, c), F32),
                   compiler_params=_params("arbitrary"))(jnp.stack([chip, core]).astype(jnp.int32), grad, recv)


def _adamw_math(w, g, m, v):
    m = ADAM_B1 * m + (1.0 - ADAM_B1) * g
    v = ADAM_B2 * v + (1.0 - ADAM_B2) * (g * g)
    m_hat = m / (1.0 - ADAM_B1 ** ADAM_STEP)
    v_hat = v / (1.0 - ADAM_B2 ** ADAM_STEP)
    delta = -ADAM_LR * (m_hat / (jnp.sqrt(v_hat) + ADAM_EPS) + ADAM_WD * w)
    return delta, m, v


def _adamw(w, g, m, v, name):
    r, c = w.shape
    tr = r
    for cand in (256, 128, 176, 64, 32, 16, 8):
        if r % cand == 0:
            tr = cand
            break

    def body(w_ref, g_ref, m_ref, v_ref, go_ref, d_ref, nm_ref, nv_ref):
        g = g_ref[...]
        go_ref[...] = g
        d_ref[...], nm_ref[...], nv_ref[...] = _adamw_math(w_ref[...], g, m_ref[...], v_ref[...])

    blk = pl.BlockSpec((tr, c), lambda i: (i, 0))
    return _pallas(body, name=name, grid=(r // tr,), out_shape=tuple(jax.ShapeDtypeStruct((r, c), F32) for _ in range(4)),
                   in_specs=[blk] * 4, out_specs=(blk,) * 4, compiler_params=_params("arbitrary"))(w, g, m, v)


_SMALL = [
    ("b_ada", ROW_DMOD, "mod"), ("norm_mix_g", ROW_GN1, "row"), ("conv_short_w", ROW_W3, "shard"),
    ("conv_conf_w", ROW_W31, "shard"), ("conv_conf_b", ROW_CB, "row"), ("conf_ln_g", ROW_LNG, "row"),
    ("conf_ln_b", ROW_LNB, "row"), ("norm_ffn_g", ROW_GN2, "row"), ("final_norm_g", ROW_GF, "row")]


def _small_update(p_all, chip_onehot, ws, ms, vs):
    n = len(_SMALL)
    d = p_all.shape[2]

    def body(*refs):
        p_ref, oh_ref = refs[:2]
        w_refs, m_refs, v_refs = refs[2:2 + n], refs[2 + n:2 + 2 * n], refs[2 + 2 * n:2 + 3 * n]
        loss_ref = refs[2 + 3 * n]
        g_out, d_out, m_out, v_out = [refs[3 + (3 + q) * n:3 + (4 + q) * n] for q in range(4)]
        g_all = p_ref[0]
        for s in range(1, N_DEV):
            g_all = g_all + p_ref[s]
        loss_ref[...] = jnp.zeros(loss_ref.shape, F32) + (0.5 / d) * jnp.sum(g_all[ROW_SQERR:ROW_SQERR + 1, :])

        def emit(i, idx, g):
            dl, nm, nv = _adamw_math(w_refs[i][idx], g, m_refs[i][idx], v_refs[i][idx])
            g_out[i][idx], d_out[i][idx], m_out[i][idx], v_out[i][idx] = g, dl, nm, nv

        everything = (slice(None), slice(None))
        for i, (_, row, kind) in enumerate(_SMALL):
            if kind == "mod":
                for k in range(N_MOD):
                    emit(i, (slice(0, 1), slice(k * d, (k + 1) * d)), g_all[row + k:row + k + 1, :])
            elif kind == "row":
                emit(i, everything, g_all[row:row + 1, :])
            else:
                taps, dq = w_refs[i].shape
                g = jnp.zeros((taps, dq), F32)
                for j in range(N_CHIP):
                    g = g + oh_ref[0:1, j:j + 1] * g_all[row:row + taps, j * dq:(j + 1) * dq]
                emit(i, everything, g)

    shapes = [jax.ShapeDtypeStruct(w.shape, F32) for w in ws]
    res = _pallas(body, name="small_update", out_shape=tuple([jax.ShapeDtypeStruct((8, 128), F32)] + shapes * 4),
                  in_specs=[VMEM_FULL] * (2 + 3 * n), out_specs=tuple([VMEM_FULL] * (1 + 4 * n)),
                  compiler_params=_params())(p_all, chip_onehot, *ws, *ms, *vs)
    return res[0], [res[1 + q * n:1 + (q + 1) * n] for q in range(4)]


def _w_ada_grad(c_t, dmod_part):
    def body(c_ref, g_ref, o_ref):
        cv = c_ref[...]
        o_ref[...] = _dot((cv * _sig(cv)).astype(BF16), g_ref[...].astype(BF16))

    return _pallas(body, name="w_ada_grad", out_shape=jax.ShapeDtypeStruct((c_t.shape[0], dmod_part.shape[1]), F32),
                   in_specs=[VMEM_FULL] * 2, out_specs=VMEM_FULL, compiler_params=_params())(c_t, dmod_part)


def _step(x, c, w_ada, b_ada, norm_mix_g, w_in, conv_short_w, w_short_out, conv_conf_w, conv_conf_b, conf_ln_g,
          conf_ln_b, w_conf_out, w_o, norm_ffn_g, w_ffn_in, w_ffn_out, final_norm_g, loss_target, moments, ts):
    xi, yi, ci = _place()
    chip = 2 * xi + yi
    me = 4 * xi + 2 * yi + ci
    x2d, tgt = x[0], loss_target[0]
    s_len, d = x2d.shape
    dq = d // N_CHIP
    mq = N_MOD * d // N_CHIP
    gf = final_norm_g.reshape(1, d)

    taps = jnp.concatenate([jnp.pad(conv_short_w[0], ((0, 8 - SHORT_K), (0, 0))),
                            jnp.pad(conv_conf_w[0], ((0, 32 - CONF_K), (0, 0)))], axis=0)
    n_tr = 40 * dq // d
    c_taps = _all_gather_small(jnp.concatenate([jnp.broadcast_to(c, (8, d)), taps.reshape(n_tr, d),
                                                jnp.zeros((16 - n_tr, d), F32)], axis=0), "gather_c_taps")
    c_taps = c_taps.reshape(N_DEV, 24, d)
    c_all = c_taps[:, 0]
    b_part = lax.dynamic_slice(b_ada, (0, chip * mq), (1, mq))
    mod_part = _mod_part(c_all, w_ada[0].astype(BF16), b_part)
    mod_all = _all_gather_small(mod_part, "gather_mod").reshape(N_CHIP, 2, N_DEV, mq)[:, 0]
    mod_full = jnp.transpose(mod_all, (1, 0, 2)).reshape(N_DEV, N_MOD * d)
    mod = lax.dynamic_slice(mod_full, (me, 0), (1, N_MOD * d))

    taps_all = c_taps[:, 8:8 + n_tr].reshape(N_CHIP, 2, 40, dq)[:, 0]
    taps_full = jnp.transpose(taps_all, (1, 0, 2)).reshape(40, d)
    w3, w31 = taps_full[0:SHORT_K], taps_full[8:8 + CONF_K]

    big = [w_in[0], w_short_out[0], w_conf_out[0], w_o[0], w_ffn_in[0], w_ffn_out[0]]
    axes = [1, 0, 0, 0, 1, 0]
    big_names = ["w_in", "w_short_out", "w_conf_out", "w_o", "w_ffn_in", "w_ffn_out"]
    placed = [_place_shard(w, ax, chip, "place_" + n) for w, ax, n in zip(big, axes, big_names)]
    win_f, wa_f, wb_f, wo_f = _gather_weights(placed[:4], axes[:4], "gather_weights")

    (proj, h, yap, c3, nrm, rstd, u3, ya, yb, merged, mix, x1), (wfi_f, wfo_f) = _fwd_mix(
        x2d, mod, norm_mix_g, w3, w31, conv_conf_b, conf_ln_g, conf_ln_b, win_f, wa_f, wb_f, wo_f, placed[4:], axes[4:], ts)
    h2, hid, df, dab, dx1, acc_f = _ffn(x1, tgt, mod, norm_ffn_g, gf, wfi_f, wfo_f, ts)
    f = wfo_f.shape[0]
    bn = min(512, d)
    g_ffn = [_weight_grad(h2, dab, bn, "grad_w_ffn_in")[0], _weight_grad(df, hid, bn // 2, "grad_w_ffn_out", transposed=True)[0]]
    (dproj, dmix, dya, dyb, acc_m), r_ffn = _bwd_mix(dx1, proj, nrm, rstd, c3, ya, yb, mix, mod, w3, w31, conf_ln_g,
                                                    conf_ln_b, wa_f, wb_f, wo_f, g_ffn, axes[4:], ts)
    g_sq = [_weight_grad(yap, dya, bn, "grad_w_short_out")[0], _weight_grad(u3, dyb, bn, "grad_w_conf_out")[0],
            _weight_grad(merged, dmix, bn, "grad_w_o")[0]]
    g_in, r_sq = _weight_grad(h, dproj, bn, "grad_w_in", ready=g_sq, ready_axes=axes[1:4])
    grad_x, acc_i, r_in = _bwd_in(dproj, x2d, dx1, mod, norm_mix_g, win_f, [g_in], axes[:1], min(2 * ts, s_len))
    g_in = [g_in]
    r_in_sq = list(r_in) + list(r_sq)
    grads = g_in + g_sq + g_ffn
    recv = list(r_in_sq) + list(r_ffn)
    g_big = _swap_halves([_sum_slots(r, g, ax, chip, ci, "sum_slots_" + n)
                          for r, g, ax, n in zip(recv, grads, axes, big_names)])

    p_all = _all_gather_small(acc_f + acc_m + acc_i, "gather_small_grads").reshape(N_DEV, N_SMALL_ROWS, d)
    small_w = dict(b_ada=b_ada, norm_mix_g=norm_mix_g, conv_short_w=conv_short_w, conv_conf_w=conv_conf_w,
                   conv_conf_b=conv_conf_b, conf_ln_g=conf_ln_g, conf_ln_b=conf_ln_b, norm_ffn_g=norm_ffn_g,
                   final_norm_g=final_norm_g)
    as_2d = lambda t: t.reshape((-1, t.shape[-1]))
    chip_onehot = (lax.broadcasted_iota(jnp.int32, (1, 128), 1) == chip).astype(F32)
    loss_tile, small_out = _small_update(
        p_all, chip_onehot, [as_2d(small_w[n]) for n, _, _ in _SMALL], [as_2d(moments["m_" + n]) for n, _, _ in _SMALL],
        [as_2d(moments["v_" + n]) for n, _, _ in _SMALL])
    small_g, small_d, small_m, small_v = [{n: o.reshape(small_w[n].shape) for (n, _, _), o in zip(_SMALL, outs)}
                                          for outs in small_out]

    dmod_all = p_all[:, 0:N_MOD].reshape(N_DEV, N_MOD * d)
    dmod_part = jnp.pad(lax.dynamic_slice(dmod_all, (0, chip * mq), (N_DEV, mq)), ((0, 128 - N_DEV), (0, 0)))
    c_t = jnp.pad(jnp.transpose(c_all), ((0, 0), (0, 128 - N_DEV)))
    g_ada = _w_ada_grad(c_t, dmod_part)

    out_g, out_d, out_m, out_v = dict(small_g), dict(small_d), dict(small_m), dict(small_v)
    for name, w, g in zip(["w_ada"] + big_names, [w_ada[0]] + big, [g_ada] + list(g_big)):
        go, dl, nm, nv = _adamw(w, g, moments["m_" + name][0], moments["v_" + name][0], "adamw_" + name)
        out_g[name], out_d[name], out_m[name], out_v[name] = go[None], dl[None], nm[None], nv[None]

    loss = loss_tile[0, 0]
    return loss, grad_x[None], out_g, out_d, out_m, out_v


_WEIGHTS = ["w_ada", "b_ada", "norm_mix_g", "w_in", "conv_short_w", "w_short_out", "conv_conf_w", "conv_conf_b",
            "conf_ln_g", "conf_ln_b", "w_conf_out", "w_o", "norm_ffn_g", "w_ffn_in", "w_ffn_out", "final_norm_g"]
ROW_TILE = 256


def kernel(x, c, w_ada, b_ada, norm_mix_g, w_in, conv_short_w, w_short_out, conv_conf_w, conv_conf_b, conf_ln_g, conf_ln_b, w_conf_out, w_o, norm_ffn_g, w_ffn_in, w_ffn_out, final_norm_g, loss_target, m_w_ada, m_b_ada, m_norm_mix_g, m_w_in, m_conv_short_w, m_w_short_out, m_conv_conf_w, m_conv_conf_b, m_conf_ln_g, m_conf_ln_b, m_w_conf_out, m_w_o, m_norm_ffn_g, m_w_ffn_in, m_w_ffn_out, m_final_norm_g, v_w_ada, v_b_ada, v_norm_mix_g, v_w_in, v_conv_short_w, v_w_short_out, v_conv_conf_w, v_conv_conf_b, v_conf_ln_g, v_conf_ln_b, v_w_conf_out, v_w_o, v_norm_ffn_g, v_w_ffn_in, v_w_ffn_out, v_final_norm_g):
    moments = dict(
        m_w_ada=m_w_ada, m_b_ada=m_b_ada, m_norm_mix_g=m_norm_mix_g, m_w_in=m_w_in, m_conv_short_w=m_conv_short_w,
        m_w_short_out=m_w_short_out, m_conv_conf_w=m_conv_conf_w, m_conv_conf_b=m_conv_conf_b, m_conf_ln_g=m_conf_ln_g,
        m_conf_ln_b=m_conf_ln_b, m_w_conf_out=m_w_conf_out, m_w_o=m_w_o, m_norm_ffn_g=m_norm_ffn_g, m_w_ffn_in=m_w_ffn_in,
        m_w_ffn_out=m_w_ffn_out, m_final_norm_g=m_final_norm_g,
        v_w_ada=v_w_ada, v_b_ada=v_b_ada, v_norm_mix_g=v_norm_mix_g, v_w_in=v_w_in, v_conv_short_w=v_conv_short_w,
        v_w_short_out=v_w_short_out, v_conv_conf_w=v_conv_conf_w, v_conv_conf_b=v_conv_conf_b, v_conf_ln_g=v_conf_ln_g,
        v_conf_ln_b=v_conf_ln_b, v_w_conf_out=v_w_conf_out, v_w_o=v_w_o, v_norm_ffn_g=v_norm_ffn_g, v_w_ffn_in=v_w_ffn_in,
        v_w_ffn_out=v_w_ffn_out, v_final_norm_g=v_final_norm_g)
    loss, grad_x, g, dl, nm, nv = _step(
        x, c, w_ada, b_ada, norm_mix_g, w_in, conv_short_w, w_short_out, conv_conf_w, conv_conf_b, conf_ln_g, conf_ln_b,
        w_conf_out, w_o, norm_ffn_g, w_ffn_in, w_ffn_out, final_norm_g, loss_target, moments, min(ROW_TILE, x.shape[1]))
    return (loss, grad_x, *[g[n] for n in _WEIGHTS], *[dl[n] for n in _WEIGHTS], *[nm[n] for n in _WEIGHTS],
            *[nv[n] for n in _WEIGHTS])
```

```python
import jax
import jax.numpy as jnp
from jax import lax
from jax.experimental import pallas as pl
from jax.experimental.pallas import tpu as pltpu

F32 = jnp.float32
BF16 = jnp.bfloat16
EPS = 1e-6
LN_EPS = 1e-5
SHORT_K = 3
CONF_K = 31
N_MOD = 6
N_DEV = 8
N_CHIP = 4
ADAM_LR = 0.001
ADAM_B1 = 0.9
ADAM_B2 = 0.999
ADAM_EPS = 1e-08
ADAM_WD = 0.01
ADAM_STEP = 10
MESH = pl.DeviceIdType.MESH
V7X_VMEM_LIMIT_BYTES = 60 * 1024 * 1024
HALO_SHORT = 8
HALO_CONF = 32
N_SMALL_ROWS = 48
ROW_DMOD, ROW_GN1, ROW_W3, ROW_W31, ROW_CB, ROW_LNG, ROW_LNB, ROW_GN2, ROW_GF, ROW_SQERR = 0, 6, 7, 10, 41, 42, 43, 44, 45, 46
CONV_ROWS = 64
CONV_LANES = 256

ANY = pl.BlockSpec(memory_space=pl.ANY)
VMEM_FULL = pl.BlockSpec(memory_space=pltpu.VMEM)


def _pallas(body, **kw):
    return pl.pallas_call(body, **kw)


def _params(*sem):
    return pltpu.CompilerParams(dimension_semantics=sem, vmem_limit_bytes=V7X_VMEM_LIMIT_BYTES)


def _sig(z):
    return jax.nn.sigmoid(z)


def _dot(a, b):
    return jnp.dot(a, b, preferred_element_type=F32)


def _dot_nt(a, b):
    return lax.dot_general(a, b, (((1,), (1,)), ((), ())), preferred_element_type=F32)


def _dot_tn(a, b):
    return lax.dot_general(a, b, (((0,), (0,)), ((), ())), preferred_element_type=F32)


def _colsum(z):
    return jnp.sum(z, axis=0, keepdims=True)


def _place():
    return lax.axis_index("x"), lax.axis_index("y"), lax.axis_index("c")


def _small_gather(x_ref, out_ref, send_sems, recv_sems, local_sem):
    m_per = x_ref.shape[0]
    x, y, c = _place()
    me, sibling = (x, y, c), (x, y, 1 - c)
    chips = [(1 - x, y), (x, 1 - y), (1 - x, 1 - y)]

    def rows(px, py, pc):
        return out_ref.at[pl.ds((4 * px + 2 * py + pc) * m_per, m_per), :]

    def copy(k, blk, to, src=None):
        return pltpu.make_async_remote_copy(
            src_ref=rows(*blk) if src is None else src, dst_ref=rows(*blk),
            send_sem=send_sems.at[k], recv_sem=recv_sems.at[k], device_id=to, device_id_type=MESH)

    mine = pltpu.make_async_copy(x_ref, rows(*me), local_sem)
    mine.start()
    first = [copy(0, me, sibling, src=x_ref)]
    first += [copy(1 + j, me, (*chip, c), src=x_ref) for j, chip in enumerate(chips)]
    for cp in first:
        cp.start()
    passed = [copy(4 + j, (*chip, c), sibling) for j, chip in enumerate(chips)]
    for j, chip in enumerate(chips):
        copy(1 + j, (*chip, c), me).wait_recv()
        passed[j].start()
    copy(0, sibling, me).wait_recv()
    for j, chip in enumerate(chips):
        copy(4 + j, (*chip, 1 - c), me).wait_recv()
    for cp in first + passed:
        cp.wait_send()
    mine.wait()


def _small_gather_sems():
    return [pltpu.SemaphoreType.DMA((7,)), pltpu.SemaphoreType.DMA((7,)), pltpu.SemaphoreType.DMA]


def _all_gather_small(block, name):
    m_per, n = block.shape
    def body(*refs):
        _small_gather(*refs)

    return _pallas(
        body, name=name, out_shape=jax.ShapeDtypeStruct((N_DEV * m_per, n), block.dtype),
        in_specs=[VMEM_FULL], out_specs=VMEM_FULL, scratch_shapes=_small_gather_sems(),
    )(block)


def _shard_slice(ref, full_shape, shard_axis, chip, half):
    m, n = full_shape
    if shard_axis == 0:
        r = m // N_CHIP
        return ref.at[pl.ds(chip * r + half * (r // 2), r // 2), :]
    cw = n // N_CHIP
    return ref.at[pl.ds(half * (m // 2), m // 2), pl.ds(pl.multiple_of(chip * cw, 128), cw)]


def _place_shard(w, axis, chip, name):
    r, cc = w.shape
    tr = r if axis == 0 else min(256, r)
    full = (r * N_CHIP, cc) if axis == 0 else (r, cc * N_CHIP)

    def body(s_ref, w_ref, o_ref):
        o_ref[...] = w_ref[...].astype(BF16)

    out_map = (lambda i, s: (s[0], 0)) if axis == 0 else (lambda i, s: (i, s[0]))
    grid_spec = pltpu.PrefetchScalarGridSpec(
        num_scalar_prefetch=1, grid=(r // tr,), in_specs=[pl.BlockSpec((tr, cc), lambda i, s: (i, 0))],
        out_specs=pl.BlockSpec((tr, cc), out_map))
    return _pallas(body, name=name, grid_spec=grid_spec, out_shape=jax.ShapeDtypeStruct(full, BF16),
                   compiler_params=_params("arbitrary"))(jnp.reshape(chip, (1,)).astype(jnp.int32), w)


def _other_chips():
    x, y, _ = _place()
    return [(1 - x, y), (x, 1 - y), (1 - x, 1 - y)]


def _gather_send(refs, shapes, axes, send_sems, recv_sems):
    x, y, c = _place()
    for a in range(len(refs)):
        mine = _shard_slice(refs[a], shapes[a], axes[a], 2 * x + y, c)
        for j, (px, py) in enumerate(_other_chips()):
            pltpu.make_async_remote_copy(src_ref=mine, dst_ref=mine, send_sem=send_sems.at[a, j], recv_sem=recv_sems.at[a, j],
                                         device_id=(px, py, c), device_id_type=MESH).start()


def _gather_forward(refs, shapes, axes, send_sems, recv_sems):
    x, y, c = _place()
    for a in range(len(refs)):
        for j, (px, py) in enumerate(_other_chips()):
            landed = _shard_slice(refs[a], shapes[a], axes[a], 2 * px + py, c)
            pltpu.make_async_remote_copy(src_ref=landed, dst_ref=landed, send_sem=send_sems.at[a, j], recv_sem=recv_sems.at[a, j],
                                         device_id=(px, py, c), device_id_type=MESH).wait_recv()
            pltpu.make_async_remote_copy(src_ref=landed, dst_ref=landed, send_sem=send_sems.at[a, 3 + j],
                                         recv_sem=recv_sems.at[a, 3 + j], device_id=(x, y, 1 - c), device_id_type=MESH).start()


def _gather_finish(refs, shapes, axes, send_sems, recv_sems):
    x, y, c = _place()
    for a in range(len(refs)):
        for j, (px, py) in enumerate(_other_chips()):
            other = _shard_slice(refs[a], shapes[a], axes[a], 2 * px + py, 1 - c)
            landed = _shard_slice(refs[a], shapes[a], axes[a], 2 * px + py, c)
            mine = _shard_slice(refs[a], shapes[a], axes[a], 2 * x + y, c)
            me = (x, y, c)
            pltpu.make_async_remote_copy(src_ref=other, dst_ref=other, send_sem=send_sems.at[a, 3 + j],
                                         recv_sem=recv_sems.at[a, 3 + j], device_id=me, device_id_type=MESH).wait_recv()
            pltpu.make_async_remote_copy(src_ref=mine, dst_ref=mine, send_sem=send_sems.at[a, j], recv_sem=recv_sems.at[a, j],
                                         device_id=me, device_id_type=MESH).wait_send()
            pltpu.make_async_remote_copy(src_ref=landed, dst_ref=landed, send_sem=send_sems.at[a, 3 + j],
                                         recv_sem=recv_sems.at[a, 3 + j], device_id=me, device_id_type=MESH).wait_send()


def _gather_sems(n_w):
    return [pltpu.SemaphoreType.DMA((n_w, 6)), pltpu.SemaphoreType.DMA((n_w, 6))]


def _gather_weights(fulls, axes, name):
    n_w = len(fulls)
    shapes = [f.shape for f in fulls]

    def body(*refs):
        outs, sems = refs[n_w:2 * n_w], refs[2 * n_w:]
        _gather_send(outs, shapes, axes, *sems)
        _gather_forward(outs, shapes, axes, *sems)
        _gather_finish(outs, shapes, axes, *sems)

    return _pallas(
        body, name=name, out_shape=tuple(jax.ShapeDtypeStruct(f.shape, BF16) for f in fulls),
        in_specs=[ANY] * n_w, out_specs=tuple([ANY] * n_w), input_output_aliases={a: a for a in range(n_w)},
        scratch_shapes=_gather_sems(n_w),
    )(*fulls)


def _piece_shapes(grads, axes):
    return [((g.shape[0] // N_CHIP // 2, g.shape[1]) if ax == 0 else (g.shape[0] // 2, g.shape[1] // N_CHIP))
            for g, ax in zip(grads, axes)]


def _exchange_send(ins, outs, shapes, axes, send_sems, recv_sems):
    x, y, c = _place()
    me = 4 * x + 2 * y + c
    for a in range(len(ins)):
        for t in range(N_DEV):
            tx, ty, tc = t // 4, (t // 2) % 2, t % 2
            src = _shard_slice(ins[a], shapes[a], axes[a], 2 * tx + ty, tc)

            @pl.when(me != t)
            def _(src=src, a=a, t=t, to=(tx, ty, tc)):
                slot = (me - t + N_DEV) % N_DEV - 1
                pltpu.make_async_remote_copy(src_ref=src, dst_ref=outs[a].at[slot], send_sem=send_sems.at[a, t],
                                             recv_sem=recv_sems.at[a, slot], device_id=to, device_id_type=MESH).start()


def _exchange_finish(outs, send_sems, recv_sems):
    x, y, c = _place()
    me = 4 * x + 2 * y + c
    for a in range(len(outs)):
        for t in range(N_DEV):
            @pl.when(me != t)
            def _(a=a, t=t):
                slot = outs[a].at[0]
                pltpu.make_async_remote_copy(src_ref=slot, dst_ref=slot, send_sem=send_sems.at[a, t], recv_sem=recv_sems.at[a, 0],
                                             device_id=(x, y, c), device_id_type=MESH).wait_send()
        for k in range(N_DEV - 1):
            slot = outs[a].at[k]
            pltpu.make_async_remote_copy(src_ref=slot, dst_ref=slot, send_sem=send_sems.at[a, 0], recv_sem=recv_sems.at[a, k],
                                         device_id=(x, y, c), device_id_type=MESH).wait_recv()


def _exchange_sems(n_w):
    return [pltpu.SemaphoreType.DMA((n_w, N_DEV)), pltpu.SemaphoreType.DMA((n_w, N_DEV - 1))]


def _exchange_out_shapes(grads, axes):
    return tuple(jax.ShapeDtypeStruct((N_DEV - 1,) + p, BF16) for p in _piece_shapes(grads, axes))


def _swap_halves(fulls, block):
    n_w = len(fulls)

    def body(*refs):
        block_ref = refs[n_w]
        outs, gathered_ref = refs[n_w + 1:2 * n_w + 1], refs[2 * n_w + 1]
        send_sems, recv_sems = refs[2 * n_w + 2:2 * n_w + 4]
        x, y, c = _place()
        sibling = (x, y, 1 - c)
        waits = []
        for a in range(n_w):
            r = fulls[a].shape[0] // 2
            mine = outs[a].at[pl.ds(c * r, r), :]
            theirs = outs[a].at[pl.ds((1 - c) * r, r), :]
            rc = pltpu.make_async_remote_copy(src_ref=mine, dst_ref=mine, send_sem=send_sems.at[a], recv_sem=recv_sems.at[a],
                                              device_id=sibling, device_id_type=MESH)
            rc.start()
            got = pltpu.make_async_remote_copy(src_ref=theirs, dst_ref=theirs, send_sem=send_sems.at[a],
                                               recv_sem=recv_sems.at[a], device_id=sibling, device_id_type=MESH)
            waits += [rc.wait_send, got.wait_recv]
        _small_gather(block_ref, gathered_ref, *refs[2 * n_w + 4:])
        for w in waits:
            w()

    res = _pallas(
        body, name="swap_halves",
        out_shape=tuple(jax.ShapeDtypeStruct(f.shape, f.dtype) for f in fulls)
        + (jax.ShapeDtypeStruct((N_DEV * block.shape[0], block.shape[1]), block.dtype),),
        in_specs=[ANY] * n_w + [VMEM_FULL], out_specs=tuple([ANY] * n_w + [VMEM_FULL]),
        input_output_aliases={a: a for a in range(n_w)},
        scratch_shapes=[pltpu.SemaphoreType.DMA((n_w,)), pltpu.SemaphoreType.DMA((n_w,))] + _small_gather_sems(),
    )(*fulls, block)
    return res[:n_w], res[n_w]


def _load_weights(pairs, sems):
    @pl.when(pl.program_id(0) == 0)
    def _():
        cps = [pltpu.make_async_copy(s, d, sems.at[k]) for k, (s, d) in enumerate(pairs)]
        for cp in cps:
            cp.start()
        for cp in cps:
            cp.wait()


def _conv_chunks(ext, e_buf, ts, d, chunk_fn, unroll, between=()):
    rows, dh = e_buf.shape[1], e_buf.shape[2]
    for hf in range(d // dh):
        for r in range(1, 8):
            e_buf[r - 1, :, :] = ext[r:r + rows, hf * dh:(hf + 1) * dh]
        for lq in range(dh // CONV_LANES):
            lo, el = hf * dh + lq * CONV_LANES, lq * CONV_LANES

            def body(ci, carry, lo=lo, el=el):
                r0 = pl.multiple_of(ci * CONV_ROWS, CONV_ROWS)

                def tap(off):
                    q, r = divmod(off, 8)
                    if r == 0:
                        return ext[pl.ds(r0 + 8 * q, CONV_ROWS), lo:lo + CONV_LANES]
                    return e_buf[r - 1, pl.ds(r0 + 8 * q, CONV_ROWS), el:el + CONV_LANES]

                chunk_fn(tap, r0, lo)
                return carry

            lax.fori_loop(0, ts // CONV_ROWS, body, 0, unroll=unroll)
            if between:
                between[0]()
                between = between[1:]
    for fn in between:
        fn()


def _mod_part(c_all, w_ada_bf, b_part):
    def body(c_ref, w_ref, b_ref, o_ref):
        cv = c_ref[...]
        o_ref[...] = _dot((cv * _sig(cv)).astype(BF16), w_ref[...]) + b_ref[...]

    return _pallas(body, name="mod_part", out_shape=jax.ShapeDtypeStruct((c_all.shape[0], w_ada_bf.shape[1]), F32),
                   in_specs=[VMEM_FULL] * 3, out_specs=VMEM_FULL, compiler_params=_params())(c_all, w_ada_bf, b_part)


def _fwd_mix(x, mod, gn1, w3, w31, cbias, lng, lnb, w_in, w_a, w_b, w_o, later, later_axes, ts):
    s_len, d = x.shape
    n_t = s_len // ts
    n_l = len(later)
    later_shapes = [w.shape for w in later]

    def body(*refs):
        (x_ref, mod_ref, gn1_ref, w3_ref, w31_ref, cb_ref, lng_ref, lnb_ref, win_hbm, wa_hbm, wb_hbm, wo_hbm) = refs[:12]
        (proj_ref, h_ref, yap_ref, c3_ref, n_ref, rstd_ref, u3_ref, ya_ref, yb_ref, mg_ref, mix_ref,
         x1_ref) = refs[12 + n_l:24 + n_l]
        later_refs = refs[24 + n_l:24 + 2 * n_l]
        win_v, wa_v, wb_v, wo_v, cv_ext, u0_ext, e_buf, u1_buf, sems, g_send, g_recv = refs[24 + 2 * n_l:]
        step = pl.program_id(0)
        pl.when(step == 0)(lambda: _gather_send(later_refs, later_shapes, later_axes, g_send, g_recv))
        pl.when(step == n_t // 2)(lambda: _gather_forward(later_refs, later_shapes, later_axes, g_send, g_recv))
        pl.when(step == n_t - 1)(lambda: _gather_finish(later_refs, later_shapes, later_axes, g_send, g_recv))
        _load_weights([(win_hbm, win_v), (wa_hbm, wa_v), (wb_hbm, wb_v), (wo_hbm, wo_v)], sems)

        @pl.when(pl.program_id(0) == 0)
        def _():
            cv_ext[0:HALO_SHORT, :] = jnp.zeros((HALO_SHORT, d), F32)
            u0_ext[0:HALO_CONF, :] = jnp.zeros((HALO_CONF, d), F32)

        xv = x_ref[...]
        sh1, sc1, g1 = mod_ref[:, 0:d], mod_ref[:, d:2 * d], mod_ref[:, 2 * d:3 * d]
        r = lax.rsqrt(jnp.mean(xv * xv, axis=-1, keepdims=True) + EPS)
        hb = (((xv * r) * gn1_ref[...]) * (1.0 + sc1) + sh1).astype(BF16)
        h_ref[...] = hb

        def pj(k):
            p = _dot(hb, win_v[:, k * d:(k + 1) * d])
            proj_ref[:, k * d:(k + 1) * d] = p.astype(BF16)
            return p

        v_c = pj(3)
        u0_ext[HALO_CONF:HALO_CONF + ts, :] = v_c * _sig(pj(4))

        def conv_chunk(tap, r0, lo):
            acc = jnp.zeros((CONV_ROWS, CONV_LANES), F32)
            for k in range(CONF_K):
                acc = acc + w31_ref[k:k + 1, lo:lo + CONV_LANES] * tap(HALO_CONF - (CONF_K - 1) + k)
            u1_buf[pl.ds(r0, CONV_ROWS), lo:lo + CONV_LANES] = acc

        pjs = {}
        _conv_chunks(u0_ext, e_buf, ts, d, conv_chunk, unroll=True,
                     between=[lambda k=k: pjs.__setitem__(k, pj(k)) for k in (1, 2, 0, 5, 6)])
        u0_ext[0:HALO_CONF, :] = u0_ext[ts:ts + HALO_CONF, :]

        cv_ext[HALO_SHORT:HALO_SHORT + ts, :] = pjs[1] * pjs[2]
        conv3 = jnp.zeros((ts, d), F32)
        for k in range(SHORT_K):
            off = HALO_SHORT - (SHORT_K - 1) + k
            conv3 = conv3 + w3_ref[k:k + 1, :] * cv_ext[off:off + ts, :]
        cv_ext[0:HALO_SHORT, :] = cv_ext[ts:ts + HALO_SHORT, :]
        c3_ref[...] = conv3.astype(BF16)
        yap = (pjs[0] * conv3).astype(BF16)
        yap_ref[...] = yap
        y_a = _dot(yap, wa_v[...])
        ya_ref[...] = y_a.astype(BF16)

        u1 = u1_buf[...] + cb_ref[...]
        mu = jnp.mean(u1, axis=-1, keepdims=True)
        var = jnp.mean(jnp.square(u1 - mu), axis=-1, keepdims=True)
        rstd = lax.rsqrt(var + LN_EPS)
        nrm = (u1 - mu) * rstd
        n_ref[...] = nrm.astype(BF16)
        rstd_ref[...] = rstd
        u2 = nrm * lng_ref[...] + lnb_ref[...]
        u3 = (u2 * _sig(u2)).astype(BF16)
        u3_ref[...] = u3
        y_b = _dot(u3, wb_v[...])
        yb_ref[...] = y_b.astype(BF16)

        merged = (_sig(pjs[5]) * y_a + _sig(pjs[6]) * y_b).astype(BF16)
        mg_ref[...] = merged
        mix = _dot(merged, wo_v[...])
        mix_ref[...] = mix
        x1_ref[...] = xv + g1 * mix

    row = lambda w, dt=None: pl.BlockSpec((ts, w), lambda i: (i, 0))
    full = lambda a: pl.BlockSpec(a.shape, lambda i: (0, 0))
    sd = lambda w, dt: jax.ShapeDtypeStruct((s_len, w), dt)
    outs = [sd(7 * d, BF16), sd(d, BF16), sd(d, BF16), sd(d, BF16), sd(d, BF16), sd(1, F32), sd(d, BF16), sd(d, BF16),
            sd(d, BF16), sd(d, BF16), sd(d, F32), sd(d, F32)]
    res = _pallas(
        body, name="fwd_mix", grid=(n_t,),
        out_shape=tuple(outs) + tuple(jax.ShapeDtypeStruct(w.shape, BF16) for w in later),
        in_specs=[row(d), full(mod), full(gn1), full(w3), full(w31), full(cbias), full(lng), full(lnb), ANY, ANY, ANY, ANY]
        + [ANY] * n_l,
        out_specs=tuple(row(o.shape[1]) for o in outs) + tuple([ANY] * n_l),
        input_output_aliases={12 + k: len(outs) + k for k in range(n_l)},
        scratch_shapes=[pltpu.VMEM(w_in.shape, BF16), pltpu.VMEM(w_a.shape, BF16), pltpu.VMEM(w_b.shape, BF16),
                        pltpu.VMEM(w_o.shape, BF16), pltpu.VMEM((ts + HALO_SHORT, d), F32),
                        pltpu.VMEM((ts + HALO_CONF, d), F32), pltpu.VMEM((7, ts + HALO_CONF - 8, d // 2), F32),
                        pltpu.VMEM((ts, d), F32), pltpu.SemaphoreType.DMA((4,))] + _gather_sems(n_l),
        compiler_params=_params("arbitrary"),
    )(x, mod, gn1, w3, w31, cbias, lng, lnb, w_in, w_a, w_b, w_o, *later)
    return res[:len(outs)], res[len(outs):]


def _ffn(x1, target, mod, gn2, gf, w_fi, w_fo, ts):
    s_len, d = x1.shape
    f = w_fo.shape[0]
    n_t = s_len // ts

    def body(x1_ref, t_ref, mod_ref, gn2_ref, gf_ref, wfi_hbm, wfo_hbm,
             h2_ref, hid_ref, df_ref, dab_ref, dx1_ref, acc_ref, wfi_v, wfo_v, sems):
        _load_weights([(wfi_hbm, wfi_v), (wfo_hbm, wfo_v)], sems)

        @pl.when(pl.program_id(0) == 0)
        def _():
            acc_ref[...] = jnp.zeros(acc_ref.shape, F32)

        x1v = x1_ref[...]
        sh2, sc2, g2 = mod_ref[:, 3 * d:4 * d], mod_ref[:, 4 * d:5 * d], mod_ref[:, 5 * d:6 * d]
        gn2v, gfv = gn2_ref[...], gf_ref[...]
        r2 = lax.rsqrt(jnp.mean(x1v * x1v, axis=-1, keepdims=True) + EPS)
        xn2 = x1v * r2
        h2 = ((xn2 * gn2v) * (1.0 + sc2) + sh2).astype(BF16)
        h2_ref[...] = h2
        a = _dot(h2, wfi_v[:, 0:f])
        bg = _dot(h2, wfi_v[:, f:2 * f])
        sa = _sig(a)
        silu_a = a * sa
        hid = (silu_a * bg).astype(BF16)
        hid_ref[...] = hid
        ffn = _dot(hid, wfo_v[...])
        x2 = x1v + g2 * ffn
        r3 = lax.rsqrt(jnp.mean(x2 * x2, axis=-1, keepdims=True) + EPS)
        xn3 = x2 * r3
        diff = xn3 * gfv - t_ref[...]
        dy = diff * (1.0 / d)
        dxn3 = dy * gfv
        dx2 = r3 * (dxn3 - xn3 * jnp.mean(dxn3 * xn3, axis=-1, keepdims=True))
        acc_ref[ROW_GF:ROW_GF + 1, :] += _colsum(dy * xn3)
        acc_ref[ROW_DMOD + 5:ROW_DMOD + 5 + 1, :] += _colsum(dx2 * ffn)
        acc_ref[ROW_SQERR:ROW_SQERR + 1, :] += _colsum(diff * diff)

        df = (dx2 * g2).astype(BF16)
        df_ref[...] = df
        dhid = _dot_nt(df, wfo_v[...])
        da = dhid * bg * (sa * (1.0 + a * (1.0 - sa)))
        dbg = dhid * silu_a
        dab_ref[:, 0:f] = da.astype(BF16)
        dab_ref[:, f:2 * f] = dbg.astype(BF16)
        dh2 = _dot_nt(da.astype(BF16), wfi_v[:, 0:f]) + _dot_nt(dbg.astype(BF16), wfi_v[:, f:2 * f])
        acc_ref[ROW_DMOD + 3:ROW_DMOD + 3 + 1, :] += _colsum(dh2)
        acc_ref[ROW_DMOD + 4:ROW_DMOD + 4 + 1, :] += _colsum(dh2 * (xn2 * gn2v))
        acc_ref[ROW_GN2:ROW_GN2 + 1, :] += _colsum(dh2 * xn2 * (1.0 + sc2))
        dxn2 = dh2 * gn2v * (1.0 + sc2)
        dx1_ref[...] = dx2 + r2 * (dxn2 - xn2 * jnp.mean(dxn2 * xn2, axis=-1, keepdims=True))

    row = lambda w: pl.BlockSpec((ts, w), lambda i: (i, 0))
    full = lambda a: pl.BlockSpec(a.shape, lambda i: (0, 0))
    sd = lambda w, dt: jax.ShapeDtypeStruct((s_len, w), dt)
    outs = [sd(d, BF16), sd(f, BF16), sd(d, BF16), sd(2 * f, BF16), sd(d, F32), jax.ShapeDtypeStruct((N_SMALL_ROWS, d), F32)]
    return _pallas(
        body, name="ffn", grid=(n_t,), out_shape=tuple(outs),
        in_specs=[row(d), row(d), full(mod), full(gn2), full(gf), ANY, ANY],
        out_specs=tuple([row(o.shape[1]) for o in outs[:-1]] + [pl.BlockSpec((N_SMALL_ROWS, d), lambda i: (0, 0))]),
        scratch_shapes=[pltpu.VMEM(w_fi.shape, BF16), pltpu.VMEM(w_fo.shape, BF16), pltpu.SemaphoreType.DMA((2,))],
        compiler_params=_params("arbitrary"),
    )(x1, target, mod, gn2, gf, w_fi, w_fo)


def _bwd_mix(dx1, proj, nrm, rstd, c3, ya, yb, mix, mod, w3, w31, lng, lnb, w_a, w_b, w_o, ready, ready_axes, ts):
    s_len, d = dx1.shape
    n_t = s_len // ts
    n_r = len(ready)
    ready_shapes = [g.shape for g in ready]

    def body(*refs):
        (dx1_ref, proj_ref, n_ref, rstd_ref, c3_ref, ya_ref, yb_ref, mix_ref, mod_ref, w3_ref, w31_ref, lng_ref,
         lnb_ref, wa_hbm, wb_hbm, wo_hbm) = refs[:16]
        ready_refs = refs[16:16 + n_r]
        dproj_ref, dmix_ref, dya_ref, dyb_ref, acc_ref = refs[16 + n_r:21 + n_r]
        recv_refs = refs[21 + n_r:21 + 2 * n_r]
        wa_v, wb_v, wo_v, dc3_ext, du1_ext, e_buf, u0_buf, du0_buf, dw8, sems, e_send, e_recv = refs[21 + 2 * n_r:]
        step = pl.program_id(0)
        pl.when(step == 0)(lambda: _exchange_send(ready_refs, recv_refs, ready_shapes, ready_axes, e_send, e_recv))
        pl.when(step == n_t - 1)(lambda: _exchange_finish(recv_refs, e_send, e_recv))
        _load_weights([(wa_hbm, wa_v), (wb_hbm, wb_v), (wo_hbm, wo_v)], sems)

        @pl.when(pl.program_id(0) == 0)
        def _():
            acc_ref[...] = jnp.zeros(acc_ref.shape, F32)
            dc3_ext[ts:ts + HALO_SHORT, :] = jnp.zeros((HALO_SHORT, d), F32)
            du1_ext[ts:ts + HALO_CONF, :] = jnp.zeros((HALO_CONF, d), F32)
            dw8[...] = jnp.zeros(dw8.shape, F32)

        pj = lambda k: proj_ref[:, k * d:(k + 1) * d].astype(F32)
        g1 = mod_ref[:, 2 * d:3 * d]
        dx1v = dx1_ref[...]
        acc_ref[ROW_DMOD + 2:ROW_DMOD + 2 + 1, :] += _colsum(dx1v * mix_ref[...])
        dmix = (dx1v * g1).astype(BF16)
        dmix_ref[...] = dmix
        dmerged = _dot_nt(dmix, wo_v[...])
        g_a, g_b = _sig(pj(5)), _sig(pj(6))
        y_a, y_b = ya_ref[...].astype(F32), yb_ref[...].astype(F32)
        dproj_ref[:, 5 * d:6 * d] = (dmerged * y_a * g_a * (1.0 - g_a)).astype(BF16)
        dproj_ref[:, 6 * d:7 * d] = (dmerged * y_b * g_b * (1.0 - g_b)).astype(BF16)
        dya = (dmerged * g_a).astype(BF16)
        dyb = (dmerged * g_b).astype(BF16)
        dya_ref[...] = dya
        dyb_ref[...] = dyb

        dyap = _dot_nt(dya, wa_v[...])
        dproj_ref[:, 0:d] = (dyap * c3_ref[...].astype(F32)).astype(BF16)
        dc3_ext[0:ts, :] = dyap * pj(0)
        c_s, v_s = pj(1), pj(2)
        cv = c_s * v_s
        dcv = jnp.zeros((ts, d), F32)
        for j in range(SHORT_K):
            k = SHORT_K - 1 - j
            sj = dc3_ext[j:j + ts, :]
            dcv = dcv + w3_ref[k:k + 1, :] * sj
            acc_ref[ROW_W3 + k:ROW_W3 + k + 1, :] += _colsum(cv * sj)
        dc3_ext[ts:ts + HALO_SHORT, :] = dc3_ext[0:HALO_SHORT, :]
        dproj_ref[:, d:2 * d] = (dcv * v_s).astype(BF16)
        dproj_ref[:, 2 * d:3 * d] = (dcv * c_s).astype(BF16)

        du3 = _dot_nt(dyb, wb_v[...])
        nv = n_ref[...].astype(F32)
        lg = lng_ref[...]
        u2 = nv * lg + lnb_ref[...]
        s2 = _sig(u2)
        du2 = du3 * (s2 * (1.0 + u2 * (1.0 - s2)))
        acc_ref[ROW_LNG:ROW_LNG + 1, :] += _colsum(du2 * nv)
        acc_ref[ROW_LNB:ROW_LNB + 1, :] += _colsum(du2)
        dn = du2 * lg
        du1 = rstd_ref[...] * (dn - jnp.mean(dn, axis=-1, keepdims=True) - nv * jnp.mean(dn * nv, axis=-1, keepdims=True))
        acc_ref[ROW_CB:ROW_CB + 1, :] += _colsum(du1)
        du1_ext[0:ts, :] = du1
        v_c = pj(3)
        sg = _sig(pj(4))
        u0_buf[...] = v_c * sg

        def conv_chunk(tap, r0, lo):
            lanes = slice(lo, lo + CONV_LANES)
            u0c = u0_buf[pl.ds(r0, CONV_ROWS), lanes]
            acc = jnp.zeros((CONV_ROWS, CONV_LANES), F32)
            for j in range(CONF_K):
                k = CONF_K - 1 - j
                sj = tap(j)
                acc = acc + w31_ref[k:k + 1, lanes] * sj
                prod = u0c * sj
                part = prod[0:8]
                for i in range(1, CONV_ROWS // 8):
                    part = part + prod[8 * i:8 * i + 8]
                dw8[k, :, lanes] += part
            du0_buf[pl.ds(r0, CONV_ROWS), lanes] = acc

        _conv_chunks(du1_ext, e_buf, ts, d, conv_chunk, unroll=False)
        du0 = du0_buf[...]
        du1_ext[ts:ts + HALO_CONF, :] = du1_ext[0:HALO_CONF, :]

        @pl.when(step == n_t - 1)
        def _():
            for k in range(CONF_K):
                acc_ref[ROW_W31 + k:ROW_W31 + k + 1, :] = _colsum(dw8[k])
        dproj_ref[:, 3 * d:4 * d] = (du0 * sg).astype(BF16)
        dproj_ref[:, 4 * d:5 * d] = (du0 * v_c * sg * (1.0 - sg)).astype(BF16)

    rev = lambda w: pl.BlockSpec((ts, w), lambda i: (n_t - 1 - i, 0))
    full = lambda a: pl.BlockSpec(a.shape, lambda i: (0, 0))
    sd = lambda w, dt: jax.ShapeDtypeStruct((s_len, w), dt)
    outs = [sd(7 * d, BF16), sd(d, BF16), sd(d, BF16), sd(d, BF16), jax.ShapeDtypeStruct((N_SMALL_ROWS, d), F32)]
    res = _pallas(
        body, name="bwd_mix", grid=(n_t,), out_shape=tuple(outs) + _exchange_out_shapes(ready, ready_axes),
        in_specs=[rev(d), rev(7 * d), rev(d), rev(1), rev(d), rev(d), rev(d), rev(d), full(mod), full(w3), full(w31),
                  full(lng), full(lnb), ANY, ANY, ANY] + [ANY] * n_r,
        out_specs=tuple([rev(o.shape[1]) for o in outs[:-1]] + [pl.BlockSpec((N_SMALL_ROWS, d), lambda i: (0, 0))] + [ANY] * n_r),
        scratch_shapes=[pltpu.VMEM(w_a.shape, BF16), pltpu.VMEM(w_b.shape, BF16), pltpu.VMEM(w_o.shape, BF16),
                        pltpu.VMEM((ts + HALO_SHORT, d), F32), pltpu.VMEM((ts + HALO_CONF, d), F32),
                        pltpu.VMEM((7, ts + HALO_CONF - 8, d // 2), F32), pltpu.VMEM((ts, d), F32), pltpu.VMEM((ts, d), F32),
                        pltpu.VMEM((CONF_K, 8, d), F32), pltpu.SemaphoreType.DMA((3,))] + _exchange_sems(n_r),
        compiler_params=_params("arbitrary"),
    )(dx1, proj, nrm, rstd, c3, ya, yb, mix, mod, w3, w31, lng, lnb, w_a, w_b, w_o, *ready)
    return res[:len(outs)], res[len(outs):]


def _bwd_in(dproj, x, dx1, mod, gn1, w_in, ready, ready_axes, ts):
    s_len, d = x.shape
    n_t = s_len // ts
    n_r = len(ready)
    ready_shapes = [g.shape for g in ready]

    def body(*refs):
        dproj_ref, x_ref, dx1_ref, mod_ref, gn1_ref, win_hbm = refs[:6]
        ready_refs = refs[6:6 + n_r]
        gx_ref, acc_ref = refs[6 + n_r:8 + n_r]
        recv_refs = refs[8 + n_r:8 + 2 * n_r]
        win_v, sems, e_send, e_recv = refs[8 + 2 * n_r:]
        step = pl.program_id(0)
        pl.when(step == 0)(lambda: _exchange_send(ready_refs, recv_refs, ready_shapes, ready_axes, e_send, e_recv))
        pl.when(step == n_t - 1)(lambda: _exchange_finish(recv_refs, e_send, e_recv))
        _load_weights([(win_hbm, win_v)], sems)

        @pl.when(pl.program_id(0) == 0)
        def _():
            acc_ref[...] = jnp.zeros(acc_ref.shape, F32)

        xv = x_ref[...]
        sc1 = mod_ref[:, d:2 * d]
        gn1v = gn1_ref[...]
        r = lax.rsqrt(jnp.mean(xv * xv, axis=-1, keepdims=True) + EPS)
        xn = xv * r
        dh = _dot_nt(dproj_ref[...], win_v[...])
        acc_ref[ROW_DMOD:ROW_DMOD + 1, :] += _colsum(dh)
        acc_ref[ROW_DMOD + 1:ROW_DMOD + 1 + 1, :] += _colsum(dh * (xn * gn1v))
        acc_ref[ROW_GN1:ROW_GN1 + 1, :] += _colsum(dh * xn * (1.0 + sc1))
        dxn = dh * gn1v * (1.0 + sc1)
        gx_ref[...] = dx1_ref[...] + r * (dxn - xn * jnp.mean(dxn * xn, axis=-1, keepdims=True))

    row = lambda w: pl.BlockSpec((ts, w), lambda i: (i, 0))
    full = lambda a: pl.BlockSpec(a.shape, lambda i: (0, 0))
    res = _pallas(
        body, name="bwd_in", grid=(n_t,),
        out_shape=(jax.ShapeDtypeStruct((s_len, d), F32), jax.ShapeDtypeStruct((N_SMALL_ROWS, d), F32))
        + _exchange_out_shapes(ready, ready_axes),
        in_specs=[row(7 * d), row(d), row(d), full(mod), full(gn1), ANY] + [ANY] * n_r,
        out_specs=(row(d), pl.BlockSpec((N_SMALL_ROWS, d), lambda i: (0, 0))) + tuple([ANY] * n_r),
        scratch_shapes=[pltpu.VMEM(w_in.shape, BF16), pltpu.SemaphoreType.DMA((1,))] + _exchange_sems(n_r),
        compiler_params=_params("arbitrary"),
    )(dproj, x, dx1, mod, gn1, w_in, *ready)
    return res[0], res[1], res[2:]


def _weight_grad(a, b, bn, name, transposed=False, ready=(), ready_axes=()):
    s_len, m = a.shape
    n = b.shape[1]
    n_j = n // bn
    n_r = len(ready)
    ready_shapes = [g.shape for g in ready]

    def body(*refs):
        a_hbm, b_ref = refs[:2]
        ready_refs = refs[2:2 + n_r]
        o_ref = refs[2 + n_r]
        recv_refs = refs[3 + n_r:3 + 2 * n_r]
        a_v, sem = refs[3 + 2 * n_r:5 + 2 * n_r]
        step = pl.program_id(0)
        if n_r:
            e_send, e_recv = refs[5 + 2 * n_r:]
            pl.when(step == 0)(lambda: _exchange_send(ready_refs, recv_refs, ready_shapes, ready_axes, e_send, e_recv))
            pl.when(step == n_j - 1)(lambda: _exchange_finish(recv_refs, e_send, e_recv))
        _load_weights([(a_hbm, a_v)], sem)
        prod = _dot_tn(a_v[...], b_ref[...])
        o_ref[...] = (prod.T if transposed else prod).astype(BF16)

    out = jax.ShapeDtypeStruct((n, m) if transposed else (m, n), BF16)
    out_spec = pl.BlockSpec((bn, m), lambda j: (j, 0)) if transposed else pl.BlockSpec((m, bn), lambda j: (0, j))
    res = _pallas(
        body, name=name, grid=(n_j,), out_shape=(out,) + _exchange_out_shapes(ready, ready_axes),
        in_specs=[ANY, pl.BlockSpec((s_len, bn), lambda j: (0, j))] + [ANY] * n_r,
        out_specs=(out_spec,) + tuple([ANY] * n_r),
        scratch_shapes=[pltpu.VMEM(a.shape, BF16), pltpu.SemaphoreType.DMA((1,))] + (_exchange_sems(n_r) if n_r else []),
        compiler_params=_params("arbitrary"),
    )(a, b, *ready)
    return res[0], res[1:]


def _sum_slots(recv, grad, axis, chip, core, name):
    n_s, r, c = recv.shape
    tr = 16
    for cand in (256, 128, 176, 64, 32, 16):
        if r % cand == 0:
            tr = cand
            break
    n_i = r // tr

    def body(s_ref, own_ref, r_ref, o_ref):
        tot = own_ref[...].astype(F32)
        for k in range(n_s):
            tot = tot + r_ref[k].astype(F32)
        o_ref[...] = tot

    if axis == 0:
        own_map = lambda i, s: ((2 * s[0] + s[1]) * n_i + i, 0)
    else:
        own_map = lambda i, s: (s[1] * n_i + i, s[0])
    grid_spec = pltpu.PrefetchScalarGridSpec(
        num_scalar_prefetch=1, grid=(n_i,),
        in_specs=[pl.BlockSpec((tr, c), own_map), pl.BlockSpec((n_s, tr, c), lambda i, s: (0, i, 0))],
        out_specs=pl.BlockSpec((tr, c), lambda i, s: (s[1] * n_i + i, 0)))
    return _pallas(body, name=name, grid_spec=grid_spec, out_shape=jax.ShapeDtypeStruct((2 * r, c), F32),
                   compiler_params=_params("arbitrary"))(jnp.stack([chip, core]).astype(jnp.int32), grad, recv)


def _adamw_math(w, g, m, v):
    m = ADAM_B1 * m + (1.0 - ADAM_B1) * g
    v = ADAM_B2 * v + (1.0 - ADAM_B2) * (g * g)
    m_hat = m / (1.0 - ADAM_B1 ** ADAM_STEP)
    v_hat = v / (1.0 - ADAM_B2 ** ADAM_STEP)
    delta = -ADAM_LR * (m_hat / (jnp.sqrt(v_hat) + ADAM_EPS) + ADAM_WD * w)
    return delta, m, v


def _adamw(w, g, m, v, name):
    r, c = w.shape
    tr = r
    for cand in (256, 128, 176, 64, 32, 16, 8):
        if r % cand == 0:
            tr = cand
            break

    def body(w_ref, g_ref, m_ref, v_ref, go_ref, d_ref, nm_ref, nv_ref):
        g = g_ref[...]
        go_ref[...] = g
        d_ref[...], nm_ref[...], nv_ref[...] = _adamw_math(w_ref[...], g, m_ref[...], v_ref[...])

    blk = pl.BlockSpec((tr, c), lambda i: (i, 0))
    return _pallas(body, name=name, grid=(r // tr,), out_shape=tuple(jax.ShapeDtypeStruct((r, c), F32) for _ in range(4)),
                   in_specs=[blk] * 4, out_specs=(blk,) * 4, compiler_params=_params("arbitrary"))(w, g, m, v)


_SMALL = [
    ("b_ada", ROW_DMOD, "mod"), ("norm_mix_g", ROW_GN1, "row"), ("conv_short_w", ROW_W3, "shard"),
    ("conv_conf_w", ROW_W31, "shard"), ("conv_conf_b", ROW_CB, "row"), ("conf_ln_g", ROW_LNG, "row"),
    ("conf_ln_b", ROW_LNB, "row"), ("norm_ffn_g", ROW_GN2, "row"), ("final_norm_g", ROW_GF, "row")]


def _small_update(p_all, chip_onehot, ws, ms, vs):
    n = len(_SMALL)
    d = p_all.shape[2]

    def body(*refs):
        p_ref, oh_ref = refs[:2]
        w_refs, m_refs, v_refs = refs[2:2 + n], refs[2 + n:2 + 2 * n], refs[2 + 2 * n:2 + 3 * n]
        loss_ref = refs[2 + 3 * n]
        g_out, d_out, m_out, v_out = [refs[3 + (3 + q) * n:3 + (4 + q) * n] for q in range(4)]
        g_all = p_ref[0]
        for s in range(1, N_DEV):
            g_all = g_all + p_ref[s]
        loss_ref[...] = jnp.zeros(loss_ref.shape, F32) + (0.5 / d) * jnp.sum(g_all[ROW_SQERR:ROW_SQERR + 1, :])

        def emit(i, idx, g):
            dl, nm, nv = _adamw_math(w_refs[i][idx], g, m_refs[i][idx], v_refs[i][idx])
            g_out[i][idx], d_out[i][idx], m_out[i][idx], v_out[i][idx] = g, dl, nm, nv

        everything = (slice(None), slice(None))
        for i, (_, row, kind) in enumerate(_SMALL):
            if kind == "mod":
                for k in range(N_MOD):
                    emit(i, (slice(0, 1), slice(k * d, (k + 1) * d)), g_all[row + k:row + k + 1, :])
            elif kind == "row":
                emit(i, everything, g_all[row:row + 1, :])
            else:
                taps, dq = w_refs[i].shape
                g = jnp.zeros((taps, dq), F32)
                for j in range(N_CHIP):
                    g = g + oh_ref[0:1, j:j + 1] * g_all[row:row + taps, j * dq:(j + 1) * dq]
                emit(i, everything, g)

    shapes = [jax.ShapeDtypeStruct(w.shape, F32) for w in ws]
    res = _pallas(body, name="small_update", out_shape=tuple([jax.ShapeDtypeStruct((8, 128), F32)] + shapes * 4),
                  in_specs=[VMEM_FULL] * (2 + 3 * n), out_specs=tuple([VMEM_FULL] * (1 + 4 * n)),
                  compiler_params=_params())(p_all, chip_onehot, *ws, *ms, *vs)
    return res[0], [res[1 + q * n:1 + (q + 1) * n] for q in range(4)]


def _w_ada_grad(c_t, dmod_part):
    def body(c_ref, g_ref, o_ref):
        cv = c_ref[...]
        o_ref[...] = _dot((cv * _sig(cv)).astype(BF16), g_ref[...].astype(BF16))

    return _pallas(body, name="w_ada_grad", out_shape=jax.ShapeDtypeStruct((c_t.shape[0], dmod_part.shape[1]), F32),
                   in_specs=[VMEM_FULL] * 2, out_specs=VMEM_FULL, compiler_params=_params())(c_t, dmod_part)


def _step(x, c, w_ada, b_ada, norm_mix_g, w_in, conv_short_w, w_short_out, conv_conf_w, conv_conf_b, conf_ln_g,
          conf_ln_b, w_conf_out, w_o, norm_ffn_g, w_ffn_in, w_ffn_out, final_norm_g, loss_target, moments, ts):
    xi, yi, ci = _place()
    chip = 2 * xi + yi
    me = 4 * xi + 2 * yi + ci
    x2d, tgt = x[0], loss_target[0]
    s_len, d = x2d.shape
    dq = d // N_CHIP
    mq = N_MOD * d // N_CHIP
    gf = final_norm_g.reshape(1, d)

    taps = jnp.concatenate([jnp.pad(conv_short_w[0], ((0, 8 - SHORT_K), (0, 0))),
                            jnp.pad(conv_conf_w[0], ((0, 32 - CONF_K), (0, 0)))], axis=0)
    n_tr = 40 * dq // d
    c_taps = _all_gather_small(jnp.concatenate([jnp.broadcast_to(c, (8, d)), taps.reshape(n_tr, d),
                                                jnp.zeros((16 - n_tr, d), F32)], axis=0), "gather_c_taps")
    c_taps = c_taps.reshape(N_DEV, 24, d)
    c_all = c_taps[:, 0]
    b_part = lax.dynamic_slice(b_ada, (0, chip * mq), (1, mq))
    mod_part = _mod_part(c_all, w_ada[0].astype(BF16), b_part)
    mod_all = _all_gather_small(mod_part, "gather_mod").reshape(N_CHIP, 2, N_DEV, mq)[:, 0]
    mod_full = jnp.transpose(mod_all, (1, 0, 2)).reshape(N_DEV, N_MOD * d)
    mod = lax.dynamic_slice(mod_full, (me, 0), (1, N_MOD * d))

    taps_all = c_taps[:, 8:8 + n_tr].reshape(N_CHIP, 2, 40, dq)[:, 0]
    taps_full = jnp.transpose(taps_all, (1, 0, 2)).reshape(40, d)
    w3, w31 = taps_full[0:SHORT_K], taps_full[8:8 + CONF_K]

    big = [w_in[0], w_short_out[0], w_conf_out[0], w_o[0], w_ffn_in[0], w_ffn_out[0]]
    axes = [1, 0, 0, 0, 1, 0]
    big_names = ["w_in", "w_short_out", "w_conf_out", "w_o", "w_ffn_in", "w_ffn_out"]
    placed = [_place_shard(w, ax, chip, "place_" + n) for w, ax, n in zip(big, axes, big_names)]
    win_f, wa_f, wb_f, wo_f = _gather_weights(placed[:4], axes[:4], "gather_weights")

    (proj, h, yap, c3, nrm, rstd, u3, ya, yb, merged, mix, x1), (wfi_f, wfo_f) = _fwd_mix(
        x2d, mod, norm_mix_g, w3, w31, conv_conf_b, conf_ln_g, conf_ln_b, win_f, wa_f, wb_f, wo_f, placed[4:], axes[4:], ts)
    h2, hid, df, dab, dx1, acc_f = _ffn(x1, tgt, mod, norm_ffn_g, gf, wfi_f, wfo_f, ts)
    f = wfo_f.shape[0]
    bn = min(512, d)
    g_ffn = [_weight_grad(h2, dab, bn, "grad_w_ffn_in")[0], _weight_grad(df, hid, bn // 2, "grad_w_ffn_out", transposed=True)[0]]
    (dproj, dmix, dya, dyb, acc_m), r_ffn = _bwd_mix(dx1, proj, nrm, rstd, c3, ya, yb, mix, mod, w3, w31, conf_ln_g,
                                                    conf_ln_b, wa_f, wb_f, wo_f, g_ffn, axes[4:], ts)
    g_sq = [_weight_grad(yap, dya, bn // 2, "grad_w_short_out")[0], _weight_grad(u3, dyb, bn // 2, "grad_w_conf_out")[0],
            _weight_grad(merged, dmix, bn // 2, "grad_w_o")[0]]
    g_in, r_sq = _weight_grad(h, dproj, bn, "grad_w_in", ready=g_sq, ready_axes=axes[1:4])
    grad_x, acc_i, r_in = _bwd_in(dproj, x2d, dx1, mod, norm_mix_g, win_f, [g_in], axes[:1], min(2 * ts, s_len))
    g_in = [g_in]
    r_in_sq = list(r_in) + list(r_sq)
    grads = g_in + g_sq + g_ffn
    recv = list(r_in_sq) + list(r_ffn)
    g_big, p_all = _swap_halves([_sum_slots(r, g, ax, chip, ci, "sum_slots_" + n)
                                 for r, g, ax, n in zip(recv, grads, axes, big_names)], acc_f + acc_m + acc_i)
    p_all = p_all.reshape(N_DEV, N_SMALL_ROWS, d)

    small_w = dict(b_ada=b_ada, norm_mix_g=norm_mix_g, conv_short_w=conv_short_w, conv_conf_w=conv_conf_w,
                   conv_conf_b=conv_conf_b, conf_ln_g=conf_ln_g, conf_ln_b=conf_ln_b, norm_ffn_g=norm_ffn_g,
                   final_norm_g=final_norm_g)
    as_2d = lambda t: t.reshape((-1, t.shape[-1]))
    chip_onehot = (lax.broadcasted_iota(jnp.int32, (1, 128), 1) == chip).astype(F32)
    loss_tile, small_out = _small_update(
        p_all, chip_onehot, [as_2d(small_w[n]) for n, _, _ in _SMALL], [as_2d(moments["m_" + n]) for n, _, _ in _SMALL],
        [as_2d(moments["v_" + n]) for n, _, _ in _SMALL])
    small_g, small_d, small_m, small_v = [{n: o.reshape(small_w[n].shape) for (n, _, _), o in zip(_SMALL, outs)}
                                          for outs in small_out]

    dmod_all = p_all[:, 0:N_MOD].reshape(N_DEV, N_MOD * d)
    dmod_part = jnp.pad(lax.dynamic_slice(dmod_all, (0, chip * mq), (N_DEV, mq)), ((0, 128 - N_DEV), (0, 0)))
    c_t = jnp.pad(jnp.transpose(c_all), ((0, 0), (0, 128 - N_DEV)))
    g_ada = _w_ada_grad(c_t, dmod_part)

    out_g, out_d, out_m, out_v = dict(small_g), dict(small_d), dict(small_m), dict(small_v)
    for name, w, g in zip(["w_ada"] + big_names, [w_ada[0]] + big, [g_ada] + list(g_big)):
        go, dl, nm, nv = _adamw(w, g, moments["m_" + name][0], moments["v_" + name][0], "adamw_" + name)
        out_g[name], out_d[name], out_m[name], out_v[name] = go[None], dl[None], nm[None], nv[None]

    loss = loss_tile[0, 0]
    return loss, grad_x[None], out_g, out_d, out_m, out_v


_WEIGHTS = ["w_ada", "b_ada", "norm_mix_g", "w_in", "conv_short_w", "w_short_out", "conv_conf_w", "conv_conf_b",
            "conf_ln_g", "conf_ln_b", "w_conf_out", "w_o", "norm_ffn_g", "w_ffn_in", "w_ffn_out", "final_norm_g"]
ROW_TILE = 256


def kernel(x, c, w_ada, b_ada, norm_mix_g, w_in, conv_short_w, w_short_out, conv_conf_w, conv_conf_b, conf_ln_g, conf_ln_b, w_conf_out, w_o, norm_ffn_g, w_ffn_in, w_ffn_out, final_norm_g, loss_target, m_w_ada, m_b_ada, m_norm_mix_g, m_w_in, m_conv_short_w, m_w_short_out, m_conv_conf_w, m_conv_conf_b, m_conf_ln_g, m_conf_ln_b, m_w_conf_out, m_w_o, m_norm_ffn_g, m_w_ffn_in, m_w_ffn_out, m_final_norm_g, v_w_ada, v_b_ada, v_norm_mix_g, v_w_in, v_conv_short_w, v_w_short_out, v_conv_conf_w, v_conv_conf_b, v_conf_ln_g, v_conf_ln_b, v_w_conf_out, v_w_o, v_norm_ffn_g, v_w_ffn_in, v_w_ffn_out, v_final_norm_g):
    moments = dict(
        m_w_ada=m_w_ada, m_b_ada=m_b_ada, m_norm_mix_g=m_norm_mix_g, m_w_in=m_w_in, m_conv_short_w=m_conv_short_w,
        m_w_short_out=m_w_short_out, m_conv_conf_w=m_conv_conf_w, m_conv_conf_b=m_conv_conf_b, m_conf_ln_g=m_conf_ln_g,
        m_conf_ln_b=m_conf_ln_b, m_w_conf_out=m_w_conf_out, m_w_o=m_w_o, m_norm_ffn_g=m_norm_ffn_g, m_w_ffn_in=m_w_ffn_in,
        m_w_ffn_out=m_w_ffn_out, m_final_norm_g=m_final_norm_g,
        v_w_ada=v_w_ada, v_b_ada=v_b_ada, v_norm_mix_g=v_norm_mix_g, v_w_in=v_w_in, v_conv_short_w=v_conv_short_w,
        v_w_short_out=v_w_short_out, v_conv_conf_w=v_conv_conf_w, v_conv_conf_b=v_conv_conf_b, v_conf_ln_g=v_conf_ln_g,
        v_conf_ln_b=v_conf_ln_b, v_w_conf_out=v_w_conf_out, v_w_o=v_w_o, v_norm_ffn_g=v_norm_ffn_g, v_w_ffn_in=v_w_ffn_in,
        v_w_ffn_out=v_w_ffn_out, v_final_norm_g=v_final_norm_g)
    loss, grad_x, g, dl, nm, nv = _step(
        x, c, w_ada, b_ada, norm_mix_g, w_in, conv_short_w, w_short_out, conv_conf_w, conv_conf_b, conf_ln_g, conf_ln_b,
        w_conf_out, w_o, norm_ffn_g, w_ffn_in, w_ffn_out, final_norm_g, loss_target, moments, min(ROW_TILE, x.shape[1]))
    return (loss, grad_x, *[g[n] for n in _WEIGHTS], *[dl[n] for n in _WEIGHTS], *[nm[n] for n in _WEIGHTS],
            *[nv[n] for n in _WEIGHTS])
```

```python
import jax
import jax.numpy as jnp
from jax import lax
from jax.experimental import pallas as pl
from jax.experimental.pallas import tpu as pltpu

F32 = jnp.float32
BF16 = jnp.bfloat16
EPS = 1e-6
LN_EPS = 1e-5
SHORT_K = 3
CONF_K = 31
N_MOD = 6
N_DEV = 8
N_CHIP = 4
ADAM_LR = 0.001
ADAM_B1 = 0.9
ADAM_B2 = 0.999
ADAM_EPS = 1e-08
ADAM_WD = 0.01
ADAM_STEP = 10
MESH = pl.DeviceIdType.MESH
V7X_VMEM_LIMIT_BYTES = 60 * 1024 * 1024
LANES = 128
HALO_SHORT = 8
HALO_CONF = 32
N_SMALL_ROWS = 48
ROW_DMOD, ROW_GN1, ROW_W3, ROW_W31, ROW_CB, ROW_LNG, ROW_LNB, ROW_GN2, ROW_GF, ROW_SQERR = 0, 6, 7, 10, 41, 42, 43, 44, 45, 46
CONV_ROWS = 64
CONV_LANES = 256

ANY = pl.BlockSpec(memory_space=pl.ANY)
VMEM_FULL = pl.BlockSpec(memory_space=pltpu.VMEM)


def _pallas(body, **kw):
    return pl.pallas_call(body, **kw)


def _params(*sem):
    return pltpu.CompilerParams(dimension_semantics=sem, vmem_limit_bytes=V7X_VMEM_LIMIT_BYTES)


def _sig(z):
    return jax.nn.sigmoid(z)


def _dot(a, b):
    return jnp.dot(a, b, preferred_element_type=F32)


def _dot_nt(a, b):
    return lax.dot_general(a, b, (((1,), (1,)), ((), ())), preferred_element_type=F32)


def _dot_tn(a, b):
    return lax.dot_general(a, b, (((0,), (0,)), ((), ())), preferred_element_type=F32)


def _colsum(z):
    return jnp.sum(z, axis=0, keepdims=True)


def _place():
    return lax.axis_index("x"), lax.axis_index("y"), lax.axis_index("c")


def _small_gather(x_ref, out_ref, send_sems, recv_sems, local_sem):
    m_per = x_ref.shape[0]
    x, y, c = _place()
    me, sibling = (x, y, c), (x, y, 1 - c)
    chips = [(1 - x, y), (x, 1 - y), (1 - x, 1 - y)]

    def rows(px, py, pc):
        return out_ref.at[pl.ds((4 * px + 2 * py + pc) * m_per, m_per), :]

    def copy(k, blk, to, src=None):
        return pltpu.make_async_remote_copy(
            src_ref=rows(*blk) if src is None else src, dst_ref=rows(*blk),
            send_sem=send_sems.at[k], recv_sem=recv_sems.at[k], device_id=to, device_id_type=MESH)

    mine = pltpu.make_async_copy(x_ref, rows(*me), local_sem)
    mine.start()
    first = [copy(0, me, sibling, src=x_ref)]
    first += [copy(1 + j, me, (*chip, c), src=x_ref) for j, chip in enumerate(chips)]
    for cp in first:
        cp.start()
    passed = [copy(4 + j, (*chip, c), sibling) for j, chip in enumerate(chips)]
    for j, chip in enumerate(chips):
        copy(1 + j, (*chip, c), me).wait_recv()
        passed[j].start()
    copy(0, sibling, me).wait_recv()
    for j, chip in enumerate(chips):
        copy(4 + j, (*chip, 1 - c), me).wait_recv()
    for cp in first + passed:
        cp.wait_send()
    mine.wait()


def _small_gather_sems():
    return [pltpu.SemaphoreType.DMA((7,)), pltpu.SemaphoreType.DMA((7,)), pltpu.SemaphoreType.DMA]


def _shard_slice(ref, full_shape, shard_axis, chip, half):
    m, n = full_shape
    if shard_axis == 0:
        r = m // N_CHIP
        return ref.at[pl.ds(chip * r + half * (r // 2), r // 2), :]
    cw = n // N_CHIP
    return ref.at[pl.ds(half * (m // 2), m // 2), pl.ds(pl.multiple_of(chip * cw, 128), cw)]


def _place_shard(w, axis, chip, name):
    r, cc = w.shape
    tr = r if axis == 0 else min(256, r)
    full = (r * N_CHIP, cc) if axis == 0 else (r, cc * N_CHIP)

    def body(s_ref, w_ref, o_ref):
        o_ref[...] = w_ref[...].astype(BF16)

    out_map = (lambda i, s: (s[0], 0)) if axis == 0 else (lambda i, s: (i, s[0]))
    grid_spec = pltpu.PrefetchScalarGridSpec(
        num_scalar_prefetch=1, grid=(r // tr,), in_specs=[pl.BlockSpec((tr, cc), lambda i, s: (i, 0))],
        out_specs=pl.BlockSpec((tr, cc), out_map))
    return _pallas(body, name=name, grid_spec=grid_spec, out_shape=jax.ShapeDtypeStruct(full, BF16),
                   compiler_params=_params("arbitrary"))(jnp.reshape(chip, (1,)).astype(jnp.int32), w)


def _other_chips():
    x, y, _ = _place()
    return [(1 - x, y), (x, 1 - y), (1 - x, 1 - y)]


def _gather_send(refs, shapes, axes, send_sems, recv_sems):
    x, y, c = _place()
    for a in range(len(refs)):
        mine = _shard_slice(refs[a], shapes[a], axes[a], 2 * x + y, c)
        for j, (px, py) in enumerate(_other_chips()):
            pltpu.make_async_remote_copy(src_ref=mine, dst_ref=mine, send_sem=send_sems.at[a, j], recv_sem=recv_sems.at[a, j],
                                         device_id=(px, py, c), device_id_type=MESH).start()


def _gather_forward(refs, shapes, axes, send_sems, recv_sems):
    x, y, c = _place()
    for a in range(len(refs)):
        for j, (px, py) in enumerate(_other_chips()):
            landed = _shard_slice(refs[a], shapes[a], axes[a], 2 * px + py, c)
            pltpu.make_async_remote_copy(src_ref=landed, dst_ref=landed, send_sem=send_sems.at[a, j], recv_sem=recv_sems.at[a, j],
                                         device_id=(px, py, c), device_id_type=MESH).wait_recv()
            pltpu.make_async_remote_copy(src_ref=landed, dst_ref=landed, send_sem=send_sems.at[a, 3 + j],
                                         recv_sem=recv_sems.at[a, 3 + j], device_id=(x, y, 1 - c), device_id_type=MESH).start()


def _gather_finish(refs, shapes, axes, send_sems, recv_sems):
    x, y, c = _place()
    for a in range(len(refs)):
        for j, (px, py) in enumerate(_other_chips()):
            other = _shard_slice(refs[a], shapes[a], axes[a], 2 * px + py, 1 - c)
            landed = _shard_slice(refs[a], shapes[a], axes[a], 2 * px + py, c)
            mine = _shard_slice(refs[a], shapes[a], axes[a], 2 * x + y, c)
            me = (x, y, c)
            pltpu.make_async_remote_copy(src_ref=other, dst_ref=other, send_sem=send_sems.at[a, 3 + j],
                                         recv_sem=recv_sems.at[a, 3 + j], device_id=me, device_id_type=MESH).wait_recv()
            pltpu.make_async_remote_copy(src_ref=mine, dst_ref=mine, send_sem=send_sems.at[a, j], recv_sem=recv_sems.at[a, j],
                                         device_id=me, device_id_type=MESH).wait_send()
            pltpu.make_async_remote_copy(src_ref=landed, dst_ref=landed, send_sem=send_sems.at[a, 3 + j],
                                         recv_sem=recv_sems.at[a, 3 + j], device_id=me, device_id_type=MESH).wait_send()


def _gather_sems(n_w):
    return [pltpu.SemaphoreType.DMA((n_w, 6)), pltpu.SemaphoreType.DMA((n_w, 6))]


def _gather_weights(fulls, axes, c_taps, w_ada_bf, b_part, name):
    n_w = len(fulls)
    shapes = [f.shape for f in fulls]
    r_ct = c_taps.shape[0]
    mq = w_ada_bf.shape[1]

    def body(*refs):
        ct_ref, wada_ref, b_ref = refs[n_w:n_w + 3]
        outs = refs[n_w + 3:2 * n_w + 3]
        ct_all_ref, mod_all_ref, mod_part = refs[2 * n_w + 3:2 * n_w + 6]
        sems, ct_sems, mod_sems = refs[2 * n_w + 6:2 * n_w + 8], refs[2 * n_w + 8:2 * n_w + 11], refs[2 * n_w + 11:]
        _gather_send(outs, shapes, axes, *sems)
        _small_gather(ct_ref, ct_all_ref, *ct_sems)
        cv = jnp.concatenate([ct_all_ref[r_ct * dv:r_ct * dv + 1, :] for dv in range(N_DEV)], axis=0)
        mod_part[...] = _dot((cv * _sig(cv)).astype(BF16), wada_ref[...]) + b_ref[...]
        _small_gather(mod_part, mod_all_ref, *mod_sems)
        _gather_forward(outs, shapes, axes, *sems)
        _gather_finish(outs, shapes, axes, *sems)

    res = _pallas(
        body, name=name,
        out_shape=tuple(jax.ShapeDtypeStruct(f.shape, BF16) for f in fulls)
        + (jax.ShapeDtypeStruct((N_DEV * r_ct, c_taps.shape[1]), F32), jax.ShapeDtypeStruct((N_DEV * N_DEV, mq), F32)),
        in_specs=[ANY] * n_w + [VMEM_FULL] * 3, out_specs=tuple([ANY] * n_w + [VMEM_FULL] * 2),
        input_output_aliases={a: a for a in range(n_w)},
        scratch_shapes=[pltpu.VMEM((N_DEV, mq), F32)] + _gather_sems(n_w) + _small_gather_sems() + _small_gather_sems(),
        compiler_params=_params(),
    )(*fulls, c_taps, w_ada_bf, b_part)
    return res[:n_w], res[n_w], res[n_w + 1]


def _piece_shapes(grads, axes):
    return [((g.shape[0] // N_CHIP // 2, g.shape[1]) if ax == 0 else (g.shape[0] // 2, g.shape[1] // N_CHIP))
            for g, ax in zip(grads, axes)]


def _exchange_send(ins, outs, shapes, axes, send_sems, recv_sems):
    x, y, c = _place()
    me = 4 * x + 2 * y + c
    for a in range(len(ins)):
        for t in range(N_DEV):
            tx, ty, tc = t // 4, (t // 2) % 2, t % 2
            src = _shard_slice(ins[a], shapes[a], axes[a], 2 * tx + ty, tc)

            @pl.when(me != t)
            def _(src=src, a=a, t=t, to=(tx, ty, tc)):
                slot = (me - t + N_DEV) % N_DEV - 1
                pltpu.make_async_remote_copy(src_ref=src, dst_ref=outs[a].at[slot], send_sem=send_sems.at[a, t],
                                             recv_sem=recv_sems.at[a, slot], device_id=to, device_id_type=MESH).start()


def _exchange_finish(outs, send_sems, recv_sems):
    x, y, c = _place()
    me = 4 * x + 2 * y + c
    for a in range(len(outs)):
        for t in range(N_DEV):
            @pl.when(me != t)
            def _(a=a, t=t):
                slot = outs[a].at[0]
                pltpu.make_async_remote_copy(src_ref=slot, dst_ref=slot, send_sem=send_sems.at[a, t], recv_sem=recv_sems.at[a, 0],
                                             device_id=(x, y, c), device_id_type=MESH).wait_send()
        for k in range(N_DEV - 1):
            slot = outs[a].at[k]
            pltpu.make_async_remote_copy(src_ref=slot, dst_ref=slot, send_sem=send_sems.at[a, 0], recv_sem=recv_sems.at[a, k],
                                         device_id=(x, y, c), device_id_type=MESH).wait_recv()


def _exchange_sems(n_w):
    return [pltpu.SemaphoreType.DMA((n_w, N_DEV)), pltpu.SemaphoreType.DMA((n_w, N_DEV - 1))]


def _exchange_out_shapes(grads, axes):
    return tuple(jax.ShapeDtypeStruct((N_DEV - 1,) + p, BF16) for p in _piece_shapes(grads, axes))


def _swap_halves(fulls, block):
    n_w = len(fulls)

    def body(*refs):
        block_ref = refs[n_w]
        outs, gathered_ref = refs[n_w + 1:2 * n_w + 1], refs[2 * n_w + 1]
        send_sems, recv_sems = refs[2 * n_w + 2:2 * n_w + 4]
        x, y, c = _place()
        sibling = (x, y, 1 - c)
        waits = []
        for a in range(n_w):
            r = fulls[a].shape[0] // 2
            mine = outs[a].at[pl.ds(c * r, r), :]
            theirs = outs[a].at[pl.ds((1 - c) * r, r), :]
            rc = pltpu.make_async_remote_copy(src_ref=mine, dst_ref=mine, send_sem=send_sems.at[a], recv_sem=recv_sems.at[a],
                                              device_id=sibling, device_id_type=MESH)
            rc.start()
            got = pltpu.make_async_remote_copy(src_ref=theirs, dst_ref=theirs, send_sem=send_sems.at[a],
                                               recv_sem=recv_sems.at[a], device_id=sibling, device_id_type=MESH)
            waits += [rc.wait_send, got.wait_recv]
        _small_gather(block_ref, gathered_ref, *refs[2 * n_w + 4:])
        for w in waits:
            w()

    res = _pallas(
        body, name="swap_halves",
        out_shape=tuple(jax.ShapeDtypeStruct(f.shape, f.dtype) for f in fulls)
        + (jax.ShapeDtypeStruct((N_DEV * block.shape[0], block.shape[1]), block.dtype),),
        in_specs=[ANY] * n_w + [VMEM_FULL], out_specs=tuple([ANY] * n_w + [VMEM_FULL]),
        input_output_aliases={a: a for a in range(n_w)},
        scratch_shapes=[pltpu.SemaphoreType.DMA((n_w,)), pltpu.SemaphoreType.DMA((n_w,))] + _small_gather_sems(),
    )(*fulls, block)
    return res[:n_w], res[n_w]


def _load_weights(pairs, sems):
    @pl.when(pl.program_id(0) == 0)
    def _():
        cps = [pltpu.make_async_copy(s, d, sems.at[k]) for k, (s, d) in enumerate(pairs)]
        for cp in cps:
            cp.start()
        for cp in cps:
            cp.wait()


def _conv_chunks(ext, e_buf, ts, d, chunk_fn, unroll, between=()):
    rows, dh = e_buf.shape[1], e_buf.shape[2]
    for hf in range(d // dh):
        for r in range(1, 8):
            e_buf[r - 1, :, :] = ext[r:r + rows, hf * dh:(hf + 1) * dh]
        for lq in range(dh // CONV_LANES):
            lo, el = hf * dh + lq * CONV_LANES, lq * CONV_LANES

            def body(ci, carry, lo=lo, el=el):
                r0 = pl.multiple_of(ci * CONV_ROWS, CONV_ROWS)

                def tap(off):
                    q, r = divmod(off, 8)
                    if r == 0:
                        return ext[pl.ds(r0 + 8 * q, CONV_ROWS), lo:lo + CONV_LANES]
                    return e_buf[r - 1, pl.ds(r0 + 8 * q, CONV_ROWS), el:el + CONV_LANES]

                chunk_fn(tap, r0, lo)
                return carry

            lax.fori_loop(0, ts // CONV_ROWS, body, 0, unroll=unroll)
            if between:
                between[0]()
                between = between[1:]
    for fn in between:
        fn()


def _fwd_mix(x, mod, gn1, w3, w31, cbias, lng, lnb, w_in, w_a, w_b, w_o, later, later_axes, ts):
    s_len, d = x.shape
    n_t = s_len // ts
    n_l = len(later)
    later_shapes = [w.shape for w in later]

    def body(*refs):
        (x_ref, mod_ref, gn1_ref, w3_ref, w31_ref, cb_ref, lng_ref, lnb_ref, win_hbm, wa_hbm, wb_hbm, wo_hbm) = refs[:12]
        (proj_ref, h_ref, yap_ref, c3_ref, n_ref, rstd_ref, u3_ref, ya_ref, yb_ref, mg_ref, mix_ref,
         x1_ref) = refs[12 + n_l:24 + n_l]
        later_refs = refs[24 + n_l:24 + 2 * n_l]
        win_v, wa_v, wb_v, wo_v, cv_ext, u0_ext, e_buf, u1_buf, sems, g_send, g_recv = refs[24 + 2 * n_l:]
        step = pl.program_id(0)
        pl.when(step == 0)(lambda: _gather_send(later_refs, later_shapes, later_axes, g_send, g_recv))
        pl.when(step == n_t // 2)(lambda: _gather_forward(later_refs, later_shapes, later_axes, g_send, g_recv))
        pl.when(step == n_t - 1)(lambda: _gather_finish(later_refs, later_shapes, later_axes, g_send, g_recv))
        _load_weights([(win_hbm, win_v), (wa_hbm, wa_v), (wb_hbm, wb_v), (wo_hbm, wo_v)], sems)

        @pl.when(pl.program_id(0) == 0)
        def _():
            cv_ext[0:HALO_SHORT, :] = jnp.zeros((HALO_SHORT, d), F32)
            u0_ext[0:HALO_CONF, :] = jnp.zeros((HALO_CONF, d), F32)

        xv = x_ref[...]
        sh1, sc1, g1 = mod_ref[:, 0:d], mod_ref[:, d:2 * d], mod_ref[:, 2 * d:3 * d]
        r = lax.rsqrt(jnp.mean(xv * xv, axis=-1, keepdims=True) + EPS)
        hb = (((xv * r) * gn1_ref[...]) * (1.0 + sc1) + sh1).astype(BF16)
        h_ref[...] = hb

        def pj(k):
            p = _dot(hb, win_v[:, k * d:(k + 1) * d])
            proj_ref[:, k * d:(k + 1) * d] = p.astype(BF16)
            return p

        v_c = pj(3)
        u0_ext[HALO_CONF:HALO_CONF + ts, :] = v_c * _sig(pj(4))

        def conv_chunk(tap, r0, lo):
            acc = jnp.zeros((CONV_ROWS, CONV_LANES), F32)
            for k in range(CONF_K):
                acc = acc + w31_ref[k:k + 1, lo:lo + CONV_LANES] * tap(HALO_CONF - (CONF_K - 1) + k)
            u1_buf[pl.ds(r0, CONV_ROWS), lo:lo + CONV_LANES] = acc

        pjs = {}
        _conv_chunks(u0_ext, e_buf, ts, d, conv_chunk, unroll=True,
                     between=[lambda k=k: pjs.__setitem__(k, pj(k)) for k in (1, 2, 0, 5, 6)])
        u0_ext[0:HALO_CONF, :] = u0_ext[ts:ts + HALO_CONF, :]

        cv_ext[HALO_SHORT:HALO_SHORT + ts, :] = pjs[1] * pjs[2]
        conv3 = jnp.zeros((ts, d), F32)
        for k in range(SHORT_K):
            off = HALO_SHORT - (SHORT_K - 1) + k
            conv3 = conv3 + w3_ref[k:k + 1, :] * cv_ext[off:off + ts, :]
        cv_ext[0:HALO_SHORT, :] = cv_ext[ts:ts + HALO_SHORT, :]
        c3_ref[...] = conv3.astype(BF16)
        yap = (pjs[0] * conv3).astype(BF16)
        yap_ref[...] = yap
        y_a = _dot(yap, wa_v[...])
        ya_ref[...] = y_a.astype(BF16)

        u1 = u1_buf[...] + cb_ref[...]
        mu = jnp.mean(u1, axis=-1, keepdims=True)
        var = jnp.mean(jnp.square(u1 - mu), axis=-1, keepdims=True)
        rstd = lax.rsqrt(var + LN_EPS)
        nrm = (u1 - mu) * rstd
        n_ref[...] = nrm.astype(BF16)
        rstd_ref[...] = jnp.broadcast_to(rstd, (ts, LANES))
        u2 = nrm * lng_ref[...] + lnb_ref[...]
        u3 = (u2 * _sig(u2)).astype(BF16)
        u3_ref[...] = u3
        y_b = _dot(u3, wb_v[...])
        yb_ref[...] = y_b.astype(BF16)

        merged = (_sig(pjs[5]) * y_a + _sig(pjs[6]) * y_b).astype(BF16)
        mg_ref[...] = merged
        mix = _dot(merged, wo_v[...])
        mix_ref[...] = mix
        x1_ref[...] = xv + g1 * mix

    row = lambda w, dt=None: pl.BlockSpec((ts, w), lambda i: (i, 0))
    full = lambda a: pl.BlockSpec(a.shape, lambda i: (0, 0))
    sd = lambda w, dt: jax.ShapeDtypeStruct((s_len, w), dt)
    outs = [sd(7 * d, BF16), sd(d, BF16), sd(d, BF16), sd(d, BF16), sd(d, BF16), sd(LANES, F32), sd(d, BF16), sd(d, BF16),
            sd(d, BF16), sd(d, BF16), sd(d, F32), sd(d, F32)]
    res = _pallas(
        body, name="fwd_mix", grid=(n_t,),
        out_shape=tuple(outs) + tuple(jax.ShapeDtypeStruct(w.shape, BF16) for w in later),
        in_specs=[row(d), full(mod), full(gn1), full(w3), full(w31), full(cbias), full(lng), full(lnb), ANY, ANY, ANY, ANY]
        + [ANY] * n_l,
        out_specs=tuple(row(o.shape[1]) for o in outs) + tuple([ANY] * n_l),
        input_output_aliases={12 + k: len(outs) + k for k in range(n_l)},
        scratch_shapes=[pltpu.VMEM(w_in.shape, BF16), pltpu.VMEM(w_a.shape, BF16), pltpu.VMEM(w_b.shape, BF16),
                        pltpu.VMEM(w_o.shape, BF16), pltpu.VMEM((ts + HALO_SHORT, d), F32),
                        pltpu.VMEM((ts + HALO_CONF, d), F32), pltpu.VMEM((7, ts + HALO_CONF - 8, d // 2), F32),
                        pltpu.VMEM((ts, d), F32), pltpu.SemaphoreType.DMA((4,))] + _gather_sems(n_l),
        compiler_params=_params("arbitrary"),
    )(x, mod, gn1, w3, w31, cbias, lng, lnb, w_in, w_a, w_b, w_o, *later)
    return res[:len(outs)], res[len(outs):]


def _ffn(x1, target, mod, gn2, gf, w_fi, w_fo, ts):
    s_len, d = x1.shape
    f = w_fo.shape[0]
    n_t = s_len // ts

    def body(x1_ref, t_ref, mod_ref, gn2_ref, gf_ref, wfi_hbm, wfo_hbm,
             h2_ref, hid_ref, df_ref, dab_ref, dx1_ref, acc_ref, wfi_v, wfo_v, sems):
        _load_weights([(wfi_hbm, wfi_v), (wfo_hbm, wfo_v)], sems)

        @pl.when(pl.program_id(0) == 0)
        def _():
            acc_ref[...] = jnp.zeros(acc_ref.shape, F32)

        x1v = x1_ref[...]
        sh2, sc2, g2 = mod_ref[:, 3 * d:4 * d], mod_ref[:, 4 * d:5 * d], mod_ref[:, 5 * d:6 * d]
        gn2v, gfv = gn2_ref[...], gf_ref[...]
        r2 = lax.rsqrt(jnp.mean(x1v * x1v, axis=-1, keepdims=True) + EPS)
        xn2 = x1v * r2
        h2 = ((xn2 * gn2v) * (1.0 + sc2) + sh2).astype(BF16)
        h2_ref[...] = h2
        a = _dot(h2, wfi_v[:, 0:f])
        bg = _dot(h2, wfi_v[:, f:2 * f])
        sa = _sig(a)
        silu_a = a * sa
        hid = (silu_a * bg).astype(BF16)
        hid_ref[...] = hid
        ffn = _dot(hid, wfo_v[...])
        x2 = x1v + g2 * ffn
        r3 = lax.rsqrt(jnp.mean(x2 * x2, axis=-1, keepdims=True) + EPS)
        xn3 = x2 * r3
        diff = xn3 * gfv - t_ref[...]
        dy = diff * (1.0 / d)
        dxn3 = dy * gfv
        dx2 = r3 * (dxn3 - xn3 * jnp.mean(dxn3 * xn3, axis=-1, keepdims=True))
        acc_ref[ROW_GF:ROW_GF + 1, :] += _colsum(dy * xn3)
        acc_ref[ROW_DMOD + 5:ROW_DMOD + 5 + 1, :] += _colsum(dx2 * ffn)
        acc_ref[ROW_SQERR:ROW_SQERR + 1, :] += _colsum(diff * diff)

        df = (dx2 * g2).astype(BF16)
        df_ref[...] = df
        dhid = _dot_nt(df, wfo_v[...])
        da = dhid * bg * (sa * (1.0 + a * (1.0 - sa)))
        dbg = dhid * silu_a
        dab_ref[:, 0:f] = da.astype(BF16)
        dab_ref[:, f:2 * f] = dbg.astype(BF16)
        dh2 = _dot_nt(da.astype(BF16), wfi_v[:, 0:f]) + _dot_nt(dbg.astype(BF16), wfi_v[:, f:2 * f])
        acc_ref[ROW_DMOD + 3:ROW_DMOD + 3 + 1, :] += _colsum(dh2)
        acc_ref[ROW_DMOD + 4:ROW_DMOD + 4 + 1, :] += _colsum(dh2 * (xn2 * gn2v))
        acc_ref[ROW_GN2:ROW_GN2 + 1, :] += _colsum(dh2 * xn2 * (1.0 + sc2))
        dxn2 = dh2 * gn2v * (1.0 + sc2)
        dx1_ref[...] = dx2 + r2 * (dxn2 - xn2 * jnp.mean(dxn2 * xn2, axis=-1, keepdims=True))

    row = lambda w: pl.BlockSpec((ts, w), lambda i: (i, 0))
    full = lambda a: pl.BlockSpec(a.shape, lambda i: (0, 0))
    sd = lambda w, dt: jax.ShapeDtypeStruct((s_len, w), dt)
    outs = [sd(d, BF16), sd(f, BF16), sd(d, BF16), sd(2 * f, BF16), sd(d, F32), jax.ShapeDtypeStruct((N_SMALL_ROWS, d), F32)]
    return _pallas(
        body, name="ffn", grid=(n_t,), out_shape=tuple(outs),
        in_specs=[row(d), row(d), full(mod), full(gn2), full(gf), ANY, ANY],
        out_specs=tuple([row(o.shape[1]) for o in outs[:-1]] + [pl.BlockSpec((N_SMALL_ROWS, d), lambda i: (0, 0))]),
        scratch_shapes=[pltpu.VMEM(w_fi.shape, BF16), pltpu.VMEM(w_fo.shape, BF16), pltpu.SemaphoreType.DMA((2,))],
        compiler_params=_params("arbitrary"),
    )(x1, target, mod, gn2, gf, w_fi, w_fo)


def _bwd_mix(dx1, proj, nrm, rstd, c3, ya, yb, mix, mod, w3, w31, lng, lnb, w_a, w_b, w_o, ready, ready_axes, ts):
    s_len, d = dx1.shape
    n_t = s_len // ts
    n_r = len(ready)
    ready_shapes = [g.shape for g in ready]

    def body(*refs):
        (dx1_ref, proj_ref, n_ref, rstd_ref, c3_ref, ya_ref, yb_ref, mix_ref, mod_ref, w3_ref, w31_ref, lng_ref,
         lnb_ref, wa_hbm, wb_hbm, wo_hbm) = refs[:16]
        ready_refs = refs[16:16 + n_r]
        dproj_ref, dmix_ref, dya_ref, dyb_ref, acc_ref = refs[16 + n_r:21 + n_r]
        recv_refs = refs[21 + n_r:21 + 2 * n_r]
        wa_v, wb_v, wo_v, dc3_ext, du1_ext, e_buf, u0_buf, du0_buf, dw8, sems, e_send, e_recv = refs[21 + 2 * n_r:]
        step = pl.program_id(0)
        pl.when(step == 0)(lambda: _exchange_send(ready_refs, recv_refs, ready_shapes, ready_axes, e_send, e_recv))
        pl.when(step == n_t - 1)(lambda: _exchange_finish(recv_refs, e_send, e_recv))
        _load_weights([(wa_hbm, wa_v), (wb_hbm, wb_v), (wo_hbm, wo_v)], sems)

        @pl.when(pl.program_id(0) == 0)
        def _():
            acc_ref[...] = jnp.zeros(acc_ref.shape, F32)
            dc3_ext[ts:ts + HALO_SHORT, :] = jnp.zeros((HALO_SHORT, d), F32)
            du1_ext[ts:ts + HALO_CONF, :] = jnp.zeros((HALO_CONF, d), F32)
            dw8[...] = jnp.zeros(dw8.shape, F32)

        pj = lambda k: proj_ref[:, k * d:(k + 1) * d].astype(F32)
        g1 = mod_ref[:, 2 * d:3 * d]
        dx1v = dx1_ref[...]
        acc_ref[ROW_DMOD + 2:ROW_DMOD + 2 + 1, :] += _colsum(dx1v * mix_ref[...])
        dmix = (dx1v * g1).astype(BF16)
        dmix_ref[...] = dmix
        dmerged = _dot_nt(dmix, wo_v[...])
        g_a, g_b = _sig(pj(5)), _sig(pj(6))
        y_a, y_b = ya_ref[...].astype(F32), yb_ref[...].astype(F32)
        dproj_ref[:, 5 * d:6 * d] = (dmerged * y_a * g_a * (1.0 - g_a)).astype(BF16)
        dproj_ref[:, 6 * d:7 * d] = (dmerged * y_b * g_b * (1.0 - g_b)).astype(BF16)
        dya = (dmerged * g_a).astype(BF16)
        dyb = (dmerged * g_b).astype(BF16)
        dya_ref[...] = dya
        dyb_ref[...] = dyb

        dyap = _dot_nt(dya, wa_v[...])
        dproj_ref[:, 0:d] = (dyap * c3_ref[...].astype(F32)).astype(BF16)
        dc3_ext[0:ts, :] = dyap * pj(0)
        c_s, v_s = pj(1), pj(2)
        cv = c_s * v_s
        dcv = jnp.zeros((ts, d), F32)
        for j in range(SHORT_K):
            k = SHORT_K - 1 - j
            sj = dc3_ext[j:j + ts, :]
            dcv = dcv + w3_ref[k:k + 1, :] * sj
            acc_ref[ROW_W3 + k:ROW_W3 + k + 1, :] += _colsum(cv * sj)
        dc3_ext[ts:ts + HALO_SHORT, :] = dc3_ext[0:HALO_SHORT, :]
        dproj_ref[:, d:2 * d] = (dcv * v_s).astype(BF16)
        dproj_ref[:, 2 * d:3 * d] = (dcv * c_s).astype(BF16)

        du3 = _dot_nt(dyb, wb_v[...])
        nv = n_ref[...].astype(F32)
        lg = lng_ref[...]
        u2 = nv * lg + lnb_ref[...]
        s2 = _sig(u2)
        du2 = du3 * (s2 * (1.0 + u2 * (1.0 - s2)))
        acc_ref[ROW_LNG:ROW_LNG + 1, :] += _colsum(du2 * nv)
        acc_ref[ROW_LNB:ROW_LNB + 1, :] += _colsum(du2)
        dn = du2 * lg
        du1 = rstd_ref[:, 0:1] * (dn - jnp.mean(dn, axis=-1, keepdims=True) - nv * jnp.mean(dn * nv, axis=-1, keepdims=True))
        acc_ref[ROW_CB:ROW_CB + 1, :] += _colsum(du1)
        du1_ext[0:ts, :] = du1
        v_c = pj(3)
        sg = _sig(pj(4))
        u0_buf[...] = v_c * sg

        def conv_chunk(tap, r0, lo):
            lanes = slice(lo, lo + CONV_LANES)
            u0c = u0_buf[pl.ds(r0, CONV_ROWS), lanes]
            acc = jnp.zeros((CONV_ROWS, CONV_LANES), F32)
            for j in range(CONF_K):
                k = CONF_K - 1 - j
                sj = tap(j)
                acc = acc + w31_ref[k:k + 1, lanes] * sj
                prod = u0c * sj
                part = prod[0:8]
                for i in range(1, CONV_ROWS // 8):
                    part = part + prod[8 * i:8 * i + 8]
                dw8[k, :, lanes] += part
            du0_buf[pl.ds(r0, CONV_ROWS), lanes] = acc

        _conv_chunks(du1_ext, e_buf, ts, d, conv_chunk, unroll=False)
        du0 = du0_buf[...]
        du1_ext[ts:ts + HALO_CONF, :] = du1_ext[0:HALO_CONF, :]

        @pl.when(step == n_t - 1)
        def _():
            for k in range(CONF_K):
                acc_ref[ROW_W31 + k:ROW_W31 + k + 1, :] = _colsum(dw8[k])
        dproj_ref[:, 3 * d:4 * d] = (du0 * sg).astype(BF16)
        dproj_ref[:, 4 * d:5 * d] = (du0 * v_c * sg * (1.0 - sg)).astype(BF16)

    rev = lambda w: pl.BlockSpec((ts, w), lambda i: (n_t - 1 - i, 0))
    full = lambda a: pl.BlockSpec(a.shape, lambda i: (0, 0))
    sd = lambda w, dt: jax.ShapeDtypeStruct((s_len, w), dt)
    outs = [sd(7 * d, BF16), sd(d, BF16), sd(d, BF16), sd(d, BF16), jax.ShapeDtypeStruct((N_SMALL_ROWS, d), F32)]
    res = _pallas(
        body, name="bwd_mix", grid=(n_t,), out_shape=tuple(outs) + _exchange_out_shapes(ready, ready_axes),
        in_specs=[rev(d), rev(7 * d), rev(d), rev(LANES), rev(d), rev(d), rev(d), rev(d), full(mod), full(w3), full(w31),
                  full(lng), full(lnb), ANY, ANY, ANY] + [ANY] * n_r,
        out_specs=tuple([rev(o.shape[1]) for o in outs[:-1]] + [pl.BlockSpec((N_SMALL_ROWS, d), lambda i: (0, 0))] + [ANY] * n_r),
        scratch_shapes=[pltpu.VMEM(w_a.shape, BF16), pltpu.VMEM(w_b.shape, BF16), pltpu.VMEM(w_o.shape, BF16),
                        pltpu.VMEM((ts + HALO_SHORT, d), F32), pltpu.VMEM((ts + HALO_CONF, d), F32),
                        pltpu.VMEM((7, ts + HALO_CONF - 8, d // 2), F32), pltpu.VMEM((ts, d), F32), pltpu.VMEM((ts, d), F32),
                        pltpu.VMEM((CONF_K, 8, d), F32), pltpu.SemaphoreType.DMA((3,))] + _exchange_sems(n_r),
        compiler_params=_params("arbitrary"),
    )(dx1, proj, nrm, rstd, c3, ya, yb, mix, mod, w3, w31, lng, lnb, w_a, w_b, w_o, *ready)
    return res[:len(outs)], res[len(outs):]


def _bwd_in(dproj, x, dx1, mod, gn1, w_in, ready, ready_axes, ts):
    s_len, d = x.shape
    n_t = s_len // ts
    n_r = len(ready)
    ready_shapes = [g.shape for g in ready]

    def body(*refs):
        dproj_ref, x_ref, dx1_ref, mod_ref, gn1_ref, win_hbm = refs[:6]
        ready_refs = refs[6:6 + n_r]
        gx_ref, acc_ref = refs[6 + n_r:8 + n_r]
        recv_refs = refs[8 + n_r:8 + 2 * n_r]
        win_v, sems, e_send, e_recv = refs[8 + 2 * n_r:]
        step = pl.program_id(0)
        pl.when(step == 0)(lambda: _exchange_send(ready_refs, recv_refs, ready_shapes, ready_axes, e_send, e_recv))
        pl.when(step == n_t - 1)(lambda: _exchange_finish(recv_refs, e_send, e_recv))
        _load_weights([(win_hbm, win_v)], sems)

        @pl.when(pl.program_id(0) == 0)
        def _():
            acc_ref[...] = jnp.zeros(acc_ref.shape, F32)

        xv = x_ref[...]
        sc1 = mod_ref[:, d:2 * d]
        gn1v = gn1_ref[...]
        r = lax.rsqrt(jnp.mean(xv * xv, axis=-1, keepdims=True) + EPS)
        xn = xv * r
        dh = _dot_nt(dproj_ref[...], win_v[...])
        acc_ref[ROW_DMOD:ROW_DMOD + 1, :] += _colsum(dh)
        acc_ref[ROW_DMOD + 1:ROW_DMOD + 1 + 1, :] += _colsum(dh * (xn * gn1v))
        acc_ref[ROW_GN1:ROW_GN1 + 1, :] += _colsum(dh * xn * (1.0 + sc1))
        dxn = dh * gn1v * (1.0 + sc1)
        gx_ref[...] = dx1_ref[...] + r * (dxn - xn * jnp.mean(dxn * xn, axis=-1, keepdims=True))

    row = lambda w: pl.BlockSpec((ts, w), lambda i: (i, 0))
    full = lambda a: pl.BlockSpec(a.shape, lambda i: (0, 0))
    res = _pallas(
        body, name="bwd_in", grid=(n_t,),
        out_shape=(jax.ShapeDtypeStruct((s_len, d), F32), jax.ShapeDtypeStruct((N_SMALL_ROWS, d), F32))
        + _exchange_out_shapes(ready, ready_axes),
        in_specs=[row(7 * d), row(d), row(d), full(mod), full(gn1), ANY] + [ANY] * n_r,
        out_specs=(row(d), pl.BlockSpec((N_SMALL_ROWS, d), lambda i: (0, 0))) + tuple([ANY] * n_r),
        scratch_shapes=[pltpu.VMEM(w_in.shape, BF16), pltpu.SemaphoreType.DMA((1,))] + _exchange_sems(n_r),
        compiler_params=_params("arbitrary"),
    )(dproj, x, dx1, mod, gn1, w_in, *ready)
    return res[0], res[1], res[2:]


def _weight_grad(a, b, bn, name, transposed=False, ready=(), ready_axes=()):
    s_len, m = a.shape
    n = b.shape[1]
    n_j = n // bn
    n_r = len(ready)
    ready_shapes = [g.shape for g in ready]

    def body(*refs):
        a_hbm, b_ref = refs[:2]
        ready_refs = refs[2:2 + n_r]
        o_ref = refs[2 + n_r]
        recv_refs = refs[3 + n_r:3 + 2 * n_r]
        a_v, sem = refs[3 + 2 * n_r:5 + 2 * n_r]
        step = pl.program_id(0)
        if n_r:
            e_send, e_recv = refs[5 + 2 * n_r:]
            pl.when(step == 0)(lambda: _exchange_send(ready_refs, recv_refs, ready_shapes, ready_axes, e_send, e_recv))
            pl.when(step == n_j - 1)(lambda: _exchange_finish(recv_refs, e_send, e_recv))
        _load_weights([(a_hbm, a_v)], sem)
        prod = _dot_tn(a_v[...], b_ref[...])
        o_ref[...] = (prod.T if transposed else prod).astype(BF16)

    out = jax.ShapeDtypeStruct((n, m) if transposed else (m, n), BF16)
    out_spec = pl.BlockSpec((bn, m), lambda j: (j, 0)) if transposed else pl.BlockSpec((m, bn), lambda j: (0, j))
    res = _pallas(
        body, name=name, grid=(n_j,), out_shape=(out,) + _exchange_out_shapes(ready, ready_axes),
        in_specs=[ANY, pl.BlockSpec((s_len, bn), lambda j: (0, j))] + [ANY] * n_r,
        out_specs=(out_spec,) + tuple([ANY] * n_r),
        scratch_shapes=[pltpu.VMEM(a.shape, BF16), pltpu.SemaphoreType.DMA((1,))] + (_exchange_sems(n_r) if n_r else []),
        compiler_params=_params("arbitrary"),
    )(a, b, *ready)
    return res[0], res[1:]


def _sum_slots(recv, grad, axis, chip, core, name):
    n_s, r, c = recv.shape
    tr = 16
    for cand in (256, 128, 176, 64, 32, 16):
        if r % cand == 0:
            tr = cand
            break
    n_i = r // tr

    def body(s_ref, own_ref, r_ref, o_ref):
        tot = own_ref[...].astype(F32)
        for k in range(n_s):
            tot = tot + r_ref[k].astype(F32)
        o_ref[...] = tot

    if axis == 0:
        own_map = lambda i, s: ((2 * s[0] + s[1]) * n_i + i, 0)
    else:
        own_map = lambda i, s: (s[1] * n_i + i, s[0])
    grid_spec = pltpu.PrefetchScalarGridSpec(
        num_scalar_prefetch=1, grid=(n_i,),
        in_specs=[pl.BlockSpec((tr, c), own_map), pl.BlockSpec((n_s, tr, c), lambda i, s: (0, i, 0))],
        out_specs=pl.BlockSpec((tr, c), lambda i, s: (s[1] * n_i + i, 0)))
    return _pallas(body, name=name, grid_spec=grid_spec, out_shape=jax.ShapeDtypeStruct((2 * r, c), F32),
                   compiler_params=_params("arbitrary"))(jnp.stack([chip, core]).astype(jnp.int32), grad, recv)


def _adamw_math(w, g, m, v):
    m = ADAM_B1 * m + (1.0 - ADAM_B1) * g
    v = ADAM_B2 * v + (1.0 - ADAM_B2) * (g * g)
    m_hat = m / (1.0 - ADAM_B1 ** ADAM_STEP)
    v_hat = v / (1.0 - ADAM_B2 ** ADAM_STEP)
    delta = -ADAM_LR * (m_hat / (jnp.sqrt(v_hat) + ADAM_EPS) + ADAM_WD * w)
    return delta, m, v


def _adamw(w, g, m, v, name):
    r, c = w.shape
    tr = r
    for cand in (256, 128, 176, 64, 32, 16, 8):
        if r % cand == 0:
            tr = cand
            break

    def body(w_ref, g_ref, m_ref, v_ref, go_ref, d_ref, nm_ref, nv_ref):
        g = g_ref[...]
        go_ref[...] = g
        d_ref[...], nm_ref[...], nv_ref[...] = _adamw_math(w_ref[...], g, m_ref[...], v_ref[...])

    blk = pl.BlockSpec((tr, c), lambda i: (i, 0))
    return _pallas(body, name=name, grid=(r // tr,), out_shape=tuple(jax.ShapeDtypeStruct((r, c), F32) for _ in range(4)),
                   in_specs=[blk] * 4, out_specs=(blk,) * 4, compiler_params=_params("arbitrary"))(w, g, m, v)


_SMALL = [
    ("b_ada", ROW_DMOD, "mod"), ("norm_mix_g", ROW_GN1, "row"), ("conv_short_w", ROW_W3, "shard"),
    ("conv_conf_w", ROW_W31, "shard"), ("conv_conf_b", ROW_CB, "row"), ("conf_ln_g", ROW_LNG, "row"),
    ("conf_ln_b", ROW_LNB, "row"), ("norm_ffn_g", ROW_GN2, "row"), ("final_norm_g", ROW_GF, "row")]


def _small_update(p_all, chip_onehot, ws, ms, vs):
    n = len(_SMALL)
    d = p_all.shape[2]

    def body(*refs):
        p_ref, oh_ref = refs[:2]
        w_refs, m_refs, v_refs = refs[2:2 + n], refs[2 + n:2 + 2 * n], refs[2 + 2 * n:2 + 3 * n]
        loss_ref = refs[2 + 3 * n]
        g_out, d_out, m_out, v_out = [refs[3 + (3 + q) * n:3 + (4 + q) * n] for q in range(4)]
        g_all = p_ref[0]
        for s in range(1, N_DEV):
            g_all = g_all + p_ref[s]
        loss_ref[...] = jnp.zeros(loss_ref.shape, F32) + (0.5 / d) * jnp.sum(g_all[ROW_SQERR:ROW_SQERR + 1, :])

        def emit(i, idx, g):
            dl, nm, nv = _adamw_math(w_refs[i][idx], g, m_refs[i][idx], v_refs[i][idx])
            g_out[i][idx], d_out[i][idx], m_out[i][idx], v_out[i][idx] = g, dl, nm, nv

        everything = (slice(None), slice(None))
        for i, (_, row, kind) in enumerate(_SMALL):
            if kind == "mod":
                for k in range(N_MOD):
                    emit(i, (slice(0, 1), slice(k * d, (k + 1) * d)), g_all[row + k:row + k + 1, :])
            elif kind == "row":
                emit(i, everything, g_all[row:row + 1, :])
            else:
                taps, dq = w_refs[i].shape
                g = jnp.zeros((taps, dq), F32)
                for j in range(N_CHIP):
                    g = g + oh_ref[0:1, j:j + 1] * g_all[row:row + taps, j * dq:(j + 1) * dq]
                emit(i, everything, g)

    shapes = [jax.ShapeDtypeStruct(w.shape, F32) for w in ws]
    res = _pallas(body, name="small_update", out_shape=tuple([jax.ShapeDtypeStruct((8, 128), F32)] + shapes * 4),
                  in_specs=[VMEM_FULL] * (2 + 3 * n), out_specs=tuple([VMEM_FULL] * (1 + 4 * n)),
                  compiler_params=_params())(p_all, chip_onehot, *ws, *ms, *vs)
    return res[0], [res[1 + q * n:1 + (q + 1) * n] for q in range(4)]


def _w_ada_grad(c_t, dmod_part):
    def body(c_ref, g_ref, o_ref):
        cv = c_ref[...]
        o_ref[...] = _dot((cv * _sig(cv)).astype(BF16), g_ref[...].astype(BF16))

    return _pallas(body, name="w_ada_grad", out_shape=jax.ShapeDtypeStruct((c_t.shape[0], dmod_part.shape[1]), F32),
                   in_specs=[VMEM_FULL] * 2, out_specs=VMEM_FULL, compiler_params=_params())(c_t, dmod_part)


def _step(x, c, w_ada, b_ada, norm_mix_g, w_in, conv_short_w, w_short_out, conv_conf_w, conv_conf_b, conf_ln_g,
          conf_ln_b, w_conf_out, w_o, norm_ffn_g, w_ffn_in, w_ffn_out, final_norm_g, loss_target, moments, ts):
    xi, yi, ci = _place()
    chip = 2 * xi + yi
    me = 4 * xi + 2 * yi + ci
    x2d, tgt = x[0], loss_target[0]
    s_len, d = x2d.shape
    dq = d // N_CHIP
    mq = N_MOD * d // N_CHIP
    gf = final_norm_g.reshape(1, d)

    taps = jnp.concatenate([jnp.pad(conv_short_w[0], ((0, 8 - SHORT_K), (0, 0))),
                            jnp.pad(conv_conf_w[0], ((0, 32 - CONF_K), (0, 0)))], axis=0)
    n_tr = 40 * dq // d
    c_taps = jnp.concatenate([jnp.broadcast_to(c, (8, d)), taps.reshape(n_tr, d), jnp.zeros((16 - n_tr, d), F32)], axis=0)
    b_part = lax.dynamic_slice(b_ada, (0, chip * mq), (1, mq))
    big = [w_in[0], w_short_out[0], w_conf_out[0], w_o[0], w_ffn_in[0], w_ffn_out[0]]
    axes = [1, 0, 0, 0, 1, 0]
    big_names = ["w_in", "w_short_out", "w_conf_out", "w_o", "w_ffn_in", "w_ffn_out"]
    placed = [_place_shard(w, ax, chip, "place_" + n) for w, ax, n in zip(big, axes, big_names)]
    (win_f, wa_f, wb_f, wo_f), c_taps, mod_all = _gather_weights(placed[:4], axes[:4], c_taps, w_ada[0].astype(BF16), b_part,
                                                                 "gather_weights")
    c_taps = c_taps.reshape(N_DEV, 24, d)
    c_all = c_taps[:, 0]
    mod_all = mod_all.reshape(N_CHIP, 2, N_DEV, mq)[:, 0]
    mod_full = jnp.transpose(mod_all, (1, 0, 2)).reshape(N_DEV, N_MOD * d)
    mod = lax.dynamic_slice(mod_full, (me, 0), (1, N_MOD * d))

    taps_all = c_taps[:, 8:8 + n_tr].reshape(N_CHIP, 2, 40, dq)[:, 0]
    taps_full = jnp.transpose(taps_all, (1, 0, 2)).reshape(40, d)
    w3, w31 = taps_full[0:SHORT_K], taps_full[8:8 + CONF_K]

    (proj, h, yap, c3, nrm, rstd, u3, ya, yb, merged, mix, x1), (wfi_f, wfo_f) = _fwd_mix(
        x2d, mod, norm_mix_g, w3, w31, conv_conf_b, conf_ln_g, conf_ln_b, win_f, wa_f, wb_f, wo_f, placed[4:], axes[4:], ts)
    h2, hid, df, dab, dx1, acc_f = _ffn(x1, tgt, mod, norm_ffn_g, gf, wfi_f, wfo_f, ts)
    f = wfo_f.shape[0]
    bn = min(512, d)
    g_ffn = [_weight_grad(h2, dab, bn, "grad_w_ffn_in")[0], _weight_grad(df, hid, bn // 2, "grad_w_ffn_out", transposed=True)[0]]
    (dproj, dmix, dya, dyb, acc_m), r_ffn = _bwd_mix(dx1, proj, nrm, rstd, c3, ya, yb, mix, mod, w3, w31, conf_ln_g,
                                                    conf_ln_b, wa_f, wb_f, wo_f, g_ffn, axes[4:], ts)
    g_sq = [_weight_grad(yap, dya, bn // 2, "grad_w_short_out")[0], _weight_grad(u3, dyb, bn // 2, "grad_w_conf_out")[0],
            _weight_grad(merged, dmix, bn // 2, "grad_w_o")[0]]
    g_in, r_sq = _weight_grad(h, dproj, bn, "grad_w_in", ready=g_sq, ready_axes=axes[1:4])
    grad_x, acc_i, r_in = _bwd_in(dproj, x2d, dx1, mod, norm_mix_g, win_f, [g_in], axes[:1], min(2 * ts, s_len))
    g_in = [g_in]
    r_in_sq = list(r_in) + list(r_sq)
    grads = g_in + g_sq + g_ffn
    recv = list(r_in_sq) + list(r_ffn)
    g_big, p_all = _swap_halves([_sum_slots(r, g, ax, chip, ci, "sum_slots_" + n)
                                 for r, g, ax, n in zip(recv, grads, axes, big_names)], acc_f + acc_m + acc_i)
    p_all = p_all.reshape(N_DEV, N_SMALL_ROWS, d)

    small_w = dict(b_ada=b_ada, norm_mix_g=norm_mix_g, conv_short_w=conv_short_w, conv_conf_w=conv_conf_w,
                   conv_conf_b=conv_conf_b, conf_ln_g=conf_ln_g, conf_ln_b=conf_ln_b, norm_ffn_g=norm_ffn_g,
                   final_norm_g=final_norm_g)
    as_2d = lambda t: t.reshape((-1, t.shape[-1]))
    chip_onehot = (lax.broadcasted_iota(jnp.int32, (1, 128), 1) == chip).astype(F32)
    loss_tile, small_out = _small_update(
        p_all, chip_onehot, [as_2d(small_w[n]) for n, _, _ in _SMALL], [as_2d(moments["m_" + n]) for n, _, _ in _SMALL],
        [as_2d(moments["v_" + n]) for n, _, _ in _SMALL])
    small_g, small_d, small_m, small_v = [{n: o.reshape(small_w[n].shape) for (n, _, _), o in zip(_SMALL, outs)}
                                          for outs in small_out]

    dmod_all = p_all[:, 0:N_MOD].reshape(N_DEV, N_MOD * d)
    dmod_part = jnp.pad(lax.dynamic_slice(dmod_all, (0, chip * mq), (N_DEV, mq)), ((0, 128 - N_DEV), (0, 0)))
    c_t = jnp.pad(jnp.transpose(c_all), ((0, 0), (0, 128 - N_DEV)))
    g_ada = _w_ada_grad(c_t, dmod_part)

    out_g, out_d, out_m, out_v = dict(small_g), dict(small_d), dict(small_m), dict(small_v)
    for name, w, g in zip(["w_ada"] + big_names, [w_ada[0]] + big, [g_ada] + list(g_big)):
        go, dl, nm, nv = _adamw(w, g, moments["m_" + name][0], moments["v_" + name][0], "adamw_" + name)
        out_g[name], out_d[name], out_m[name], out_v[name] = go[None], dl[None], nm[None], nv[None]

    loss = loss_tile[0, 0]
    return loss, grad_x[None], out_g, out_d, out_m, out_v


_WEIGHTS = ["w_ada", "b_ada", "norm_mix_g", "w_in", "conv_short_w", "w_short_out", "conv_conf_w", "conv_conf_b",
            "conf_ln_g", "conf_ln_b", "w_conf_out", "w_o", "norm_ffn_g", "w_ffn_in", "w_ffn_out", "final_norm_g"]
ROW_TILE = 256


def kernel(x, c, w_ada, b_ada, norm_mix_g, w_in, conv_short_w, w_short_out, conv_conf_w, conv_conf_b, conf_ln_g, conf_ln_b, w_conf_out, w_o, norm_ffn_g, w_ffn_in, w_ffn_out, final_norm_g, loss_target, m_w_ada, m_b_ada, m_norm_mix_g, m_w_in, m_conv_short_w, m_w_short_out, m_conv_conf_w, m_conv_conf_b, m_conf_ln_g, m_conf_ln_b, m_w_conf_out, m_w_o, m_norm_ffn_g, m_w_ffn_in, m_w_ffn_out, m_final_norm_g, v_w_ada, v_b_ada, v_norm_mix_g, v_w_in, v_conv_short_w, v_w_short_out, v_conv_conf_w, v_conv_conf_b, v_conf_ln_g, v_conf_ln_b, v_w_conf_out, v_w_o, v_norm_ffn_g, v_w_ffn_in, v_w_ffn_out, v_final_norm_g):
    moments = dict(
        m_w_ada=m_w_ada, m_b_ada=m_b_ada, m_norm_mix_g=m_norm_mix_g, m_w_in=m_w_in, m_conv_short_w=m_conv_short_w,
        m_w_short_out=m_w_short_out, m_conv_conf_w=m_conv_conf_w, m_conv_conf_b=m_conv_conf_b, m_conf_ln_g=m_conf_ln_g,
        m_conf_ln_b=m_conf_ln_b, m_w_conf_out=m_w_conf_out, m_w_o=m_w_o, m_norm_ffn_g=m_norm_ffn_g, m_w_ffn_in=m_w_ffn_in,
        m_w_ffn_out=m_w_ffn_out, m_final_norm_g=m_final_norm_g,
        v_w_ada=v_w_ada, v_b_ada=v_b_ada, v_norm_mix_g=v_norm_mix_g, v_w_in=v_w_in, v_conv_short_w=v_conv_short_w,
        v_w_short_out=v_w_short_out, v_conv_conf_w=v_conv_conf_w, v_conv_conf_b=v_conv_conf_b, v_conf_ln_g=v_conf_ln_g,
        v_conf_ln_b=v_conf_ln_b, v_w_conf_out=v_w_conf_out, v_w_o=v_w_o, v_norm_ffn_g=v_norm_ffn_g, v_w_ffn_in=v_w_ffn_in,
        v_w_ffn_out=v_w_ffn_out, v_final_norm_g=v_final_norm_g)
    loss, grad_x, g, dl, nm, nv = _step(
        x, c, w_ada, b_ada, norm_mix_g, w_in, conv_short_w, w_short_out, conv_conf_w, conv_conf_b, conf_ln_g, conf_ln_b,
        w_conf_out, w_o, norm_ffn_g, w_ffn_in, w_ffn_out, final_norm_g, loss_target, moments, min(ROW_TILE, x.shape[1]))
    return (loss, grad_x, *[g[n] for n in _WEIGHTS], *[dl[n] for n in _WEIGHTS], *[nm[n] for n in _WEIGHTS],
            *[nv[n] for n in _WEIGHTS])
```

```python
import jax
import jax.numpy as jnp
from jax import lax
from jax.experimental import pallas as pl
from jax.experimental.pallas import tpu as pltpu

F32 = jnp.float32
BF16 = jnp.bfloat16
EPS = 1e-6
LN_EPS = 1e-5
SHORT_K = 3
CONF_K = 31
N_MOD = 6
N_DEV = 8
N_CHIP = 4
ADAM_LR = 0.001
ADAM_B1 = 0.9
ADAM_B2 = 0.999
ADAM_EPS = 1e-08
ADAM_WD = 0.01
ADAM_STEP = 10
MESH = pl.DeviceIdType.MESH
V7X_VMEM_LIMIT_BYTES = 60 * 1024 * 1024
LANES = 128
HALO_SHORT = 8
HALO_CONF = 32
N_SMALL_ROWS = 48
ROW_DMOD, ROW_GN1, ROW_W3, ROW_W31, ROW_CB, ROW_LNG, ROW_LNB, ROW_GN2, ROW_GF, ROW_SQERR = 0, 6, 7, 10, 41, 42, 43, 44, 45, 46
CONV_ROWS = 64
CONV_LANES = 256

ANY = pl.BlockSpec(memory_space=pl.ANY)
VMEM_FULL = pl.BlockSpec(memory_space=pltpu.VMEM)


def _pallas(body, **kw):
    return pl.pallas_call(body, **kw)


def _params(*sem):
    return pltpu.CompilerParams(dimension_semantics=sem, vmem_limit_bytes=V7X_VMEM_LIMIT_BYTES)


def _sig(z):
    return jax.nn.sigmoid(z)


def _dot(a, b):
    return jnp.dot(a, b, preferred_element_type=F32)


def _dot_nt(a, b):
    return lax.dot_general(a, b, (((1,), (1,)), ((), ())), preferred_element_type=F32)


def _dot_tn(a, b):
    return lax.dot_general(a, b, (((0,), (0,)), ((), ())), preferred_element_type=F32)


def _colsum(z):
    return jnp.sum(z, axis=0, keepdims=True)


def _place():
    return lax.axis_index("x"), lax.axis_index("y"), lax.axis_index("c")


def _small_gather(x_ref, out_ref, send_sems, recv_sems, local_sem):
    m_per = x_ref.shape[0]
    x, y, c = _place()
    me, sibling = (x, y, c), (x, y, 1 - c)
    chips = [(1 - x, y), (x, 1 - y), (1 - x, 1 - y)]

    def rows(px, py, pc):
        return out_ref.at[pl.ds((4 * px + 2 * py + pc) * m_per, m_per), :]

    def copy(k, blk, to, src=None):
        return pltpu.make_async_remote_copy(
            src_ref=rows(*blk) if src is None else src, dst_ref=rows(*blk),
            send_sem=send_sems.at[k], recv_sem=recv_sems.at[k], device_id=to, device_id_type=MESH)

    mine = pltpu.make_async_copy(x_ref, rows(*me), local_sem)
    mine.start()
    first = [copy(0, me, sibling, src=x_ref)]
    first += [copy(1 + j, me, (*chip, c), src=x_ref) for j, chip in enumerate(chips)]
    for cp in first:
        cp.start()
    passed = [copy(4 + j, (*chip, c), sibling) for j, chip in enumerate(chips)]
    for j, chip in enumerate(chips):
        copy(1 + j, (*chip, c), me).wait_recv()
        passed[j].start()
    copy(0, sibling, me).wait_recv()
    for j, chip in enumerate(chips):
        copy(4 + j, (*chip, 1 - c), me).wait_recv()
    for cp in first + passed:
        cp.wait_send()
    mine.wait()


def _small_gather_sems():
    return [pltpu.SemaphoreType.DMA((7,)), pltpu.SemaphoreType.DMA((7,)), pltpu.SemaphoreType.DMA]


def _shard_slice(ref, full_shape, shard_axis, chip, half, quarter=None):
    m, n = full_shape
    r = (m // N_CHIP if shard_axis == 0 else m) // 2
    start = half * r + (chip * 2 * r if shard_axis == 0 else 0)
    if quarter is not None:
        r = r // 2
        start = start + quarter * r
    if shard_axis == 0:
        return ref.at[pl.ds(start, r), :]
    cw = n // N_CHIP
    return ref.at[pl.ds(start, r), pl.ds(pl.multiple_of(chip * cw, 128), cw)]


def _place_shard(w, axis, chip, name):
    r, cc = w.shape
    tr = r if axis == 0 else min(256, r)
    full = (r * N_CHIP, cc) if axis == 0 else (r, cc * N_CHIP)

    def body(s_ref, w_ref, o_ref):
        o_ref[...] = w_ref[...].astype(BF16)

    out_map = (lambda i, s: (s[0], 0)) if axis == 0 else (lambda i, s: (i, s[0]))
    grid_spec = pltpu.PrefetchScalarGridSpec(
        num_scalar_prefetch=1, grid=(r // tr,), in_specs=[pl.BlockSpec((tr, cc), lambda i, s: (i, 0))],
        out_specs=pl.BlockSpec((tr, cc), out_map))
    return _pallas(body, name=name, grid_spec=grid_spec, out_shape=jax.ShapeDtypeStruct(full, BF16),
                   compiler_params=_params("arbitrary"))(jnp.reshape(chip, (1,)).astype(jnp.int32), w)


def _gather_copy(ref, region, k, a, to, sems):
    return pltpu.make_async_remote_copy(src_ref=region, dst_ref=region, send_sem=sems[0].at[a, k], recv_sem=sems[1].at[a, k],
                                        device_id=to, device_id_type=MESH)


def _gather_send(refs, shapes, axes, *sems):
    x, y, c = _place()
    for a in range(len(refs)):
        mine = _shard_slice(refs[a], shapes[a], axes[a], 2 * x + y, c)
        _gather_copy(refs[a], mine, 0, a, (1 - x, y, c), sems).start()
        _gather_copy(refs[a], mine, 1, a, (x, 1 - y, c), sems).start()


def _gather_forward(refs, shapes, axes, *sems):
    x, y, c = _place()
    me, sibling, x_nbr, y_nbr = (x, y, c), (x, y, 1 - c), (1 - x, y, c), (x, 1 - y, c)
    chip_x, chip_y, chip_d = 2 * (1 - x) + y, 2 * x + (1 - y), 2 * (1 - x) + (1 - y)
    for a in range(len(refs)):
        region = lambda chip, quarter=None, a=a: _shard_slice(refs[a], shapes[a], axes[a], chip, c, quarter)
        _gather_copy(refs[a], region(chip_x), 0, a, me, sems).wait_recv()
        _gather_copy(refs[a], region(chip_x, 1), 3, a, y_nbr, sems).start()
        _gather_copy(refs[a], region(chip_x), 4, a, sibling, sems).start()
        _gather_copy(refs[a], region(chip_y), 1, a, me, sems).wait_recv()
        _gather_copy(refs[a], region(chip_y, 0), 2, a, x_nbr, sems).start()
        _gather_copy(refs[a], region(chip_y), 5, a, sibling, sems).start()
    for a in range(len(refs)):
        region = lambda chip, quarter=None, a=a: _shard_slice(refs[a], shapes[a], axes[a], chip, c, quarter)
        _gather_copy(refs[a], region(chip_d, 0), 2, a, me, sems).wait_recv()
        _gather_copy(refs[a], region(chip_d, 1), 3, a, me, sems).wait_recv()
        _gather_copy(refs[a], region(chip_d), 6, a, sibling, sems).start()


def _gather_finish(refs, shapes, axes, *sems):
    x, y, c = _place()
    me = (x, y, c)
    chip_me, chip_x, chip_y, chip_d = 2 * x + y, 2 * (1 - x) + y, 2 * x + (1 - y), 2 * (1 - x) + (1 - y)
    for a in range(len(refs)):
        region = lambda chip, half, quarter=None, a=a: _shard_slice(refs[a], shapes[a], axes[a], chip, half, quarter)
        for k, chip in ((4, chip_x), (5, chip_y), (6, chip_d)):
            _gather_copy(refs[a], region(chip, 1 - c), k, a, me, sems).wait_recv()
            _gather_copy(refs[a], region(chip, c), k, a, me, sems).wait_send()
        _gather_copy(refs[a], region(chip_me, c), 0, a, me, sems).wait_send()
        _gather_copy(refs[a], region(chip_me, c), 1, a, me, sems).wait_send()
        _gather_copy(refs[a], region(chip_y, c, 0), 2, a, me, sems).wait_send()
        _gather_copy(refs[a], region(chip_x, c, 1), 3, a, me, sems).wait_send()


def _gather_sems(n_w):
    return [pltpu.SemaphoreType.DMA((n_w, 7)), pltpu.SemaphoreType.DMA((n_w, 7))]


def _gather_weights(fulls, axes, c_taps, w_ada_bf, b_part, name):
    n_w = len(fulls)
    shapes = [f.shape for f in fulls]
    r_ct = c_taps.shape[0]
    mq = w_ada_bf.shape[1]

    def body(*refs):
        ct_ref, wada_ref, b_ref = refs[n_w:n_w + 3]
        outs = refs[n_w + 3:2 * n_w + 3]
        ct_all_ref, mod_all_ref, mod_part = refs[2 * n_w + 3:2 * n_w + 6]
        sems, ct_sems, mod_sems = refs[2 * n_w + 6:2 * n_w + 8], refs[2 * n_w + 8:2 * n_w + 11], refs[2 * n_w + 11:]
        _gather_send(outs, shapes, axes, *sems)
        _small_gather(ct_ref, ct_all_ref, *ct_sems)
        cv = jnp.concatenate([ct_all_ref[r_ct * dv:r_ct * dv + 1, :] for dv in range(N_DEV)], axis=0)
        mod_part[...] = _dot((cv * _sig(cv)).astype(BF16), wada_ref[...]) + b_ref[...]
        _small_gather(mod_part, mod_all_ref, *mod_sems)
        _gather_forward(outs, shapes, axes, *sems)
        _gather_finish(outs, shapes, axes, *sems)

    res = _pallas(
        body, name=name,
        out_shape=tuple(jax.ShapeDtypeStruct(f.shape, BF16) for f in fulls)
        + (jax.ShapeDtypeStruct((N_DEV * r_ct, c_taps.shape[1]), F32), jax.ShapeDtypeStruct((N_DEV * N_DEV, mq), F32)),
        in_specs=[ANY] * n_w + [VMEM_FULL] * 3, out_specs=tuple([ANY] * n_w + [VMEM_FULL] * 2),
        input_output_aliases={a: a for a in range(n_w)},
        scratch_shapes=[pltpu.VMEM((N_DEV, mq), F32)] + _gather_sems(n_w) + _small_gather_sems() + _small_gather_sems(),
        compiler_params=_params(),
    )(*fulls, c_taps, w_ada_bf, b_part)
    return res[:n_w], res[n_w], res[n_w + 1]


def _piece_shapes(grads, axes):
    return [((g.shape[0] // N_CHIP // 2, g.shape[1]) if ax == 0 else (g.shape[0] // 2, g.shape[1] // N_CHIP))
            for g, ax in zip(grads, axes)]


def _exchange_send(ins, outs, shapes, axes, send_sems, recv_sems):
    x, y, c = _place()
    me = 4 * x + 2 * y + c
    for a in range(len(ins)):
        for t in range(N_DEV):
            tx, ty, tc = t // 4, (t // 2) % 2, t % 2
            src = _shard_slice(ins[a], shapes[a], axes[a], 2 * tx + ty, tc)

            @pl.when(me != t)
            def _(src=src, a=a, t=t, to=(tx, ty, tc)):
                slot = (me - t + N_DEV) % N_DEV - 1
                pltpu.make_async_remote_copy(src_ref=src, dst_ref=outs[a].at[slot], send_sem=send_sems.at[a, t],
                                             recv_sem=recv_sems.at[a, slot], device_id=to, device_id_type=MESH).start()


def _exchange_finish(outs, send_sems, recv_sems):
    x, y, c = _place()
    me = 4 * x + 2 * y + c
    for a in range(len(outs)):
        for t in range(N_DEV):
            @pl.when(me != t)
            def _(a=a, t=t):
                slot = outs[a].at[0]
                pltpu.make_async_remote_copy(src_ref=slot, dst_ref=slot, send_sem=send_sems.at[a, t], recv_sem=recv_sems.at[a, 0],
                                             device_id=(x, y, c), device_id_type=MESH).wait_send()
        for k in range(N_DEV - 1):
            slot = outs[a].at[k]
            pltpu.make_async_remote_copy(src_ref=slot, dst_ref=slot, send_sem=send_sems.at[a, 0], recv_sem=recv_sems.at[a, k],
                                         device_id=(x, y, c), device_id_type=MESH).wait_recv()


def _exchange_sems(n_w):
    return [pltpu.SemaphoreType.DMA((n_w, N_DEV)), pltpu.SemaphoreType.DMA((n_w, N_DEV - 1))]


def _exchange_out_shapes(grads, axes):
    return tuple(jax.ShapeDtypeStruct((N_DEV - 1,) + p, BF16) for p in _piece_shapes(grads, axes))


def _swap_halves(fulls, block):
    n_w = len(fulls)

    def body(*refs):
        block_ref = refs[n_w]
        outs, gathered_ref = refs[n_w + 1:2 * n_w + 1], refs[2 * n_w + 1]
        send_sems, recv_sems = refs[2 * n_w + 2:2 * n_w + 4]
        x, y, c = _place()
        sibling = (x, y, 1 - c)
        waits = []
        for a in range(n_w):
            r = fulls[a].shape[0] // 2
            mine = outs[a].at[pl.ds(c * r, r), :]
            theirs = outs[a].at[pl.ds((1 - c) * r, r), :]
            rc = pltpu.make_async_remote_copy(src_ref=mine, dst_ref=mine, send_sem=send_sems.at[a], recv_sem=recv_sems.at[a],
                                              device_id=sibling, device_id_type=MESH)
            rc.start()
            got = pltpu.make_async_remote_copy(src_ref=theirs, dst_ref=theirs, send_sem=send_sems.at[a],
                                               recv_sem=recv_sems.at[a], device_id=sibling, device_id_type=MESH)
            waits += [rc.wait_send, got.wait_recv]
        _small_gather(block_ref, gathered_ref, *refs[2 * n_w + 4:])
        for w in waits:
            w()

    res = _pallas(
        body, name="swap_halves",
        out_shape=tuple(jax.ShapeDtypeStruct(f.shape, f.dtype) for f in fulls)
        + (jax.ShapeDtypeStruct((N_DEV * block.shape[0], block.shape[1]), block.dtype),),
        in_specs=[ANY] * n_w + [VMEM_FULL], out_specs=tuple([ANY] * n_w + [VMEM_FULL]),
        input_output_aliases={a: a for a in range(n_w)},
        scratch_shapes=[pltpu.SemaphoreType.DMA((n_w,)), pltpu.SemaphoreType.DMA((n_w,))] + _small_gather_sems(),
    )(*fulls, block)
    return res[:n_w], res[n_w]


def _load_weights(pairs, sems):
    @pl.when(pl.program_id(0) == 0)
    def _():
        cps = [pltpu.make_async_copy(s, d, sems.at[k]) for k, (s, d) in enumerate(pairs)]
        for cp in cps:
            cp.start()
        for cp in cps:
            cp.wait()


def _conv_chunks(ext, e_buf, ts, d, chunk_fn, unroll, between=()):
    rows, dh = e_buf.shape[1], e_buf.shape[2]
    for hf in range(d // dh):
        for r in range(1, 8):
            e_buf[r - 1, :, :] = ext[r:r + rows, hf * dh:(hf + 1) * dh]
        for lq in range(dh // CONV_LANES):
            lo, el = hf * dh + lq * CONV_LANES, lq * CONV_LANES

            def body(ci, carry, lo=lo, el=el):
                r0 = pl.multiple_of(ci * CONV_ROWS, CONV_ROWS)

                def tap(off):
                    q, r = divmod(off, 8)
                    if r == 0:
                        return ext[pl.ds(r0 + 8 * q, CONV_ROWS), lo:lo + CONV_LANES]
                    return e_buf[r - 1, pl.ds(r0 + 8 * q, CONV_ROWS), el:el + CONV_LANES]

                chunk_fn(tap, r0, lo)
                return carry

            lax.fori_loop(0, ts // CONV_ROWS, body, 0, unroll=unroll)
            if between:
                between[0]()
                between = between[1:]
    for fn in between:
        fn()


def _fwd_mix(x, mod, gn1, w3, w31, cbias, lng, lnb, w_in, w_a, w_b, w_o, later, later_axes, ts):
    s_len, d = x.shape
    n_t = s_len // ts
    n_l = len(later)
    later_shapes = [w.shape for w in later]

    def body(*refs):
        (x_ref, mod_ref, gn1_ref, w3_ref, w31_ref, cb_ref, lng_ref, lnb_ref, win_hbm, wa_hbm, wb_hbm, wo_hbm) = refs[:12]
        (proj_ref, h_ref, yap_ref, c3_ref, n_ref, rstd_ref, u3_ref, ya_ref, yb_ref, mg_ref, mix_ref,
         x1_ref) = refs[12 + n_l:24 + n_l]
        later_refs = refs[24 + n_l:24 + 2 * n_l]
        win_v, wa_v, wb_v, wo_v, cv_ext, u0_ext, e_buf, u1_buf, sems, g_send, g_recv = refs[24 + 2 * n_l:]
        step = pl.program_id(0)
        pl.when(step == 0)(lambda: _gather_send(later_refs, later_shapes, later_axes, g_send, g_recv))
        pl.when(step == n_t // 2)(lambda: _gather_forward(later_refs, later_shapes, later_axes, g_send, g_recv))
        pl.when(step == n_t - 1)(lambda: _gather_finish(later_refs, later_shapes, later_axes, g_send, g_recv))
        _load_weights([(win_hbm, win_v), (wa_hbm, wa_v), (wb_hbm, wb_v), (wo_hbm, wo_v)], sems)

        @pl.when(pl.program_id(0) == 0)
        def _():
            cv_ext[0:HALO_SHORT, :] = jnp.zeros((HALO_SHORT, d), F32)
            u0_ext[0:HALO_CONF, :] = jnp.zeros((HALO_CONF, d), F32)

        xv = x_ref[...]
        sh1, sc1, g1 = mod_ref[:, 0:d], mod_ref[:, d:2 * d], mod_ref[:, 2 * d:3 * d]
        r = lax.rsqrt(jnp.mean(xv * xv, axis=-1, keepdims=True) + EPS)
        hb = (((xv * r) * gn1_ref[...]) * (1.0 + sc1) + sh1).astype(BF16)
        h_ref[...] = hb

        def pj(k):
            p = _dot(hb, win_v[:, k * d:(k + 1) * d])
            proj_ref[:, k * d:(k + 1) * d] = p.astype(BF16)
            return p

        v_c = pj(3)
        u0_ext[HALO_CONF:HALO_CONF + ts, :] = v_c * _sig(pj(4))

        def conv_chunk(tap, r0, lo):
            acc = jnp.zeros((CONV_ROWS, CONV_LANES), F32)
            for k in range(CONF_K):
                acc = acc + w31_ref[k:k + 1, lo:lo + CONV_LANES] * tap(HALO_CONF - (CONF_K - 1) + k)
            u1_buf[pl.ds(r0, CONV_ROWS), lo:lo + CONV_LANES] = acc

        pjs = {}
        _conv_chunks(u0_ext, e_buf, ts, d, conv_chunk, unroll=True,
                     between=[lambda k=k: pjs.__setitem__(k, pj(k)) for k in (1, 2, 0, 5, 6)])
        u0_ext[0:HALO_CONF, :] = u0_ext[ts:ts + HALO_CONF, :]

        cv_ext[HALO_SHORT:HALO_SHORT + ts, :] = pjs[1] * pjs[2]
        conv3 = jnp.zeros((ts, d), F32)
        for k in range(SHORT_K):
            off = HALO_SHORT - (SHORT_K - 1) + k
            conv3 = conv3 + w3_ref[k:k + 1, :] * cv_ext[off:off + ts, :]
        cv_ext[0:HALO_SHORT, :] = cv_ext[ts:ts + HALO_SHORT, :]
        c3_ref[...] = conv3.astype(BF16)
        yap = (pjs[0] * conv3).astype(BF16)
        yap_ref[...] = yap
        y_a = _dot(yap, wa_v[...])
        ya_ref[...] = y_a.astype(BF16)

        u1 = u1_buf[...] + cb_ref[...]
        mu = jnp.mean(u1, axis=-1, keepdims=True)
        var = jnp.mean(jnp.square(u1 - mu), axis=-1, keepdims=True)
        rstd = lax.rsqrt(var + LN_EPS)
        nrm = (u1 - mu) * rstd
        n_ref[...] = nrm.astype(BF16)
        rstd_ref[...] = jnp.broadcast_to(rstd, (ts, LANES))
        u2 = nrm * lng_ref[...] + lnb_ref[...]
        u3 = (u2 * _sig(u2)).astype(BF16)
        u3_ref[...] = u3
        y_b = _dot(u3, wb_v[...])
        yb_ref[...] = y_b.astype(BF16)

        merged = (_sig(pjs[5]) * y_a + _sig(pjs[6]) * y_b).astype(BF16)
        mg_ref[...] = merged
        mix = _dot(merged, wo_v[...])
        mix_ref[...] = mix
        x1_ref[...] = xv + g1 * mix

    row = lambda w, dt=None: pl.BlockSpec((ts, w), lambda i: (i, 0))
    full = lambda a: pl.BlockSpec(a.shape, lambda i: (0, 0))
    sd = lambda w, dt: jax.ShapeDtypeStruct((s_len, w), dt)
    outs = [sd(7 * d, BF16), sd(d, BF16), sd(d, BF16), sd(d, BF16), sd(d, BF16), sd(LANES, F32), sd(d, BF16), sd(d, BF16),
            sd(d, BF16), sd(d, BF16), sd(d, F32), sd(d, F32)]
    res = _pallas(
        body, name="fwd_mix", grid=(n_t,),
        out_shape=tuple(outs) + tuple(jax.ShapeDtypeStruct(w.shape, BF16) for w in later),
        in_specs=[row(d), full(mod), full(gn1), full(w3), full(w31), full(cbias), full(lng), full(lnb), ANY, ANY, ANY, ANY]
        + [ANY] * n_l,
        out_specs=tuple(row(o.shape[1]) for o in outs) + tuple([ANY] * n_l),
        input_output_aliases={12 + k: len(outs) + k for k in range(n_l)},
        scratch_shapes=[pltpu.VMEM(w_in.shape, BF16), pltpu.VMEM(w_a.shape, BF16), pltpu.VMEM(w_b.shape, BF16),
                        pltpu.VMEM(w_o.shape, BF16), pltpu.VMEM((ts + HALO_SHORT, d), F32),
                        pltpu.VMEM((ts + HALO_CONF, d), F32), pltpu.VMEM((7, ts + HALO_CONF - 8, d // 2), F32),
                        pltpu.VMEM((ts, d), F32), pltpu.SemaphoreType.DMA((4,))] + _gather_sems(n_l),
        compiler_params=_params("arbitrary"),
    )(x, mod, gn1, w3, w31, cbias, lng, lnb, w_in, w_a, w_b, w_o, *later)
    return res[:len(outs)], res[len(outs):]


def _ffn(x1, target, mod, gn2, gf, w_fi, w_fo, ts):
    s_len, d = x1.shape
    f = w_fo.shape[0]
    n_t = s_len // ts

    def body(x1_ref, t_ref, mod_ref, gn2_ref, gf_ref, wfi_hbm, wfo_hbm,
             h2_ref, hid_ref, df_ref, dab_ref, dx1_ref, acc_ref, wfi_v, wfo_v, sems):
        _load_weights([(wfi_hbm, wfi_v), (wfo_hbm, wfo_v)], sems)

        @pl.when(pl.program_id(0) == 0)
        def _():
            acc_ref[...] = jnp.zeros(acc_ref.shape, F32)

        x1v = x1_ref[...]
        sh2, sc2, g2 = mod_ref[:, 3 * d:4 * d], mod_ref[:, 4 * d:5 * d], mod_ref[:, 5 * d:6 * d]
        gn2v, gfv = gn2_ref[...], gf_ref[...]
        r2 = lax.rsqrt(jnp.mean(x1v * x1v, axis=-1, keepdims=True) + EPS)
        xn2 = x1v * r2
        h2 = ((xn2 * gn2v) * (1.0 + sc2) + sh2).astype(BF16)
        h2_ref[...] = h2
        a = _dot(h2, wfi_v[:, 0:f])
        bg = _dot(h2, wfi_v[:, f:2 * f])
        sa = _sig(a)
        silu_a = a * sa
        hid = (silu_a * bg).astype(BF16)
        hid_ref[...] = hid
        ffn = _dot(hid, wfo_v[...])
        x2 = x1v + g2 * ffn
        r3 = lax.rsqrt(jnp.mean(x2 * x2, axis=-1, keepdims=True) + EPS)
        xn3 = x2 * r3
        diff = xn3 * gfv - t_ref[...]
        dy = diff * (1.0 / d)
        dxn3 = dy * gfv
        dx2 = r3 * (dxn3 - xn3 * jnp.mean(dxn3 * xn3, axis=-1, keepdims=True))
        acc_ref[ROW_GF:ROW_GF + 1, :] += _colsum(dy * xn3)
        acc_ref[ROW_DMOD + 5:ROW_DMOD + 5 + 1, :] += _colsum(dx2 * ffn)
        acc_ref[ROW_SQERR:ROW_SQERR + 1, :] += _colsum(diff * diff)

        df = (dx2 * g2).astype(BF16)
        df_ref[...] = df
        dhid = _dot_nt(df, wfo_v[...])
        da = dhid * bg * (sa * (1.0 + a * (1.0 - sa)))
        dbg = dhid * silu_a
        dab_ref[:, 0:f] = da.astype(BF16)
        dab_ref[:, f:2 * f] = dbg.astype(BF16)
        dh2 = _dot_nt(da.astype(BF16), wfi_v[:, 0:f]) + _dot_nt(dbg.astype(BF16), wfi_v[:, f:2 * f])
        acc_ref[ROW_DMOD + 3:ROW_DMOD + 3 + 1, :] += _colsum(dh2)
        acc_ref[ROW_DMOD + 4:ROW_DMOD + 4 + 1, :] += _colsum(dh2 * (xn2 * gn2v))
        acc_ref[ROW_GN2:ROW_GN2 + 1, :] += _colsum(dh2 * xn2 * (1.0 + sc2))
        dxn2 = dh2 * gn2v * (1.0 + sc2)
        dx1_ref[...] = dx2 + r2 * (dxn2 - xn2 * jnp.mean(dxn2 * xn2, axis=-1, keepdims=True))

    row = lambda w: pl.BlockSpec((ts, w), lambda i: (i, 0))
    full = lambda a: pl.BlockSpec(a.shape, lambda i: (0, 0))
    sd = lambda w, dt: jax.ShapeDtypeStruct((s_len, w), dt)
    outs = [sd(d, BF16), sd(f, BF16), sd(d, BF16), sd(2 * f, BF16), sd(d, F32), jax.ShapeDtypeStruct((N_SMALL_ROWS, d), F32)]
    return _pallas(
        body, name="ffn", grid=(n_t,), out_shape=tuple(outs),
        in_specs=[row(d), row(d), full(mod), full(gn2), full(gf), ANY, ANY],
        out_specs=tuple([row(o.shape[1]) for o in outs[:-1]] + [pl.BlockSpec((N_SMALL_ROWS, d), lambda i: (0, 0))]),
        scratch_shapes=[pltpu.VMEM(w_fi.shape, BF16), pltpu.VMEM(w_fo.shape, BF16), pltpu.SemaphoreType.DMA((2,))],
        compiler_params=_params("arbitrary"),
    )(x1, target, mod, gn2, gf, w_fi, w_fo)


def _bwd_mix(dx1, proj, nrm, rstd, c3, ya, yb, mix, mod, w3, w31, lng, lnb, w_a, w_b, w_o, ready, ready_axes, ts):
    s_len, d = dx1.shape
    n_t = s_len // ts
    n_r = len(ready)
    ready_shapes = [g.shape for g in ready]

    def body(*refs):
        (dx1_ref, proj_ref, n_ref, rstd_ref, c3_ref, ya_ref, yb_ref, mix_ref, mod_ref, w3_ref, w31_ref, lng_ref,
         lnb_ref, wa_hbm, wb_hbm, wo_hbm) = refs[:16]
        ready_refs = refs[16:16 + n_r]
        dproj_ref, dmix_ref, dya_ref, dyb_ref, acc_ref = refs[16 + n_r:21 + n_r]
        recv_refs = refs[21 + n_r:21 + 2 * n_r]
        wa_v, wb_v, wo_v, dc3_ext, du1_ext, e_buf, u0_buf, du0_buf, dw8, sems, e_send, e_recv = refs[21 + 2 * n_r:]
        step = pl.program_id(0)
        pl.when(step == 0)(lambda: _exchange_send(ready_refs, recv_refs, ready_shapes, ready_axes, e_send, e_recv))
        pl.when(step == n_t - 1)(lambda: _exchange_finish(recv_refs, e_send, e_recv))
        _load_weights([(wa_hbm, wa_v), (wb_hbm, wb_v), (wo_hbm, wo_v)], sems)

        @pl.when(pl.program_id(0) == 0)
        def _():
            acc_ref[...] = jnp.zeros(acc_ref.shape, F32)
            dc3_ext[ts:ts + HALO_SHORT, :] = jnp.zeros((HALO_SHORT, d), F32)
            du1_ext[ts:ts + HALO_CONF, :] = jnp.zeros((HALO_CONF, d), F32)
            dw8[...] = jnp.zeros(dw8.shape, F32)

        pj = lambda k: proj_ref[:, k * d:(k + 1) * d].astype(F32)
        g1 = mod_ref[:, 2 * d:3 * d]
        dx1v = dx1_ref[...]
        acc_ref[ROW_DMOD + 2:ROW_DMOD + 2 + 1, :] += _colsum(dx1v * mix_ref[...])
        dmix = (dx1v * g1).astype(BF16)
        dmix_ref[...] = dmix
        dmerged = _dot_nt(dmix, wo_v[...])
        g_a, g_b = _sig(pj(5)), _sig(pj(6))
        y_a, y_b = ya_ref[...].astype(F32), yb_ref[...].astype(F32)
        dproj_ref[:, 5 * d:6 * d] = (dmerged * y_a * g_a * (1.0 - g_a)).astype(BF16)
        dproj_ref[:, 6 * d:7 * d] = (dmerged * y_b * g_b * (1.0 - g_b)).astype(BF16)
        dya = (dmerged * g_a).astype(BF16)
        dyb = (dmerged * g_b).astype(BF16)
        dya_ref[...] = dya
        dyb_ref[...] = dyb

        dyap = _dot_nt(dya, wa_v[...])
        dproj_ref[:, 0:d] = (dyap * c3_ref[...].astype(F32)).astype(BF16)
        dc3_ext[0:ts, :] = dyap * pj(0)
        c_s, v_s = pj(1), pj(2)
        cv = c_s * v_s
        dcv = jnp.zeros((ts, d), F32)
        for j in range(SHORT_K):
            k = SHORT_K - 1 - j
            sj = dc3_ext[j:j + ts, :]
            dcv = dcv + w3_ref[k:k + 1, :] * sj
            acc_ref[ROW_W3 + k:ROW_W3 + k + 1, :] += _colsum(cv * sj)
        dc3_ext[ts:ts + HALO_SHORT, :] = dc3_ext[0:HALO_SHORT, :]
        dproj_ref[:, d:2 * d] = (dcv * v_s).astype(BF16)
        dproj_ref[:, 2 * d:3 * d] = (dcv * c_s).astype(BF16)

        du3 = _dot_nt(dyb, wb_v[...])
        nv = n_ref[...].astype(F32)
        lg = lng_ref[...]
        u2 = nv * lg + lnb_ref[...]
        s2 = _sig(u2)
        du2 = du3 * (s2 * (1.0 + u2 * (1.0 - s2)))
        acc_ref[ROW_LNG:ROW_LNG + 1, :] += _colsum(du2 * nv)
        acc_ref[ROW_LNB:ROW_LNB + 1, :] += _colsum(du2)
        dn = du2 * lg
        du1 = rstd_ref[:, 0:1] * (dn - jnp.mean(dn, axis=-1, keepdims=True) - nv * jnp.mean(dn * nv, axis=-1, keepdims=True))
        acc_ref[ROW_CB:ROW_CB + 1, :] += _colsum(du1)
        du1_ext[0:ts, :] = du1
        v_c = pj(3)
        sg = _sig(pj(4))
        u0_buf[...] = v_c * sg

        def conv_chunk(tap, r0, lo):
            lanes = slice(lo, lo + CONV_LANES)
            u0c = u0_buf[pl.ds(r0, CONV_ROWS), lanes]
            acc = jnp.zeros((CONV_ROWS, CONV_LANES), F32)
            for j in range(CONF_K):
                k = CONF_K - 1 - j
                sj = tap(j)
                acc = acc + w31_ref[k:k + 1, lanes] * sj
                prod = u0c * sj
                part = prod[0:8]
                for i in range(1, CONV_ROWS // 8):
                    part = part + prod[8 * i:8 * i + 8]
                dw8[k, :, lanes] += part
            du0_buf[pl.ds(r0, CONV_ROWS), lanes] = acc

        _conv_chunks(du1_ext, e_buf, ts, d, conv_chunk, unroll=False)
        du0 = du0_buf[...]
        du1_ext[ts:ts + HALO_CONF, :] = du1_ext[0:HALO_CONF, :]

        @pl.when(step == n_t - 1)
        def _():
            for k in range(CONF_K):
                acc_ref[ROW_W31 + k:ROW_W31 + k + 1, :] = _colsum(dw8[k])
        dproj_ref[:, 3 * d:4 * d] = (du0 * sg).astype(BF16)
        dproj_ref[:, 4 * d:5 * d] = (du0 * v_c * sg * (1.0 - sg)).astype(BF16)

    rev = lambda w: pl.BlockSpec((ts, w), lambda i: (n_t - 1 - i, 0))
    full = lambda a: pl.BlockSpec(a.shape, lambda i: (0, 0))
    sd = lambda w, dt: jax.ShapeDtypeStruct((s_len, w), dt)
    outs = [sd(7 * d, BF16), sd(d, BF16), sd(d, BF16), sd(d, BF16), jax.ShapeDtypeStruct((N_SMALL_ROWS, d), F32)]
    res = _pallas(
        body, name="bwd_mix", grid=(n_t,), out_shape=tuple(outs) + _exchange_out_shapes(ready, ready_axes),
        in_specs=[rev(d), rev(7 * d), rev(d), rev(LANES), rev(d), rev(d), rev(d), rev(d), full(mod), full(w3), full(w31),
                  full(lng), full(lnb), ANY, ANY, ANY] + [ANY] * n_r,
        out_specs=tuple([rev(o.shape[1]) for o in outs[:-1]] + [pl.BlockSpec((N_SMALL_ROWS, d), lambda i: (0, 0))] + [ANY] * n_r),
        scratch_shapes=[pltpu.VMEM(w_a.shape, BF16), pltpu.VMEM(w_b.shape, BF16), pltpu.VMEM(w_o.shape, BF16),
                        pltpu.VMEM((ts + HALO_SHORT, d), F32), pltpu.VMEM((ts + HALO_CONF, d), F32),
                        pltpu.VMEM((7, ts + HALO_CONF - 8, d // 2), F32), pltpu.VMEM((ts, d), F32), pltpu.VMEM((ts, d), F32),
                        pltpu.VMEM((CONF_K, 8, d), F32), pltpu.SemaphoreType.DMA((3,))] + _exchange_sems(n_r),
        compiler_params=_params("arbitrary"),
    )(dx1, proj, nrm, rstd, c3, ya, yb, mix, mod, w3, w31, lng, lnb, w_a, w_b, w_o, *ready)
    return res[:len(outs)], res[len(outs):]


def _bwd_in(dproj, x, dx1, mod, gn1, w_in, ready, ready_axes, ts):
    s_len, d = x.shape
    n_t = s_len // ts
    n_r = len(ready)
    ready_shapes = [g.shape for g in ready]

    def body(*refs):
        dproj_ref, x_ref, dx1_ref, mod_ref, gn1_ref, win_hbm = refs[:6]
        ready_refs = refs[6:6 + n_r]
        gx_ref, acc_ref = refs[6 + n_r:8 + n_r]
        recv_refs = refs[8 + n_r:8 + 2 * n_r]
        win_v, sems, e_send, e_recv = refs[8 + 2 * n_r:]
        step = pl.program_id(0)
        pl.when(step == 0)(lambda: _exchange_send(ready_refs, recv_refs, ready_shapes, ready_axes, e_send, e_recv))
        pl.when(step == n_t - 1)(lambda: _exchange_finish(recv_refs, e_send, e_recv))
        _load_weights([(win_hbm, win_v)], sems)

        @pl.when(pl.program_id(0) == 0)
        def _():
            acc_ref[...] = jnp.zeros(acc_ref.shape, F32)

        xv = x_ref[...]
        sc1 = mod_ref[:, d:2 * d]
        gn1v = gn1_ref[...]
        r = lax.rsqrt(jnp.mean(xv * xv, axis=-1, keepdims=True) + EPS)
        xn = xv * r
        dh = _dot_nt(dproj_ref[...], win_v[...])
        acc_ref[ROW_DMOD:ROW_DMOD + 1, :] += _colsum(dh)
        acc_ref[ROW_DMOD + 1:ROW_DMOD + 1 + 1, :] += _colsum(dh * (xn * gn1v))
        acc_ref[ROW_GN1:ROW_GN1 + 1, :] += _colsum(dh * xn * (1.0 + sc1))
        dxn = dh * gn1v * (1.0 + sc1)
        gx_ref[...] = dx1_ref[...] + r * (dxn - xn * jnp.mean(dxn * xn, axis=-1, keepdims=True))

    row = lambda w: pl.BlockSpec((ts, w), lambda i: (i, 0))
    full = lambda a: pl.BlockSpec(a.shape, lambda i: (0, 0))
    res = _pallas(
        body, name="bwd_in", grid=(n_t,),
        out_shape=(jax.ShapeDtypeStruct((s_len, d), F32), jax.ShapeDtypeStruct((N_SMALL_ROWS, d), F32))
        + _exchange_out_shapes(ready, ready_axes),
        in_specs=[row(7 * d), row(d), row(d), full(mod), full(gn1), ANY] + [ANY] * n_r,
        out_specs=(row(d), pl.BlockSpec((N_SMALL_ROWS, d), lambda i: (0, 0))) + tuple([ANY] * n_r),
        scratch_shapes=[pltpu.VMEM(w_in.shape, BF16), pltpu.SemaphoreType.DMA((1,))] + _exchange_sems(n_r),
        compiler_params=_params("arbitrary"),
    )(dproj, x, dx1, mod, gn1, w_in, *ready)
    return res[0], res[1], res[2:]


def _weight_grad(a, b, bn, name, transposed=False, ready=(), ready_axes=()):
    s_len, m = a.shape
    n = b.shape[1]
    n_j = n // bn
    n_r = len(ready)
    ready_shapes = [g.shape for g in ready]

    def body(*refs):
        a_hbm, b_ref = refs[:2]
        ready_refs = refs[2:2 + n_r]
        o_ref = refs[2 + n_r]
        recv_refs = refs[3 + n_r:3 + 2 * n_r]
        a_v, sem = refs[3 + 2 * n_r:5 + 2 * n_r]
        step = pl.program_id(0)
        if n_r:
            e_send, e_recv = refs[5 + 2 * n_r:]
            pl.when(step == 0)(lambda: _exchange_send(ready_refs, recv_refs, ready_shapes, ready_axes, e_send, e_recv))
            pl.when(step == n_j - 1)(lambda: _exchange_finish(recv_refs, e_send, e_recv))
        _load_weights([(a_hbm, a_v)], sem)
        prod = _dot_tn(a_v[...], b_ref[...])
        o_ref[...] = (prod.T if transposed else prod).astype(BF16)

    out = jax.ShapeDtypeStruct((n, m) if transposed else (m, n), BF16)
    out_spec = pl.BlockSpec((bn, m), lambda j: (j, 0)) if transposed else pl.BlockSpec((m, bn), lambda j: (0, j))
    res = _pallas(
        body, name=name, grid=(n_j,), out_shape=(out,) + _exchange_out_shapes(ready, ready_axes),
        in_specs=[ANY, pl.BlockSpec((s_len, bn), lambda j: (0, j))] + [ANY] * n_r,
        out_specs=(out_spec,) + tuple([ANY] * n_r),
        scratch_shapes=[pltpu.VMEM(a.shape, BF16), pltpu.SemaphoreType.DMA((1,))] + (_exchange_sems(n_r) if n_r else []),
        compiler_params=_params("arbitrary"),
    )(a, b, *ready)
    return res[0], res[1:]


def _sum_slots(recv, grad, axis, chip, core, name):
    n_s, r, c = recv.shape
    tr = 16
    for cand in (256, 128, 176, 64, 32, 16):
        if r % cand == 0:
            tr = cand
            break
    n_i = r // tr

    def body(s_ref, own_ref, r_ref, o_ref):
        tot = own_ref[...].astype(F32)
        for k in range(n_s):
            tot = tot + r_ref[k].astype(F32)
        o_ref[...] = tot

    if axis == 0:
        own_map = lambda i, s: ((2 * s[0] + s[1]) * n_i + i, 0)
    else:
        own_map = lambda i, s: (s[1] * n_i + i, s[0])
    grid_spec = pltpu.PrefetchScalarGridSpec(
        num_scalar_prefetch=1, grid=(n_i,),
        in_specs=[pl.BlockSpec((tr, c), own_map), pl.BlockSpec((n_s, tr, c), lambda i, s: (0, i, 0))],
        out_specs=pl.BlockSpec((tr, c), lambda i, s: (s[1] * n_i + i, 0)))
    return _pallas(body, name=name, grid_spec=grid_spec, out_shape=jax.ShapeDtypeStruct((2 * r, c), F32),
                   compiler_params=_params("arbitrary"))(jnp.stack([chip, core]).astype(jnp.int32), grad, recv)


def _adamw_math(w, g, m, v):
    m = ADAM_B1 * m + (1.0 - ADAM_B1) * g
    v = ADAM_B2 * v + (1.0 - ADAM_B2) * (g * g)
    m_hat = m / (1.0 - ADAM_B1 ** ADAM_STEP)
    v_hat = v / (1.0 - ADAM_B2 ** ADAM_STEP)
    delta = -ADAM_LR * (m_hat / (jnp.sqrt(v_hat) + ADAM_EPS) + ADAM_WD * w)
    return delta, m, v


def _adamw(w, g, m, v, name):
    r, c = w.shape
    tr = r
    for cand in (256, 128, 176, 64, 32, 16, 8):
        if r % cand == 0:
            tr = cand
            break

    def body(w_ref, g_ref, m_ref, v_ref, go_ref, d_ref, nm_ref, nv_ref):
        g = g_ref[...]
        go_ref[...] = g
        d_ref[...], nm_ref[...], nv_ref[...] = _adamw_math(w_ref[...], g, m_ref[...], v_ref[...])

    blk = pl.BlockSpec((tr, c), lambda i: (i, 0))
    return _pallas(body, name=name, grid=(r // tr,), out_shape=tuple(jax.ShapeDtypeStruct((r, c), F32) for _ in range(4)),
                   in_specs=[blk] * 4, out_specs=(blk,) * 4, compiler_params=_params("arbitrary"))(w, g, m, v)


_SMALL = [
    ("b_ada", ROW_DMOD, "mod"), ("norm_mix_g", ROW_GN1, "row"), ("conv_short_w", ROW_W3, "shard"),
    ("conv_conf_w", ROW_W31, "shard"), ("conv_conf_b", ROW_CB, "row"), ("conf_ln_g", ROW_LNG, "row"),
    ("conf_ln_b", ROW_LNB, "row"), ("norm_ffn_g", ROW_GN2, "row"), ("final_norm_g", ROW_GF, "row")]


def _small_update(p_all, chip_onehot, ws, ms, vs):
    n = len(_SMALL)
    d = p_all.shape[2]

    def body(*refs):
        p_ref, oh_ref = refs[:2]
        w_refs, m_refs, v_refs = refs[2:2 + n], refs[2 + n:2 + 2 * n], refs[2 + 2 * n:2 + 3 * n]
        loss_ref = refs[2 + 3 * n]
        g_out, d_out, m_out, v_out = [refs[3 + (3 + q) * n:3 + (4 + q) * n] for q in range(4)]
        g_all = p_ref[0]
        for s in range(1, N_DEV):
            g_all = g_all + p_ref[s]
        loss_ref[...] = jnp.zeros(loss_ref.shape, F32) + (0.5 / d) * jnp.sum(g_all[ROW_SQERR:ROW_SQERR + 1, :])

        def emit(i, idx, g):
            dl, nm, nv = _adamw_math(w_refs[i][idx], g, m_refs[i][idx], v_refs[i][idx])
            g_out[i][idx], d_out[i][idx], m_out[i][idx], v_out[i][idx] = g, dl, nm, nv

        everything = (slice(None), slice(None))
        for i, (_, row, kind) in enumerate(_SMALL):
            if kind == "mod":
                for k in range(N_MOD):
                    emit(i, (slice(0, 1), slice(k * d, (k + 1) * d)), g_all[row + k:row + k + 1, :])
            elif kind == "row":
                emit(i, everything, g_all[row:row + 1, :])
            else:
                taps, dq = w_refs[i].shape
                g = jnp.zeros((taps, dq), F32)
                for j in range(N_CHIP):
                    g = g + oh_ref[0:1, j:j + 1] * g_all[row:row + taps, j * dq:(j + 1) * dq]
                emit(i, everything, g)

    shapes = [jax.ShapeDtypeStruct(w.shape, F32) for w in ws]
    res = _pallas(body, name="small_update", out_shape=tuple([jax.ShapeDtypeStruct((8, 128), F32)] + shapes * 4),
                  in_specs=[VMEM_FULL] * (2 + 3 * n), out_specs=tuple([VMEM_FULL] * (1 + 4 * n)),
                  compiler_params=_params())(p_all, chip_onehot, *ws, *ms, *vs)
    return res[0], [res[1 + q * n:1 + (q + 1) * n] for q in range(4)]


def _w_ada_grad(c_t, dmod_part):
    def body(c_ref, g_ref, o_ref):
        cv = c_ref[...]
        o_ref[...] = _dot((cv * _sig(cv)).astype(BF16), g_ref[...].astype(BF16))

    return _pallas(body, name="w_ada_grad", out_shape=jax.ShapeDtypeStruct((c_t.shape[0], dmod_part.shape[1]), F32),
                   in_specs=[VMEM_FULL] * 2, out_specs=VMEM_FULL, compiler_params=_params())(c_t, dmod_part)


def _step(x, c, w_ada, b_ada, norm_mix_g, w_in, conv_short_w, w_short_out, conv_conf_w, conv_conf_b, conf_ln_g,
          conf_ln_b, w_conf_out, w_o, norm_ffn_g, w_ffn_in, w_ffn_out, final_norm_g, loss_target, moments, ts):
    xi, yi, ci = _place()
    chip = 2 * xi + yi
    me = 4 * xi + 2 * yi + ci
    x2d, tgt = x[0], loss_target[0]
    s_len, d = x2d.shape
    dq = d // N_CHIP
    mq = N_MOD * d // N_CHIP
    gf = final_norm_g.reshape(1, d)

    taps = jnp.concatenate([jnp.pad(conv_short_w[0], ((0, 8 - SHORT_K), (0, 0))),
                            jnp.pad(conv_conf_w[0], ((0, 32 - CONF_K), (0, 0)))], axis=0)
    n_tr = 40 * dq // d
    c_taps = jnp.concatenate([jnp.broadcast_to(c, (8, d)), taps.reshape(n_tr, d), jnp.zeros((16 - n_tr, d), F32)], axis=0)
    b_part = lax.dynamic_slice(b_ada, (0, chip * mq), (1, mq))
    big = [w_in[0], w_short_out[0], w_conf_out[0], w_o[0], w_ffn_in[0], w_ffn_out[0]]
    axes = [1, 0, 0, 0, 1, 0]
    big_names = ["w_in", "w_short_out", "w_conf_out", "w_o", "w_ffn_in", "w_ffn_out"]
    placed = [_place_shard(w, ax, chip, "place_" + n) for w, ax, n in zip(big, axes, big_names)]
    (win_f, wa_f, wb_f, wo_f), c_taps, mod_all = _gather_weights(placed[:4], axes[:4], c_taps, w_ada[0].astype(BF16), b_part,
                                                                 "gather_weights")
    c_taps = c_taps.reshape(N_DEV, 24, d)
    c_all = c_taps[:, 0]
    mod_all = mod_all.reshape(N_CHIP, 2, N_DEV, mq)[:, 0]
    mod_full = jnp.transpose(mod_all, (1, 0, 2)).reshape(N_DEV, N_MOD * d)
    mod = lax.dynamic_slice(mod_full, (me, 0), (1, N_MOD * d))

    taps_all = c_taps[:, 8:8 + n_tr].reshape(N_CHIP, 2, 40, dq)[:, 0]
    taps_full = jnp.transpose(taps_all, (1, 0, 2)).reshape(40, d)
    w3, w31 = taps_full[0:SHORT_K], taps_full[8:8 + CONF_K]

    (proj, h, yap, c3, nrm, rstd, u3, ya, yb, merged, mix, x1), (wfi_f, wfo_f) = _fwd_mix(
        x2d, mod, norm_mix_g, w3, w31, conv_conf_b, conf_ln_g, conf_ln_b, win_f, wa_f, wb_f, wo_f, placed[4:], axes[4:], ts)
    h2, hid, df, dab, dx1, acc_f = _ffn(x1, tgt, mod, norm_ffn_g, gf, wfi_f, wfo_f, ts)
    f = wfo_f.shape[0]
    bn = min(512, d)
    g_ffn = [_weight_grad(h2, dab, bn, "grad_w_ffn_in")[0], _weight_grad(df, hid, bn // 2, "grad_w_ffn_out", transposed=True)[0]]
    (dproj, dmix, dya, dyb, acc_m), r_ffn = _bwd_mix(dx1, proj, nrm, rstd, c3, ya, yb, mix, mod, w3, w31, conf_ln_g,
                                                    conf_ln_b, wa_f, wb_f, wo_f, g_ffn, axes[4:], ts)
    g_sq = [_weight_grad(yap, dya, bn // 2, "grad_w_short_out")[0], _weight_grad(u3, dyb, bn // 2, "grad_w_conf_out")[0],
            _weight_grad(merged, dmix, bn // 2, "grad_w_o")[0]]
    g_in, r_sq = _weight_grad(h, dproj, bn, "grad_w_in", ready=g_sq, ready_axes=axes[1:4])
    grad_x, acc_i, r_in = _bwd_in(dproj, x2d, dx1, mod, norm_mix_g, win_f, [g_in], axes[:1], min(2 * ts, s_len))
    g_in = [g_in]
    r_in_sq = list(r_in) + list(r_sq)
    grads = g_in + g_sq + g_ffn
    recv = list(r_in_sq) + list(r_ffn)
    g_big, p_all = _swap_halves([_sum_slots(r, g, ax, chip, ci, "sum_slots_" + n)
                                 for r, g, ax, n in zip(recv, grads, axes, big_names)], acc_f + acc_m + acc_i)
    p_all = p_all.reshape(N_DEV, N_SMALL_ROWS, d)

    small_w = dict(b_ada=b_ada, norm_mix_g=norm_mix_g, conv_short_w=conv_short_w, conv_conf_w=conv_conf_w,
                   conv_conf_b=conv_conf_b, conf_ln_g=conf_ln_g, conf_ln_b=conf_ln_b, norm_ffn_g=norm_ffn_g,
                   final_norm_g=final_norm_g)
    as_2d = lambda t: t.reshape((-1, t.shape[-1]))
    chip_onehot = (lax.broadcasted_iota(jnp.int32, (1, 128), 1) == chip).astype(F32)
    loss_tile, small_out = _small_update(
        p_all, chip_onehot, [as_2d(small_w[n]) for n, _, _ in _SMALL], [as_2d(moments["m_" + n]) for n, _, _ in _SMALL],
        [as_2d(moments["v_" + n]) for n, _, _ in _SMALL])
    small_g, small_d, small_m, small_v = [{n: o.reshape(small_w[n].shape) for (n, _, _), o in zip(_SMALL, outs)}
                                          for outs in small_out]

    dmod_all = p_all[:, 0:N_MOD].reshape(N_DEV, N_MOD * d)
    dmod_part = jnp.pad(lax.dynamic_slice(dmod_all, (0, chip * mq), (N_DEV, mq)), ((0, 128 - N_DEV), (0, 0)))
    c_t = jnp.pad(jnp.transpose(c_all), ((0, 0), (0, 128 - N_DEV)))
    g_ada = _w_ada_grad(c_t, dmod_part)

    out_g, out_d, out_m, out_v = dict(small_g), dict(small_d), dict(small_m), dict(small_v)
    for name, w, g in zip(["w_ada"] + big_names, [w_ada[0]] + big, [g_ada] + list(g_big)):
        go, dl, nm, nv = _adamw(w, g, moments["m_" + name][0], moments["v_" + name][0], "adamw_" + name)
        out_g[name], out_d[name], out_m[name], out_v[name] = go[None], dl[None], nm[None], nv[None]

    loss = loss_tile[0, 0]
    return loss, grad_x[None], out_g, out_d, out_m, out_v


_WEIGHTS = ["w_ada", "b_ada", "norm_mix_g", "w_in", "conv_short_w", "w_short_out", "conv_conf_w", "conv_conf_b",
            "conf_ln_g", "conf_ln_b", "w_conf_out", "w_o", "norm_ffn_g", "w_ffn_in", "w_ffn_out", "final_norm_g"]
ROW_TILE = 256


def kernel(x, c, w_ada, b_ada, norm_mix_g, w_in, conv_short_w, w_short_out, conv_conf_w, conv_conf_b, conf_ln_g, conf_ln_b, w_conf_out, w_o, norm_ffn_g, w_ffn_in, w_ffn_out, final_norm_g, loss_target, m_w_ada, m_b_ada, m_norm_mix_g, m_w_in, m_conv_short_w, m_w_short_out, m_conv_conf_w, m_conv_conf_b, m_conf_ln_g, m_conf_ln_b, m_w_conf_out, m_w_o, m_norm_ffn_g, m_w_ffn_in, m_w_ffn_out, m_final_norm_g, v_w_ada, v_b_ada, v_norm_mix_g, v_w_in, v_conv_short_w, v_w_short_out, v_conv_conf_w, v_conv_conf_b, v_conf_ln_g, v_conf_ln_b, v_w_conf_out, v_w_o, v_norm_ffn_g, v_w_ffn_in, v_w_ffn_out, v_final_norm_g):
    moments = dict(
        m_w_ada=m_w_ada, m_b_ada=m_b_ada, m_norm_mix_g=m_norm_mix_g, m_w_in=m_w_in, m_conv_short_w=m_conv_short_w,
        m_w_short_out=m_w_short_out, m_conv_conf_w=m_conv_conf_w, m_conv_conf_b=m_conv_conf_b, m_conf_ln_g=m_conf_ln_g,
        m_conf_ln_b=m_conf_ln_b, m_w_conf_out=m_w_conf_out, m_w_o=m_w_o, m_norm_ffn_g=m_norm_ffn_g, m_w_ffn_in=m_w_ffn_in,
        m_w_ffn_out=m_w_ffn_out, m_final_norm_g=m_final_norm_g,
        v_w_ada=v_w_ada, v_b_ada=v_b_ada, v_norm_mix_g=v_norm_mix_g, v_w_in=v_w_in, v_conv_short_w=v_conv_short_w,
        v_w_short_out=v_w_short_out, v_conv_conf_w=v_conv_conf_w, v_conv_conf_b=v_conv_conf_b, v_conf_ln_g=v_conf_ln_g,
        v_conf_ln_b=v_conf_ln_b, v_w_conf_out=v_w_conf_out, v_w_o=v_w_o, v_norm_ffn_g=v_norm_ffn_g, v_w_ffn_in=v_w_ffn_in,
        v_w_ffn_out=v_w_ffn_out, v_final_norm_g=v_final_norm_g)
    loss, grad_x, g, dl, nm, nv = _step(
        x, c, w_ada, b_ada, norm_mix_g, w_in, conv_short_w, w_short_out, conv_conf_w, conv_conf_b, conf_ln_g, conf_ln_b,
        w_conf_out, w_o, norm_ffn_g, w_ffn_in, w_ffn_out, final_norm_g, loss_target, moments, min(ROW_TILE, x.shape[1]))
    return (loss, grad_x, *[g[n] for n in _WEIGHTS], *[dl[n] for n in _WEIGHTS], *[nm[n] for n in _WEIGHTS],
            *[nv[n] for n in _WEIGHTS])
```

```python
import jax
import jax.numpy as jnp
from jax import lax
from jax.experimental import pallas as pl
from jax.experimental.pallas import tpu as pltpu

F32 = jnp.float32
BF16 = jnp.bfloat16
EPS = 1e-6
LN_EPS = 1e-5
SHORT_K = 3
CONF_K = 31
N_MOD = 6
N_DEV = 8
N_CHIP = 4
ADAM_LR = 0.001
ADAM_B1 = 0.9
ADAM_B2 = 0.999
ADAM_EPS = 1e-08
ADAM_WD = 0.01
ADAM_STEP = 10
MESH = pl.DeviceIdType.MESH
V7X_VMEM_LIMIT_BYTES = 60 * 1024 * 1024
LANES = 128
HALO_SHORT = 8
HALO_CONF = 32
N_SMALL_ROWS = 48
ROW_DMOD, ROW_GN1, ROW_W3, ROW_W31, ROW_CB, ROW_LNG, ROW_LNB, ROW_GN2, ROW_GF, ROW_SQERR = 0, 6, 7, 10, 41, 42, 43, 44, 45, 46
CONV_ROWS = 64
CONV_LANES = 256

ANY = pl.BlockSpec(memory_space=pl.ANY)
VMEM_FULL = pl.BlockSpec(memory_space=pltpu.VMEM)


def _pallas(body, **kw):
    return pl.pallas_call(body, **kw)


def _params(*sem):
    return pltpu.CompilerParams(dimension_semantics=sem, vmem_limit_bytes=V7X_VMEM_LIMIT_BYTES)


def _sig(z):
    return jax.nn.sigmoid(z)


def _dot(a, b):
    return jnp.dot(a, b, preferred_element_type=F32)


def _dot_nt(a, b):
    return lax.dot_general(a, b, (((1,), (1,)), ((), ())), preferred_element_type=F32)


def _dot_tn(a, b):
    return lax.dot_general(a, b, (((0,), (0,)), ((), ())), preferred_element_type=F32)


def _colsum(z):
    return jnp.sum(z, axis=0, keepdims=True)


def _place():
    return lax.axis_index("x"), lax.axis_index("y"), lax.axis_index("c")


def _small_gather(x_ref, out_ref, send_sems, recv_sems, local_sem):
    m_per = x_ref.shape[0]
    x, y, c = _place()
    me, sibling = (x, y, c), (x, y, 1 - c)
    chips = [(1 - x, y), (x, 1 - y), (1 - x, 1 - y)]

    def rows(px, py, pc):
        return out_ref.at[pl.ds((4 * px + 2 * py + pc) * m_per, m_per), :]

    def copy(k, blk, to, src=None):
        return pltpu.make_async_remote_copy(
            src_ref=rows(*blk) if src is None else src, dst_ref=rows(*blk),
            send_sem=send_sems.at[k], recv_sem=recv_sems.at[k], device_id=to, device_id_type=MESH)

    mine = pltpu.make_async_copy(x_ref, rows(*me), local_sem)
    mine.start()
    first = [copy(0, me, sibling, src=x_ref)]
    first += [copy(1 + j, me, (*chip, c), src=x_ref) for j, chip in enumerate(chips)]
    for cp in first:
        cp.start()
    passed = [copy(4 + j, (*chip, c), sibling) for j, chip in enumerate(chips)]
    for j, chip in enumerate(chips):
        copy(1 + j, (*chip, c), me).wait_recv()
        passed[j].start()
    copy(0, sibling, me).wait_recv()
    for j, chip in enumerate(chips):
        copy(4 + j, (*chip, 1 - c), me).wait_recv()
    for cp in first + passed:
        cp.wait_send()
    mine.wait()


def _small_gather_sems():
    return [pltpu.SemaphoreType.DMA((7,)), pltpu.SemaphoreType.DMA((7,)), pltpu.SemaphoreType.DMA]


def _shard_slice(ref, full_shape, shard_axis, chip, half, quarter=None):
    m, n = full_shape
    r = (m // N_CHIP if shard_axis == 0 else m) // 2
    start = half * r + (chip * 2 * r if shard_axis == 0 else 0)
    if quarter is not None:
        r = r // 2
        start = start + quarter * r
    if shard_axis == 0:
        return ref.at[pl.ds(start, r), :]
    cw = n // N_CHIP
    return ref.at[pl.ds(start, r), pl.ds(pl.multiple_of(chip * cw, 128), cw)]


def _place_shard(w, axis, chip, name):
    r, cc = w.shape
    tr = r if axis == 0 else min(256, r)
    full = (r * N_CHIP, cc) if axis == 0 else (r, cc * N_CHIP)

    def body(s_ref, w_ref, o_ref):
        o_ref[...] = w_ref[...].astype(BF16)

    out_map = (lambda i, s: (s[0], 0)) if axis == 0 else (lambda i, s: (i, s[0]))
    grid_spec = pltpu.PrefetchScalarGridSpec(
        num_scalar_prefetch=1, grid=(r // tr,), in_specs=[pl.BlockSpec((tr, cc), lambda i, s: (i, 0))],
        out_specs=pl.BlockSpec((tr, cc), out_map))
    return _pallas(body, name=name, grid_spec=grid_spec, out_shape=jax.ShapeDtypeStruct(full, BF16),
                   compiler_params=_params("arbitrary"))(jnp.reshape(chip, (1,)).astype(jnp.int32), w)


def _gather_copy(ref, region, k, a, to, sems):
    return pltpu.make_async_remote_copy(src_ref=region, dst_ref=region, send_sem=sems[0].at[a, k], recv_sem=sems[1].at[a, k],
                                        device_id=to, device_id_type=MESH)


def _gather_send(refs, shapes, axes, *sems):
    x, y, c = _place()
    for a in range(len(refs)):
        mine = _shard_slice(refs[a], shapes[a], axes[a], 2 * x + y, c)
        _gather_copy(refs[a], mine, 0, a, (1 - x, y, c), sems).start()
        _gather_copy(refs[a], mine, 1, a, (x, 1 - y, c), sems).start()


def _gather_forward(refs, shapes, axes, *sems):
    x, y, c = _place()
    me, sibling, x_nbr, y_nbr = (x, y, c), (x, y, 1 - c), (1 - x, y, c), (x, 1 - y, c)
    chip_x, chip_y, chip_d = 2 * (1 - x) + y, 2 * x + (1 - y), 2 * (1 - x) + (1 - y)
    for a in range(len(refs)):
        region = lambda chip, quarter=None, a=a: _shard_slice(refs[a], shapes[a], axes[a], chip, c, quarter)
        _gather_copy(refs[a], region(chip_x), 0, a, me, sems).wait_recv()
        _gather_copy(refs[a], region(chip_x, 1), 3, a, y_nbr, sems).start()
        _gather_copy(refs[a], region(chip_x), 4, a, sibling, sems).start()
        _gather_copy(refs[a], region(chip_y), 1, a, me, sems).wait_recv()
        _gather_copy(refs[a], region(chip_y, 0), 2, a, x_nbr, sems).start()
        _gather_copy(refs[a], region(chip_y), 5, a, sibling, sems).start()


def _gather_forward_diagonal(refs, shapes, axes, *sems):
    x, y, c = _place()
    me, sibling, chip_d = (x, y, c), (x, y, 1 - c), 2 * (1 - x) + (1 - y)
    for a in range(len(refs)):
        region = lambda chip, quarter=None, a=a: _shard_slice(refs[a], shapes[a], axes[a], chip, c, quarter)
        _gather_copy(refs[a], region(chip_d, 0), 2, a, me, sems).wait_recv()
        _gather_copy(refs[a], region(chip_d, 1), 3, a, me, sems).wait_recv()
        _gather_copy(refs[a], region(chip_d), 6, a, sibling, sems).start()


def _gather_finish(refs, shapes, axes, *sems):
    x, y, c = _place()
    me = (x, y, c)
    chip_me, chip_x, chip_y, chip_d = 2 * x + y, 2 * (1 - x) + y, 2 * x + (1 - y), 2 * (1 - x) + (1 - y)
    for a in range(len(refs)):
        region = lambda chip, half, quarter=None, a=a: _shard_slice(refs[a], shapes[a], axes[a], chip, half, quarter)
        for k, chip in ((4, chip_x), (5, chip_y), (6, chip_d)):
            _gather_copy(refs[a], region(chip, 1 - c), k, a, me, sems).wait_recv()
            _gather_copy(refs[a], region(chip, c), k, a, me, sems).wait_send()
        _gather_copy(refs[a], region(chip_me, c), 0, a, me, sems).wait_send()
        _gather_copy(refs[a], region(chip_me, c), 1, a, me, sems).wait_send()
        _gather_copy(refs[a], region(chip_y, c, 0), 2, a, me, sems).wait_send()
        _gather_copy(refs[a], region(chip_x, c, 1), 3, a, me, sems).wait_send()


def _gather_sems(n_w):
    return [pltpu.SemaphoreType.DMA((n_w, 7)), pltpu.SemaphoreType.DMA((n_w, 7))]


def _gather_weights(fulls, axes, c_taps, w_ada_bf, b_part, name):
    n_w = len(fulls)
    shapes = [f.shape for f in fulls]
    r_ct = c_taps.shape[0]
    mq = w_ada_bf.shape[1]

    def body(*refs):
        ct_ref, wada_ref, b_ref = refs[n_w:n_w + 3]
        outs = refs[n_w + 3:2 * n_w + 3]
        ct_all_ref, mod_all_ref, mod_part = refs[2 * n_w + 3:2 * n_w + 6]
        sems, ct_sems, mod_sems = refs[2 * n_w + 6:2 * n_w + 8], refs[2 * n_w + 8:2 * n_w + 11], refs[2 * n_w + 11:]
        _gather_send(outs, shapes, axes, *sems)
        _small_gather(ct_ref, ct_all_ref, *ct_sems)
        cv = jnp.concatenate([ct_all_ref[r_ct * dv:r_ct * dv + 1, :] for dv in range(N_DEV)], axis=0)
        mod_part[...] = _dot((cv * _sig(cv)).astype(BF16), wada_ref[...]) + b_ref[...]
        _small_gather(mod_part, mod_all_ref, *mod_sems)
        _gather_forward(outs, shapes, axes, *sems)
        _gather_forward_diagonal(outs, shapes, axes, *sems)
        _gather_finish(outs, shapes, axes, *sems)

    res = _pallas(
        body, name=name,
        out_shape=tuple(jax.ShapeDtypeStruct(f.shape, BF16) for f in fulls)
        + (jax.ShapeDtypeStruct((N_DEV * r_ct, c_taps.shape[1]), F32), jax.ShapeDtypeStruct((N_DEV * N_DEV, mq), F32)),
        in_specs=[ANY] * n_w + [VMEM_FULL] * 3, out_specs=tuple([ANY] * n_w + [VMEM_FULL] * 2),
        input_output_aliases={a: a for a in range(n_w)},
        scratch_shapes=[pltpu.VMEM((N_DEV, mq), F32)] + _gather_sems(n_w) + _small_gather_sems() + _small_gather_sems(),
        compiler_params=_params(),
    )(*fulls, c_taps, w_ada_bf, b_part)
    return res[:n_w], res[n_w], res[n_w + 1]


def _piece_shapes(grads, axes):
    return [((g.shape[0] // N_CHIP // 2, g.shape[1]) if ax == 0 else (g.shape[0] // 2, g.shape[1] // N_CHIP))
            for g, ax in zip(grads, axes)]


def _exchange_send(ins, outs, shapes, axes, send_sems, recv_sems):
    x, y, c = _place()
    me = 4 * x + 2 * y + c
    for a in range(len(ins)):
        for t in range(N_DEV):
            tx, ty, tc = t // 4, (t // 2) % 2, t % 2
            src = _shard_slice(ins[a], shapes[a], axes[a], 2 * tx + ty, tc)

            @pl.when(me != t)
            def _(src=src, a=a, t=t, to=(tx, ty, tc)):
                slot = (me - t + N_DEV) % N_DEV - 1
                pltpu.make_async_remote_copy(src_ref=src, dst_ref=outs[a].at[slot], send_sem=send_sems.at[a, t],
                                             recv_sem=recv_sems.at[a, slot], device_id=to, device_id_type=MESH).start()


def _exchange_finish(outs, send_sems, recv_sems):
    x, y, c = _place()
    me = 4 * x + 2 * y + c
    for a in range(len(outs)):
        for t in range(N_DEV):
            @pl.when(me != t)
            def _(a=a, t=t):
                slot = outs[a].at[0]
                pltpu.make_async_remote_copy(src_ref=slot, dst_ref=slot, send_sem=send_sems.at[a, t], recv_sem=recv_sems.at[a, 0],
                                             device_id=(x, y, c), device_id_type=MESH).wait_send()
        for k in range(N_DEV - 1):
            slot = outs[a].at[k]
            pltpu.make_async_remote_copy(src_ref=slot, dst_ref=slot, send_sem=send_sems.at[a, 0], recv_sem=recv_sems.at[a, k],
                                         device_id=(x, y, c), device_id_type=MESH).wait_recv()


def _exchange_sems(n_w):
    return [pltpu.SemaphoreType.DMA((n_w, N_DEV)), pltpu.SemaphoreType.DMA((n_w, N_DEV - 1))]


def _exchange_out_shapes(grads, axes):
    return tuple(jax.ShapeDtypeStruct((N_DEV - 1,) + p, BF16) for p in _piece_shapes(grads, axes))


def _swap_halves(fulls, block):
    n_w = len(fulls)

    def body(*refs):
        block_ref = refs[n_w]
        outs, gathered_ref = refs[n_w + 1:2 * n_w + 1], refs[2 * n_w + 1]
        send_sems, recv_sems = refs[2 * n_w + 2:2 * n_w + 4]
        x, y, c = _place()
        sibling = (x, y, 1 - c)
        waits = []
        for a in range(n_w):
            r = fulls[a].shape[0] // 2
            mine = outs[a].at[pl.ds(c * r, r), :]
            theirs = outs[a].at[pl.ds((1 - c) * r, r), :]
            rc = pltpu.make_async_remote_copy(src_ref=mine, dst_ref=mine, send_sem=send_sems.at[a], recv_sem=recv_sems.at[a],
                                              device_id=sibling, device_id_type=MESH)
            rc.start()
            got = pltpu.make_async_remote_copy(src_ref=theirs, dst_ref=theirs, send_sem=send_sems.at[a],
                                               recv_sem=recv_sems.at[a], device_id=sibling, device_id_type=MESH)
            waits += [rc.wait_send, got.wait_recv]
        _small_gather(block_ref, gathered_ref, *refs[2 * n_w + 4:])
        for w in waits:
            w()

    res = _pallas(
        body, name="swap_halves",
        out_shape=tuple(jax.ShapeDtypeStruct(f.shape, f.dtype) for f in fulls)
        + (jax.ShapeDtypeStruct((N_DEV * block.shape[0], block.shape[1]), block.dtype),),
        in_specs=[ANY] * n_w + [VMEM_FULL], out_specs=tuple([ANY] * n_w + [VMEM_FULL]),
        input_output_aliases={a: a for a in range(n_w)},
        scratch_shapes=[pltpu.SemaphoreType.DMA((n_w,)), pltpu.SemaphoreType.DMA((n_w,))] + _small_gather_sems(),
    )(*fulls, block)
    return res[:n_w], res[n_w]


def _load_weights(pairs, sems):
    @pl.when(pl.program_id(0) == 0)
    def _():
        cps = [pltpu.make_async_copy(s, d, sems.at[k]) for k, (s, d) in enumerate(pairs)]
        for cp in cps:
            cp.start()
        for cp in cps:
            cp.wait()


def _conv_chunks(ext, e_buf, ts, d, chunk_fn, unroll, between=()):
    rows, dh = e_buf.shape[1], e_buf.shape[2]
    for hf in range(d // dh):
        for r in range(1, 8):
            e_buf[r - 1, :, :] = ext[r:r + rows, hf * dh:(hf + 1) * dh]
        for lq in range(dh // CONV_LANES):
            lo, el = hf * dh + lq * CONV_LANES, lq * CONV_LANES

            def body(ci, carry, lo=lo, el=el):
                r0 = pl.multiple_of(ci * CONV_ROWS, CONV_ROWS)

                def tap(off):
                    q, r = divmod(off, 8)
                    if r == 0:
                        return ext[pl.ds(r0 + 8 * q, CONV_ROWS), lo:lo + CONV_LANES]
                    return e_buf[r - 1, pl.ds(r0 + 8 * q, CONV_ROWS), el:el + CONV_LANES]

                chunk_fn(tap, r0, lo)
                return carry

            lax.fori_loop(0, ts // CONV_ROWS, body, 0, unroll=unroll)
            if between:
                between[0]()
                between = between[1:]
    for fn in between:
        fn()


def _fwd_mix(x, mod, gn1, w3, w31, cbias, lng, lnb, w_in, w_a, w_b, w_o, later, later_axes, ts):
    s_len, d = x.shape
    n_t = s_len // ts
    n_l = len(later)
    later_shapes = [w.shape for w in later]

    def body(*refs):
        (x_ref, mod_ref, gn1_ref, w3_ref, w31_ref, cb_ref, lng_ref, lnb_ref, win_hbm, wa_hbm, wb_hbm, wo_hbm) = refs[:12]
        (proj_ref, h_ref, yap_ref, c3_ref, n_ref, rstd_ref, u3_ref, ya_ref, yb_ref, mg_ref, mix_ref,
         x1_ref) = refs[12 + n_l:24 + n_l]
        later_refs = refs[24 + n_l:24 + 2 * n_l]
        win_v, wa_v, wb_v, wo_v, cv_ext, u0_ext, e_buf, u1_buf, sems, g_send, g_recv = refs[24 + 2 * n_l:]
        step = pl.program_id(0)
        pl.when(step == 0)(lambda: _gather_send(later_refs, later_shapes, later_axes, g_send, g_recv))
        pl.when(step == n_t // 2)(lambda: _gather_forward(later_refs, later_shapes, later_axes, g_send, g_recv))
        pl.when(step == (3 * n_t) // 4)(lambda: _gather_forward_diagonal(later_refs, later_shapes, later_axes, g_send, g_recv))
        pl.when(step == n_t - 1)(lambda: _gather_finish(later_refs, later_shapes, later_axes, g_send, g_recv))
        _load_weights([(win_hbm, win_v), (wa_hbm, wa_v), (wb_hbm, wb_v), (wo_hbm, wo_v)], sems)

        @pl.when(pl.program_id(0) == 0)
        def _():
            cv_ext[0:HALO_SHORT, :] = jnp.zeros((HALO_SHORT, d), F32)
            u0_ext[0:HALO_CONF, :] = jnp.zeros((HALO_CONF, d), F32)

        xv = x_ref[...]
        sh1, sc1, g1 = mod_ref[:, 0:d], mod_ref[:, d:2 * d], mod_ref[:, 2 * d:3 * d]
        r = lax.rsqrt(jnp.mean(xv * xv, axis=-1, keepdims=True) + EPS)
        hb = (((xv * r) * gn1_ref[...]) * (1.0 + sc1) + sh1).astype(BF16)
        h_ref[...] = hb

        def pj(k):
            p = _dot(hb, win_v[:, k * d:(k + 1) * d])
            proj_ref[:, k * d:(k + 1) * d] = p.astype(BF16)
            return p

        v_c = pj(3)
        u0_ext[HALO_CONF:HALO_CONF + ts, :] = v_c * _sig(pj(4))

        def conv_chunk(tap, r0, lo):
            acc = jnp.zeros((CONV_ROWS, CONV_LANES), F32)
            for k in range(CONF_K):
                acc = acc + w31_ref[k:k + 1, lo:lo + CONV_LANES] * tap(HALO_CONF - (CONF_K - 1) + k)
            u1_buf[pl.ds(r0, CONV_ROWS), lo:lo + CONV_LANES] = acc

        pjs = {}
        _conv_chunks(u0_ext, e_buf, ts, d, conv_chunk, unroll=True,
                     between=[lambda k=k: pjs.__setitem__(k, pj(k)) for k in (1, 2, 0, 5, 6)])
        u0_ext[0:HALO_CONF, :] = u0_ext[ts:ts + HALO_CONF, :]

        cv_ext[HALO_SHORT:HALO_SHORT + ts, :] = pjs[1] * pjs[2]
        conv3 = jnp.zeros((ts, d), F32)
        for k in range(SHORT_K):
            off = HALO_SHORT - (SHORT_K - 1) + k
            conv3 = conv3 + w3_ref[k:k + 1, :] * cv_ext[off:off + ts, :]
        cv_ext[0:HALO_SHORT, :] = cv_ext[ts:ts + HALO_SHORT, :]
        c3_ref[...] = conv3.astype(BF16)
        yap = (pjs[0] * conv3).astype(BF16)
        yap_ref[...] = yap
        y_a = _dot(yap, wa_v[...])
        ya_ref[...] = y_a.astype(BF16)

        u1 = u1_buf[...] + cb_ref[...]
        mu = jnp.mean(u1, axis=-1, keepdims=True)
        var = jnp.mean(jnp.square(u1 - mu), axis=-1, keepdims=True)
        rstd = lax.rsqrt(var + LN_EPS)
        nrm = (u1 - mu) * rstd
        n_ref[...] = nrm.astype(BF16)
        rstd_ref[...] = jnp.broadcast_to(rstd, (ts, LANES))
        u2 = nrm * lng_ref[...] + lnb_ref[...]
        u3 = (u2 * _sig(u2)).astype(BF16)
        u3_ref[...] = u3
        y_b = _dot(u3, wb_v[...])
        yb_ref[...] = y_b.astype(BF16)

        merged = (_sig(pjs[5]) * y_a + _sig(pjs[6]) * y_b).astype(BF16)
        mg_ref[...] = merged
        mix = _dot(merged, wo_v[...])
        mix_ref[...] = mix
        x1_ref[...] = xv + g1 * mix

    row = lambda w, dt=None: pl.BlockSpec((ts, w), lambda i: (i, 0))
    full = lambda a: pl.BlockSpec(a.shape, lambda i: (0, 0))
    sd = lambda w, dt: jax.ShapeDtypeStruct((s_len, w), dt)
    outs = [sd(7 * d, BF16), sd(d, BF16), sd(d, BF16), sd(d, BF16), sd(d, BF16), sd(LANES, F32), sd(d, BF16), sd(d, BF16),
            sd(d, BF16), sd(d, BF16), sd(d, F32), sd(d, F32)]
    res = _pallas(
        body, name="fwd_mix", grid=(n_t,),
        out_shape=tuple(outs) + tuple(jax.ShapeDtypeStruct(w.shape, BF16) for w in later),
        in_specs=[row(d), full(mod), full(gn1), full(w3), full(w31), full(cbias), full(lng), full(lnb), ANY, ANY, ANY, ANY]
        + [ANY] * n_l,
        out_specs=tuple(row(o.shape[1]) for o in outs) + tuple([ANY] * n_l),
        input_output_aliases={12 + k: len(outs) + k for k in range(n_l)},
        scratch_shapes=[pltpu.VMEM(w_in.shape, BF16), pltpu.VMEM(w_a.shape, BF16), pltpu.VMEM(w_b.shape, BF16),
                        pltpu.VMEM(w_o.shape, BF16), pltpu.VMEM((ts + HALO_SHORT, d), F32),
                        pltpu.VMEM((ts + HALO_CONF, d), F32), pltpu.VMEM((7, ts + HALO_CONF - 8, d // 2), F32),
                        pltpu.VMEM((ts, d), F32), pltpu.SemaphoreType.DMA((4,))] + _gather_sems(n_l),
        compiler_params=_params("arbitrary"),
    )(x, mod, gn1, w3, w31, cbias, lng, lnb, w_in, w_a, w_b, w_o, *later)
    return res[:len(outs)], res[len(outs):]


def _ffn(x1, target, mod, gn2, gf, w_fi, w_fo, ts):
    s_len, d = x1.shape
    f = w_fo.shape[0]
    n_t = s_len // ts

    def body(x1_ref, t_ref, mod_ref, gn2_ref, gf_ref, wfi_hbm, wfo_hbm,
             h2_ref, hid_ref, df_ref, dab_ref, dx1_ref, acc_ref, wfi_v, wfo_v, sems):
        _load_weights([(wfi_hbm, wfi_v), (wfo_hbm, wfo_v)], sems)

        @pl.when(pl.program_id(0) == 0)
        def _():
            acc_ref[...] = jnp.zeros(acc_ref.shape, F32)

        x1v = x1_ref[...]
        sh2, sc2, g2 = mod_ref[:, 3 * d:4 * d], mod_ref[:, 4 * d:5 * d], mod_ref[:, 5 * d:6 * d]
        gn2v, gfv = gn2_ref[...], gf_ref[...]
        r2 = lax.rsqrt(jnp.mean(x1v * x1v, axis=-1, keepdims=True) + EPS)
        xn2 = x1v * r2
        h2 = ((xn2 * gn2v) * (1.0 + sc2) + sh2).astype(BF16)
        h2_ref[...] = h2
        a = _dot(h2, wfi_v[:, 0:f])
        bg = _dot(h2, wfi_v[:, f:2 * f])
        sa = _sig(a)
        silu_a = a * sa
        hid = (silu_a * bg).astype(BF16)
        hid_ref[...] = hid
        ffn = _dot(hid, wfo_v[...])
        x2 = x1v + g2 * ffn
        r3 = lax.rsqrt(jnp.mean(x2 * x2, axis=-1, keepdims=True) + EPS)
        xn3 = x2 * r3
        diff = xn3 * gfv - t_ref[...]
        dy = diff * (1.0 / d)
        dxn3 = dy * gfv
        dx2 = r3 * (dxn3 - xn3 * jnp.mean(dxn3 * xn3, axis=-1, keepdims=True))
        acc_ref[ROW_GF:ROW_GF + 1, :] += _colsum(dy * xn3)
        acc_ref[ROW_DMOD + 5:ROW_DMOD + 5 + 1, :] += _colsum(dx2 * ffn)
        acc_ref[ROW_SQERR:ROW_SQERR + 1, :] += _colsum(diff * diff)

        df = (dx2 * g2).astype(BF16)
        df_ref[...] = df
        dhid = _dot_nt(df, wfo_v[...])
        da = dhid * bg * (sa * (1.0 + a * (1.0 - sa)))
        dbg = dhid * silu_a
        dab_ref[:, 0:f] = da.astype(BF16)
        dab_ref[:, f:2 * f] = dbg.astype(BF16)
        dh2 = _dot_nt(da.astype(BF16), wfi_v[:, 0:f]) + _dot_nt(dbg.astype(BF16), wfi_v[:, f:2 * f])
        acc_ref[ROW_DMOD + 3:ROW_DMOD + 3 + 1, :] += _colsum(dh2)
        acc_ref[ROW_DMOD + 4:ROW_DMOD + 4 + 1, :] += _colsum(dh2 * (xn2 * gn2v))
        acc_ref[ROW_GN2:ROW_GN2 + 1, :] += _colsum(dh2 * xn2 * (1.0 + sc2))
        dxn2 = dh2 * gn2v * (1.0 + sc2)
        dx1_ref[...] = dx2 + r2 * (dxn2 - xn2 * jnp.mean(dxn2 * xn2, axis=-1, keepdims=True))

    row = lambda w: pl.BlockSpec((ts, w), lambda i: (i, 0))
    full = lambda a: pl.BlockSpec(a.shape, lambda i: (0, 0))
    sd = lambda w, dt: jax.ShapeDtypeStruct((s_len, w), dt)
    outs = [sd(d, BF16), sd(f, BF16), sd(d, BF16), sd(2 * f, BF16), sd(d, F32), jax.ShapeDtypeStruct((N_SMALL_ROWS, d), F32)]
    return _pallas(
        body, name="ffn", grid=(n_t,), out_shape=tuple(outs),
        in_specs=[row(d), row(d), full(mod), full(gn2), full(gf), ANY, ANY],
        out_specs=tuple([row(o.shape[1]) for o in outs[:-1]] + [pl.BlockSpec((N_SMALL_ROWS, d), lambda i: (0, 0))]),
        scratch_shapes=[pltpu.VMEM(w_fi.shape, BF16), pltpu.VMEM(w_fo.shape, BF16), pltpu.SemaphoreType.DMA((2,))],
        compiler_params=_params("arbitrary"),
    )(x1, target, mod, gn2, gf, w_fi, w_fo)


def _bwd_mix(dx1, proj, nrm, rstd, c3, ya, yb, mix, mod, w3, w31, lng, lnb, w_a, w_b, w_o, ready, ready_axes, ts):
    s_len, d = dx1.shape
    n_t = s_len // ts
    n_r = len(ready)
    ready_shapes = [g.shape for g in ready]

    def body(*refs):
        (dx1_ref, proj_ref, n_ref, rstd_ref, c3_ref, ya_ref, yb_ref, mix_ref, mod_ref, w3_ref, w31_ref, lng_ref,
         lnb_ref, wa_hbm, wb_hbm, wo_hbm) = refs[:16]
        ready_refs = refs[16:16 + n_r]
        dproj_ref, dmix_ref, dya_ref, dyb_ref, acc_ref = refs[16 + n_r:21 + n_r]
        recv_refs = refs[21 + n_r:21 + 2 * n_r]
        wa_v, wb_v, wo_v, dc3_ext, du1_ext, e_buf, u0_buf, du0_buf, dw8, sems, e_send, e_recv = refs[21 + 2 * n_r:]
        step = pl.program_id(0)
        pl.when(step == 0)(lambda: _exchange_send(ready_refs, recv_refs, ready_shapes, ready_axes, e_send, e_recv))
        pl.when(step == n_t - 1)(lambda: _exchange_finish(recv_refs, e_send, e_recv))
        _load_weights([(wa_hbm, wa_v), (wb_hbm, wb_v), (wo_hbm, wo_v)], sems)

        @pl.when(pl.program_id(0) == 0)
        def _():
            acc_ref[...] = jnp.zeros(acc_ref.shape, F32)
            dc3_ext[ts:ts + HALO_SHORT, :] = jnp.zeros((HALO_SHORT, d), F32)
            du1_ext[ts:ts + HALO_CONF, :] = jnp.zeros((HALO_CONF, d), F32)
            dw8[...] = jnp.zeros(dw8.shape, F32)

        pj = lambda k: proj_ref[:, k * d:(k + 1) * d].astype(F32)
        g1 = mod_ref[:, 2 * d:3 * d]
        dx1v = dx1_ref[...]
        acc_ref[ROW_DMOD + 2:ROW_DMOD + 2 + 1, :] += _colsum(dx1v * mix_ref[...])
        dmix = (dx1v * g1).astype(BF16)
        dmix_ref[...] = dmix
        dmerged = _dot_nt(dmix, wo_v[...])
        g_a, g_b = _sig(pj(5)), _sig(pj(6))
        y_a, y_b = ya_ref[...].astype(F32), yb_ref[...].astype(F32)
        dproj_ref[:, 5 * d:6 * d] = (dmerged * y_a * g_a * (1.0 - g_a)).astype(BF16)
        dproj_ref[:, 6 * d:7 * d] = (dmerged * y_b * g_b * (1.0 - g_b)).astype(BF16)
        dya = (dmerged * g_a).astype(BF16)
        dyb = (dmerged * g_b).astype(BF16)
        dya_ref[...] = dya
        dyb_ref[...] = dyb

        dyap = _dot_nt(dya, wa_v[...])
        dproj_ref[:, 0:d] = (dyap * c3_ref[...].astype(F32)).astype(BF16)
        dc3_ext[0:ts, :] = dyap * pj(0)
        c_s, v_s = pj(1), pj(2)
        cv = c_s * v_s
        dcv = jnp.zeros((ts, d), F32)
        for j in range(SHORT_K):
            k = SHORT_K - 1 - j
            sj = dc3_ext[j:j + ts, :]
            dcv = dcv + w3_ref[k:k + 1, :] * sj
            acc_ref[ROW_W3 + k:ROW_W3 + k + 1, :] += _colsum(cv * sj)
        dc3_ext[ts:ts + HALO_SHORT, :] = dc3_ext[0:HALO_SHORT, :]
        dproj_ref[:, d:2 * d] = (dcv * v_s).astype(BF16)
        dproj_ref[:, 2 * d:3 * d] = (dcv * c_s).astype(BF16)

        du3 = _dot_nt(dyb, wb_v[...])
        nv = n_ref[...].astype(F32)
        lg = lng_ref[...]
        u2 = nv * lg + lnb_ref[...]
        s2 = _sig(u2)
        du2 = du3 * (s2 * (1.0 + u2 * (1.0 - s2)))
        acc_ref[ROW_LNG:ROW_LNG + 1, :] += _colsum(du2 * nv)
        acc_ref[ROW_LNB:ROW_LNB + 1, :] += _colsum(du2)
        dn = du2 * lg
        du1 = rstd_ref[:, 0:1] * (dn - jnp.mean(dn, axis=-1, keepdims=True) - nv * jnp.mean(dn * nv, axis=-1, keepdims=True))
        acc_ref[ROW_CB:ROW_CB + 1, :] += _colsum(du1)
        du1_ext[0:ts, :] = du1
        v_c = pj(3)
        sg = _sig(pj(4))
        u0_buf[...] = v_c * sg

        def conv_chunk(tap, r0, lo):
            lanes = slice(lo, lo + CONV_LANES)
            u0c = u0_buf[pl.ds(r0, CONV_ROWS), lanes]
            acc = jnp.zeros((CONV_ROWS, CONV_LANES), F32)
            for j in range(CONF_K):
                k = CONF_K - 1 - j
                sj = tap(j)
                acc = acc + w31_ref[k:k + 1, lanes] * sj
                prod = u0c * sj
                part = prod[0:8]
                for i in range(1, CONV_ROWS // 8):
                    part = part + prod[8 * i:8 * i + 8]
                dw8[k, :, lanes] += part
            du0_buf[pl.ds(r0, CONV_ROWS), lanes] = acc

        _conv_chunks(du1_ext, e_buf, ts, d, conv_chunk, unroll=False)
        du0 = du0_buf[...]
        du1_ext[ts:ts + HALO_CONF, :] = du1_ext[0:HALO_CONF, :]

        @pl.when(step == n_t - 1)
        def _():
            for k in range(CONF_K):
                acc_ref[ROW_W31 + k:ROW_W31 + k + 1, :] = _colsum(dw8[k])
        dproj_ref[:, 3 * d:4 * d] = (du0 * sg).astype(BF16)
        dproj_ref[:, 4 * d:5 * d] = (du0 * v_c * sg * (1.0 - sg)).astype(BF16)

    rev = lambda w: pl.BlockSpec((ts, w), lambda i: (n_t - 1 - i, 0))
    full = lambda a: pl.BlockSpec(a.shape, lambda i: (0, 0))
    sd = lambda w, dt: jax.ShapeDtypeStruct((s_len, w), dt)
    outs = [sd(7 * d, BF16), sd(d, BF16), sd(d, BF16), sd(d, BF16), jax.ShapeDtypeStruct((N_SMALL_ROWS, d), F32)]
    res = _pallas(
        body, name="bwd_mix", grid=(n_t,), out_shape=tuple(outs) + _exchange_out_shapes(ready, ready_axes),
        in_specs=[rev(d), rev(7 * d), rev(d), rev(LANES), rev(d), rev(d), rev(d), rev(d), full(mod), full(w3), full(w31),
                  full(lng), full(lnb), ANY, ANY, ANY] + [ANY] * n_r,
        out_specs=tuple([rev(o.shape[1]) for o in outs[:-1]] + [pl.BlockSpec((N_SMALL_ROWS, d), lambda i: (0, 0))] + [ANY] * n_r),
        scratch_shapes=[pltpu.VMEM(w_a.shape, BF16), pltpu.VMEM(w_b.shape, BF16), pltpu.VMEM(w_o.shape, BF16),
                        pltpu.VMEM((ts + HALO_SHORT, d), F32), pltpu.VMEM((ts + HALO_CONF, d), F32),
                        pltpu.VMEM((7, ts + HALO_CONF - 8, d // 2), F32), pltpu.VMEM((ts, d), F32), pltpu.VMEM((ts, d), F32),
                        pltpu.VMEM((CONF_K, 8, d), F32), pltpu.SemaphoreType.DMA((3,))] + _exchange_sems(n_r),
        compiler_params=_params("arbitrary"),
    )(dx1, proj, nrm, rstd, c3, ya, yb, mix, mod, w3, w31, lng, lnb, w_a, w_b, w_o, *ready)
    return res[:len(outs)], res[len(outs):]


def _bwd_in(dproj, x, dx1, mod, gn1, w_in, ready, ready_axes, ts):
    s_len, d = x.shape
    n_t = s_len // ts
    n_r = len(ready)
    ready_shapes = [g.shape for g in ready]

    def body(*refs):
        dproj_ref, x_ref, dx1_ref, mod_ref, gn1_ref, win_hbm = refs[:6]
        ready_refs = refs[6:6 + n_r]
        gx_ref, acc_ref = refs[6 + n_r:8 + n_r]
        recv_refs = refs[8 + n_r:8 + 2 * n_r]
        win_v, sems, e_send, e_recv = refs[8 + 2 * n_r:]
        step = pl.program_id(0)
        pl.when(step == 0)(lambda: _exchange_send(ready_refs, recv_refs, ready_shapes, ready_axes, e_send, e_recv))
        pl.when(step == n_t - 1)(lambda: _exchange_finish(recv_refs, e_send, e_recv))
        _load_weights([(win_hbm, win_v)], sems)

        @pl.when(pl.program_id(0) == 0)
        def _():
            acc_ref[...] = jnp.zeros(acc_ref.shape, F32)

        xv = x_ref[...]
        sc1 = mod_ref[:, d:2 * d]
        gn1v = gn1_ref[...]
        r = lax.rsqrt(jnp.mean(xv * xv, axis=-1, keepdims=True) + EPS)
        xn = xv * r
        dh = _dot_nt(dproj_ref[...], win_v[...])
        acc_ref[ROW_DMOD:ROW_DMOD + 1, :] += _colsum(dh)
        acc_ref[ROW_DMOD + 1:ROW_DMOD + 1 + 1, :] += _colsum(dh * (xn * gn1v))
        acc_ref[ROW_GN1:ROW_GN1 + 1, :] += _colsum(dh * xn * (1.0 + sc1))
        dxn = dh * gn1v * (1.0 + sc1)
        gx_ref[...] = dx1_ref[...] + r * (dxn - xn * jnp.mean(dxn * xn, axis=-1, keepdims=True))

    row = lambda w: pl.BlockSpec((ts, w), lambda i: (i, 0))
    full = lambda a: pl.BlockSpec(a.shape, lambda i: (0, 0))
    res = _pallas(
        body, name="bwd_in", grid=(n_t,),
        out_shape=(jax.ShapeDtypeStruct((s_len, d), F32), jax.ShapeDtypeStruct((N_SMALL_ROWS, d), F32))
        + _exchange_out_shapes(ready, ready_axes),
        in_specs=[row(7 * d), row(d), row(d), full(mod), full(gn1), ANY] + [ANY] * n_r,
        out_specs=(row(d), pl.BlockSpec((N_SMALL_ROWS, d), lambda i: (0, 0))) + tuple([ANY] * n_r),
        scratch_shapes=[pltpu.VMEM(w_in.shape, BF16), pltpu.SemaphoreType.DMA((1,))] + _exchange_sems(n_r),
        compiler_params=_params("arbitrary"),
    )(dproj, x, dx1, mod, gn1, w_in, *ready)
    return res[0], res[1], res[2:]


def _weight_grad(a, b, bn, name, transposed=False, ready=(), ready_axes=()):
    s_len, m = a.shape
    n = b.shape[1]
    n_j = n // bn
    n_r = len(ready)
    ready_shapes = [g.shape for g in ready]

    def body(*refs):
        a_hbm, b_ref = refs[:2]
        ready_refs = refs[2:2 + n_r]
        o_ref = refs[2 + n_r]
        recv_refs = refs[3 + n_r:3 + 2 * n_r]
        a_v, sem = refs[3 + 2 * n_r:5 + 2 * n_r]
        step = pl.program_id(0)
        if n_r:
            e_send, e_recv = refs[5 + 2 * n_r:]
            pl.when(step == 0)(lambda: _exchange_send(ready_refs, recv_refs, ready_shapes, ready_axes, e_send, e_recv))
            pl.when(step == n_j - 1)(lambda: _exchange_finish(recv_refs, e_send, e_recv))
        _load_weights([(a_hbm, a_v)], sem)
        prod = _dot_tn(a_v[...], b_ref[...])
        o_ref[...] = (prod.T if transposed else prod).astype(BF16)

    out = jax.ShapeDtypeStruct((n, m) if transposed else (m, n), BF16)
    out_spec = pl.BlockSpec((bn, m), lambda j: (j, 0)) if transposed else pl.BlockSpec((m, bn), lambda j: (0, j))
    res = _pallas(
        body, name=name, grid=(n_j,), out_shape=(out,) + _exchange_out_shapes(ready, ready_axes),
        in_specs=[ANY, pl.BlockSpec((s_len, bn), lambda j: (0, j))] + [ANY] * n_r,
        out_specs=(out_spec,) + tuple([ANY] * n_r),
        scratch_shapes=[pltpu.VMEM(a.shape, BF16), pltpu.SemaphoreType.DMA((1,))] + (_exchange_sems(n_r) if n_r else []),
        compiler_params=_params("arbitrary"),
    )(a, b, *ready)
    return res[0], res[1:]


def _sum_slots(recv, grad, axis, chip, core, name):
    n_s, r, c = recv.shape
    tr = 16
    for cand in (256, 128, 176, 64, 32, 16):
        if r % cand == 0:
            tr = cand
            break
    n_i = r // tr

    def body(s_ref, own_ref, r_ref, o_ref):
        tot = own_ref[...].astype(F32)
        for k in range(n_s):
            tot = tot + r_ref[k].astype(F32)
        o_ref[...] = tot

    if axis == 0:
        own_map = lambda i, s: ((2 * s[0] + s[1]) * n_i + i, 0)
    else:
        own_map = lambda i, s: (s[1] * n_i + i, s[0])
    grid_spec = pltpu.PrefetchScalarGridSpec(
        num_scalar_prefetch=1, grid=(n_i,),
        in_specs=[pl.BlockSpec((tr, c), own_map), pl.BlockSpec((n_s, tr, c), lambda i, s: (0, i, 0))],
        out_specs=pl.BlockSpec((tr, c), lambda i, s: (s[1] * n_i + i, 0)))
    return _pallas(body, name=name, grid_spec=grid_spec, out_shape=jax.ShapeDtypeStruct((2 * r, c), F32),
                   compiler_params=_params("arbitrary"))(jnp.stack([chip, core]).astype(jnp.int32), grad, recv)


def _adamw_math(w, g, m, v):
    m = ADAM_B1 * m + (1.0 - ADAM_B1) * g
    v = ADAM_B2 * v + (1.0 - ADAM_B2) * (g * g)
    m_hat = m / (1.0 - ADAM_B1 ** ADAM_STEP)
    v_hat = v / (1.0 - ADAM_B2 ** ADAM_STEP)
    delta = -ADAM_LR * (m_hat / (jnp.sqrt(v_hat) + ADAM_EPS) + ADAM_WD * w)
    return delta, m, v


def _adamw(w, g, m, v, name):
    r, c = w.shape
    tr = r
    for cand in (256, 128, 176, 64, 32, 16, 8):
        if r % cand == 0:
            tr = cand
            break

    def body(w_ref, g_ref, m_ref, v_ref, go_ref, d_ref, nm_ref, nv_ref):
        g = g_ref[...]
        go_ref[...] = g
        d_ref[...], nm_ref[...], nv_ref[...] = _adamw_math(w_ref[...], g, m_ref[...], v_ref[...])

    blk = pl.BlockSpec((tr, c), lambda i: (i, 0))
    return _pallas(body, name=name, grid=(r // tr,), out_shape=tuple(jax.ShapeDtypeStruct((r, c), F32) for _ in range(4)),
                   in_specs=[blk] * 4, out_specs=(blk,) * 4, compiler_params=_params("arbitrary"))(w, g, m, v)


_SMALL = [
    ("b_ada", ROW_DMOD, "mod"), ("norm_mix_g", ROW_GN1, "row"), ("conv_short_w", ROW_W3, "shard"),
    ("conv_conf_w", ROW_W31, "shard"), ("conv_conf_b", ROW_CB, "row"), ("conf_ln_g", ROW_LNG, "row"),
    ("conf_ln_b", ROW_LNB, "row"), ("norm_ffn_g", ROW_GN2, "row"), ("final_norm_g", ROW_GF, "row")]


def _small_update(p_all, chip_onehot, ws, ms, vs):
    n = len(_SMALL)
    d = p_all.shape[2]

    def body(*refs):
        p_ref, oh_ref = refs[:2]
        w_refs, m_refs, v_refs = refs[2:2 + n], refs[2 + n:2 + 2 * n], refs[2 + 2 * n:2 + 3 * n]
        loss_ref = refs[2 + 3 * n]
        g_out, d_out, m_out, v_out = [refs[3 + (3 + q) * n:3 + (4 + q) * n] for q in range(4)]
        g_all = p_ref[0]
        for s in range(1, N_DEV):
            g_all = g_all + p_ref[s]
        loss_ref[...] = jnp.zeros(loss_ref.shape, F32) + (0.5 / d) * jnp.sum(g_all[ROW_SQERR:ROW_SQERR + 1, :])

        def emit(i, idx, g):
            dl, nm, nv = _adamw_math(w_refs[i][idx], g, m_refs[i][idx], v_refs[i][idx])
            g_out[i][idx], d_out[i][idx], m_out[i][idx], v_out[i][idx] = g, dl, nm, nv

        everything = (slice(None), slice(None))
        for i, (_, row, kind) in enumerate(_SMALL):
            if kind == "mod":
                for k in range(N_MOD):
                    emit(i, (slice(0, 1), slice(k * d, (k + 1) * d)), g_all[row + k:row + k + 1, :])
            elif kind == "row":
                emit(i, everything, g_all[row:row + 1, :])
            else:
                taps, dq = w_refs[i].shape
                g = jnp.zeros((taps, dq), F32)
                for j in range(N_CHIP):
                    g = g + oh_ref[0:1, j:j + 1] * g_all[row:row + taps, j * dq:(j + 1) * dq]
                emit(i, everything, g)

    shapes = [jax.ShapeDtypeStruct(w.shape, F32) for w in ws]
    res = _pallas(body, name="small_update", out_shape=tuple([jax.ShapeDtypeStruct((8, 128), F32)] + shapes * 4),
                  in_specs=[VMEM_FULL] * (2 + 3 * n), out_specs=tuple([VMEM_FULL] * (1 + 4 * n)),
                  compiler_params=_params())(p_all, chip_onehot, *ws, *ms, *vs)
    return res[0], [res[1 + q * n:1 + (q + 1) * n] for q in range(4)]


def _w_ada_grad(c_t, dmod_part):
    def body(c_ref, g_ref, o_ref):
        cv = c_ref[...]
        o_ref[...] = _dot((cv * _sig(cv)).astype(BF16), g_ref[...].astype(BF16))

    return _pallas(body, name="w_ada_grad", out_shape=jax.ShapeDtypeStruct((c_t.shape[0], dmod_part.shape[1]), F32),
                   in_specs=[VMEM_FULL] * 2, out_specs=VMEM_FULL, compiler_params=_params())(c_t, dmod_part)


def _step(x, c, w_ada, b_ada, norm_mix_g, w_in, conv_short_w, w_short_out, conv_conf_w, conv_conf_b, conf_ln_g,
          conf_ln_b, w_conf_out, w_o, norm_ffn_g, w_ffn_in, w_ffn_out, final_norm_g, loss_target, moments, ts):
    xi, yi, ci = _place()
    chip = 2 * xi + yi
    me = 4 * xi + 2 * yi + ci
    x2d, tgt = x[0], loss_target[0]
    s_len, d = x2d.shape
    dq = d // N_CHIP
    mq = N_MOD * d // N_CHIP
    gf = final_norm_g.reshape(1, d)

    taps = jnp.concatenate([jnp.pad(conv_short_w[0], ((0, 8 - SHORT_K), (0, 0))),
                            jnp.pad(conv_conf_w[0], ((0, 32 - CONF_K), (0, 0)))], axis=0)
    n_tr = 40 * dq // d
    c_taps = jnp.concatenate([jnp.broadcast_to(c, (8, d)), taps.reshape(n_tr, d), jnp.zeros((16 - n_tr, d), F32)], axis=0)
    b_part = lax.dynamic_slice(b_ada, (0, chip * mq), (1, mq))
    big = [w_in[0], w_short_out[0], w_conf_out[0], w_o[0], w_ffn_in[0], w_ffn_out[0]]
    axes = [1, 0, 0, 0, 1, 0]
    big_names = ["w_in", "w_short_out", "w_conf_out", "w_o", "w_ffn_in", "w_ffn_out"]
    placed = [_place_shard(w, ax, chip, "place_" + n) for w, ax, n in zip(big, axes, big_names)]
    (win_f, wa_f, wb_f, wo_f), c_taps, mod_all = _gather_weights(placed[:4], axes[:4], c_taps, w_ada[0].astype(BF16), b_part,
                                                                 "gather_weights")
    c_taps = c_taps.reshape(N_DEV, 24, d)
    c_all = c_taps[:, 0]
    mod_all = mod_all.reshape(N_CHIP, 2, N_DEV, mq)[:, 0]
    mod_full = jnp.transpose(mod_all, (1, 0, 2)).reshape(N_DEV, N_MOD * d)
    mod = lax.dynamic_slice(mod_full, (me, 0), (1, N_MOD * d))

    taps_all = c_taps[:, 8:8 + n_tr].reshape(N_CHIP, 2, 40, dq)[:, 0]
    taps_full = jnp.transpose(taps_all, (1, 0, 2)).reshape(40, d)
    w3, w31 = taps_full[0:SHORT_K], taps_full[8:8 + CONF_K]

    (proj, h, yap, c3, nrm, rstd, u3, ya, yb, merged, mix, x1), (wfi_f, wfo_f) = _fwd_mix(
        x2d, mod, norm_mix_g, w3, w31, conv_conf_b, conf_ln_g, conf_ln_b, win_f, wa_f, wb_f, wo_f, placed[4:], axes[4:], ts)
    h2, hid, df, dab, dx1, acc_f = _ffn(x1, tgt, mod, norm_ffn_g, gf, wfi_f, wfo_f, ts)
    f = wfo_f.shape[0]
    bn = min(512, d)
    g_ffn = [_weight_grad(h2, dab, bn, "grad_w_ffn_in")[0], _weight_grad(df, hid, bn // 2, "grad_w_ffn_out", transposed=True)[0]]
    (dproj, dmix, dya, dyb, acc_m), r_ffn = _bwd_mix(dx1, proj, nrm, rstd, c3, ya, yb, mix, mod, w3, w31, conf_ln_g,
                                                    conf_ln_b, wa_f, wb_f, wo_f, g_ffn, axes[4:], ts)
    g_sq = [_weight_grad(yap, dya, bn // 2, "grad_w_short_out")[0], _weight_grad(u3, dyb, bn // 2, "grad_w_conf_out")[0],
            _weight_grad(merged, dmix, bn // 2, "grad_w_o")[0]]
    g_in, r_sq = _weight_grad(h, dproj, bn, "grad_w_in", ready=g_sq, ready_axes=axes[1:4])
    grad_x, acc_i, r_in = _bwd_in(dproj, x2d, dx1, mod, norm_mix_g, win_f, [g_in], axes[:1], min(2 * ts, s_len))
    g_in = [g_in]
    r_in_sq = list(r_in) + list(r_sq)
    grads = g_in + g_sq + g_ffn
    recv = list(r_in_sq) + list(r_ffn)
    g_big, p_all = _swap_halves([_sum_slots(r, g, ax, chip, ci, "sum_slots_" + n)
                                 for r, g, ax, n in zip(recv, grads, axes, big_names)], acc_f + acc_m + acc_i)
    p_all = p_all.reshape(N_DEV, N_SMALL_ROWS, d)

    small_w = dict(b_ada=b_ada, norm_mix_g=norm_mix_g, conv_short_w=conv_short_w, conv_conf_w=conv_conf_w,
                   conv_conf_b=conv_conf_b, conf_ln_g=conf_ln_g, conf_ln_b=conf_ln_b, norm_ffn_g=norm_ffn_g,
                   final_norm_g=final_norm_g)
    as_2d = lambda t: t.reshape((-1, t.shape[-1]))
    chip_onehot = (lax.broadcasted_iota(jnp.int32, (1, 128), 1) == chip).astype(F32)
    loss_tile, small_out = _small_update(
        p_all, chip_onehot, [as_2d(small_w[n]) for n, _, _ in _SMALL], [as_2d(moments["m_" + n]) for n, _, _ in _SMALL],
        [as_2d(moments["v_" + n]) for n, _, _ in _SMALL])
    small_g, small_d, small_m, small_v = [{n: o.reshape(small_w[n].shape) for (n, _, _), o in zip(_SMALL, outs)}
                                          for outs in small_out]

    dmod_all = p_all[:, 0:N_MOD].reshape(N_DEV, N_MOD * d)
    dmod_part = jnp.pad(lax.dynamic_slice(dmod_all, (0, chip * mq), (N_DEV, mq)), ((0, 128 - N_DEV), (0, 0)))
    c_t = jnp.pad(jnp.transpose(c_all), ((0, 0), (0, 128 - N_DEV)))
    g_ada = _w_ada_grad(c_t, dmod_part)

    out_g, out_d, out_m, out_v = dict(small_g), dict(small_d), dict(small_m), dict(small_v)
    for name, w, g in zip(["w_ada"] + big_names, [w_ada[0]] + big, [g_ada] + list(g_big)):
        go, dl, nm, nv = _adamw(w, g, moments["m_" + name][0], moments["v_" + name][0], "adamw_" + name)
        out_g[name], out_d[name], out_m[name], out_v[name] = go[None], dl[None], nm[None], nv[None]

    loss = loss_tile[0, 0]
    return loss, grad_x[None], out_g, out_d, out_m, out_v


_WEIGHTS = ["w_ada", "b_ada", "norm_mix_g", "w_in", "conv_short_w", "w_short_out", "conv_conf_w", "conv_conf_b",
            "conf_ln_g", "conf_ln_b", "w_conf_out", "w_o", "norm_ffn_g", "w_ffn_in", "w_ffn_out", "final_norm_g"]
ROW_TILE = 256


def kernel(x, c, w_ada, b_ada, norm_mix_g, w_in, conv_short_w, w_short_out, conv_conf_w, conv_conf_b, conf_ln_g, conf_ln_b, w_conf_out, w_o, norm_ffn_g, w_ffn_in, w_ffn_out, final_norm_g, loss_target, m_w_ada, m_b_ada, m_norm_mix_g, m_w_in, m_conv_short_w, m_w_short_out, m_conv_conf_w, m_conv_conf_b, m_conf_ln_g, m_conf_ln_b, m_w_conf_out, m_w_o, m_norm_ffn_g, m_w_ffn_in, m_w_ffn_out, m_final_norm_g, v_w_ada, v_b_ada, v_norm_mix_g, v_w_in, v_conv_short_w, v_w_short_out, v_conv_conf_w, v_conv_conf_b, v_conf_ln_g, v_conf_ln_b, v_w_conf_out, v_w_o, v_norm_ffn_g, v_w_ffn_in, v_w_ffn_out, v_final_norm_g):
    moments = dict(
        m_w_ada=m_w_ada, m_b_ada=m_b_ada, m_norm_mix_g=m_norm_mix_g, m_w_in=m_w_in, m_conv_short_w=m_conv_short_w,
        m_w_short_out=m_w_short_out, m_conv_conf_w=m_conv_conf_w, m_conv_conf_b=m_conv_conf_b, m_conf_ln_g=m_conf_ln_g,
        m_conf_ln_b=m_conf_ln_b, m_w_conf_out=m_w_conf_out, m_w_o=m_w_o, m_norm_ffn_g=m_norm_ffn_g, m_w_ffn_in=m_w_ffn_in,
        m_w_ffn_out=m_w_ffn_out, m_final_norm_g=m_final_norm_g,
        v_w_ada=v_w_ada, v_b_ada=v_b_ada, v_norm_mix_g=v_norm_mix_g, v_w_in=v_w_in, v_conv_short_w=v_conv_short_w,
        v_w_short_out=v_w_short_out, v_conv_conf_w=v_conv_conf_w, v_conv_conf_b=v_conv_conf_b, v_conf_ln_g=v_conf_ln_g,
        v_conf_ln_b=v_conf_ln_b, v_w_conf_out=v_w_conf_out, v_w_o=v_w_o, v_norm_ffn_g=v_norm_ffn_g, v_w_ffn_in=v_w_ffn_in,
        v_w_ffn_out=v_w_ffn_out, v_final_norm_g=v_final_norm_g)
    loss, grad_x, g, dl, nm, nv = _step(
        x, c, w_ada, b_ada, norm_mix_g, w_in, conv_short_w, w_short_out, conv_conf_w, conv_conf_b, conf_ln_g, conf_ln_b,
        w_conf_out, w_o, norm_ffn_g, w_ffn_in, w_ffn_out, final_norm_g, loss_target, moments, min(ROW_TILE, x.shape[1]))
    return (loss, grad_x, *[g[n] for n in _WEIGHTS], *[dl[n] for n in _WEIGHTS], *[nm[n] for n in _WEIGHTS],
            *[nv[n] for n in _WEIGHTS])
```

```python
import jax
import jax.numpy as jnp
from jax import lax
from jax.experimental import pallas as pl
from jax.experimental.pallas import tpu as pltpu

F32 = jnp.float32
BF16 = jnp.bfloat16
EPS = 1e-6
LN_EPS = 1e-5
SHORT_K = 3
CONF_K = 31
N_MOD = 6
N_DEV = 8
N_CHIP = 4
ADAM_LR = 0.001
ADAM_B1 = 0.9
ADAM_B2 = 0.999
ADAM_EPS = 1e-08
ADAM_WD = 0.01
ADAM_STEP = 10
MESH = pl.DeviceIdType.MESH
V7X_VMEM_LIMIT_BYTES = 60 * 1024 * 1024
LANES = 128
HALO_SHORT = 8
HALO_CONF = 32
N_SMALL_ROWS = 48
ROW_DMOD, ROW_GN1, ROW_W3, ROW_W31, ROW_CB, ROW_LNG, ROW_LNB, ROW_GN2, ROW_GF, ROW_SQERR = 0, 6, 7, 10, 41, 42, 43, 44, 45, 46
CONV_ROWS = 64
CONV_LANES = 256

ANY = pl.BlockSpec(memory_space=pl.ANY)
VMEM_FULL = pl.BlockSpec(memory_space=pltpu.VMEM)


def _pallas(body, **kw):
    return pl.pallas_call(body, **kw)


def _params(*sem):
    return pltpu.CompilerParams(dimension_semantics=sem, vmem_limit_bytes=V7X_VMEM_LIMIT_BYTES)


def _sig(z):
    return jax.nn.sigmoid(z)


def _dot(a, b):
    return jnp.dot(a, b, preferred_element_type=F32)


def _dot_nt(a, b):
    return lax.dot_general(a, b, (((1,), (1,)), ((), ())), preferred_element_type=F32)


def _dot_tn(a, b):
    return lax.dot_general(a, b, (((0,), (0,)), ((), ())), preferred_element_type=F32)


def _colsum(z):
    return jnp.sum(z, axis=0, keepdims=True)


def _place():
    return lax.axis_index("x"), lax.axis_index("y"), lax.axis_index("c")


def _small_gather_copies(x_ref, out_ref, send_sems, recv_sems, local_sem):
    m_per = x_ref.shape[0]
    x, y, c = _place()
    me, sibling = (x, y, c), (x, y, 1 - c)
    chips = [(1 - x, y), (x, 1 - y), (1 - x, 1 - y)]

    def rows(px, py, pc):
        return out_ref.at[pl.ds((4 * px + 2 * py + pc) * m_per, m_per), :]

    def copy(k, blk, to, src=None):
        return pltpu.make_async_remote_copy(
            src_ref=rows(*blk) if src is None else src, dst_ref=rows(*blk),
            send_sem=send_sems.at[k], recv_sem=recv_sems.at[k], device_id=to, device_id_type=MESH)

    mine = pltpu.make_async_copy(x_ref, rows(*me), local_sem)
    first = [copy(0, me, sibling, src=x_ref)] + [copy(1 + j, me, (*chip, c), src=x_ref) for j, chip in enumerate(chips)]
    return me, sibling, chips, c, copy, mine, first


def _small_gather_start(x_ref, out_ref, send_sems, recv_sems, local_sem):
    _, _, _, _, _, mine, first = _small_gather_copies(x_ref, out_ref, send_sems, recv_sems, local_sem)
    mine.start()
    for cp in first:
        cp.start()


def _small_gather_finish(x_ref, out_ref, send_sems, recv_sems, local_sem):
    me, sibling, chips, c, copy, mine, first = _small_gather_copies(x_ref, out_ref, send_sems, recv_sems, local_sem)
    passed = [copy(4 + j, (*chip, c), sibling) for j, chip in enumerate(chips)]
    for j, chip in enumerate(chips):
        copy(1 + j, (*chip, c), me).wait_recv()
        passed[j].start()
    copy(0, sibling, me).wait_recv()
    for j, chip in enumerate(chips):
        copy(4 + j, (*chip, 1 - c), me).wait_recv()
    for cp in first + passed:
        cp.wait_send()
    mine.wait()


def _small_gather(x_ref, out_ref, *sems):
    _small_gather_start(x_ref, out_ref, *sems)
    _small_gather_finish(x_ref, out_ref, *sems)


def _small_gather_sems():
    return [pltpu.SemaphoreType.DMA((7,)), pltpu.SemaphoreType.DMA((7,)), pltpu.SemaphoreType.DMA]


def _shard_slice(ref, full_shape, shard_axis, chip, half, quarter=None):
    m, n = full_shape
    r = (m // N_CHIP if shard_axis == 0 else m) // 2
    start = half * r + (chip * 2 * r if shard_axis == 0 else 0)
    if quarter is not None:
        r = r // 2
        start = start + quarter * r
    if shard_axis == 0:
        return ref.at[pl.ds(start, r), :]
    cw = n // N_CHIP
    return ref.at[pl.ds(start, r), pl.ds(pl.multiple_of(chip * cw, 128), cw)]


def _gather_copy(ref, region, k, a, to, sems):
    return pltpu.make_async_remote_copy(src_ref=region, dst_ref=region, send_sem=sems[0].at[a, k], recv_sem=sems[1].at[a, k],
                                        device_id=to, device_id_type=MESH)


def _gather_send(refs, shapes, axes, *sems):
    x, y, c = _place()
    for a in range(len(refs)):
        mine = _shard_slice(refs[a], shapes[a], axes[a], 2 * x + y, c)
        _gather_copy(refs[a], mine, 0, a, (1 - x, y, c), sems).start()
        _gather_copy(refs[a], mine, 1, a, (x, 1 - y, c), sems).start()


def _gather_forward(refs, shapes, axes, *sems):
    x, y, c = _place()
    me, sibling, x_nbr, y_nbr = (x, y, c), (x, y, 1 - c), (1 - x, y, c), (x, 1 - y, c)
    chip_x, chip_y, chip_d = 2 * (1 - x) + y, 2 * x + (1 - y), 2 * (1 - x) + (1 - y)
    for a in range(len(refs)):
        region = lambda chip, quarter=None, a=a: _shard_slice(refs[a], shapes[a], axes[a], chip, c, quarter)
        _gather_copy(refs[a], region(chip_x), 0, a, me, sems).wait_recv()
        _gather_copy(refs[a], region(chip_x, 1), 3, a, y_nbr, sems).start()
        _gather_copy(refs[a], region(chip_x), 4, a, sibling, sems).start()
        _gather_copy(refs[a], region(chip_y), 1, a, me, sems).wait_recv()
        _gather_copy(refs[a], region(chip_y, 0), 2, a, x_nbr, sems).start()
        _gather_copy(refs[a], region(chip_y), 5, a, sibling, sems).start()


def _gather_forward_diagonal(refs, shapes, axes, *sems):
    x, y, c = _place()
    me, sibling, chip_d = (x, y, c), (x, y, 1 - c), 2 * (1 - x) + (1 - y)
    for a in range(len(refs)):
        region = lambda chip, quarter=None, a=a: _shard_slice(refs[a], shapes[a], axes[a], chip, c, quarter)
        _gather_copy(refs[a], region(chip_d, 0), 2, a, me, sems).wait_recv()
        _gather_copy(refs[a], region(chip_d, 1), 3, a, me, sems).wait_recv()
        _gather_copy(refs[a], region(chip_d), 6, a, sibling, sems).start()


def _gather_finish(refs, shapes, axes, *sems):
    x, y, c = _place()
    me = (x, y, c)
    chip_me, chip_x, chip_y, chip_d = 2 * x + y, 2 * (1 - x) + y, 2 * x + (1 - y), 2 * (1 - x) + (1 - y)
    for a in range(len(refs)):
        region = lambda chip, half, quarter=None, a=a: _shard_slice(refs[a], shapes[a], axes[a], chip, half, quarter)
        for k, chip in ((4, chip_x), (5, chip_y), (6, chip_d)):
            _gather_copy(refs[a], region(chip, 1 - c), k, a, me, sems).wait_recv()
            _gather_copy(refs[a], region(chip, c), k, a, me, sems).wait_send()
        _gather_copy(refs[a], region(chip_me, c), 0, a, me, sems).wait_send()
        _gather_copy(refs[a], region(chip_me, c), 1, a, me, sems).wait_send()
        _gather_copy(refs[a], region(chip_y, c, 0), 2, a, me, sems).wait_send()
        _gather_copy(refs[a], region(chip_x, c, 1), 3, a, me, sems).wait_send()


def _gather_sems(n_w):
    return [pltpu.SemaphoreType.DMA((n_w, 7)), pltpu.SemaphoreType.DMA((n_w, 7))]


def _gather_weights(shards, axes, n_gather, c_taps, w_ada, b_part, name):
    n_w = len(shards)
    shapes = [(w.shape[0] * N_CHIP, w.shape[1]) if ax == 0 else (w.shape[0], w.shape[1] * N_CHIP) for w, ax in zip(shards, axes)]
    r_ct = c_taps.shape[0]
    mq = w_ada.shape[1]

    def body(*refs):
        ins = refs[:n_w]
        ct_ref, wada_ref, b_ref = refs[n_w:n_w + 3]
        outs = refs[n_w + 3:2 * n_w + 3]
        ct_all_ref, mod_all_ref = refs[2 * n_w + 3:2 * n_w + 5]
        scratch = refs[2 * n_w + 5:]
        stage32, stage16, mod_part = scratch[:n_w], scratch[n_w:2 * n_w], scratch[2 * n_w]
        load_sems, store_sems = scratch[2 * n_w + 1:2 * n_w + 3]
        sems, ct_sems, mod_sems = scratch[2 * n_w + 3:2 * n_w + 5], scratch[2 * n_w + 5:2 * n_w + 8], scratch[2 * n_w + 8:]
        x, y, c = _place()
        chip = 2 * x + y
        loads = [pltpu.make_async_copy(ins[a], stage32[a], load_sems.at[a]) for a in range(n_w)]
        for cp in loads:
            cp.start()
        _small_gather_start(ct_ref, ct_all_ref, *ct_sems)
        stores = []
        for a in range(n_w):
            loads[a].wait()
            stage16[a][...] = stage32[a][...].astype(BF16)
            r, cw = shards[a].shape
            own = outs[a].at[pl.ds(chip * r, r), :] if axes[a] == 0 else outs[a].at[:, pl.ds(pl.multiple_of(chip * cw, 128), cw)]
            stores.append(pltpu.make_async_copy(stage16[a], own, store_sems.at[a]))
            stores[-1].start()
            if a < n_gather:
                half = stage16[a].at[pl.ds(c * (r // 2), r // 2), :]
                there = _shard_slice(outs[a], shapes[a], axes[a], chip, c)
                for k, to in ((0, (1 - x, y, c)), (1, (x, 1 - y, c))):
                    pltpu.make_async_remote_copy(src_ref=half, dst_ref=there, send_sem=sems[0].at[a, k],
                                                 recv_sem=sems[1].at[a, k], device_id=to, device_id_type=MESH).start()
        _small_gather_finish(ct_ref, ct_all_ref, *ct_sems)
        cv = jnp.concatenate([ct_all_ref[r_ct * dv:r_ct * dv + 1, :] for dv in range(N_DEV)], axis=0)
        mod_part[...] = _dot((cv * _sig(cv)).astype(BF16), wada_ref[...].astype(BF16)) + b_ref[...]
        _small_gather(mod_part, mod_all_ref, *mod_sems)
        _gather_forward(outs[:n_gather], shapes, axes, *sems)
        _gather_forward_diagonal(outs[:n_gather], shapes, axes, *sems)
        _gather_finish(outs[:n_gather], shapes, axes, *sems)
        for cp in stores:
            cp.wait()

    res = _pallas(
        body, name=name,
        out_shape=tuple(jax.ShapeDtypeStruct(sh, BF16) for sh in shapes)
        + (jax.ShapeDtypeStruct((N_DEV * r_ct, c_taps.shape[1]), F32), jax.ShapeDtypeStruct((N_DEV * N_DEV, mq), F32)),
        in_specs=[ANY] * n_w + [VMEM_FULL] * 3, out_specs=tuple([ANY] * n_w + [VMEM_FULL] * 2),
        scratch_shapes=[pltpu.VMEM(w.shape, F32) for w in shards] + [pltpu.VMEM(w.shape, BF16) for w in shards]
        + [pltpu.VMEM((N_DEV, mq), F32), pltpu.SemaphoreType.DMA((n_w,)), pltpu.SemaphoreType.DMA((n_w,))]
        + _gather_sems(n_gather) + _small_gather_sems() + _small_gather_sems(),
        compiler_params=_params(),
    )(*shards, c_taps, w_ada, b_part)
    return res[:n_w], res[n_w], res[n_w + 1]


def _piece_shapes(grads, axes):
    return [((g.shape[0] // N_CHIP // 2, g.shape[1]) if ax == 0 else (g.shape[0] // 2, g.shape[1] // N_CHIP))
            for g, ax in zip(grads, axes)]


def _exchange_send(ins, outs, shapes, axes, send_sems, recv_sems):
    x, y, c = _place()
    me = 4 * x + 2 * y + c
    for a in range(len(ins)):
        for t in range(N_DEV):
            tx, ty, tc = t // 4, (t // 2) % 2, t % 2
            src = _shard_slice(ins[a], shapes[a], axes[a], 2 * tx + ty, tc)

            @pl.when(me != t)
            def _(src=src, a=a, t=t, to=(tx, ty, tc)):
                slot = (me - t + N_DEV) % N_DEV - 1
                pltpu.make_async_remote_copy(src_ref=src, dst_ref=outs[a].at[slot], send_sem=send_sems.at[a, t],
                                             recv_sem=recv_sems.at[a, slot], device_id=to, device_id_type=MESH).start()


def _exchange_finish(outs, send_sems, recv_sems):
    x, y, c = _place()
    me = 4 * x + 2 * y + c
    for a in range(len(outs)):
        for t in range(N_DEV):
            @pl.when(me != t)
            def _(a=a, t=t):
                slot = outs[a].at[0]
                pltpu.make_async_remote_copy(src_ref=slot, dst_ref=slot, send_sem=send_sems.at[a, t], recv_sem=recv_sems.at[a, 0],
                                             device_id=(x, y, c), device_id_type=MESH).wait_send()
        for k in range(N_DEV - 1):
            slot = outs[a].at[k]
            pltpu.make_async_remote_copy(src_ref=slot, dst_ref=slot, send_sem=send_sems.at[a, 0], recv_sem=recv_sems.at[a, k],
                                         device_id=(x, y, c), device_id_type=MESH).wait_recv()


def _exchange_sems(n_w):
    return [pltpu.SemaphoreType.DMA((n_w, N_DEV)), pltpu.SemaphoreType.DMA((n_w, N_DEV - 1))]


def _exchange_out_shapes(grads, axes):
    return tuple(jax.ShapeDtypeStruct((N_DEV - 1,) + p, BF16) for p in _piece_shapes(grads, axes))


def _swap_halves(fulls, block):
    n_w = len(fulls)

    def body(*refs):
        block_ref = refs[n_w]
        outs, gathered_ref = refs[n_w + 1:2 * n_w + 1], refs[2 * n_w + 1]
        send_sems, recv_sems = refs[2 * n_w + 2:2 * n_w + 4]
        x, y, c = _place()
        sibling = (x, y, 1 - c)
        waits = []
        for a in range(n_w):
            r = fulls[a].shape[0] // 2
            mine = outs[a].at[pl.ds(c * r, r), :]
            theirs = outs[a].at[pl.ds((1 - c) * r, r), :]
            rc = pltpu.make_async_remote_copy(src_ref=mine, dst_ref=mine, send_sem=send_sems.at[a], recv_sem=recv_sems.at[a],
                                              device_id=sibling, device_id_type=MESH)
            rc.start()
            got = pltpu.make_async_remote_copy(src_ref=theirs, dst_ref=theirs, send_sem=send_sems.at[a],
                                               recv_sem=recv_sems.at[a], device_id=sibling, device_id_type=MESH)
            waits += [rc.wait_send, got.wait_recv]
        _small_gather(block_ref, gathered_ref, *refs[2 * n_w + 4:])
        for w in waits:
            w()

    res = _pallas(
        body, name="swap_halves",
        out_shape=tuple(jax.ShapeDtypeStruct(f.shape, f.dtype) for f in fulls)
        + (jax.ShapeDtypeStruct((N_DEV * block.shape[0], block.shape[1]), block.dtype),),
        in_specs=[ANY] * n_w + [VMEM_FULL], out_specs=tuple([ANY] * n_w + [VMEM_FULL]),
        input_output_aliases={a: a for a in range(n_w)},
        scratch_shapes=[pltpu.SemaphoreType.DMA((n_w,)), pltpu.SemaphoreType.DMA((n_w,))] + _small_gather_sems(),
    )(*fulls, block)
    return res[:n_w], res[n_w]


def _load_weights(pairs, sems):
    @pl.when(pl.program_id(0) == 0)
    def _():
        cps = [pltpu.make_async_copy(s, d, sems.at[k]) for k, (s, d) in enumerate(pairs)]
        for cp in cps:
            cp.start()
        for cp in cps:
            cp.wait()


def _conv_chunks(ext, e_buf, ts, d, chunk_fn, unroll, between=()):
    rows, dh = e_buf.shape[1], e_buf.shape[2]
    for hf in range(d // dh):
        for r in range(1, 8):
            e_buf[r - 1, :, :] = ext[r:r + rows, hf * dh:(hf + 1) * dh]
        for lq in range(dh // CONV_LANES):
            lo, el = hf * dh + lq * CONV_LANES, lq * CONV_LANES

            def body(ci, carry, lo=lo, el=el):
                r0 = pl.multiple_of(ci * CONV_ROWS, CONV_ROWS)

                def tap(off):
                    q, r = divmod(off, 8)
                    if r == 0:
                        return ext[pl.ds(r0 + 8 * q, CONV_ROWS), lo:lo + CONV_LANES]
                    return e_buf[r - 1, pl.ds(r0 + 8 * q, CONV_ROWS), el:el + CONV_LANES]

                chunk_fn(tap, r0, lo)
                return carry

            lax.fori_loop(0, ts // CONV_ROWS, body, 0, unroll=unroll)
            if between:
                between[0]()
                between = between[1:]
    for fn in between:
        fn()


def _fwd_mix(x, mod, gn1, w3, w31, cbias, lng, lnb, w_in, w_a, w_b, w_o, later, later_axes, ts):
    s_len, d = x.shape
    n_t = s_len // ts
    n_l = len(later)
    later_shapes = [w.shape for w in later]

    def body(*refs):
        (x_ref, mod_ref, gn1_ref, w3_ref, w31_ref, cb_ref, lng_ref, lnb_ref, win_hbm, wa_hbm, wb_hbm, wo_hbm) = refs[:12]
        (proj_ref, h_ref, yap_ref, c3_ref, n_ref, rstd_ref, u3_ref, ya_ref, yb_ref, mg_ref, mix_ref,
         x1_ref) = refs[12 + n_l:24 + n_l]
        later_refs = refs[24 + n_l:24 + 2 * n_l]
        win_v, wa_v, wb_v, wo_v, cv_ext, u0_ext, e_buf, u1_buf, sems, g_send, g_recv = refs[24 + 2 * n_l:]
        step = pl.program_id(0)
        pl.when(step == 0)(lambda: _gather_send(later_refs, later_shapes, later_axes, g_send, g_recv))
        pl.when(step == n_t // 2)(lambda: _gather_forward(later_refs, later_shapes, later_axes, g_send, g_recv))
        pl.when(step == (3 * n_t) // 4)(lambda: _gather_forward_diagonal(later_refs, later_shapes, later_axes, g_send, g_recv))
        pl.when(step == n_t - 1)(lambda: _gather_finish(later_refs, later_shapes, later_axes, g_send, g_recv))
        _load_weights([(win_hbm, win_v), (wa_hbm, wa_v), (wb_hbm, wb_v), (wo_hbm, wo_v)], sems)

        @pl.when(pl.program_id(0) == 0)
        def _():
            cv_ext[0:HALO_SHORT, :] = jnp.zeros((HALO_SHORT, d), F32)
            u0_ext[0:HALO_CONF, :] = jnp.zeros((HALO_CONF, d), F32)

        xv = x_ref[...]
        sh1, sc1, g1 = mod_ref[:, 0:d], mod_ref[:, d:2 * d], mod_ref[:, 2 * d:3 * d]
        r = lax.rsqrt(jnp.mean(xv * xv, axis=-1, keepdims=True) + EPS)
        hb = (((xv * r) * gn1_ref[...]) * (1.0 + sc1) + sh1).astype(BF16)
        h_ref[...] = hb

        def pj(k):
            p = _dot(hb, win_v[:, k * d:(k + 1) * d])
            proj_ref[:, k * d:(k + 1) * d] = p.astype(BF16)
            return p

        v_c = pj(3)
        u0_ext[HALO_CONF:HALO_CONF + ts, :] = v_c * _sig(pj(4))

        def conv_chunk(tap, r0, lo):
            acc = jnp.zeros((CONV_ROWS, CONV_LANES), F32)
            for k in range(CONF_K):
                acc = acc + w31_ref[k:k + 1, lo:lo + CONV_LANES] * tap(HALO_CONF - (CONF_K - 1) + k)
            u1_buf[pl.ds(r0, CONV_ROWS), lo:lo + CONV_LANES] = acc

        pjs = {}
        _conv_chunks(u0_ext, e_buf, ts, d, conv_chunk, unroll=True,
                     between=[lambda k=k: pjs.__setitem__(k, pj(k)) for k in (1, 2, 0, 5, 6)])
        u0_ext[0:HALO_CONF, :] = u0_ext[ts:ts + HALO_CONF, :]

        cv_ext[HALO_SHORT:HALO_SHORT + ts, :] = pjs[1] * pjs[2]
        conv3 = jnp.zeros((ts, d), F32)
        for k in range(SHORT_K):
            off = HALO_SHORT - (SHORT_K - 1) + k
            conv3 = conv3 + w3_ref[k:k + 1, :] * cv_ext[off:off + ts, :]
        cv_ext[0:HALO_SHORT, :] = cv_ext[ts:ts + HALO_SHORT, :]
        c3_ref[...] = conv3.astype(BF16)
        yap = (pjs[0] * conv3).astype(BF16)
        yap_ref[...] = yap
        y_a = _dot(yap, wa_v[...])
        ya_ref[...] = y_a.astype(BF16)

        u1 = u1_buf[...] + cb_ref[...]
        mu = jnp.mean(u1, axis=-1, keepdims=True)
        var = jnp.mean(jnp.square(u1 - mu), axis=-1, keepdims=True)
        rstd = lax.rsqrt(var + LN_EPS)
        nrm = (u1 - mu) * rstd
        n_ref[...] = nrm.astype(BF16)
        rstd_ref[...] = jnp.broadcast_to(rstd, (ts, LANES))
        u2 = nrm * lng_ref[...] + lnb_ref[...]
        u3 = (u2 * _sig(u2)).astype(BF16)
        u3_ref[...] = u3
        y_b = _dot(u3, wb_v[...])
        yb_ref[...] = y_b.astype(BF16)

        merged = (_sig(pjs[5]) * y_a + _sig(pjs[6]) * y_b).astype(BF16)
        mg_ref[...] = merged
        mix = _dot(merged, wo_v[...])
        mix_ref[...] = mix
        x1_ref[...] = xv + g1 * mix

    row = lambda w, dt=None: pl.BlockSpec((ts, w), lambda i: (i, 0))
    full = lambda a: pl.BlockSpec(a.shape, lambda i: (0, 0))
    sd = lambda w, dt: jax.ShapeDtypeStruct((s_len, w), dt)
    outs = [sd(7 * d, BF16), sd(d, BF16), sd(d, BF16), sd(d, BF16), sd(d, BF16), sd(LANES, F32), sd(d, BF16), sd(d, BF16),
            sd(d, BF16), sd(d, BF16), sd(d, F32), sd(d, F32)]
    res = _pallas(
        body, name="fwd_mix", grid=(n_t,),
        out_shape=tuple(outs) + tuple(jax.ShapeDtypeStruct(w.shape, BF16) for w in later),
        in_specs=[row(d), full(mod), full(gn1), full(w3), full(w31), full(cbias), full(lng), full(lnb), ANY, ANY, ANY, ANY]
        + [ANY] * n_l,
        out_specs=tuple(row(o.shape[1]) for o in outs) + tuple([ANY] * n_l),
        input_output_aliases={12 + k: len(outs) + k for k in range(n_l)},
        scratch_shapes=[pltpu.VMEM(w_in.shape, BF16), pltpu.VMEM(w_a.shape, BF16), pltpu.VMEM(w_b.shape, BF16),
                        pltpu.VMEM(w_o.shape, BF16), pltpu.VMEM((ts + HALO_SHORT, d), F32),
                        pltpu.VMEM((ts + HALO_CONF, d), F32), pltpu.VMEM((7, ts + HALO_CONF - 8, d // 2), F32),
                        pltpu.VMEM((ts, d), F32), pltpu.SemaphoreType.DMA((4,))] + _gather_sems(n_l),
        compiler_params=_params("arbitrary"),
    )(x, mod, gn1, w3, w31, cbias, lng, lnb, w_in, w_a, w_b, w_o, *later)
    return res[:len(outs)], res[len(outs):]


def _ffn(x1, target, mod, gn2, gf, w_fi, w_fo, ts):
    s_len, d = x1.shape
    f = w_fo.shape[0]
    n_t = s_len // ts

    def body(x1_ref, t_ref, mod_ref, gn2_ref, gf_ref, wfi_hbm, wfo_hbm,
             h2_ref, hid_ref, df_ref, dab_ref, dx1_ref, acc_ref, wfi_v, wfo_v, sems):
        _load_weights([(wfi_hbm, wfi_v), (wfo_hbm, wfo_v)], sems)

        @pl.when(pl.program_id(0) == 0)
        def _():
            acc_ref[...] = jnp.zeros(acc_ref.shape, F32)

        x1v = x1_ref[...]
        sh2, sc2, g2 = mod_ref[:, 3 * d:4 * d], mod_ref[:, 4 * d:5 * d], mod_ref[:, 5 * d:6 * d]
        gn2v, gfv = gn2_ref[...], gf_ref[...]
        r2 = lax.rsqrt(jnp.mean(x1v * x1v, axis=-1, keepdims=True) + EPS)
        xn2 = x1v * r2
        h2 = ((xn2 * gn2v) * (1.0 + sc2) + sh2).astype(BF16)
        h2_ref[...] = h2
        a = _dot(h2, wfi_v[:, 0:f])
        bg = _dot(h2, wfi_v[:, f:2 * f])
        sa = _sig(a)
        silu_a = a * sa
        hid = (silu_a * bg).astype(BF16)
        hid_ref[...] = hid
        ffn = _dot(hid, wfo_v[...])
        x2 = x1v + g2 * ffn
        r3 = lax.rsqrt(jnp.mean(x2 * x2, axis=-1, keepdims=True) + EPS)
        xn3 = x2 * r3
        diff = xn3 * gfv - t_ref[...]
        dy = diff * (1.0 / d)
        dxn3 = dy * gfv
        dx2 = r3 * (dxn3 - xn3 * jnp.mean(dxn3 * xn3, axis=-1, keepdims=True))
        acc_ref[ROW_GF:ROW_GF + 1, :] += _colsum(dy * xn3)
        acc_ref[ROW_DMOD + 5:ROW_DMOD + 5 + 1, :] += _colsum(dx2 * ffn)
        acc_ref[ROW_SQERR:ROW_SQERR + 1, :] += _colsum(diff * diff)

        df = (dx2 * g2).astype(BF16)
        df_ref[...] = df
        dhid = _dot_nt(df, wfo_v[...])
        da = dhid * bg * (sa * (1.0 + a * (1.0 - sa)))
        dbg = dhid * silu_a
        dab_ref[:, 0:f] = da.astype(BF16)
        dab_ref[:, f:2 * f] = dbg.astype(BF16)
        dh2 = _dot_nt(da.astype(BF16), wfi_v[:, 0:f]) + _dot_nt(dbg.astype(BF16), wfi_v[:, f:2 * f])
        acc_ref[ROW_DMOD + 3:ROW_DMOD + 3 + 1, :] += _colsum(dh2)
        acc_ref[ROW_DMOD + 4:ROW_DMOD + 4 + 1, :] += _colsum(dh2 * (xn2 * gn2v))
        acc_ref[ROW_GN2:ROW_GN2 + 1, :] += _colsum(dh2 * xn2 * (1.0 + sc2))
        dxn2 = dh2 * gn2v * (1.0 + sc2)
        dx1_ref[...] = dx2 + r2 * (dxn2 - xn2 * jnp.mean(dxn2 * xn2, axis=-1, keepdims=True))

    row = lambda w: pl.BlockSpec((ts, w), lambda i: (i, 0))
    full = lambda a: pl.BlockSpec(a.shape, lambda i: (0, 0))
    sd = lambda w, dt: jax.ShapeDtypeStruct((s_len, w), dt)
    outs = [sd(d, BF16), sd(f, BF16), sd(d, BF16), sd(2 * f, BF16), sd(d, F32), jax.ShapeDtypeStruct((N_SMALL_ROWS, d), F32)]
    return _pallas(
        body, name="ffn", grid=(n_t,), out_shape=tuple(outs),
        in_specs=[row(d), row(d), full(mod), full(gn2), full(gf), ANY, ANY],
        out_specs=tuple([row(o.shape[1]) for o in outs[:-1]] + [pl.BlockSpec((N_SMALL_ROWS, d), lambda i: (0, 0))]),
        scratch_shapes=[pltpu.VMEM(w_fi.shape, BF16), pltpu.VMEM(w_fo.shape, BF16), pltpu.SemaphoreType.DMA((2,))],
        compiler_params=_params("arbitrary"),
    )(x1, target, mod, gn2, gf, w_fi, w_fo)


def _bwd_mix(dx1, proj, nrm, rstd, c3, ya, yb, mix, mod, w3, w31, lng, lnb, w_a, w_b, w_o, ready, ready_axes, ts):
    s_len, d = dx1.shape
    n_t = s_len // ts
    n_r = len(ready)
    ready_shapes = [g.shape for g in ready]

    def body(*refs):
        (dx1_ref, proj_ref, n_ref, rstd_ref, c3_ref, ya_ref, yb_ref, mix_ref, mod_ref, w3_ref, w31_ref, lng_ref,
         lnb_ref, wa_hbm, wb_hbm, wo_hbm) = refs[:16]
        ready_refs = refs[16:16 + n_r]
        dproj_ref, dmix_ref, dya_ref, dyb_ref, acc_ref = refs[16 + n_r:21 + n_r]
        recv_refs = refs[21 + n_r:21 + 2 * n_r]
        wa_v, wb_v, wo_v, dc3_ext, du1_ext, e_buf, u0_buf, du0_buf, dw8, sems, e_send, e_recv = refs[21 + 2 * n_r:]
        step = pl.program_id(0)
        pl.when(step == 0)(lambda: _exchange_send(ready_refs, recv_refs, ready_shapes, ready_axes, e_send, e_recv))
        pl.when(step == n_t - 1)(lambda: _exchange_finish(recv_refs, e_send, e_recv))
        _load_weights([(wa_hbm, wa_v), (wb_hbm, wb_v), (wo_hbm, wo_v)], sems)

        @pl.when(pl.program_id(0) == 0)
        def _():
            acc_ref[...] = jnp.zeros(acc_ref.shape, F32)
            dc3_ext[ts:ts + HALO_SHORT, :] = jnp.zeros((HALO_SHORT, d), F32)
            du1_ext[ts:ts + HALO_CONF, :] = jnp.zeros((HALO_CONF, d), F32)
            dw8[...] = jnp.zeros(dw8.shape, F32)

        pj = lambda k: proj_ref[:, k * d:(k + 1) * d].astype(F32)
        g1 = mod_ref[:, 2 * d:3 * d]
        dx1v = dx1_ref[...]
        acc_ref[ROW_DMOD + 2:ROW_DMOD + 2 + 1, :] += _colsum(dx1v * mix_ref[...])
        dmix = (dx1v * g1).astype(BF16)
        dmix_ref[...] = dmix
        dmerged = _dot_nt(dmix, wo_v[...])
        g_a, g_b = _sig(pj(5)), _sig(pj(6))
        y_a, y_b = ya_ref[...].astype(F32), yb_ref[...].astype(F32)
        dproj_ref[:, 5 * d:6 * d] = (dmerged * y_a * g_a * (1.0 - g_a)).astype(BF16)
        dproj_ref[:, 6 * d:7 * d] = (dmerged * y_b * g_b * (1.0 - g_b)).astype(BF16)
        dya = (dmerged * g_a).astype(BF16)
        dyb = (dmerged * g_b).astype(BF16)
        dya_ref[...] = dya
        dyb_ref[...] = dyb

        dyap = _dot_nt(dya, wa_v[...])
        dproj_ref[:, 0:d] = (dyap * c3_ref[...].astype(F32)).astype(BF16)
        dc3_ext[0:ts, :] = dyap * pj(0)
        c_s, v_s = pj(1), pj(2)
        cv = c_s * v_s
        dcv = jnp.zeros((ts, d), F32)
        for j in range(SHORT_K):
            k = SHORT_K - 1 - j
            sj = dc3_ext[j:j + ts, :]
            dcv = dcv + w3_ref[k:k + 1, :] * sj
            acc_ref[ROW_W3 + k:ROW_W3 + k + 1, :] += _colsum(cv * sj)
        dc3_ext[ts:ts + HALO_SHORT, :] = dc3_ext[0:HALO_SHORT, :]
        dproj_ref[:, d:2 * d] = (dcv * v_s).astype(BF16)
        dproj_ref[:, 2 * d:3 * d] = (dcv * c_s).astype(BF16)

        du3 = _dot_nt(dyb, wb_v[...])
        nv = n_ref[...].astype(F32)
        lg = lng_ref[...]
        u2 = nv * lg + lnb_ref[...]
        s2 = _sig(u2)
        du2 = du3 * (s2 * (1.0 + u2 * (1.0 - s2)))
        acc_ref[ROW_LNG:ROW_LNG + 1, :] += _colsum(du2 * nv)
        acc_ref[ROW_LNB:ROW_LNB + 1, :] += _colsum(du2)
        dn = du2 * lg
        du1 = rstd_ref[:, 0:1] * (dn - jnp.mean(dn, axis=-1, keepdims=True) - nv * jnp.mean(dn * nv, axis=-1, keepdims=True))
        acc_ref[ROW_CB:ROW_CB + 1, :] += _colsum(du1)
        du1_ext[0:ts, :] = du1
        v_c = pj(3)
        sg = _sig(pj(4))
        u0_buf[...] = v_c * sg

        def conv_chunk(tap, r0, lo):
            lanes = slice(lo, lo + CONV_LANES)
            u0c = u0_buf[pl.ds(r0, CONV_ROWS), lanes]
            acc = jnp.zeros((CONV_ROWS, CONV_LANES), F32)
            for j in range(CONF_K):
                k = CONF_K - 1 - j
                sj = tap(j)
                acc = acc + w31_ref[k:k + 1, lanes] * sj
                prod = u0c * sj
                part = prod[0:8]
                for i in range(1, CONV_ROWS // 8):
                    part = part + prod[8 * i:8 * i + 8]
                dw8[k, :, lanes] += part
            du0_buf[pl.ds(r0, CONV_ROWS), lanes] = acc

        _conv_chunks(du1_ext, e_buf, ts, d, conv_chunk, unroll=False)
        du0 = du0_buf[...]
        du1_ext[ts:ts + HALO_CONF, :] = du1_ext[0:HALO_CONF, :]

        @pl.when(step == n_t - 1)
        def _():
            for k in range(CONF_K):
                acc_ref[ROW_W31 + k:ROW_W31 + k + 1, :] = _colsum(dw8[k])
        dproj_ref[:, 3 * d:4 * d] = (du0 * sg).astype(BF16)
        dproj_ref[:, 4 * d:5 * d] = (du0 * v_c * sg * (1.0 - sg)).astype(BF16)

    rev = lambda w: pl.BlockSpec((ts, w), lambda i: (n_t - 1 - i, 0))
    full = lambda a: pl.BlockSpec(a.shape, lambda i: (0, 0))
    sd = lambda w, dt: jax.ShapeDtypeStruct((s_len, w), dt)
    outs = [sd(7 * d, BF16), sd(d, BF16), sd(d, BF16), sd(d, BF16), jax.ShapeDtypeStruct((N_SMALL_ROWS, d), F32)]
    res = _pallas(
        body, name="bwd_mix", grid=(n_t,), out_shape=tuple(outs) + _exchange_out_shapes(ready, ready_axes),
        in_specs=[rev(d), rev(7 * d), rev(d), rev(LANES), rev(d), rev(d), rev(d), rev(d), full(mod), full(w3), full(w31),
                  full(lng), full(lnb), ANY, ANY, ANY] + [ANY] * n_r,
        out_specs=tuple([rev(o.shape[1]) for o in outs[:-1]] + [pl.BlockSpec((N_SMALL_ROWS, d), lambda i: (0, 0))] + [ANY] * n_r),
        scratch_shapes=[pltpu.VMEM(w_a.shape, BF16), pltpu.VMEM(w_b.shape, BF16), pltpu.VMEM(w_o.shape, BF16),
                        pltpu.VMEM((ts + HALO_SHORT, d), F32), pltpu.VMEM((ts + HALO_CONF, d), F32),
                        pltpu.VMEM((7, ts + HALO_CONF - 8, d // 2), F32), pltpu.VMEM((ts, d), F32), pltpu.VMEM((ts, d), F32),
                        pltpu.VMEM((CONF_K, 8, d), F32), pltpu.SemaphoreType.DMA((3,))] + _exchange_sems(n_r),
        compiler_params=_params("arbitrary"),
    )(dx1, proj, nrm, rstd, c3, ya, yb, mix, mod, w3, w31, lng, lnb, w_a, w_b, w_o, *ready)
    return res[:len(outs)], res[len(outs):]


def _bwd_in(dproj, x, dx1, mod, gn1, w_in, ready, ready_axes, ts):
    s_len, d = x.shape
    n_t = s_len // ts
    n_r = len(ready)
    ready_shapes = [g.shape for g in ready]

    def body(*refs):
        dproj_ref, x_ref, dx1_ref, mod_ref, gn1_ref, win_hbm = refs[:6]
        ready_refs = refs[6:6 + n_r]
        gx_ref, acc_ref = refs[6 + n_r:8 + n_r]
        recv_refs = refs[8 + n_r:8 + 2 * n_r]
        win_v, sems, e_send, e_recv = refs[8 + 2 * n_r:]
        step = pl.program_id(0)
        pl.when(step == 0)(lambda: _exchange_send(ready_refs, recv_refs, ready_shapes, ready_axes, e_send, e_recv))
        pl.when(step == n_t - 1)(lambda: _exchange_finish(recv_refs, e_send, e_recv))
        _load_weights([(win_hbm, win_v)], sems)

        @pl.when(pl.program_id(0) == 0)
        def _():
            acc_ref[...] = jnp.zeros(acc_ref.shape, F32)

        xv = x_ref[...]
        sc1 = mod_ref[:, d:2 * d]
        gn1v = gn1_ref[...]
        r = lax.rsqrt(jnp.mean(xv * xv, axis=-1, keepdims=True) + EPS)
        xn = xv * r
        dh = _dot_nt(dproj_ref[...], win_v[...])
        acc_ref[ROW_DMOD:ROW_DMOD + 1, :] += _colsum(dh)
        acc_ref[ROW_DMOD + 1:ROW_DMOD + 1 + 1, :] += _colsum(dh * (xn * gn1v))
        acc_ref[ROW_GN1:ROW_GN1 + 1, :] += _colsum(dh * xn * (1.0 + sc1))
        dxn = dh * gn1v * (1.0 + sc1)
        gx_ref[...] = dx1_ref[...] + r * (dxn - xn * jnp.mean(dxn * xn, axis=-1, keepdims=True))

    row = lambda w: pl.BlockSpec((ts, w), lambda i: (i, 0))
    full = lambda a: pl.BlockSpec(a.shape, lambda i: (0, 0))
    res = _pallas(
        body, name="bwd_in", grid=(n_t,),
        out_shape=(jax.ShapeDtypeStruct((s_len, d), F32), jax.ShapeDtypeStruct((N_SMALL_ROWS, d), F32))
        + _exchange_out_shapes(ready, ready_axes),
        in_specs=[row(7 * d), row(d), row(d), full(mod), full(gn1), ANY] + [ANY] * n_r,
        out_specs=(row(d), pl.BlockSpec((N_SMALL_ROWS, d), lambda i: (0, 0))) + tuple([ANY] * n_r),
        scratch_shapes=[pltpu.VMEM(w_in.shape, BF16), pltpu.SemaphoreType.DMA((1,))] + _exchange_sems(n_r),
        compiler_params=_params("arbitrary"),
    )(dproj, x, dx1, mod, gn1, w_in, *ready)
    return res[0], res[1], res[2:]


def _weight_grad(a, b, bn, name, transposed=False, ready=(), ready_axes=()):
    s_len, m = a.shape
    n = b.shape[1]
    n_j = n // bn
    n_r = len(ready)
    ready_shapes = [g.shape for g in ready]

    def body(*refs):
        a_hbm, b_ref = refs[:2]
        ready_refs = refs[2:2 + n_r]
        o_ref = refs[2 + n_r]
        recv_refs = refs[3 + n_r:3 + 2 * n_r]
        a_v, sem = refs[3 + 2 * n_r:5 + 2 * n_r]
        step = pl.program_id(0)
        if n_r:
            e_send, e_recv = refs[5 + 2 * n_r:]
            pl.when(step == 0)(lambda: _exchange_send(ready_refs, recv_refs, ready_shapes, ready_axes, e_send, e_recv))
            pl.when(step == n_j - 1)(lambda: _exchange_finish(recv_refs, e_send, e_recv))
        _load_weights([(a_hbm, a_v)], sem)
        prod = _dot_tn(a_v[...], b_ref[...])
        o_ref[...] = (prod.T if transposed else prod).astype(BF16)

    out = jax.ShapeDtypeStruct((n, m) if transposed else (m, n), BF16)
    out_spec = pl.BlockSpec((bn, m), lambda j: (j, 0)) if transposed else pl.BlockSpec((m, bn), lambda j: (0, j))
    res = _pallas(
        body, name=name, grid=(n_j,), out_shape=(out,) + _exchange_out_shapes(ready, ready_axes),
        in_specs=[ANY, pl.BlockSpec((s_len, bn), lambda j: (0, j))] + [ANY] * n_r,
        out_specs=(out_spec,) + tuple([ANY] * n_r),
        scratch_shapes=[pltpu.VMEM(a.shape, BF16), pltpu.SemaphoreType.DMA((1,))] + (_exchange_sems(n_r) if n_r else []),
        compiler_params=_params("arbitrary"),
    )(a, b, *ready)
    return res[0], res[1:]


def _sum_slots(recv, grad, axis, chip, core, name):
    n_s, r, c = recv.shape
    tr = 16
    for cand in (256, 128, 176, 64, 32, 16):
        if r % cand == 0:
            tr = cand
            break
    n_i = r // tr

    def body(s_ref, own_ref, r_ref, o_ref):
        tot = own_ref[...].astype(F32)
        for k in range(n_s):
            tot = tot + r_ref[k].astype(F32)
        o_ref[...] = tot

    if axis == 0:
        own_map = lambda i, s: ((2 * s[0] + s[1]) * n_i + i, 0)
    else:
        own_map = lambda i, s: (s[1] * n_i + i, s[0])
    grid_spec = pltpu.PrefetchScalarGridSpec(
        num_scalar_prefetch=1, grid=(n_i,),
        in_specs=[pl.BlockSpec((tr, c), own_map), pl.BlockSpec((n_s, tr, c), lambda i, s: (0, i, 0))],
        out_specs=pl.BlockSpec((tr, c), lambda i, s: (s[1] * n_i + i, 0)))
    return _pallas(body, name=name, grid_spec=grid_spec, out_shape=jax.ShapeDtypeStruct((2 * r, c), F32),
                   compiler_params=_params("arbitrary"))(jnp.stack([chip, core]).astype(jnp.int32), grad, recv)


def _adamw_math(w, g, m, v):
    m = ADAM_B1 * m + (1.0 - ADAM_B1) * g
    v = ADAM_B2 * v + (1.0 - ADAM_B2) * (g * g)
    m_hat = m / (1.0 - ADAM_B1 ** ADAM_STEP)
    v_hat = v / (1.0 - ADAM_B2 ** ADAM_STEP)
    delta = -ADAM_LR * (m_hat / (jnp.sqrt(v_hat) + ADAM_EPS) + ADAM_WD * w)
    return delta, m, v


def _adamw(w, g, m, v, name):
    r, c = w.shape
    tr = r
    for cand in (256, 128, 176, 64, 32, 16, 8):
        if r % cand == 0:
            tr = cand
            break

    def body(w_ref, g_ref, m_ref, v_ref, go_ref, d_ref, nm_ref, nv_ref):
        g = g_ref[...]
        go_ref[...] = g
        d_ref[...], nm_ref[...], nv_ref[...] = _adamw_math(w_ref[...], g, m_ref[...], v_ref[...])

    blk = pl.BlockSpec((tr, c), lambda i: (i, 0))
    return _pallas(body, name=name, grid=(r // tr,), out_shape=tuple(jax.ShapeDtypeStruct((r, c), F32) for _ in range(4)),
                   in_specs=[blk] * 4, out_specs=(blk,) * 4, compiler_params=_params("arbitrary"))(w, g, m, v)


_SMALL = [
    ("b_ada", ROW_DMOD, "mod"), ("norm_mix_g", ROW_GN1, "row"), ("conv_short_w", ROW_W3, "shard"),
    ("conv_conf_w", ROW_W31, "shard"), ("conv_conf_b", ROW_CB, "row"), ("conf_ln_g", ROW_LNG, "row"),
    ("conf_ln_b", ROW_LNB, "row"), ("norm_ffn_g", ROW_GN2, "row"), ("final_norm_g", ROW_GF, "row")]


def _small_update(p_all, chip_onehot, ws, ms, vs):
    n = len(_SMALL)
    d = p_all.shape[2]

    def body(*refs):
        p_ref, oh_ref = refs[:2]
        w_refs, m_refs, v_refs = refs[2:2 + n], refs[2 + n:2 + 2 * n], refs[2 + 2 * n:2 + 3 * n]
        loss_ref = refs[2 + 3 * n]
        g_out, d_out, m_out, v_out = [refs[3 + (3 + q) * n:3 + (4 + q) * n] for q in range(4)]
        g_all = p_ref[0]
        for s in range(1, N_DEV):
            g_all = g_all + p_ref[s]
        loss_ref[...] = jnp.zeros(loss_ref.shape, F32) + (0.5 / d) * jnp.sum(g_all[ROW_SQERR:ROW_SQERR + 1, :])

        def emit(i, idx, g):
            dl, nm, nv = _adamw_math(w_refs[i][idx], g, m_refs[i][idx], v_refs[i][idx])
            g_out[i][idx], d_out[i][idx], m_out[i][idx], v_out[i][idx] = g, dl, nm, nv

        everything = (slice(None), slice(None))
        for i, (_, row, kind) in enumerate(_SMALL):
            if kind == "mod":
                for k in range(N_MOD):
                    emit(i, (slice(0, 1), slice(k * d, (k + 1) * d)), g_all[row + k:row + k + 1, :])
            elif kind == "row":
                emit(i, everything, g_all[row:row + 1, :])
            else:
                taps, dq = w_refs[i].shape
                g = jnp.zeros((taps, dq), F32)
                for j in range(N_CHIP):
                    g = g + oh_ref[0:1, j:j + 1] * g_all[row:row + taps, j * dq:(j + 1) * dq]
                emit(i, everything, g)

    shapes = [jax.ShapeDtypeStruct(w.shape, F32) for w in ws]
    res = _pallas(body, name="small_update", out_shape=tuple([jax.ShapeDtypeStruct((8, 128), F32)] + shapes * 4),
                  in_specs=[VMEM_FULL] * (2 + 3 * n), out_specs=tuple([VMEM_FULL] * (1 + 4 * n)),
                  compiler_params=_params())(p_all, chip_onehot, *ws, *ms, *vs)
    return res[0], [res[1 + q * n:1 + (q + 1) * n] for q in range(4)]


def _w_ada_grad(c_t, dmod_part):
    def body(c_ref, g_ref, o_ref):
        cv = c_ref[...]
        o_ref[...] = _dot((cv * _sig(cv)).astype(BF16), g_ref[...].astype(BF16))

    return _pallas(body, name="w_ada_grad", out_shape=jax.ShapeDtypeStruct((c_t.shape[0], dmod_part.shape[1]), F32),
                   in_specs=[VMEM_FULL] * 2, out_specs=VMEM_FULL, compiler_params=_params())(c_t, dmod_part)


def _step(x, c, w_ada, b_ada, norm_mix_g, w_in, conv_short_w, w_short_out, conv_conf_w, conv_conf_b, conf_ln_g,
          conf_ln_b, w_conf_out, w_o, norm_ffn_g, w_ffn_in, w_ffn_out, final_norm_g, loss_target, moments, ts):
    xi, yi, ci = _place()
    chip = 2 * xi + yi
    me = 4 * xi + 2 * yi + ci
    x2d, tgt = x[0], loss_target[0]
    s_len, d = x2d.shape
    dq = d // N_CHIP
    mq = N_MOD * d // N_CHIP
    gf = final_norm_g.reshape(1, d)

    taps = jnp.concatenate([jnp.pad(conv_short_w[0], ((0, 8 - SHORT_K), (0, 0))),
                            jnp.pad(conv_conf_w[0], ((0, 32 - CONF_K), (0, 0)))], axis=0)
    n_tr = 40 * dq // d
    c_taps = jnp.concatenate([jnp.broadcast_to(c, (8, d)), taps.reshape(n_tr, d), jnp.zeros((16 - n_tr, d), F32)], axis=0)
    b_part = lax.dynamic_slice(b_ada, (0, chip * mq), (1, mq))
    big = [w_in[0], w_short_out[0], w_conf_out[0], w_o[0], w_ffn_in[0], w_ffn_out[0]]
    axes = [1, 0, 0, 0, 1, 0]
    big_names = ["w_in", "w_short_out", "w_conf_out", "w_o", "w_ffn_in", "w_ffn_out"]
    (win_f, wa_f, wb_f, wo_f, *placed_ffn), c_taps, mod_all = _gather_weights(big, axes, 4, c_taps, w_ada[0], b_part,
                                                                              "gather_weights")
    c_taps = c_taps.reshape(N_DEV, 24, d)
    c_all = c_taps[:, 0]
    mod_all = mod_all.reshape(N_CHIP, 2, N_DEV, mq)[:, 0]
    mod_full = jnp.transpose(mod_all, (1, 0, 2)).reshape(N_DEV, N_MOD * d)
    mod = lax.dynamic_slice(mod_full, (me, 0), (1, N_MOD * d))

    taps_all = c_taps[:, 8:8 + n_tr].reshape(N_CHIP, 2, 40, dq)[:, 0]
    taps_full = jnp.transpose(taps_all, (1, 0, 2)).reshape(40, d)
    w3, w31 = taps_full[0:SHORT_K], taps_full[8:8 + CONF_K]

    (proj, h, yap, c3, nrm, rstd, u3, ya, yb, merged, mix, x1), (wfi_f, wfo_f) = _fwd_mix(
        x2d, mod, norm_mix_g, w3, w31, conv_conf_b, conf_ln_g, conf_ln_b, win_f, wa_f, wb_f, wo_f, placed_ffn, axes[4:], ts)
    h2, hid, df, dab, dx1, acc_f = _ffn(x1, tgt, mod, norm_ffn_g, gf, wfi_f, wfo_f, ts)
    f = wfo_f.shape[0]
    bn = min(512, d)
    g_ffn = [_weight_grad(h2, dab, bn, "grad_w_ffn_in")[0], _weight_grad(df, hid, bn // 2, "grad_w_ffn_out", transposed=True)[0]]
    (dproj, dmix, dya, dyb, acc_m), r_ffn = _bwd_mix(dx1, proj, nrm, rstd, c3, ya, yb, mix, mod, w3, w31, conf_ln_g,
                                                    conf_ln_b, wa_f, wb_f, wo_f, g_ffn, axes[4:], ts)
    g_sq = [_weight_grad(yap, dya, bn // 2, "grad_w_short_out")[0], _weight_grad(u3, dyb, bn // 2, "grad_w_conf_out")[0],
            _weight_grad(merged, dmix, bn // 2, "grad_w_o")[0]]
    g_in, r_sq = _weight_grad(h, dproj, bn, "grad_w_in", ready=g_sq, ready_axes=axes[1:4])
    grad_x, acc_i, r_in = _bwd_in(dproj, x2d, dx1, mod, norm_mix_g, win_f, [g_in], axes[:1], min(2 * ts, s_len))
    g_in = [g_in]
    r_in_sq = list(r_in) + list(r_sq)
    grads = g_in + g_sq + g_ffn
    recv = list(r_in_sq) + list(r_ffn)
    g_big, p_all = _swap_halves([_sum_slots(r, g, ax, chip, ci, "sum_slots_" + n)
                                 for r, g, ax, n in zip(recv, grads, axes, big_names)], acc_f + acc_m + acc_i)
    p_all = p_all.reshape(N_DEV, N_SMALL_ROWS, d)

    small_w = dict(b_ada=b_ada, norm_mix_g=norm_mix_g, conv_short_w=conv_short_w, conv_conf_w=conv_conf_w,
                   conv_conf_b=conv_conf_b, conf_ln_g=conf_ln_g, conf_ln_b=conf_ln_b, norm_ffn_g=norm_ffn_g,
                   final_norm_g=final_norm_g)
    as_2d = lambda t: t.reshape((-1, t.shape[-1]))
    chip_onehot = (lax.broadcasted_iota(jnp.int32, (1, 128), 1) == chip).astype(F32)
    loss_tile, small_out = _small_update(
        p_all, chip_onehot, [as_2d(small_w[n]) for n, _, _ in _SMALL], [as_2d(moments["m_" + n]) for n, _, _ in _SMALL],
        [as_2d(moments["v_" + n]) for n, _, _ in _SMALL])
    small_g, small_d, small_m, small_v = [{n: o.reshape(small_w[n].shape) for (n, _, _), o in zip(_SMALL, outs)}
                                          for outs in small_out]

    dmod_all = p_all[:, 0:N_MOD].reshape(N_DEV, N_MOD * d)
    dmod_part = jnp.pad(lax.dynamic_slice(dmod_all, (0, chip * mq), (N_DEV, mq)), ((0, 128 - N_DEV), (0, 0)))
    c_t = jnp.pad(jnp.transpose(c_all), ((0, 0), (0, 128 - N_DEV)))
    g_ada = _w_ada_grad(c_t, dmod_part)

    out_g, out_d, out_m, out_v = dict(small_g), dict(small_d), dict(small_m), dict(small_v)
    for name, w, g in zip(["w_ada"] + big_names, [w_ada[0]] + big, [g_ada] + list(g_big)):
        go, dl, nm, nv = _adamw(w, g, moments["m_" + name][0], moments["v_" + name][0], "adamw_" + name)
        out_g[name], out_d[name], out_m[name], out_v[name] = go[None], dl[None], nm[None], nv[None]

    loss = loss_tile[0, 0]
    return loss, grad_x[None], out_g, out_d, out_m, out_v


_WEIGHTS = ["w_ada", "b_ada", "norm_mix_g", "w_in", "conv_short_w", "w_short_out", "conv_conf_w", "conv_conf_b",
            "conf_ln_g", "conf_ln_b", "w_conf_out", "w_o", "norm_ffn_g", "w_ffn_in", "w_ffn_out", "final_norm_g"]
ROW_TILE = 256


def kernel(x, c, w_ada, b_ada, norm_mix_g, w_in, conv_short_w, w_short_out, conv_conf_w, conv_conf_b, conf_ln_g, conf_ln_b, w_conf_out, w_o, norm_ffn_g, w_ffn_in, w_ffn_out, final_norm_g, loss_target, m_w_ada, m_b_ada, m_norm_mix_g, m_w_in, m_conv_short_w, m_w_short_out, m_conv_conf_w, m_conv_conf_b, m_conf_ln_g, m_conf_ln_b, m_w_conf_out, m_w_o, m_norm_ffn_g, m_w_ffn_in, m_w_ffn_out, m_final_norm_g, v_w_ada, v_b_ada, v_norm_mix_g, v_w_in, v_conv_short_w, v_w_short_out, v_conv_conf_w, v_conv_conf_b, v_conf_ln_g, v_conf_ln_b, v_w_conf_out, v_w_o, v_norm_ffn_g, v_w_ffn_in, v_w_ffn_out, v_final_norm_g):
    moments = dict(
        m_w_ada=m_w_ada, m_b_ada=m_b_ada, m_norm_mix_g=m_norm_mix_g, m_w_in=m_w_in, m_conv_short_w=m_conv_short_w,
        m_w_short_out=m_w_short_out, m_conv_conf_w=m_conv_conf_w, m_conv_conf_b=m_conv_conf_b, m_conf_ln_g=m_conf_ln_g,
        m_conf_ln_b=m_conf_ln_b, m_w_conf_out=m_w_conf_out, m_w_o=m_w_o, m_norm_ffn_g=m_norm_ffn_g, m_w_ffn_in=m_w_ffn_in,
        m_w_ffn_out=m_w_ffn_out, m_final_norm_g=m_final_norm_g,
        v_w_ada=v_w_ada, v_b_ada=v_b_ada, v_norm_mix_g=v_norm_mix_g, v_w_in=v_w_in, v_conv_short_w=v_conv_short_w,
        v_w_short_out=v_w_short_out, v_conv_conf_w=v_conv_conf_w, v_conv_conf_b=v_conv_conf_b, v_conf_ln_g=v_conf_ln_g,
        v_conf_ln_b=v_conf_ln_b, v_w_conf_out=v_w_conf_out, v_w_o=v_w_o, v_norm_ffn_g=v_norm_ffn_g, v_w_ffn_in=v_w_ffn_in,
        v_w_ffn_out=v_w_ffn_out, v_final_norm_g=v_final_norm_g)
    loss, grad_x, g, dl, nm, nv = _step(
        x, c, w_ada, b_ada, norm_mix_g, w_in, conv_short_w, w_short_out, conv_conf_w, conv_conf_b, conf_ln_g, conf_ln_b,
        w_conf_out, w_o, norm_ffn_g, w_ffn_in, w_ffn_out, final_norm_g, loss_target, moments, min(ROW_TILE, x.shape[1]))
    return (loss, grad_x, *[g[n] for n in _WEIGHTS], *[dl[n] for n in _WEIGHTS], *[nm[n] for n in _WEIGHTS],
            *[nv[n] for n in _WEIGHTS])
```

```python
import jax
import jax.numpy as jnp
from jax import lax
from jax.experimental import pallas as pl
from jax.experimental.pallas import tpu as pltpu

F32 = jnp.float32
BF16 = jnp.bfloat16
EPS = 1e-6
LN_EPS = 1e-5
SHORT_K = 3
CONF_K = 31
N_MOD = 6
N_DEV = 8
N_CHIP = 4
ADAM_LR = 0.001
ADAM_B1 = 0.9
ADAM_B2 = 0.999
ADAM_EPS = 1e-08
ADAM_WD = 0.01
ADAM_STEP = 10
MESH = pl.DeviceIdType.MESH
V7X_VMEM_LIMIT_BYTES = 60 * 1024 * 1024
LANES = 128
HALO_SHORT = 8
HALO_CONF = 32
N_SMALL_ROWS = 48
ROW_DMOD, ROW_GN1, ROW_W3, ROW_W31, ROW_CB, ROW_LNG, ROW_LNB, ROW_GN2, ROW_GF, ROW_SQERR = 0, 6, 7, 10, 41, 42, 43, 44, 45, 46
CONV_ROWS = 64
CONV_LANES = 256

ANY = pl.BlockSpec(memory_space=pl.ANY)
VMEM_FULL = pl.BlockSpec(memory_space=pltpu.VMEM)


def _pallas(body, **kw):
    return pl.pallas_call(body, **kw)


def _params(*sem):
    return pltpu.CompilerParams(dimension_semantics=sem, vmem_limit_bytes=V7X_VMEM_LIMIT_BYTES)


def _sig(z):
    return jax.nn.sigmoid(z)


def _dot(a, b):
    return jnp.dot(a, b, preferred_element_type=F32)


def _dot_nt(a, b):
    return lax.dot_general(a, b, (((1,), (1,)), ((), ())), preferred_element_type=F32)


def _dot_tn(a, b):
    return lax.dot_general(a, b, (((0,), (0,)), ((), ())), preferred_element_type=F32)


def _colsum(z):
    return jnp.sum(z, axis=0, keepdims=True)


def _place():
    return lax.axis_index("x"), lax.axis_index("y"), lax.axis_index("c")


def _small_gather_copies(x_ref, out_ref, send_sems, recv_sems, local_sem):
    m_per = x_ref.shape[0]
    x, y, c = _place()
    me, sibling = (x, y, c), (x, y, 1 - c)
    chips = [(1 - x, y), (x, 1 - y), (1 - x, 1 - y)]

    def rows(px, py, pc):
        return out_ref.at[pl.ds((4 * px + 2 * py + pc) * m_per, m_per), :]

    def copy(k, blk, to, src=None):
        return pltpu.make_async_remote_copy(
            src_ref=rows(*blk) if src is None else src, dst_ref=rows(*blk),
            send_sem=send_sems.at[k], recv_sem=recv_sems.at[k], device_id=to, device_id_type=MESH)

    mine = pltpu.make_async_copy(x_ref, rows(*me), local_sem)
    first = [copy(0, me, sibling, src=x_ref)] + [copy(1 + j, me, (*chip, c), src=x_ref) for j, chip in enumerate(chips)]
    return me, sibling, chips, c, copy, mine, first


def _small_gather_start(x_ref, out_ref, send_sems, recv_sems, local_sem):
    _, _, _, _, _, mine, first = _small_gather_copies(x_ref, out_ref, send_sems, recv_sems, local_sem)
    mine.start()
    for cp in first:
        cp.start()


def _small_gather_finish(x_ref, out_ref, send_sems, recv_sems, local_sem):
    me, sibling, chips, c, copy, mine, first = _small_gather_copies(x_ref, out_ref, send_sems, recv_sems, local_sem)
    passed = [copy(4 + j, (*chip, c), sibling) for j, chip in enumerate(chips)]
    for j, chip in enumerate(chips):
        copy(1 + j, (*chip, c), me).wait_recv()
        passed[j].start()
    copy(0, sibling, me).wait_recv()
    for j, chip in enumerate(chips):
        copy(4 + j, (*chip, 1 - c), me).wait_recv()
    for cp in first + passed:
        cp.wait_send()
    mine.wait()


def _small_gather(x_ref, out_ref, *sems):
    _small_gather_start(x_ref, out_ref, *sems)
    _small_gather_finish(x_ref, out_ref, *sems)


def _small_gather_sems():
    return [pltpu.SemaphoreType.DMA((7,)), pltpu.SemaphoreType.DMA((7,)), pltpu.SemaphoreType.DMA]


def _shard_slice(ref, full_shape, shard_axis, chip, half, quarter=None):
    m, n = full_shape
    r = (m // N_CHIP if shard_axis == 0 else m) // 2
    start = half * r + (chip * 2 * r if shard_axis == 0 else 0)
    if quarter is not None:
        r = r // 2
        start = start + quarter * r
    if shard_axis == 0:
        return ref.at[pl.ds(start, r), :]
    cw = n // N_CHIP
    return ref.at[pl.ds(start, r), pl.ds(pl.multiple_of(chip * cw, 128), cw)]


def _gather_copy(ref, region, k, a, to, sems):
    return pltpu.make_async_remote_copy(src_ref=region, dst_ref=region, send_sem=sems[0].at[a, k], recv_sem=sems[1].at[a, k],
                                        device_id=to, device_id_type=MESH)


def _gather_send(refs, shapes, axes, *sems):
    x, y, c = _place()
    for a in range(len(refs)):
        mine = _shard_slice(refs[a], shapes[a], axes[a], 2 * x + y, c)
        _gather_copy(refs[a], mine, 0, a, (1 - x, y, c), sems).start()
        _gather_copy(refs[a], mine, 1, a, (x, 1 - y, c), sems).start()


def _gather_forward(refs, shapes, axes, *sems):
    x, y, c = _place()
    me, sibling, x_nbr, y_nbr = (x, y, c), (x, y, 1 - c), (1 - x, y, c), (x, 1 - y, c)
    chip_x, chip_y, chip_d = 2 * (1 - x) + y, 2 * x + (1 - y), 2 * (1 - x) + (1 - y)
    for a in range(len(refs)):
        region = lambda chip, quarter=None, a=a: _shard_slice(refs[a], shapes[a], axes[a], chip, c, quarter)
        _gather_copy(refs[a], region(chip_x), 0, a, me, sems).wait_recv()
        _gather_copy(refs[a], region(chip_x, 1), 3, a, y_nbr, sems).start()
        _gather_copy(refs[a], region(chip_x), 4, a, sibling, sems).start()
        _gather_copy(refs[a], region(chip_y), 1, a, me, sems).wait_recv()
        _gather_copy(refs[a], region(chip_y, 0), 2, a, x_nbr, sems).start()
        _gather_copy(refs[a], region(chip_y), 5, a, sibling, sems).start()


def _gather_forward_diagonal(refs, shapes, axes, *sems):
    x, y, c = _place()
    me, sibling, chip_d = (x, y, c), (x, y, 1 - c), 2 * (1 - x) + (1 - y)
    for a in range(len(refs)):
        region = lambda chip, quarter=None, a=a: _shard_slice(refs[a], shapes[a], axes[a], chip, c, quarter)
        _gather_copy(refs[a], region(chip_d, 0), 2, a, me, sems).wait_recv()
        _gather_copy(refs[a], region(chip_d, 1), 3, a, me, sems).wait_recv()
        _gather_copy(refs[a], region(chip_d), 6, a, sibling, sems).start()


def _gather_finish(refs, shapes, axes, *sems):
    x, y, c = _place()
    me = (x, y, c)
    chip_me, chip_x, chip_y, chip_d = 2 * x + y, 2 * (1 - x) + y, 2 * x + (1 - y), 2 * (1 - x) + (1 - y)
    for a in range(len(refs)):
        region = lambda chip, half, quarter=None, a=a: _shard_slice(refs[a], shapes[a], axes[a], chip, half, quarter)
        for k, chip in ((4, chip_x), (5, chip_y), (6, chip_d)):
            _gather_copy(refs[a], region(chip, 1 - c), k, a, me, sems).wait_recv()
            _gather_copy(refs[a], region(chip, c), k, a, me, sems).wait_send()
        _gather_copy(refs[a], region(chip_me, c), 0, a, me, sems).wait_send()
        _gather_copy(refs[a], region(chip_me, c), 1, a, me, sems).wait_send()
        _gather_copy(refs[a], region(chip_y, c, 0), 2, a, me, sems).wait_send()
        _gather_copy(refs[a], region(chip_x, c, 1), 3, a, me, sems).wait_send()


def _gather_sems(n_w):
    return [pltpu.SemaphoreType.DMA((n_w, 7)), pltpu.SemaphoreType.DMA((n_w, 7))]


def _gather_weights(shards, axes, n_gather, c_taps, w_ada, b_part, name):
    n_w = len(shards)
    shapes = [(w.shape[0] * N_CHIP, w.shape[1]) if ax == 0 else (w.shape[0], w.shape[1] * N_CHIP) for w, ax in zip(shards, axes)]
    r_ct = c_taps.shape[0]
    mq = w_ada.shape[1]

    def body(*refs):
        ins = refs[:n_w]
        ct_ref, wada_ref, b_ref = refs[n_w:n_w + 3]
        outs = refs[n_w + 3:2 * n_w + 3]
        ct_all_ref, mod_all_ref = refs[2 * n_w + 3:2 * n_w + 5]
        scratch = refs[2 * n_w + 5:]
        stage32, stage16, mod_part = scratch[:n_w], scratch[n_w:2 * n_w], scratch[2 * n_w]
        load_sems, store_sems = scratch[2 * n_w + 1:2 * n_w + 3]
        sems, ct_sems, mod_sems = scratch[2 * n_w + 3:2 * n_w + 5], scratch[2 * n_w + 5:2 * n_w + 8], scratch[2 * n_w + 8:]
        x, y, c = _place()
        chip = 2 * x + y
        loads = [pltpu.make_async_copy(ins[a], stage32[a], load_sems.at[a]) for a in range(n_w)]
        for cp in loads:
            cp.start()
        _small_gather_start(ct_ref, ct_all_ref, *ct_sems)
        stores = []
        for a in range(n_w):
            loads[a].wait()
            stage16[a][...] = stage32[a][...].astype(BF16)
            r, cw = shards[a].shape
            own = outs[a].at[pl.ds(chip * r, r), :] if axes[a] == 0 else outs[a].at[:, pl.ds(pl.multiple_of(chip * cw, 128), cw)]
            stores.append(pltpu.make_async_copy(stage16[a], own, store_sems.at[a]))
            stores[-1].start()
            if a < n_gather:
                half = stage16[a].at[pl.ds(c * (r // 2), r // 2), :]
                there = _shard_slice(outs[a], shapes[a], axes[a], chip, c)
                for k, to in ((0, (1 - x, y, c)), (1, (x, 1 - y, c))):
                    pltpu.make_async_remote_copy(src_ref=half, dst_ref=there, send_sem=sems[0].at[a, k],
                                                 recv_sem=sems[1].at[a, k], device_id=to, device_id_type=MESH).start()
        _small_gather_finish(ct_ref, ct_all_ref, *ct_sems)
        cv = jnp.concatenate([ct_all_ref[r_ct * dv:r_ct * dv + 1, :] for dv in range(N_DEV)], axis=0)
        mod_part[...] = _dot((cv * _sig(cv)).astype(BF16), wada_ref[...].astype(BF16)) + b_ref[...]
        _small_gather(mod_part, mod_all_ref, *mod_sems)
        _gather_forward(outs[:n_gather], shapes, axes, *sems)
        _gather_forward_diagonal(outs[:n_gather], shapes, axes, *sems)
        _gather_finish(outs[:n_gather], shapes, axes, *sems)
        for cp in stores:
            cp.wait()

    res = _pallas(
        body, name=name,
        out_shape=tuple(jax.ShapeDtypeStruct(sh, BF16) for sh in shapes)
        + (jax.ShapeDtypeStruct((N_DEV * r_ct, c_taps.shape[1]), F32), jax.ShapeDtypeStruct((N_DEV * N_DEV, mq), F32)),
        in_specs=[ANY] * n_w + [VMEM_FULL] * 3, out_specs=tuple([ANY] * n_w + [VMEM_FULL] * 2),
        scratch_shapes=[pltpu.VMEM(w.shape, F32) for w in shards] + [pltpu.VMEM(w.shape, BF16) for w in shards]
        + [pltpu.VMEM((N_DEV, mq), F32), pltpu.SemaphoreType.DMA((n_w,)), pltpu.SemaphoreType.DMA((n_w,))]
        + _gather_sems(n_gather) + _small_gather_sems() + _small_gather_sems(),
        compiler_params=_params(),
    )(*shards, c_taps, w_ada, b_part)
    return res[:n_w], res[n_w], res[n_w + 1]


def _piece_shapes(grads, axes):
    return [((g.shape[0] // N_CHIP // 2, g.shape[1]) if ax == 0 else (g.shape[0] // 2, g.shape[1] // N_CHIP))
            for g, ax in zip(grads, axes)]


def _exchange_send(ins, outs, shapes, axes, send_sems, recv_sems):
    x, y, c = _place()
    me = 4 * x + 2 * y + c
    for a in range(len(ins)):
        for t in range(N_DEV):
            tx, ty, tc = t // 4, (t // 2) % 2, t % 2
            src = _shard_slice(ins[a], shapes[a], axes[a], 2 * tx + ty, tc)

            @pl.when(me != t)
            def _(src=src, a=a, t=t, to=(tx, ty, tc)):
                slot = (me - t + N_DEV) % N_DEV - 1
                pltpu.make_async_remote_copy(src_ref=src, dst_ref=outs[a].at[slot], send_sem=send_sems.at[a, t],
                                             recv_sem=recv_sems.at[a, slot], device_id=to, device_id_type=MESH).start()


def _exchange_finish(outs, send_sems, recv_sems):
    x, y, c = _place()
    me = 4 * x + 2 * y + c
    for a in range(len(outs)):
        for t in range(N_DEV):
            @pl.when(me != t)
            def _(a=a, t=t):
                slot = outs[a].at[0]
                pltpu.make_async_remote_copy(src_ref=slot, dst_ref=slot, send_sem=send_sems.at[a, t], recv_sem=recv_sems.at[a, 0],
                                             device_id=(x, y, c), device_id_type=MESH).wait_send()
        for k in range(N_DEV - 1):
            slot = outs[a].at[k]
            pltpu.make_async_remote_copy(src_ref=slot, dst_ref=slot, send_sem=send_sems.at[a, 0], recv_sem=recv_sems.at[a, k],
                                         device_id=(x, y, c), device_id_type=MESH).wait_recv()


def _exchange_sems(n_w):
    return [pltpu.SemaphoreType.DMA((n_w, N_DEV)), pltpu.SemaphoreType.DMA((n_w, N_DEV - 1))]


def _exchange_out_shapes(grads, axes):
    return tuple(jax.ShapeDtypeStruct((N_DEV - 1,) + p, BF16) for p in _piece_shapes(grads, axes))


def _swap_halves(fulls, block):
    n_w = len(fulls)

    def body(*refs):
        block_ref = refs[n_w]
        outs, gathered_ref = refs[n_w + 1:2 * n_w + 1], refs[2 * n_w + 1]
        send_sems, recv_sems = refs[2 * n_w + 2:2 * n_w + 4]
        x, y, c = _place()
        sibling = (x, y, 1 - c)
        waits = []
        for a in range(n_w):
            r = fulls[a].shape[0] // 2
            mine = outs[a].at[pl.ds(c * r, r), :]
            theirs = outs[a].at[pl.ds((1 - c) * r, r), :]
            rc = pltpu.make_async_remote_copy(src_ref=mine, dst_ref=mine, send_sem=send_sems.at[a], recv_sem=recv_sems.at[a],
                                              device_id=sibling, device_id_type=MESH)
            rc.start()
            got = pltpu.make_async_remote_copy(src_ref=theirs, dst_ref=theirs, send_sem=send_sems.at[a],
                                               recv_sem=recv_sems.at[a], device_id=sibling, device_id_type=MESH)
            waits += [rc.wait_send, got.wait_recv]
        _small_gather(block_ref, gathered_ref, *refs[2 * n_w + 4:])
        for w in waits:
            w()

    res = _pallas(
        body, name="swap_halves",
        out_shape=tuple(jax.ShapeDtypeStruct(f.shape, f.dtype) for f in fulls)
        + (jax.ShapeDtypeStruct((N_DEV * block.shape[0], block.shape[1]), block.dtype),),
        in_specs=[ANY] * n_w + [VMEM_FULL], out_specs=tuple([ANY] * n_w + [VMEM_FULL]),
        input_output_aliases={a: a for a in range(n_w)},
        scratch_shapes=[pltpu.SemaphoreType.DMA((n_w,)), pltpu.SemaphoreType.DMA((n_w,))] + _small_gather_sems(),
    )(*fulls, block)
    return res[:n_w], res[n_w]


def _load_weights(pairs, sems):
    @pl.when(pl.program_id(0) == 0)
    def _():
        cps = [pltpu.make_async_copy(s, d, sems.at[k]) for k, (s, d) in enumerate(pairs)]
        for cp in cps:
            cp.start()
        for cp in cps:
            cp.wait()


def _conv_chunks(ext, e_buf, ts, d, chunk_fn, unroll, between=()):
    rows, dh = e_buf.shape[1], e_buf.shape[2]
    for hf in range(d // dh):
        for r in range(1, 8):
            e_buf[r - 1, :, :] = ext[r:r + rows, hf * dh:(hf + 1) * dh]
        for lq in range(dh // CONV_LANES):
            lo, el = hf * dh + lq * CONV_LANES, lq * CONV_LANES

            def body(ci, carry, lo=lo, el=el):
                r0 = pl.multiple_of(ci * CONV_ROWS, CONV_ROWS)

                def tap(off):
                    q, r = divmod(off, 8)
                    if r == 0:
                        return ext[pl.ds(r0 + 8 * q, CONV_ROWS), lo:lo + CONV_LANES]
                    return e_buf[r - 1, pl.ds(r0 + 8 * q, CONV_ROWS), el:el + CONV_LANES]

                chunk_fn(tap, r0, lo)
                return carry

            lax.fori_loop(0, ts // CONV_ROWS, body, 0, unroll=unroll)
            if between:
                between[0]()
                between = between[1:]
    for fn in between:
        fn()


def _fwd_mix(x, mod, gn1, w3, w31, cbias, lng, lnb, w_in, w_a, w_b, w_o, later, later_axes, ts):
    s_len, d = x.shape
    n_t = s_len // ts
    n_l = len(later)
    later_shapes = [w.shape for w in later]

    def body(*refs):
        (x_ref, mod_ref, gn1_ref, w3_ref, w31_ref, cb_ref, lng_ref, lnb_ref, win_hbm, wa_hbm, wb_hbm, wo_hbm) = refs[:12]
        (proj_ref, h_ref, lhs3_ref, c3_ref, n_ref, rstd_ref, ya_ref, yb_ref, mix_ref, x1_ref) = refs[12 + n_l:22 + n_l]
        later_refs = refs[22 + n_l:22 + 2 * n_l]
        win_v, wa_v, wb_v, wo_v, cv_ext, u0_ext, e_buf, u1_buf, sems, g_send, g_recv = refs[22 + 2 * n_l:]
        step = pl.program_id(0)
        pl.when(step == 0)(lambda: _gather_send(later_refs, later_shapes, later_axes, g_send, g_recv))
        pl.when(step == n_t // 2)(lambda: _gather_forward(later_refs, later_shapes, later_axes, g_send, g_recv))
        pl.when(step == (3 * n_t) // 4)(lambda: _gather_forward_diagonal(later_refs, later_shapes, later_axes, g_send, g_recv))
        pl.when(step == n_t - 1)(lambda: _gather_finish(later_refs, later_shapes, later_axes, g_send, g_recv))
        _load_weights([(win_hbm, win_v), (wa_hbm, wa_v), (wb_hbm, wb_v), (wo_hbm, wo_v)], sems)

        @pl.when(pl.program_id(0) == 0)
        def _():
            cv_ext[0:HALO_SHORT, :] = jnp.zeros((HALO_SHORT, d), F32)
            u0_ext[0:HALO_CONF, :] = jnp.zeros((HALO_CONF, d), F32)

        xv = x_ref[...]
        sh1, sc1, g1 = mod_ref[:, 0:d], mod_ref[:, d:2 * d], mod_ref[:, 2 * d:3 * d]
        r = lax.rsqrt(jnp.mean(xv * xv, axis=-1, keepdims=True) + EPS)
        hb = (((xv * r) * gn1_ref[...]) * (1.0 + sc1) + sh1).astype(BF16)
        h_ref[...] = hb

        def pj(k):
            p = _dot(hb, win_v[:, k * d:(k + 1) * d])
            proj_ref[:, k * d:(k + 1) * d] = p.astype(BF16)
            return p

        v_c = pj(3)
        u0_ext[HALO_CONF:HALO_CONF + ts, :] = v_c * _sig(pj(4))

        def conv_chunk(tap, r0, lo):
            acc = jnp.zeros((CONV_ROWS, CONV_LANES), F32)
            for k in range(CONF_K):
                acc = acc + w31_ref[k:k + 1, lo:lo + CONV_LANES] * tap(HALO_CONF - (CONF_K - 1) + k)
            u1_buf[pl.ds(r0, CONV_ROWS), lo:lo + CONV_LANES] = acc

        pjs = {}
        _conv_chunks(u0_ext, e_buf, ts, d, conv_chunk, unroll=True,
                     between=[lambda k=k: pjs.__setitem__(k, pj(k)) for k in (1, 2, 0, 5, 6)])
        u0_ext[0:HALO_CONF, :] = u0_ext[ts:ts + HALO_CONF, :]

        cv_ext[HALO_SHORT:HALO_SHORT + ts, :] = pjs[1] * pjs[2]
        conv3 = jnp.zeros((ts, d), F32)
        for k in range(SHORT_K):
            off = HALO_SHORT - (SHORT_K - 1) + k
            conv3 = conv3 + w3_ref[k:k + 1, :] * cv_ext[off:off + ts, :]
        cv_ext[0:HALO_SHORT, :] = cv_ext[ts:ts + HALO_SHORT, :]
        c3_ref[...] = conv3.astype(BF16)
        yap = (pjs[0] * conv3).astype(BF16)
        lhs3_ref[:, 0:d] = yap
        y_a = _dot(yap, wa_v[...])
        ya_ref[...] = y_a.astype(BF16)

        u1 = u1_buf[...] + cb_ref[...]
        mu = jnp.mean(u1, axis=-1, keepdims=True)
        var = jnp.mean(jnp.square(u1 - mu), axis=-1, keepdims=True)
        rstd = lax.rsqrt(var + LN_EPS)
        nrm = (u1 - mu) * rstd
        n_ref[...] = nrm.astype(BF16)
        rstd_ref[...] = jnp.broadcast_to(rstd, (ts, LANES))
        u2 = nrm * lng_ref[...] + lnb_ref[...]
        u3 = (u2 * _sig(u2)).astype(BF16)
        lhs3_ref[:, d:2 * d] = u3
        y_b = _dot(u3, wb_v[...])
        yb_ref[...] = y_b.astype(BF16)

        merged = (_sig(pjs[5]) * y_a + _sig(pjs[6]) * y_b).astype(BF16)
        lhs3_ref[:, 2 * d:3 * d] = merged
        mix = _dot(merged, wo_v[...])
        mix_ref[...] = mix
        x1_ref[...] = xv + g1 * mix

    row = lambda w, dt=None: pl.BlockSpec((ts, w), lambda i: (i, 0))
    full = lambda a: pl.BlockSpec(a.shape, lambda i: (0, 0))
    sd = lambda w, dt: jax.ShapeDtypeStruct((s_len, w), dt)
    outs = [sd(7 * d, BF16), sd(d, BF16), sd(3 * d, BF16), sd(d, BF16), sd(d, BF16), sd(LANES, F32), sd(d, BF16), sd(d, BF16),
            sd(d, F32), sd(d, F32)]
    res = _pallas(
        body, name="fwd_mix", grid=(n_t,),
        out_shape=tuple(outs) + tuple(jax.ShapeDtypeStruct(w.shape, BF16) for w in later),
        in_specs=[row(d), full(mod), full(gn1), full(w3), full(w31), full(cbias), full(lng), full(lnb), ANY, ANY, ANY, ANY]
        + [ANY] * n_l,
        out_specs=tuple(row(o.shape[1]) for o in outs) + tuple([ANY] * n_l),
        input_output_aliases={12 + k: len(outs) + k for k in range(n_l)},
        scratch_shapes=[pltpu.VMEM(w_in.shape, BF16), pltpu.VMEM(w_a.shape, BF16), pltpu.VMEM(w_b.shape, BF16),
                        pltpu.VMEM(w_o.shape, BF16), pltpu.VMEM((ts + HALO_SHORT, d), F32),
                        pltpu.VMEM((ts + HALO_CONF, d), F32), pltpu.VMEM((7, ts + HALO_CONF - 8, d // 2), F32),
                        pltpu.VMEM((ts, d), F32), pltpu.SemaphoreType.DMA((4,))] + _gather_sems(n_l),
        compiler_params=_params("arbitrary"),
    )(x, mod, gn1, w3, w31, cbias, lng, lnb, w_in, w_a, w_b, w_o, *later)
    return res[:len(outs)], res[len(outs):]


def _ffn(x1, target, mod, gn2, gf, w_fi, w_fo, ts):
    s_len, d = x1.shape
    f = w_fo.shape[0]
    n_t = s_len // ts

    def body(x1_ref, t_ref, mod_ref, gn2_ref, gf_ref, wfi_hbm, wfo_hbm,
             h2_ref, hid_ref, df_ref, dab_ref, dx1_ref, acc_ref, wfi_v, wfo_v, sems):
        _load_weights([(wfi_hbm, wfi_v), (wfo_hbm, wfo_v)], sems)

        @pl.when(pl.program_id(0) == 0)
        def _():
            acc_ref[...] = jnp.zeros(acc_ref.shape, F32)

        x1v = x1_ref[...]
        sh2, sc2, g2 = mod_ref[:, 3 * d:4 * d], mod_ref[:, 4 * d:5 * d], mod_ref[:, 5 * d:6 * d]
        gn2v, gfv = gn2_ref[...], gf_ref[...]
        r2 = lax.rsqrt(jnp.mean(x1v * x1v, axis=-1, keepdims=True) + EPS)
        xn2 = x1v * r2
        h2 = ((xn2 * gn2v) * (1.0 + sc2) + sh2).astype(BF16)
        h2_ref[...] = h2
        a = _dot(h2, wfi_v[:, 0:f])
        bg = _dot(h2, wfi_v[:, f:2 * f])
        sa = _sig(a)
        silu_a = a * sa
        hid = (silu_a * bg).astype(BF16)
        hid_ref[...] = hid
        ffn = _dot(hid, wfo_v[...])
        x2 = x1v + g2 * ffn
        r3 = lax.rsqrt(jnp.mean(x2 * x2, axis=-1, keepdims=True) + EPS)
        xn3 = x2 * r3
        diff = xn3 * gfv - t_ref[...]
        dy = diff * (1.0 / d)
        dxn3 = dy * gfv
        dx2 = r3 * (dxn3 - xn3 * jnp.mean(dxn3 * xn3, axis=-1, keepdims=True))
        acc_ref[ROW_GF:ROW_GF + 1, :] += _colsum(dy * xn3)
        acc_ref[ROW_DMOD + 5:ROW_DMOD + 5 + 1, :] += _colsum(dx2 * ffn)
        acc_ref[ROW_SQERR:ROW_SQERR + 1, :] += _colsum(diff * diff)

        df = (dx2 * g2).astype(BF16)
        df_ref[...] = df
        dhid = _dot_nt(df, wfo_v[...])
        da = dhid * bg * (sa * (1.0 + a * (1.0 - sa)))
        dbg = dhid * silu_a
        dab_ref[:, 0:f] = da.astype(BF16)
        dab_ref[:, f:2 * f] = dbg.astype(BF16)
        dh2 = _dot_nt(da.astype(BF16), wfi_v[:, 0:f]) + _dot_nt(dbg.astype(BF16), wfi_v[:, f:2 * f])
        acc_ref[ROW_DMOD + 3:ROW_DMOD + 3 + 1, :] += _colsum(dh2)
        acc_ref[ROW_DMOD + 4:ROW_DMOD + 4 + 1, :] += _colsum(dh2 * (xn2 * gn2v))
        acc_ref[ROW_GN2:ROW_GN2 + 1, :] += _colsum(dh2 * xn2 * (1.0 + sc2))
        dxn2 = dh2 * gn2v * (1.0 + sc2)
        dx1_ref[...] = dx2 + r2 * (dxn2 - xn2 * jnp.mean(dxn2 * xn2, axis=-1, keepdims=True))

    row = lambda w: pl.BlockSpec((ts, w), lambda i: (i, 0))
    full = lambda a: pl.BlockSpec(a.shape, lambda i: (0, 0))
    sd = lambda w, dt: jax.ShapeDtypeStruct((s_len, w), dt)
    outs = [sd(d, BF16), sd(f, BF16), sd(d, BF16), sd(2 * f, BF16), sd(d, F32), jax.ShapeDtypeStruct((N_SMALL_ROWS, d), F32)]
    return _pallas(
        body, name="ffn", grid=(n_t,), out_shape=tuple(outs),
        in_specs=[row(d), row(d), full(mod), full(gn2), full(gf), ANY, ANY],
        out_specs=tuple([row(o.shape[1]) for o in outs[:-1]] + [pl.BlockSpec((N_SMALL_ROWS, d), lambda i: (0, 0))]),
        scratch_shapes=[pltpu.VMEM(w_fi.shape, BF16), pltpu.VMEM(w_fo.shape, BF16), pltpu.SemaphoreType.DMA((2,))],
        compiler_params=_params("arbitrary"),
    )(x1, target, mod, gn2, gf, w_fi, w_fo)


def _bwd_mix(dx1, proj, nrm, rstd, c3, ya, yb, mix, mod, w3, w31, lng, lnb, w_a, w_b, w_o, ready, ready_axes, ts):
    s_len, d = dx1.shape
    n_t = s_len // ts
    n_r = len(ready)
    ready_shapes = [g.shape for g in ready]

    def body(*refs):
        (dx1_ref, proj_ref, n_ref, rstd_ref, c3_ref, ya_ref, yb_ref, mix_ref, mod_ref, w3_ref, w31_ref, lng_ref,
         lnb_ref, wa_hbm, wb_hbm, wo_hbm) = refs[:16]
        ready_refs = refs[16:16 + n_r]
        dproj_ref, rhs3_ref, acc_ref = refs[16 + n_r:19 + n_r]
        recv_refs = refs[19 + n_r:19 + 2 * n_r]
        wa_v, wb_v, wo_v, dc3_ext, du1_ext, e_buf, u0_buf, du0_buf, dw8, sems, e_send, e_recv = refs[19 + 2 * n_r:]
        step = pl.program_id(0)
        pl.when(step == 0)(lambda: _exchange_send(ready_refs, recv_refs, ready_shapes, ready_axes, e_send, e_recv))
        pl.when(step == n_t - 1)(lambda: _exchange_finish(recv_refs, e_send, e_recv))
        _load_weights([(wa_hbm, wa_v), (wb_hbm, wb_v), (wo_hbm, wo_v)], sems)

        @pl.when(pl.program_id(0) == 0)
        def _():
            acc_ref[...] = jnp.zeros(acc_ref.shape, F32)
            dc3_ext[ts:ts + HALO_SHORT, :] = jnp.zeros((HALO_SHORT, d), F32)
            du1_ext[ts:ts + HALO_CONF, :] = jnp.zeros((HALO_CONF, d), F32)
            dw8[...] = jnp.zeros(dw8.shape, F32)

        pj = lambda k: proj_ref[:, k * d:(k + 1) * d].astype(F32)
        g1 = mod_ref[:, 2 * d:3 * d]
        dx1v = dx1_ref[...]
        acc_ref[ROW_DMOD + 2:ROW_DMOD + 2 + 1, :] += _colsum(dx1v * mix_ref[...])
        dmix = (dx1v * g1).astype(BF16)
        rhs3_ref[:, 2 * d:3 * d] = dmix
        dmerged = _dot_nt(dmix, wo_v[...])
        g_a, g_b = _sig(pj(5)), _sig(pj(6))
        y_a, y_b = ya_ref[...].astype(F32), yb_ref[...].astype(F32)
        dproj_ref[:, 5 * d:6 * d] = (dmerged * y_a * g_a * (1.0 - g_a)).astype(BF16)
        dproj_ref[:, 6 * d:7 * d] = (dmerged * y_b * g_b * (1.0 - g_b)).astype(BF16)
        dya = (dmerged * g_a).astype(BF16)
        dyb = (dmerged * g_b).astype(BF16)
        rhs3_ref[:, 0:d] = dya
        rhs3_ref[:, d:2 * d] = dyb

        dyap = _dot_nt(dya, wa_v[...])
        dproj_ref[:, 0:d] = (dyap * c3_ref[...].astype(F32)).astype(BF16)
        dc3_ext[0:ts, :] = dyap * pj(0)
        c_s, v_s = pj(1), pj(2)
        cv = c_s * v_s
        dcv = jnp.zeros((ts, d), F32)
        for j in range(SHORT_K):
            k = SHORT_K - 1 - j
            sj = dc3_ext[j:j + ts, :]
            dcv = dcv + w3_ref[k:k + 1, :] * sj
            acc_ref[ROW_W3 + k:ROW_W3 + k + 1, :] += _colsum(cv * sj)
        dc3_ext[ts:ts + HALO_SHORT, :] = dc3_ext[0:HALO_SHORT, :]
        dproj_ref[:, d:2 * d] = (dcv * v_s).astype(BF16)
        dproj_ref[:, 2 * d:3 * d] = (dcv * c_s).astype(BF16)

        du3 = _dot_nt(dyb, wb_v[...])
        nv = n_ref[...].astype(F32)
        lg = lng_ref[...]
        u2 = nv * lg + lnb_ref[...]
        s2 = _sig(u2)
        du2 = du3 * (s2 * (1.0 + u2 * (1.0 - s2)))
        acc_ref[ROW_LNG:ROW_LNG + 1, :] += _colsum(du2 * nv)
        acc_ref[ROW_LNB:ROW_LNB + 1, :] += _colsum(du2)
        dn = du2 * lg
        du1 = rstd_ref[:, 0:1] * (dn - jnp.mean(dn, axis=-1, keepdims=True) - nv * jnp.mean(dn * nv, axis=-1, keepdims=True))
        acc_ref[ROW_CB:ROW_CB + 1, :] += _colsum(du1)
        du1_ext[0:ts, :] = du1
        v_c = pj(3)
        sg = _sig(pj(4))
        u0_buf[...] = v_c * sg

        def conv_chunk(tap, r0, lo):
            lanes = slice(lo, lo + CONV_LANES)
            u0c = u0_buf[pl.ds(r0, CONV_ROWS), lanes]
            acc = jnp.zeros((CONV_ROWS, CONV_LANES), F32)
            for j in range(CONF_K):
                k = CONF_K - 1 - j
                sj = tap(j)
                acc = acc + w31_ref[k:k + 1, lanes] * sj
                prod = u0c * sj
                part = prod[0:8]
                for i in range(1, CONV_ROWS // 8):
                    part = part + prod[8 * i:8 * i + 8]
                dw8[k, :, lanes] += part
            du0_buf[pl.ds(r0, CONV_ROWS), lanes] = acc

        _conv_chunks(du1_ext, e_buf, ts, d, conv_chunk, unroll=False)
        du0 = du0_buf[...]
        du1_ext[ts:ts + HALO_CONF, :] = du1_ext[0:HALO_CONF, :]

        @pl.when(step == n_t - 1)
        def _():
            for k in range(CONF_K):
                acc_ref[ROW_W31 + k:ROW_W31 + k + 1, :] = _colsum(dw8[k])
        dproj_ref[:, 3 * d:4 * d] = (du0 * sg).astype(BF16)
        dproj_ref[:, 4 * d:5 * d] = (du0 * v_c * sg * (1.0 - sg)).astype(BF16)

    rev = lambda w: pl.BlockSpec((ts, w), lambda i: (n_t - 1 - i, 0))
    full = lambda a: pl.BlockSpec(a.shape, lambda i: (0, 0))
    sd = lambda w, dt: jax.ShapeDtypeStruct((s_len, w), dt)
    outs = [sd(7 * d, BF16), sd(3 * d, BF16), jax.ShapeDtypeStruct((N_SMALL_ROWS, d), F32)]
    res = _pallas(
        body, name="bwd_mix", grid=(n_t,), out_shape=tuple(outs) + _exchange_out_shapes(ready, ready_axes),
        in_specs=[rev(d), rev(7 * d), rev(d), rev(LANES), rev(d), rev(d), rev(d), rev(d), full(mod), full(w3), full(w31),
                  full(lng), full(lnb), ANY, ANY, ANY] + [ANY] * n_r,
        out_specs=tuple([rev(o.shape[1]) for o in outs[:-1]] + [pl.BlockSpec((N_SMALL_ROWS, d), lambda i: (0, 0))] + [ANY] * n_r),
        scratch_shapes=[pltpu.VMEM(w_a.shape, BF16), pltpu.VMEM(w_b.shape, BF16), pltpu.VMEM(w_o.shape, BF16),
                        pltpu.VMEM((ts + HALO_SHORT, d), F32), pltpu.VMEM((ts + HALO_CONF, d), F32),
                        pltpu.VMEM((7, ts + HALO_CONF - 8, d // 2), F32), pltpu.VMEM((ts, d), F32), pltpu.VMEM((ts, d), F32),
                        pltpu.VMEM((CONF_K, 8, d), F32), pltpu.SemaphoreType.DMA((3,))] + _exchange_sems(n_r),
        compiler_params=_params("arbitrary"),
    )(dx1, proj, nrm, rstd, c3, ya, yb, mix, mod, w3, w31, lng, lnb, w_a, w_b, w_o, *ready)
    return res[:len(outs)], res[len(outs):]


def _bwd_in(dproj, x, dx1, mod, gn1, w_in, ready, ready_axes, ts):
    s_len, d = x.shape
    n_t = s_len // ts
    n_r = len(ready)
    ready_shapes = [g.shape for g in ready]

    def body(*refs):
        dproj_ref, x_ref, dx1_ref, mod_ref, gn1_ref, win_hbm = refs[:6]
        ready_refs = refs[6:6 + n_r]
        gx_ref, acc_ref = refs[6 + n_r:8 + n_r]
        recv_refs = refs[8 + n_r:8 + 2 * n_r]
        win_v, sems, e_send, e_recv = refs[8 + 2 * n_r:]
        step = pl.program_id(0)
        pl.when(step == 0)(lambda: _exchange_send(ready_refs, recv_refs, ready_shapes, ready_axes, e_send, e_recv))
        pl.when(step == n_t - 1)(lambda: _exchange_finish(recv_refs, e_send, e_recv))
        _load_weights([(win_hbm, win_v)], sems)

        @pl.when(pl.program_id(0) == 0)
        def _():
            acc_ref[...] = jnp.zeros(acc_ref.shape, F32)

        xv = x_ref[...]
        sc1 = mod_ref[:, d:2 * d]
        gn1v = gn1_ref[...]
        r = lax.rsqrt(jnp.mean(xv * xv, axis=-1, keepdims=True) + EPS)
        xn = xv * r
        dh = _dot_nt(dproj_ref[...], win_v[...])
        acc_ref[ROW_DMOD:ROW_DMOD + 1, :] += _colsum(dh)
        acc_ref[ROW_DMOD + 1:ROW_DMOD + 1 + 1, :] += _colsum(dh * (xn * gn1v))
        acc_ref[ROW_GN1:ROW_GN1 + 1, :] += _colsum(dh * xn * (1.0 + sc1))
        dxn = dh * gn1v * (1.0 + sc1)
        gx_ref[...] = dx1_ref[...] + r * (dxn - xn * jnp.mean(dxn * xn, axis=-1, keepdims=True))

    row = lambda w: pl.BlockSpec((ts, w), lambda i: (i, 0))
    full = lambda a: pl.BlockSpec(a.shape, lambda i: (0, 0))
    res = _pallas(
        body, name="bwd_in", grid=(n_t,),
        out_shape=(jax.ShapeDtypeStruct((s_len, d), F32), jax.ShapeDtypeStruct((N_SMALL_ROWS, d), F32))
        + _exchange_out_shapes(ready, ready_axes),
        in_specs=[row(7 * d), row(d), row(d), full(mod), full(gn1), ANY] + [ANY] * n_r,
        out_specs=(row(d), pl.BlockSpec((N_SMALL_ROWS, d), lambda i: (0, 0))) + tuple([ANY] * n_r),
        scratch_shapes=[pltpu.VMEM(w_in.shape, BF16), pltpu.SemaphoreType.DMA((1,))] + _exchange_sems(n_r),
        compiler_params=_params("arbitrary"),
    )(dproj, x, dx1, mod, gn1, w_in, *ready)
    return res[0], res[1], res[2:]


def _weight_grad(a, b, bn, name, transposed=False, ready=(), ready_axes=(), a_cols=None):
    s_len, m = a.shape[0], (a_cols or a.shape[1])
    n = b.shape[1]
    n_j = n // bn
    n_r = len(ready)
    ready_shapes = [g.shape for g in ready]

    def body(*refs):
        a_hbm, b_ref = refs[:2]
        ready_refs = refs[2:2 + n_r]
        o_ref = refs[2 + n_r]
        recv_refs = refs[3 + n_r:3 + 2 * n_r]
        a_v, sem = refs[3 + 2 * n_r:5 + 2 * n_r]
        step = pl.program_id(0)
        if n_r:
            e_send, e_recv = refs[5 + 2 * n_r:]
            pl.when(step == 0)(lambda: _exchange_send(ready_refs, recv_refs, ready_shapes, ready_axes, e_send, e_recv))
            pl.when(step == n_j - 1)(lambda: _exchange_finish(recv_refs, e_send, e_recv))
        if a_cols is None:
            _load_weights([(a_hbm, a_v)], sem)
            prod = _dot_tn(a_v[...], b_ref[...])
        else:
            prod = _dot_tn(a_hbm[...], b_ref[...])
        o_ref[...] = (prod.T if transposed else prod).astype(BF16)

    out = jax.ShapeDtypeStruct((n, m) if transposed else (m, n), BF16)
    out_spec = pl.BlockSpec((bn, m), lambda j: (j, 0)) if transposed else pl.BlockSpec((m, bn), lambda j: (0, j))
    res = _pallas(
        body, name=name, grid=(n_j,), out_shape=(out,) + _exchange_out_shapes(ready, ready_axes),
        in_specs=[ANY if a_cols is None else pl.BlockSpec((s_len, a_cols), lambda j: (0, (j * bn) // a_cols)),
                  pl.BlockSpec((s_len, bn), lambda j: (0, j))] + [ANY] * n_r,
        out_specs=(out_spec,) + tuple([ANY] * n_r),
        scratch_shapes=[pltpu.VMEM((8, 128) if a_cols else a.shape, BF16), pltpu.SemaphoreType.DMA((1,))]
        + (_exchange_sems(n_r) if n_r else []),
        compiler_params=_params("arbitrary"),
    )(a, b, *ready)
    return res[0], res[1:]


def _sum_slots(recv, grad, axis, chip, core, name):
    n_s, r, c = recv.shape
    tr = 16
    for cand in (256, 128, 176, 64, 32, 16):
        if r % cand == 0:
            tr = cand
            break
    n_i = r // tr

    def body(s_ref, own_ref, r_ref, o_ref):
        tot = own_ref[...].astype(F32)
        for k in range(n_s):
            tot = tot + r_ref[k].astype(F32)
        o_ref[...] = tot

    if axis == 0:
        own_map = lambda i, s: ((2 * s[0] + s[1]) * n_i + i, 0)
    else:
        own_map = lambda i, s: (s[1] * n_i + i, s[0])
    grid_spec = pltpu.PrefetchScalarGridSpec(
        num_scalar_prefetch=1, grid=(n_i,),
        in_specs=[pl.BlockSpec((tr, c), own_map), pl.BlockSpec((n_s, tr, c), lambda i, s: (0, i, 0))],
        out_specs=pl.BlockSpec((tr, c), lambda i, s: (s[1] * n_i + i, 0)))
    return _pallas(body, name=name, grid_spec=grid_spec, out_shape=jax.ShapeDtypeStruct((2 * r, c), F32),
                   compiler_params=_params("arbitrary"))(jnp.stack([chip, core]).astype(jnp.int32), grad, recv)


def _adamw_math(w, g, m, v):
    m = ADAM_B1 * m + (1.0 - ADAM_B1) * g
    v = ADAM_B2 * v + (1.0 - ADAM_B2) * (g * g)
    m_hat = m / (1.0 - ADAM_B1 ** ADAM_STEP)
    v_hat = v / (1.0 - ADAM_B2 ** ADAM_STEP)
    delta = -ADAM_LR * (m_hat / (jnp.sqrt(v_hat) + ADAM_EPS) + ADAM_WD * w)
    return delta, m, v


def _adamw(w, g, m, v, name):
    r, c = w.shape
    tr = r
    for cand in (256, 128, 176, 64, 32, 16, 8):
        if r % cand == 0:
            tr = cand
            break

    def body(w_ref, g_ref, m_ref, v_ref, go_ref, d_ref, nm_ref, nv_ref):
        g = g_ref[...]
        go_ref[...] = g
        d_ref[...], nm_ref[...], nv_ref[...] = _adamw_math(w_ref[...], g, m_ref[...], v_ref[...])

    blk = pl.BlockSpec((tr, c), lambda i: (i, 0))
    return _pallas(body, name=name, grid=(r // tr,), out_shape=tuple(jax.ShapeDtypeStruct((r, c), F32) for _ in range(4)),
                   in_specs=[blk] * 4, out_specs=(blk,) * 4, compiler_params=_params("arbitrary"))(w, g, m, v)


_SMALL = [
    ("b_ada", ROW_DMOD, "mod"), ("norm_mix_g", ROW_GN1, "row"), ("conv_short_w", ROW_W3, "shard"),
    ("conv_conf_w", ROW_W31, "shard"), ("conv_conf_b", ROW_CB, "row"), ("conf_ln_g", ROW_LNG, "row"),
    ("conf_ln_b", ROW_LNB, "row"), ("norm_ffn_g", ROW_GN2, "row"), ("final_norm_g", ROW_GF, "row")]


def _small_update(p_all, chip_onehot, ws, ms, vs):
    n = len(_SMALL)
    d = p_all.shape[2]

    def body(*refs):
        p_ref, oh_ref = refs[:2]
        w_refs, m_refs, v_refs = refs[2:2 + n], refs[2 + n:2 + 2 * n], refs[2 + 2 * n:2 + 3 * n]
        loss_ref = refs[2 + 3 * n]
        g_out, d_out, m_out, v_out = [refs[3 + (3 + q) * n:3 + (4 + q) * n] for q in range(4)]
        g_all = p_ref[0]
        for s in range(1, N_DEV):
            g_all = g_all + p_ref[s]
        loss_ref[...] = jnp.zeros(loss_ref.shape, F32) + (0.5 / d) * jnp.sum(g_all[ROW_SQERR:ROW_SQERR + 1, :])

        def emit(i, idx, g):
            dl, nm, nv = _adamw_math(w_refs[i][idx], g, m_refs[i][idx], v_refs[i][idx])
            g_out[i][idx], d_out[i][idx], m_out[i][idx], v_out[i][idx] = g, dl, nm, nv

        everything = (slice(None), slice(None))
        for i, (_, row, kind) in enumerate(_SMALL):
            if kind == "mod":
                for k in range(N_MOD):
                    emit(i, (slice(0, 1), slice(k * d, (k + 1) * d)), g_all[row + k:row + k + 1, :])
            elif kind == "row":
                emit(i, everything, g_all[row:row + 1, :])
            else:
                taps, dq = w_refs[i].shape
                g = jnp.zeros((taps, dq), F32)
                for j in range(N_CHIP):
                    g = g + oh_ref[0:1, j:j + 1] * g_all[row:row + taps, j * dq:(j + 1) * dq]
                emit(i, everything, g)

    shapes = [jax.ShapeDtypeStruct(w.shape, F32) for w in ws]
    res = _pallas(body, name="small_update", out_shape=tuple([jax.ShapeDtypeStruct((8, 128), F32)] + shapes * 4),
                  in_specs=[VMEM_FULL] * (2 + 3 * n), out_specs=tuple([VMEM_FULL] * (1 + 4 * n)),
                  compiler_params=_params())(p_all, chip_onehot, *ws, *ms, *vs)
    return res[0], [res[1 + q * n:1 + (q + 1) * n] for q in range(4)]


def _w_ada_grad(c_t, dmod_part):
    def body(c_ref, g_ref, o_ref):
        cv = c_ref[...]
        o_ref[...] = _dot((cv * _sig(cv)).astype(BF16), g_ref[...].astype(BF16))

    return _pallas(body, name="w_ada_grad", out_shape=jax.ShapeDtypeStruct((c_t.shape[0], dmod_part.shape[1]), F32),
                   in_specs=[VMEM_FULL] * 2, out_specs=VMEM_FULL, compiler_params=_params())(c_t, dmod_part)


def _step(x, c, w_ada, b_ada, norm_mix_g, w_in, conv_short_w, w_short_out, conv_conf_w, conv_conf_b, conf_ln_g,
          conf_ln_b, w_conf_out, w_o, norm_ffn_g, w_ffn_in, w_ffn_out, final_norm_g, loss_target, moments, ts):
    xi, yi, ci = _place()
    chip = 2 * xi + yi
    me = 4 * xi + 2 * yi + ci
    x2d, tgt = x[0], loss_target[0]
    s_len, d = x2d.shape
    dq = d // N_CHIP
    mq = N_MOD * d // N_CHIP
    gf = final_norm_g.reshape(1, d)

    taps = jnp.concatenate([jnp.pad(conv_short_w[0], ((0, 8 - SHORT_K), (0, 0))),
                            jnp.pad(conv_conf_w[0], ((0, 32 - CONF_K), (0, 0)))], axis=0)
    n_tr = 40 * dq // d
    c_taps = jnp.concatenate([jnp.broadcast_to(c, (8, d)), taps.reshape(n_tr, d), jnp.zeros((16 - n_tr, d), F32)], axis=0)
    b_part = lax.dynamic_slice(b_ada, (0, chip * mq), (1, mq))
    big = [w_in[0], w_short_out[0], w_conf_out[0], w_o[0], w_ffn_in[0], w_ffn_out[0]]
    axes = [1, 0, 0, 0, 1, 0]
    big_names = ["w_in", "w_short_out", "w_conf_out", "w_o", "w_ffn_in", "w_ffn_out"]
    (win_f, wa_f, wb_f, wo_f, *placed_ffn), c_taps, mod_all = _gather_weights(big, axes, 4, c_taps, w_ada[0], b_part,
                                                                              "gather_weights")
    c_taps = c_taps.reshape(N_DEV, 24, d)
    c_all = c_taps[:, 0]
    mod_all = mod_all.reshape(N_CHIP, 2, N_DEV, mq)[:, 0]
    mod_full = jnp.transpose(mod_all, (1, 0, 2)).reshape(N_DEV, N_MOD * d)
    mod = lax.dynamic_slice(mod_full, (me, 0), (1, N_MOD * d))

    taps_all = c_taps[:, 8:8 + n_tr].reshape(N_CHIP, 2, 40, dq)[:, 0]
    taps_full = jnp.transpose(taps_all, (1, 0, 2)).reshape(40, d)
    w3, w31 = taps_full[0:SHORT_K], taps_full[8:8 + CONF_K]

    (proj, h, lhs3, c3, nrm, rstd, ya, yb, mix, x1), (wfi_f, wfo_f) = _fwd_mix(
        x2d, mod, norm_mix_g, w3, w31, conv_conf_b, conf_ln_g, conf_ln_b, win_f, wa_f, wb_f, wo_f, placed_ffn, axes[4:], ts)
    h2, hid, df, dab, dx1, acc_f = _ffn(x1, tgt, mod, norm_ffn_g, gf, wfi_f, wfo_f, ts)
    f = wfo_f.shape[0]
    bn = min(512, d)
    g_ffn = [_weight_grad(h2, dab, bn, "grad_w_ffn_in")[0], _weight_grad(df, hid, bn // 2, "grad_w_ffn_out", transposed=True)[0]]
    (dproj, rhs3, acc_m), r_ffn = _bwd_mix(dx1, proj, nrm, rstd, c3, ya, yb, mix, mod, w3, w31, conf_ln_g, conf_ln_b,
                                          wa_f, wb_f, wo_f, g_ffn, axes[4:], ts)
    g3 = _weight_grad(lhs3, rhs3, bn // 2, "grad_w_sq", a_cols=d)[0]
    g_sq = [g3[:, k * d:(k + 1) * d] for k in range(3)]
    g_in, r_sq = _weight_grad(h, dproj, bn, "grad_w_in", ready=g_sq, ready_axes=axes[1:4])
    grad_x, acc_i, r_in = _bwd_in(dproj, x2d, dx1, mod, norm_mix_g, win_f, [g_in], axes[:1], min(2 * ts, s_len))
    g_in = [g_in]
    r_in_sq = list(r_in) + list(r_sq)
    grads = g_in + g_sq + g_ffn
    recv = list(r_in_sq) + list(r_ffn)
    g_big, p_all = _swap_halves([_sum_slots(r, g, ax, chip, ci, "sum_slots_" + n)
                                 for r, g, ax, n in zip(recv, grads, axes, big_names)], acc_f + acc_m + acc_i)
    p_all = p_all.reshape(N_DEV, N_SMALL_ROWS, d)

    small_w = dict(b_ada=b_ada, norm_mix_g=norm_mix_g, conv_short_w=conv_short_w, conv_conf_w=conv_conf_w,
                   conv_conf_b=conv_conf_b, conf_ln_g=conf_ln_g, conf_ln_b=conf_ln_b, norm_ffn_g=norm_ffn_g,
                   final_norm_g=final_norm_g)
    as_2d = lambda t: t.reshape((-1, t.shape[-1]))
    chip_onehot = (lax.broadcasted_iota(jnp.int32, (1, 128), 1) == chip).astype(F32)
    loss_tile, small_out = _small_update(
        p_all, chip_onehot, [as_2d(small_w[n]) for n, _, _ in _SMALL], [as_2d(moments["m_" + n]) for n, _, _ in _SMALL],
        [as_2d(moments["v_" + n]) for n, _, _ in _SMALL])
    small_g, small_d, small_m, small_v = [{n: o.reshape(small_w[n].shape) for (n, _, _), o in zip(_SMALL, outs)}
                                          for outs in small_out]

    dmod_all = p_all[:, 0:N_MOD].reshape(N_DEV, N_MOD * d)
    dmod_part = jnp.pad(lax.dynamic_slice(dmod_all, (0, chip * mq), (N_DEV, mq)), ((0, 128 - N_DEV), (0, 0)))
    c_t = jnp.pad(jnp.transpose(c_all), ((0, 0), (0, 128 - N_DEV)))
    g_ada = _w_ada_grad(c_t, dmod_part)

    out_g, out_d, out_m, out_v = dict(small_g), dict(small_d), dict(small_m), dict(small_v)
    for name, w, g in zip(["w_ada"] + big_names, [w_ada[0]] + big, [g_ada] + list(g_big)):
        go, dl, nm, nv = _adamw(w, g, moments["m_" + name][0], moments["v_" + name][0], "adamw_" + name)
        out_g[name], out_d[name], out_m[name], out_v[name] = go[None], dl[None], nm[None], nv[None]

    loss = loss_tile[0, 0]
    return loss, grad_x[None], out_g, out_d, out_m, out_v


_WEIGHTS = ["w_ada", "b_ada", "norm_mix_g", "w_in", "conv_short_w", "w_short_out", "conv_conf_w", "conv_conf_b",
            "conf_ln_g", "conf_ln_b", "w_conf_out", "w_o", "norm_ffn_g", "w_ffn_in", "w_ffn_out", "final_norm_g"]
ROW_TILE = 256


def kernel(x, c, w_ada, b_ada, norm_mix_g, w_in, conv_short_w, w_short_out, conv_conf_w, conv_conf_b, conf_ln_g, conf_ln_b, w_conf_out, w_o, norm_ffn_g, w_ffn_in, w_ffn_out, final_norm_g, loss_target, m_w_ada, m_b_ada, m_norm_mix_g, m_w_in, m_conv_short_w, m_w_short_out, m_conv_conf_w, m_conv_conf_b, m_conf_ln_g, m_conf_ln_b, m_w_conf_out, m_w_o, m_norm_ffn_g, m_w_ffn_in, m_w_ffn_out, m_final_norm_g, v_w_ada, v_b_ada, v_norm_mix_g, v_w_in, v_conv_short_w, v_w_short_out, v_conv_conf_w, v_conv_conf_b, v_conf_ln_g, v_conf_ln_b, v_w_conf_out, v_w_o, v_norm_ffn_g, v_w_ffn_in, v_w_ffn_out, v_final_norm_g):
    moments = dict(
        m_w_ada=m_w_ada, m_b_ada=m_b_ada, m_norm_mix_g=m_norm_mix_g, m_w_in=m_w_in, m_conv_short_w=m_conv_short_w,
        m_w_short_out=m_w_short_out, m_conv_conf_w=m_conv_conf_w, m_conv_conf_b=m_conv_conf_b, m_conf_ln_g=m_conf_ln_g,
        m_conf_ln_b=m_conf_ln_b, m_w_conf_out=m_w_conf_out, m_w_o=m_w_o, m_norm_ffn_g=m_norm_ffn_g, m_w_ffn_in=m_w_ffn_in,
        m_w_ffn_out=m_w_ffn_out, m_final_norm_g=m_final_norm_g,
        v_w_ada=v_w_ada, v_b_ada=v_b_ada, v_norm_mix_g=v_norm_mix_g, v_w_in=v_w_in, v_conv_short_w=v_conv_short_w,
        v_w_short_out=v_w_short_out, v_conv_conf_w=v_conv_conf_w, v_conv_conf_b=v_conv_conf_b, v_conf_ln_g=v_conf_ln_g,
        v_conf_ln_b=v_conf_ln_b, v_w_conf_out=v_w_conf_out, v_w_o=v_w_o, v_norm_ffn_g=v_norm_ffn_g, v_w_ffn_in=v_w_ffn_in,
        v_w_ffn_out=v_w_ffn_out, v_final_norm_g=v_final_norm_g)
    loss, grad_x, g, dl, nm, nv = _step(
        x, c, w_ada, b_ada, norm_mix_g, w_in, conv_short_w, w_short_out, conv_conf_w, conv_conf_b, conf_ln_g, conf_ln_b,
        w_conf_out, w_o, norm_ffn_g, w_ffn_in, w_ffn_out, final_norm_g, loss_target, moments, min(ROW_TILE, x.shape[1]))
    return (loss, grad_x, *[g[n] for n in _WEIGHTS], *[dl[n] for n in _WEIGHTS], *[nm[n] for n in _WEIGHTS],
            *[nv[n] for n in _WEIGHTS])
```

```python
import jax
import jax.numpy as jnp
from jax import lax
from jax.experimental import pallas as pl
from jax.experimental.pallas import tpu as pltpu

F32 = jnp.float32
BF16 = jnp.bfloat16
EPS = 1e-6
LN_EPS = 1e-5
SHORT_K = 3
CONF_K = 31
N_MOD = 6
N_DEV = 8
N_CHIP = 4
ADAM_LR = 0.001
ADAM_B1 = 0.9
ADAM_B2 = 0.999
ADAM_EPS = 1e-08
ADAM_WD = 0.01
ADAM_STEP = 10
MESH = pl.DeviceIdType.MESH
V7X_VMEM_LIMIT_BYTES = 60 * 1024 * 1024
LANES = 128
HALO_SHORT = 8
HALO_CONF = 32
N_SMALL_ROWS = 48
ROW_DMOD, ROW_GN1, ROW_W3, ROW_W31, ROW_CB, ROW_LNG, ROW_LNB, ROW_GN2, ROW_GF, ROW_SQERR = 0, 6, 7, 10, 41, 42, 43, 44, 45, 46
CONV_ROWS = 64
CONV_LANES = 256

ANY = pl.BlockSpec(memory_space=pl.ANY)
VMEM_FULL = pl.BlockSpec(memory_space=pltpu.VMEM)


def _pallas(body, **kw):
    return pl.pallas_call(body, **kw)


def _params(*sem):
    return pltpu.CompilerParams(dimension_semantics=sem, vmem_limit_bytes=V7X_VMEM_LIMIT_BYTES)


def _sig(z):
    return jax.nn.sigmoid(z)


def _dot(a, b):
    return jnp.dot(a, b, preferred_element_type=F32)


def _dot_nt(a, b):
    return lax.dot_general(a, b, (((1,), (1,)), ((), ())), preferred_element_type=F32)


def _dot_tn(a, b):
    return lax.dot_general(a, b, (((0,), (0,)), ((), ())), preferred_element_type=F32)


def _colsum(z):
    return jnp.sum(z, axis=0, keepdims=True)


def _place():
    return lax.axis_index("x"), lax.axis_index("y"), lax.axis_index("c")


def _small_gather_copies(x_ref, out_ref, send_sems, recv_sems, local_sem):
    m_per = x_ref.shape[0]
    x, y, c = _place()
    me, sibling = (x, y, c), (x, y, 1 - c)
    chips = [(1 - x, y), (x, 1 - y), (1 - x, 1 - y)]

    def rows(px, py, pc):
        return out_ref.at[pl.ds((4 * px + 2 * py + pc) * m_per, m_per), :]

    def copy(k, blk, to, src=None):
        return pltpu.make_async_remote_copy(
            src_ref=rows(*blk) if src is None else src, dst_ref=rows(*blk),
            send_sem=send_sems.at[k], recv_sem=recv_sems.at[k], device_id=to, device_id_type=MESH)

    mine = pltpu.make_async_copy(x_ref, rows(*me), local_sem)
    first = [copy(0, me, sibling, src=x_ref)] + [copy(1 + j, me, (*chip, c), src=x_ref) for j, chip in enumerate(chips)]
    return me, sibling, chips, c, copy, mine, first


def _small_gather_start(x_ref, out_ref, send_sems, recv_sems, local_sem):
    _, _, _, _, _, mine, first = _small_gather_copies(x_ref, out_ref, send_sems, recv_sems, local_sem)
    mine.start()
    for cp in first:
        cp.start()


def _small_gather_finish(x_ref, out_ref, send_sems, recv_sems, local_sem):
    me, sibling, chips, c, copy, mine, first = _small_gather_copies(x_ref, out_ref, send_sems, recv_sems, local_sem)
    passed = [copy(4 + j, (*chip, c), sibling) for j, chip in enumerate(chips)]
    for j, chip in enumerate(chips):
        copy(1 + j, (*chip, c), me).wait_recv()
        passed[j].start()
    copy(0, sibling, me).wait_recv()
    for j, chip in enumerate(chips):
        copy(4 + j, (*chip, 1 - c), me).wait_recv()
    for cp in first + passed:
        cp.wait_send()
    mine.wait()


def _small_gather(x_ref, out_ref, *sems):
    _small_gather_start(x_ref, out_ref, *sems)
    _small_gather_finish(x_ref, out_ref, *sems)


def _small_gather_sems():
    return [pltpu.SemaphoreType.DMA((7,)), pltpu.SemaphoreType.DMA((7,)), pltpu.SemaphoreType.DMA]


def _shard_slice(ref, full_shape, shard_axis, chip, half, quarter=None):
    m, n = full_shape
    r = (m // N_CHIP if shard_axis == 0 else m) // 2
    start = half * r + (chip * 2 * r if shard_axis == 0 else 0)
    if quarter is not None:
        r = r // 2
        start = start + quarter * r
    if shard_axis == 0:
        return ref.at[pl.ds(start, r), :]
    cw = n // N_CHIP
    return ref.at[pl.ds(start, r), pl.ds(pl.multiple_of(chip * cw, 128), cw)]


def _gather_copy(ref, region, k, a, to, sems):
    return pltpu.make_async_remote_copy(src_ref=region, dst_ref=region, send_sem=sems[0].at[a, k], recv_sem=sems[1].at[a, k],
                                        device_id=to, device_id_type=MESH)


def _gather_send(refs, shapes, axes, *sems):
    x, y, c = _place()
    for a in range(len(refs)):
        mine = _shard_slice(refs[a], shapes[a], axes[a], 2 * x + y, c)
        _gather_copy(refs[a], mine, 0, a, (1 - x, y, c), sems).start()
        _gather_copy(refs[a], mine, 1, a, (x, 1 - y, c), sems).start()


def _gather_forward(refs, shapes, axes, *sems):
    x, y, c = _place()
    me, sibling, x_nbr, y_nbr = (x, y, c), (x, y, 1 - c), (1 - x, y, c), (x, 1 - y, c)
    chip_x, chip_y, chip_d = 2 * (1 - x) + y, 2 * x + (1 - y), 2 * (1 - x) + (1 - y)
    for a in range(len(refs)):
        region = lambda chip, quarter=None, a=a: _shard_slice(refs[a], shapes[a], axes[a], chip, c, quarter)
        _gather_copy(refs[a], region(chip_x), 0, a, me, sems).wait_recv()
        _gather_copy(refs[a], region(chip_x, 1), 3, a, y_nbr, sems).start()
        _gather_copy(refs[a], region(chip_x), 4, a, sibling, sems).start()
        _gather_copy(refs[a], region(chip_y), 1, a, me, sems).wait_recv()
        _gather_copy(refs[a], region(chip_y, 0), 2, a, x_nbr, sems).start()
        _gather_copy(refs[a], region(chip_y), 5, a, sibling, sems).start()


def _gather_forward_diagonal(refs, shapes, axes, *sems):
    x, y, c = _place()
    me, sibling, chip_d = (x, y, c), (x, y, 1 - c), 2 * (1 - x) + (1 - y)
    for a in range(len(refs)):
        region = lambda chip, quarter=None, a=a: _shard_slice(refs[a], shapes[a], axes[a], chip, c, quarter)
        _gather_copy(refs[a], region(chip_d, 0), 2, a, me, sems).wait_recv()
        _gather_copy(refs[a], region(chip_d, 1), 3, a, me, sems).wait_recv()
        _gather_copy(refs[a], region(chip_d), 6, a, sibling, sems).start()


def _gather_finish(refs, shapes, axes, *sems):
    x, y, c = _place()
    me = (x, y, c)
    chip_me, chip_x, chip_y, chip_d = 2 * x + y, 2 * (1 - x) + y, 2 * x + (1 - y), 2 * (1 - x) + (1 - y)
    for a in range(len(refs)):
        region = lambda chip, half, quarter=None, a=a: _shard_slice(refs[a], shapes[a], axes[a], chip, half, quarter)
        for k, chip in ((4, chip_x), (5, chip_y), (6, chip_d)):
            _gather_copy(refs[a], region(chip, 1 - c), k, a, me, sems).wait_recv()
            _gather_copy(refs[a], region(chip, c), k, a, me, sems).wait_send()
        _gather_copy(refs[a], region(chip_me, c), 0, a, me, sems).wait_send()
        _gather_copy(refs[a], region(chip_me, c), 1, a, me, sems).wait_send()
        _gather_copy(refs[a], region(chip_y, c, 0), 2, a, me, sems).wait_send()
        _gather_copy(refs[a], region(chip_x, c, 1), 3, a, me, sems).wait_send()


def _gather_sems(n_w):
    return [pltpu.SemaphoreType.DMA((n_w, 7)), pltpu.SemaphoreType.DMA((n_w, 7))]


def _gather_weights(shards, axes, n_gather, c_taps, w_ada, b_part, name):
    n_w = len(shards)
    shapes = [(w.shape[0] * N_CHIP, w.shape[1]) if ax == 0 else (w.shape[0], w.shape[1] * N_CHIP) for w, ax in zip(shards, axes)]
    r_ct = c_taps.shape[0]
    mq = w_ada.shape[1]

    def body(*refs):
        ins = refs[:n_w]
        ct_ref, wada_ref, b_ref = refs[n_w:n_w + 3]
        outs = refs[n_w + 3:2 * n_w + 3]
        ct_all_ref, mod_all_ref = refs[2 * n_w + 3:2 * n_w + 5]
        scratch = refs[2 * n_w + 5:]
        stage32, stage16, mod_part = scratch[:n_w], scratch[n_w:2 * n_w], scratch[2 * n_w]
        load_sems, store_sems = scratch[2 * n_w + 1:2 * n_w + 3]
        sems, ct_sems, mod_sems = scratch[2 * n_w + 3:2 * n_w + 5], scratch[2 * n_w + 5:2 * n_w + 8], scratch[2 * n_w + 8:]
        x, y, c = _place()
        chip = 2 * x + y
        loads = [pltpu.make_async_copy(ins[a], stage32[a], load_sems.at[a]) for a in range(n_w)]
        for cp in loads:
            cp.start()
        _small_gather_start(ct_ref, ct_all_ref, *ct_sems)
        stores = []
        for a in range(n_w):
            loads[a].wait()
            stage16[a][...] = stage32[a][...].astype(BF16)
            r, cw = shards[a].shape
            own = outs[a].at[pl.ds(chip * r, r), :] if axes[a] == 0 else outs[a].at[:, pl.ds(pl.multiple_of(chip * cw, 128), cw)]
            stores.append(pltpu.make_async_copy(stage16[a], own, store_sems.at[a]))
            stores[-1].start()
            if a < n_gather:
                half = stage16[a].at[pl.ds(c * (r // 2), r // 2), :]
                there = _shard_slice(outs[a], shapes[a], axes[a], chip, c)
                for k, to in ((0, (1 - x, y, c)), (1, (x, 1 - y, c))):
                    pltpu.make_async_remote_copy(src_ref=half, dst_ref=there, send_sem=sems[0].at[a, k],
                                                 recv_sem=sems[1].at[a, k], device_id=to, device_id_type=MESH).start()
        _small_gather_finish(ct_ref, ct_all_ref, *ct_sems)
        cv = jnp.concatenate([ct_all_ref[r_ct * dv:r_ct * dv + 1, :] for dv in range(N_DEV)], axis=0)
        mod_part[...] = _dot((cv * _sig(cv)).astype(BF16), wada_ref[...].astype(BF16)) + b_ref[...]
        _small_gather(mod_part, mod_all_ref, *mod_sems)
        _gather_forward(outs[:n_gather], shapes, axes, *sems)
        _gather_forward_diagonal(outs[:n_gather], shapes, axes, *sems)
        _gather_finish(outs[:n_gather], shapes, axes, *sems)
        for cp in stores:
            cp.wait()

    res = _pallas(
        body, name=name,
        out_shape=tuple(jax.ShapeDtypeStruct(sh, BF16) for sh in shapes)
        + (jax.ShapeDtypeStruct((N_DEV * r_ct, c_taps.shape[1]), F32), jax.ShapeDtypeStruct((N_DEV * N_DEV, mq), F32)),
        in_specs=[ANY] * n_w + [VMEM_FULL] * 3, out_specs=tuple([ANY] * n_w + [VMEM_FULL] * 2),
        scratch_shapes=[pltpu.VMEM(w.shape, F32) for w in shards] + [pltpu.VMEM(w.shape, BF16) for w in shards]
        + [pltpu.VMEM((N_DEV, mq), F32), pltpu.SemaphoreType.DMA((n_w,)), pltpu.SemaphoreType.DMA((n_w,))]
        + _gather_sems(n_gather) + _small_gather_sems() + _small_gather_sems(),
        compiler_params=_params(),
    )(*shards, c_taps, w_ada, b_part)
    return res[:n_w], res[n_w], res[n_w + 1]


def _piece_shapes(grads, axes):
    return [((g.shape[0] // N_CHIP // 2, g.shape[1]) if ax == 0 else (g.shape[0] // 2, g.shape[1] // N_CHIP))
            for g, ax in zip(grads, axes)]


def _exchange_send(ins, outs, shapes, axes, send_sems, recv_sems):
    x, y, c = _place()
    me = 4 * x + 2 * y + c
    for a in range(len(ins)):
        for t in range(N_DEV):
            tx, ty, tc = t // 4, (t // 2) % 2, t % 2
            src = _shard_slice(ins[a], shapes[a], axes[a], 2 * tx + ty, tc)

            @pl.when(me != t)
            def _(src=src, a=a, t=t, to=(tx, ty, tc)):
                slot = (me - t + N_DEV) % N_DEV - 1
                pltpu.make_async_remote_copy(src_ref=src, dst_ref=outs[a].at[slot], send_sem=send_sems.at[a, t],
                                             recv_sem=recv_sems.at[a, slot], device_id=to, device_id_type=MESH).start()


def _exchange_finish(outs, send_sems, recv_sems):
    x, y, c = _place()
    me = 4 * x + 2 * y + c
    for a in range(len(outs)):
        for t in range(N_DEV):
            @pl.when(me != t)
            def _(a=a, t=t):
                slot = outs[a].at[0]
                pltpu.make_async_remote_copy(src_ref=slot, dst_ref=slot, send_sem=send_sems.at[a, t], recv_sem=recv_sems.at[a, 0],
                                             device_id=(x, y, c), device_id_type=MESH).wait_send()
        for k in range(N_DEV - 1):
            slot = outs[a].at[k]
            pltpu.make_async_remote_copy(src_ref=slot, dst_ref=slot, send_sem=send_sems.at[a, 0], recv_sem=recv_sems.at[a, k],
                                         device_id=(x, y, c), device_id_type=MESH).wait_recv()


def _exchange_sems(n_w):
    return [pltpu.SemaphoreType.DMA((n_w, N_DEV)), pltpu.SemaphoreType.DMA((n_w, N_DEV - 1))]


def _exchange_out_shapes(grads, axes):
    return tuple(jax.ShapeDtypeStruct((N_DEV - 1,) + p, BF16) for p in _piece_shapes(grads, axes))


def _swap_halves(fulls, block):
    n_w = len(fulls)

    def body(*refs):
        block_ref = refs[n_w]
        outs, gathered_ref = refs[n_w + 1:2 * n_w + 1], refs[2 * n_w + 1]
        send_sems, recv_sems = refs[2 * n_w + 2:2 * n_w + 4]
        x, y, c = _place()
        sibling = (x, y, 1 - c)
        waits = []
        for a in range(n_w):
            r = fulls[a].shape[0] // 2
            mine = outs[a].at[pl.ds(c * r, r), :]
            theirs = outs[a].at[pl.ds((1 - c) * r, r), :]
            rc = pltpu.make_async_remote_copy(src_ref=mine, dst_ref=mine, send_sem=send_sems.at[a], recv_sem=recv_sems.at[a],
                                              device_id=sibling, device_id_type=MESH)
            rc.start()
            got = pltpu.make_async_remote_copy(src_ref=theirs, dst_ref=theirs, send_sem=send_sems.at[a],
                                               recv_sem=recv_sems.at[a], device_id=sibling, device_id_type=MESH)
            waits += [rc.wait_send, got.wait_recv]
        _small_gather(block_ref, gathered_ref, *refs[2 * n_w + 4:])
        for w in waits:
            w()

    res = _pallas(
        body, name="swap_halves",
        out_shape=tuple(jax.ShapeDtypeStruct(f.shape, f.dtype) for f in fulls)
        + (jax.ShapeDtypeStruct((N_DEV * block.shape[0], block.shape[1]), block.dtype),),
        in_specs=[ANY] * n_w + [VMEM_FULL], out_specs=tuple([ANY] * n_w + [VMEM_FULL]),
        input_output_aliases={a: a for a in range(n_w)},
        scratch_shapes=[pltpu.SemaphoreType.DMA((n_w,)), pltpu.SemaphoreType.DMA((n_w,))] + _small_gather_sems(),
    )(*fulls, block)
    return res[:n_w], res[n_w]


def _load_weights(pairs, sems):
    @pl.when(pl.program_id(0) == 0)
    def _():
        cps = [pltpu.make_async_copy(s, d, sems.at[k]) for k, (s, d) in enumerate(pairs)]
        for cp in cps:
            cp.start()
        for cp in cps:
            cp.wait()


def _conv_chunks(ext, e_buf, ts, d, chunk_fn, unroll, between=()):
    rows, dh = e_buf.shape[1], e_buf.shape[2]
    for hf in range(d // dh):
        for r in range(1, 8):
            e_buf[r - 1, :, :] = ext[r:r + rows, hf * dh:(hf + 1) * dh]
        for lq in range(dh // CONV_LANES):
            lo, el = hf * dh + lq * CONV_LANES, lq * CONV_LANES

            def body(ci, carry, lo=lo, el=el):
                r0 = pl.multiple_of(ci * CONV_ROWS, CONV_ROWS)

                def tap(off):
                    q, r = divmod(off, 8)
                    if r == 0:
                        return ext[pl.ds(r0 + 8 * q, CONV_ROWS), lo:lo + CONV_LANES]
                    return e_buf[r - 1, pl.ds(r0 + 8 * q, CONV_ROWS), el:el + CONV_LANES]

                chunk_fn(tap, r0, lo)
                return carry

            lax.fori_loop(0, ts // CONV_ROWS, body, 0, unroll=unroll)
            if between:
                between[0]()
                between = between[1:]
    for fn in between:
        fn()


def _fwd_mix(x, mod, gn1, w3, w31, cbias, lng, lnb, w_in, w_a, w_b, w_o, later, later_axes, ts):
    s_len, d = x.shape
    n_t = s_len // ts
    n_l = len(later)
    later_shapes = [w.shape for w in later]

    def body(*refs):
        (x_ref, mod_ref, gn1_ref, w3_ref, w31_ref, cb_ref, lng_ref, lnb_ref, win_hbm, wa_hbm, wb_hbm, wo_hbm) = refs[:12]
        (proj_ref, h_ref, lhs3_ref, c3_ref, n_ref, rstd_ref, ya_ref, yb_ref, mix_ref, x1_ref) = refs[12 + n_l:22 + n_l]
        later_refs = refs[22 + n_l:22 + 2 * n_l]
        win_v, wa_v, wb_v, wo_v, cv_ext, u0_ext, e_buf, u1_buf, sems, g_send, g_recv = refs[22 + 2 * n_l:]
        step = pl.program_id(0)
        pl.when(step == 0)(lambda: _gather_send(later_refs, later_shapes, later_axes, g_send, g_recv))
        pl.when(step == n_t // 2)(lambda: _gather_forward(later_refs, later_shapes, later_axes, g_send, g_recv))
        pl.when(step == (3 * n_t) // 4)(lambda: _gather_forward_diagonal(later_refs, later_shapes, later_axes, g_send, g_recv))
        pl.when(step == n_t - 1)(lambda: _gather_finish(later_refs, later_shapes, later_axes, g_send, g_recv))
        _load_weights([(win_hbm, win_v), (wa_hbm, wa_v), (wb_hbm, wb_v), (wo_hbm, wo_v)], sems)

        @pl.when(pl.program_id(0) == 0)
        def _():
            cv_ext[0:HALO_SHORT, :] = jnp.zeros((HALO_SHORT, d), F32)
            u0_ext[0:HALO_CONF, :] = jnp.zeros((HALO_CONF, d), F32)

        xv = x_ref[...]
        sh1, sc1, g1 = mod_ref[:, 0:d], mod_ref[:, d:2 * d], mod_ref[:, 2 * d:3 * d]
        r = lax.rsqrt(jnp.mean(xv * xv, axis=-1, keepdims=True) + EPS)
        hb = (((xv * r) * gn1_ref[...]) * (1.0 + sc1) + sh1).astype(BF16)
        h_ref[...] = hb

        def pj(k):
            p = _dot(hb, win_v[:, k * d:(k + 1) * d])
            proj_ref[:, k * d:(k + 1) * d] = p.astype(BF16)
            return p

        v_c = pj(3)
        u0_ext[HALO_CONF:HALO_CONF + ts, :] = v_c * _sig(pj(4))

        def conv_chunk(tap, r0, lo):
            acc = jnp.zeros((CONV_ROWS, CONV_LANES), F32)
            for k in range(CONF_K):
                acc = acc + w31_ref[k:k + 1, lo:lo + CONV_LANES] * tap(HALO_CONF - (CONF_K - 1) + k)
            u1_buf[pl.ds(r0, CONV_ROWS), lo:lo + CONV_LANES] = acc

        pjs = {}
        _conv_chunks(u0_ext, e_buf, ts, d, conv_chunk, unroll=True,
                     between=[lambda k=k: pjs.__setitem__(k, pj(k)) for k in (1, 2, 0, 5, 6)])
        u0_ext[0:HALO_CONF, :] = u0_ext[ts:ts + HALO_CONF, :]

        cv_ext[HALO_SHORT:HALO_SHORT + ts, :] = pjs[1] * pjs[2]
        conv3 = jnp.zeros((ts, d), F32)
        for k in range(SHORT_K):
            off = HALO_SHORT - (SHORT_K - 1) + k
            conv3 = conv3 + w3_ref[k:k + 1, :] * cv_ext[off:off + ts, :]
        cv_ext[0:HALO_SHORT, :] = cv_ext[ts:ts + HALO_SHORT, :]
        c3_ref[...] = conv3.astype(BF16)
        yap = (pjs[0] * conv3).astype(BF16)
        lhs3_ref[:, 0:d] = yap
        y_a = _dot(yap, wa_v[...])
        ya_ref[...] = y_a.astype(BF16)

        u1 = u1_buf[...] + cb_ref[...]
        mu = jnp.mean(u1, axis=-1, keepdims=True)
        var = jnp.mean(jnp.square(u1 - mu), axis=-1, keepdims=True)
        rstd = lax.rsqrt(var + LN_EPS)
        nrm = (u1 - mu) * rstd
        n_ref[...] = nrm.astype(BF16)
        rstd_ref[...] = jnp.broadcast_to(rstd, (ts, LANES))
        u2 = nrm * lng_ref[...] + lnb_ref[...]
        u3 = (u2 * _sig(u2)).astype(BF16)
        lhs3_ref[:, d:2 * d] = u3
        y_b = _dot(u3, wb_v[...])
        yb_ref[...] = y_b.astype(BF16)

        merged = (_sig(pjs[5]) * y_a + _sig(pjs[6]) * y_b).astype(BF16)
        lhs3_ref[:, 2 * d:3 * d] = merged
        mix = _dot(merged, wo_v[...])
        mix_ref[...] = mix
        x1_ref[...] = xv + g1 * mix

    row = lambda w, dt=None: pl.BlockSpec((ts, w), lambda i: (i, 0))
    full = lambda a: pl.BlockSpec(a.shape, lambda i: (0, 0))
    sd = lambda w, dt: jax.ShapeDtypeStruct((s_len, w), dt)
    outs = [sd(7 * d, BF16), sd(d, BF16), sd(3 * d, BF16), sd(d, BF16), sd(d, BF16), sd(LANES, F32), sd(d, BF16), sd(d, BF16),
            sd(d, F32), sd(d, F32)]
    res = _pallas(
        body, name="fwd_mix", grid=(n_t,),
        out_shape=tuple(outs) + tuple(jax.ShapeDtypeStruct(w.shape, BF16) for w in later),
        in_specs=[row(d), full(mod), full(gn1), full(w3), full(w31), full(cbias), full(lng), full(lnb), ANY, ANY, ANY, ANY]
        + [ANY] * n_l,
        out_specs=tuple(row(o.shape[1]) for o in outs) + tuple([ANY] * n_l),
        input_output_aliases={12 + k: len(outs) + k for k in range(n_l)},
        scratch_shapes=[pltpu.VMEM(w_in.shape, BF16), pltpu.VMEM(w_a.shape, BF16), pltpu.VMEM(w_b.shape, BF16),
                        pltpu.VMEM(w_o.shape, BF16), pltpu.VMEM((ts + HALO_SHORT, d), F32),
                        pltpu.VMEM((ts + HALO_CONF, d), F32), pltpu.VMEM((7, ts + HALO_CONF - 8, d // 2), F32),
                        pltpu.VMEM((ts, d), F32), pltpu.SemaphoreType.DMA((4,))] + _gather_sems(n_l),
        compiler_params=_params("arbitrary"),
    )(x, mod, gn1, w3, w31, cbias, lng, lnb, w_in, w_a, w_b, w_o, *later)
    return res[:len(outs)], res[len(outs):]


def _ffn(x1, target, mod, gn2, gf, w_fi, w_fo, ts):
    s_len, d = x1.shape
    f = w_fo.shape[0]
    n_t = s_len // ts

    def body(x1_ref, t_ref, mod_ref, gn2_ref, gf_ref, wfi_hbm, wfo_hbm,
             h2_ref, hid_ref, df_ref, dab_ref, dx1_ref, acc_ref, wfi_v, wfo_v, sems):
        _load_weights([(wfi_hbm, wfi_v), (wfo_hbm, wfo_v)], sems)

        @pl.when(pl.program_id(0) == 0)
        def _():
            acc_ref[...] = jnp.zeros(acc_ref.shape, F32)

        x1v = x1_ref[...]
        sh2, sc2, g2 = mod_ref[:, 3 * d:4 * d], mod_ref[:, 4 * d:5 * d], mod_ref[:, 5 * d:6 * d]
        gn2v, gfv = gn2_ref[...], gf_ref[...]
        r2 = lax.rsqrt(jnp.mean(x1v * x1v, axis=-1, keepdims=True) + EPS)
        xn2 = x1v * r2
        h2 = ((xn2 * gn2v) * (1.0 + sc2) + sh2).astype(BF16)
        h2_ref[...] = h2
        a = _dot(h2, wfi_v[:, 0:f])
        bg = _dot(h2, wfi_v[:, f:2 * f])
        sa = _sig(a)
        silu_a = a * sa
        hid = (silu_a * bg).astype(BF16)
        hid_ref[...] = hid
        ffn = _dot(hid, wfo_v[...])
        x2 = x1v + g2 * ffn
        r3 = lax.rsqrt(jnp.mean(x2 * x2, axis=-1, keepdims=True) + EPS)
        xn3 = x2 * r3
        diff = xn3 * gfv - t_ref[...]
        dy = diff * (1.0 / d)
        dxn3 = dy * gfv
        dx2 = r3 * (dxn3 - xn3 * jnp.mean(dxn3 * xn3, axis=-1, keepdims=True))
        acc_ref[ROW_GF:ROW_GF + 1, :] += _colsum(dy * xn3)
        acc_ref[ROW_DMOD + 5:ROW_DMOD + 5 + 1, :] += _colsum(dx2 * ffn)
        acc_ref[ROW_SQERR:ROW_SQERR + 1, :] += _colsum(diff * diff)

        df = (dx2 * g2).astype(BF16)
        df_ref[...] = df
        dhid = _dot_nt(df, wfo_v[...])
        da = dhid * bg * (sa * (1.0 + a * (1.0 - sa)))
        dbg = dhid * silu_a
        dab_ref[:, 0:f] = da.astype(BF16)
        dab_ref[:, f:2 * f] = dbg.astype(BF16)
        dh2 = _dot_nt(da.astype(BF16), wfi_v[:, 0:f]) + _dot_nt(dbg.astype(BF16), wfi_v[:, f:2 * f])
        acc_ref[ROW_DMOD + 3:ROW_DMOD + 3 + 1, :] += _colsum(dh2)
        acc_ref[ROW_DMOD + 4:ROW_DMOD + 4 + 1, :] += _colsum(dh2 * (xn2 * gn2v))
        acc_ref[ROW_GN2:ROW_GN2 + 1, :] += _colsum(dh2 * xn2 * (1.0 + sc2))
        dxn2 = dh2 * gn2v * (1.0 + sc2)
        dx1_ref[...] = dx2 + r2 * (dxn2 - xn2 * jnp.mean(dxn2 * xn2, axis=-1, keepdims=True))

    row = lambda w: pl.BlockSpec((ts, w), lambda i: (i, 0))
    full = lambda a: pl.BlockSpec(a.shape, lambda i: (0, 0))
    sd = lambda w, dt: jax.ShapeDtypeStruct((s_len, w), dt)
    outs = [sd(d, BF16), sd(f, BF16), sd(d, BF16), sd(2 * f, BF16), sd(d, F32), jax.ShapeDtypeStruct((N_SMALL_ROWS, d), F32)]
    return _pallas(
        body, name="ffn", grid=(n_t,), out_shape=tuple(outs),
        in_specs=[row(d), row(d), full(mod), full(gn2), full(gf), ANY, ANY],
        out_specs=tuple([row(o.shape[1]) for o in outs[:-1]] + [pl.BlockSpec((N_SMALL_ROWS, d), lambda i: (0, 0))]),
        scratch_shapes=[pltpu.VMEM(w_fi.shape, BF16), pltpu.VMEM(w_fo.shape, BF16), pltpu.SemaphoreType.DMA((2,))],
        compiler_params=_params("arbitrary"),
    )(x1, target, mod, gn2, gf, w_fi, w_fo)


def _bwd_mix(dx1, proj, nrm, rstd, c3, ya, yb, mix, mod, w3, w31, lng, lnb, w_a, w_b, w_o, ready, ready_axes, ts):
    s_len, d = dx1.shape
    n_t = s_len // ts
    n_r = len(ready)
    ready_shapes = [g.shape for g in ready]

    def body(*refs):
        (dx1_ref, proj_ref, n_ref, rstd_ref, c3_ref, ya_ref, yb_ref, mix_ref, mod_ref, w3_ref, w31_ref, lng_ref,
         lnb_ref, wa_hbm, wb_hbm, wo_hbm) = refs[:16]
        ready_refs = refs[16:16 + n_r]
        dproj_ref, rhs3_ref, acc_ref = refs[16 + n_r:19 + n_r]
        recv_refs = refs[19 + n_r:19 + 2 * n_r]
        wa_v, wb_v, wo_v, dc3_ext, du1_ext, e_buf, u0_buf, du0_buf, dw8, sems, e_send, e_recv = refs[19 + 2 * n_r:]
        step = pl.program_id(0)
        pl.when(step == 0)(lambda: _exchange_send(ready_refs, recv_refs, ready_shapes, ready_axes, e_send, e_recv))
        pl.when(step == n_t - 1)(lambda: _exchange_finish(recv_refs, e_send, e_recv))
        _load_weights([(wa_hbm, wa_v), (wb_hbm, wb_v), (wo_hbm, wo_v)], sems)

        @pl.when(pl.program_id(0) == 0)
        def _():
            acc_ref[...] = jnp.zeros(acc_ref.shape, F32)
            dc3_ext[ts:ts + HALO_SHORT, :] = jnp.zeros((HALO_SHORT, d), F32)
            du1_ext[ts:ts + HALO_CONF, :] = jnp.zeros((HALO_CONF, d), F32)
            dw8[...] = jnp.zeros(dw8.shape, F32)

        pj = lambda k: proj_ref[:, k * d:(k + 1) * d].astype(F32)
        g1 = mod_ref[:, 2 * d:3 * d]
        dx1v = dx1_ref[...]
        acc_ref[ROW_DMOD + 2:ROW_DMOD + 2 + 1, :] += _colsum(dx1v * mix_ref[...])
        dmix = (dx1v * g1).astype(BF16)
        rhs3_ref[:, 2 * d:3 * d] = dmix
        dmerged = _dot_nt(dmix, wo_v[...])
        g_a, g_b = _sig(pj(5)), _sig(pj(6))
        y_a, y_b = ya_ref[...].astype(F32), yb_ref[...].astype(F32)
        dproj_ref[:, 5 * d:6 * d] = (dmerged * y_a * g_a * (1.0 - g_a)).astype(BF16)
        dproj_ref[:, 6 * d:7 * d] = (dmerged * y_b * g_b * (1.0 - g_b)).astype(BF16)
        dya = (dmerged * g_a).astype(BF16)
        dyb = (dmerged * g_b).astype(BF16)
        rhs3_ref[:, 0:d] = dya
        rhs3_ref[:, d:2 * d] = dyb

        dyap = _dot_nt(dya, wa_v[...])
        dproj_ref[:, 0:d] = (dyap * c3_ref[...].astype(F32)).astype(BF16)
        dc3_ext[0:ts, :] = dyap * pj(0)
        c_s, v_s = pj(1), pj(2)
        cv = c_s * v_s
        dcv = jnp.zeros((ts, d), F32)
        for j in range(SHORT_K):
            k = SHORT_K - 1 - j
            sj = dc3_ext[j:j + ts, :]
            dcv = dcv + w3_ref[k:k + 1, :] * sj
            acc_ref[ROW_W3 + k:ROW_W3 + k + 1, :] += _colsum(cv * sj)
        dc3_ext[ts:ts + HALO_SHORT, :] = dc3_ext[0:HALO_SHORT, :]
        dproj_ref[:, d:2 * d] = (dcv * v_s).astype(BF16)
        dproj_ref[:, 2 * d:3 * d] = (dcv * c_s).astype(BF16)

        du3 = _dot_nt(dyb, wb_v[...])
        nv = n_ref[...].astype(F32)
        lg = lng_ref[...]
        u2 = nv * lg + lnb_ref[...]
        s2 = _sig(u2)
        du2 = du3 * (s2 * (1.0 + u2 * (1.0 - s2)))
        acc_ref[ROW_LNG:ROW_LNG + 1, :] += _colsum(du2 * nv)
        acc_ref[ROW_LNB:ROW_LNB + 1, :] += _colsum(du2)
        dn = du2 * lg
        du1 = rstd_ref[:, 0:1] * (dn - jnp.mean(dn, axis=-1, keepdims=True) - nv * jnp.mean(dn * nv, axis=-1, keepdims=True))
        acc_ref[ROW_CB:ROW_CB + 1, :] += _colsum(du1)
        du1_ext[0:ts, :] = du1
        v_c = pj(3)
        sg = _sig(pj(4))
        u0_buf[...] = v_c * sg

        def conv_chunk(tap, r0, lo):
            lanes = slice(lo, lo + CONV_LANES)
            u0c = u0_buf[pl.ds(r0, CONV_ROWS), lanes]
            acc = jnp.zeros((CONV_ROWS, CONV_LANES), F32)
            for j in range(CONF_K):
                k = CONF_K - 1 - j
                sj = tap(j)
                acc = acc + w31_ref[k:k + 1, lanes] * sj
                prod = u0c * sj
                part = prod[0:8]
                for i in range(1, CONV_ROWS // 8):
                    part = part + prod[8 * i:8 * i + 8]
                dw8[k, :, lanes] += part
            du0_buf[pl.ds(r0, CONV_ROWS), lanes] = acc

        _conv_chunks(du1_ext, e_buf, ts, d, conv_chunk, unroll=False)
        du0 = du0_buf[...]
        du1_ext[ts:ts + HALO_CONF, :] = du1_ext[0:HALO_CONF, :]

        @pl.when(step == n_t - 1)
        def _():
            for k in range(CONF_K):
                acc_ref[ROW_W31 + k:ROW_W31 + k + 1, :] = _colsum(dw8[k])
        dproj_ref[:, 3 * d:4 * d] = (du0 * sg).astype(BF16)
        dproj_ref[:, 4 * d:5 * d] = (du0 * v_c * sg * (1.0 - sg)).astype(BF16)

    rev = lambda w: pl.BlockSpec((ts, w), lambda i: (n_t - 1 - i, 0))
    full = lambda a: pl.BlockSpec(a.shape, lambda i: (0, 0))
    sd = lambda w, dt: jax.ShapeDtypeStruct((s_len, w), dt)
    outs = [sd(7 * d, BF16), sd(3 * d, BF16), jax.ShapeDtypeStruct((N_SMALL_ROWS, d), F32)]
    res = _pallas(
        body, name="bwd_mix", grid=(n_t,), out_shape=tuple(outs) + _exchange_out_shapes(ready, ready_axes),
        in_specs=[rev(d), rev(7 * d), rev(d), rev(LANES), rev(d), rev(d), rev(d), rev(d), full(mod), full(w3), full(w31),
                  full(lng), full(lnb), ANY, ANY, ANY] + [ANY] * n_r,
        out_specs=tuple([rev(o.shape[1]) for o in outs[:-1]] + [pl.BlockSpec((N_SMALL_ROWS, d), lambda i: (0, 0))] + [ANY] * n_r),
        scratch_shapes=[pltpu.VMEM(w_a.shape, BF16), pltpu.VMEM(w_b.shape, BF16), pltpu.VMEM(w_o.shape, BF16),
                        pltpu.VMEM((ts + HALO_SHORT, d), F32), pltpu.VMEM((ts + HALO_CONF, d), F32),
                        pltpu.VMEM((7, ts + HALO_CONF - 8, d // 2), F32), pltpu.VMEM((ts, d), F32), pltpu.VMEM((ts, d), F32),
                        pltpu.VMEM((CONF_K, 8, d), F32), pltpu.SemaphoreType.DMA((3,))] + _exchange_sems(n_r),
        compiler_params=_params("arbitrary"),
    )(dx1, proj, nrm, rstd, c3, ya, yb, mix, mod, w3, w31, lng, lnb, w_a, w_b, w_o, *ready)
    return res[:len(outs)], res[len(outs):]


def _bwd_in(dproj, x, dx1, mod, gn1, w_in, ready, ready_axes, ts):
    s_len, d = x.shape
    n_t = s_len // ts
    n_r = len(ready)
    ready_shapes = [g.shape for g in ready]

    def body(*refs):
        dproj_ref, x_ref, dx1_ref, mod_ref, gn1_ref, win_hbm = refs[:6]
        ready_refs = refs[6:6 + n_r]
        gx_ref, acc_ref = refs[6 + n_r:8 + n_r]
        recv_refs = refs[8 + n_r:8 + 2 * n_r]
        win_v, sems, e_send, e_recv = refs[8 + 2 * n_r:]
        step = pl.program_id(0)
        pl.when(step == 0)(lambda: _exchange_send(ready_refs, recv_refs, ready_shapes, ready_axes, e_send, e_recv))
        pl.when(step == n_t - 1)(lambda: _exchange_finish(recv_refs, e_send, e_recv))
        _load_weights([(win_hbm, win_v)], sems)

        @pl.when(pl.program_id(0) == 0)
        def _():
            acc_ref[...] = jnp.zeros(acc_ref.shape, F32)

        xv = x_ref[...]
        sc1 = mod_ref[:, d:2 * d]
        gn1v = gn1_ref[...]
        r = lax.rsqrt(jnp.mean(xv * xv, axis=-1, keepdims=True) + EPS)
        xn = xv * r
        dh = _dot_nt(dproj_ref[...], win_v[...])
        acc_ref[ROW_DMOD:ROW_DMOD + 1, :] += _colsum(dh)
        acc_ref[ROW_DMOD + 1:ROW_DMOD + 1 + 1, :] += _colsum(dh * (xn * gn1v))
        acc_ref[ROW_GN1:ROW_GN1 + 1, :] += _colsum(dh * xn * (1.0 + sc1))
        dxn = dh * gn1v * (1.0 + sc1)
        gx_ref[...] = dx1_ref[...] + r * (dxn - xn * jnp.mean(dxn * xn, axis=-1, keepdims=True))

    row = lambda w: pl.BlockSpec((ts, w), lambda i: (i, 0))
    full = lambda a: pl.BlockSpec(a.shape, lambda i: (0, 0))
    res = _pallas(
        body, name="bwd_in", grid=(n_t,),
        out_shape=(jax.ShapeDtypeStruct((s_len, d), F32), jax.ShapeDtypeStruct((N_SMALL_ROWS, d), F32))
        + _exchange_out_shapes(ready, ready_axes),
        in_specs=[row(7 * d), row(d), row(d), full(mod), full(gn1), ANY] + [ANY] * n_r,
        out_specs=(row(d), pl.BlockSpec((N_SMALL_ROWS, d), lambda i: (0, 0))) + tuple([ANY] * n_r),
        scratch_shapes=[pltpu.VMEM(w_in.shape, BF16), pltpu.SemaphoreType.DMA((1,))] + _exchange_sems(n_r),
        compiler_params=_params("arbitrary"),
    )(dproj, x, dx1, mod, gn1, w_in, *ready)
    return res[0], res[1], res[2:]


def _weight_grad(a, b, bn, name, transposed=False, ready=(), ready_axes=(), a_cols=None):
    s_len, m = a.shape[0], (a_cols or a.shape[1])
    n = b.shape[1]
    n_j = n // bn
    n_r = len(ready)
    ready_shapes = [g.shape for g in ready]

    def body(*refs):
        a_hbm, b_ref = refs[:2]
        ready_refs = refs[2:2 + n_r]
        o_ref = refs[2 + n_r]
        recv_refs = refs[3 + n_r:3 + 2 * n_r]
        a_v, sem = refs[3 + 2 * n_r:5 + 2 * n_r]
        step = pl.program_id(0)
        if n_r:
            e_send, e_recv = refs[5 + 2 * n_r:]
            pl.when(step == 0)(lambda: _exchange_send(ready_refs, recv_refs, ready_shapes, ready_axes, e_send, e_recv))
            pl.when(step == n_j - 1)(lambda: _exchange_finish(recv_refs, e_send, e_recv))
        if a_cols is None:
            _load_weights([(a_hbm, a_v)], sem)
            prod = _dot_tn(a_v[...], b_ref[...])
        else:
            prod = _dot_tn(a_hbm[...], b_ref[...])
        o_ref[...] = (prod.T if transposed else prod).astype(BF16)

    out = jax.ShapeDtypeStruct((n, m) if transposed else (m, n), BF16)
    out_spec = pl.BlockSpec((bn, m), lambda j: (j, 0)) if transposed else pl.BlockSpec((m, bn), lambda j: (0, j))
    res = _pallas(
        body, name=name, grid=(n_j,), out_shape=(out,) + _exchange_out_shapes(ready, ready_axes),
        in_specs=[ANY if a_cols is None else pl.BlockSpec((s_len, a_cols), lambda j: (0, (j * bn) // a_cols)),
                  pl.BlockSpec((s_len, bn), lambda j: (0, j))] + [ANY] * n_r,
        out_specs=(out_spec,) + tuple([ANY] * n_r),
        scratch_shapes=[pltpu.VMEM((8, 128) if a_cols else a.shape, BF16), pltpu.SemaphoreType.DMA((1,))]
        + (_exchange_sems(n_r) if n_r else []),
        compiler_params=_params("arbitrary"),
    )(a, b, *ready)
    return res[0], res[1:]


def _sum_slots(recv, grad, axis, chip, core, name):
    n_s, r, c = recv.shape
    tr = 16
    for cand in (256, 128, 176, 64, 32, 16):
        if r % cand == 0:
            tr = cand
            break
    n_i = r // tr

    def body(s_ref, own_ref, r_ref, o_ref):
        tot = own_ref[...].astype(F32)
        for k in range(n_s):
            tot = tot + r_ref[k].astype(F32)
        o_ref[...] = tot

    if axis == 0:
        own_map = lambda i, s: ((2 * s[0] + s[1]) * n_i + i, 0)
    else:
        own_map = lambda i, s: (s[1] * n_i + i, s[0])
    grid_spec = pltpu.PrefetchScalarGridSpec(
        num_scalar_prefetch=1, grid=(n_i,),
        in_specs=[pl.BlockSpec((tr, c), own_map), pl.BlockSpec((n_s, tr, c), lambda i, s: (0, i, 0))],
        out_specs=pl.BlockSpec((tr, c), lambda i, s: (s[1] * n_i + i, 0)))
    return _pallas(body, name=name, grid_spec=grid_spec, out_shape=jax.ShapeDtypeStruct((2 * r, c), F32),
                   compiler_params=_params("arbitrary"))(jnp.stack([chip, core]).astype(jnp.int32), grad, recv)


def _adamw_math(w, g, m, v):
    m = ADAM_B1 * m + (1.0 - ADAM_B1) * g
    v = ADAM_B2 * v + (1.0 - ADAM_B2) * (g * g)
    m_hat = m / (1.0 - ADAM_B1 ** ADAM_STEP)
    v_hat = v / (1.0 - ADAM_B2 ** ADAM_STEP)
    delta = -ADAM_LR * (m_hat / (jnp.sqrt(v_hat) + ADAM_EPS) + ADAM_WD * w)
    return delta, m, v


def _adamw(w, g, m, v, name):
    r, c = w.shape
    tr = r
    for cand in (256, 128, 176, 64, 32, 16, 8):
        if r % cand == 0:
            tr = cand
            break

    def body(w_ref, g_ref, m_ref, v_ref, go_ref, d_ref, nm_ref, nv_ref):
        g = g_ref[...]
        go_ref[...] = g
        d_ref[...], nm_ref[...], nv_ref[...] = _adamw_math(w_ref[...], g, m_ref[...], v_ref[...])

    blk = pl.BlockSpec((tr, c), lambda i: (i, 0))
    return _pallas(body, name=name, grid=(r // tr,), out_shape=tuple(jax.ShapeDtypeStruct((r, c), F32) for _ in range(4)),
                   in_specs=[blk] * 4, out_specs=(blk,) * 4, compiler_params=_params("arbitrary"))(w, g, m, v)


_SMALL = [
    ("b_ada", ROW_DMOD, "mod"), ("norm_mix_g", ROW_GN1, "row"), ("conv_short_w", ROW_W3, "shard"),
    ("conv_conf_w", ROW_W31, "shard"), ("conv_conf_b", ROW_CB, "row"), ("conf_ln_g", ROW_LNG, "row"),
    ("conf_ln_b", ROW_LNB, "row"), ("norm_ffn_g", ROW_GN2, "row"), ("final_norm_g", ROW_GF, "row")]


def _small_update(p_all, chip_onehot, ws, ms, vs):
    n = len(_SMALL)
    d = p_all.shape[2]

    def body(*refs):
        p_ref, oh_ref = refs[:2]
        w_refs, m_refs, v_refs = refs[2:2 + n], refs[2 + n:2 + 2 * n], refs[2 + 2 * n:2 + 3 * n]
        loss_ref = refs[2 + 3 * n]
        g_out, d_out, m_out, v_out = [refs[3 + (3 + q) * n:3 + (4 + q) * n] for q in range(4)]
        g_all = p_ref[0]
        for s in range(1, N_DEV):
            g_all = g_all + p_ref[s]
        loss_ref[...] = jnp.zeros(loss_ref.shape, F32) + (0.5 / d) * jnp.sum(g_all[ROW_SQERR:ROW_SQERR + 1, :])

        def emit(i, idx, g):
            dl, nm, nv = _adamw_math(w_refs[i][idx], g, m_refs[i][idx], v_refs[i][idx])
            g_out[i][idx], d_out[i][idx], m_out[i][idx], v_out[i][idx] = g, dl, nm, nv

        everything = (slice(None), slice(None))
        for i, (_, row, kind) in enumerate(_SMALL):
            if kind == "mod":
                for k in range(N_MOD):
                    emit(i, (slice(0, 1), slice(k * d, (k + 1) * d)), g_all[row + k:row + k + 1, :])
            elif kind == "row":
                emit(i, everything, g_all[row:row + 1, :])
            else:
                _, taps, dq = w_refs[i].shape
                g = jnp.zeros((taps, dq), F32)
                for j in range(N_CHIP):
                    g = g + oh_ref[0:1, j:j + 1] * g_all[row:row + taps, j * dq:(j + 1) * dq]
                emit(i, (0, slice(None), slice(None)), g)

    shapes = [jax.ShapeDtypeStruct(w.shape, F32) for w in ws]
    res = _pallas(body, name="small_update", out_shape=tuple([jax.ShapeDtypeStruct((8, 128), F32)] + shapes * 4),
                  in_specs=[VMEM_FULL] * (2 + 3 * n), out_specs=tuple([VMEM_FULL] * (1 + 4 * n)),
                  compiler_params=_params())(p_all, chip_onehot, *ws, *ms, *vs)
    return res[0], [res[1 + q * n:1 + (q + 1) * n] for q in range(4)]


def _w_ada_grad(c_t, dmod_part):
    def body(c_ref, g_ref, o_ref):
        cv = c_ref[...]
        o_ref[...] = _dot((cv * _sig(cv)).astype(BF16), g_ref[...].astype(BF16))

    return _pallas(body, name="w_ada_grad", out_shape=jax.ShapeDtypeStruct((c_t.shape[0], dmod_part.shape[1]), F32),
                   in_specs=[VMEM_FULL] * 2, out_specs=VMEM_FULL, compiler_params=_params())(c_t, dmod_part)


def _step(x, c, w_ada, b_ada, norm_mix_g, w_in, conv_short_w, w_short_out, conv_conf_w, conv_conf_b, conf_ln_g,
          conf_ln_b, w_conf_out, w_o, norm_ffn_g, w_ffn_in, w_ffn_out, final_norm_g, loss_target, moments, ts):
    xi, yi, ci = _place()
    chip = 2 * xi + yi
    me = 4 * xi + 2 * yi + ci
    x2d, tgt = x[0], loss_target[0]
    s_len, d = x2d.shape
    dq = d // N_CHIP
    mq = N_MOD * d // N_CHIP
    gf = final_norm_g.reshape(1, d)

    taps = jnp.concatenate([jnp.pad(conv_short_w[0], ((0, 8 - SHORT_K), (0, 0))),
                            jnp.pad(conv_conf_w[0], ((0, 32 - CONF_K), (0, 0)))], axis=0)
    n_tr = 40 * dq // d
    c_taps = jnp.concatenate([jnp.broadcast_to(c, (8, d)), taps.reshape(n_tr, d), jnp.zeros((16 - n_tr, d), F32)], axis=0)
    b_part = lax.dynamic_slice(b_ada, (0, chip * mq), (1, mq))
    big = [w_in[0], w_short_out[0], w_conf_out[0], w_o[0], w_ffn_in[0], w_ffn_out[0]]
    axes = [1, 0, 0, 0, 1, 0]
    big_names = ["w_in", "w_short_out", "w_conf_out", "w_o", "w_ffn_in", "w_ffn_out"]
    (win_f, wa_f, wb_f, wo_f, *placed_ffn), c_taps, mod_all = _gather_weights(big, axes, 4, c_taps, w_ada[0], b_part,
                                                                              "gather_weights")
    c_taps = c_taps.reshape(N_DEV, 24, d)
    c_all = c_taps[:, 0]
    mod_all = mod_all.reshape(N_CHIP, 2, N_DEV, mq)[:, 0]
    mod_full = jnp.transpose(mod_all, (1, 0, 2)).reshape(N_DEV, N_MOD * d)
    mod = lax.dynamic_slice(mod_full, (me, 0), (1, N_MOD * d))

    taps_all = c_taps[:, 8:8 + n_tr].reshape(N_CHIP, 2, 40, dq)[:, 0]
    taps_full = jnp.transpose(taps_all, (1, 0, 2)).reshape(40, d)
    w3, w31 = taps_full[0:SHORT_K], taps_full[8:8 + CONF_K]

    (proj, h, lhs3, c3, nrm, rstd, ya, yb, mix, x1), (wfi_f, wfo_f) = _fwd_mix(
        x2d, mod, norm_mix_g, w3, w31, conv_conf_b, conf_ln_g, conf_ln_b, win_f, wa_f, wb_f, wo_f, placed_ffn, axes[4:], ts)
    h2, hid, df, dab, dx1, acc_f = _ffn(x1, tgt, mod, norm_ffn_g, gf, wfi_f, wfo_f, ts)
    f = wfo_f.shape[0]
    bn = min(512, d)
    g_ffn = [_weight_grad(h2, dab, bn, "grad_w_ffn_in")[0], _weight_grad(df, hid, bn // 2, "grad_w_ffn_out", transposed=True)[0]]
    (dproj, rhs3, acc_m), r_ffn = _bwd_mix(dx1, proj, nrm, rstd, c3, ya, yb, mix, mod, w3, w31, conf_ln_g, conf_ln_b,
                                          wa_f, wb_f, wo_f, g_ffn, axes[4:], ts)
    g3 = _weight_grad(lhs3, rhs3, bn // 2, "grad_w_sq", a_cols=d)[0]
    g_sq = [g3[:, k * d:(k + 1) * d] for k in range(3)]
    g_in, r_sq = _weight_grad(h, dproj, bn, "grad_w_in", ready=g_sq, ready_axes=axes[1:4])
    grad_x, acc_i, r_in = _bwd_in(dproj, x2d, dx1, mod, norm_mix_g, win_f, [g_in], axes[:1], min(2 * ts, s_len))
    g_in = [g_in]
    r_in_sq = list(r_in) + list(r_sq)
    grads = g_in + g_sq + g_ffn
    recv = list(r_in_sq) + list(r_ffn)
    g_big, p_all = _swap_halves([_sum_slots(r, g, ax, chip, ci, "sum_slots_" + n)
                                 for r, g, ax, n in zip(recv, grads, axes, big_names)], acc_f + acc_m + acc_i)
    p_all = p_all.reshape(N_DEV, N_SMALL_ROWS, d)

    small_w = dict(b_ada=b_ada, norm_mix_g=norm_mix_g, conv_short_w=conv_short_w, conv_conf_w=conv_conf_w,
                   conv_conf_b=conv_conf_b, conf_ln_g=conf_ln_g, conf_ln_b=conf_ln_b, norm_ffn_g=norm_ffn_g,
                   final_norm_g=final_norm_g)
    as_2d = lambda t: t if t.ndim == 3 else t.reshape((-1, t.shape[-1]))
    chip_onehot = (lax.broadcasted_iota(jnp.int32, (1, 128), 1) == chip).astype(F32)
    loss_tile, small_out = _small_update(
        p_all, chip_onehot, [as_2d(small_w[n]) for n, _, _ in _SMALL], [as_2d(moments["m_" + n]) for n, _, _ in _SMALL],
        [as_2d(moments["v_" + n]) for n, _, _ in _SMALL])
    small_g, small_d, small_m, small_v = [{n: o.reshape(small_w[n].shape) for (n, _, _), o in zip(_SMALL, outs)}
                                          for outs in small_out]

    dmod_all = p_all[:, 0:N_MOD].reshape(N_DEV, N_MOD * d)
    dmod_part = jnp.pad(lax.dynamic_slice(dmod_all, (0, chip * mq), (N_DEV, mq)), ((0, 128 - N_DEV), (0, 0)))
    c_t = jnp.pad(jnp.transpose(c_all), ((0, 0), (0, 128 - N_DEV)))
    g_ada = _w_ada_grad(c_t, dmod_part)

    out_g, out_d, out_m, out_v = dict(small_g), dict(small_d), dict(small_m), dict(small_v)
    for name, w, g in zip(["w_ada"] + big_names, [w_ada[0]] + big, [g_ada] + list(g_big)):
        go, dl, nm, nv = _adamw(w, g, moments["m_" + name][0], moments["v_" + name][0], "adamw_" + name)
        out_g[name], out_d[name], out_m[name], out_v[name] = go[None], dl[None], nm[None], nv[None]

    loss = loss_tile[0, 0]
    return loss, grad_x[None], out_g, out_d, out_m, out_v


_WEIGHTS = ["w_ada", "b_ada", "norm_mix_g", "w_in", "conv_short_w", "w_short_out", "conv_conf_w", "conv_conf_b",
            "conf_ln_g", "conf_ln_b", "w_conf_out", "w_o", "norm_ffn_g", "w_ffn_in", "w_ffn_out", "final_norm_g"]
ROW_TILE = 256


def kernel(x, c, w_ada, b_ada, norm_mix_g, w_in, conv_short_w, w_short_out, conv_conf_w, conv_conf_b, conf_ln_g, conf_ln_b, w_conf_out, w_o, norm_ffn_g, w_ffn_in, w_ffn_out, final_norm_g, loss_target, m_w_ada, m_b_ada, m_norm_mix_g, m_w_in, m_conv_short_w, m_w_short_out, m_conv_conf_w, m_conv_conf_b, m_conf_ln_g, m_conf_ln_b, m_w_conf_out, m_w_o, m_norm_ffn_g, m_w_ffn_in, m_w_ffn_out, m_final_norm_g, v_w_ada, v_b_ada, v_norm_mix_g, v_w_in, v_conv_short_w, v_w_short_out, v_conv_conf_w, v_conv_conf_b, v_conf_ln_g, v_conf_ln_b, v_w_conf_out, v_w_o, v_norm_ffn_g, v_w_ffn_in, v_w_ffn_out, v_final_norm_g):
    moments = dict(
        m_w_ada=m_w_ada, m_b_ada=m_b_ada, m_norm_mix_g=m_norm_mix_g, m_w_in=m_w_in, m_conv_short_w=m_conv_short_w,
        m_w_short_out=m_w_short_out, m_conv_conf_w=m_conv_conf_w, m_conv_conf_b=m_conv_conf_b, m_conf_ln_g=m_conf_ln_g,
        m_conf_ln_b=m_conf_ln_b, m_w_conf_out=m_w_conf_out, m_w_o=m_w_o, m_norm_ffn_g=m_norm_ffn_g, m_w_ffn_in=m_w_ffn_in,
        m_w_ffn_out=m_w_ffn_out, m_final_norm_g=m_final_norm_g,
        v_w_ada=v_w_ada, v_b_ada=v_b_ada, v_norm_mix_g=v_norm_mix_g, v_w_in=v_w_in, v_conv_short_w=v_conv_short_w,
        v_w_short_out=v_w_short_out, v_conv_conf_w=v_conv_conf_w, v_conv_conf_b=v_conv_conf_b, v_conf_ln_g=v_conf_ln_g,
        v_conf_ln_b=v_conf_ln_b, v_w_conf_out=v_w_conf_out, v_w_o=v_w_o, v_norm_ffn_g=v_norm_ffn_g, v_w_ffn_in=v_w_ffn_in,
        v_w_ffn_out=v_w_ffn_out, v_final_norm_g=v_final_norm_g)
    loss, grad_x, g, dl, nm, nv = _step(
        x, c, w_ada, b_ada, norm_mix_g, w_in, conv_short_w, w_short_out, conv_conf_w, conv_conf_b, conf_ln_g, conf_ln_b,
        w_conf_out, w_o, norm_ffn_g, w_ffn_in, w_ffn_out, final_norm_g, loss_target, moments, min(ROW_TILE, x.shape[1]))
    return (loss, grad_x, *[g[n] for n in _WEIGHTS], *[dl[n] for n in _WEIGHTS], *[nm[n] for n in _WEIGHTS],
            *[nv[n] for n in _WEIGHTS])
```

```python
import jax
import jax.numpy as jnp
from jax import lax
from jax.experimental import pallas as pl
from jax.experimental.pallas import tpu as pltpu

F32 = jnp.float32
BF16 = jnp.bfloat16
EPS = 1e-6
LN_EPS = 1e-5
SHORT_K = 3
CONF_K = 31
N_MOD = 6
N_DEV = 8
N_CHIP = 4
ADAM_LR = 0.001
ADAM_B1 = 0.9
ADAM_B2 = 0.999
ADAM_EPS = 1e-08
ADAM_WD = 0.01
ADAM_STEP = 10
MESH = pl.DeviceIdType.MESH
V7X_VMEM_LIMIT_BYTES = 60 * 1024 * 1024
LANES = 128
HALO_SHORT = 8
HALO_CONF = 32
N_SMALL_ROWS = 48
ROW_DMOD, ROW_GN1, ROW_W3, ROW_W31, ROW_CB, ROW_LNG, ROW_LNB, ROW_GN2, ROW_GF, ROW_SQERR = 0, 6, 7, 10, 41, 42, 43, 44, 45, 46
CONV_ROWS = 64
CONV_LANES = 256

ANY = pl.BlockSpec(memory_space=pl.ANY)
VMEM_FULL = pl.BlockSpec(memory_space=pltpu.VMEM)


def _pallas(body, **kw):
    return pl.pallas_call(body, **kw)


def _params(*sem):
    return pltpu.CompilerParams(dimension_semantics=sem, vmem_limit_bytes=V7X_VMEM_LIMIT_BYTES)


def _sig(z):
    return jax.nn.sigmoid(z)


def _dot(a, b):
    return jnp.dot(a, b, preferred_element_type=F32)


def _dot_nt(a, b):
    return lax.dot_general(a, b, (((1,), (1,)), ((), ())), preferred_element_type=F32)


def _dot_tn(a, b):
    return lax.dot_general(a, b, (((0,), (0,)), ((), ())), preferred_element_type=F32)


def _colsum(z):
    return jnp.sum(z, axis=0, keepdims=True)


def _place():
    return lax.axis_index("x"), lax.axis_index("y"), lax.axis_index("c")


def _small_gather_copies(x_ref, out_ref, send_sems, recv_sems, local_sem):
    m_per = x_ref.shape[0]
    x, y, c = _place()
    me, sibling = (x, y, c), (x, y, 1 - c)
    chips = [(1 - x, y), (x, 1 - y), (1 - x, 1 - y)]

    def rows(px, py, pc):
        return out_ref.at[pl.ds((4 * px + 2 * py + pc) * m_per, m_per), :]

    def copy(k, blk, to, src=None):
        return pltpu.make_async_remote_copy(
            src_ref=rows(*blk) if src is None else src, dst_ref=rows(*blk),
            send_sem=send_sems.at[k], recv_sem=recv_sems.at[k], device_id=to, device_id_type=MESH)

    mine = pltpu.make_async_copy(x_ref, rows(*me), local_sem)
    first = [copy(0, me, sibling, src=x_ref)] + [copy(1 + j, me, (*chip, c), src=x_ref) for j, chip in enumerate(chips)]
    return me, sibling, chips, c, copy, mine, first


def _small_gather_start(x_ref, out_ref, send_sems, recv_sems, local_sem):
    _, _, _, _, _, mine, first = _small_gather_copies(x_ref, out_ref, send_sems, recv_sems, local_sem)
    mine.start()
    for cp in first:
        cp.start()


def _small_gather_finish(x_ref, out_ref, send_sems, recv_sems, local_sem):
    me, sibling, chips, c, copy, mine, first = _small_gather_copies(x_ref, out_ref, send_sems, recv_sems, local_sem)
    passed = [copy(4 + j, (*chip, c), sibling) for j, chip in enumerate(chips)]
    for j, chip in enumerate(chips):
        copy(1 + j, (*chip, c), me).wait_recv()
        passed[j].start()
    copy(0, sibling, me).wait_recv()
    for j, chip in enumerate(chips):
        copy(4 + j, (*chip, 1 - c), me).wait_recv()
    for cp in first + passed:
        cp.wait_send()
    mine.wait()


def _small_gather(x_ref, out_ref, *sems):
    _small_gather_start(x_ref, out_ref, *sems)
    _small_gather_finish(x_ref, out_ref, *sems)


def _small_gather_sems():
    return [pltpu.SemaphoreType.DMA((7,)), pltpu.SemaphoreType.DMA((7,)), pltpu.SemaphoreType.DMA]


def _shard_slice(ref, full_shape, shard_axis, chip, half, quarter=None):
    m, n = full_shape
    r = (m // N_CHIP if shard_axis == 0 else m) // 2
    start = half * r + (chip * 2 * r if shard_axis == 0 else 0)
    if quarter is not None:
        r = r // 2
        start = start + quarter * r
    if shard_axis == 0:
        return ref.at[pl.ds(start, r), :]
    cw = n // N_CHIP
    return ref.at[pl.ds(start, r), pl.ds(pl.multiple_of(chip * cw, 128), cw)]


def _gather_copy(ref, region, k, a, to, sems):
    return pltpu.make_async_remote_copy(src_ref=region, dst_ref=region, send_sem=sems[0].at[a, k], recv_sem=sems[1].at[a, k],
                                        device_id=to, device_id_type=MESH)


def _gather_send(refs, shapes, axes, *sems):
    x, y, c = _place()
    for a in range(len(refs)):
        mine = _shard_slice(refs[a], shapes[a], axes[a], 2 * x + y, c)
        _gather_copy(refs[a], mine, 0, a, (1 - x, y, c), sems).start()
        _gather_copy(refs[a], mine, 1, a, (x, 1 - y, c), sems).start()


def _gather_forward(refs, shapes, axes, *sems):
    x, y, c = _place()
    me, sibling, x_nbr, y_nbr = (x, y, c), (x, y, 1 - c), (1 - x, y, c), (x, 1 - y, c)
    chip_x, chip_y, chip_d = 2 * (1 - x) + y, 2 * x + (1 - y), 2 * (1 - x) + (1 - y)
    for a in range(len(refs)):
        region = lambda chip, quarter=None, a=a: _shard_slice(refs[a], shapes[a], axes[a], chip, c, quarter)
        _gather_copy(refs[a], region(chip_x), 0, a, me, sems).wait_recv()
        _gather_copy(refs[a], region(chip_x, 1), 3, a, y_nbr, sems).start()
        _gather_copy(refs[a], region(chip_x), 4, a, sibling, sems).start()
        _gather_copy(refs[a], region(chip_y), 1, a, me, sems).wait_recv()
        _gather_copy(refs[a], region(chip_y, 0), 2, a, x_nbr, sems).start()
        _gather_copy(refs[a], region(chip_y), 5, a, sibling, sems).start()


def _gather_forward_diagonal(refs, shapes, axes, *sems):
    x, y, c = _place()
    me, sibling, chip_d = (x, y, c), (x, y, 1 - c), 2 * (1 - x) + (1 - y)
    for a in range(len(refs)):
        region = lambda chip, quarter=None, a=a: _shard_slice(refs[a], shapes[a], axes[a], chip, c, quarter)
        _gather_copy(refs[a], region(chip_d, 0), 2, a, me, sems).wait_recv()
        _gather_copy(refs[a], region(chip_d, 1), 3, a, me, sems).wait_recv()
        _gather_copy(refs[a], region(chip_d), 6, a, sibling, sems).start()


def _gather_finish(refs, shapes, axes, *sems):
    x, y, c = _place()
    me = (x, y, c)
    chip_me, chip_x, chip_y, chip_d = 2 * x + y, 2 * (1 - x) + y, 2 * x + (1 - y), 2 * (1 - x) + (1 - y)
    for a in range(len(refs)):
        region = lambda chip, half, quarter=None, a=a: _shard_slice(refs[a], shapes[a], axes[a], chip, half, quarter)
        for k, chip in ((4, chip_x), (5, chip_y), (6, chip_d)):
            _gather_copy(refs[a], region(chip, 1 - c), k, a, me, sems).wait_recv()
            _gather_copy(refs[a], region(chip, c), k, a, me, sems).wait_send()
        _gather_copy(refs[a], region(chip_me, c), 0, a, me, sems).wait_send()
        _gather_copy(refs[a], region(chip_me, c), 1, a, me, sems).wait_send()
        _gather_copy(refs[a], region(chip_y, c, 0), 2, a, me, sems).wait_send()
        _gather_copy(refs[a], region(chip_x, c, 1), 3, a, me, sems).wait_send()


def _gather_sems(n_w):
    return [pltpu.SemaphoreType.DMA((n_w, 7)), pltpu.SemaphoreType.DMA((n_w, 7))]


def _gather_weights(shards, axes, n_gather, c_taps, w_ada, b_part, name):
    n_w = len(shards)
    shapes = [(w.shape[0] * N_CHIP, w.shape[1]) if ax == 0 else (w.shape[0], w.shape[1] * N_CHIP) for w, ax in zip(shards, axes)]
    r_ct = c_taps.shape[0]
    mq = w_ada.shape[1]

    def body(*refs):
        ins = refs[:n_w]
        ct_ref, wada_ref, b_ref = refs[n_w:n_w + 3]
        outs = refs[n_w + 3:2 * n_w + 3]
        ct_all_ref, mod_all_ref = refs[2 * n_w + 3:2 * n_w + 5]
        scratch = refs[2 * n_w + 5:]
        stage32, stage16, mod_part = scratch[:n_w], scratch[n_w:2 * n_w], scratch[2 * n_w]
        load_sems, store_sems = scratch[2 * n_w + 1:2 * n_w + 3]
        sems, ct_sems, mod_sems = scratch[2 * n_w + 3:2 * n_w + 5], scratch[2 * n_w + 5:2 * n_w + 8], scratch[2 * n_w + 8:]
        x, y, c = _place()
        chip = 2 * x + y
        loads = [pltpu.make_async_copy(ins[a], stage32[a], load_sems.at[a]) for a in range(n_w)]
        for cp in loads:
            cp.start()
        _small_gather_start(ct_ref, ct_all_ref, *ct_sems)
        stores = []
        for a in range(n_w):
            loads[a].wait()
            stage16[a][...] = stage32[a][...].astype(BF16)
            r, cw = shards[a].shape
            own = outs[a].at[pl.ds(chip * r, r), :] if axes[a] == 0 else outs[a].at[:, pl.ds(pl.multiple_of(chip * cw, 128), cw)]
            stores.append(pltpu.make_async_copy(stage16[a], own, store_sems.at[a]))
            stores[-1].start()
            if a < n_gather:
                half = stage16[a].at[pl.ds(c * (r // 2), r // 2), :]
                there = _shard_slice(outs[a], shapes[a], axes[a], chip, c)
                for k, to in ((0, (1 - x, y, c)), (1, (x, 1 - y, c))):
                    pltpu.make_async_remote_copy(src_ref=half, dst_ref=there, send_sem=sems[0].at[a, k],
                                                 recv_sem=sems[1].at[a, k], device_id=to, device_id_type=MESH).start()
        _small_gather_finish(ct_ref, ct_all_ref, *ct_sems)
        cv = jnp.concatenate([ct_all_ref[r_ct * dv:r_ct * dv + 1, :] for dv in range(N_DEV)], axis=0)
        mod_part[...] = _dot((cv * _sig(cv)).astype(BF16), wada_ref[...].astype(BF16)) + b_ref[...]
        _small_gather_start(mod_part, mod_all_ref, *mod_sems)
        _gather_forward(outs[:n_gather], shapes, axes, *sems)
        _small_gather_finish(mod_part, mod_all_ref, *mod_sems)
        _gather_forward_diagonal(outs[:n_gather], shapes, axes, *sems)
        _gather_finish(outs[:n_gather], shapes, axes, *sems)
        for cp in stores:
            cp.wait()

    res = _pallas(
        body, name=name,
        out_shape=tuple(jax.ShapeDtypeStruct(sh, BF16) for sh in shapes)
        + (jax.ShapeDtypeStruct((N_DEV * r_ct, c_taps.shape[1]), F32), jax.ShapeDtypeStruct((N_DEV * N_DEV, mq), F32)),
        in_specs=[ANY] * n_w + [VMEM_FULL] * 3, out_specs=tuple([ANY] * n_w + [VMEM_FULL] * 2),
        scratch_shapes=[pltpu.VMEM(w.shape, F32) for w in shards] + [pltpu.VMEM(w.shape, BF16) for w in shards]
        + [pltpu.VMEM((N_DEV, mq), F32), pltpu.SemaphoreType.DMA((n_w,)), pltpu.SemaphoreType.DMA((n_w,))]
        + _gather_sems(n_gather) + _small_gather_sems() + _small_gather_sems(),
        compiler_params=_params(),
    )(*shards, c_taps, w_ada, b_part)
    return res[:n_w], res[n_w], res[n_w + 1]


def _piece_shapes(grads, axes):
    return [((g.shape[0] // N_CHIP // 2, g.shape[1]) if ax == 0 else (g.shape[0] // 2, g.shape[1] // N_CHIP))
            for g, ax in zip(grads, axes)]


def _exchange_send(ins, outs, shapes, axes, send_sems, recv_sems):
    x, y, c = _place()
    me = 4 * x + 2 * y + c
    for a in range(len(ins)):
        for t in range(N_DEV):
            tx, ty, tc = t // 4, (t // 2) % 2, t % 2
            src = _shard_slice(ins[a], shapes[a], axes[a], 2 * tx + ty, tc)

            @pl.when(me != t)
            def _(src=src, a=a, t=t, to=(tx, ty, tc)):
                slot = (me - t + N_DEV) % N_DEV - 1
                pltpu.make_async_remote_copy(src_ref=src, dst_ref=outs[a].at[slot], send_sem=send_sems.at[a, t],
                                             recv_sem=recv_sems.at[a, slot], device_id=to, device_id_type=MESH).start()


def _exchange_finish(outs, send_sems, recv_sems):
    x, y, c = _place()
    me = 4 * x + 2 * y + c
    for a in range(len(outs)):
        for t in range(N_DEV):
            @pl.when(me != t)
            def _(a=a, t=t):
                slot = outs[a].at[0]
                pltpu.make_async_remote_copy(src_ref=slot, dst_ref=slot, send_sem=send_sems.at[a, t], recv_sem=recv_sems.at[a, 0],
                                             device_id=(x, y, c), device_id_type=MESH).wait_send()
        for k in range(N_DEV - 1):
            slot = outs[a].at[k]
            pltpu.make_async_remote_copy(src_ref=slot, dst_ref=slot, send_sem=send_sems.at[a, 0], recv_sem=recv_sems.at[a, k],
                                         device_id=(x, y, c), device_id_type=MESH).wait_recv()


def _exchange_sems(n_w):
    return [pltpu.SemaphoreType.DMA((n_w, N_DEV)), pltpu.SemaphoreType.DMA((n_w, N_DEV - 1))]


def _exchange_out_shapes(grads, axes):
    return tuple(jax.ShapeDtypeStruct((N_DEV - 1,) + p, BF16) for p in _piece_shapes(grads, axes))


def _swap_halves(fulls, block):
    n_w = len(fulls)

    def body(*refs):
        block_ref = refs[n_w]
        outs, gathered_ref = refs[n_w + 1:2 * n_w + 1], refs[2 * n_w + 1]
        send_sems, recv_sems = refs[2 * n_w + 2:2 * n_w + 4]
        x, y, c = _place()
        sibling = (x, y, 1 - c)
        waits = []
        for a in range(n_w):
            r = fulls[a].shape[0] // 2
            mine = outs[a].at[pl.ds(c * r, r), :]
            theirs = outs[a].at[pl.ds((1 - c) * r, r), :]
            rc = pltpu.make_async_remote_copy(src_ref=mine, dst_ref=mine, send_sem=send_sems.at[a], recv_sem=recv_sems.at[a],
                                              device_id=sibling, device_id_type=MESH)
            rc.start()
            got = pltpu.make_async_remote_copy(src_ref=theirs, dst_ref=theirs, send_sem=send_sems.at[a],
                                               recv_sem=recv_sems.at[a], device_id=sibling, device_id_type=MESH)
            waits += [rc.wait_send, got.wait_recv]
        _small_gather(block_ref, gathered_ref, *refs[2 * n_w + 4:])
        for w in waits:
            w()

    res = _pallas(
        body, name="swap_halves",
        out_shape=tuple(jax.ShapeDtypeStruct(f.shape, f.dtype) for f in fulls)
        + (jax.ShapeDtypeStruct((N_DEV * block.shape[0], block.shape[1]), block.dtype),),
        in_specs=[ANY] * n_w + [VMEM_FULL], out_specs=tuple([ANY] * n_w + [VMEM_FULL]),
        input_output_aliases={a: a for a in range(n_w)},
        scratch_shapes=[pltpu.SemaphoreType.DMA((n_w,)), pltpu.SemaphoreType.DMA((n_w,))] + _small_gather_sems(),
    )(*fulls, block)
    return res[:n_w], res[n_w]


def _load_weights(pairs, sems):
    @pl.when(pl.program_id(0) == 0)
    def _():
        cps = [pltpu.make_async_copy(s, d, sems.at[k]) for k, (s, d) in enumerate(pairs)]
        for cp in cps:
            cp.start()
        for cp in cps:
            cp.wait()


def _conv_chunks(ext, e_buf, ts, d, chunk_fn, unroll, between=()):
    rows, dh = e_buf.shape[1], e_buf.shape[2]
    for hf in range(d // dh):
        for r in range(1, 8):
            e_buf[r - 1, :, :] = ext[r:r + rows, hf * dh:(hf + 1) * dh]
        for lq in range(dh // CONV_LANES):
            lo, el = hf * dh + lq * CONV_LANES, lq * CONV_LANES

            def body(ci, carry, lo=lo, el=el):
                r0 = pl.multiple_of(ci * CONV_ROWS, CONV_ROWS)

                def tap(off):
                    q, r = divmod(off, 8)
                    if r == 0:
                        return ext[pl.ds(r0 + 8 * q, CONV_ROWS), lo:lo + CONV_LANES]
                    return e_buf[r - 1, pl.ds(r0 + 8 * q, CONV_ROWS), el:el + CONV_LANES]

                chunk_fn(tap, r0, lo)
                return carry

            lax.fori_loop(0, ts // CONV_ROWS, body, 0, unroll=unroll)
            if between:
                between[0]()
                between = between[1:]
    for fn in between:
        fn()


def _fwd_mix(x, mod, gn1, w3, w31, cbias, lng, lnb, w_in, w_a, w_b, w_o, later, later_axes, ts):
    s_len, d = x.shape
    n_t = s_len // ts
    n_l = len(later)
    later_shapes = [w.shape for w in later]

    def body(*refs):
        (x_ref, mod_ref, gn1_ref, w3_ref, w31_ref, cb_ref, lng_ref, lnb_ref, win_hbm, wa_hbm, wb_hbm, wo_hbm) = refs[:12]
        (proj_ref, h_ref, lhs3_ref, c3_ref, n_ref, rstd_ref, ya_ref, yb_ref, mix_ref, x1_ref) = refs[12 + n_l:22 + n_l]
        later_refs = refs[22 + n_l:22 + 2 * n_l]
        win_v, wa_v, wb_v, wo_v, cv_ext, u0_ext, e_buf, u1_buf, sems, g_send, g_recv = refs[22 + 2 * n_l:]
        step = pl.program_id(0)
        pl.when(step == 0)(lambda: _gather_send(later_refs, later_shapes, later_axes, g_send, g_recv))
        pl.when(step == n_t // 2)(lambda: _gather_forward(later_refs, later_shapes, later_axes, g_send, g_recv))
        pl.when(step == (3 * n_t) // 4)(lambda: _gather_forward_diagonal(later_refs, later_shapes, later_axes, g_send, g_recv))
        pl.when(step == n_t - 1)(lambda: _gather_finish(later_refs, later_shapes, later_axes, g_send, g_recv))
        _load_weights([(win_hbm, win_v), (wa_hbm, wa_v), (wb_hbm, wb_v), (wo_hbm, wo_v)], sems)

        @pl.when(pl.program_id(0) == 0)
        def _():
            cv_ext[0:HALO_SHORT, :] = jnp.zeros((HALO_SHORT, d), F32)
            u0_ext[0:HALO_CONF, :] = jnp.zeros((HALO_CONF, d), F32)

        xv = x_ref[...]
        sh1, sc1, g1 = mod_ref[:, 0:d], mod_ref[:, d:2 * d], mod_ref[:, 2 * d:3 * d]
        r = lax.rsqrt(jnp.mean(xv * xv, axis=-1, keepdims=True) + EPS)
        hb = (((xv * r) * gn1_ref[...]) * (1.0 + sc1) + sh1).astype(BF16)
        h_ref[...] = hb

        def pj(k):
            p = _dot(hb, win_v[:, k * d:(k + 1) * d])
            proj_ref[:, k * d:(k + 1) * d] = p.astype(BF16)
            return p

        v_c = pj(3)
        u0_ext[HALO_CONF:HALO_CONF + ts, :] = v_c * _sig(pj(4))

        def conv_chunk(tap, r0, lo):
            acc = jnp.zeros((CONV_ROWS, CONV_LANES), F32)
            for k in range(CONF_K):
                acc = acc + w31_ref[k:k + 1, lo:lo + CONV_LANES] * tap(HALO_CONF - (CONF_K - 1) + k)
            u1_buf[pl.ds(r0, CONV_ROWS), lo:lo + CONV_LANES] = acc

        pjs = {}
        _conv_chunks(u0_ext, e_buf, ts, d, conv_chunk, unroll=True,
                     between=[lambda k=k: pjs.__setitem__(k, pj(k)) for k in (1, 2, 0, 5, 6)])
        u0_ext[0:HALO_CONF, :] = u0_ext[ts:ts + HALO_CONF, :]

        cv_ext[HALO_SHORT:HALO_SHORT + ts, :] = pjs[1] * pjs[2]
        conv3 = jnp.zeros((ts, d), F32)
        for k in range(SHORT_K):
            off = HALO_SHORT - (SHORT_K - 1) + k
            conv3 = conv3 + w3_ref[k:k + 1, :] * cv_ext[off:off + ts, :]
        cv_ext[0:HALO_SHORT, :] = cv_ext[ts:ts + HALO_SHORT, :]
        c3_ref[...] = conv3.astype(BF16)
        yap = (pjs[0] * conv3).astype(BF16)
        lhs3_ref[:, 0:d] = yap
        y_a = _dot(yap, wa_v[...])
        ya_ref[...] = y_a.astype(BF16)

        u1 = u1_buf[...] + cb_ref[...]
        mu = jnp.mean(u1, axis=-1, keepdims=True)
        var = jnp.mean(jnp.square(u1 - mu), axis=-1, keepdims=True)
        rstd = lax.rsqrt(var + LN_EPS)
        nrm = (u1 - mu) * rstd
        n_ref[...] = nrm.astype(BF16)
        rstd_ref[...] = jnp.broadcast_to(rstd, (ts, LANES))
        u2 = nrm * lng_ref[...] + lnb_ref[...]
        u3 = (u2 * _sig(u2)).astype(BF16)
        lhs3_ref[:, d:2 * d] = u3
        y_b = _dot(u3, wb_v[...])
        yb_ref[...] = y_b.astype(BF16)

        merged = (_sig(pjs[5]) * y_a + _sig(pjs[6]) * y_b).astype(BF16)
        lhs3_ref[:, 2 * d:3 * d] = merged
        mix = _dot(merged, wo_v[...])
        mix_ref[...] = mix
        x1_ref[...] = xv + g1 * mix

    row = lambda w, dt=None: pl.BlockSpec((ts, w), lambda i: (i, 0))
    full = lambda a: pl.BlockSpec(a.shape, lambda i: (0, 0))
    sd = lambda w, dt: jax.ShapeDtypeStruct((s_len, w), dt)
    outs = [sd(7 * d, BF16), sd(d, BF16), sd(3 * d, BF16), sd(d, BF16), sd(d, BF16), sd(LANES, F32), sd(d, BF16), sd(d, BF16),
            sd(d, F32), sd(d, F32)]
    res = _pallas(
        body, name="fwd_mix", grid=(n_t,),
        out_shape=tuple(outs) + tuple(jax.ShapeDtypeStruct(w.shape, BF16) for w in later),
        in_specs=[row(d), full(mod), full(gn1), full(w3), full(w31), full(cbias), full(lng), full(lnb), ANY, ANY, ANY, ANY]
        + [ANY] * n_l,
        out_specs=tuple(row(o.shape[1]) for o in outs) + tuple([ANY] * n_l),
        input_output_aliases={12 + k: len(outs) + k for k in range(n_l)},
        scratch_shapes=[pltpu.VMEM(w_in.shape, BF16), pltpu.VMEM(w_a.shape, BF16), pltpu.VMEM(w_b.shape, BF16),
                        pltpu.VMEM(w_o.shape, BF16), pltpu.VMEM((ts + HALO_SHORT, d), F32),
                        pltpu.VMEM((ts + HALO_CONF, d), F32), pltpu.VMEM((7, ts + HALO_CONF - 8, d // 2), F32),
                        pltpu.VMEM((ts, d), F32), pltpu.SemaphoreType.DMA((4,))] + _gather_sems(n_l),
        compiler_params=_params("arbitrary"),
    )(x, mod, gn1, w3, w31, cbias, lng, lnb, w_in, w_a, w_b, w_o, *later)
    return res[:len(outs)], res[len(outs):]


def _ffn(x1, target, mod, gn2, gf, w_fi, w_fo, ts):
    s_len, d = x1.shape
    f = w_fo.shape[0]
    n_t = s_len // ts

    def body(x1_ref, t_ref, mod_ref, gn2_ref, gf_ref, wfi_hbm, wfo_hbm,
             h2_ref, hid_ref, df_ref, dab_ref, dx1_ref, acc_ref, wfi_v, wfo_v, sems):
        _load_weights([(wfi_hbm, wfi_v), (wfo_hbm, wfo_v)], sems)

        @pl.when(pl.program_id(0) == 0)
        def _():
            acc_ref[...] = jnp.zeros(acc_ref.shape, F32)

        x1v = x1_ref[...]
        sh2, sc2, g2 = mod_ref[:, 3 * d:4 * d], mod_ref[:, 4 * d:5 * d], mod_ref[:, 5 * d:6 * d]
        gn2v, gfv = gn2_ref[...], gf_ref[...]
        r2 = lax.rsqrt(jnp.mean(x1v * x1v, axis=-1, keepdims=True) + EPS)
        xn2 = x1v * r2
        h2 = ((xn2 * gn2v) * (1.0 + sc2) + sh2).astype(BF16)
        h2_ref[...] = h2
        a = _dot(h2, wfi_v[:, 0:f])
        bg = _dot(h2, wfi_v[:, f:2 * f])
        sa = _sig(a)
        silu_a = a * sa
        hid = (silu_a * bg).astype(BF16)
        hid_ref[...] = hid
        ffn = _dot(hid, wfo_v[...])
        x2 = x1v + g2 * ffn
        r3 = lax.rsqrt(jnp.mean(x2 * x2, axis=-1, keepdims=True) + EPS)
        xn3 = x2 * r3
        diff = xn3 * gfv - t_ref[...]
        dy = diff * (1.0 / d)
        dxn3 = dy * gfv
        dx2 = r3 * (dxn3 - xn3 * jnp.mean(dxn3 * xn3, axis=-1, keepdims=True))
        acc_ref[ROW_GF:ROW_GF + 1, :] += _colsum(dy * xn3)
        acc_ref[ROW_DMOD + 5:ROW_DMOD + 5 + 1, :] += _colsum(dx2 * ffn)
        acc_ref[ROW_SQERR:ROW_SQERR + 1, :] += _colsum(diff * diff)

        df = (dx2 * g2).astype(BF16)
        df_ref[...] = df
        dhid = _dot_nt(df, wfo_v[...])
        da = dhid * bg * (sa * (1.0 + a * (1.0 - sa)))
        dbg = dhid * silu_a
        dab_ref[:, 0:f] = da.astype(BF16)
        dab_ref[:, f:2 * f] = dbg.astype(BF16)
        dh2 = _dot_nt(da.astype(BF16), wfi_v[:, 0:f]) + _dot_nt(dbg.astype(BF16), wfi_v[:, f:2 * f])
        acc_ref[ROW_DMOD + 3:ROW_DMOD + 3 + 1, :] += _colsum(dh2)
        acc_ref[ROW_DMOD + 4:ROW_DMOD + 4 + 1, :] += _colsum(dh2 * (xn2 * gn2v))
        acc_ref[ROW_GN2:ROW_GN2 + 1, :] += _colsum(dh2 * xn2 * (1.0 + sc2))
        dxn2 = dh2 * gn2v * (1.0 + sc2)
        dx1_ref[...] = dx2 + r2 * (dxn2 - xn2 * jnp.mean(dxn2 * xn2, axis=-1, keepdims=True))

    row = lambda w: pl.BlockSpec((ts, w), lambda i: (i, 0))
    full = lambda a: pl.BlockSpec(a.shape, lambda i: (0, 0))
    sd = lambda w, dt: jax.ShapeDtypeStruct((s_len, w), dt)
    outs = [sd(d, BF16), sd(f, BF16), sd(d, BF16), sd(2 * f, BF16), sd(d, F32), jax.ShapeDtypeStruct((N_SMALL_ROWS, d), F32)]
    return _pallas(
        body, name="ffn", grid=(n_t,), out_shape=tuple(outs),
        in_specs=[row(d), row(d), full(mod), full(gn2), full(gf), ANY, ANY],
        out_specs=tuple([row(o.shape[1]) for o in outs[:-1]] + [pl.BlockSpec((N_SMALL_ROWS, d), lambda i: (0, 0))]),
        scratch_shapes=[pltpu.VMEM(w_fi.shape, BF16), pltpu.VMEM(w_fo.shape, BF16), pltpu.SemaphoreType.DMA((2,))],
        compiler_params=_params("arbitrary"),
    )(x1, target, mod, gn2, gf, w_fi, w_fo)


def _bwd_mix(dx1, proj, nrm, rstd, c3, ya, yb, mix, mod, w3, w31, lng, lnb, w_a, w_b, w_o, ready, ready_axes, ts):
    s_len, d = dx1.shape
    n_t = s_len // ts
    n_r = len(ready)
    ready_shapes = [g.shape for g in ready]

    def body(*refs):
        (dx1_ref, proj_ref, n_ref, rstd_ref, c3_ref, ya_ref, yb_ref, mix_ref, mod_ref, w3_ref, w31_ref, lng_ref,
         lnb_ref, wa_hbm, wb_hbm, wo_hbm) = refs[:16]
        ready_refs = refs[16:16 + n_r]
        dproj_ref, rhs3_ref, acc_ref = refs[16 + n_r:19 + n_r]
        recv_refs = refs[19 + n_r:19 + 2 * n_r]
        wa_v, wb_v, wo_v, dc3_ext, du1_ext, e_buf, u0_buf, du0_buf, dw8, sems, e_send, e_recv = refs[19 + 2 * n_r:]
        step = pl.program_id(0)
        pl.when(step == 0)(lambda: _exchange_send(ready_refs, recv_refs, ready_shapes, ready_axes, e_send, e_recv))
        pl.when(step == n_t - 1)(lambda: _exchange_finish(recv_refs, e_send, e_recv))
        _load_weights([(wa_hbm, wa_v), (wb_hbm, wb_v), (wo_hbm, wo_v)], sems)

        @pl.when(pl.program_id(0) == 0)
        def _():
            acc_ref[...] = jnp.zeros(acc_ref.shape, F32)
            dc3_ext[ts:ts + HALO_SHORT, :] = jnp.zeros((HALO_SHORT, d), F32)
            du1_ext[ts:ts + HALO_CONF, :] = jnp.zeros((HALO_CONF, d), F32)
            dw8[...] = jnp.zeros(dw8.shape, F32)

        pj = lambda k: proj_ref[:, k * d:(k + 1) * d].astype(F32)
        g1 = mod_ref[:, 2 * d:3 * d]
        dx1v = dx1_ref[...]
        acc_ref[ROW_DMOD + 2:ROW_DMOD + 2 + 1, :] += _colsum(dx1v * mix_ref[...])
        dmix = (dx1v * g1).astype(BF16)
        rhs3_ref[:, 2 * d:3 * d] = dmix
        dmerged = _dot_nt(dmix, wo_v[...])
        g_a, g_b = _sig(pj(5)), _sig(pj(6))
        y_a, y_b = ya_ref[...].astype(F32), yb_ref[...].astype(F32)
        dproj_ref[:, 5 * d:6 * d] = (dmerged * y_a * g_a * (1.0 - g_a)).astype(BF16)
        dproj_ref[:, 6 * d:7 * d] = (dmerged * y_b * g_b * (1.0 - g_b)).astype(BF16)
        dya = (dmerged * g_a).astype(BF16)
        dyb = (dmerged * g_b).astype(BF16)
        rhs3_ref[:, 0:d] = dya
        rhs3_ref[:, d:2 * d] = dyb

        dyap = _dot_nt(dya, wa_v[...])
        dproj_ref[:, 0:d] = (dyap * c3_ref[...].astype(F32)).astype(BF16)
        dc3_ext[0:ts, :] = dyap * pj(0)
        c_s, v_s = pj(1), pj(2)
        cv = c_s * v_s
        dcv = jnp.zeros((ts, d), F32)
        for j in range(SHORT_K):
            k = SHORT_K - 1 - j
            sj = dc3_ext[j:j + ts, :]
            dcv = dcv + w3_ref[k:k + 1, :] * sj
            acc_ref[ROW_W3 + k:ROW_W3 + k + 1, :] += _colsum(cv * sj)
        dc3_ext[ts:ts + HALO_SHORT, :] = dc3_ext[0:HALO_SHORT, :]
        dproj_ref[:, d:2 * d] = (dcv * v_s).astype(BF16)
        dproj_ref[:, 2 * d:3 * d] = (dcv * c_s).astype(BF16)

        du3 = _dot_nt(dyb, wb_v[...])
        nv = n_ref[...].astype(F32)
        lg = lng_ref[...]
        u2 = nv * lg + lnb_ref[...]
        s2 = _sig(u2)
        du2 = du3 * (s2 * (1.0 + u2 * (1.0 - s2)))
        acc_ref[ROW_LNG:ROW_LNG + 1, :] += _colsum(du2 * nv)
        acc_ref[ROW_LNB:ROW_LNB + 1, :] += _colsum(du2)
        dn = du2 * lg
        du1 = rstd_ref[:, 0:1] * (dn - jnp.mean(dn, axis=-1, keepdims=True) - nv * jnp.mean(dn * nv, axis=-1, keepdims=True))
        acc_ref[ROW_CB:ROW_CB + 1, :] += _colsum(du1)
        du1_ext[0:ts, :] = du1
        v_c = pj(3)
        sg = _sig(pj(4))
        u0_buf[...] = v_c * sg

        def conv_chunk(tap, r0, lo):
            lanes = slice(lo, lo + CONV_LANES)
            u0c = u0_buf[pl.ds(r0, CONV_ROWS), lanes]
            acc = jnp.zeros((CONV_ROWS, CONV_LANES), F32)
            for j in range(CONF_K):
                k = CONF_K - 1 - j
                sj = tap(j)
                acc = acc + w31_ref[k:k + 1, lanes] * sj
                prod = u0c * sj
                part = prod[0:8]
                for i in range(1, CONV_ROWS // 8):
                    part = part + prod[8 * i:8 * i + 8]
                dw8[k, :, lanes] += part
            du0_buf[pl.ds(r0, CONV_ROWS), lanes] = acc

        _conv_chunks(du1_ext, e_buf, ts, d, conv_chunk, unroll=False)
        du0 = du0_buf[...]
        du1_ext[ts:ts + HALO_CONF, :] = du1_ext[0:HALO_CONF, :]

        @pl.when(step == n_t - 1)
        def _():
            for k in range(CONF_K):
                acc_ref[ROW_W31 + k:ROW_W31 + k + 1, :] = _colsum(dw8[k])
        dproj_ref[:, 3 * d:4 * d] = (du0 * sg).astype(BF16)
        dproj_ref[:, 4 * d:5 * d] = (du0 * v_c * sg * (1.0 - sg)).astype(BF16)

    rev = lambda w: pl.BlockSpec((ts, w), lambda i: (n_t - 1 - i, 0))
    full = lambda a: pl.BlockSpec(a.shape, lambda i: (0, 0))
    sd = lambda w, dt: jax.ShapeDtypeStruct((s_len, w), dt)
    outs = [sd(7 * d, BF16), sd(3 * d, BF16), jax.ShapeDtypeStruct((N_SMALL_ROWS, d), F32)]
    res = _pallas(
        body, name="bwd_mix", grid=(n_t,), out_shape=tuple(outs) + _exchange_out_shapes(ready, ready_axes),
        in_specs=[rev(d), rev(7 * d), rev(d), rev(LANES), rev(d), rev(d), rev(d), rev(d), full(mod), full(w3), full(w31),
                  full(lng), full(lnb), ANY, ANY, ANY] + [ANY] * n_r,
        out_specs=tuple([rev(o.shape[1]) for o in outs[:-1]] + [pl.BlockSpec((N_SMALL_ROWS, d), lambda i: (0, 0))] + [ANY] * n_r),
        scratch_shapes=[pltpu.VMEM(w_a.shape, BF16), pltpu.VMEM(w_b.shape, BF16), pltpu.VMEM(w_o.shape, BF16),
                        pltpu.VMEM((ts + HALO_SHORT, d), F32), pltpu.VMEM((ts + HALO_CONF, d), F32),
                        pltpu.VMEM((7, ts + HALO_CONF - 8, d // 2), F32), pltpu.VMEM((ts, d), F32), pltpu.VMEM((ts, d), F32),
                        pltpu.VMEM((CONF_K, 8, d), F32), pltpu.SemaphoreType.DMA((3,))] + _exchange_sems(n_r),
        compiler_params=_params("arbitrary"),
    )(dx1, proj, nrm, rstd, c3, ya, yb, mix, mod, w3, w31, lng, lnb, w_a, w_b, w_o, *ready)
    return res[:len(outs)], res[len(outs):]


def _bwd_in(dproj, x, dx1, mod, gn1, w_in, ready, ready_axes, ts):
    s_len, d = x.shape
    n_t = s_len // ts
    n_r = len(ready)
    ready_shapes = [g.shape for g in ready]

    def body(*refs):
        dproj_ref, x_ref, dx1_ref, mod_ref, gn1_ref, win_hbm = refs[:6]
        ready_refs = refs[6:6 + n_r]
        gx_ref, acc_ref = refs[6 + n_r:8 + n_r]
        recv_refs = refs[8 + n_r:8 + 2 * n_r]
        win_v, sems, e_send, e_recv = refs[8 + 2 * n_r:]
        step = pl.program_id(0)
        pl.when(step == 0)(lambda: _exchange_send(ready_refs, recv_refs, ready_shapes, ready_axes, e_send, e_recv))
        pl.when(step == n_t - 1)(lambda: _exchange_finish(recv_refs, e_send, e_recv))
        _load_weights([(win_hbm, win_v)], sems)

        @pl.when(pl.program_id(0) == 0)
        def _():
            acc_ref[...] = jnp.zeros(acc_ref.shape, F32)

        xv = x_ref[...]
        sc1 = mod_ref[:, d:2 * d]
        gn1v = gn1_ref[...]
        r = lax.rsqrt(jnp.mean(xv * xv, axis=-1, keepdims=True) + EPS)
        xn = xv * r
        dh = _dot_nt(dproj_ref[...], win_v[...])
        acc_ref[ROW_DMOD:ROW_DMOD + 1, :] += _colsum(dh)
        acc_ref[ROW_DMOD + 1:ROW_DMOD + 1 + 1, :] += _colsum(dh * (xn * gn1v))
        acc_ref[ROW_GN1:ROW_GN1 + 1, :] += _colsum(dh * xn * (1.0 + sc1))
        dxn = dh * gn1v * (1.0 + sc1)
        gx_ref[...] = dx1_ref[...] + r * (dxn - xn * jnp.mean(dxn * xn, axis=-1, keepdims=True))

    row = lambda w: pl.BlockSpec((ts, w), lambda i: (i, 0))
    full = lambda a: pl.BlockSpec(a.shape, lambda i: (0, 0))
    res = _pallas(
        body, name="bwd_in", grid=(n_t,),
        out_shape=(jax.ShapeDtypeStruct((s_len, d), F32), jax.ShapeDtypeStruct((N_SMALL_ROWS, d), F32))
        + _exchange_out_shapes(ready, ready_axes),
        in_specs=[row(7 * d), row(d), row(d), full(mod), full(gn1), ANY] + [ANY] * n_r,
        out_specs=(row(d), pl.BlockSpec((N_SMALL_ROWS, d), lambda i: (0, 0))) + tuple([ANY] * n_r),
        scratch_shapes=[pltpu.VMEM(w_in.shape, BF16), pltpu.SemaphoreType.DMA((1,))] + _exchange_sems(n_r),
        compiler_params=_params("arbitrary"),
    )(dproj, x, dx1, mod, gn1, w_in, *ready)
    return res[0], res[1], res[2:]


def _weight_grad(a, b, bn, name, transposed=False, ready=(), ready_axes=(), a_cols=None):
    s_len, m = a.shape[0], (a_cols or a.shape[1])
    n = b.shape[1]
    n_j = n // bn
    n_r = len(ready)
    ready_shapes = [g.shape for g in ready]

    def body(*refs):
        a_hbm, b_ref = refs[:2]
        ready_refs = refs[2:2 + n_r]
        o_ref = refs[2 + n_r]
        recv_refs = refs[3 + n_r:3 + 2 * n_r]
        a_v, sem = refs[3 + 2 * n_r:5 + 2 * n_r]
        step = pl.program_id(0)
        if n_r:
            e_send, e_recv = refs[5 + 2 * n_r:]
            pl.when(step == 0)(lambda: _exchange_send(ready_refs, recv_refs, ready_shapes, ready_axes, e_send, e_recv))
            pl.when(step == n_j - 1)(lambda: _exchange_finish(recv_refs, e_send, e_recv))
        if a_cols is None:
            _load_weights([(a_hbm, a_v)], sem)
            prod = _dot_tn(a_v[...], b_ref[...])
        else:
            prod = _dot_tn(a_hbm[...], b_ref[...])
        o_ref[...] = (prod.T if transposed else prod).astype(BF16)

    out = jax.ShapeDtypeStruct((n, m) if transposed else (m, n), BF16)
    out_spec = pl.BlockSpec((bn, m), lambda j: (j, 0)) if transposed else pl.BlockSpec((m, bn), lambda j: (0, j))
    res = _pallas(
        body, name=name, grid=(n_j,), out_shape=(out,) + _exchange_out_shapes(ready, ready_axes),
        in_specs=[ANY if a_cols is None else pl.BlockSpec((s_len, a_cols), lambda j: (0, (j * bn) // a_cols)),
                  pl.BlockSpec((s_len, bn), lambda j: (0, j))] + [ANY] * n_r,
        out_specs=(out_spec,) + tuple([ANY] * n_r),
        scratch_shapes=[pltpu.VMEM((8, 128) if a_cols else a.shape, BF16), pltpu.SemaphoreType.DMA((1,))]
        + (_exchange_sems(n_r) if n_r else []),
        compiler_params=_params("arbitrary"),
    )(a, b, *ready)
    return res[0], res[1:]


def _sum_slots(recv, grad, axis, chip, core, name):
    n_s, r, c = recv.shape
    tr = 16
    for cand in (256, 128, 176, 64, 32, 16):
        if r % cand == 0:
            tr = cand
            break
    n_i = r // tr

    def body(s_ref, own_ref, r_ref, o_ref):
        tot = own_ref[...].astype(F32)
        for k in range(n_s):
            tot = tot + r_ref[k].astype(F32)
        o_ref[...] = tot

    if axis == 0:
        own_map = lambda i, s: ((2 * s[0] + s[1]) * n_i + i, 0)
    else:
        own_map = lambda i, s: (s[1] * n_i + i, s[0])
    grid_spec = pltpu.PrefetchScalarGridSpec(
        num_scalar_prefetch=1, grid=(n_i,),
        in_specs=[pl.BlockSpec((tr, c), own_map), pl.BlockSpec((n_s, tr, c), lambda i, s: (0, i, 0))],
        out_specs=pl.BlockSpec((tr, c), lambda i, s: (s[1] * n_i + i, 0)))
    return _pallas(body, name=name, grid_spec=grid_spec, out_shape=jax.ShapeDtypeStruct((2 * r, c), F32),
                   compiler_params=_params("arbitrary"))(jnp.stack([chip, core]).astype(jnp.int32), grad, recv)


def _adamw_math(w, g, m, v):
    m = ADAM_B1 * m + (1.0 - ADAM_B1) * g
    v = ADAM_B2 * v + (1.0 - ADAM_B2) * (g * g)
    m_hat = m / (1.0 - ADAM_B1 ** ADAM_STEP)
    v_hat = v / (1.0 - ADAM_B2 ** ADAM_STEP)
    delta = -ADAM_LR * (m_hat / (jnp.sqrt(v_hat) + ADAM_EPS) + ADAM_WD * w)
    return delta, m, v


def _adamw(w, g, m, v, name):
    r, c = w.shape
    tr = r
    for cand in (256, 128, 176, 64, 32, 16, 8):
        if r % cand == 0:
            tr = cand
            break

    def body(w_ref, g_ref, m_ref, v_ref, go_ref, d_ref, nm_ref, nv_ref):
        g = g_ref[...]
        go_ref[...] = g
        d_ref[...], nm_ref[...], nv_ref[...] = _adamw_math(w_ref[...], g, m_ref[...], v_ref[...])

    blk = pl.BlockSpec((tr, c), lambda i: (i, 0))
    return _pallas(body, name=name, grid=(r // tr,), out_shape=tuple(jax.ShapeDtypeStruct((r, c), F32) for _ in range(4)),
                   in_specs=[blk] * 4, out_specs=(blk,) * 4, compiler_params=_params("arbitrary"))(w, g, m, v)


_SMALL = [
    ("b_ada", ROW_DMOD, "mod"), ("norm_mix_g", ROW_GN1, "row"), ("conv_short_w", ROW_W3, "shard"),
    ("conv_conf_w", ROW_W31, "shard"), ("conv_conf_b", ROW_CB, "row"), ("conf_ln_g", ROW_LNG, "row"),
    ("conf_ln_b", ROW_LNB, "row"), ("norm_ffn_g", ROW_GN2, "row"), ("final_norm_g", ROW_GF, "row")]


def _small_update(p_all, chip_onehot, ws, ms, vs):
    n = len(_SMALL)
    d = p_all.shape[2]

    def body(*refs):
        p_ref, oh_ref = refs[:2]
        w_refs, m_refs, v_refs = refs[2:2 + n], refs[2 + n:2 + 2 * n], refs[2 + 2 * n:2 + 3 * n]
        loss_ref = refs[2 + 3 * n]
        g_out, d_out, m_out, v_out = [refs[3 + (3 + q) * n:3 + (4 + q) * n] for q in range(4)]
        g_all = p_ref[0]
        for s in range(1, N_DEV):
            g_all = g_all + p_ref[s]
        loss_ref[...] = jnp.zeros(loss_ref.shape, F32) + (0.5 / d) * jnp.sum(g_all[ROW_SQERR:ROW_SQERR + 1, :])

        def emit(i, idx, g):
            dl, nm, nv = _adamw_math(w_refs[i][idx], g, m_refs[i][idx], v_refs[i][idx])
            g_out[i][idx], d_out[i][idx], m_out[i][idx], v_out[i][idx] = g, dl, nm, nv

        everything = (slice(None), slice(None))
        for i, (_, row, kind) in enumerate(_SMALL):
            if kind == "mod":
                for k in range(N_MOD):
                    emit(i, (slice(0, 1), slice(k * d, (k + 1) * d)), g_all[row + k:row + k + 1, :])
            elif kind == "row":
                emit(i, everything, g_all[row:row + 1, :])
            else:
                _, taps, dq = w_refs[i].shape
                g = jnp.zeros((taps, dq), F32)
                for j in range(N_CHIP):
                    g = g + oh_ref[0:1, j:j + 1] * g_all[row:row + taps, j * dq:(j + 1) * dq]
                emit(i, (0, slice(None), slice(None)), g)

    shapes = [jax.ShapeDtypeStruct(w.shape, F32) for w in ws]
    res = _pallas(body, name="small_update", out_shape=tuple([jax.ShapeDtypeStruct((8, 128), F32)] + shapes * 4),
                  in_specs=[VMEM_FULL] * (2 + 3 * n), out_specs=tuple([VMEM_FULL] * (1 + 4 * n)),
                  compiler_params=_params())(p_all, chip_onehot, *ws, *ms, *vs)
    return res[0], [res[1 + q * n:1 + (q + 1) * n] for q in range(4)]


def _w_ada_grad(c_t, dmod_part):
    def body(c_ref, g_ref, o_ref):
        cv = c_ref[...]
        o_ref[...] = _dot((cv * _sig(cv)).astype(BF16), g_ref[...].astype(BF16))

    return _pallas(body, name="w_ada_grad", out_shape=jax.ShapeDtypeStruct((c_t.shape[0], dmod_part.shape[1]), F32),
                   in_specs=[VMEM_FULL] * 2, out_specs=VMEM_FULL, compiler_params=_params())(c_t, dmod_part)


def _step(x, c, w_ada, b_ada, norm_mix_g, w_in, conv_short_w, w_short_out, conv_conf_w, conv_conf_b, conf_ln_g,
          conf_ln_b, w_conf_out, w_o, norm_ffn_g, w_ffn_in, w_ffn_out, final_norm_g, loss_target, moments, ts):
    xi, yi, ci = _place()
    chip = 2 * xi + yi
    me = 4 * xi + 2 * yi + ci
    x2d, tgt = x[0], loss_target[0]
    s_len, d = x2d.shape
    dq = d // N_CHIP
    mq = N_MOD * d // N_CHIP
    gf = final_norm_g.reshape(1, d)

    taps = jnp.concatenate([jnp.pad(conv_short_w[0], ((0, 8 - SHORT_K), (0, 0))),
                            jnp.pad(conv_conf_w[0], ((0, 32 - CONF_K), (0, 0)))], axis=0)
    n_tr = 40 * dq // d
    c_taps = jnp.concatenate([jnp.broadcast_to(c, (8, d)), taps.reshape(n_tr, d), jnp.zeros((16 - n_tr, d), F32)], axis=0)
    b_part = lax.dynamic_slice(b_ada, (0, chip * mq), (1, mq))
    big = [w_in[0], w_short_out[0], w_conf_out[0], w_o[0], w_ffn_in[0], w_ffn_out[0]]
    axes = [1, 0, 0, 0, 1, 0]
    big_names = ["w_in", "w_short_out", "w_conf_out", "w_o", "w_ffn_in", "w_ffn_out"]
    (win_f, wa_f, wb_f, wo_f, *placed_ffn), c_taps, mod_all = _gather_weights(big, axes, 4, c_taps, w_ada[0], b_part,
                                                                              "gather_weights")
    c_taps = c_taps.reshape(N_DEV, 24, d)
    c_all = c_taps[:, 0]
    mod_all = mod_all.reshape(N_CHIP, 2, N_DEV, mq)[:, 0]
    mod_full = jnp.transpose(mod_all, (1, 0, 2)).reshape(N_DEV, N_MOD * d)
    mod = lax.dynamic_slice(mod_full, (me, 0), (1, N_MOD * d))

    taps_all = c_taps[:, 8:8 + n_tr].reshape(N_CHIP, 2, 40, dq)[:, 0]
    taps_full = jnp.transpose(taps_all, (1, 0, 2)).reshape(40, d)
    w3, w31 = taps_full[0:SHORT_K], taps_full[8:8 + CONF_K]

    (proj, h, lhs3, c3, nrm, rstd, ya, yb, mix, x1), (wfi_f, wfo_f) = _fwd_mix(
        x2d, mod, norm_mix_g, w3, w31, conv_conf_b, conf_ln_g, conf_ln_b, win_f, wa_f, wb_f, wo_f, placed_ffn, axes[4:], ts)
    h2, hid, df, dab, dx1, acc_f = _ffn(x1, tgt, mod, norm_ffn_g, gf, wfi_f, wfo_f, ts)
    f = wfo_f.shape[0]
    bn = min(512, d)
    g_ffn = [_weight_grad(h2, dab, bn, "grad_w_ffn_in")[0], _weight_grad(df, hid, bn // 2, "grad_w_ffn_out", transposed=True)[0]]
    (dproj, rhs3, acc_m), r_ffn = _bwd_mix(dx1, proj, nrm, rstd, c3, ya, yb, mix, mod, w3, w31, conf_ln_g, conf_ln_b,
                                          wa_f, wb_f, wo_f, g_ffn, axes[4:], ts)
    g3 = _weight_grad(lhs3, rhs3, bn // 2, "grad_w_sq", a_cols=d)[0]
    g_sq = [g3[:, k * d:(k + 1) * d] for k in range(3)]
    g_in, r_sq = _weight_grad(h, dproj, bn, "grad_w_in", ready=g_sq, ready_axes=axes[1:4])
    grad_x, acc_i, r_in = _bwd_in(dproj, x2d, dx1, mod, norm_mix_g, win_f, [g_in], axes[:1], min(2 * ts, s_len))
    g_in = [g_in]
    r_in_sq = list(r_in) + list(r_sq)
    grads = g_in + g_sq + g_ffn
    recv = list(r_in_sq) + list(r_ffn)
    g_big, p_all = _swap_halves([_sum_slots(r, g, ax, chip, ci, "sum_slots_" + n)
                                 for r, g, ax, n in zip(recv, grads, axes, big_names)], acc_f + acc_m + acc_i)
    p_all = p_all.reshape(N_DEV, N_SMALL_ROWS, d)

    small_w = dict(b_ada=b_ada, norm_mix_g=norm_mix_g, conv_short_w=conv_short_w, conv_conf_w=conv_conf_w,
                   conv_conf_b=conv_conf_b, conf_ln_g=conf_ln_g, conf_ln_b=conf_ln_b, norm_ffn_g=norm_ffn_g,
                   final_norm_g=final_norm_g)
    as_2d = lambda t: t if t.ndim == 3 else t.reshape((-1, t.shape[-1]))
    chip_onehot = (lax.broadcasted_iota(jnp.int32, (1, 128), 1) == chip).astype(F32)
    loss_tile, small_out = _small_update(
        p_all, chip_onehot, [as_2d(small_w[n]) for n, _, _ in _SMALL], [as_2d(moments["m_" + n]) for n, _, _ in _SMALL],
        [as_2d(moments["v_" + n]) for n, _, _ in _SMALL])
    small_g, small_d, small_m, small_v = [{n: o.reshape(small_w[n].shape) for (n, _, _), o in zip(_SMALL, outs)}
                                          for outs in small_out]

    dmod_all = p_all[:, 0:N_MOD].reshape(N_DEV, N_MOD * d)
    dmod_part = jnp.pad(lax.dynamic_slice(dmod_all, (0, chip * mq), (N_DEV, mq)), ((0, 128 - N_DEV), (0, 0)))
    c_t = jnp.pad(jnp.transpose(c_all), ((0, 0), (0, 128 - N_DEV)))
    g_ada = _w_ada_grad(c_t, dmod_part)

    out_g, out_d, out_m, out_v = dict(small_g), dict(small_d), dict(small_m), dict(small_v)
    for name, w, g in zip(["w_ada"] + big_names, [w_ada[0]] + big, [g_ada] + list(g_big)):
        go, dl, nm, nv = _adamw(w, g, moments["m_" + name][0], moments["v_" + name][0], "adamw_" + name)
        out_g[name], out_d[name], out_m[name], out_v[name] = go[None], dl[None], nm[None], nv[None]

    loss = loss_tile[0, 0]
    return loss, grad_x[None], out_g, out_d, out_m, out_v


_WEIGHTS = ["w_ada", "b_ada", "norm_mix_g", "w_in", "conv_short_w", "w_short_out", "conv_conf_w", "conv_conf_b",
            "conf_ln_g", "conf_ln_b", "w_conf_out", "w_o", "norm_ffn_g", "w_ffn_in", "w_ffn_out", "final_norm_g"]
ROW_TILE = 256


def kernel(x, c, w_ada, b_ada, norm_mix_g, w_in, conv_short_w, w_short_out, conv_conf_w, conv_conf_b, conf_ln_g, conf_ln_b, w_conf_out, w_o, norm_ffn_g, w_ffn_in, w_ffn_out, final_norm_g, loss_target, m_w_ada, m_b_ada, m_norm_mix_g, m_w_in, m_conv_short_w, m_w_short_out, m_conv_conf_w, m_conv_conf_b, m_conf_ln_g, m_conf_ln_b, m_w_conf_out, m_w_o, m_norm_ffn_g, m_w_ffn_in, m_w_ffn_out, m_final_norm_g, v_w_ada, v_b_ada, v_norm_mix_g, v_w_in, v_conv_short_w, v_w_short_out, v_conv_conf_w, v_conv_conf_b, v_conf_ln_g, v_conf_ln_b, v_w_conf_out, v_w_o, v_norm_ffn_g, v_w_ffn_in, v_w_ffn_out, v_final_norm_g):
    moments = dict(
        m_w_ada=m_w_ada, m_b_ada=m_b_ada, m_norm_mix_g=m_norm_mix_g, m_w_in=m_w_in, m_conv_short_w=m_conv_short_w,
        m_w_short_out=m_w_short_out, m_conv_conf_w=m_conv_conf_w, m_conv_conf_b=m_conv_conf_b, m_conf_ln_g=m_conf_ln_g,
        m_conf_ln_b=m_conf_ln_b, m_w_conf_out=m_w_conf_out, m_w_o=m_w_o, m_norm_ffn_g=m_norm_ffn_g, m_w_ffn_in=m_w_ffn_in,
        m_w_ffn_out=m_w_ffn_out, m_final_norm_g=m_final_norm_g,
        v_w_ada=v_w_ada, v_b_ada=v_b_ada, v_norm_mix_g=v_norm_mix_g, v_w_in=v_w_in, v_conv_short_w=v_conv_short_w,
        v_w_short_out=v_w_short_out, v_conv_conf_w=v_conv_conf_w, v_conv_conf_b=v_conv_conf_b, v_conf_ln_g=v_conf_ln_g,
        v_conf_ln_b=v_conf_ln_b, v_w_conf_out=v_w_conf_out, v_w_o=v_w_o, v_norm_ffn_g=v_norm_ffn_g, v_w_ffn_in=v_w_ffn_in,
        v_w_ffn_out=v_w_ffn_out, v_final_norm_g=v_final_norm_g)
    loss, grad_x, g, dl, nm, nv = _step(
        x, c, w_ada, b_ada, norm_mix_g, w_in, conv_short_w, w_short_out, conv_conf_w, conv_conf_b, conf_ln_g, conf_ln_b,
        w_conf_out, w_o, norm_ffn_g, w_ffn_in, w_ffn_out, final_norm_g, loss_target, moments, min(ROW_TILE, x.shape[1]))
    return (loss, grad_x, *[g[n] for n in _WEIGHTS], *[dl[n] for n in _WEIGHTS], *[nm[n] for n in _WEIGHTS],
            *[nv[n] for n in _WEIGHTS])
```

```python
import jax
import jax.numpy as jnp
from jax import lax
from jax.experimental import pallas as pl
from jax.experimental.pallas import tpu as pltpu

F32 = jnp.float32
BF16 = jnp.bfloat16
EPS = 1e-6
LN_EPS = 1e-5
SHORT_K = 3
CONF_K = 31
N_MOD = 6
N_DEV = 8
N_CHIP = 4
ADAM_LR = 0.001
ADAM_B1 = 0.9
ADAM_B2 = 0.999
ADAM_EPS = 1e-08
ADAM_WD = 0.01
ADAM_STEP = 10
MESH = pl.DeviceIdType.MESH
V7X_VMEM_LIMIT_BYTES = 60 * 1024 * 1024
LANES = 128
HALO_SHORT = 8
HALO_CONF = 32
N_SMALL_ROWS = 48
ROW_DMOD, ROW_GN1, ROW_W3, ROW_W31, ROW_CB, ROW_LNG, ROW_LNB, ROW_GN2, ROW_GF, ROW_SQERR = 0, 6, 7, 10, 41, 42, 43, 44, 45, 46
CONV_ROWS = 64
CONV_LANES = 256

ANY = pl.BlockSpec(memory_space=pl.ANY)
VMEM_FULL = pl.BlockSpec(memory_space=pltpu.VMEM)


def _pallas(body, **kw):
    return pl.pallas_call(body, **kw)


def _params(*sem):
    return pltpu.CompilerParams(dimension_semantics=sem, vmem_limit_bytes=V7X_VMEM_LIMIT_BYTES)


def _sig(z):
    return jax.nn.sigmoid(z)


def _dot(a, b):
    return jnp.dot(a, b, preferred_element_type=F32)


def _dot_nt(a, b):
    return lax.dot_general(a, b, (((1,), (1,)), ((), ())), preferred_element_type=F32)


def _dot_tn(a, b):
    return lax.dot_general(a, b, (((0,), (0,)), ((), ())), preferred_element_type=F32)


def _colsum(z):
    return jnp.sum(z, axis=0, keepdims=True)


def _place():
    return lax.axis_index("x"), lax.axis_index("y"), lax.axis_index("c")


def _small_gather_copies(x_ref, out_ref, send_sems, recv_sems, local_sem):
    m_per = x_ref.shape[0]
    x, y, c = _place()
    me, sibling = (x, y, c), (x, y, 1 - c)
    chips = [(1 - x, y), (x, 1 - y), (1 - x, 1 - y)]

    def rows(px, py, pc):
        return out_ref.at[pl.ds((4 * px + 2 * py + pc) * m_per, m_per), :]

    def copy(k, blk, to, src=None):
        return pltpu.make_async_remote_copy(
            src_ref=rows(*blk) if src is None else src, dst_ref=rows(*blk),
            send_sem=send_sems.at[k], recv_sem=recv_sems.at[k], device_id=to, device_id_type=MESH)

    mine = pltpu.make_async_copy(x_ref, rows(*me), local_sem)
    first = [copy(0, me, sibling, src=x_ref)] + [copy(1 + j, me, (*chip, c), src=x_ref) for j, chip in enumerate(chips)]
    return me, sibling, chips, c, copy, mine, first


def _small_gather_start(x_ref, out_ref, send_sems, recv_sems, local_sem):
    _, _, _, _, _, mine, first = _small_gather_copies(x_ref, out_ref, send_sems, recv_sems, local_sem)
    mine.start()
    for cp in first:
        cp.start()


def _small_gather_finish(x_ref, out_ref, send_sems, recv_sems, local_sem):
    me, sibling, chips, c, copy, mine, first = _small_gather_copies(x_ref, out_ref, send_sems, recv_sems, local_sem)
    passed = [copy(4 + j, (*chip, c), sibling) for j, chip in enumerate(chips)]
    for j, chip in enumerate(chips):
        copy(1 + j, (*chip, c), me).wait_recv()
        passed[j].start()
    copy(0, sibling, me).wait_recv()
    for j, chip in enumerate(chips):
        copy(4 + j, (*chip, 1 - c), me).wait_recv()
    for cp in first + passed:
        cp.wait_send()
    mine.wait()


def _small_gather(x_ref, out_ref, *sems):
    _small_gather_start(x_ref, out_ref, *sems)
    _small_gather_finish(x_ref, out_ref, *sems)


def _small_gather_sems():
    return [pltpu.SemaphoreType.DMA((7,)), pltpu.SemaphoreType.DMA((7,)), pltpu.SemaphoreType.DMA]


def _shard_slice(ref, full_shape, shard_axis, chip, half, quarter=None):
    m, n = full_shape
    r = (m // N_CHIP if shard_axis == 0 else m) // 2
    start = half * r + (chip * 2 * r if shard_axis == 0 else 0)
    if quarter is not None:
        r = r // 2
        start = start + quarter * r
    if shard_axis == 0:
        return ref.at[pl.ds(start, r), :]
    cw = n // N_CHIP
    return ref.at[pl.ds(start, r), pl.ds(pl.multiple_of(chip * cw, 128), cw)]


def _gather_copy(ref, region, k, a, to, sems):
    return pltpu.make_async_remote_copy(src_ref=region, dst_ref=region, send_sem=sems[0].at[a, k], recv_sem=sems[1].at[a, k],
                                        device_id=to, device_id_type=MESH)


def _gather_send(refs, shapes, axes, *sems):
    x, y, c = _place()
    for a in range(len(refs)):
        mine = _shard_slice(refs[a], shapes[a], axes[a], 2 * x + y, c)
        _gather_copy(refs[a], mine, 0, a, (1 - x, y, c), sems).start()
        _gather_copy(refs[a], mine, 1, a, (x, 1 - y, c), sems).start()


def _gather_forward(refs, shapes, axes, *sems):
    x, y, c = _place()
    me, sibling, x_nbr, y_nbr = (x, y, c), (x, y, 1 - c), (1 - x, y, c), (x, 1 - y, c)
    chip_x, chip_y, chip_d = 2 * (1 - x) + y, 2 * x + (1 - y), 2 * (1 - x) + (1 - y)
    for a in range(len(refs)):
        region = lambda chip, quarter=None, a=a: _shard_slice(refs[a], shapes[a], axes[a], chip, c, quarter)
        _gather_copy(refs[a], region(chip_x), 0, a, me, sems).wait_recv()
        _gather_copy(refs[a], region(chip_x, 1), 3, a, y_nbr, sems).start()
        _gather_copy(refs[a], region(chip_x), 4, a, sibling, sems).start()
        _gather_copy(refs[a], region(chip_y), 1, a, me, sems).wait_recv()
        _gather_copy(refs[a], region(chip_y, 0), 2, a, x_nbr, sems).start()
        _gather_copy(refs[a], region(chip_y), 5, a, sibling, sems).start()


def _gather_forward_diagonal(refs, shapes, axes, *sems):
    x, y, c = _place()
    me, sibling, chip_d = (x, y, c), (x, y, 1 - c), 2 * (1 - x) + (1 - y)
    for a in range(len(refs)):
        region = lambda chip, quarter=None, a=a: _shard_slice(refs[a], shapes[a], axes[a], chip, c, quarter)
        _gather_copy(refs[a], region(chip_d, 0), 2, a, me, sems).wait_recv()
        _gather_copy(refs[a], region(chip_d, 1), 3, a, me, sems).wait_recv()
        _gather_copy(refs[a], region(chip_d), 6, a, sibling, sems).start()


def _gather_finish(refs, shapes, axes, *sems):
    x, y, c = _place()
    me = (x, y, c)
    chip_me, chip_x, chip_y, chip_d = 2 * x + y, 2 * (1 - x) + y, 2 * x + (1 - y), 2 * (1 - x) + (1 - y)
    for a in range(len(refs)):
        region = lambda chip, half, quarter=None, a=a: _shard_slice(refs[a], shapes[a], axes[a], chip, half, quarter)
        for k, chip in ((4, chip_x), (5, chip_y), (6, chip_d)):
            _gather_copy(refs[a], region(chip, 1 - c), k, a, me, sems).wait_recv()
            _gather_copy(refs[a], region(chip, c), k, a, me, sems).wait_send()
        _gather_copy(refs[a], region(chip_me, c), 0, a, me, sems).wait_send()
        _gather_copy(refs[a], region(chip_me, c), 1, a, me, sems).wait_send()
        _gather_copy(refs[a], region(chip_y, c, 0), 2, a, me, sems).wait_send()
        _gather_copy(refs[a], region(chip_x, c, 1), 3, a, me, sems).wait_send()


def _gather_sems(n_w):
    return [pltpu.SemaphoreType.DMA((n_w, 7)), pltpu.SemaphoreType.DMA((n_w, 7))]


def _gather_weights(shards, axes, n_gather, c_taps, w_ada, b_part, name):
    n_w = len(shards)
    shapes = [(w.shape[0] * N_CHIP, w.shape[1]) if ax == 0 else (w.shape[0], w.shape[1] * N_CHIP) for w, ax in zip(shards, axes)]
    r_ct = c_taps.shape[0]
    mq = w_ada.shape[1]

    def body(*refs):
        ins = refs[:n_w]
        ct_ref, wada_ref, b_ref = refs[n_w:n_w + 3]
        outs = refs[n_w + 3:2 * n_w + 3]
        ct_all_ref, mod_all_ref = refs[2 * n_w + 3:2 * n_w + 5]
        scratch = refs[2 * n_w + 5:]
        stage32, stage16, mod_part = scratch[:n_w], scratch[n_w:2 * n_w], scratch[2 * n_w]
        load_sems, store_sems = scratch[2 * n_w + 1:2 * n_w + 3]
        sems, ct_sems, mod_sems = scratch[2 * n_w + 3:2 * n_w + 5], scratch[2 * n_w + 5:2 * n_w + 8], scratch[2 * n_w + 8:]
        x, y, c = _place()
        chip = 2 * x + y
        loads = [pltpu.make_async_copy(ins[a], stage32[a], load_sems.at[a]) for a in range(n_w)]
        for cp in loads:
            cp.start()
        _small_gather_start(ct_ref, ct_all_ref, *ct_sems)
        stores = []
        for a in range(n_w):
            loads[a].wait()
            stage16[a][...] = stage32[a][...].astype(BF16)
            r, cw = shards[a].shape
            own = outs[a].at[pl.ds(chip * r, r), :] if axes[a] == 0 else outs[a].at[:, pl.ds(pl.multiple_of(chip * cw, 128), cw)]
            stores.append(pltpu.make_async_copy(stage16[a], own, store_sems.at[a]))
            stores[-1].start()
            if a < n_gather:
                half = stage16[a].at[pl.ds(c * (r // 2), r // 2), :]
                there = _shard_slice(outs[a], shapes[a], axes[a], chip, c)
                for k, to in ((0, (1 - x, y, c)), (1, (x, 1 - y, c))):
                    pltpu.make_async_remote_copy(src_ref=half, dst_ref=there, send_sem=sems[0].at[a, k],
                                                 recv_sem=sems[1].at[a, k], device_id=to, device_id_type=MESH).start()
        _small_gather_finish(ct_ref, ct_all_ref, *ct_sems)
        cv = jnp.concatenate([ct_all_ref[r_ct * dv:r_ct * dv + 1, :] for dv in range(N_DEV)], axis=0)
        mod_part[...] = _dot((cv * _sig(cv)).astype(BF16), wada_ref[...].astype(BF16)) + b_ref[...]
        _small_gather_start(mod_part, mod_all_ref, *mod_sems)
        _gather_forward(outs[:n_gather], shapes, axes, *sems)
        _small_gather_finish(mod_part, mod_all_ref, *mod_sems)
        _gather_forward_diagonal(outs[:n_gather], shapes, axes, *sems)
        _gather_finish(outs[:n_gather], shapes, axes, *sems)
        for cp in stores:
            cp.wait()

    res = _pallas(
        body, name=name,
        out_shape=tuple(jax.ShapeDtypeStruct(sh, BF16) for sh in shapes)
        + (jax.ShapeDtypeStruct((N_DEV * r_ct, c_taps.shape[1]), F32), jax.ShapeDtypeStruct((N_DEV * N_DEV, mq), F32)),
        in_specs=[ANY] * n_w + [VMEM_FULL] * 3, out_specs=tuple([ANY] * n_w + [VMEM_FULL] * 2),
        scratch_shapes=[pltpu.VMEM(w.shape, F32) for w in shards] + [pltpu.VMEM(w.shape, BF16) for w in shards]
        + [pltpu.VMEM((N_DEV, mq), F32), pltpu.SemaphoreType.DMA((n_w,)), pltpu.SemaphoreType.DMA((n_w,))]
        + _gather_sems(n_gather) + _small_gather_sems() + _small_gather_sems(),
        compiler_params=_params(),
    )(*shards, c_taps, w_ada, b_part)
    return res[:n_w], res[n_w], res[n_w + 1]


def _piece_shapes(grads, axes):
    return [((g.shape[0] // N_CHIP // 2, g.shape[1]) if ax == 0 else (g.shape[0] // 2, g.shape[1] // N_CHIP))
            for g, ax in zip(grads, axes)]


def _exchange_send(ins, outs, shapes, axes, send_sems, recv_sems):
    x, y, c = _place()
    me = 4 * x + 2 * y + c
    for a in range(len(ins)):
        for t in range(N_DEV):
            tx, ty, tc = t // 4, (t // 2) % 2, t % 2
            src = _shard_slice(ins[a], shapes[a], axes[a], 2 * tx + ty, tc)

            @pl.when(me != t)
            def _(src=src, a=a, t=t, to=(tx, ty, tc)):
                slot = (me - t + N_DEV) % N_DEV - 1
                pltpu.make_async_remote_copy(src_ref=src, dst_ref=outs[a].at[slot], send_sem=send_sems.at[a, t],
                                             recv_sem=recv_sems.at[a, slot], device_id=to, device_id_type=MESH).start()


def _exchange_finish(outs, send_sems, recv_sems):
    x, y, c = _place()
    me = 4 * x + 2 * y + c
    for a in range(len(outs)):
        for t in range(N_DEV):
            @pl.when(me != t)
            def _(a=a, t=t):
                slot = outs[a].at[0]
                pltpu.make_async_remote_copy(src_ref=slot, dst_ref=slot, send_sem=send_sems.at[a, t], recv_sem=recv_sems.at[a, 0],
                                             device_id=(x, y, c), device_id_type=MESH).wait_send()
        for k in range(N_DEV - 1):
            slot = outs[a].at[k]
            pltpu.make_async_remote_copy(src_ref=slot, dst_ref=slot, send_sem=send_sems.at[a, 0], recv_sem=recv_sems.at[a, k],
                                         device_id=(x, y, c), device_id_type=MESH).wait_recv()


def _exchange_sems(n_w):
    return [pltpu.SemaphoreType.DMA((n_w, N_DEV)), pltpu.SemaphoreType.DMA((n_w, N_DEV - 1))]


def _exchange_out_shapes(grads, axes):
    return tuple(jax.ShapeDtypeStruct((N_DEV - 1,) + p, BF16) for p in _piece_shapes(grads, axes))


def _swap_halves(fulls, block):
    n_w = len(fulls)

    def body(*refs):
        block_ref = refs[n_w]
        outs, gathered_ref = refs[n_w + 1:2 * n_w + 1], refs[2 * n_w + 1]
        send_sems, recv_sems = refs[2 * n_w + 2:2 * n_w + 4]
        x, y, c = _place()
        sibling = (x, y, 1 - c)
        waits = []
        for a in range(n_w):
            r = fulls[a].shape[0] // 2
            mine = outs[a].at[pl.ds(c * r, r), :]
            theirs = outs[a].at[pl.ds((1 - c) * r, r), :]
            rc = pltpu.make_async_remote_copy(src_ref=mine, dst_ref=mine, send_sem=send_sems.at[a], recv_sem=recv_sems.at[a],
                                              device_id=sibling, device_id_type=MESH)
            rc.start()
            got = pltpu.make_async_remote_copy(src_ref=theirs, dst_ref=theirs, send_sem=send_sems.at[a],
                                               recv_sem=recv_sems.at[a], device_id=sibling, device_id_type=MESH)
            waits += [rc.wait_send, got.wait_recv]
        _small_gather(block_ref, gathered_ref, *refs[2 * n_w + 4:])
        for w in waits:
            w()

    res = _pallas(
        body, name="swap_halves",
        out_shape=tuple(jax.ShapeDtypeStruct(f.shape, f.dtype) for f in fulls)
        + (jax.ShapeDtypeStruct((N_DEV * block.shape[0], block.shape[1]), block.dtype),),
        in_specs=[ANY] * n_w + [VMEM_FULL], out_specs=tuple([ANY] * n_w + [VMEM_FULL]),
        input_output_aliases={a: a for a in range(n_w)},
        scratch_shapes=[pltpu.SemaphoreType.DMA((n_w,)), pltpu.SemaphoreType.DMA((n_w,))] + _small_gather_sems(),
    )(*fulls, block)
    return res[:n_w], res[n_w]


def _load_weights(pairs, sems):
    @pl.when(pl.program_id(0) == 0)
    def _():
        cps = [pltpu.make_async_copy(s, d, sems.at[k]) for k, (s, d) in enumerate(pairs)]
        for cp in cps:
            cp.start()
        for cp in cps:
            cp.wait()


def _conv_chunks(ext, e_buf, ts, d, chunk_fn, unroll, between=()):
    rows, dh = e_buf.shape[1], e_buf.shape[2]
    for hf in range(d // dh):
        for r in range(1, 8):
            e_buf[r - 1, :, :] = ext[r:r + rows, hf * dh:(hf + 1) * dh]
        for lq in range(dh // CONV_LANES):
            lo, el = hf * dh + lq * CONV_LANES, lq * CONV_LANES

            def body(ci, carry, lo=lo, el=el):
                r0 = pl.multiple_of(ci * CONV_ROWS, CONV_ROWS)

                def tap(off):
                    q, r = divmod(off, 8)
                    if r == 0:
                        return ext[pl.ds(r0 + 8 * q, CONV_ROWS), lo:lo + CONV_LANES]
                    return e_buf[r - 1, pl.ds(r0 + 8 * q, CONV_ROWS), el:el + CONV_LANES]

                chunk_fn(tap, r0, lo)
                return carry

            lax.fori_loop(0, ts // CONV_ROWS, body, 0, unroll=unroll)
            if between:
                between[0]()
                between = between[1:]
    for fn in between:
        fn()


def _fwd_mix(x, mod, gn1, w3, w31, cbias, lng, lnb, w_in, w_a, w_b, w_o, later, later_axes, ts):
    s_len, d = x.shape
    n_t = s_len // ts
    n_l = len(later)
    later_shapes = [w.shape for w in later]

    def body(*refs):
        (x_ref, mod_ref, gn1_ref, w3_ref, w31_ref, cb_ref, lng_ref, lnb_ref, win_hbm, wa_hbm, wb_hbm, wo_hbm) = refs[:12]
        (proj_ref, h_ref, lhs3_ref, c3_ref, n_ref, rstd_ref, ya_ref, yb_ref, mix_ref, x1_ref) = refs[12 + n_l:22 + n_l]
        later_refs = refs[22 + n_l:22 + 2 * n_l]
        win_v, wa_v, wb_v, wo_v, cv_ext, u0_ext, e_buf, u1_buf, sems, g_send, g_recv = refs[22 + 2 * n_l:]
        step = pl.program_id(0)
        pl.when(step == 0)(lambda: _gather_send(later_refs, later_shapes, later_axes, g_send, g_recv))
        pl.when(step == n_t // 2)(lambda: _gather_forward(later_refs, later_shapes, later_axes, g_send, g_recv))
        pl.when(step == (3 * n_t) // 4)(lambda: _gather_forward_diagonal(later_refs, later_shapes, later_axes, g_send, g_recv))
        pl.when(step == n_t - 1)(lambda: _gather_finish(later_refs, later_shapes, later_axes, g_send, g_recv))
        _load_weights([(win_hbm, win_v), (wa_hbm, wa_v), (wb_hbm, wb_v), (wo_hbm, wo_v)], sems)

        @pl.when(pl.program_id(0) == 0)
        def _():
            cv_ext[0:HALO_SHORT, :] = jnp.zeros((HALO_SHORT, d), F32)
            u0_ext[0:HALO_CONF, :] = jnp.zeros((HALO_CONF, d), F32)

        xv = x_ref[...]
        sh1, sc1, g1 = mod_ref[:, 0:d], mod_ref[:, d:2 * d], mod_ref[:, 2 * d:3 * d]
        r = lax.rsqrt(jnp.mean(xv * xv, axis=-1, keepdims=True) + EPS)
        hb = (((xv * r) * gn1_ref[...]) * (1.0 + sc1) + sh1).astype(BF16)
        h_ref[...] = hb

        def pj(k):
            p = _dot(hb, win_v[:, k * d:(k + 1) * d])
            proj_ref[:, k * d:(k + 1) * d] = p.astype(BF16)
            return p

        v_c = pj(3)
        u0_ext[HALO_CONF:HALO_CONF + ts, :] = v_c * _sig(pj(4))

        def conv_chunk(tap, r0, lo):
            acc = jnp.zeros((CONV_ROWS, CONV_LANES), F32)
            for k in range(CONF_K):
                acc = acc + w31_ref[k:k + 1, lo:lo + CONV_LANES] * tap(HALO_CONF - (CONF_K - 1) + k)
            u1_buf[pl.ds(r0, CONV_ROWS), lo:lo + CONV_LANES] = acc

        pjs = {}
        _conv_chunks(u0_ext, e_buf, ts, d, conv_chunk, unroll=True,
                     between=[lambda k=k: pjs.__setitem__(k, pj(k)) for k in (1, 2, 0, 5, 6)])
        u0_ext[0:HALO_CONF, :] = u0_ext[ts:ts + HALO_CONF, :]

        cv_ext[HALO_SHORT:HALO_SHORT + ts, :] = pjs[1] * pjs[2]
        conv3 = jnp.zeros((ts, d), F32)
        for k in range(SHORT_K):
            off = HALO_SHORT - (SHORT_K - 1) + k
            conv3 = conv3 + w3_ref[k:k + 1, :] * cv_ext[off:off + ts, :]
        cv_ext[0:HALO_SHORT, :] = cv_ext[ts:ts + HALO_SHORT, :]
        c3_ref[...] = conv3.astype(BF16)
        yap = (pjs[0] * conv3).astype(BF16)
        lhs3_ref[:, 0:d] = yap
        y_a = _dot(yap, wa_v[...])
        ya_ref[...] = y_a.astype(BF16)

        u1 = u1_buf[...] + cb_ref[...]
        mu = jnp.mean(u1, axis=-1, keepdims=True)
        var = jnp.mean(jnp.square(u1 - mu), axis=-1, keepdims=True)
        rstd = lax.rsqrt(var + LN_EPS)
        nrm = (u1 - mu) * rstd
        n_ref[...] = nrm.astype(BF16)
        rstd_ref[...] = jnp.broadcast_to(rstd, (ts, LANES))
        u2 = nrm * lng_ref[...] + lnb_ref[...]
        u3 = (u2 * _sig(u2)).astype(BF16)
        lhs3_ref[:, d:2 * d] = u3
        y_b = _dot(u3, wb_v[...])
        yb_ref[...] = y_b.astype(BF16)

        merged = (_sig(pjs[5]) * y_a + _sig(pjs[6]) * y_b).astype(BF16)
        lhs3_ref[:, 2 * d:3 * d] = merged
        mix = _dot(merged, wo_v[...])
        mix_ref[...] = mix
        x1_ref[...] = xv + g1 * mix

    row = lambda w, dt=None: pl.BlockSpec((ts, w), lambda i: (i, 0))
    full = lambda a: pl.BlockSpec(a.shape, lambda i: (0, 0))
    sd = lambda w, dt: jax.ShapeDtypeStruct((s_len, w), dt)
    outs = [sd(7 * d, BF16), sd(d, BF16), sd(3 * d, BF16), sd(d, BF16), sd(d, BF16), sd(LANES, F32), sd(d, BF16), sd(d, BF16),
            sd(d, F32), sd(d, F32)]
    res = _pallas(
        body, name="fwd_mix", grid=(n_t,),
        out_shape=tuple(outs) + tuple(jax.ShapeDtypeStruct(w.shape, BF16) for w in later),
        in_specs=[row(d), full(mod), full(gn1), full(w3), full(w31), full(cbias), full(lng), full(lnb), ANY, ANY, ANY, ANY]
        + [ANY] * n_l,
        out_specs=tuple(row(o.shape[1]) for o in outs) + tuple([ANY] * n_l),
        input_output_aliases={12 + k: len(outs) + k for k in range(n_l)},
        scratch_shapes=[pltpu.VMEM(w_in.shape, BF16), pltpu.VMEM(w_a.shape, BF16), pltpu.VMEM(w_b.shape, BF16),
                        pltpu.VMEM(w_o.shape, BF16), pltpu.VMEM((ts + HALO_SHORT, d), F32),
                        pltpu.VMEM((ts + HALO_CONF, d), F32), pltpu.VMEM((7, ts + HALO_CONF - 8, d // 2), F32),
                        pltpu.VMEM((ts, d), F32), pltpu.SemaphoreType.DMA((4,))] + _gather_sems(n_l),
        compiler_params=_params("arbitrary"),
    )(x, mod, gn1, w3, w31, cbias, lng, lnb, w_in, w_a, w_b, w_o, *later)
    return res[:len(outs)], res[len(outs):]


def _ffn(x1, target, mod, gn2, gf, w_fi, w_fo, ts):
    s_len, d = x1.shape
    f = w_fo.shape[0]
    n_t = s_len // ts

    def body(x1_ref, t_ref, mod_ref, gn2_ref, gf_ref, wfi_hbm, wfo_hbm,
             h2_ref, hid_ref, df_ref, dab_ref, dx1_ref, acc_ref, wfi_v, wfo_v, sems):
        _load_weights([(wfi_hbm, wfi_v), (wfo_hbm, wfo_v)], sems)

        @pl.when(pl.program_id(0) == 0)
        def _():
            acc_ref[...] = jnp.zeros(acc_ref.shape, F32)

        x1v = x1_ref[...]
        sh2, sc2, g2 = mod_ref[:, 3 * d:4 * d], mod_ref[:, 4 * d:5 * d], mod_ref[:, 5 * d:6 * d]
        gn2v, gfv = gn2_ref[...], gf_ref[...]
        r2 = lax.rsqrt(jnp.mean(x1v * x1v, axis=-1, keepdims=True) + EPS)
        xn2 = x1v * r2
        h2 = ((xn2 * gn2v) * (1.0 + sc2) + sh2).astype(BF16)
        h2_ref[...] = h2
        a = _dot(h2, wfi_v[:, 0:f])
        bg = _dot(h2, wfi_v[:, f:2 * f])
        sa = _sig(a)
        silu_a = a * sa
        hid = (silu_a * bg).astype(BF16)
        hid_ref[...] = hid
        ffn = _dot(hid, wfo_v[...])
        x2 = x1v + g2 * ffn
        r3 = lax.rsqrt(jnp.mean(x2 * x2, axis=-1, keepdims=True) + EPS)
        xn3 = x2 * r3
        diff = xn3 * gfv - t_ref[...]
        dy = diff * (1.0 / d)
        dxn3 = dy * gfv
        dx2 = r3 * (dxn3 - xn3 * jnp.mean(dxn3 * xn3, axis=-1, keepdims=True))
        acc_ref[ROW_GF:ROW_GF + 1, :] += _colsum(dy * xn3)
        acc_ref[ROW_DMOD + 5:ROW_DMOD + 5 + 1, :] += _colsum(dx2 * ffn)
        acc_ref[ROW_SQERR:ROW_SQERR + 1, :] += _colsum(diff * diff)

        df = (dx2 * g2).astype(BF16)
        df_ref[...] = df
        dhid = _dot_nt(df, wfo_v[...])
        da = dhid * bg * (sa * (1.0 + a * (1.0 - sa)))
        dbg = dhid * silu_a
        dab_ref[:, 0:f] = da.astype(BF16)
        dab_ref[:, f:2 * f] = dbg.astype(BF16)
        dh2 = _dot_nt(da.astype(BF16), wfi_v[:, 0:f]) + _dot_nt(dbg.astype(BF16), wfi_v[:, f:2 * f])
        acc_ref[ROW_DMOD + 3:ROW_DMOD + 3 + 1, :] += _colsum(dh2)
        acc_ref[ROW_DMOD + 4:ROW_DMOD + 4 + 1, :] += _colsum(dh2 * (xn2 * gn2v))
        acc_ref[ROW_GN2:ROW_GN2 + 1, :] += _colsum(dh2 * xn2 * (1.0 + sc2))
        dxn2 = dh2 * gn2v * (1.0 + sc2)
        dx1_ref[...] = dx2 + r2 * (dxn2 - xn2 * jnp.mean(dxn2 * xn2, axis=-1, keepdims=True))

    row = lambda w: pl.BlockSpec((ts, w), lambda i: (i, 0))
    full = lambda a: pl.BlockSpec(a.shape, lambda i: (0, 0))
    sd = lambda w, dt: jax.ShapeDtypeStruct((s_len, w), dt)
    outs = [sd(d, BF16), sd(f, BF16), sd(d, BF16), sd(2 * f, BF16), sd(d, F32), jax.ShapeDtypeStruct((N_SMALL_ROWS, d), F32)]
    return _pallas(
        body, name="ffn", grid=(n_t,), out_shape=tuple(outs),
        in_specs=[row(d), row(d), full(mod), full(gn2), full(gf), ANY, ANY],
        out_specs=tuple([row(o.shape[1]) for o in outs[:-1]] + [pl.BlockSpec((N_SMALL_ROWS, d), lambda i: (0, 0))]),
        scratch_shapes=[pltpu.VMEM(w_fi.shape, BF16), pltpu.VMEM(w_fo.shape, BF16), pltpu.SemaphoreType.DMA((2,))],
        compiler_params=_params("arbitrary"),
    )(x1, target, mod, gn2, gf, w_fi, w_fo)


def _bwd_mix(dx1, proj, nrm, rstd, c3, ya, yb, mix, mod, w3, w31, lng, lnb, w_a, w_b, w_o, ready, ready_axes, ts):
    s_len, d = dx1.shape
    n_t = s_len // ts
    n_r = len(ready)
    ready_shapes = [g.shape for g in ready]

    def body(*refs):
        (dx1_ref, proj_ref, n_ref, rstd_ref, c3_ref, ya_ref, yb_ref, mix_ref, mod_ref, w3_ref, w31_ref, lng_ref,
         lnb_ref, wa_hbm, wb_hbm, wo_hbm) = refs[:16]
        ready_refs = refs[16:16 + n_r]
        dproj_ref, rhs3_ref, acc_ref = refs[16 + n_r:19 + n_r]
        recv_refs = refs[19 + n_r:19 + 2 * n_r]
        wa_v, wb_v, wo_v, dc3_ext, du1_ext, e_buf, u0_buf, du0_buf, dw8, sems, e_send, e_recv = refs[19 + 2 * n_r:]
        step = pl.program_id(0)
        pl.when(step == 0)(lambda: _exchange_send(ready_refs, recv_refs, ready_shapes, ready_axes, e_send, e_recv))
        pl.when(step == n_t - 1)(lambda: _exchange_finish(recv_refs, e_send, e_recv))
        _load_weights([(wa_hbm, wa_v), (wb_hbm, wb_v), (wo_hbm, wo_v)], sems)

        @pl.when(pl.program_id(0) == 0)
        def _():
            acc_ref[...] = jnp.zeros(acc_ref.shape, F32)
            dc3_ext[ts:ts + HALO_SHORT, :] = jnp.zeros((HALO_SHORT, d), F32)
            du1_ext[ts:ts + HALO_CONF, :] = jnp.zeros((HALO_CONF, d), F32)
            dw8[...] = jnp.zeros(dw8.shape, F32)

        pj = lambda k: proj_ref[:, k * d:(k + 1) * d].astype(F32)
        g1 = mod_ref[:, 2 * d:3 * d]
        dx1v = dx1_ref[...]
        acc_ref[ROW_DMOD + 2:ROW_DMOD + 2 + 1, :] += _colsum(dx1v * mix_ref[...])
        dmix = (dx1v * g1).astype(BF16)
        rhs3_ref[:, 2 * d:3 * d] = dmix
        dmerged = _dot_nt(dmix, wo_v[...])
        g_a, g_b = _sig(pj(5)), _sig(pj(6))
        y_a, y_b = ya_ref[...].astype(F32), yb_ref[...].astype(F32)
        dproj_ref[:, 5 * d:6 * d] = (dmerged * y_a * g_a * (1.0 - g_a)).astype(BF16)
        dproj_ref[:, 6 * d:7 * d] = (dmerged * y_b * g_b * (1.0 - g_b)).astype(BF16)
        dya = (dmerged * g_a).astype(BF16)
        dyb = (dmerged * g_b).astype(BF16)
        rhs3_ref[:, 0:d] = dya
        rhs3_ref[:, d:2 * d] = dyb

        dyap = _dot_nt(dya, wa_v[...])
        dproj_ref[:, 0:d] = (dyap * c3_ref[...].astype(F32)).astype(BF16)
        dc3_ext[0:ts, :] = dyap * pj(0)
        c_s, v_s = pj(1), pj(2)
        cv = c_s * v_s
        dcv = jnp.zeros((ts, d), F32)
        for j in range(SHORT_K):
            k = SHORT_K - 1 - j
            sj = dc3_ext[j:j + ts, :]
            dcv = dcv + w3_ref[k:k + 1, :] * sj
            acc_ref[ROW_W3 + k:ROW_W3 + k + 1, :] += _colsum(cv * sj)
        dc3_ext[ts:ts + HALO_SHORT, :] = dc3_ext[0:HALO_SHORT, :]
        dproj_ref[:, d:2 * d] = (dcv * v_s).astype(BF16)
        dproj_ref[:, 2 * d:3 * d] = (dcv * c_s).astype(BF16)

        du3 = _dot_nt(dyb, wb_v[...])
        nv = n_ref[...].astype(F32)
        lg = lng_ref[...]
        u2 = nv * lg + lnb_ref[...]
        s2 = _sig(u2)
        du2 = du3 * (s2 * (1.0 + u2 * (1.0 - s2)))
        acc_ref[ROW_LNG:ROW_LNG + 1, :] += _colsum(du2 * nv)
        acc_ref[ROW_LNB:ROW_LNB + 1, :] += _colsum(du2)
        dn = du2 * lg
        du1 = rstd_ref[:, 0:1] * (dn - jnp.mean(dn, axis=-1, keepdims=True) - nv * jnp.mean(dn * nv, axis=-1, keepdims=True))
        acc_ref[ROW_CB:ROW_CB + 1, :] += _colsum(du1)
        du1_ext[0:ts, :] = du1
        v_c = pj(3)
        sg = _sig(pj(4))
        u0_buf[...] = v_c * sg

        def conv_chunk(tap, r0, lo):
            lanes = slice(lo, lo + CONV_LANES)
            u0c = u0_buf[pl.ds(r0, CONV_ROWS), lanes]
            acc = jnp.zeros((CONV_ROWS, CONV_LANES), F32)
            for j in range(CONF_K):
                k = CONF_K - 1 - j
                sj = tap(j)
                acc = acc + w31_ref[k:k + 1, lanes] * sj
                prod = u0c * sj
                part = prod[0:8]
                for i in range(1, CONV_ROWS // 8):
                    part = part + prod[8 * i:8 * i + 8]
                dw8[k, :, lanes] += part
            du0_buf[pl.ds(r0, CONV_ROWS), lanes] = acc

        _conv_chunks(du1_ext, e_buf, ts, d, conv_chunk, unroll=False)
        du0 = du0_buf[...]
        du1_ext[ts:ts + HALO_CONF, :] = du1_ext[0:HALO_CONF, :]

        @pl.when(step == n_t - 1)
        def _():
            for k in range(CONF_K):
                acc_ref[ROW_W31 + k:ROW_W31 + k + 1, :] = _colsum(dw8[k])
        dproj_ref[:, 3 * d:4 * d] = (du0 * sg).astype(BF16)
        dproj_ref[:, 4 * d:5 * d] = (du0 * v_c * sg * (1.0 - sg)).astype(BF16)

    rev = lambda w: pl.BlockSpec((ts, w), lambda i: (n_t - 1 - i, 0))
    full = lambda a: pl.BlockSpec(a.shape, lambda i: (0, 0))
    sd = lambda w, dt: jax.ShapeDtypeStruct((s_len, w), dt)
    outs = [sd(7 * d, BF16), sd(3 * d, BF16), jax.ShapeDtypeStruct((N_SMALL_ROWS, d), F32)]
    res = _pallas(
        body, name="bwd_mix", grid=(n_t,), out_shape=tuple(outs) + _exchange_out_shapes(ready, ready_axes),
        in_specs=[rev(d), rev(7 * d), rev(d), rev(LANES), rev(d), rev(d), rev(d), rev(d), full(mod), full(w3), full(w31),
                  full(lng), full(lnb), ANY, ANY, ANY] + [ANY] * n_r,
        out_specs=tuple([rev(o.shape[1]) for o in outs[:-1]] + [pl.BlockSpec((N_SMALL_ROWS, d), lambda i: (0, 0))] + [ANY] * n_r),
        scratch_shapes=[pltpu.VMEM(w_a.shape, BF16), pltpu.VMEM(w_b.shape, BF16), pltpu.VMEM(w_o.shape, BF16),
                        pltpu.VMEM((ts + HALO_SHORT, d), F32), pltpu.VMEM((ts + HALO_CONF, d), F32),
                        pltpu.VMEM((7, ts + HALO_CONF - 8, d // 2), F32), pltpu.VMEM((ts, d), F32), pltpu.VMEM((ts, d), F32),
                        pltpu.VMEM((CONF_K, 8, d), F32), pltpu.SemaphoreType.DMA((3,))] + _exchange_sems(n_r),
        compiler_params=_params("arbitrary"),
    )(dx1, proj, nrm, rstd, c3, ya, yb, mix, mod, w3, w31, lng, lnb, w_a, w_b, w_o, *ready)
    return res[:len(outs)], res[len(outs):]


def _bwd_in(dproj, x, dx1, mod, gn1, w_in, ready, ready_axes, ts):
    s_len, d = x.shape
    n_t = s_len // ts
    n_r = len(ready)
    ready_shapes = [g.shape for g in ready]

    def body(*refs):
        dproj_ref, x_ref, dx1_ref, mod_ref, gn1_ref, win_hbm = refs[:6]
        ready_refs = refs[6:6 + n_r]
        gx_ref, acc_ref = refs[6 + n_r:8 + n_r]
        recv_refs = refs[8 + n_r:8 + 2 * n_r]
        win_v, sems, e_send, e_recv = refs[8 + 2 * n_r:]
        step = pl.program_id(0)
        pl.when(step == 0)(lambda: _exchange_send(ready_refs, recv_refs, ready_shapes, ready_axes, e_send, e_recv))
        _load_weights([(win_hbm, win_v)], sems)

        @pl.when(pl.program_id(0) == 0)
        def _():
            acc_ref[...] = jnp.zeros(acc_ref.shape, F32)

        xv = x_ref[...]
        sc1 = mod_ref[:, d:2 * d]
        gn1v = gn1_ref[...]
        r = lax.rsqrt(jnp.mean(xv * xv, axis=-1, keepdims=True) + EPS)
        xn = xv * r
        dh = _dot_nt(dproj_ref[...], win_v[...])
        acc_ref[ROW_DMOD:ROW_DMOD + 1, :] += _colsum(dh)
        acc_ref[ROW_DMOD + 1:ROW_DMOD + 1 + 1, :] += _colsum(dh * (xn * gn1v))
        acc_ref[ROW_GN1:ROW_GN1 + 1, :] += _colsum(dh * xn * (1.0 + sc1))
        dxn = dh * gn1v * (1.0 + sc1)
        gx_ref[...] = dx1_ref[...] + r * (dxn - xn * jnp.mean(dxn * xn, axis=-1, keepdims=True))
        pl.when(step == n_t - 1)(lambda: _exchange_finish(recv_refs, e_send, e_recv))

    row = lambda w: pl.BlockSpec((ts, w), lambda i: (i, 0))
    full = lambda a: pl.BlockSpec(a.shape, lambda i: (0, 0))
    res = _pallas(
        body, name="bwd_in", grid=(n_t,),
        out_shape=(jax.ShapeDtypeStruct((s_len, d), F32), jax.ShapeDtypeStruct((N_SMALL_ROWS, d), F32))
        + _exchange_out_shapes(ready, ready_axes),
        in_specs=[row(7 * d), row(d), row(d), full(mod), full(gn1), ANY] + [ANY] * n_r,
        out_specs=(row(d), pl.BlockSpec((N_SMALL_ROWS, d), lambda i: (0, 0))) + tuple([ANY] * n_r),
        scratch_shapes=[pltpu.VMEM(w_in.shape, BF16), pltpu.SemaphoreType.DMA((1,))] + _exchange_sems(n_r),
        compiler_params=_params("arbitrary"),
    )(dproj, x, dx1, mod, gn1, w_in, *ready)
    return res[0], res[1], res[2:]


def _weight_grad(a, b, bn, name, transposed=False, ready=(), ready_axes=(), a_cols=None):
    s_len, m = a.shape[0], (a_cols or a.shape[1])
    n = b.shape[1]
    n_j = n // bn
    n_r = len(ready)
    ready_shapes = [g.shape for g in ready]

    def body(*refs):
        a_hbm, b_ref = refs[:2]
        ready_refs = refs[2:2 + n_r]
        o_ref = refs[2 + n_r]
        recv_refs = refs[3 + n_r:3 + 2 * n_r]
        a_v, sem = refs[3 + 2 * n_r:5 + 2 * n_r]
        step = pl.program_id(0)
        if n_r:
            e_send, e_recv = refs[5 + 2 * n_r:]
            pl.when(step == 0)(lambda: _exchange_send(ready_refs, recv_refs, ready_shapes, ready_axes, e_send, e_recv))
            pl.when(step == n_j - 1)(lambda: _exchange_finish(recv_refs, e_send, e_recv))
        if a_cols is None:
            _load_weights([(a_hbm, a_v)], sem)
            prod = _dot_tn(a_v[...], b_ref[...])
        else:
            prod = _dot_tn(a_hbm[...], b_ref[...])
        o_ref[...] = (prod.T if transposed else prod).astype(BF16)

    out = jax.ShapeDtypeStruct((n, m) if transposed else (m, n), BF16)
    out_spec = pl.BlockSpec((bn, m), lambda j: (j, 0)) if transposed else pl.BlockSpec((m, bn), lambda j: (0, j))
    res = _pallas(
        body, name=name, grid=(n_j,), out_shape=(out,) + _exchange_out_shapes(ready, ready_axes),
        in_specs=[ANY if a_cols is None else pl.BlockSpec((s_len, a_cols), lambda j: (0, (j * bn) // a_cols)),
                  pl.BlockSpec((s_len, bn), lambda j: (0, j))] + [ANY] * n_r,
        out_specs=(out_spec,) + tuple([ANY] * n_r),
        scratch_shapes=[pltpu.VMEM((8, 128) if a_cols else a.shape, BF16), pltpu.SemaphoreType.DMA((1,))]
        + (_exchange_sems(n_r) if n_r else []),
        compiler_params=_params("arbitrary"),
    )(a, b, *ready)
    return res[0], res[1:]


def _sum_slots(recv, grad, axis, chip, core, name):
    n_s, r, c = recv.shape
    tr = 16
    for cand in (256, 128, 176, 64, 32, 16):
        if r % cand == 0:
            tr = cand
            break
    n_i = r // tr

    def body(s_ref, own_ref, r_ref, o_ref):
        tot = own_ref[...].astype(F32)
        for k in range(n_s):
            tot = tot + r_ref[k].astype(F32)
        o_ref[...] = tot

    if axis == 0:
        own_map = lambda i, s: ((2 * s[0] + s[1]) * n_i + i, 0)
    else:
        own_map = lambda i, s: (s[1] * n_i + i, s[0])
    grid_spec = pltpu.PrefetchScalarGridSpec(
        num_scalar_prefetch=1, grid=(n_i,),
        in_specs=[pl.BlockSpec((tr, c), own_map), pl.BlockSpec((n_s, tr, c), lambda i, s: (0, i, 0))],
        out_specs=pl.BlockSpec((tr, c), lambda i, s: (s[1] * n_i + i, 0)))
    return _pallas(body, name=name, grid_spec=grid_spec, out_shape=jax.ShapeDtypeStruct((2 * r, c), F32),
                   compiler_params=_params("arbitrary"))(jnp.stack([chip, core]).astype(jnp.int32), grad, recv)


def _adamw_math(w, g, m, v):
    m = ADAM_B1 * m + (1.0 - ADAM_B1) * g
    v = ADAM_B2 * v + (1.0 - ADAM_B2) * (g * g)
    m_hat = m / (1.0 - ADAM_B1 ** ADAM_STEP)
    v_hat = v / (1.0 - ADAM_B2 ** ADAM_STEP)
    delta = -ADAM_LR * (m_hat / (jnp.sqrt(v_hat) + ADAM_EPS) + ADAM_WD * w)
    return delta, m, v


def _adamw(w, g, m, v, name):
    r, c = w.shape
    tr = r
    for cand in (256, 128, 176, 64, 32, 16, 8):
        if r % cand == 0:
            tr = cand
            break

    def body(w_ref, g_ref, m_ref, v_ref, go_ref, d_ref, nm_ref, nv_ref):
        g = g_ref[...]
        go_ref[...] = g
        d_ref[...], nm_ref[...], nv_ref[...] = _adamw_math(w_ref[...], g, m_ref[...], v_ref[...])

    blk = pl.BlockSpec((tr, c), lambda i: (i, 0))
    return _pallas(body, name=name, grid=(r // tr,), out_shape=tuple(jax.ShapeDtypeStruct((r, c), F32) for _ in range(4)),
                   in_specs=[blk] * 4, out_specs=(blk,) * 4, compiler_params=_params("arbitrary"))(w, g, m, v)


_SMALL = [
    ("b_ada", ROW_DMOD, "mod"), ("norm_mix_g", ROW_GN1, "row"), ("conv_short_w", ROW_W3, "shard"),
    ("conv_conf_w", ROW_W31, "shard"), ("conv_conf_b", ROW_CB, "row"), ("conf_ln_g", ROW_LNG, "row"),
    ("conf_ln_b", ROW_LNB, "row"), ("norm_ffn_g", ROW_GN2, "row"), ("final_norm_g", ROW_GF, "row")]


def _small_update(p_all, chip_onehot, ws, ms, vs):
    n = len(_SMALL)
    d = p_all.shape[2]

    def body(*refs):
        p_ref, oh_ref = refs[:2]
        w_refs, m_refs, v_refs = refs[2:2 + n], refs[2 + n:2 + 2 * n], refs[2 + 2 * n:2 + 3 * n]
        loss_ref = refs[2 + 3 * n]
        g_out, d_out, m_out, v_out = [refs[3 + (3 + q) * n:3 + (4 + q) * n] for q in range(4)]
        g_all = p_ref[0]
        for s in range(1, N_DEV):
            g_all = g_all + p_ref[s]
        loss_ref[...] = jnp.zeros(loss_ref.shape, F32) + (0.5 / d) * jnp.sum(g_all[ROW_SQERR:ROW_SQERR + 1, :])

        def emit(i, idx, g):
            dl, nm, nv = _adamw_math(w_refs[i][idx], g, m_refs[i][idx], v_refs[i][idx])
            g_out[i][idx], d_out[i][idx], m_out[i][idx], v_out[i][idx] = g, dl, nm, nv

        everything = (slice(None), slice(None))
        for i, (_, row, kind) in enumerate(_SMALL):
            if kind == "mod":
                for k in range(N_MOD):
                    emit(i, (slice(0, 1), slice(k * d, (k + 1) * d)), g_all[row + k:row + k + 1, :])
            elif kind == "row":
                emit(i, everything, g_all[row:row + 1, :])
            else:
                _, taps, dq = w_refs[i].shape
                g = jnp.zeros((taps, dq), F32)
                for j in range(N_CHIP):
                    g = g + oh_ref[0:1, j:j + 1] * g_all[row:row + taps, j * dq:(j + 1) * dq]
                emit(i, (0, slice(None), slice(None)), g)

    shapes = [jax.ShapeDtypeStruct(w.shape, F32) for w in ws]
    res = _pallas(body, name="small_update", out_shape=tuple([jax.ShapeDtypeStruct((8, 128), F32)] + shapes * 4),
                  in_specs=[VMEM_FULL] * (2 + 3 * n), out_specs=tuple([VMEM_FULL] * (1 + 4 * n)),
                  compiler_params=_params())(p_all, chip_onehot, *ws, *ms, *vs)
    return res[0], [res[1 + q * n:1 + (q + 1) * n] for q in range(4)]


def _w_ada_grad(c_t, dmod_part):
    def body(c_ref, g_ref, o_ref):
        cv = c_ref[...]
        o_ref[...] = _dot((cv * _sig(cv)).astype(BF16), g_ref[...].astype(BF16))

    return _pallas(body, name="w_ada_grad", out_shape=jax.ShapeDtypeStruct((c_t.shape[0], dmod_part.shape[1]), F32),
                   in_specs=[VMEM_FULL] * 2, out_specs=VMEM_FULL, compiler_params=_params())(c_t, dmod_part)


def _step(x, c, w_ada, b_ada, norm_mix_g, w_in, conv_short_w, w_short_out, conv_conf_w, conv_conf_b, conf_ln_g,
          conf_ln_b, w_conf_out, w_o, norm_ffn_g, w_ffn_in, w_ffn_out, final_norm_g, loss_target, moments, ts):
    xi, yi, ci = _place()
    chip = 2 * xi + yi
    me = 4 * xi + 2 * yi + ci
    x2d, tgt = x[0], loss_target[0]
    s_len, d = x2d.shape
    dq = d // N_CHIP
    mq = N_MOD * d // N_CHIP
    gf = final_norm_g.reshape(1, d)

    taps = jnp.concatenate([jnp.pad(conv_short_w[0], ((0, 8 - SHORT_K), (0, 0))),
                            jnp.pad(conv_conf_w[0], ((0, 32 - CONF_K), (0, 0)))], axis=0)
    n_tr = 40 * dq // d
    c_taps = jnp.concatenate([jnp.broadcast_to(c, (8, d)), taps.reshape(n_tr, d), jnp.zeros((16 - n_tr, d), F32)], axis=0)
    b_part = lax.dynamic_slice(b_ada, (0, chip * mq), (1, mq))
    big = [w_in[0], w_short_out[0], w_conf_out[0], w_o[0], w_ffn_in[0], w_ffn_out[0]]
    axes = [1, 0, 0, 0, 1, 0]
    big_names = ["w_in", "w_short_out", "w_conf_out", "w_o", "w_ffn_in", "w_ffn_out"]
    (win_f, wa_f, wb_f, wo_f, *placed_ffn), c_taps, mod_all = _gather_weights(big, axes, 4, c_taps, w_ada[0], b_part,
                                                                              "gather_weights")
    c_taps = c_taps.reshape(N_DEV, 24, d)
    c_all = c_taps[:, 0]
    mod_all = mod_all.reshape(N_CHIP, 2, N_DEV, mq)[:, 0]
    mod_full = jnp.transpose(mod_all, (1, 0, 2)).reshape(N_DEV, N_MOD * d)
    mod = lax.dynamic_slice(mod_full, (me, 0), (1, N_MOD * d))

    taps_all = c_taps[:, 8:8 + n_tr].reshape(N_CHIP, 2, 40, dq)[:, 0]
    taps_full = jnp.transpose(taps_all, (1, 0, 2)).reshape(40, d)
    w3, w31 = taps_full[0:SHORT_K], taps_full[8:8 + CONF_K]

    (proj, h, lhs3, c3, nrm, rstd, ya, yb, mix, x1), (wfi_f, wfo_f) = _fwd_mix(
        x2d, mod, norm_mix_g, w3, w31, conv_conf_b, conf_ln_g, conf_ln_b, win_f, wa_f, wb_f, wo_f, placed_ffn, axes[4:], ts)
    h2, hid, df, dab, dx1, acc_f = _ffn(x1, tgt, mod, norm_ffn_g, gf, wfi_f, wfo_f, ts)
    f = wfo_f.shape[0]
    bn = min(512, d)
    g_ffn = [_weight_grad(h2, dab, bn, "grad_w_ffn_in")[0], _weight_grad(df, hid, bn // 2, "grad_w_ffn_out", transposed=True)[0]]
    (dproj, rhs3, acc_m), r_ffn = _bwd_mix(dx1, proj, nrm, rstd, c3, ya, yb, mix, mod, w3, w31, conf_ln_g, conf_ln_b,
                                          wa_f, wb_f, wo_f, g_ffn, axes[4:], ts)
    g3 = _weight_grad(lhs3, rhs3, bn // 2, "grad_w_sq", a_cols=d)[0]
    g_sq = [g3[:, k * d:(k + 1) * d] for k in range(3)]
    g_in, r_sq = _weight_grad(h, dproj, bn, "grad_w_in", ready=g_sq, ready_axes=axes[1:4])
    grad_x, acc_i, r_in = _bwd_in(dproj, x2d, dx1, mod, norm_mix_g, win_f, [g_in], axes[:1], min(2 * ts, s_len))
    g_in = [g_in]
    r_in_sq = list(r_in) + list(r_sq)
    grads = g_in + g_sq + g_ffn
    recv = list(r_in_sq) + list(r_ffn)
    g_big, p_all = _swap_halves([_sum_slots(r, g, ax, chip, ci, "sum_slots_" + n)
                                 for r, g, ax, n in zip(recv, grads, axes, big_names)], acc_f + acc_m + acc_i)
    p_all = p_all.reshape(N_DEV, N_SMALL_ROWS, d)

    small_w = dict(b_ada=b_ada, norm_mix_g=norm_mix_g, conv_short_w=conv_short_w, conv_conf_w=conv_conf_w,
                   conv_conf_b=conv_conf_b, conf_ln_g=conf_ln_g, conf_ln_b=conf_ln_b, norm_ffn_g=norm_ffn_g,
                   final_norm_g=final_norm_g)
    as_2d = lambda t: t if t.ndim == 3 else t.reshape((-1, t.shape[-1]))
    chip_onehot = (lax.broadcasted_iota(jnp.int32, (1, 128), 1) == chip).astype(F32)
    loss_tile, small_out = _small_update(
        p_all, chip_onehot, [as_2d(small_w[n]) for n, _, _ in _SMALL], [as_2d(moments["m_" + n]) for n, _, _ in _SMALL],
        [as_2d(moments["v_" + n]) for n, _, _ in _SMALL])
    small_g, small_d, small_m, small_v = [{n: o.reshape(small_w[n].shape) for (n, _, _), o in zip(_SMALL, outs)}
                                          for outs in small_out]

    dmod_all = p_all[:, 0:N_MOD].reshape(N_DEV, N_MOD * d)
    dmod_part = jnp.pad(lax.dynamic_slice(dmod_all, (0, chip * mq), (N_DEV, mq)), ((0, 128 - N_DEV), (0, 0)))
    c_t = jnp.pad(jnp.transpose(c_all), ((0, 0), (0, 128 - N_DEV)))
    g_ada = _w_ada_grad(c_t, dmod_part)

    out_g, out_d, out_m, out_v = dict(small_g), dict(small_d), dict(small_m), dict(small_v)
    for name, w, g in zip(["w_ada"] + big_names, [w_ada[0]] + big, [g_ada] + list(g_big)):
        go, dl, nm, nv = _adamw(w, g, moments["m_" + name][0], moments["v_" + name][0], "adamw_" + name)
        out_g[name], out_d[name], out_m[name], out_v[name] = go[None], dl[None], nm[None], nv[None]

    loss = loss_tile[0, 0]
    return loss, grad_x[None], out_g, out_d, out_m, out_v


_WEIGHTS = ["w_ada", "b_ada", "norm_mix_g", "w_in", "conv_short_w", "w_short_out", "conv_conf_w", "conv_conf_b",
            "conf_ln_g", "conf_ln_b", "w_conf_out", "w_o", "norm_ffn_g", "w_ffn_in", "w_ffn_out", "final_norm_g"]
ROW_TILE = 256


def kernel(x, c, w_ada, b_ada, norm_mix_g, w_in, conv_short_w, w_short_out, conv_conf_w, conv_conf_b, conf_ln_g, conf_ln_b, w_conf_out, w_o, norm_ffn_g, w_ffn_in, w_ffn_out, final_norm_g, loss_target, m_w_ada, m_b_ada, m_norm_mix_g, m_w_in, m_conv_short_w, m_w_short_out, m_conv_conf_w, m_conv_conf_b, m_conf_ln_g, m_conf_ln_b, m_w_conf_out, m_w_o, m_norm_ffn_g, m_w_ffn_in, m_w_ffn_out, m_final_norm_g, v_w_ada, v_b_ada, v_norm_mix_g, v_w_in, v_conv_short_w, v_w_short_out, v_conv_conf_w, v_conv_conf_b, v_conf_ln_g, v_conf_ln_b, v_w_conf_out, v_w_o, v_norm_ffn_g, v_w_ffn_in, v_w_ffn_out, v_final_norm_g):
    moments = dict(
        m_w_ada=m_w_ada, m_b_ada=m_b_ada, m_norm_mix_g=m_norm_mix_g, m_w_in=m_w_in, m_conv_short_w=m_conv_short_w,
        m_w_short_out=m_w_short_out, m_conv_conf_w=m_conv_conf_w, m_conv_conf_b=m_conv_conf_b, m_conf_ln_g=m_conf_ln_g,
        m_conf_ln_b=m_conf_ln_b, m_w_conf_out=m_w_conf_out, m_w_o=m_w_o, m_norm_ffn_g=m_norm_ffn_g, m_w_ffn_in=m_w_ffn_in,
        m_w_ffn_out=m_w_ffn_out, m_final_norm_g=m_final_norm_g,
        v_w_ada=v_w_ada, v_b_ada=v_b_ada, v_norm_mix_g=v_norm_mix_g, v_w_in=v_w_in, v_conv_short_w=v_conv_short_w,
        v_w_short_out=v_w_short_out, v_conv_conf_w=v_conv_conf_w, v_conv_conf_b=v_conv_conf_b, v_conf_ln_g=v_conf_ln_g,
        v_conf_ln_b=v_conf_ln_b, v_w_conf_out=v_w_conf_out, v_w_o=v_w_o, v_norm_ffn_g=v_norm_ffn_g, v_w_ffn_in=v_w_ffn_in,
        v_w_ffn_out=v_w_ffn_out, v_final_norm_g=v_final_norm_g)
    loss, grad_x, g, dl, nm, nv = _step(
        x, c, w_ada, b_ada, norm_mix_g, w_in, conv_short_w, w_short_out, conv_conf_w, conv_conf_b, conf_ln_g, conf_ln_b,
        w_conf_out, w_o, norm_ffn_g, w_ffn_in, w_ffn_out, final_norm_g, loss_target, moments, min(ROW_TILE, x.shape[1]))
    return (loss, grad_x, *[g[n] for n in _WEIGHTS], *[dl[n] for n in _WEIGHTS], *[nm[n] for n in _WEIGHTS],
            *[nv[n] for n in _WEIGHTS])
```

```python
import jax
import jax.numpy as jnp
from jax import lax
from jax.experimental import pallas as pl
from jax.experimental.pallas import tpu as pltpu

F32 = jnp.float32
BF16 = jnp.bfloat16
EPS = 1e-6
LN_EPS = 1e-5
SHORT_K = 3
CONF_K = 31
N_MOD = 6
N_DEV = 8
N_CHIP = 4
ADAM_LR = 0.001
ADAM_B1 = 0.9
ADAM_B2 = 0.999
ADAM_EPS = 1e-08
ADAM_WD = 0.01
ADAM_STEP = 10
MESH = pl.DeviceIdType.MESH
V7X_VMEM_LIMIT_BYTES = 60 * 1024 * 1024
LANES = 128
HALO_SHORT = 8
HALO_CONF = 32
N_SMALL_ROWS = 48
ROW_DMOD, ROW_GN1, ROW_W3, ROW_W31, ROW_CB, ROW_LNG, ROW_LNB, ROW_GN2, ROW_GF, ROW_SQERR = 0, 6, 7, 10, 41, 42, 43, 44, 45, 46
CONV_ROWS = 64
CONV_LANES = 256

ANY = pl.BlockSpec(memory_space=pl.ANY)
VMEM_FULL = pl.BlockSpec(memory_space=pltpu.VMEM)


def _pallas(body, **kw):
    return pl.pallas_call(body, **kw)


def _params(*sem):
    return pltpu.CompilerParams(dimension_semantics=sem, vmem_limit_bytes=V7X_VMEM_LIMIT_BYTES)


def _sig(z):
    return jax.nn.sigmoid(z)


def _dot(a, b):
    return jnp.dot(a, b, preferred_element_type=F32)


def _dot_nt(a, b):
    return lax.dot_general(a, b, (((1,), (1,)), ((), ())), preferred_element_type=F32)


def _dot_tn(a, b):
    return lax.dot_general(a, b, (((0,), (0,)), ((), ())), preferred_element_type=F32)


def _colsum(z):
    return jnp.sum(z, axis=0, keepdims=True)


def _place():
    return lax.axis_index("x"), lax.axis_index("y"), lax.axis_index("c")


def _small_gather_copies(x_ref, out_ref, send_sems, recv_sems, local_sem):
    m_per = x_ref.shape[0]
    x, y, c = _place()
    me, sibling = (x, y, c), (x, y, 1 - c)
    chips = [(1 - x, y), (x, 1 - y), (1 - x, 1 - y)]

    def rows(px, py, pc):
        return out_ref.at[pl.ds((4 * px + 2 * py + pc) * m_per, m_per), :]

    def copy(k, blk, to, src=None):
        return pltpu.make_async_remote_copy(
            src_ref=rows(*blk) if src is None else src, dst_ref=rows(*blk),
            send_sem=send_sems.at[k], recv_sem=recv_sems.at[k], device_id=to, device_id_type=MESH)

    mine = pltpu.make_async_copy(x_ref, rows(*me), local_sem)
    first = [copy(0, me, sibling, src=x_ref)] + [copy(1 + j, me, (*chip, c), src=x_ref) for j, chip in enumerate(chips)]
    return me, sibling, chips, c, copy, mine, first


def _small_gather_start(x_ref, out_ref, send_sems, recv_sems, local_sem):
    _, _, _, _, _, mine, first = _small_gather_copies(x_ref, out_ref, send_sems, recv_sems, local_sem)
    mine.start()
    for cp in first:
        cp.start()


def _small_gather_finish(x_ref, out_ref, send_sems, recv_sems, local_sem):
    me, sibling, chips, c, copy, mine, first = _small_gather_copies(x_ref, out_ref, send_sems, recv_sems, local_sem)
    passed = [copy(4 + j, (*chip, c), sibling) for j, chip in enumerate(chips)]
    for j, chip in enumerate(chips):
        copy(1 + j, (*chip, c), me).wait_recv()
        passed[j].start()
    copy(0, sibling, me).wait_recv()
    for j, chip in enumerate(chips):
        copy(4 + j, (*chip, 1 - c), me).wait_recv()
    for cp in first + passed:
        cp.wait_send()
    mine.wait()


def _small_gather(x_ref, out_ref, *sems):
    _small_gather_start(x_ref, out_ref, *sems)
    _small_gather_finish(x_ref, out_ref, *sems)


def _small_gather_sems():
    return [pltpu.SemaphoreType.DMA((7,)), pltpu.SemaphoreType.DMA((7,)), pltpu.SemaphoreType.DMA]


def _shard_slice(ref, full_shape, shard_axis, chip, half, quarter=None):
    m, n = full_shape
    r = (m // N_CHIP if shard_axis == 0 else m) // 2
    start = half * r + (chip * 2 * r if shard_axis == 0 else 0)
    if quarter is not None:
        r = r // 2
        start = start + quarter * r
    if shard_axis == 0:
        return ref.at[pl.ds(start, r), :]
    cw = n // N_CHIP
    return ref.at[pl.ds(start, r), pl.ds(pl.multiple_of(chip * cw, 128), cw)]


def _gather_copy(region, k, a, to, sems):
    return pltpu.make_async_remote_copy(src_ref=region, dst_ref=region, send_sem=sems[0].at[a, k], recv_sem=sems[1].at[a, k],
                                        device_id=to, device_id_type=MESH)


def _gather_send(refs, shapes, axes, *sems):
    x, y, c = _place()
    for a in range(len(refs)):
        mine = _shard_slice(refs[a], shapes[a], axes[a], 2 * x + y, c)
        _gather_copy(mine, 0, a, (1 - x, y, c), sems).start()
        _gather_copy(mine, 1, a, (x, 1 - y, c), sems).start()


def _gather_forward(refs, shapes, axes, *sems):
    x, y, c = _place()
    me, sibling, x_nbr, y_nbr = (x, y, c), (x, y, 1 - c), (1 - x, y, c), (x, 1 - y, c)
    chip_x, chip_y, chip_d = 2 * (1 - x) + y, 2 * x + (1 - y), 2 * (1 - x) + (1 - y)
    for a in range(len(refs)):
        region = lambda chip, quarter=None, a=a: _shard_slice(refs[a], shapes[a], axes[a], chip, c, quarter)
        _gather_copy(region(chip_x), 0, a, me, sems).wait_recv()
        _gather_copy(region(chip_x, 1), 3, a, y_nbr, sems).start()
        _gather_copy(region(chip_x), 4, a, sibling, sems).start()
        _gather_copy(region(chip_y), 1, a, me, sems).wait_recv()
        _gather_copy(region(chip_y, 0), 2, a, x_nbr, sems).start()
        _gather_copy(region(chip_y), 5, a, sibling, sems).start()


def _gather_forward_diagonal(refs, shapes, axes, *sems):
    x, y, c = _place()
    me, sibling, chip_d = (x, y, c), (x, y, 1 - c), 2 * (1 - x) + (1 - y)
    for a in range(len(refs)):
        region = lambda chip, quarter=None, a=a: _shard_slice(refs[a], shapes[a], axes[a], chip, c, quarter)
        _gather_copy(region(chip_d, 0), 2, a, me, sems).wait_recv()
        _gather_copy(region(chip_d, 1), 3, a, me, sems).wait_recv()
        _gather_copy(region(chip_d), 6, a, sibling, sems).start()


def _gather_finish(refs, shapes, axes, *sems):
    x, y, c = _place()
    me = (x, y, c)
    chip_me, chip_x, chip_y, chip_d = 2 * x + y, 2 * (1 - x) + y, 2 * x + (1 - y), 2 * (1 - x) + (1 - y)
    for a in range(len(refs)):
        region = lambda chip, half, quarter=None, a=a: _shard_slice(refs[a], shapes[a], axes[a], chip, half, quarter)
        for k, chip in ((4, chip_x), (5, chip_y), (6, chip_d)):
            _gather_copy(region(chip, 1 - c), k, a, me, sems).wait_recv()
            _gather_copy(region(chip, c), k, a, me, sems).wait_send()
        _gather_copy(region(chip_me, c), 0, a, me, sems).wait_send()
        _gather_copy(region(chip_me, c), 1, a, me, sems).wait_send()
        _gather_copy(region(chip_y, c, 0), 2, a, me, sems).wait_send()
        _gather_copy(region(chip_x, c, 1), 3, a, me, sems).wait_send()


def _gather_sems(n_w):
    return [pltpu.SemaphoreType.DMA((n_w, 7)), pltpu.SemaphoreType.DMA((n_w, 7))]


def _gather_weights(shards, axes, n_gather, c_taps, w_ada, b_part, name):
    n_w = len(shards)
    shapes = [(w.shape[0] * N_CHIP, w.shape[1]) if ax == 0 else (w.shape[0], w.shape[1] * N_CHIP) for w, ax in zip(shards, axes)]
    r_ct = c_taps.shape[0]
    mq = w_ada.shape[1]

    def body(*refs):
        ins = refs[:n_w]
        ct_ref, wada_ref, b_ref = refs[n_w:n_w + 3]
        outs = refs[n_w + 3:2 * n_w + 3]
        ct_all_ref, mod_all_ref = refs[2 * n_w + 3:2 * n_w + 5]
        scratch = refs[2 * n_w + 5:]
        stage32, stage16, mod_part = scratch[:n_w], scratch[n_w:2 * n_w], scratch[2 * n_w]
        load_sems, store_sems = scratch[2 * n_w + 1:2 * n_w + 3]
        sems, ct_sems, mod_sems = scratch[2 * n_w + 3:2 * n_w + 5], scratch[2 * n_w + 5:2 * n_w + 8], scratch[2 * n_w + 8:]
        x, y, c = _place()
        chip = 2 * x + y
        loads = [pltpu.make_async_copy(ins[a], stage32[a], load_sems.at[a]) for a in range(n_w)]
        for cp in loads:
            cp.start()
        _small_gather_start(ct_ref, ct_all_ref, *ct_sems)
        stores = []
        for a in range(n_w):
            loads[a].wait()
            stage16[a][...] = stage32[a][...].astype(BF16)
            r, cw = shards[a].shape
            own = outs[a].at[pl.ds(chip * r, r), :] if axes[a] == 0 else outs[a].at[:, pl.ds(pl.multiple_of(chip * cw, 128), cw)]
            stores.append(pltpu.make_async_copy(stage16[a], own, store_sems.at[a]))
            stores[-1].start()
            if a < n_gather:
                half = stage16[a].at[pl.ds(c * (r // 2), r // 2), :]
                there = _shard_slice(outs[a], shapes[a], axes[a], chip, c)
                for k, to in ((0, (1 - x, y, c)), (1, (x, 1 - y, c))):
                    pltpu.make_async_remote_copy(src_ref=half, dst_ref=there, send_sem=sems[0].at[a, k],
                                                 recv_sem=sems[1].at[a, k], device_id=to, device_id_type=MESH).start()
        _small_gather_finish(ct_ref, ct_all_ref, *ct_sems)
        cv = jnp.concatenate([ct_all_ref[r_ct * dv:r_ct * dv + 1, :] for dv in range(N_DEV)], axis=0)
        mod_part[...] = _dot((cv * _sig(cv)).astype(BF16), wada_ref[...].astype(BF16)) + b_ref[...]
        _small_gather_start(mod_part, mod_all_ref, *mod_sems)
        _gather_forward(outs[:n_gather], shapes, axes, *sems)
        _small_gather_finish(mod_part, mod_all_ref, *mod_sems)
        _gather_forward_diagonal(outs[:n_gather], shapes, axes, *sems)
        _gather_finish(outs[:n_gather], shapes, axes, *sems)
        for cp in stores:
            cp.wait()

    res = _pallas(
        body, name=name,
        out_shape=tuple(jax.ShapeDtypeStruct(sh, BF16) for sh in shapes)
        + (jax.ShapeDtypeStruct((N_DEV * r_ct, c_taps.shape[1]), F32), jax.ShapeDtypeStruct((N_DEV * N_DEV, mq), F32)),
        in_specs=[ANY] * n_w + [VMEM_FULL] * 3, out_specs=tuple([ANY] * n_w + [VMEM_FULL] * 2),
        scratch_shapes=[pltpu.VMEM(w.shape, F32) for w in shards] + [pltpu.VMEM(w.shape, BF16) for w in shards]
        + [pltpu.VMEM((N_DEV, mq), F32), pltpu.SemaphoreType.DMA((n_w,)), pltpu.SemaphoreType.DMA((n_w,))]
        + _gather_sems(n_gather) + _small_gather_sems() + _small_gather_sems(),
        compiler_params=_params(),
    )(*shards, c_taps, w_ada, b_part)
    return res[:n_w], res[n_w], res[n_w + 1]


def _piece_shapes(grads, axes):
    return [((g.shape[0] // N_CHIP // 2, g.shape[1]) if ax == 0 else (g.shape[0] // 2, g.shape[1] // N_CHIP))
            for g, ax in zip(grads, axes)]


def _exchange_send(ins, outs, shapes, axes, send_sems, recv_sems):
    x, y, c = _place()
    me = 4 * x + 2 * y + c
    for a in range(len(ins)):
        for t in range(N_DEV):
            tx, ty, tc = t // 4, (t // 2) % 2, t % 2
            src = _shard_slice(ins[a], shapes[a], axes[a], 2 * tx + ty, tc)

            @pl.when(me != t)
            def _(src=src, a=a, t=t, to=(tx, ty, tc)):
                slot = (me - t + N_DEV) % N_DEV - 1
                pltpu.make_async_remote_copy(src_ref=src, dst_ref=outs[a].at[slot], send_sem=send_sems.at[a, t],
                                             recv_sem=recv_sems.at[a, slot], device_id=to, device_id_type=MESH).start()


def _exchange_finish(outs, send_sems, recv_sems):
    x, y, c = _place()
    me = 4 * x + 2 * y + c
    for a in range(len(outs)):
        for t in range(N_DEV):
            @pl.when(me != t)
            def _(a=a, t=t):
                slot = outs[a].at[0]
                pltpu.make_async_remote_copy(src_ref=slot, dst_ref=slot, send_sem=send_sems.at[a, t], recv_sem=recv_sems.at[a, 0],
                                             device_id=(x, y, c), device_id_type=MESH).wait_send()
        for k in range(N_DEV - 1):
            slot = outs[a].at[k]
            pltpu.make_async_remote_copy(src_ref=slot, dst_ref=slot, send_sem=send_sems.at[a, 0], recv_sem=recv_sems.at[a, k],
                                         device_id=(x, y, c), device_id_type=MESH).wait_recv()


def _exchange_sems(n_w):
    return [pltpu.SemaphoreType.DMA((n_w, N_DEV)), pltpu.SemaphoreType.DMA((n_w, N_DEV - 1))]


def _exchange_out_shapes(grads, axes):
    return tuple(jax.ShapeDtypeStruct((N_DEV - 1,) + p, BF16) for p in _piece_shapes(grads, axes))


def _swap_halves(fulls, block):
    n_w = len(fulls)

    def body(*refs):
        block_ref = refs[n_w]
        outs, gathered_ref = refs[n_w + 1:2 * n_w + 1], refs[2 * n_w + 1]
        send_sems, recv_sems = refs[2 * n_w + 2:2 * n_w + 4]
        x, y, c = _place()
        sibling = (x, y, 1 - c)
        waits = []
        for a in range(n_w):
            r = fulls[a].shape[0] // 2
            mine = outs[a].at[pl.ds(c * r, r), :]
            theirs = outs[a].at[pl.ds((1 - c) * r, r), :]
            rc = pltpu.make_async_remote_copy(src_ref=mine, dst_ref=mine, send_sem=send_sems.at[a], recv_sem=recv_sems.at[a],
                                              device_id=sibling, device_id_type=MESH)
            rc.start()
            got = pltpu.make_async_remote_copy(src_ref=theirs, dst_ref=theirs, send_sem=send_sems.at[a],
                                               recv_sem=recv_sems.at[a], device_id=sibling, device_id_type=MESH)
            waits += [rc.wait_send, got.wait_recv]
        _small_gather(block_ref, gathered_ref, *refs[2 * n_w + 4:])
        for w in waits:
            w()

    res = _pallas(
        body, name="swap_halves",
        out_shape=tuple(jax.ShapeDtypeStruct(f.shape, f.dtype) for f in fulls)
        + (jax.ShapeDtypeStruct((N_DEV * block.shape[0], block.shape[1]), block.dtype),),
        in_specs=[ANY] * n_w + [VMEM_FULL], out_specs=tuple([ANY] * n_w + [VMEM_FULL]),
        input_output_aliases={a: a for a in range(n_w)},
        scratch_shapes=[pltpu.SemaphoreType.DMA((n_w,)), pltpu.SemaphoreType.DMA((n_w,))] + _small_gather_sems(),
    )(*fulls, block)
    return res[:n_w], res[n_w]


def _load_weights(pairs, sems):
    @pl.when(pl.program_id(0) == 0)
    def _():
        cps = [pltpu.make_async_copy(s, d, sems.at[k]) for k, (s, d) in enumerate(pairs)]
        for cp in cps:
            cp.start()
        for cp in cps:
            cp.wait()


def _conv_chunks(ext, e_buf, ts, d, chunk_fn, unroll, between=()):
    rows, dh = e_buf.shape[1], e_buf.shape[2]
    for hf in range(d // dh):
        for r in range(1, 8):
            e_buf[r - 1, :, :] = ext[r:r + rows, hf * dh:(hf + 1) * dh]
        for lq in range(dh // CONV_LANES):
            lo, el = hf * dh + lq * CONV_LANES, lq * CONV_LANES

            def body(ci, carry, lo=lo, el=el):
                r0 = pl.multiple_of(ci * CONV_ROWS, CONV_ROWS)

                def tap(off):
                    q, r = divmod(off, 8)
                    if r == 0:
                        return ext[pl.ds(r0 + 8 * q, CONV_ROWS), lo:lo + CONV_LANES]
                    return e_buf[r - 1, pl.ds(r0 + 8 * q, CONV_ROWS), el:el + CONV_LANES]

                chunk_fn(tap, r0, lo)
                return carry

            lax.fori_loop(0, ts // CONV_ROWS, body, 0, unroll=unroll)
            if between:
                between[0]()
                between = between[1:]
    for fn in between:
        fn()


def _fwd_mix(x, mod, gn1, w3, w31, cbias, lng, lnb, w_in, w_a, w_b, w_o, later, later_axes, ts):
    s_len, d = x.shape
    n_t = s_len // ts
    n_l = len(later)
    later_shapes = [w.shape for w in later]

    def body(*refs):
        (x_ref, mod_ref, gn1_ref, w3_ref, w31_ref, cb_ref, lng_ref, lnb_ref, win_hbm, wa_hbm, wb_hbm, wo_hbm) = refs[:12]
        (proj_ref, h_ref, lhs3_ref, c3_ref, n_ref, rstd_ref, ya_ref, yb_ref, mix_ref, x1_ref) = refs[12 + n_l:22 + n_l]
        later_refs = refs[22 + n_l:22 + 2 * n_l]
        win_v, wa_v, wb_v, wo_v, cv_ext, u0_ext, e_buf, u1_buf, sems, g_send, g_recv = refs[22 + 2 * n_l:]
        step = pl.program_id(0)
        pl.when(step == 0)(lambda: _gather_send(later_refs, later_shapes, later_axes, g_send, g_recv))
        pl.when(step == n_t // 2)(lambda: _gather_forward(later_refs, later_shapes, later_axes, g_send, g_recv))
        pl.when(step == (3 * n_t) // 4)(lambda: _gather_forward_diagonal(later_refs, later_shapes, later_axes, g_send, g_recv))
        pl.when(step == n_t - 1)(lambda: _gather_finish(later_refs, later_shapes, later_axes, g_send, g_recv))
        _load_weights([(win_hbm, win_v), (wa_hbm, wa_v), (wb_hbm, wb_v), (wo_hbm, wo_v)], sems)

        @pl.when(pl.program_id(0) == 0)
        def _():
            cv_ext[0:HALO_SHORT, :] = jnp.zeros((HALO_SHORT, d), F32)
            u0_ext[0:HALO_CONF, :] = jnp.zeros((HALO_CONF, d), F32)

        xv = x_ref[...]
        sh1, sc1, g1 = mod_ref[:, 0:d], mod_ref[:, d:2 * d], mod_ref[:, 2 * d:3 * d]
        r = lax.rsqrt(jnp.mean(xv * xv, axis=-1, keepdims=True) + EPS)
        hb = (((xv * r) * gn1_ref[...]) * (1.0 + sc1) + sh1).astype(BF16)
        h_ref[...] = hb

        def pj(k):
            p = _dot(hb, win_v[:, k * d:(k + 1) * d])
            proj_ref[:, k * d:(k + 1) * d] = p.astype(BF16)
            return p

        v_c = pj(3)
        u0_ext[HALO_CONF:HALO_CONF + ts, :] = v_c * _sig(pj(4))

        def conv_chunk(tap, r0, lo):
            acc = jnp.zeros((CONV_ROWS, CONV_LANES), F32)
            for k in range(CONF_K):
                acc = acc + w31_ref[k:k + 1, lo:lo + CONV_LANES] * tap(HALO_CONF - (CONF_K - 1) + k)
            u1_buf[pl.ds(r0, CONV_ROWS), lo:lo + CONV_LANES] = acc

        pjs = {}
        _conv_chunks(u0_ext, e_buf, ts, d, conv_chunk, unroll=True,
                     between=[lambda k=k: pjs.__setitem__(k, pj(k)) for k in (1, 2, 0, 5, 6)])
        u0_ext[0:HALO_CONF, :] = u0_ext[ts:ts + HALO_CONF, :]

        cv_ext[HALO_SHORT:HALO_SHORT + ts, :] = pjs[1] * pjs[2]
        conv3 = jnp.zeros((ts, d), F32)
        for k in range(SHORT_K):
            off = HALO_SHORT - (SHORT_K - 1) + k
            conv3 = conv3 + w3_ref[k:k + 1, :] * cv_ext[off:off + ts, :]
        cv_ext[0:HALO_SHORT, :] = cv_ext[ts:ts + HALO_SHORT, :]
        c3_ref[...] = conv3.astype(BF16)
        yap = (pjs[0] * conv3).astype(BF16)
        lhs3_ref[:, 0:d] = yap
        y_a = _dot(yap, wa_v[...])
        ya_ref[...] = y_a.astype(BF16)

        u1 = u1_buf[...] + cb_ref[...]
        mu = jnp.mean(u1, axis=-1, keepdims=True)
        var = jnp.mean(jnp.square(u1 - mu), axis=-1, keepdims=True)
        rstd = lax.rsqrt(var + LN_EPS)
        nrm = (u1 - mu) * rstd
        n_ref[...] = nrm.astype(BF16)
        rstd_ref[...] = jnp.broadcast_to(rstd, (ts, LANES))
        u2 = nrm * lng_ref[...] + lnb_ref[...]
        u3 = (u2 * _sig(u2)).astype(BF16)
        lhs3_ref[:, d:2 * d] = u3
        y_b = _dot(u3, wb_v[...])
        yb_ref[...] = y_b.astype(BF16)

        merged = (_sig(pjs[5]) * y_a + _sig(pjs[6]) * y_b).astype(BF16)
        lhs3_ref[:, 2 * d:3 * d] = merged
        mix = _dot(merged, wo_v[...])
        mix_ref[...] = mix
        x1_ref[...] = xv + g1 * mix

    row = lambda w, dt=None: pl.BlockSpec((ts, w), lambda i: (i, 0))
    full = lambda a: pl.BlockSpec(a.shape, lambda i: (0, 0))
    sd = lambda w, dt: jax.ShapeDtypeStruct((s_len, w), dt)
    outs = [sd(7 * d, BF16), sd(d, BF16), sd(3 * d, BF16), sd(d, BF16), sd(d, BF16), sd(LANES, F32), sd(d, BF16), sd(d, BF16),
            sd(d, F32), sd(d, F32)]
    res = _pallas(
        body, name="fwd_mix", grid=(n_t,),
        out_shape=tuple(outs) + tuple(jax.ShapeDtypeStruct(w.shape, BF16) for w in later),
        in_specs=[row(d), full(mod), full(gn1), full(w3), full(w31), full(cbias), full(lng), full(lnb), ANY, ANY, ANY, ANY]
        + [ANY] * n_l,
        out_specs=tuple(row(o.shape[1]) for o in outs) + tuple([ANY] * n_l),
        input_output_aliases={12 + k: len(outs) + k for k in range(n_l)},
        scratch_shapes=[pltpu.VMEM(w_in.shape, BF16), pltpu.VMEM(w_a.shape, BF16), pltpu.VMEM(w_b.shape, BF16),
                        pltpu.VMEM(w_o.shape, BF16), pltpu.VMEM((ts + HALO_SHORT, d), F32),
                        pltpu.VMEM((ts + HALO_CONF, d), F32), pltpu.VMEM((7, ts + HALO_CONF - 8, d // 2), F32),
                        pltpu.VMEM((ts, d), F32), pltpu.SemaphoreType.DMA((4,))] + _gather_sems(n_l),
        compiler_params=_params("arbitrary"),
    )(x, mod, gn1, w3, w31, cbias, lng, lnb, w_in, w_a, w_b, w_o, *later)
    return res[:len(outs)], res[len(outs):]


def _ffn(x1, target, mod, gn2, gf, w_fi, w_fo, ts):
    s_len, d = x1.shape
    f = w_fo.shape[0]
    n_t = s_len // ts

    def body(x1_ref, t_ref, mod_ref, gn2_ref, gf_ref, wfi_hbm, wfo_hbm,
             h2_ref, hid_ref, df_ref, dab_ref, dx1_ref, acc_ref, wfi_v, wfo_v, sems):
        _load_weights([(wfi_hbm, wfi_v), (wfo_hbm, wfo_v)], sems)

        @pl.when(pl.program_id(0) == 0)
        def _():
            acc_ref[...] = jnp.zeros(acc_ref.shape, F32)

        x1v = x1_ref[...]
        sh2, sc2, g2 = mod_ref[:, 3 * d:4 * d], mod_ref[:, 4 * d:5 * d], mod_ref[:, 5 * d:6 * d]
        gn2v, gfv = gn2_ref[...], gf_ref[...]
        r2 = lax.rsqrt(jnp.mean(x1v * x1v, axis=-1, keepdims=True) + EPS)
        xn2 = x1v * r2
        h2 = ((xn2 * gn2v) * (1.0 + sc2) + sh2).astype(BF16)
        h2_ref[...] = h2
        a = _dot(h2, wfi_v[:, 0:f])
        bg = _dot(h2, wfi_v[:, f:2 * f])
        sa = _sig(a)
        silu_a = a * sa
        hid = (silu_a * bg).astype(BF16)
        hid_ref[...] = hid
        ffn = _dot(hid, wfo_v[...])
        x2 = x1v + g2 * ffn
        r3 = lax.rsqrt(jnp.mean(x2 * x2, axis=-1, keepdims=True) + EPS)
        xn3 = x2 * r3
        diff = xn3 * gfv - t_ref[...]
        dy = diff * (1.0 / d)
        dxn3 = dy * gfv
        dx2 = r3 * (dxn3 - xn3 * jnp.mean(dxn3 * xn3, axis=-1, keepdims=True))
        acc_ref[ROW_GF:ROW_GF + 1, :] += _colsum(dy * xn3)
        acc_ref[ROW_DMOD + 5:ROW_DMOD + 5 + 1, :] += _colsum(dx2 * ffn)
        acc_ref[ROW_SQERR:ROW_SQERR + 1, :] += _colsum(diff * diff)

        df = (dx2 * g2).astype(BF16)
        df_ref[...] = df
        dhid = _dot_nt(df, wfo_v[...])
        da = dhid * bg * (sa * (1.0 + a * (1.0 - sa)))
        dbg = dhid * silu_a
        dab_ref[:, 0:f] = da.astype(BF16)
        dab_ref[:, f:2 * f] = dbg.astype(BF16)
        dh2 = _dot_nt(da.astype(BF16), wfi_v[:, 0:f]) + _dot_nt(dbg.astype(BF16), wfi_v[:, f:2 * f])
        acc_ref[ROW_DMOD + 3:ROW_DMOD + 3 + 1, :] += _colsum(dh2)
        acc_ref[ROW_DMOD + 4:ROW_DMOD + 4 + 1, :] += _colsum(dh2 * (xn2 * gn2v))
        acc_ref[ROW_GN2:ROW_GN2 + 1, :] += _colsum(dh2 * xn2 * (1.0 + sc2))
        dxn2 = dh2 * gn2v * (1.0 + sc2)
        dx1_ref[...] = dx2 + r2 * (dxn2 - xn2 * jnp.mean(dxn2 * xn2, axis=-1, keepdims=True))

    row = lambda w: pl.BlockSpec((ts, w), lambda i: (i, 0))
    full = lambda a: pl.BlockSpec(a.shape, lambda i: (0, 0))
    sd = lambda w, dt: jax.ShapeDtypeStruct((s_len, w), dt)
    outs = [sd(d, BF16), sd(f, BF16), sd(d, BF16), sd(2 * f, BF16), sd(d, F32), jax.ShapeDtypeStruct((N_SMALL_ROWS, d), F32)]
    return _pallas(
        body, name="ffn", grid=(n_t,), out_shape=tuple(outs),
        in_specs=[row(d), row(d), full(mod), full(gn2), full(gf), ANY, ANY],
        out_specs=tuple([row(o.shape[1]) for o in outs[:-1]] + [pl.BlockSpec((N_SMALL_ROWS, d), lambda i: (0, 0))]),
        scratch_shapes=[pltpu.VMEM(w_fi.shape, BF16), pltpu.VMEM(w_fo.shape, BF16), pltpu.SemaphoreType.DMA((2,))],
        compiler_params=_params("arbitrary"),
    )(x1, target, mod, gn2, gf, w_fi, w_fo)


def _bwd_mix(dx1, proj, nrm, rstd, c3, ya, yb, mix, mod, w3, w31, lng, lnb, w_a, w_b, w_o, ready, ready_axes, ts):
    s_len, d = dx1.shape
    n_t = s_len // ts
    n_r = len(ready)
    ready_shapes = [g.shape for g in ready]

    def body(*refs):
        (dx1_ref, proj_ref, n_ref, rstd_ref, c3_ref, ya_ref, yb_ref, mix_ref, mod_ref, w3_ref, w31_ref, lng_ref,
         lnb_ref, wa_hbm, wb_hbm, wo_hbm) = refs[:16]
        ready_refs = refs[16:16 + n_r]
        dproj_ref, rhs3_ref, acc_ref = refs[16 + n_r:19 + n_r]
        recv_refs = refs[19 + n_r:19 + 2 * n_r]
        wa_v, wb_v, wo_v, dc3_ext, du1_ext, e_buf, u0_buf, du0_buf, dw8, sems, e_send, e_recv = refs[19 + 2 * n_r:]
        step = pl.program_id(0)
        pl.when(step == 0)(lambda: _exchange_send(ready_refs, recv_refs, ready_shapes, ready_axes, e_send, e_recv))
        pl.when(step == n_t - 1)(lambda: _exchange_finish(recv_refs, e_send, e_recv))
        _load_weights([(wa_hbm, wa_v), (wb_hbm, wb_v), (wo_hbm, wo_v)], sems)

        @pl.when(pl.program_id(0) == 0)
        def _():
            acc_ref[...] = jnp.zeros(acc_ref.shape, F32)
            dc3_ext[ts:ts + HALO_SHORT, :] = jnp.zeros((HALO_SHORT, d), F32)
            du1_ext[ts:ts + HALO_CONF, :] = jnp.zeros((HALO_CONF, d), F32)
            dw8[...] = jnp.zeros(dw8.shape, F32)

        pj = lambda k: proj_ref[:, k * d:(k + 1) * d].astype(F32)
        g1 = mod_ref[:, 2 * d:3 * d]
        dx1v = dx1_ref[...]
        acc_ref[ROW_DMOD + 2:ROW_DMOD + 2 + 1, :] += _colsum(dx1v * mix_ref[...])
        dmix = (dx1v * g1).astype(BF16)
        rhs3_ref[:, 2 * d:3 * d] = dmix
        dmerged = _dot_nt(dmix, wo_v[...])
        g_a, g_b = _sig(pj(5)), _sig(pj(6))
        y_a, y_b = ya_ref[...].astype(F32), yb_ref[...].astype(F32)
        dproj_ref[:, 5 * d:6 * d] = (dmerged * y_a * g_a * (1.0 - g_a)).astype(BF16)
        dproj_ref[:, 6 * d:7 * d] = (dmerged * y_b * g_b * (1.0 - g_b)).astype(BF16)
        dya = (dmerged * g_a).astype(BF16)
        dyb = (dmerged * g_b).astype(BF16)
        rhs3_ref[:, 0:d] = dya
        rhs3_ref[:, d:2 * d] = dyb

        dyap = _dot_nt(dya, wa_v[...])
        dproj_ref[:, 0:d] = (dyap * c3_ref[...].astype(F32)).astype(BF16)
        dc3_ext[0:ts, :] = dyap * pj(0)
        c_s, v_s = pj(1), pj(2)
        cv = c_s * v_s
        dcv = jnp.zeros((ts, d), F32)
        for j in range(SHORT_K):
            k = SHORT_K - 1 - j
            sj = dc3_ext[j:j + ts, :]
            dcv = dcv + w3_ref[k:k + 1, :] * sj
            acc_ref[ROW_W3 + k:ROW_W3 + k + 1, :] += _colsum(cv * sj)
        dc3_ext[ts:ts + HALO_SHORT, :] = dc3_ext[0:HALO_SHORT, :]
        dproj_ref[:, d:2 * d] = (dcv * v_s).astype(BF16)
        dproj_ref[:, 2 * d:3 * d] = (dcv * c_s).astype(BF16)

        du3 = _dot_nt(dyb, wb_v[...])
        nv = n_ref[...].astype(F32)
        lg = lng_ref[...]
        u2 = nv * lg + lnb_ref[...]
        s2 = _sig(u2)
        du2 = du3 * (s2 * (1.0 + u2 * (1.0 - s2)))
        acc_ref[ROW_LNG:ROW_LNG + 1, :] += _colsum(du2 * nv)
        acc_ref[ROW_LNB:ROW_LNB + 1, :] += _colsum(du2)
        dn = du2 * lg
        du1 = rstd_ref[:, 0:1] * (dn - jnp.mean(dn, axis=-1, keepdims=True) - nv * jnp.mean(dn * nv, axis=-1, keepdims=True))
        acc_ref[ROW_CB:ROW_CB + 1, :] += _colsum(du1)
        du1_ext[0:ts, :] = du1
        v_c = pj(3)
        sg = _sig(pj(4))
        u0_buf[...] = v_c * sg

        def conv_chunk(tap, r0, lo):
            lanes = slice(lo, lo + CONV_LANES)
            u0c = u0_buf[pl.ds(r0, CONV_ROWS), lanes]
            acc = jnp.zeros((CONV_ROWS, CONV_LANES), F32)
            for j in range(CONF_K):
                k = CONF_K - 1 - j
                sj = tap(j)
                acc = acc + w31_ref[k:k + 1, lanes] * sj
                prod = u0c * sj
                part = prod[0:8]
                for i in range(1, CONV_ROWS // 8):
                    part = part + prod[8 * i:8 * i + 8]
                dw8[k, :, lanes] += part
            du0_buf[pl.ds(r0, CONV_ROWS), lanes] = acc

        _conv_chunks(du1_ext, e_buf, ts, d, conv_chunk, unroll=False)
        du0 = du0_buf[...]
        du1_ext[ts:ts + HALO_CONF, :] = du1_ext[0:HALO_CONF, :]

        @pl.when(step == n_t - 1)
        def _():
            for k in range(CONF_K):
                acc_ref[ROW_W31 + k:ROW_W31 + k + 1, :] = _colsum(dw8[k])
        dproj_ref[:, 3 * d:4 * d] = (du0 * sg).astype(BF16)
        dproj_ref[:, 4 * d:5 * d] = (du0 * v_c * sg * (1.0 - sg)).astype(BF16)

    rev = lambda w: pl.BlockSpec((ts, w), lambda i: (n_t - 1 - i, 0))
    full = lambda a: pl.BlockSpec(a.shape, lambda i: (0, 0))
    sd = lambda w, dt: jax.ShapeDtypeStruct((s_len, w), dt)
    outs = [sd(7 * d, BF16), sd(3 * d, BF16), jax.ShapeDtypeStruct((N_SMALL_ROWS, d), F32)]
    res = _pallas(
        body, name="bwd_mix", grid=(n_t,), out_shape=tuple(outs) + _exchange_out_shapes(ready, ready_axes),
        in_specs=[rev(d), rev(7 * d), rev(d), rev(LANES), rev(d), rev(d), rev(d), rev(d), full(mod), full(w3), full(w31),
                  full(lng), full(lnb), ANY, ANY, ANY] + [ANY] * n_r,
        out_specs=tuple([rev(o.shape[1]) for o in outs[:-1]] + [pl.BlockSpec((N_SMALL_ROWS, d), lambda i: (0, 0))] + [ANY] * n_r),
        scratch_shapes=[pltpu.VMEM(w_a.shape, BF16), pltpu.VMEM(w_b.shape, BF16), pltpu.VMEM(w_o.shape, BF16),
                        pltpu.VMEM((ts + HALO_SHORT, d), F32), pltpu.VMEM((ts + HALO_CONF, d), F32),
                        pltpu.VMEM((7, ts + HALO_CONF - 8, d // 2), F32), pltpu.VMEM((ts, d), F32), pltpu.VMEM((ts, d), F32),
                        pltpu.VMEM((CONF_K, 8, d), F32), pltpu.SemaphoreType.DMA((3,))] + _exchange_sems(n_r),
        compiler_params=_params("arbitrary"),
    )(dx1, proj, nrm, rstd, c3, ya, yb, mix, mod, w3, w31, lng, lnb, w_a, w_b, w_o, *ready)
    return res[:len(outs)], res[len(outs):]


def _bwd_in(dproj, x, dx1, mod, gn1, w_in, ready, ready_axes, ts):
    s_len, d = x.shape
    n_t = s_len // ts
    n_r = len(ready)
    ready_shapes = [g.shape for g in ready]

    def body(*refs):
        dproj_ref, x_ref, dx1_ref, mod_ref, gn1_ref, win_hbm = refs[:6]
        ready_refs = refs[6:6 + n_r]
        gx_ref, acc_ref = refs[6 + n_r:8 + n_r]
        recv_refs = refs[8 + n_r:8 + 2 * n_r]
        win_v, sems, e_send, e_recv = refs[8 + 2 * n_r:]
        step = pl.program_id(0)
        pl.when(step == 0)(lambda: _exchange_send(ready_refs, recv_refs, ready_shapes, ready_axes, e_send, e_recv))
        _load_weights([(win_hbm, win_v)], sems)

        @pl.when(pl.program_id(0) == 0)
        def _():
            acc_ref[...] = jnp.zeros(acc_ref.shape, F32)

        xv = x_ref[...]
        sc1 = mod_ref[:, d:2 * d]
        gn1v = gn1_ref[...]
        r = lax.rsqrt(jnp.mean(xv * xv, axis=-1, keepdims=True) + EPS)
        xn = xv * r
        dh = _dot_nt(dproj_ref[...], win_v[...])
        acc_ref[ROW_DMOD:ROW_DMOD + 1, :] += _colsum(dh)
        acc_ref[ROW_DMOD + 1:ROW_DMOD + 1 + 1, :] += _colsum(dh * (xn * gn1v))
        acc_ref[ROW_GN1:ROW_GN1 + 1, :] += _colsum(dh * xn * (1.0 + sc1))
        dxn = dh * gn1v * (1.0 + sc1)
        gx_ref[...] = dx1_ref[...] + r * (dxn - xn * jnp.mean(dxn * xn, axis=-1, keepdims=True))
        pl.when(step == n_t - 1)(lambda: _exchange_finish(recv_refs, e_send, e_recv))

    row = lambda w: pl.BlockSpec((ts, w), lambda i: (i, 0))
    full = lambda a: pl.BlockSpec(a.shape, lambda i: (0, 0))
    res = _pallas(
        body, name="bwd_in", grid=(n_t,),
        out_shape=(jax.ShapeDtypeStruct((s_len, d), F32), jax.ShapeDtypeStruct((N_SMALL_ROWS, d), F32))
        + _exchange_out_shapes(ready, ready_axes),
        in_specs=[row(7 * d), row(d), row(d), full(mod), full(gn1), ANY] + [ANY] * n_r,
        out_specs=(row(d), pl.BlockSpec((N_SMALL_ROWS, d), lambda i: (0, 0))) + tuple([ANY] * n_r),
        scratch_shapes=[pltpu.VMEM(w_in.shape, BF16), pltpu.SemaphoreType.DMA((1,))] + _exchange_sems(n_r),
        compiler_params=_params("arbitrary"),
    )(dproj, x, dx1, mod, gn1, w_in, *ready)
    return res[0], res[1], res[2:]


def _weight_grad(a, b, bn, name, transposed=False, ready=(), ready_axes=(), a_cols=None):
    s_len, m = a.shape[0], (a_cols or a.shape[1])
    n = b.shape[1]
    n_j = n // bn
    n_r = len(ready)
    ready_shapes = [g.shape for g in ready]

    def body(*refs):
        a_hbm, b_ref = refs[:2]
        ready_refs = refs[2:2 + n_r]
        o_ref = refs[2 + n_r]
        recv_refs = refs[3 + n_r:3 + 2 * n_r]
        a_v, sem = refs[3 + 2 * n_r:5 + 2 * n_r]
        step = pl.program_id(0)
        if n_r:
            e_send, e_recv = refs[5 + 2 * n_r:]
            pl.when(step == 0)(lambda: _exchange_send(ready_refs, recv_refs, ready_shapes, ready_axes, e_send, e_recv))
            pl.when(step == n_j - 1)(lambda: _exchange_finish(recv_refs, e_send, e_recv))
        if a_cols is None:
            _load_weights([(a_hbm, a_v)], sem)
            prod = _dot_tn(a_v[...], b_ref[...])
        else:
            prod = _dot_tn(a_hbm[...], b_ref[...])
        o_ref[...] = (prod.T if transposed else prod).astype(BF16)

    out = jax.ShapeDtypeStruct((n, m) if transposed else (m, n), BF16)
    out_spec = pl.BlockSpec((bn, m), lambda j: (j, 0)) if transposed else pl.BlockSpec((m, bn), lambda j: (0, j))
    res = _pallas(
        body, name=name, grid=(n_j,), out_shape=(out,) + _exchange_out_shapes(ready, ready_axes),
        in_specs=[ANY if a_cols is None else pl.BlockSpec((s_len, a_cols), lambda j: (0, (j * bn) // a_cols)),
                  pl.BlockSpec((s_len, bn), lambda j: (0, j))] + [ANY] * n_r,
        out_specs=(out_spec,) + tuple([ANY] * n_r),
        scratch_shapes=[pltpu.VMEM((8, 128) if a_cols else a.shape, BF16), pltpu.SemaphoreType.DMA((1,))]
        + (_exchange_sems(n_r) if n_r else []),
        compiler_params=_params("arbitrary"),
    )(a, b, *ready)
    return res[0], res[1:]


def _sum_slots(recv, grad, axis, chip, core, name):
    n_s, r, c = recv.shape
    tr = 16
    for cand in (256, 128, 176, 64, 32, 16):
        if r % cand == 0:
            tr = cand
            break
    n_i = r // tr

    def body(s_ref, own_ref, r_ref, o_ref):
        tot = own_ref[...].astype(F32)
        for k in range(n_s):
            tot = tot + r_ref[k].astype(F32)
        o_ref[...] = tot

    if axis == 0:
        own_map = lambda i, s: ((2 * s[0] + s[1]) * n_i + i, 0)
    else:
        own_map = lambda i, s: (s[1] * n_i + i, s[0])
    grid_spec = pltpu.PrefetchScalarGridSpec(
        num_scalar_prefetch=1, grid=(n_i,),
        in_specs=[pl.BlockSpec((tr, c), own_map), pl.BlockSpec((n_s, tr, c), lambda i, s: (0, i, 0))],
        out_specs=pl.BlockSpec((tr, c), lambda i, s: (s[1] * n_i + i, 0)))
    return _pallas(body, name=name, grid_spec=grid_spec, out_shape=jax.ShapeDtypeStruct((2 * r, c), F32),
                   compiler_params=_params("arbitrary"))(jnp.stack([chip, core]).astype(jnp.int32), grad, recv)


def _adamw_math(w, g, m, v):
    m = ADAM_B1 * m + (1.0 - ADAM_B1) * g
    v = ADAM_B2 * v + (1.0 - ADAM_B2) * (g * g)
    m_hat = m / (1.0 - ADAM_B1 ** ADAM_STEP)
    v_hat = v / (1.0 - ADAM_B2 ** ADAM_STEP)
    delta = -ADAM_LR * (m_hat / (jnp.sqrt(v_hat) + ADAM_EPS) + ADAM_WD * w)
    return delta, m, v


def _adamw(w, g, m, v, name):
    r, c = w.shape
    tr = r
    for cand in (256, 128, 176, 64, 32, 16, 8):
        if r % cand == 0:
            tr = cand
            break

    def body(w_ref, g_ref, m_ref, v_ref, go_ref, d_ref, nm_ref, nv_ref):
        g = g_ref[...]
        go_ref[...] = g
        d_ref[...], nm_ref[...], nv_ref[...] = _adamw_math(w_ref[...], g, m_ref[...], v_ref[...])

    blk = pl.BlockSpec((tr, c), lambda i: (i, 0))
    return _pallas(body, name=name, grid=(r // tr,), out_shape=tuple(jax.ShapeDtypeStruct((r, c), F32) for _ in range(4)),
                   in_specs=[blk] * 4, out_specs=(blk,) * 4, compiler_params=_params("arbitrary"))(w, g, m, v)


_SMALL = [
    ("b_ada", ROW_DMOD, "mod"), ("norm_mix_g", ROW_GN1, "row"), ("conv_short_w", ROW_W3, "shard"),
    ("conv_conf_w", ROW_W31, "shard"), ("conv_conf_b", ROW_CB, "row"), ("conf_ln_g", ROW_LNG, "row"),
    ("conf_ln_b", ROW_LNB, "row"), ("norm_ffn_g", ROW_GN2, "row"), ("final_norm_g", ROW_GF, "row")]


def _small_update(p_all, chip_onehot, ws, ms, vs):
    n = len(_SMALL)
    d = p_all.shape[2]

    def body(*refs):
        p_ref, oh_ref = refs[:2]
        w_refs, m_refs, v_refs = refs[2:2 + n], refs[2 + n:2 + 2 * n], refs[2 + 2 * n:2 + 3 * n]
        loss_ref = refs[2 + 3 * n]
        g_out, d_out, m_out, v_out = [refs[3 + (3 + q) * n:3 + (4 + q) * n] for q in range(4)]
        g_all = p_ref[0]
        for s in range(1, N_DEV):
            g_all = g_all + p_ref[s]
        loss_ref[...] = jnp.zeros(loss_ref.shape, F32) + (0.5 / d) * jnp.sum(g_all[ROW_SQERR:ROW_SQERR + 1, :])

        def emit(i, idx, g):
            dl, nm, nv = _adamw_math(w_refs[i][idx], g, m_refs[i][idx], v_refs[i][idx])
            g_out[i][idx], d_out[i][idx], m_out[i][idx], v_out[i][idx] = g, dl, nm, nv

        everything = (slice(None), slice(None))
        for i, (_, row, kind) in enumerate(_SMALL):
            if kind == "mod":
                for k in range(N_MOD):
                    emit(i, (slice(0, 1), slice(k * d, (k + 1) * d)), g_all[row + k:row + k + 1, :])
            elif kind == "row":
                emit(i, everything, g_all[row:row + 1, :])
            else:
                _, taps, dq = w_refs[i].shape
                g = jnp.zeros((taps, dq), F32)
                for j in range(N_CHIP):
                    g = g + oh_ref[0:1, j:j + 1] * g_all[row:row + taps, j * dq:(j + 1) * dq]
                emit(i, (0, slice(None), slice(None)), g)

    shapes = [jax.ShapeDtypeStruct(w.shape, F32) for w in ws]
    res = _pallas(body, name="small_update", out_shape=tuple([jax.ShapeDtypeStruct((8, 128), F32)] + shapes * 4),
                  in_specs=[VMEM_FULL] * (2 + 3 * n), out_specs=tuple([VMEM_FULL] * (1 + 4 * n)),
                  compiler_params=_params())(p_all, chip_onehot, *ws, *ms, *vs)
    return res[0], [res[1 + q * n:1 + (q + 1) * n] for q in range(4)]


def _adamw_w_ada(w, c_t, dmod_part, m, v):
    r, c = w.shape
    tr = min(256, r)

    def body(w_ref, c_ref, dm_ref, m_ref, v_ref, go_ref, d_ref, nm_ref, nv_ref):
        cv = c_ref[...]
        g = _dot((cv * _sig(cv)).astype(BF16), dm_ref[...].astype(BF16))
        go_ref[...] = g
        d_ref[...], nm_ref[...], nv_ref[...] = _adamw_math(w_ref[...], g, m_ref[...], v_ref[...])

    blk = pl.BlockSpec((tr, c), lambda i: (i, 0))
    return _pallas(body, name="adamw_w_ada", grid=(r // tr,), out_shape=tuple(jax.ShapeDtypeStruct((r, c), F32) for _ in range(4)),
                   in_specs=[blk, pl.BlockSpec((tr, c_t.shape[1]), lambda i: (i, 0)),
                             pl.BlockSpec(dmod_part.shape, lambda i: (0, 0)), blk, blk],
                   out_specs=(blk,) * 4, compiler_params=_params("arbitrary"))(w, c_t, dmod_part, m, v)


def _step(x, c, w_ada, b_ada, norm_mix_g, w_in, conv_short_w, w_short_out, conv_conf_w, conv_conf_b, conf_ln_g,
          conf_ln_b, w_conf_out, w_o, norm_ffn_g, w_ffn_in, w_ffn_out, final_norm_g, loss_target, moments, ts):
    xi, yi, ci = _place()
    chip = 2 * xi + yi
    me = 4 * xi + 2 * yi + ci
    x2d, tgt = x[0], loss_target[0]
    s_len, d = x2d.shape
    dq = d // N_CHIP
    mq = N_MOD * d // N_CHIP
    gf = final_norm_g.reshape(1, d)

    taps = jnp.concatenate([jnp.pad(conv_short_w[0], ((0, 8 - SHORT_K), (0, 0))),
                            jnp.pad(conv_conf_w[0], ((0, 32 - CONF_K), (0, 0)))], axis=0)
    n_tr = 40 * dq // d
    c_taps = jnp.concatenate([jnp.broadcast_to(c, (8, d)), taps.reshape(n_tr, d), jnp.zeros((16 - n_tr, d), F32)], axis=0)
    b_part = lax.dynamic_slice(b_ada, (0, chip * mq), (1, mq))
    big = [w_in[0], w_short_out[0], w_conf_out[0], w_o[0], w_ffn_in[0], w_ffn_out[0]]
    axes = [1, 0, 0, 0, 1, 0]
    big_names = ["w_in", "w_short_out", "w_conf_out", "w_o", "w_ffn_in", "w_ffn_out"]
    (win_f, wa_f, wb_f, wo_f, *placed_ffn), c_taps, mod_all = _gather_weights(big, axes, 4, c_taps, w_ada[0], b_part,
                                                                              "gather_weights")
    c_taps = c_taps.reshape(N_DEV, 24, d)
    c_all = c_taps[:, 0]
    mod_all = mod_all.reshape(N_CHIP, 2, N_DEV, mq)[:, 0]
    mod_full = jnp.transpose(mod_all, (1, 0, 2)).reshape(N_DEV, N_MOD * d)
    mod = lax.dynamic_slice(mod_full, (me, 0), (1, N_MOD * d))

    taps_all = c_taps[:, 8:8 + n_tr].reshape(N_CHIP, 2, 40, dq)[:, 0]
    taps_full = jnp.transpose(taps_all, (1, 0, 2)).reshape(40, d)
    w3, w31 = taps_full[0:SHORT_K], taps_full[8:8 + CONF_K]

    (proj, h, lhs3, c3, nrm, rstd, ya, yb, mix, x1), (wfi_f, wfo_f) = _fwd_mix(
        x2d, mod, norm_mix_g, w3, w31, conv_conf_b, conf_ln_g, conf_ln_b, win_f, wa_f, wb_f, wo_f, placed_ffn, axes[4:], ts)
    h2, hid, df, dab, dx1, acc_f = _ffn(x1, tgt, mod, norm_ffn_g, gf, wfi_f, wfo_f, ts)
    f = wfo_f.shape[0]
    bn = min(512, d)
    g_ffn = [_weight_grad(h2, dab, bn, "grad_w_ffn_in")[0], _weight_grad(df, hid, bn // 2, "grad_w_ffn_out", transposed=True)[0]]
    (dproj, rhs3, acc_m), r_ffn = _bwd_mix(dx1, proj, nrm, rstd, c3, ya, yb, mix, mod, w3, w31, conf_ln_g, conf_ln_b,
                                          wa_f, wb_f, wo_f, g_ffn, axes[4:], ts)
    g3 = _weight_grad(lhs3, rhs3, bn // 2, "grad_w_sq", a_cols=d)[0]
    g_sq = [g3[:, k * d:(k + 1) * d] for k in range(3)]
    g_in, r_sq = _weight_grad(h, dproj, bn, "grad_w_in", ready=g_sq, ready_axes=axes[1:4])
    grad_x, acc_i, r_in = _bwd_in(dproj, x2d, dx1, mod, norm_mix_g, win_f, [g_in], axes[:1], min(2 * ts, s_len))
    g_in = [g_in]
    r_in_sq = list(r_in) + list(r_sq)
    grads = g_in + g_sq + g_ffn
    recv = list(r_in_sq) + list(r_ffn)
    g_big, p_all = _swap_halves([_sum_slots(r, g, ax, chip, ci, "sum_slots_" + n)
                                 for r, g, ax, n in zip(recv, grads, axes, big_names)], acc_f + acc_m + acc_i)
    p_all = p_all.reshape(N_DEV, N_SMALL_ROWS, d)

    small_w = dict(b_ada=b_ada, norm_mix_g=norm_mix_g, conv_short_w=conv_short_w, conv_conf_w=conv_conf_w,
                   conv_conf_b=conv_conf_b, conf_ln_g=conf_ln_g, conf_ln_b=conf_ln_b, norm_ffn_g=norm_ffn_g,
                   final_norm_g=final_norm_g)
    as_2d = lambda t: t if t.ndim == 3 else t.reshape((-1, t.shape[-1]))
    chip_onehot = (lax.broadcasted_iota(jnp.int32, (1, 128), 1) == chip).astype(F32)
    loss_tile, small_out = _small_update(
        p_all, chip_onehot, [as_2d(small_w[n]) for n, _, _ in _SMALL], [as_2d(moments["m_" + n]) for n, _, _ in _SMALL],
        [as_2d(moments["v_" + n]) for n, _, _ in _SMALL])
    small_g, small_d, small_m, small_v = [{n: o.reshape(small_w[n].shape) for (n, _, _), o in zip(_SMALL, outs)}
                                          for outs in small_out]

    dmod_all = p_all[:, 0:N_MOD].reshape(N_DEV, N_MOD * d)
    dmod_part = jnp.pad(lax.dynamic_slice(dmod_all, (0, chip * mq), (N_DEV, mq)), ((0, 128 - N_DEV), (0, 0)))
    c_t = jnp.pad(jnp.transpose(c_all), ((0, 0), (0, 128 - N_DEV)))
    out_g, out_d, out_m, out_v = dict(small_g), dict(small_d), dict(small_m), dict(small_v)
    updates = [("w_ada", _adamw_w_ada(w_ada[0], c_t, dmod_part, moments["m_w_ada"][0], moments["v_w_ada"][0]))]
    updates += [(name, _adamw(w, g, moments["m_" + name][0], moments["v_" + name][0], "adamw_" + name))
                for name, w, g in zip(big_names, big, g_big)]
    for name, (go, dl, nm, nv) in updates:
        out_g[name], out_d[name], out_m[name], out_v[name] = go[None], dl[None], nm[None], nv[None]

    loss = loss_tile[0, 0]
    return loss, grad_x[None], out_g, out_d, out_m, out_v


_WEIGHTS = ["w_ada", "b_ada", "norm_mix_g", "w_in", "conv_short_w", "w_short_out", "conv_conf_w", "conv_conf_b",
            "conf_ln_g", "conf_ln_b", "w_conf_out", "w_o", "norm_ffn_g", "w_ffn_in", "w_ffn_out", "final_norm_g"]
ROW_TILE = 256


def kernel(x, c, w_ada, b_ada, norm_mix_g, w_in, conv_short_w, w_short_out, conv_conf_w, conv_conf_b, conf_ln_g, conf_ln_b, w_conf_out, w_o, norm_ffn_g, w_ffn_in, w_ffn_out, final_norm_g, loss_target, m_w_ada, m_b_ada, m_norm_mix_g, m_w_in, m_conv_short_w, m_w_short_out, m_conv_conf_w, m_conv_conf_b, m_conf_ln_g, m_conf_ln_b, m_w_conf_out, m_w_o, m_norm_ffn_g, m_w_ffn_in, m_w_ffn_out, m_final_norm_g, v_w_ada, v_b_ada, v_norm_mix_g, v_w_in, v_conv_short_w, v_w_short_out, v_conv_conf_w, v_conv_conf_b, v_conf_ln_g, v_conf_ln_b, v_w_conf_out, v_w_o, v_norm_ffn_g, v_w_ffn_in, v_w_ffn_out, v_final_norm_g):
    moments = dict(
        m_w_ada=m_w_ada, m_b_ada=m_b_ada, m_norm_mix_g=m_norm_mix_g, m_w_in=m_w_in, m_conv_short_w=m_conv_short_w,
        m_w_short_out=m_w_short_out, m_conv_conf_w=m_conv_conf_w, m_conv_conf_b=m_conv_conf_b, m_conf_ln_g=m_conf_ln_g,
        m_conf_ln_b=m_conf_ln_b, m_w_conf_out=m_w_conf_out, m_w_o=m_w_o, m_norm_ffn_g=m_norm_ffn_g, m_w_ffn_in=m_w_ffn_in,
        m_w_ffn_out=m_w_ffn_out, m_final_norm_g=m_final_norm_g,
        v_w_ada=v_w_ada, v_b_ada=v_b_ada, v_norm_mix_g=v_norm_mix_g, v_w_in=v_w_in, v_conv_short_w=v_conv_short_w,
        v_w_short_out=v_w_short_out, v_conv_conf_w=v_conv_conf_w, v_conv_conf_b=v_conv_conf_b, v_conf_ln_g=v_conf_ln_g,
        v_conf_ln_b=v_conf_ln_b, v_w_conf_out=v_w_conf_out, v_w_o=v_w_o, v_norm_ffn_g=v_norm_ffn_g, v_w_ffn_in=v_w_ffn_in,
        v_w_ffn_out=v_w_ffn_out, v_final_norm_g=v_final_norm_g)
    loss, grad_x, g, dl, nm, nv = _step(
        x, c, w_ada, b_ada, norm_mix_g, w_in, conv_short_w, w_short_out, conv_conf_w, conv_conf_b, conf_ln_g, conf_ln_b,
        w_conf_out, w_o, norm_ffn_g, w_ffn_in, w_ffn_out, final_norm_g, loss_target, moments, min(ROW_TILE, x.shape[1]))
    return (loss, grad_x, *[g[n] for n in _WEIGHTS], *[dl[n] for n in _WEIGHTS], *[nm[n] for n in _WEIGHTS],
            *[nv[n] for n in _WEIGHTS])
```

```python
import jax
import jax.numpy as jnp
from jax import lax
from jax.experimental import pallas as pl
from jax.experimental.pallas import tpu as pltpu

F32 = jnp.float32
BF16 = jnp.bfloat16
EPS = 1e-6
LN_EPS = 1e-5
SHORT_K = 3
CONF_K = 31
N_MOD = 6
N_DEV = 8
N_CHIP = 4
ADAM_LR = 0.001
ADAM_B1 = 0.9
ADAM_B2 = 0.999
ADAM_EPS = 1e-08
ADAM_WD = 0.01
ADAM_STEP = 10
MESH = pl.DeviceIdType.MESH
V7X_VMEM_LIMIT_BYTES = 60 * 1024 * 1024
LANES = 128
HALO_SHORT = 8
HALO_CONF = 32
N_SMALL_ROWS = 48
ROW_DMOD, ROW_GN1, ROW_W3, ROW_W31, ROW_CB, ROW_LNG, ROW_LNB, ROW_GN2, ROW_GF, ROW_SQERR = 0, 6, 7, 10, 41, 42, 43, 44, 45, 46
CONV_ROWS = 64
CONV_LANES = 256

ANY = pl.BlockSpec(memory_space=pl.ANY)
VMEM_FULL = pl.BlockSpec(memory_space=pltpu.VMEM)


def _pallas(body, **kw):
    return pl.pallas_call(body, **kw)


def _params(*sem):
    return pltpu.CompilerParams(dimension_semantics=sem, vmem_limit_bytes=V7X_VMEM_LIMIT_BYTES)


def _sig(z):
    return jax.nn.sigmoid(z)


def _dot(a, b):
    return jnp.dot(a, b, preferred_element_type=F32)


def _dot_nt(a, b):
    return lax.dot_general(a, b, (((1,), (1,)), ((), ())), preferred_element_type=F32)


def _dot_tn(a, b):
    return lax.dot_general(a, b, (((0,), (0,)), ((), ())), preferred_element_type=F32)


def _colsum(z):
    return jnp.sum(z, axis=0, keepdims=True)


def _place():
    return lax.axis_index("x"), lax.axis_index("y"), lax.axis_index("c")


def _small_gather_copies(x_ref, out_ref, send_sems, recv_sems, local_sem):
    m_per = x_ref.shape[0]
    x, y, c = _place()
    me, sibling = (x, y, c), (x, y, 1 - c)
    chips = [(1 - x, y), (x, 1 - y), (1 - x, 1 - y)]

    def rows(px, py, pc):
        return out_ref.at[pl.ds((4 * px + 2 * py + pc) * m_per, m_per), :]

    def copy(k, blk, to, src=None):
        return pltpu.make_async_remote_copy(
            src_ref=rows(*blk) if src is None else src, dst_ref=rows(*blk),
            send_sem=send_sems.at[k], recv_sem=recv_sems.at[k], device_id=to, device_id_type=MESH)

    mine = pltpu.make_async_copy(x_ref, rows(*me), local_sem)
    first = [copy(0, me, sibling, src=x_ref)] + [copy(1 + j, me, (*chip, c), src=x_ref) for j, chip in enumerate(chips)]
    return me, sibling, chips, c, copy, mine, first


def _small_gather_start(x_ref, out_ref, send_sems, recv_sems, local_sem):
    _, _, _, _, _, mine, first = _small_gather_copies(x_ref, out_ref, send_sems, recv_sems, local_sem)
    mine.start()
    for cp in first:
        cp.start()


def _small_gather_finish(x_ref, out_ref, send_sems, recv_sems, local_sem):
    me, sibling, chips, c, copy, mine, first = _small_gather_copies(x_ref, out_ref, send_sems, recv_sems, local_sem)
    passed = [copy(4 + j, (*chip, c), sibling) for j, chip in enumerate(chips)]
    for j, chip in enumerate(chips):
        copy(1 + j, (*chip, c), me).wait_recv()
        passed[j].start()
    copy(0, sibling, me).wait_recv()
    for j, chip in enumerate(chips):
        copy(4 + j, (*chip, 1 - c), me).wait_recv()
    for cp in first + passed:
        cp.wait_send()
    mine.wait()


def _small_gather(x_ref, out_ref, *sems):
    _small_gather_start(x_ref, out_ref, *sems)
    _small_gather_finish(x_ref, out_ref, *sems)


def _small_gather_sems():
    return [pltpu.SemaphoreType.DMA((7,)), pltpu.SemaphoreType.DMA((7,)), pltpu.SemaphoreType.DMA]


def _shard_slice(ref, full_shape, shard_axis, chip, half, quarter=None):
    m, n = full_shape
    r = (m // N_CHIP if shard_axis == 0 else m) // 2
    start = half * r + (chip * 2 * r if shard_axis == 0 else 0)
    if quarter is not None:
        r = r // 2
        start = start + quarter * r
    if shard_axis == 0:
        return ref.at[pl.ds(start, r), :]
    cw = n // N_CHIP
    return ref.at[pl.ds(start, r), pl.ds(pl.multiple_of(chip * cw, 128), cw)]


def _gather_copy(region, k, a, to, sems):
    return pltpu.make_async_remote_copy(src_ref=region, dst_ref=region, send_sem=sems[0].at[a, k], recv_sem=sems[1].at[a, k],
                                        device_id=to, device_id_type=MESH)


def _gather_send(refs, shapes, axes, *sems):
    x, y, c = _place()
    for a in range(len(refs)):
        mine = _shard_slice(refs[a], shapes[a], axes[a], 2 * x + y, c)
        _gather_copy(mine, 0, a, (1 - x, y, c), sems).start()
        _gather_copy(mine, 1, a, (x, 1 - y, c), sems).start()


def _gather_forward(refs, shapes, axes, *sems):
    x, y, c = _place()
    me, sibling, x_nbr, y_nbr = (x, y, c), (x, y, 1 - c), (1 - x, y, c), (x, 1 - y, c)
    chip_x, chip_y, chip_d = 2 * (1 - x) + y, 2 * x + (1 - y), 2 * (1 - x) + (1 - y)
    for a in range(len(refs)):
        region = lambda chip, quarter=None, a=a: _shard_slice(refs[a], shapes[a], axes[a], chip, c, quarter)
        _gather_copy(region(chip_x), 0, a, me, sems).wait_recv()
        _gather_copy(region(chip_x, 1), 3, a, y_nbr, sems).start()
        _gather_copy(region(chip_x), 4, a, sibling, sems).start()
        _gather_copy(region(chip_y), 1, a, me, sems).wait_recv()
        _gather_copy(region(chip_y, 0), 2, a, x_nbr, sems).start()
        _gather_copy(region(chip_y), 5, a, sibling, sems).start()


def _gather_forward_diagonal(refs, shapes, axes, *sems):
    x, y, c = _place()
    me, sibling, chip_d = (x, y, c), (x, y, 1 - c), 2 * (1 - x) + (1 - y)
    for a in range(len(refs)):
        region = lambda chip, quarter=None, a=a: _shard_slice(refs[a], shapes[a], axes[a], chip, c, quarter)
        _gather_copy(region(chip_d, 0), 2, a, me, sems).wait_recv()
        _gather_copy(region(chip_d, 1), 3, a, me, sems).wait_recv()
        _gather_copy(region(chip_d), 6, a, sibling, sems).start()


def _gather_finish(refs, shapes, axes, *sems):
    x, y, c = _place()
    me = (x, y, c)
    chip_me, chip_x, chip_y, chip_d = 2 * x + y, 2 * (1 - x) + y, 2 * x + (1 - y), 2 * (1 - x) + (1 - y)
    for a in range(len(refs)):
        region = lambda chip, half, quarter=None, a=a: _shard_slice(refs[a], shapes[a], axes[a], chip, half, quarter)
        for k, chip in ((4, chip_x), (5, chip_y), (6, chip_d)):
            _gather_copy(region(chip, 1 - c), k, a, me, sems).wait_recv()
            _gather_copy(region(chip, c), k, a, me, sems).wait_send()
        _gather_copy(region(chip_me, c), 0, a, me, sems).wait_send()
        _gather_copy(region(chip_me, c), 1, a, me, sems).wait_send()
        _gather_copy(region(chip_y, c, 0), 2, a, me, sems).wait_send()
        _gather_copy(region(chip_x, c, 1), 3, a, me, sems).wait_send()


def _gather_sems(n_w):
    return [pltpu.SemaphoreType.DMA((n_w, 7)), pltpu.SemaphoreType.DMA((n_w, 7))]


def _gather_weights(shards, axes, n_gather, c_taps, w_ada, b_part, name):
    n_w = len(shards)
    shapes = [(w.shape[0] * N_CHIP, w.shape[1]) if ax == 0 else (w.shape[0], w.shape[1] * N_CHIP) for w, ax in zip(shards, axes)]
    r_ct = c_taps.shape[0]
    mq = w_ada.shape[1]

    def body(*refs):
        ins = refs[:n_w]
        ct_ref, wada_ref, b_ref = refs[n_w:n_w + 3]
        outs = refs[n_w + 3:2 * n_w + 3]
        ct_all_ref, mod_all_ref = refs[2 * n_w + 3:2 * n_w + 5]
        scratch = refs[2 * n_w + 5:]
        stage32, stage16, mod_part = scratch[:n_w], scratch[n_w:2 * n_w], scratch[2 * n_w]
        load_sems, store_sems = scratch[2 * n_w + 1:2 * n_w + 3]
        sems, ct_sems, mod_sems = scratch[2 * n_w + 3:2 * n_w + 5], scratch[2 * n_w + 5:2 * n_w + 8], scratch[2 * n_w + 8:]
        x, y, c = _place()
        chip = 2 * x + y
        loads = [pltpu.make_async_copy(ins[a], stage32[a], load_sems.at[a]) for a in range(n_w)]
        for cp in loads:
            cp.start()
        _small_gather_start(ct_ref, ct_all_ref, *ct_sems)
        stores = []
        for a in range(n_w):
            loads[a].wait()
            stage16[a][...] = stage32[a][...].astype(BF16)
            r, cw = shards[a].shape
            own = outs[a].at[pl.ds(chip * r, r), :] if axes[a] == 0 else outs[a].at[:, pl.ds(pl.multiple_of(chip * cw, 128), cw)]
            stores.append(pltpu.make_async_copy(stage16[a], own, store_sems.at[a]))
            stores[-1].start()
            if a < n_gather:
                half = stage16[a].at[pl.ds(c * (r // 2), r // 2), :]
                there = _shard_slice(outs[a], shapes[a], axes[a], chip, c)
                for k, to in ((0, (1 - x, y, c)), (1, (x, 1 - y, c))):
                    pltpu.make_async_remote_copy(src_ref=half, dst_ref=there, send_sem=sems[0].at[a, k],
                                                 recv_sem=sems[1].at[a, k], device_id=to, device_id_type=MESH).start()
        _small_gather_finish(ct_ref, ct_all_ref, *ct_sems)
        cv = jnp.concatenate([ct_all_ref[r_ct * dv:r_ct * dv + 1, :] for dv in range(N_DEV)], axis=0)
        mod_part[...] = _dot((cv * _sig(cv)).astype(BF16), wada_ref[...].astype(BF16)) + b_ref[...]
        _small_gather_start(mod_part, mod_all_ref, *mod_sems)
        _gather_forward(outs[:n_gather], shapes, axes, *sems)
        _small_gather_finish(mod_part, mod_all_ref, *mod_sems)
        _gather_forward_diagonal(outs[:n_gather], shapes, axes, *sems)
        _gather_finish(outs[:n_gather], shapes, axes, *sems)
        for cp in stores:
            cp.wait()

    res = _pallas(
        body, name=name,
        out_shape=tuple(jax.ShapeDtypeStruct(sh, BF16) for sh in shapes)
        + (jax.ShapeDtypeStruct((N_DEV * r_ct, c_taps.shape[1]), F32), jax.ShapeDtypeStruct((N_DEV * N_DEV, mq), F32)),
        in_specs=[ANY] * n_w + [VMEM_FULL] * 3, out_specs=tuple([ANY] * n_w + [VMEM_FULL] * 2),
        scratch_shapes=[pltpu.VMEM(w.shape, F32) for w in shards] + [pltpu.VMEM(w.shape, BF16) for w in shards]
        + [pltpu.VMEM((N_DEV, mq), F32), pltpu.SemaphoreType.DMA((n_w,)), pltpu.SemaphoreType.DMA((n_w,))]
        + _gather_sems(n_gather) + _small_gather_sems() + _small_gather_sems(),
        compiler_params=_params(),
    )(*shards, c_taps, w_ada, b_part)
    return res[:n_w], res[n_w], res[n_w + 1]


def _piece_shapes(grads, axes):
    return [((g.shape[0] // N_CHIP // 2, g.shape[1]) if ax == 0 else (g.shape[0] // 2, g.shape[1] // N_CHIP))
            for g, ax in zip(grads, axes)]


def _exchange_send(ins, outs, shapes, axes, send_sems, recv_sems):
    x, y, c = _place()
    me = 4 * x + 2 * y + c
    for a in range(len(ins)):
        for t in range(N_DEV):
            tx, ty, tc = t // 4, (t // 2) % 2, t % 2
            src = _shard_slice(ins[a], shapes[a], axes[a], 2 * tx + ty, tc)

            @pl.when(me != t)
            def _(src=src, a=a, t=t, to=(tx, ty, tc)):
                slot = (me - t + N_DEV) % N_DEV - 1
                pltpu.make_async_remote_copy(src_ref=src, dst_ref=outs[a].at[slot], send_sem=send_sems.at[a, t],
                                             recv_sem=recv_sems.at[a, slot], device_id=to, device_id_type=MESH).start()


def _exchange_finish(outs, send_sems, recv_sems):
    x, y, c = _place()
    me = 4 * x + 2 * y + c
    for a in range(len(outs)):
        for t in range(N_DEV):
            @pl.when(me != t)
            def _(a=a, t=t):
                slot = outs[a].at[0]
                pltpu.make_async_remote_copy(src_ref=slot, dst_ref=slot, send_sem=send_sems.at[a, t], recv_sem=recv_sems.at[a, 0],
                                             device_id=(x, y, c), device_id_type=MESH).wait_send()
        for k in range(N_DEV - 1):
            slot = outs[a].at[k]
            pltpu.make_async_remote_copy(src_ref=slot, dst_ref=slot, send_sem=send_sems.at[a, 0], recv_sem=recv_sems.at[a, k],
                                         device_id=(x, y, c), device_id_type=MESH).wait_recv()


def _exchange_sems(n_w):
    return [pltpu.SemaphoreType.DMA((n_w, N_DEV)), pltpu.SemaphoreType.DMA((n_w, N_DEV - 1))]


def _exchange_out_shapes(grads, axes):
    return tuple(jax.ShapeDtypeStruct((N_DEV - 1,) + p, BF16) for p in _piece_shapes(grads, axes))


def _swap_halves(fulls, block):
    n_w = len(fulls)

    def body(*refs):
        block_ref = refs[n_w]
        outs, gathered_ref = refs[n_w + 1:2 * n_w + 1], refs[2 * n_w + 1]
        send_sems, recv_sems = refs[2 * n_w + 2:2 * n_w + 4]
        x, y, c = _place()
        sibling = (x, y, 1 - c)
        waits = []
        for a in range(n_w):
            r = fulls[a].shape[0] // 2
            mine = outs[a].at[pl.ds(c * r, r), :]
            theirs = outs[a].at[pl.ds((1 - c) * r, r), :]
            rc = pltpu.make_async_remote_copy(src_ref=mine, dst_ref=mine, send_sem=send_sems.at[a], recv_sem=recv_sems.at[a],
                                              device_id=sibling, device_id_type=MESH)
            rc.start()
            got = pltpu.make_async_remote_copy(src_ref=theirs, dst_ref=theirs, send_sem=send_sems.at[a],
                                               recv_sem=recv_sems.at[a], device_id=sibling, device_id_type=MESH)
            waits += [rc.wait_send, got.wait_recv]
        _small_gather(block_ref, gathered_ref, *refs[2 * n_w + 4:])
        for w in waits:
            w()

    res = _pallas(
        body, name="swap_halves",
        out_shape=tuple(jax.ShapeDtypeStruct(f.shape, f.dtype) for f in fulls)
        + (jax.ShapeDtypeStruct((N_DEV * block.shape[0], block.shape[1]), block.dtype),),
        in_specs=[ANY] * n_w + [VMEM_FULL], out_specs=tuple([ANY] * n_w + [VMEM_FULL]),
        input_output_aliases={a: a for a in range(n_w)},
        scratch_shapes=[pltpu.SemaphoreType.DMA((n_w,)), pltpu.SemaphoreType.DMA((n_w,))] + _small_gather_sems(),
    )(*fulls, block)
    return res[:n_w], res[n_w]


def _load_weights(pairs, sems):
    @pl.when(pl.program_id(0) == 0)
    def _():
        cps = [pltpu.make_async_copy(s, d, sems.at[k]) for k, (s, d) in enumerate(pairs)]
        for cp in cps:
            cp.start()
        for cp in cps:
            cp.wait()


def _conv_chunks(ext, e_buf, ts, d, chunk_fn, unroll, between=()):
    rows, dh = e_buf.shape[1], e_buf.shape[2]
    for hf in range(d // dh):
        for r in range(1, 8):
            e_buf[r - 1, :, :] = ext[r:r + rows, hf * dh:(hf + 1) * dh]
        for lq in range(dh // CONV_LANES):
            lo, el = hf * dh + lq * CONV_LANES, lq * CONV_LANES

            def body(ci, carry, lo=lo, el=el):
                r0 = pl.multiple_of(ci * CONV_ROWS, CONV_ROWS)

                def tap(off):
                    q, r = divmod(off, 8)
                    if r == 0:
                        return ext[pl.ds(r0 + 8 * q, CONV_ROWS), lo:lo + CONV_LANES]
                    return e_buf[r - 1, pl.ds(r0 + 8 * q, CONV_ROWS), el:el + CONV_LANES]

                chunk_fn(tap, r0, lo)
                return carry

            lax.fori_loop(0, ts // CONV_ROWS, body, 0, unroll=unroll)
            if between:
                between[0]()
                between = between[1:]
    for fn in between:
        fn()


def _fwd_mix(x, mod, gn1, w3, w31, cbias, lng, lnb, w_in, w_a, w_b, w_o, later, later_axes, ts):
    s_len, d = x.shape
    n_t = s_len // ts
    n_l = len(later)
    later_shapes = [w.shape for w in later]

    def body(*refs):
        (x_ref, mod_ref, gn1_ref, w3_ref, w31_ref, cb_ref, lng_ref, lnb_ref, win_hbm, wa_hbm, wb_hbm, wo_hbm) = refs[:12]
        (proj_ref, h_ref, lhs3_ref, c3_ref, n_ref, rstd_ref, ya_ref, yb_ref, mix_ref, x1_ref) = refs[12 + n_l:22 + n_l]
        later_refs = refs[22 + n_l:22 + 2 * n_l]
        win_v, wa_v, wb_v, wo_v, cv_ext, u0_ext, e_buf, u1_buf, sems, g_send, g_recv = refs[22 + 2 * n_l:]
        step = pl.program_id(0)
        pl.when(step == 0)(lambda: _gather_send(later_refs, later_shapes, later_axes, g_send, g_recv))
        pl.when(step == n_t // 2)(lambda: _gather_forward(later_refs, later_shapes, later_axes, g_send, g_recv))
        pl.when(step == (3 * n_t) // 4)(lambda: _gather_forward_diagonal(later_refs, later_shapes, later_axes, g_send, g_recv))
        pl.when(step == n_t - 1)(lambda: _gather_finish(later_refs, later_shapes, later_axes, g_send, g_recv))
        _load_weights([(win_hbm, win_v), (wa_hbm, wa_v), (wb_hbm, wb_v), (wo_hbm, wo_v)], sems)

        @pl.when(pl.program_id(0) == 0)
        def _():
            cv_ext[0:HALO_SHORT, :] = jnp.zeros((HALO_SHORT, d), F32)
            u0_ext[0:HALO_CONF, :] = jnp.zeros((HALO_CONF, d), F32)

        xv = x_ref[...]
        sh1, sc1, g1 = mod_ref[:, 0:d], mod_ref[:, d:2 * d], mod_ref[:, 2 * d:3 * d]
        r = lax.rsqrt(jnp.mean(xv * xv, axis=-1, keepdims=True) + EPS)
        hb = (((xv * r) * gn1_ref[...]) * (1.0 + sc1) + sh1).astype(BF16)
        h_ref[...] = hb

        def pj(k):
            p = _dot(hb, win_v[:, k * d:(k + 1) * d])
            proj_ref[:, k * d:(k + 1) * d] = p.astype(BF16)
            return p

        v_c = pj(3)
        u0_ext[HALO_CONF:HALO_CONF + ts, :] = v_c * _sig(pj(4))

        def conv_chunk(tap, r0, lo):
            acc = jnp.zeros((CONV_ROWS, CONV_LANES), F32)
            for k in range(CONF_K):
                acc = acc + w31_ref[k:k + 1, lo:lo + CONV_LANES] * tap(HALO_CONF - (CONF_K - 1) + k)
            u1_buf[pl.ds(r0, CONV_ROWS), lo:lo + CONV_LANES] = acc

        pjs = {}
        _conv_chunks(u0_ext, e_buf, ts, d, conv_chunk, unroll=True,
                     between=[lambda k=k: pjs.__setitem__(k, pj(k)) for k in (1, 2, 0, 5, 6)])
        u0_ext[0:HALO_CONF, :] = u0_ext[ts:ts + HALO_CONF, :]

        cv_ext[HALO_SHORT:HALO_SHORT + ts, :] = pjs[1] * pjs[2]
        conv3 = jnp.zeros((ts, d), F32)
        for k in range(SHORT_K):
            off = HALO_SHORT - (SHORT_K - 1) + k
            conv3 = conv3 + w3_ref[k:k + 1, :] * cv_ext[off:off + ts, :]
        cv_ext[0:HALO_SHORT, :] = cv_ext[ts:ts + HALO_SHORT, :]
        c3_ref[...] = conv3.astype(BF16)
        yap = (pjs[0] * conv3).astype(BF16)
        lhs3_ref[:, 0:d] = yap
        y_a = _dot(yap, wa_v[...])
        ya_ref[...] = y_a.astype(BF16)

        u1 = u1_buf[...] + cb_ref[...]
        mu = jnp.mean(u1, axis=-1, keepdims=True)
        var = jnp.mean(jnp.square(u1 - mu), axis=-1, keepdims=True)
        rstd = lax.rsqrt(var + LN_EPS)
        nrm = (u1 - mu) * rstd
        n_ref[...] = nrm.astype(BF16)
        rstd_ref[...] = jnp.broadcast_to(rstd, (ts, LANES))
        u2 = nrm * lng_ref[...] + lnb_ref[...]
        u3 = (u2 * _sig(u2)).astype(BF16)
        lhs3_ref[:, d:2 * d] = u3
        y_b = _dot(u3, wb_v[...])
        yb_ref[...] = y_b.astype(BF16)

        merged = (_sig(pjs[5]) * y_a + _sig(pjs[6]) * y_b).astype(BF16)
        lhs3_ref[:, 2 * d:3 * d] = merged
        mix = _dot(merged, wo_v[...])
        mix_ref[...] = mix
        x1_ref[...] = xv + g1 * mix

    row = lambda w, dt=None: pl.BlockSpec((ts, w), lambda i: (i, 0))
    full = lambda a: pl.BlockSpec(a.shape, lambda i: (0, 0))
    sd = lambda w, dt: jax.ShapeDtypeStruct((s_len, w), dt)
    outs = [sd(7 * d, BF16), sd(d, BF16), sd(3 * d, BF16), sd(d, BF16), sd(d, BF16), sd(LANES, F32), sd(d, BF16), sd(d, BF16),
            sd(d, F32), sd(d, F32)]
    res = _pallas(
        body, name="fwd_mix", grid=(n_t,),
        out_shape=tuple(outs) + tuple(jax.ShapeDtypeStruct(w.shape, BF16) for w in later),
        in_specs=[row(d), full(mod), full(gn1), full(w3), full(w31), full(cbias), full(lng), full(lnb), ANY, ANY, ANY, ANY]
        + [ANY] * n_l,
        out_specs=tuple(row(o.shape[1]) for o in outs) + tuple([ANY] * n_l),
        input_output_aliases={12 + k: len(outs) + k for k in range(n_l)},
        scratch_shapes=[pltpu.VMEM(w_in.shape, BF16), pltpu.VMEM(w_a.shape, BF16), pltpu.VMEM(w_b.shape, BF16),
                        pltpu.VMEM(w_o.shape, BF16), pltpu.VMEM((ts + HALO_SHORT, d), F32),
                        pltpu.VMEM((ts + HALO_CONF, d), F32), pltpu.VMEM((7, ts + HALO_CONF - 8, d // 2), F32),
                        pltpu.VMEM((ts, d), F32), pltpu.SemaphoreType.DMA((4,))] + _gather_sems(n_l),
        compiler_params=_params("arbitrary"),
    )(x, mod, gn1, w3, w31, cbias, lng, lnb, w_in, w_a, w_b, w_o, *later)
    return res[:len(outs)], res[len(outs):]


def _ffn(x1, target, mod, gn2, gf, w_fi, w_fo, ts):
    s_len, d = x1.shape
    f = w_fo.shape[0]
    n_t = s_len // ts

    def body(x1_ref, t_ref, mod_ref, gn2_ref, gf_ref, wfi_hbm, wfo_hbm,
             h2_ref, hid_ref, df_ref, dab_ref, dx1_ref, acc_ref, wfi_v, wfo_v, sems):
        _load_weights([(wfi_hbm, wfi_v), (wfo_hbm, wfo_v)], sems)

        @pl.when(pl.program_id(0) == 0)
        def _():
            acc_ref[...] = jnp.zeros(acc_ref.shape, F32)

        x1v = x1_ref[...]
        sh2, sc2, g2 = mod_ref[:, 3 * d:4 * d], mod_ref[:, 4 * d:5 * d], mod_ref[:, 5 * d:6 * d]
        gn2v, gfv = gn2_ref[...], gf_ref[...]
        r2 = lax.rsqrt(jnp.mean(x1v * x1v, axis=-1, keepdims=True) + EPS)
        xn2 = x1v * r2
        h2 = ((xn2 * gn2v) * (1.0 + sc2) + sh2).astype(BF16)
        h2_ref[...] = h2
        a = _dot(h2, wfi_v[:, 0:f])
        bg = _dot(h2, wfi_v[:, f:2 * f])
        sa = _sig(a)
        silu_a = a * sa
        hid = (silu_a * bg).astype(BF16)
        hid_ref[...] = hid
        ffn = _dot(hid, wfo_v[...])
        x2 = x1v + g2 * ffn
        r3 = lax.rsqrt(jnp.mean(x2 * x2, axis=-1, keepdims=True) + EPS)
        xn3 = x2 * r3
        diff = xn3 * gfv - t_ref[...]
        dy = diff * (1.0 / d)
        dxn3 = dy * gfv
        dx2 = r3 * (dxn3 - xn3 * jnp.mean(dxn3 * xn3, axis=-1, keepdims=True))
        acc_ref[ROW_GF:ROW_GF + 1, :] += _colsum(dy * xn3)
        acc_ref[ROW_DMOD + 5:ROW_DMOD + 5 + 1, :] += _colsum(dx2 * ffn)
        acc_ref[ROW_SQERR:ROW_SQERR + 1, :] += _colsum(diff * diff)

        df = (dx2 * g2).astype(BF16)
        df_ref[...] = df
        dhid = _dot_nt(df, wfo_v[...])
        da = dhid * bg * (sa * (1.0 + a * (1.0 - sa)))
        dbg = dhid * silu_a
        dab_ref[:, 0:f] = da.astype(BF16)
        dab_ref[:, f:2 * f] = dbg.astype(BF16)
        dh2 = _dot_nt(da.astype(BF16), wfi_v[:, 0:f]) + _dot_nt(dbg.astype(BF16), wfi_v[:, f:2 * f])
        acc_ref[ROW_DMOD + 3:ROW_DMOD + 3 + 1, :] += _colsum(dh2)
        acc_ref[ROW_DMOD + 4:ROW_DMOD + 4 + 1, :] += _colsum(dh2 * (xn2 * gn2v))
        acc_ref[ROW_GN2:ROW_GN2 + 1, :] += _colsum(dh2 * xn2 * (1.0 + sc2))
        dxn2 = dh2 * gn2v * (1.0 + sc2)
        dx1_ref[...] = dx2 + r2 * (dxn2 - xn2 * jnp.mean(dxn2 * xn2, axis=-1, keepdims=True))

    row = lambda w: pl.BlockSpec((ts, w), lambda i: (i, 0))
    full = lambda a: pl.BlockSpec(a.shape, lambda i: (0, 0))
    sd = lambda w, dt: jax.ShapeDtypeStruct((s_len, w), dt)
    outs = [sd(d, BF16), sd(f, BF16), sd(d, BF16), sd(2 * f, BF16), sd(d, F32), jax.ShapeDtypeStruct((N_SMALL_ROWS, d), F32)]
    return _pallas(
        body, name="ffn", grid=(n_t,), out_shape=tuple(outs),
        in_specs=[row(d), row(d), full(mod), full(gn2), full(gf), ANY, ANY],
        out_specs=tuple([row(o.shape[1]) for o in outs[:-1]] + [pl.BlockSpec((N_SMALL_ROWS, d), lambda i: (0, 0))]),
        scratch_shapes=[pltpu.VMEM(w_fi.shape, BF16), pltpu.VMEM(w_fo.shape, BF16), pltpu.SemaphoreType.DMA((2,))],
        compiler_params=_params("arbitrary"),
    )(x1, target, mod, gn2, gf, w_fi, w_fo)


def _bwd_mix(dx1, proj, nrm, rstd, c3, ya, yb, mix, mod, w3, w31, lng, lnb, w_a, w_b, w_o, ready, ready_axes, ts):
    s_len, d = dx1.shape
    n_t = s_len // ts
    n_r = len(ready)
    ready_shapes = [g.shape for g in ready]

    def body(*refs):
        (dx1_ref, proj_ref, n_ref, rstd_ref, c3_ref, ya_ref, yb_ref, mix_ref, mod_ref, w3_ref, w31_ref, lng_ref,
         lnb_ref, wa_hbm, wb_hbm, wo_hbm) = refs[:16]
        ready_refs = refs[16:16 + n_r]
        dproj_ref, rhs3_ref, acc_ref = refs[16 + n_r:19 + n_r]
        recv_refs = refs[19 + n_r:19 + 2 * n_r]
        wa_v, wb_v, wo_v, dc3_ext, du1_ext, e_buf, u0_buf, du0_buf, dw8, sems, e_send, e_recv = refs[19 + 2 * n_r:]
        step = pl.program_id(0)
        pl.when(step == 0)(lambda: _exchange_send(ready_refs, recv_refs, ready_shapes, ready_axes, e_send, e_recv))
        pl.when(step == n_t - 1)(lambda: _exchange_finish(recv_refs, e_send, e_recv))
        _load_weights([(wa_hbm, wa_v), (wb_hbm, wb_v), (wo_hbm, wo_v)], sems)

        @pl.when(pl.program_id(0) == 0)
        def _():
            acc_ref[...] = jnp.zeros(acc_ref.shape, F32)
            dc3_ext[ts:ts + HALO_SHORT, :] = jnp.zeros((HALO_SHORT, d), F32)
            du1_ext[ts:ts + HALO_CONF, :] = jnp.zeros((HALO_CONF, d), F32)
            dw8[...] = jnp.zeros(dw8.shape, F32)

        pj = lambda k: proj_ref[:, k * d:(k + 1) * d].astype(F32)
        g1 = mod_ref[:, 2 * d:3 * d]
        dx1v = dx1_ref[...]
        acc_ref[ROW_DMOD + 2:ROW_DMOD + 2 + 1, :] += _colsum(dx1v * mix_ref[...])
        dmix = (dx1v * g1).astype(BF16)
        rhs3_ref[:, 2 * d:3 * d] = dmix
        dmerged = _dot_nt(dmix, wo_v[...])
        g_a, g_b = _sig(pj(5)), _sig(pj(6))
        y_a, y_b = ya_ref[...].astype(F32), yb_ref[...].astype(F32)
        dproj_ref[:, 5 * d:6 * d] = (dmerged * y_a * g_a * (1.0 - g_a)).astype(BF16)
        dproj_ref[:, 6 * d:7 * d] = (dmerged * y_b * g_b * (1.0 - g_b)).astype(BF16)
        dya = (dmerged * g_a).astype(BF16)
        dyb = (dmerged * g_b).astype(BF16)
        rhs3_ref[:, 0:d] = dya
        rhs3_ref[:, d:2 * d] = dyb

        dyap = _dot_nt(dya, wa_v[...])
        dproj_ref[:, 0:d] = (dyap * c3_ref[...].astype(F32)).astype(BF16)
        dc3_ext[0:ts, :] = dyap * pj(0)
        c_s, v_s = pj(1), pj(2)
        cv = c_s * v_s
        dcv = jnp.zeros((ts, d), F32)
        for j in range(SHORT_K):
            k = SHORT_K - 1 - j
            sj = dc3_ext[j:j + ts, :]
            dcv = dcv + w3_ref[k:k + 1, :] * sj
            acc_ref[ROW_W3 + k:ROW_W3 + k + 1, :] += _colsum(cv * sj)
        dc3_ext[ts:ts + HALO_SHORT, :] = dc3_ext[0:HALO_SHORT, :]
        dproj_ref[:, d:2 * d] = (dcv * v_s).astype(BF16)
        dproj_ref[:, 2 * d:3 * d] = (dcv * c_s).astype(BF16)

        du3 = _dot_nt(dyb, wb_v[...])
        nv = n_ref[...].astype(F32)
        lg = lng_ref[...]
        u2 = nv * lg + lnb_ref[...]
        s2 = _sig(u2)
        du2 = du3 * (s2 * (1.0 + u2 * (1.0 - s2)))
        acc_ref[ROW_LNG:ROW_LNG + 1, :] += _colsum(du2 * nv)
        acc_ref[ROW_LNB:ROW_LNB + 1, :] += _colsum(du2)
        dn = du2 * lg
        du1 = rstd_ref[:, 0:1] * (dn - jnp.mean(dn, axis=-1, keepdims=True) - nv * jnp.mean(dn * nv, axis=-1, keepdims=True))
        acc_ref[ROW_CB:ROW_CB + 1, :] += _colsum(du1)
        du1_ext[0:ts, :] = du1
        v_c = pj(3)
        sg = _sig(pj(4))
        u0_buf[...] = v_c * sg

        def conv_chunk(tap, r0, lo):
            lanes = slice(lo, lo + CONV_LANES)
            u0c = u0_buf[pl.ds(r0, CONV_ROWS), lanes]
            acc = jnp.zeros((CONV_ROWS, CONV_LANES), F32)
            for j in range(CONF_K):
                k = CONF_K - 1 - j
                sj = tap(j)
                acc = acc + w31_ref[k:k + 1, lanes] * sj
                prod = u0c * sj
                part = prod[0:8]
                for i in range(1, CONV_ROWS // 8):
                    part = part + prod[8 * i:8 * i + 8]
                dw8[k, :, lanes] += part
            du0_buf[pl.ds(r0, CONV_ROWS), lanes] = acc

        _conv_chunks(du1_ext, e_buf, ts, d, conv_chunk, unroll=False)
        du0 = du0_buf[...]
        du1_ext[ts:ts + HALO_CONF, :] = du1_ext[0:HALO_CONF, :]

        @pl.when(step == n_t - 1)
        def _():
            for k in range(CONF_K):
                acc_ref[ROW_W31 + k:ROW_W31 + k + 1, :] = _colsum(dw8[k])
        dproj_ref[:, 3 * d:4 * d] = (du0 * sg).astype(BF16)
        dproj_ref[:, 4 * d:5 * d] = (du0 * v_c * sg * (1.0 - sg)).astype(BF16)

    rev = lambda w: pl.BlockSpec((ts, w), lambda i: (n_t - 1 - i, 0))
    full = lambda a: pl.BlockSpec(a.shape, lambda i: (0, 0))
    sd = lambda w, dt: jax.ShapeDtypeStruct((s_len, w), dt)
    outs = [sd(7 * d, BF16), sd(3 * d, BF16), jax.ShapeDtypeStruct((N_SMALL_ROWS, d), F32)]
    res = _pallas(
        body, name="bwd_mix", grid=(n_t,), out_shape=tuple(outs) + _exchange_out_shapes(ready, ready_axes),
        in_specs=[rev(d), rev(7 * d), rev(d), rev(LANES), rev(d), rev(d), rev(d), rev(d), full(mod), full(w3), full(w31),
                  full(lng), full(lnb), ANY, ANY, ANY] + [ANY] * n_r,
        out_specs=tuple([rev(o.shape[1]) for o in outs[:-1]] + [pl.BlockSpec((N_SMALL_ROWS, d), lambda i: (0, 0))] + [ANY] * n_r),
        scratch_shapes=[pltpu.VMEM(w_a.shape, BF16), pltpu.VMEM(w_b.shape, BF16), pltpu.VMEM(w_o.shape, BF16),
                        pltpu.VMEM((ts + HALO_SHORT, d), F32), pltpu.VMEM((ts + HALO_CONF, d), F32),
                        pltpu.VMEM((7, ts + HALO_CONF - 8, d // 2), F32), pltpu.VMEM((ts, d), F32), pltpu.VMEM((ts, d), F32),
                        pltpu.VMEM((CONF_K, 8, d), F32), pltpu.SemaphoreType.DMA((3,))] + _exchange_sems(n_r),
        compiler_params=_params("arbitrary"),
    )(dx1, proj, nrm, rstd, c3, ya, yb, mix, mod, w3, w31, lng, lnb, w_a, w_b, w_o, *ready)
    return res[:len(outs)], res[len(outs):]


def _bwd_in(dproj, x, dx1, mod, gn1, w_in, ts):
    s_len, d = x.shape
    n_t = s_len // ts

    def body(dproj_ref, x_ref, dx1_ref, mod_ref, gn1_ref, win_hbm, gx_ref, acc_ref, win_v, sems):
        _load_weights([(win_hbm, win_v)], sems)

        @pl.when(pl.program_id(0) == 0)
        def _():
            acc_ref[...] = jnp.zeros(acc_ref.shape, F32)

        xv = x_ref[...]
        sc1 = mod_ref[:, d:2 * d]
        gn1v = gn1_ref[...]
        r = lax.rsqrt(jnp.mean(xv * xv, axis=-1, keepdims=True) + EPS)
        xn = xv * r
        dh = _dot_nt(dproj_ref[...], win_v[...])
        acc_ref[ROW_DMOD:ROW_DMOD + 1, :] += _colsum(dh)
        acc_ref[ROW_DMOD + 1:ROW_DMOD + 1 + 1, :] += _colsum(dh * (xn * gn1v))
        acc_ref[ROW_GN1:ROW_GN1 + 1, :] += _colsum(dh * xn * (1.0 + sc1))
        dxn = dh * gn1v * (1.0 + sc1)
        gx_ref[...] = dx1_ref[...] + r * (dxn - xn * jnp.mean(dxn * xn, axis=-1, keepdims=True))

    row = lambda w: pl.BlockSpec((ts, w), lambda i: (i, 0))
    full = lambda a: pl.BlockSpec(a.shape, lambda i: (0, 0))
    return _pallas(
        body, name="bwd_in", grid=(n_t,),
        out_shape=(jax.ShapeDtypeStruct((s_len, d), F32), jax.ShapeDtypeStruct((N_SMALL_ROWS, d), F32)),
        in_specs=[row(7 * d), row(d), row(d), full(mod), full(gn1), ANY],
        out_specs=(row(d), pl.BlockSpec((N_SMALL_ROWS, d), lambda i: (0, 0))),
        scratch_shapes=[pltpu.VMEM(w_in.shape, BF16), pltpu.SemaphoreType.DMA((1,))],
        compiler_params=_params("arbitrary"),
    )(dproj, x, dx1, mod, gn1, w_in)


class _OneWeight:
    def __init__(self, sems):
        self.sems = sems

    @property
    def at(self):
        return self

    def __getitem__(self, idx):
        return self.sems.at[idx[1]]


_HBM = pl.BlockSpec(memory_space=pltpu.HBM)
_SEM = pl.BlockSpec(memory_space=pltpu.SEMAPHORE)
_DATAFLOW = pltpu.SideEffectType.DATAFLOW_SIDE_EFFECTING


def _exchange_begin(grad, axis, name):
    slots = (N_DEV - 1,) + _piece_shapes([grad], [axis])[0]

    def body(g_ref, slots_ref, send_sems, recv_sems, g_thru, slots_thru, token):
        _exchange_send([g_ref], [slots_ref], [grad.shape], [axis], _OneWeight(send_sems), _OneWeight(recv_sems))
        token[...] = jnp.zeros(token.shape, F32)

    return _pallas(
        body, name=name,
        out_shape=(pltpu.SemaphoreType.DMA((N_DEV,)), pltpu.SemaphoreType.DMA((N_DEV - 1,)), pltpu.HBM(grad.shape, BF16),
                   pltpu.HBM(slots, BF16), jax.ShapeDtypeStruct((8, LANES), F32)),
        in_specs=(_HBM, _HBM), out_specs=(_SEM, _SEM, _HBM, _HBM, VMEM_FULL), input_output_aliases={0: 2, 1: 3},
        compiler_params=pltpu.CompilerParams(has_side_effects=_DATAFLOW),
    )(pltpu.with_memory_space_constraint(grad, pltpu.HBM), pltpu.with_memory_space_constraint(lax.empty(slots, BF16), pltpu.HBM))


def _exchange_end(send_sems, recv_sems, grad, slots, after, name):
    def body(g_ref, slots_ref, send_sems, recv_sems, after_ref, g_out, slots_out):
        _exchange_finish([slots_ref], _OneWeight(send_sems), _OneWeight(recv_sems))

    return _pallas(
        body, name=name, out_shape=(pltpu.HBM(grad.shape, BF16), pltpu.HBM(slots.shape, BF16)),
        in_specs=(_HBM, _HBM, _SEM, _SEM, ANY), out_specs=(_HBM, _HBM), input_output_aliases={0: 0, 1: 1},
        compiler_params=pltpu.CompilerParams(has_side_effects=_DATAFLOW),
    )(grad, slots, send_sems, recv_sems, after)


def _weight_grad(a, b, bn, name, transposed=False, ready=(), ready_axes=(), a_cols=None):
    s_len, m = a.shape[0], (a_cols or a.shape[1])
    n = b.shape[1]
    n_j = n // bn
    n_r = len(ready)
    ready_shapes = [g.shape for g in ready]

    def body(*refs):
        a_hbm, b_ref = refs[:2]
        ready_refs = refs[2:2 + n_r]
        o_ref = refs[2 + n_r]
        recv_refs = refs[3 + n_r:3 + 2 * n_r]
        a_v, sem = refs[3 + 2 * n_r:5 + 2 * n_r]
        step = pl.program_id(0)
        if n_r:
            e_send, e_recv = refs[5 + 2 * n_r:]
            pl.when(step == 0)(lambda: _exchange_send(ready_refs, recv_refs, ready_shapes, ready_axes, e_send, e_recv))
            pl.when(step == n_j - 1)(lambda: _exchange_finish(recv_refs, e_send, e_recv))
        if a_cols is None:
            _load_weights([(a_hbm, a_v)], sem)
            prod = _dot_tn(a_v[...], b_ref[...])
        else:
            prod = _dot_tn(a_hbm[...], b_ref[...])
        o_ref[...] = (prod.T if transposed else prod).astype(BF16)

    out = jax.ShapeDtypeStruct((n, m) if transposed else (m, n), BF16)
    out_spec = pl.BlockSpec((bn, m), lambda j: (j, 0)) if transposed else pl.BlockSpec((m, bn), lambda j: (0, j))
    res = _pallas(
        body, name=name, grid=(n_j,), out_shape=(out,) + _exchange_out_shapes(ready, ready_axes),
        in_specs=[ANY if a_cols is None else pl.BlockSpec((s_len, a_cols), lambda j: (0, (j * bn) // a_cols)),
                  pl.BlockSpec((s_len, bn), lambda j: (0, j))] + [ANY] * n_r,
        out_specs=(out_spec,) + tuple([ANY] * n_r),
        scratch_shapes=[pltpu.VMEM((8, 128) if a_cols else a.shape, BF16), pltpu.SemaphoreType.DMA((1,))]
        + (_exchange_sems(n_r) if n_r else []),
        compiler_params=_params("arbitrary"),
    )(a, b, *ready)
    return res[0], res[1:]


def _sum_slots(recv, grad, axis, chip, core, name):
    n_s, r, c = recv.shape
    tr = 16
    for cand in (256, 128, 176, 64, 32, 16):
        if r % cand == 0:
            tr = cand
            break
    n_i = r // tr

    def body(s_ref, own_ref, r_ref, o_ref):
        tot = own_ref[...].astype(F32)
        for k in range(n_s):
            tot = tot + r_ref[k].astype(F32)
        o_ref[...] = tot

    if axis == 0:
        own_map = lambda i, s: ((2 * s[0] + s[1]) * n_i + i, 0)
    else:
        own_map = lambda i, s: (s[1] * n_i + i, s[0])
    grid_spec = pltpu.PrefetchScalarGridSpec(
        num_scalar_prefetch=1, grid=(n_i,),
        in_specs=[pl.BlockSpec((tr, c), own_map), pl.BlockSpec((n_s, tr, c), lambda i, s: (0, i, 0))],
        out_specs=pl.BlockSpec((tr, c), lambda i, s: (s[1] * n_i + i, 0)))
    return _pallas(body, name=name, grid_spec=grid_spec, out_shape=jax.ShapeDtypeStruct((2 * r, c), F32),
                   compiler_params=_params("arbitrary"))(jnp.stack([chip, core]).astype(jnp.int32), grad, recv)


def _adamw_math(w, g, m, v):
    m = ADAM_B1 * m + (1.0 - ADAM_B1) * g
    v = ADAM_B2 * v + (1.0 - ADAM_B2) * (g * g)
    m_hat = m / (1.0 - ADAM_B1 ** ADAM_STEP)
    v_hat = v / (1.0 - ADAM_B2 ** ADAM_STEP)
    delta = -ADAM_LR * (m_hat / (jnp.sqrt(v_hat) + ADAM_EPS) + ADAM_WD * w)
    return delta, m, v


def _adamw(w, g, m, v, name):
    r, c = w.shape
    tr = r
    for cand in (256, 128, 176, 64, 32, 16, 8):
        if r % cand == 0:
            tr = cand
            break

    def body(w_ref, g_ref, m_ref, v_ref, go_ref, d_ref, nm_ref, nv_ref):
        g = g_ref[...]
        go_ref[...] = g
        d_ref[...], nm_ref[...], nv_ref[...] = _adamw_math(w_ref[...], g, m_ref[...], v_ref[...])

    blk = pl.BlockSpec((tr, c), lambda i: (i, 0))
    return _pallas(body, name=name, grid=(r // tr,), out_shape=tuple(jax.ShapeDtypeStruct((r, c), F32) for _ in range(4)),
                   in_specs=[blk] * 4, out_specs=(blk,) * 4, compiler_params=_params("arbitrary"))(w, g, m, v)


_SMALL = [
    ("b_ada", ROW_DMOD, "mod"), ("norm_mix_g", ROW_GN1, "row"), ("conv_short_w", ROW_W3, "shard"),
    ("conv_conf_w", ROW_W31, "shard"), ("conv_conf_b", ROW_CB, "row"), ("conf_ln_g", ROW_LNG, "row"),
    ("conf_ln_b", ROW_LNB, "row"), ("norm_ffn_g", ROW_GN2, "row"), ("final_norm_g", ROW_GF, "row")]


def _small_update(p_all, chip_onehot, ws, ms, vs):
    n = len(_SMALL)
    d = p_all.shape[2]

    def body(*refs):
        p_ref, oh_ref = refs[:2]
        w_refs, m_refs, v_refs = refs[2:2 + n], refs[2 + n:2 + 2 * n], refs[2 + 2 * n:2 + 3 * n]
        loss_ref = refs[2 + 3 * n]
        g_out, d_out, m_out, v_out = [refs[3 + (3 + q) * n:3 + (4 + q) * n] for q in range(4)]
        g_all = p_ref[0]
        for s in range(1, N_DEV):
            g_all = g_all + p_ref[s]
        loss_ref[...] = jnp.zeros(loss_ref.shape, F32) + (0.5 / d) * jnp.sum(g_all[ROW_SQERR:ROW_SQERR + 1, :])

        def emit(i, idx, g):
            dl, nm, nv = _adamw_math(w_refs[i][idx], g, m_refs[i][idx], v_refs[i][idx])
            g_out[i][idx], d_out[i][idx], m_out[i][idx], v_out[i][idx] = g, dl, nm, nv

        everything = (slice(None), slice(None))
        for i, (_, row, kind) in enumerate(_SMALL):
            if kind == "mod":
                for k in range(N_MOD):
                    emit(i, (slice(0, 1), slice(k * d, (k + 1) * d)), g_all[row + k:row + k + 1, :])
            elif kind == "row":
                emit(i, everything, g_all[row:row + 1, :])
            else:
                _, taps, dq = w_refs[i].shape
                g = jnp.zeros((taps, dq), F32)
                for j in range(N_CHIP):
                    g = g + oh_ref[0:1, j:j + 1] * g_all[row:row + taps, j * dq:(j + 1) * dq]
                emit(i, (0, slice(None), slice(None)), g)

    shapes = [jax.ShapeDtypeStruct(w.shape, F32) for w in ws]
    res = _pallas(body, name="small_update", out_shape=tuple([jax.ShapeDtypeStruct((8, 128), F32)] + shapes * 4),
                  in_specs=[VMEM_FULL] * (2 + 3 * n), out_specs=tuple([VMEM_FULL] * (1 + 4 * n)),
                  compiler_params=_params())(p_all, chip_onehot, *ws, *ms, *vs)
    return res[0], [res[1 + q * n:1 + (q + 1) * n] for q in range(4)]


def _adamw_w_ada(w, c_t, dmod_part, m, v):
    r, c = w.shape
    tr = min(256, r)

    def body(w_ref, c_ref, dm_ref, m_ref, v_ref, go_ref, d_ref, nm_ref, nv_ref):
        cv = c_ref[...]
        g = _dot((cv * _sig(cv)).astype(BF16), dm_ref[...].astype(BF16))
        go_ref[...] = g
        d_ref[...], nm_ref[...], nv_ref[...] = _adamw_math(w_ref[...], g, m_ref[...], v_ref[...])

    blk = pl.BlockSpec((tr, c), lambda i: (i, 0))
    return _pallas(body, name="adamw_w_ada", grid=(r // tr,), out_shape=tuple(jax.ShapeDtypeStruct((r, c), F32) for _ in range(4)),
                   in_specs=[blk, pl.BlockSpec((tr, c_t.shape[1]), lambda i: (i, 0)),
                             pl.BlockSpec(dmod_part.shape, lambda i: (0, 0)), blk, blk],
                   out_specs=(blk,) * 4, compiler_params=_params("arbitrary"))(w, c_t, dmod_part, m, v)


def _step(x, c, w_ada, b_ada, norm_mix_g, w_in, conv_short_w, w_short_out, conv_conf_w, conv_conf_b, conf_ln_g,
          conf_ln_b, w_conf_out, w_o, norm_ffn_g, w_ffn_in, w_ffn_out, final_norm_g, loss_target, moments, ts):
    xi, yi, ci = _place()
    chip = 2 * xi + yi
    me = 4 * xi + 2 * yi + ci
    x2d, tgt = x[0], loss_target[0]
    s_len, d = x2d.shape
    dq = d // N_CHIP
    mq = N_MOD * d // N_CHIP
    gf = final_norm_g.reshape(1, d)

    taps = jnp.concatenate([jnp.pad(conv_short_w[0], ((0, 8 - SHORT_K), (0, 0))),
                            jnp.pad(conv_conf_w[0], ((0, 32 - CONF_K), (0, 0)))], axis=0)
    n_tr = 40 * dq // d
    c_taps = jnp.concatenate([jnp.broadcast_to(c, (8, d)), taps.reshape(n_tr, d), jnp.zeros((16 - n_tr, d), F32)], axis=0)
    b_part = lax.dynamic_slice(b_ada, (0, chip * mq), (1, mq))
    big = [w_in[0], w_short_out[0], w_conf_out[0], w_o[0], w_ffn_in[0], w_ffn_out[0]]
    axes = [1, 0, 0, 0, 1, 0]
    big_names = ["w_in", "w_short_out", "w_conf_out", "w_o", "w_ffn_in", "w_ffn_out"]
    (win_f, wa_f, wb_f, wo_f, *placed_ffn), c_taps, mod_all = _gather_weights(big, axes, 4, c_taps, w_ada[0], b_part,
                                                                              "gather_weights")
    c_taps = c_taps.reshape(N_DEV, 24, d)
    c_all = c_taps[:, 0]
    mod_all = mod_all.reshape(N_CHIP, 2, N_DEV, mq)[:, 0]
    mod_full = jnp.transpose(mod_all, (1, 0, 2)).reshape(N_DEV, N_MOD * d)
    mod = lax.dynamic_slice(mod_full, (me, 0), (1, N_MOD * d))

    taps_all = c_taps[:, 8:8 + n_tr].reshape(N_CHIP, 2, 40, dq)[:, 0]
    taps_full = jnp.transpose(taps_all, (1, 0, 2)).reshape(40, d)
    w3, w31 = taps_full[0:SHORT_K], taps_full[8:8 + CONF_K]

    (proj, h, lhs3, c3, nrm, rstd, ya, yb, mix, x1), (wfi_f, wfo_f) = _fwd_mix(
        x2d, mod, norm_mix_g, w3, w31, conv_conf_b, conf_ln_g, conf_ln_b, win_f, wa_f, wb_f, wo_f, placed_ffn, axes[4:], ts)
    h2, hid, df, dab, dx1, acc_f = _ffn(x1, tgt, mod, norm_ffn_g, gf, wfi_f, wfo_f, ts)
    f = wfo_f.shape[0]
    bn = min(512, d)
    g_ffn = [_weight_grad(h2, dab, bn, "grad_w_ffn_in")[0], _weight_grad(df, hid, bn // 2, "grad_w_ffn_out", transposed=True)[0]]
    (dproj, rhs3, acc_m), r_ffn = _bwd_mix(dx1, proj, nrm, rstd, c3, ya, yb, mix, mod, w3, w31, conf_ln_g, conf_ln_b,
                                          wa_f, wb_f, wo_f, g_ffn, axes[4:], ts)
    g3 = _weight_grad(lhs3, rhs3, bn // 2, "grad_w_sq", a_cols=d)[0]
    g_sq = [g3[:, k * d:(k + 1) * d] for k in range(3)]
    g_in, r_sq = _weight_grad(h, dproj, bn, "grad_w_in", ready=g_sq, ready_axes=axes[1:4])
    send_sems, recv_sems, g_in, slots, token = _exchange_begin(g_in, axes[0], "exchange_w_in_begin")
    grad_x, acc_i = _bwd_in(dproj, x2d, dx1, mod + token[0, 0], norm_mix_g, win_f, min(2 * ts, s_len))
    g_in, r_in = _exchange_end(send_sems, recv_sems, g_in, slots, acc_i, "exchange_w_in_end")
    g_in = [g_in]
    r_in_sq = [r_in] + list(r_sq)
    grads = g_in + g_sq + g_ffn
    recv = list(r_in_sq) + list(r_ffn)
    g_big, p_all = _swap_halves([_sum_slots(r, g, ax, chip, ci, "sum_slots_" + n)
                                 for r, g, ax, n in zip(recv, grads, axes, big_names)], acc_f + acc_m + acc_i)
    p_all = p_all.reshape(N_DEV, N_SMALL_ROWS, d)

    small_w = dict(b_ada=b_ada, norm_mix_g=norm_mix_g, conv_short_w=conv_short_w, conv_conf_w=conv_conf_w,
                   conv_conf_b=conv_conf_b, conf_ln_g=conf_ln_g, conf_ln_b=conf_ln_b, norm_ffn_g=norm_ffn_g,
                   final_norm_g=final_norm_g)
    as_2d = lambda t: t if t.ndim == 3 else t.reshape((-1, t.shape[-1]))
    chip_onehot = (lax.broadcasted_iota(jnp.int32, (1, 128), 1) == chip).astype(F32)
    loss_tile, small_out = _small_update(
        p_all, chip_onehot, [as_2d(small_w[n]) for n, _, _ in _SMALL], [as_2d(moments["m_" + n]) for n, _, _ in _SMALL],
        [as_2d(moments["v_" + n]) for n, _, _ in _SMALL])
    small_g, small_d, small_m, small_v = [{n: o.reshape(small_w[n].shape) for (n, _, _), o in zip(_SMALL, outs)}
                                          for outs in small_out]

    dmod_all = p_all[:, 0:N_MOD].reshape(N_DEV, N_MOD * d)
    dmod_part = jnp.pad(lax.dynamic_slice(dmod_all, (0, chip * mq), (N_DEV, mq)), ((0, 128 - N_DEV), (0, 0)))
    c_t = jnp.pad(jnp.transpose(c_all), ((0, 0), (0, 128 - N_DEV)))
    out_g, out_d, out_m, out_v = dict(small_g), dict(small_d), dict(small_m), dict(small_v)
    updates = [("w_ada", _adamw_w_ada(w_ada[0], c_t, dmod_part, moments["m_w_ada"][0], moments["v_w_ada"][0]))]
    updates += [(name, _adamw(w, g, moments["m_" + name][0], moments["v_" + name][0], "adamw_" + name))
                for name, w, g in zip(big_names, big, g_big)]
    for name, (go, dl, nm, nv) in updates:
        out_g[name], out_d[name], out_m[name], out_v[name] = go[None], dl[None], nm[None], nv[None]

    loss = loss_tile[0, 0]
    return loss, grad_x[None], out_g, out_d, out_m, out_v


_WEIGHTS = ["w_ada", "b_ada", "norm_mix_g", "w_in", "conv_short_w", "w_short_out", "conv_conf_w", "conv_conf_b",
            "conf_ln_g", "conf_ln_b", "w_conf_out", "w_o", "norm_ffn_g", "w_ffn_in", "w_ffn_out", "final_norm_g"]
ROW_TILE = 256


def kernel(x, c, w_ada, b_ada, norm_mix_g, w_in, conv_short_w, w_short_out, conv_conf_w, conv_conf_b, conf_ln_g, conf_ln_b, w_conf_out, w_o, norm_ffn_g, w_ffn_in, w_ffn_out, final_norm_g, loss_target, m_w_ada, m_b_ada, m_norm_mix_g, m_w_in, m_conv_short_w, m_w_short_out, m_conv_conf_w, m_conv_conf_b, m_conf_ln_g, m_conf_ln_b, m_w_conf_out, m_w_o, m_norm_ffn_g, m_w_ffn_in, m_w_ffn_out, m_final_norm_g, v_w_ada, v_b_ada, v_norm_mix_g, v_w_in, v_conv_short_w, v_w_short_out, v_conv_conf_w, v_conv_conf_b, v_conf_ln_g, v_conf_ln_b, v_w_conf_out, v_w_o, v_norm_ffn_g, v_w_ffn_in, v_w_ffn_out, v_final_norm_g):
    moments = dict(
        m_w_ada=m_w_ada, m_b_ada=m_b_ada, m_norm_mix_g=m_norm_mix_g, m_w_in=m_w_in, m_conv_short_w=m_conv_short_w,
        m_w_short_out=m_w_short_out, m_conv_conf_w=m_conv_conf_w, m_conv_conf_b=m_conv_conf_b, m_conf_ln_g=m_conf_ln_g,
        m_conf_ln_b=m_conf_ln_b, m_w_conf_out=m_w_conf_out, m_w_o=m_w_o, m_norm_ffn_g=m_norm_ffn_g, m_w_ffn_in=m_w_ffn_in,
        m_w_ffn_out=m_w_ffn_out, m_final_norm_g=m_final_norm_g,
        v_w_ada=v_w_ada, v_b_ada=v_b_ada, v_norm_mix_g=v_norm_mix_g, v_w_in=v_w_in, v_conv_short_w=v_conv_short_w,
        v_w_short_out=v_w_short_out, v_conv_conf_w=v_conv_conf_w, v_conv_conf_b=v_conv_conf_b, v_conf_ln_g=v_conf_ln_g,
        v_conf_ln_b=v_conf_ln_b, v_w_conf_out=v_w_conf_out, v_w_o=v_w_o, v_norm_ffn_g=v_norm_ffn_g, v_w_ffn_in=v_w_ffn_in,
        v_w_ffn_out=v_w_ffn_out, v_final_norm_g=v_final_norm_g)
    loss, grad_x, g, dl, nm, nv = _step(
        x, c, w_ada, b_ada, norm_mix_g, w_in, conv_short_w, w_short_out, conv_conf_w, conv_conf_b, conf_ln_g, conf_ln_b,
        w_conf_out, w_o, norm_ffn_g, w_ffn_in, w_ffn_out, final_norm_g, loss_target, moments, min(ROW_TILE, x.shape[1]))
    return (loss, grad_x, *[g[n] for n in _WEIGHTS], *[dl[n] for n in _WEIGHTS], *[nm[n] for n in _WEIGHTS],
            *[nv[n] for n in _WEIGHTS])
```

```python
import jax
import jax.numpy as jnp
from jax import lax
from jax.experimental import pallas as pl
from jax.experimental.pallas import tpu as pltpu

F32 = jnp.float32
BF16 = jnp.bfloat16
EPS = 1e-6
LN_EPS = 1e-5
SHORT_K = 3
CONF_K = 31
N_MOD = 6
N_DEV = 8
N_CHIP = 4
ADAM_LR = 0.001
ADAM_B1 = 0.9
ADAM_B2 = 0.999
ADAM_EPS = 1e-08
ADAM_WD = 0.01
ADAM_STEP = 10
MESH = pl.DeviceIdType.MESH
V7X_VMEM_LIMIT_BYTES = 60 * 1024 * 1024
LANES = 128
HALO_SHORT = 8
HALO_CONF = 32
N_SMALL_ROWS = 48
ROW_DMOD, ROW_GN1, ROW_W3, ROW_W31, ROW_CB, ROW_LNG, ROW_LNB, ROW_GN2, ROW_GF, ROW_SQERR = 0, 6, 7, 10, 41, 42, 43, 44, 45, 46
CONV_ROWS = 64
CONV_LANES = 256

ANY = pl.BlockSpec(memory_space=pl.ANY)
VMEM_FULL = pl.BlockSpec(memory_space=pltpu.VMEM)


def _pallas(body, **kw):
    return pl.pallas_call(body, **kw)


def _params(*sem):
    return pltpu.CompilerParams(dimension_semantics=sem, vmem_limit_bytes=V7X_VMEM_LIMIT_BYTES)


def _sig(z):
    return jax.nn.sigmoid(z)


def _dot(a, b):
    return jnp.dot(a, b, preferred_element_type=F32)


def _dot_nt(a, b):
    return lax.dot_general(a, b, (((1,), (1,)), ((), ())), preferred_element_type=F32)


def _dot_tn(a, b):
    return lax.dot_general(a, b, (((0,), (0,)), ((), ())), preferred_element_type=F32)


def _colsum(z):
    return jnp.sum(z, axis=0, keepdims=True)


def _place():
    return lax.axis_index("x"), lax.axis_index("y"), lax.axis_index("c")


def _small_gather_copies(x_ref, out_ref, send_sems, recv_sems, local_sem):
    m_per = x_ref.shape[0]
    x, y, c = _place()
    me, sibling = (x, y, c), (x, y, 1 - c)
    chips = [(1 - x, y), (x, 1 - y), (1 - x, 1 - y)]

    def rows(px, py, pc):
        return out_ref.at[pl.ds((4 * px + 2 * py + pc) * m_per, m_per), :]

    def copy(k, blk, to, src=None):
        return pltpu.make_async_remote_copy(
            src_ref=rows(*blk) if src is None else src, dst_ref=rows(*blk),
            send_sem=send_sems.at[k], recv_sem=recv_sems.at[k], device_id=to, device_id_type=MESH)

    mine = pltpu.make_async_copy(x_ref, rows(*me), local_sem)
    first = [copy(0, me, sibling, src=x_ref)] + [copy(1 + j, me, (*chip, c), src=x_ref) for j, chip in enumerate(chips)]
    return me, sibling, chips, c, copy, mine, first


def _small_gather_start(x_ref, out_ref, send_sems, recv_sems, local_sem):
    _, _, _, _, _, mine, first = _small_gather_copies(x_ref, out_ref, send_sems, recv_sems, local_sem)
    mine.start()
    for cp in first:
        cp.start()


def _small_gather_finish(x_ref, out_ref, send_sems, recv_sems, local_sem):
    me, sibling, chips, c, copy, mine, first = _small_gather_copies(x_ref, out_ref, send_sems, recv_sems, local_sem)
    passed = [copy(4 + j, (*chip, c), sibling) for j, chip in enumerate(chips)]
    for j, chip in enumerate(chips):
        copy(1 + j, (*chip, c), me).wait_recv()
        passed[j].start()
    copy(0, sibling, me).wait_recv()
    for j, chip in enumerate(chips):
        copy(4 + j, (*chip, 1 - c), me).wait_recv()
    for cp in first + passed:
        cp.wait_send()
    mine.wait()


def _small_gather(x_ref, out_ref, *sems):
    _small_gather_start(x_ref, out_ref, *sems)
    _small_gather_finish(x_ref, out_ref, *sems)


def _small_gather_sems():
    return [pltpu.SemaphoreType.DMA((7,)), pltpu.SemaphoreType.DMA((7,)), pltpu.SemaphoreType.DMA]


def _shard_slice(ref, full_shape, shard_axis, chip, half, quarter=None):
    m, n = full_shape
    r = (m // N_CHIP if shard_axis == 0 else m) // 2
    start = half * r + (chip * 2 * r if shard_axis == 0 else 0)
    if quarter is not None:
        r = r // 2
        start = start + quarter * r
    if shard_axis == 0:
        return ref.at[pl.ds(start, r), :]
    cw = n // N_CHIP
    return ref.at[pl.ds(start, r), pl.ds(pl.multiple_of(chip * cw, 128), cw)]


def _gather_copy(region, k, a, to, sems):
    return pltpu.make_async_remote_copy(src_ref=region, dst_ref=region, send_sem=sems[0].at[a, k], recv_sem=sems[1].at[a, k],
                                        device_id=to, device_id_type=MESH)


def _gather_send(refs, shapes, axes, *sems):
    x, y, c = _place()
    for a in range(len(refs)):
        mine = _shard_slice(refs[a], shapes[a], axes[a], 2 * x + y, c)
        _gather_copy(mine, 0, a, (1 - x, y, c), sems).start()
        _gather_copy(mine, 1, a, (x, 1 - y, c), sems).start()


def _gather_forward(refs, shapes, axes, *sems):
    x, y, c = _place()
    me, sibling, x_nbr, y_nbr = (x, y, c), (x, y, 1 - c), (1 - x, y, c), (x, 1 - y, c)
    chip_x, chip_y, chip_d = 2 * (1 - x) + y, 2 * x + (1 - y), 2 * (1 - x) + (1 - y)
    for a in range(len(refs)):
        region = lambda chip, quarter=None, a=a: _shard_slice(refs[a], shapes[a], axes[a], chip, c, quarter)
        _gather_copy(region(chip_x), 0, a, me, sems).wait_recv()
        _gather_copy(region(chip_x, 1), 3, a, y_nbr, sems).start()
        _gather_copy(region(chip_x), 4, a, sibling, sems).start()
        _gather_copy(region(chip_y), 1, a, me, sems).wait_recv()
        _gather_copy(region(chip_y, 0), 2, a, x_nbr, sems).start()
        _gather_copy(region(chip_y), 5, a, sibling, sems).start()


def _gather_forward_diagonal(refs, shapes, axes, *sems):
    x, y, c = _place()
    me, sibling, chip_d = (x, y, c), (x, y, 1 - c), 2 * (1 - x) + (1 - y)
    for a in range(len(refs)):
        region = lambda chip, quarter=None, a=a: _shard_slice(refs[a], shapes[a], axes[a], chip, c, quarter)
        _gather_copy(region(chip_d, 0), 2, a, me, sems).wait_recv()
        _gather_copy(region(chip_d, 1), 3, a, me, sems).wait_recv()
        _gather_copy(region(chip_d), 6, a, sibling, sems).start()


def _gather_finish(refs, shapes, axes, *sems):
    x, y, c = _place()
    me = (x, y, c)
    chip_me, chip_x, chip_y, chip_d = 2 * x + y, 2 * (1 - x) + y, 2 * x + (1 - y), 2 * (1 - x) + (1 - y)
    for a in range(len(refs)):
        region = lambda chip, half, quarter=None, a=a: _shard_slice(refs[a], shapes[a], axes[a], chip, half, quarter)
        for k, chip in ((4, chip_x), (5, chip_y), (6, chip_d)):
            _gather_copy(region(chip, 1 - c), k, a, me, sems).wait_recv()
            _gather_copy(region(chip, c), k, a, me, sems).wait_send()
        _gather_copy(region(chip_me, c), 0, a, me, sems).wait_send()
        _gather_copy(region(chip_me, c), 1, a, me, sems).wait_send()
        _gather_copy(region(chip_y, c, 0), 2, a, me, sems).wait_send()
        _gather_copy(region(chip_x, c, 1), 3, a, me, sems).wait_send()


def _gather_sems(n_w):
    return [pltpu.SemaphoreType.DMA((n_w, 7)), pltpu.SemaphoreType.DMA((n_w, 7))]


def _gather_weights(shards, axes, n_gather, c_taps, w_ada, b_part, name):
    n_w = len(shards)
    shapes = [(w.shape[0] * N_CHIP, w.shape[1]) if ax == 0 else (w.shape[0], w.shape[1] * N_CHIP) for w, ax in zip(shards, axes)]
    r_ct = c_taps.shape[0]
    mq = w_ada.shape[1]

    def body(*refs):
        ins = refs[:n_w]
        ct_ref, wada_ref, b_ref = refs[n_w:n_w + 3]
        outs = refs[n_w + 3:2 * n_w + 3]
        ct_all_ref, mod_all_ref = refs[2 * n_w + 3:2 * n_w + 5]
        scratch = refs[2 * n_w + 5:]
        stage32, stage16, mod_part = scratch[:n_w], scratch[n_w:2 * n_w], scratch[2 * n_w]
        load_sems, store_sems = scratch[2 * n_w + 1:2 * n_w + 3]
        sems, ct_sems, mod_sems = scratch[2 * n_w + 3:2 * n_w + 5], scratch[2 * n_w + 5:2 * n_w + 8], scratch[2 * n_w + 8:]
        x, y, c = _place()
        chip = 2 * x + y
        loads = [pltpu.make_async_copy(ins[a], stage32[a], load_sems.at[a]) for a in range(n_w)]
        for cp in loads:
            cp.start()
        _small_gather_start(ct_ref, ct_all_ref, *ct_sems)
        stores = []
        for a in range(n_w):
            loads[a].wait()
            stage16[a][...] = stage32[a][...].astype(BF16)
            r, cw = shards[a].shape
            own = outs[a].at[pl.ds(chip * r, r), :] if axes[a] == 0 else outs[a].at[:, pl.ds(pl.multiple_of(chip * cw, 128), cw)]
            stores.append(pltpu.make_async_copy(stage16[a], own, store_sems.at[a]))
            stores[-1].start()
            if a < n_gather:
                half = stage16[a].at[pl.ds(c * (r // 2), r // 2), :]
                there = _shard_slice(outs[a], shapes[a], axes[a], chip, c)
                for k, to in ((0, (1 - x, y, c)), (1, (x, 1 - y, c))):
                    pltpu.make_async_remote_copy(src_ref=half, dst_ref=there, send_sem=sems[0].at[a, k],
                                                 recv_sem=sems[1].at[a, k], device_id=to, device_id_type=MESH).start()
        _small_gather_finish(ct_ref, ct_all_ref, *ct_sems)
        cv = jnp.concatenate([ct_all_ref[r_ct * dv:r_ct * dv + 1, :] for dv in range(N_DEV)], axis=0)
        mod_part[...] = _dot((cv * _sig(cv)).astype(BF16), wada_ref[...].astype(BF16)) + b_ref[...]
        _small_gather_start(mod_part, mod_all_ref, *mod_sems)
        _gather_forward(outs[:n_gather], shapes, axes, *sems)
        _small_gather_finish(mod_part, mod_all_ref, *mod_sems)
        _gather_forward_diagonal(outs[:n_gather], shapes, axes, *sems)
        _gather_finish(outs[:n_gather], shapes, axes, *sems)
        for cp in stores:
            cp.wait()

    res = _pallas(
        body, name=name,
        out_shape=tuple(jax.ShapeDtypeStruct(sh, BF16) for sh in shapes)
        + (jax.ShapeDtypeStruct((N_DEV * r_ct, c_taps.shape[1]), F32), jax.ShapeDtypeStruct((N_DEV * N_DEV, mq), F32)),
        in_specs=[ANY] * n_w + [VMEM_FULL] * 3, out_specs=tuple([ANY] * n_w + [VMEM_FULL] * 2),
        scratch_shapes=[pltpu.VMEM(w.shape, F32) for w in shards] + [pltpu.VMEM(w.shape, BF16) for w in shards]
        + [pltpu.VMEM((N_DEV, mq), F32), pltpu.SemaphoreType.DMA((n_w,)), pltpu.SemaphoreType.DMA((n_w,))]
        + _gather_sems(n_gather) + _small_gather_sems() + _small_gather_sems(),
        compiler_params=_params(),
    )(*shards, c_taps, w_ada, b_part)
    return res[:n_w], res[n_w], res[n_w + 1]


def _piece_shapes(grads, axes):
    return [((g.shape[0] // N_CHIP // 2, g.shape[1]) if ax == 0 else (g.shape[0] // 2, g.shape[1] // N_CHIP))
            for g, ax in zip(grads, axes)]


def _exchange_send(ins, outs, shapes, axes, send_sems, recv_sems):
    x, y, c = _place()
    me = 4 * x + 2 * y + c
    for a in range(len(ins)):
        for t in range(N_DEV):
            tx, ty, tc = t // 4, (t // 2) % 2, t % 2
            src = _shard_slice(ins[a], shapes[a], axes[a], 2 * tx + ty, tc)

            @pl.when(me != t)
            def _(src=src, a=a, t=t, to=(tx, ty, tc)):
                slot = (me - t + N_DEV) % N_DEV - 1
                pltpu.make_async_remote_copy(src_ref=src, dst_ref=outs[a].at[slot], send_sem=send_sems.at[a, t],
                                             recv_sem=recv_sems.at[a, slot], device_id=to, device_id_type=MESH).start()


def _exchange_finish(outs, send_sems, recv_sems):
    x, y, c = _place()
    me = 4 * x + 2 * y + c
    for a in range(len(outs)):
        for t in range(N_DEV):
            @pl.when(me != t)
            def _(a=a, t=t):
                slot = outs[a].at[0]
                pltpu.make_async_remote_copy(src_ref=slot, dst_ref=slot, send_sem=send_sems.at[a, t], recv_sem=recv_sems.at[a, 0],
                                             device_id=(x, y, c), device_id_type=MESH).wait_send()
        for k in range(N_DEV - 1):
            slot = outs[a].at[k]
            pltpu.make_async_remote_copy(src_ref=slot, dst_ref=slot, send_sem=send_sems.at[a, 0], recv_sem=recv_sems.at[a, k],
                                         device_id=(x, y, c), device_id_type=MESH).wait_recv()


def _exchange_sems(n_w):
    return [pltpu.SemaphoreType.DMA((n_w, N_DEV)), pltpu.SemaphoreType.DMA((n_w, N_DEV - 1))]


def _exchange_out_shapes(grads, axes):
    return tuple(jax.ShapeDtypeStruct((N_DEV - 1,) + p, BF16) for p in _piece_shapes(grads, axes))


def _swap_halves(fulls, block):
    n_w = len(fulls)

    def body(*refs):
        block_ref = refs[n_w]
        outs, gathered_ref = refs[n_w + 1:2 * n_w + 1], refs[2 * n_w + 1]
        send_sems, recv_sems = refs[2 * n_w + 2:2 * n_w + 4]
        x, y, c = _place()
        sibling = (x, y, 1 - c)
        waits = []
        for a in range(n_w):
            r = fulls[a].shape[0] // 2
            mine = outs[a].at[pl.ds(c * r, r), :]
            theirs = outs[a].at[pl.ds((1 - c) * r, r), :]
            rc = pltpu.make_async_remote_copy(src_ref=mine, dst_ref=mine, send_sem=send_sems.at[a], recv_sem=recv_sems.at[a],
                                              device_id=sibling, device_id_type=MESH)
            rc.start()
            got = pltpu.make_async_remote_copy(src_ref=theirs, dst_ref=theirs, send_sem=send_sems.at[a],
                                               recv_sem=recv_sems.at[a], device_id=sibling, device_id_type=MESH)
            waits += [rc.wait_send, got.wait_recv]
        _small_gather(block_ref, gathered_ref, *refs[2 * n_w + 4:])
        for w in waits:
            w()

    res = _pallas(
        body, name="swap_halves",
        out_shape=tuple(jax.ShapeDtypeStruct(f.shape, f.dtype) for f in fulls)
        + (jax.ShapeDtypeStruct((N_DEV * block.shape[0], block.shape[1]), block.dtype),),
        in_specs=[ANY] * n_w + [VMEM_FULL], out_specs=tuple([ANY] * n_w + [VMEM_FULL]),
        input_output_aliases={a: a for a in range(n_w)},
        scratch_shapes=[pltpu.SemaphoreType.DMA((n_w,)), pltpu.SemaphoreType.DMA((n_w,))] + _small_gather_sems(),
    )(*fulls, block)
    return res[:n_w], res[n_w]


def _load_weights(pairs, sems):
    @pl.when(pl.program_id(0) == 0)
    def _():
        cps = [pltpu.make_async_copy(s, d, sems.at[k]) for k, (s, d) in enumerate(pairs)]
        for cp in cps:
            cp.start()
        for cp in cps:
            cp.wait()


def _conv_chunks(ext, e_buf, ts, d, chunk_fn, unroll, between=()):
    rows, dh = e_buf.shape[1], e_buf.shape[2]
    for hf in range(d // dh):
        for r in range(1, 8):
            e_buf[r - 1, :, :] = ext[r:r + rows, hf * dh:(hf + 1) * dh]
        for lq in range(dh // CONV_LANES):
            lo, el = hf * dh + lq * CONV_LANES, lq * CONV_LANES

            def body(ci, carry, lo=lo, el=el):
                r0 = pl.multiple_of(ci * CONV_ROWS, CONV_ROWS)

                def tap(off):
                    q, r = divmod(off, 8)
                    if r == 0:
                        return ext[pl.ds(r0 + 8 * q, CONV_ROWS), lo:lo + CONV_LANES]
                    return e_buf[r - 1, pl.ds(r0 + 8 * q, CONV_ROWS), el:el + CONV_LANES]

                chunk_fn(tap, r0, lo)
                return carry

            lax.fori_loop(0, ts // CONV_ROWS, body, 0, unroll=unroll)
            if between:
                between[0]()
                between = between[1:]
    for fn in between:
        fn()


def _conv_chunks_strided(ext, ts, chunk_fn):
    for b in range(ext.shape[0]):
        def body(ci, carry, b=b):
            r0 = pl.multiple_of(ci * CONV_ROWS, CONV_ROWS)
            tap = lambda m, off: ext[b, pl.ds(r0 + m + off, 8, stride=8), :]
            rows = lambda m: pl.ds(r0 + m, 8, stride=8)
            chunk_fn(tap, rows, b)
            return carry

        lax.fori_loop(0, ts // CONV_ROWS, body, 0)


def _lane_blocks(value):
    return [value[:, b * LANES:(b + 1) * LANES] for b in range(value.shape[1] // LANES)]


def _fwd_mix(x, mod, gn1, w3, w31, cbias, lng, lnb, w_in, w_a, w_b, w_o, later, later_axes, ts):
    s_len, d = x.shape
    n_t = s_len // ts
    n_l = len(later)
    later_shapes = [w.shape for w in later]

    def body(*refs):
        (x_ref, mod_ref, gn1_ref, w3_ref, w31_ref, cb_ref, lng_ref, lnb_ref, win_hbm, wa_hbm, wb_hbm, wo_hbm) = refs[:12]
        (proj_ref, h_ref, lhs3_ref, c3_ref, n_ref, rstd_ref, ya_ref, yb_ref, mix_ref, x1_ref) = refs[12 + n_l:22 + n_l]
        later_refs = refs[22 + n_l:22 + 2 * n_l]
        win_v, wa_v, wb_v, wo_v, cv_ext, u0_ext, e_buf, u1_buf, sems, g_send, g_recv = refs[22 + 2 * n_l:]
        step = pl.program_id(0)
        pl.when(step == 0)(lambda: _gather_send(later_refs, later_shapes, later_axes, g_send, g_recv))
        pl.when(step == n_t // 2)(lambda: _gather_forward(later_refs, later_shapes, later_axes, g_send, g_recv))
        pl.when(step == (3 * n_t) // 4)(lambda: _gather_forward_diagonal(later_refs, later_shapes, later_axes, g_send, g_recv))
        pl.when(step == n_t - 1)(lambda: _gather_finish(later_refs, later_shapes, later_axes, g_send, g_recv))
        _load_weights([(win_hbm, win_v), (wa_hbm, wa_v), (wb_hbm, wb_v), (wo_hbm, wo_v)], sems)

        @pl.when(pl.program_id(0) == 0)
        def _():
            cv_ext[0:HALO_SHORT, :] = jnp.zeros((HALO_SHORT, d), F32)
            u0_ext[0:HALO_CONF, :] = jnp.zeros((HALO_CONF, d), F32)

        xv = x_ref[...]
        sh1, sc1, g1 = mod_ref[:, 0:d], mod_ref[:, d:2 * d], mod_ref[:, 2 * d:3 * d]
        r = lax.rsqrt(jnp.mean(xv * xv, axis=-1, keepdims=True) + EPS)
        hb = (((xv * r) * gn1_ref[...]) * (1.0 + sc1) + sh1).astype(BF16)
        h_ref[...] = hb

        def pj(k):
            p = _dot(hb, win_v[:, k * d:(k + 1) * d])
            proj_ref[:, k * d:(k + 1) * d] = p.astype(BF16)
            return p

        v_c = pj(3)
        u0_ext[HALO_CONF:HALO_CONF + ts, :] = v_c * _sig(pj(4))

        def conv_chunk(tap, r0, lo):
            acc = jnp.zeros((CONV_ROWS, CONV_LANES), F32)
            for k in range(CONF_K):
                acc = acc + w31_ref[k:k + 1, lo:lo + CONV_LANES] * tap(HALO_CONF - (CONF_K - 1) + k)
            u1_buf[pl.ds(r0, CONV_ROWS), lo:lo + CONV_LANES] = acc

        pjs = {}
        _conv_chunks(u0_ext, e_buf, ts, d, conv_chunk, unroll=True,
                     between=[lambda k=k: pjs.__setitem__(k, pj(k)) for k in (1, 2, 0, 5, 6)])
        u0_ext[0:HALO_CONF, :] = u0_ext[ts:ts + HALO_CONF, :]

        cv_ext[HALO_SHORT:HALO_SHORT + ts, :] = pjs[1] * pjs[2]
        conv3 = jnp.zeros((ts, d), F32)
        for k in range(SHORT_K):
            off = HALO_SHORT - (SHORT_K - 1) + k
            conv3 = conv3 + w3_ref[k:k + 1, :] * cv_ext[off:off + ts, :]
        cv_ext[0:HALO_SHORT, :] = cv_ext[ts:ts + HALO_SHORT, :]
        c3_ref[...] = conv3.astype(BF16)
        yap = (pjs[0] * conv3).astype(BF16)
        lhs3_ref[:, 0:d] = yap
        y_a = _dot(yap, wa_v[...])
        ya_ref[...] = y_a.astype(BF16)

        u1 = u1_buf[...] + cb_ref[...]
        mu = jnp.mean(u1, axis=-1, keepdims=True)
        var = jnp.mean(jnp.square(u1 - mu), axis=-1, keepdims=True)
        rstd = lax.rsqrt(var + LN_EPS)
        nrm = (u1 - mu) * rstd
        n_ref[...] = nrm.astype(BF16)
        rstd_ref[...] = jnp.broadcast_to(rstd, (ts, LANES))
        u2 = nrm * lng_ref[...] + lnb_ref[...]
        u3 = (u2 * _sig(u2)).astype(BF16)
        lhs3_ref[:, d:2 * d] = u3
        y_b = _dot(u3, wb_v[...])
        yb_ref[...] = y_b.astype(BF16)

        merged = (_sig(pjs[5]) * y_a + _sig(pjs[6]) * y_b).astype(BF16)
        lhs3_ref[:, 2 * d:3 * d] = merged
        mix = _dot(merged, wo_v[...])
        mix_ref[...] = mix
        x1_ref[...] = xv + g1 * mix

    row = lambda w, dt=None: pl.BlockSpec((ts, w), lambda i: (i, 0))
    full = lambda a: pl.BlockSpec(a.shape, lambda i: (0, 0))
    sd = lambda w, dt: jax.ShapeDtypeStruct((s_len, w), dt)
    outs = [sd(7 * d, BF16), sd(d, BF16), sd(3 * d, BF16), sd(d, BF16), sd(d, BF16), sd(LANES, F32), sd(d, BF16), sd(d, BF16),
            sd(d, F32), sd(d, F32)]
    res = _pallas(
        body, name="fwd_mix", grid=(n_t,),
        out_shape=tuple(outs) + tuple(jax.ShapeDtypeStruct(w.shape, BF16) for w in later),
        in_specs=[row(d), full(mod), full(gn1), full(w3), full(w31), full(cbias), full(lng), full(lnb), ANY, ANY, ANY, ANY]
        + [ANY] * n_l,
        out_specs=tuple(row(o.shape[1]) for o in outs) + tuple([ANY] * n_l),
        input_output_aliases={12 + k: len(outs) + k for k in range(n_l)},
        scratch_shapes=[pltpu.VMEM(w_in.shape, BF16), pltpu.VMEM(w_a.shape, BF16), pltpu.VMEM(w_b.shape, BF16),
                        pltpu.VMEM(w_o.shape, BF16), pltpu.VMEM((ts + HALO_SHORT, d), F32),
                        pltpu.VMEM((ts + HALO_CONF, d), F32), pltpu.VMEM((7, ts + HALO_CONF - 8, d // 2), F32),
                        pltpu.VMEM((ts, d), F32), pltpu.SemaphoreType.DMA((4,))] + _gather_sems(n_l),
        compiler_params=_params("arbitrary"),
    )(x, mod, gn1, w3, w31, cbias, lng, lnb, w_in, w_a, w_b, w_o, *later)
    return res[:len(outs)], res[len(outs):]


def _ffn(x1, target, mod, gn2, gf, w_fi, w_fo, ts):
    s_len, d = x1.shape
    f = w_fo.shape[0]
    n_t = s_len // ts

    def body(x1_ref, t_ref, mod_ref, gn2_ref, gf_ref, wfi_hbm, wfo_hbm,
             h2_ref, hid_ref, df_ref, dab_ref, dx1_ref, acc_ref, wfi_v, wfo_v, sems):
        _load_weights([(wfi_hbm, wfi_v), (wfo_hbm, wfo_v)], sems)

        @pl.when(pl.program_id(0) == 0)
        def _():
            acc_ref[...] = jnp.zeros(acc_ref.shape, F32)

        x1v = x1_ref[...]
        sh2, sc2, g2 = mod_ref[:, 3 * d:4 * d], mod_ref[:, 4 * d:5 * d], mod_ref[:, 5 * d:6 * d]
        gn2v, gfv = gn2_ref[...], gf_ref[...]
        r2 = lax.rsqrt(jnp.mean(x1v * x1v, axis=-1, keepdims=True) + EPS)
        xn2 = x1v * r2
        h2 = ((xn2 * gn2v) * (1.0 + sc2) + sh2).astype(BF16)
        h2_ref[...] = h2
        a = _dot(h2, wfi_v[:, 0:f])
        bg = _dot(h2, wfi_v[:, f:2 * f])
        sa = _sig(a)
        silu_a = a * sa
        hid = (silu_a * bg).astype(BF16)
        hid_ref[...] = hid
        ffn = _dot(hid, wfo_v[...])
        x2 = x1v + g2 * ffn
        r3 = lax.rsqrt(jnp.mean(x2 * x2, axis=-1, keepdims=True) + EPS)
        xn3 = x2 * r3
        diff = xn3 * gfv - t_ref[...]
        dy = diff * (1.0 / d)
        dxn3 = dy * gfv
        dx2 = r3 * (dxn3 - xn3 * jnp.mean(dxn3 * xn3, axis=-1, keepdims=True))
        acc_ref[ROW_GF:ROW_GF + 1, :] += _colsum(dy * xn3)
        acc_ref[ROW_DMOD + 5:ROW_DMOD + 5 + 1, :] += _colsum(dx2 * ffn)
        acc_ref[ROW_SQERR:ROW_SQERR + 1, :] += _colsum(diff * diff)

        df = (dx2 * g2).astype(BF16)
        df_ref[...] = df
        dhid = _dot_nt(df, wfo_v[...])
        da = dhid * bg * (sa * (1.0 + a * (1.0 - sa)))
        dbg = dhid * silu_a
        dab_ref[:, 0:f] = da.astype(BF16)
        dab_ref[:, f:2 * f] = dbg.astype(BF16)
        dh2 = _dot_nt(da.astype(BF16), wfi_v[:, 0:f]) + _dot_nt(dbg.astype(BF16), wfi_v[:, f:2 * f])
        acc_ref[ROW_DMOD + 3:ROW_DMOD + 3 + 1, :] += _colsum(dh2)
        acc_ref[ROW_DMOD + 4:ROW_DMOD + 4 + 1, :] += _colsum(dh2 * (xn2 * gn2v))
        acc_ref[ROW_GN2:ROW_GN2 + 1, :] += _colsum(dh2 * xn2 * (1.0 + sc2))
        dxn2 = dh2 * gn2v * (1.0 + sc2)
        dx1_ref[...] = dx2 + r2 * (dxn2 - xn2 * jnp.mean(dxn2 * xn2, axis=-1, keepdims=True))

    row = lambda w: pl.BlockSpec((ts, w), lambda i: (i, 0))
    full = lambda a: pl.BlockSpec(a.shape, lambda i: (0, 0))
    sd = lambda w, dt: jax.ShapeDtypeStruct((s_len, w), dt)
    outs = [sd(d, BF16), sd(f, BF16), sd(d, BF16), sd(2 * f, BF16), sd(d, F32), jax.ShapeDtypeStruct((N_SMALL_ROWS, d), F32)]
    return _pallas(
        body, name="ffn", grid=(n_t,), out_shape=tuple(outs),
        in_specs=[row(d), row(d), full(mod), full(gn2), full(gf), ANY, ANY],
        out_specs=tuple([row(o.shape[1]) for o in outs[:-1]] + [pl.BlockSpec((N_SMALL_ROWS, d), lambda i: (0, 0))]),
        scratch_shapes=[pltpu.VMEM(w_fi.shape, BF16), pltpu.VMEM(w_fo.shape, BF16), pltpu.SemaphoreType.DMA((2,))],
        compiler_params=_params("arbitrary"),
    )(x1, target, mod, gn2, gf, w_fi, w_fo)


def _bwd_mix(dx1, proj, nrm, rstd, c3, ya, yb, mix, mod, w3, w31, lng, lnb, w_a, w_b, w_o, ready, ready_axes, ts):
    s_len, d = dx1.shape
    n_t = s_len // ts
    n_r = len(ready)
    ready_shapes = [g.shape for g in ready]

    def body(*refs):
        (dx1_ref, proj_ref, n_ref, rstd_ref, c3_ref, ya_ref, yb_ref, mix_ref, mod_ref, w3_ref, w31_ref, lng_ref,
         lnb_ref, wa_hbm, wb_hbm, wo_hbm) = refs[:16]
        ready_refs = refs[16:16 + n_r]
        dproj_ref, rhs3_ref, acc_ref = refs[16 + n_r:19 + n_r]
        recv_refs = refs[19 + n_r:19 + 2 * n_r]
        wa_v, wb_v, wo_v, dc3_ext, du1_ext, u0_buf, du0_buf, dw8, sems, e_send, e_recv = refs[19 + 2 * n_r:]
        step = pl.program_id(0)
        pl.when(step == 0)(lambda: _exchange_send(ready_refs, recv_refs, ready_shapes, ready_axes, e_send, e_recv))
        pl.when(step == n_t - 1)(lambda: _exchange_finish(recv_refs, e_send, e_recv))
        _load_weights([(wa_hbm, wa_v), (wb_hbm, wb_v), (wo_hbm, wo_v)], sems)

        @pl.when(pl.program_id(0) == 0)
        def _():
            acc_ref[...] = jnp.zeros(acc_ref.shape, F32)
            dc3_ext[ts:ts + HALO_SHORT, :] = jnp.zeros((HALO_SHORT, d), F32)
            du1_ext[:, ts:ts + HALO_CONF, :] = jnp.zeros((d // LANES, HALO_CONF, LANES), F32)
            dw8[...] = jnp.zeros(dw8.shape, F32)

        pj = lambda k: proj_ref[:, k * d:(k + 1) * d].astype(F32)
        g1 = mod_ref[:, 2 * d:3 * d]
        dx1v = dx1_ref[...]
        acc_ref[ROW_DMOD + 2:ROW_DMOD + 2 + 1, :] += _colsum(dx1v * mix_ref[...])
        dmix = (dx1v * g1).astype(BF16)
        rhs3_ref[:, 2 * d:3 * d] = dmix
        dmerged = _dot_nt(dmix, wo_v[...])
        g_a, g_b = _sig(pj(5)), _sig(pj(6))
        y_a, y_b = ya_ref[...].astype(F32), yb_ref[...].astype(F32)
        dproj_ref[:, 5 * d:6 * d] = (dmerged * y_a * g_a * (1.0 - g_a)).astype(BF16)
        dproj_ref[:, 6 * d:7 * d] = (dmerged * y_b * g_b * (1.0 - g_b)).astype(BF16)
        dya = (dmerged * g_a).astype(BF16)
        dyb = (dmerged * g_b).astype(BF16)
        rhs3_ref[:, 0:d] = dya
        rhs3_ref[:, d:2 * d] = dyb

        dyap = _dot_nt(dya, wa_v[...])
        dproj_ref[:, 0:d] = (dyap * c3_ref[...].astype(F32)).astype(BF16)
        dc3_ext[0:ts, :] = dyap * pj(0)
        c_s, v_s = pj(1), pj(2)
        cv = c_s * v_s
        dcv = jnp.zeros((ts, d), F32)
        for j in range(SHORT_K):
            k = SHORT_K - 1 - j
            sj = dc3_ext[j:j + ts, :]
            dcv = dcv + w3_ref[k:k + 1, :] * sj
            acc_ref[ROW_W3 + k:ROW_W3 + k + 1, :] += _colsum(cv * sj)
        dc3_ext[ts:ts + HALO_SHORT, :] = dc3_ext[0:HALO_SHORT, :]
        dproj_ref[:, d:2 * d] = (dcv * v_s).astype(BF16)
        dproj_ref[:, 2 * d:3 * d] = (dcv * c_s).astype(BF16)

        du3 = _dot_nt(dyb, wb_v[...])
        nv = n_ref[...].astype(F32)
        lg = lng_ref[...]
        u2 = nv * lg + lnb_ref[...]
        s2 = _sig(u2)
        du2 = du3 * (s2 * (1.0 + u2 * (1.0 - s2)))
        acc_ref[ROW_LNG:ROW_LNG + 1, :] += _colsum(du2 * nv)
        acc_ref[ROW_LNB:ROW_LNB + 1, :] += _colsum(du2)
        dn = du2 * lg
        du1 = rstd_ref[:, 0:1] * (dn - jnp.mean(dn, axis=-1, keepdims=True) - nv * jnp.mean(dn * nv, axis=-1, keepdims=True))
        acc_ref[ROW_CB:ROW_CB + 1, :] += _colsum(du1)
        for b, blk in enumerate(_lane_blocks(du1)):
            du1_ext[b, 0:ts, :] = blk
        v_c = pj(3)
        sg = _sig(pj(4))
        for b, blk in enumerate(_lane_blocks(v_c * sg)):
            u0_buf[b] = blk

        def conv_chunk(tap, rows, b):
            lanes = slice(b * LANES, (b + 1) * LANES)
            u0c = [u0_buf[b, rows(m), :] for m in range(8)]
            acc = [jnp.zeros((8, LANES), F32) for _ in range(8)]
            for j in range(CONF_K):
                k = CONF_K - 1 - j
                wk = w31_ref[k:k + 1, lanes]
                part = jnp.zeros((8, LANES), F32)
                for m in range(8):
                    sj = tap(m, j)
                    acc[m] = acc[m] + wk * sj
                    part = part + u0c[m] * sj
                dw8[k, :, lanes] += part
            for m in range(8):
                du0_buf[b, rows(m), :] = acc[m]

        _conv_chunks_strided(du1_ext, ts, conv_chunk)
        du0 = jnp.concatenate([du0_buf[b] for b in range(d // LANES)], axis=1)
        du1_ext[:, ts:ts + HALO_CONF, :] = du1_ext[:, 0:HALO_CONF, :]

        @pl.when(step == n_t - 1)
        def _():
            for k in range(CONF_K):
                acc_ref[ROW_W31 + k:ROW_W31 + k + 1, :] = _colsum(dw8[k])
        dproj_ref[:, 3 * d:4 * d] = (du0 * sg).astype(BF16)
        dproj_ref[:, 4 * d:5 * d] = (du0 * v_c * sg * (1.0 - sg)).astype(BF16)

    rev = lambda w: pl.BlockSpec((ts, w), lambda i: (n_t - 1 - i, 0))
    full = lambda a: pl.BlockSpec(a.shape, lambda i: (0, 0))
    sd = lambda w, dt: jax.ShapeDtypeStruct((s_len, w), dt)
    outs = [sd(7 * d, BF16), sd(3 * d, BF16), jax.ShapeDtypeStruct((N_SMALL_ROWS, d), F32)]
    res = _pallas(
        body, name="bwd_mix", grid=(n_t,), out_shape=tuple(outs) + _exchange_out_shapes(ready, ready_axes),
        in_specs=[rev(d), rev(7 * d), rev(d), rev(LANES), rev(d), rev(d), rev(d), rev(d), full(mod), full(w3), full(w31),
                  full(lng), full(lnb), ANY, ANY, ANY] + [ANY] * n_r,
        out_specs=tuple([rev(o.shape[1]) for o in outs[:-1]] + [pl.BlockSpec((N_SMALL_ROWS, d), lambda i: (0, 0))] + [ANY] * n_r),
        scratch_shapes=[pltpu.VMEM(w_a.shape, BF16), pltpu.VMEM(w_b.shape, BF16), pltpu.VMEM(w_o.shape, BF16),
                        pltpu.VMEM((ts + HALO_SHORT, d), F32), pltpu.VMEM((d // LANES, ts + HALO_CONF, LANES), F32),
                        pltpu.VMEM((d // LANES, ts, LANES), F32), pltpu.VMEM((d // LANES, ts, LANES), F32),
                        pltpu.VMEM((CONF_K, 8, d), F32), pltpu.SemaphoreType.DMA((3,))] + _exchange_sems(n_r),
        compiler_params=_params("arbitrary"),
    )(dx1, proj, nrm, rstd, c3, ya, yb, mix, mod, w3, w31, lng, lnb, w_a, w_b, w_o, *ready)
    return res[:len(outs)], res[len(outs):]


def _bwd_in(dproj, x, dx1, mod, gn1, w_in, ts):
    s_len, d = x.shape
    n_t = s_len // ts

    def body(dproj_ref, x_ref, dx1_ref, mod_ref, gn1_ref, win_hbm, gx_ref, acc_ref, win_v, sems):
        _load_weights([(win_hbm, win_v)], sems)

        @pl.when(pl.program_id(0) == 0)
        def _():
            acc_ref[...] = jnp.zeros(acc_ref.shape, F32)

        xv = x_ref[...]
        sc1 = mod_ref[:, d:2 * d]
        gn1v = gn1_ref[...]
        r = lax.rsqrt(jnp.mean(xv * xv, axis=-1, keepdims=True) + EPS)
        xn = xv * r
        dh = _dot_nt(dproj_ref[...], win_v[...])
        acc_ref[ROW_DMOD:ROW_DMOD + 1, :] += _colsum(dh)
        acc_ref[ROW_DMOD + 1:ROW_DMOD + 1 + 1, :] += _colsum(dh * (xn * gn1v))
        acc_ref[ROW_GN1:ROW_GN1 + 1, :] += _colsum(dh * xn * (1.0 + sc1))
        dxn = dh * gn1v * (1.0 + sc1)
        gx_ref[...] = dx1_ref[...] + r * (dxn - xn * jnp.mean(dxn * xn, axis=-1, keepdims=True))

    row = lambda w: pl.BlockSpec((ts, w), lambda i: (i, 0))
    full = lambda a: pl.BlockSpec(a.shape, lambda i: (0, 0))
    return _pallas(
        body, name="bwd_in", grid=(n_t,),
        out_shape=(jax.ShapeDtypeStruct((s_len, d), F32), jax.ShapeDtypeStruct((N_SMALL_ROWS, d), F32)),
        in_specs=[row(7 * d), row(d), row(d), full(mod), full(gn1), ANY],
        out_specs=(row(d), pl.BlockSpec((N_SMALL_ROWS, d), lambda i: (0, 0))),
        scratch_shapes=[pltpu.VMEM(w_in.shape, BF16), pltpu.SemaphoreType.DMA((1,))],
        compiler_params=_params("arbitrary"),
    )(dproj, x, dx1, mod, gn1, w_in)


class _OneWeight:
    def __init__(self, sems):
        self.sems = sems

    @property
    def at(self):
        return self

    def __getitem__(self, idx):
        return self.sems.at[idx[1]]


_HBM = pl.BlockSpec(memory_space=pltpu.HBM)
_SEM = pl.BlockSpec(memory_space=pltpu.SEMAPHORE)
_DATAFLOW = pltpu.SideEffectType.DATAFLOW_SIDE_EFFECTING


def _exchange_begin(grad, axis, name):
    slots = (N_DEV - 1,) + _piece_shapes([grad], [axis])[0]

    def body(g_ref, slots_ref, send_sems, recv_sems, g_thru, slots_thru, token):
        _exchange_send([g_ref], [slots_ref], [grad.shape], [axis], _OneWeight(send_sems), _OneWeight(recv_sems))
        token[...] = jnp.zeros(token.shape, F32)

    return _pallas(
        body, name=name,
        out_shape=(pltpu.SemaphoreType.DMA((N_DEV,)), pltpu.SemaphoreType.DMA((N_DEV - 1,)), pltpu.HBM(grad.shape, BF16),
                   pltpu.HBM(slots, BF16), jax.ShapeDtypeStruct((8, LANES), F32)),
        in_specs=(_HBM, _HBM), out_specs=(_SEM, _SEM, _HBM, _HBM, VMEM_FULL), input_output_aliases={0: 2, 1: 3},
        compiler_params=pltpu.CompilerParams(has_side_effects=_DATAFLOW),
    )(pltpu.with_memory_space_constraint(grad, pltpu.HBM), pltpu.with_memory_space_constraint(lax.empty(slots, BF16), pltpu.HBM))


def _exchange_end(send_sems, recv_sems, grad, slots, after, name):
    def body(g_ref, slots_ref, send_sems, recv_sems, after_ref, g_out, slots_out):
        _exchange_finish([slots_ref], _OneWeight(send_sems), _OneWeight(recv_sems))

    return _pallas(
        body, name=name, out_shape=(pltpu.HBM(grad.shape, BF16), pltpu.HBM(slots.shape, BF16)),
        in_specs=(_HBM, _HBM, _SEM, _SEM, ANY), out_specs=(_HBM, _HBM), input_output_aliases={0: 0, 1: 1},
        compiler_params=pltpu.CompilerParams(has_side_effects=_DATAFLOW),
    )(grad, slots, send_sems, recv_sems, after)


def _weight_grad(a, b, bn, name, transposed=False, ready=(), ready_axes=(), a_cols=None):
    s_len, m = a.shape[0], (a_cols or a.shape[1])
    n = b.shape[1]
    n_j = n // bn
    n_r = len(ready)
    ready_shapes = [g.shape for g in ready]

    def body(*refs):
        a_hbm, b_ref = refs[:2]
        ready_refs = refs[2:2 + n_r]
        o_ref = refs[2 + n_r]
        recv_refs = refs[3 + n_r:3 + 2 * n_r]
        a_v, sem = refs[3 + 2 * n_r:5 + 2 * n_r]
        step = pl.program_id(0)
        if n_r:
            e_send, e_recv = refs[5 + 2 * n_r:]
            pl.when(step == 0)(lambda: _exchange_send(ready_refs, recv_refs, ready_shapes, ready_axes, e_send, e_recv))
            pl.when(step == n_j - 1)(lambda: _exchange_finish(recv_refs, e_send, e_recv))
        if a_cols is None:
            _load_weights([(a_hbm, a_v)], sem)
            prod = _dot_tn(a_v[...], b_ref[...])
        else:
            prod = _dot_tn(a_hbm[...], b_ref[...])
        o_ref[...] = (prod.T if transposed else prod).astype(BF16)

    out = jax.ShapeDtypeStruct((n, m) if transposed else (m, n), BF16)
    out_spec = pl.BlockSpec((bn, m), lambda j: (j, 0)) if transposed else pl.BlockSpec((m, bn), lambda j: (0, j))
    res = _pallas(
        body, name=name, grid=(n_j,), out_shape=(out,) + _exchange_out_shapes(ready, ready_axes),
        in_specs=[ANY if a_cols is None else pl.BlockSpec((s_len, a_cols), lambda j: (0, (j * bn) // a_cols)),
                  pl.BlockSpec((s_len, bn), lambda j: (0, j))] + [ANY] * n_r,
        out_specs=(out_spec,) + tuple([ANY] * n_r),
        scratch_shapes=[pltpu.VMEM((8, 128) if a_cols else a.shape, BF16), pltpu.SemaphoreType.DMA((1,))]
        + (_exchange_sems(n_r) if n_r else []),
        compiler_params=_params("arbitrary"),
    )(a, b, *ready)
    return res[0], res[1:]


def _sum_slots(recv, grad, axis, chip, core, name):
    n_s, r, c = recv.shape
    tr = 16
    for cand in (256, 128, 176, 64, 32, 16):
        if r % cand == 0:
            tr = cand
            break
    n_i = r // tr

    def body(s_ref, own_ref, r_ref, o_ref):
        tot = own_ref[...].astype(F32)
        for k in range(n_s):
            tot = tot + r_ref[k].astype(F32)
        o_ref[...] = tot

    if axis == 0:
        own_map = lambda i, s: ((2 * s[0] + s[1]) * n_i + i, 0)
    else:
        own_map = lambda i, s: (s[1] * n_i + i, s[0])
    grid_spec = pltpu.PrefetchScalarGridSpec(
        num_scalar_prefetch=1, grid=(n_i,),
        in_specs=[pl.BlockSpec((tr, c), own_map), pl.BlockSpec((n_s, tr, c), lambda i, s: (0, i, 0))],
        out_specs=pl.BlockSpec((tr, c), lambda i, s: (s[1] * n_i + i, 0)))
    return _pallas(body, name=name, grid_spec=grid_spec, out_shape=jax.ShapeDtypeStruct((2 * r, c), F32),
                   compiler_params=_params("arbitrary"))(jnp.stack([chip, core]).astype(jnp.int32), grad, recv)


def _adamw_math(w, g, m, v):
    m = ADAM_B1 * m + (1.0 - ADAM_B1) * g
    v = ADAM_B2 * v + (1.0 - ADAM_B2) * (g * g)
    m_hat = m / (1.0 - ADAM_B1 ** ADAM_STEP)
    v_hat = v / (1.0 - ADAM_B2 ** ADAM_STEP)
    delta = -ADAM_LR * (m_hat / (jnp.sqrt(v_hat) + ADAM_EPS) + ADAM_WD * w)
    return delta, m, v


def _adamw(w, g, m, v, name):
    r, c = w.shape
    tr = r
    for cand in (256, 128, 176, 64, 32, 16, 8):
        if r % cand == 0:
            tr = cand
            break

    def body(w_ref, g_ref, m_ref, v_ref, go_ref, d_ref, nm_ref, nv_ref):
        g = g_ref[...]
        go_ref[...] = g
        d_ref[...], nm_ref[...], nv_ref[...] = _adamw_math(w_ref[...], g, m_ref[...], v_ref[...])

    blk = pl.BlockSpec((tr, c), lambda i: (i, 0))
    return _pallas(body, name=name, grid=(r // tr,), out_shape=tuple(jax.ShapeDtypeStruct((r, c), F32) for _ in range(4)),
                   in_specs=[blk] * 4, out_specs=(blk,) * 4, compiler_params=_params("arbitrary"))(w, g, m, v)


_SMALL = [
    ("b_ada", ROW_DMOD, "mod"), ("norm_mix_g", ROW_GN1, "row"), ("conv_short_w", ROW_W3, "shard"),
    ("conv_conf_w", ROW_W31, "shard"), ("conv_conf_b", ROW_CB, "row"), ("conf_ln_g", ROW_LNG, "row"),
    ("conf_ln_b", ROW_LNB, "row"), ("norm_ffn_g", ROW_GN2, "row"), ("final_norm_g", ROW_GF, "row")]


def _small_update(p_all, chip_onehot, ws, ms, vs):
    n = len(_SMALL)
    d = p_all.shape[2]

    def body(*refs):
        p_ref, oh_ref = refs[:2]
        w_refs, m_refs, v_refs = refs[2:2 + n], refs[2 + n:2 + 2 * n], refs[2 + 2 * n:2 + 3 * n]
        loss_ref = refs[2 + 3 * n]
        g_out, d_out, m_out, v_out = [refs[3 + (3 + q) * n:3 + (4 + q) * n] for q in range(4)]
        g_all = p_ref[0]
        for s in range(1, N_DEV):
            g_all = g_all + p_ref[s]
        loss_ref[...] = jnp.zeros(loss_ref.shape, F32) + (0.5 / d) * jnp.sum(g_all[ROW_SQERR:ROW_SQERR + 1, :])

        def emit(i, idx, g):
            dl, nm, nv = _adamw_math(w_refs[i][idx], g, m_refs[i][idx], v_refs[i][idx])
            g_out[i][idx], d_out[i][idx], m_out[i][idx], v_out[i][idx] = g, dl, nm, nv

        everything = (slice(None), slice(None))
        for i, (_, row, kind) in enumerate(_SMALL):
            if kind == "mod":
                for k in range(N_MOD):
                    emit(i, (slice(0, 1), slice(k * d, (k + 1) * d)), g_all[row + k:row + k + 1, :])
            elif kind == "row":
                emit(i, everything, g_all[row:row + 1, :])
            else:
                _, taps, dq = w_refs[i].shape
                g = jnp.zeros((taps, dq), F32)
                for j in range(N_CHIP):
                    g = g + oh_ref[0:1, j:j + 1] * g_all[row:row + taps, j * dq:(j + 1) * dq]
                emit(i, (0, slice(None), slice(None)), g)

    shapes = [jax.ShapeDtypeStruct(w.shape, F32) for w in ws]
    res = _pallas(body, name="small_update", out_shape=tuple([jax.ShapeDtypeStruct((8, 128), F32)] + shapes * 4),
                  in_specs=[VMEM_FULL] * (2 + 3 * n), out_specs=tuple([VMEM_FULL] * (1 + 4 * n)),
                  compiler_params=_params())(p_all, chip_onehot, *ws, *ms, *vs)
    return res[0], [res[1 + q * n:1 + (q + 1) * n] for q in range(4)]


def _adamw_w_ada(w, c_t, dmod_part, m, v):
    r, c = w.shape
    tr = min(256, r)

    def body(w_ref, c_ref, dm_ref, m_ref, v_ref, go_ref, d_ref, nm_ref, nv_ref):
        cv = c_ref[...]
        g = _dot((cv * _sig(cv)).astype(BF16), dm_ref[...].astype(BF16))
        go_ref[...] = g
        d_ref[...], nm_ref[...], nv_ref[...] = _adamw_math(w_ref[...], g, m_ref[...], v_ref[...])

    blk = pl.BlockSpec((tr, c), lambda i: (i, 0))
    return _pallas(body, name="adamw_w_ada", grid=(r // tr,), out_shape=tuple(jax.ShapeDtypeStruct((r, c), F32) for _ in range(4)),
                   in_specs=[blk, pl.BlockSpec((tr, c_t.shape[1]), lambda i: (i, 0)),
                             pl.BlockSpec(dmod_part.shape, lambda i: (0, 0)), blk, blk],
                   out_specs=(blk,) * 4, compiler_params=_params("arbitrary"))(w, c_t, dmod_part, m, v)


def _step(x, c, w_ada, b_ada, norm_mix_g, w_in, conv_short_w, w_short_out, conv_conf_w, conv_conf_b, conf_ln_g,
          conf_ln_b, w_conf_out, w_o, norm_ffn_g, w_ffn_in, w_ffn_out, final_norm_g, loss_target, moments, ts):
    xi, yi, ci = _place()
    chip = 2 * xi + yi
    me = 4 * xi + 2 * yi + ci
    x2d, tgt = x[0], loss_target[0]
    s_len, d = x2d.shape
    dq = d // N_CHIP
    mq = N_MOD * d // N_CHIP
    gf = final_norm_g.reshape(1, d)

    taps = jnp.concatenate([jnp.pad(conv_short_w[0], ((0, 8 - SHORT_K), (0, 0))),
                            jnp.pad(conv_conf_w[0], ((0, 32 - CONF_K), (0, 0)))], axis=0)
    n_tr = 40 * dq // d
    c_taps = jnp.concatenate([jnp.broadcast_to(c, (8, d)), taps.reshape(n_tr, d), jnp.zeros((16 - n_tr, d), F32)], axis=0)
    b_part = lax.dynamic_slice(b_ada, (0, chip * mq), (1, mq))
    big = [w_in[0], w_short_out[0], w_conf_out[0], w_o[0], w_ffn_in[0], w_ffn_out[0]]
    axes = [1, 0, 0, 0, 1, 0]
    big_names = ["w_in", "w_short_out", "w_conf_out", "w_o", "w_ffn_in", "w_ffn_out"]
    (win_f, wa_f, wb_f, wo_f, *placed_ffn), c_taps, mod_all = _gather_weights(big, axes, 4, c_taps, w_ada[0], b_part,
                                                                              "gather_weights")
    c_taps = c_taps.reshape(N_DEV, 24, d)
    c_all = c_taps[:, 0]
    mod_all = mod_all.reshape(N_CHIP, 2, N_DEV, mq)[:, 0]
    mod_full = jnp.transpose(mod_all, (1, 0, 2)).reshape(N_DEV, N_MOD * d)
    mod = lax.dynamic_slice(mod_full, (me, 0), (1, N_MOD * d))

    taps_all = c_taps[:, 8:8 + n_tr].reshape(N_CHIP, 2, 40, dq)[:, 0]
    taps_full = jnp.transpose(taps_all, (1, 0, 2)).reshape(40, d)
    w3, w31 = taps_full[0:SHORT_K], taps_full[8:8 + CONF_K]

    (proj, h, lhs3, c3, nrm, rstd, ya, yb, mix, x1), (wfi_f, wfo_f) = _fwd_mix(
        x2d, mod, norm_mix_g, w3, w31, conv_conf_b, conf_ln_g, conf_ln_b, win_f, wa_f, wb_f, wo_f, placed_ffn, axes[4:], ts)
    h2, hid, df, dab, dx1, acc_f = _ffn(x1, tgt, mod, norm_ffn_g, gf, wfi_f, wfo_f, ts)
    f = wfo_f.shape[0]
    bn = min(512, d)
    g_ffn = [_weight_grad(h2, dab, bn, "grad_w_ffn_in")[0], _weight_grad(df, hid, bn // 2, "grad_w_ffn_out", transposed=True)[0]]
    (dproj, rhs3, acc_m), r_ffn = _bwd_mix(dx1, proj, nrm, rstd, c3, ya, yb, mix, mod, w3, w31, conf_ln_g, conf_ln_b,
                                          wa_f, wb_f, wo_f, g_ffn, axes[4:], ts)
    g3 = _weight_grad(lhs3, rhs3, bn // 2, "grad_w_sq", a_cols=d)[0]
    g_sq = [g3[:, k * d:(k + 1) * d] for k in range(3)]
    g_in, r_sq = _weight_grad(h, dproj, bn, "grad_w_in", ready=g_sq, ready_axes=axes[1:4])
    send_sems, recv_sems, g_in, slots, token = _exchange_begin(g_in, axes[0], "exchange_w_in_begin")
    grad_x, acc_i = _bwd_in(dproj, x2d, dx1, mod + token[0, 0], norm_mix_g, win_f, min(2 * ts, s_len))
    g_in, r_in = _exchange_end(send_sems, recv_sems, g_in, slots, acc_i, "exchange_w_in_end")
    g_in = [g_in]
    r_in_sq = [r_in] + list(r_sq)
    grads = g_in + g_sq + g_ffn
    recv = list(r_in_sq) + list(r_ffn)
    g_big, p_all = _swap_halves([_sum_slots(r, g, ax, chip, ci, "sum_slots_" + n)
                                 for r, g, ax, n in zip(recv, grads, axes, big_names)], acc_f + acc_m + acc_i)
    p_all = p_all.reshape(N_DEV, N_SMALL_ROWS, d)

    small_w = dict(b_ada=b_ada, norm_mix_g=norm_mix_g, conv_short_w=conv_short_w, conv_conf_w=conv_conf_w,
                   conv_conf_b=conv_conf_b, conf_ln_g=conf_ln_g, conf_ln_b=conf_ln_b, norm_ffn_g=norm_ffn_g,
                   final_norm_g=final_norm_g)
    as_2d = lambda t: t if t.ndim == 3 else t.reshape((-1, t.shape[-1]))
    chip_onehot = (lax.broadcasted_iota(jnp.int32, (1, 128), 1) == chip).astype(F32)
    loss_tile, small_out = _small_update(
        p_all, chip_onehot, [as_2d(small_w[n]) for n, _, _ in _SMALL], [as_2d(moments["m_" + n]) for n, _, _ in _SMALL],
        [as_2d(moments["v_" + n]) for n, _, _ in _SMALL])
    small_g, small_d, small_m, small_v = [{n: o.reshape(small_w[n].shape) for (n, _, _), o in zip(_SMALL, outs)}
                                          for outs in small_out]

    dmod_all = p_all[:, 0:N_MOD].reshape(N_DEV, N_MOD * d)
    dmod_part = jnp.pad(lax.dynamic_slice(dmod_all, (0, chip * mq), (N_DEV, mq)), ((0, 128 - N_DEV), (0, 0)))
    c_t = jnp.pad(jnp.transpose(c_all), ((0, 0), (0, 128 - N_DEV)))
    out_g, out_d, out_m, out_v = dict(small_g), dict(small_d), dict(small_m), dict(small_v)
    updates = [("w_ada", _adamw_w_ada(w_ada[0], c_t, dmod_part, moments["m_w_ada"][0], moments["v_w_ada"][0]))]
    updates += [(name, _adamw(w, g, moments["m_" + name][0], moments["v_" + name][0], "adamw_" + name))
                for name, w, g in zip(big_names, big, g_big)]
    for name, (go, dl, nm, nv) in updates:
        out_g[name], out_d[name], out_m[name], out_v[name] = go[None], dl[None], nm[None], nv[None]

    loss = loss_tile[0, 0]
    return loss, grad_x[None], out_g, out_d, out_m, out_v


_WEIGHTS = ["w_ada", "b_ada", "norm_mix_g", "w_in", "conv_short_w", "w_short_out", "conv_conf_w", "conv_conf_b",
            "conf_ln_g", "conf_ln_b", "w_conf_out", "w_o", "norm_ffn_g", "w_ffn_in", "w_ffn_out", "final_norm_g"]
ROW_TILE = 256


def kernel(x, c, w_ada, b_ada, norm_mix_g, w_in, conv_short_w, w_short_out, conv_conf_w, conv_conf_b, conf_ln_g, conf_ln_b, w_conf_out, w_o, norm_ffn_g, w_ffn_in, w_ffn_out, final_norm_g, loss_target, m_w_ada, m_b_ada, m_norm_mix_g, m_w_in, m_conv_short_w, m_w_short_out, m_conv_conf_w, m_conv_conf_b, m_conf_ln_g, m_conf_ln_b, m_w_conf_out, m_w_o, m_norm_ffn_g, m_w_ffn_in, m_w_ffn_out, m_final_norm_g, v_w_ada, v_b_ada, v_norm_mix_g, v_w_in, v_conv_short_w, v_w_short_out, v_conv_conf_w, v_conv_conf_b, v_conf_ln_g, v_conf_ln_b, v_w_conf_out, v_w_o, v_norm_ffn_g, v_w_ffn_in, v_w_ffn_out, v_final_norm_g):
    moments = dict(
        m_w_ada=m_w_ada, m_b_ada=m_b_ada, m_norm_mix_g=m_norm_mix_g, m_w_in=m_w_in, m_conv_short_w=m_conv_short_w,
        m_w_short_out=m_w_short_out, m_conv_conf_w=m_conv_conf_w, m_conv_conf_b=m_conv_conf_b, m_conf_ln_g=m_conf_ln_g,
        m_conf_ln_b=m_conf_ln_b, m_w_conf_out=m_w_conf_out, m_w_o=m_w_o, m_norm_ffn_g=m_norm_ffn_g, m_w_ffn_in=m_w_ffn_in,
        m_w_ffn_out=m_w_ffn_out, m_final_norm_g=m_final_norm_g,
        v_w_ada=v_w_ada, v_b_ada=v_b_ada, v_norm_mix_g=v_norm_mix_g, v_w_in=v_w_in, v_conv_short_w=v_conv_short_w,
        v_w_short_out=v_w_short_out, v_conv_conf_w=v_conv_conf_w, v_conv_conf_b=v_conv_conf_b, v_conf_ln_g=v_conf_ln_g,
        v_conf_ln_b=v_conf_ln_b, v_w_conf_out=v_w_conf_out, v_w_o=v_w_o, v_norm_ffn_g=v_norm_ffn_g, v_w_ffn_in=v_w_ffn_in,
        v_w_ffn_out=v_w_ffn_out, v_final_norm_g=v_final_norm_g)
    loss, grad_x, g, dl, nm, nv = _step(
        x, c, w_ada, b_ada, norm_mix_g, w_in, conv_short_w, w_short_out, conv_conf_w, conv_conf_b, conf_ln_g, conf_ln_b,
        w_conf_out, w_o, norm_ffn_g, w_ffn_in, w_ffn_out, final_norm_g, loss_target, moments, min(ROW_TILE, x.shape[1]))
    return (loss, grad_x, *[g[n] for n in _WEIGHTS], *[dl[n] for n in _WEIGHTS], *[nm[n] for n in _WEIGHTS],
            *[nv[n] for n in _WEIGHTS])
```
